```python
import math
import jax, jax.numpy as jnp
from jax import lax
import numpy as np

D_MODEL = 1024
BATCH = 2
SEQ = 8192
DEPTH = 1

N_HEADS_A = 8
HEAD_DIM_A = 64
D_LATENT = 256
N_HEADS_IDX = 8
HEAD_DIM_IDX = 64
TOPK_MAX = 256
Q_BLOCK = 128
N_BUCKETS = 32
MAX_DISTANCE = 128
N_HEADS_M = 4
HEAD_DIM_M = 128
CHUNK = 64
CONV_WIDTH = 4
F_BIAS_INIT = 3.0
D_FF = 2816
EPS = 1e-6

W_A = N_HEADS_A * HEAD_DIM_A
W_M = N_HEADS_M * HEAD_DIM_M
IDX_SCALE = (N_HEADS_IDX ** -0.5) * (HEAD_DIM_IDX ** -0.5)
SPLITS = (W_A, D_LATENT, N_HEADS_IDX * HEAD_DIM_IDX, HEAD_DIM_IDX, N_HEADS_IDX,
          W_M, W_M, W_M, N_HEADS_M, N_HEADS_M, W_M, D_MODEL, D_MODEL)
D_IN = sum(SPLITS)

kernel_name = "hybrid_dsa_mlstm_macaron_block"


def rms_norm(x, g):
    xf = x.astype(jnp.float32)
    y = xf * lax.rsqrt(jnp.mean(xf * xf, axis=-1, keepdims=True) + EPS)
    return (y * g.astype(jnp.float32)).astype(x.dtype)


def modulate(h, shift, scale):
    return h * (1.0 + scale[:, None, :]) + shift[:, None, :]


def swiglu(h, w1, w3, w2):
    return (jax.nn.silu(h @ w1) * (h @ w3)) @ w2


def t5_bucket(dist):
    n = jnp.maximum(dist, 0)
    max_exact = N_BUCKETS // 2
    nf = jnp.maximum(n, 1).astype(jnp.float32)
    large = max_exact + (jnp.log(nf / max_exact) / math.log(MAX_DISTANCE / max_exact)
                         * (N_BUCKETS - max_exact)).astype(jnp.int32)
    large = jnp.minimum(large, N_BUCKETS - 1)
    return jnp.where(n < max_exact, n, large)


def dsa_attention(q, c_kv, q_idx, k_idx, w_idx, w_uk, w_uv, rel_bias):
    B, S = q.shape[:2]
    nb = S // Q_BLOCK
    topk = min(TOPK_MAX, S // 4)
    kpos = jnp.arange(S)
    scale = HEAD_DIM_A ** -0.5

    def blocks(a):
        return jnp.moveaxis(a.reshape(B, nb, Q_BLOCK, *a.shape[2:]), 1, 0)

    def one_block(args):
        blk, qb, qib, wb = args
        tpos = blk * Q_BLOCK + jnp.arange(Q_BLOCK)
        idx_logits = jax.nn.relu(jnp.einsum('bthd,bsd->bths', qib, k_idx).astype(jnp.float32))
        score = jnp.einsum('bth,bths->bts', wb.astype(jnp.float32), idx_logits)
        causal = kpos[None, :] <= tpos[:, None]
        score = jnp.where(causal[None], score, -jnp.inf)
        _, sel = lax.top_k(score, topk)
        ckv_sel = jax.vmap(lambda cb, ib: cb[ib])(c_kv, sel)
        q_abs = jnp.einsum('bthd,hcd->bthc', qb, w_uk)
        logits = jnp.einsum('bthc,btkc->bthk', q_abs, ckv_sel).astype(jnp.float32) * scale
        dist = tpos[None, :, None] - sel
        bias = jnp.moveaxis(rel_bias[t5_bucket(dist)], -1, 2)
        logits = logits + bias.astype(jnp.float32)
        logits = jnp.where((dist >= 0)[:, :, None, :], logits, -jnp.inf)
        p = jax.nn.softmax(logits, axis=-1).astype(ckv_sel.dtype)
        o_lat = jnp.einsum('bthk,btkc->bthc', p, ckv_sel)
        return jnp.einsum('bthc,hcd->bthd', o_lat, w_uv)

    out = lax.map(one_block, (jnp.arange(nb), blocks(q), blocks(q_idx), blocks(w_idx)))
    return jnp.moveaxis(out, 0, 1).reshape(B, S, W_A)


def mlstm_chunkwise(q, k, v, i_pre, logf):
    B, H, S, dk = q.shape
    dv = v.shape[-1]
    nc = S // CHUNK
    tri = jnp.tril(jnp.ones((CHUNK, CHUNK), dtype=bool))

    def to_chunks(a):
        return jnp.moveaxis(a.reshape(B, H, nc, CHUNK, *a.shape[3:]), 2, 0)

    def step(carry, inp):
        C, n, m = carry
        qc, kc, vc, ic, fc = inp
        b = jnp.cumsum(fc, axis=-1)
        d_log = jnp.where(tri, b[..., :, None] - b[..., None, :] + ic[..., None, :], -jnp.inf)
        inter_log = b + m[..., None]
        m_j = jnp.maximum(inter_log, jnp.max(d_log, axis=-1))
        s = jnp.einsum('bhjd,bhsd->bhjs', qc, kc) * jnp.exp(d_log - m_j[..., None])
        w_inter = jnp.exp(inter_log - m_j)
        num = (w_inter[..., None] * jnp.einsum('bhjd,bhde->bhje', qc, C)
               + jnp.einsum('bhjs,bhse->bhje', s, vc))
        den = w_inter * jnp.einsum('bhjd,bhd->bhj', qc, n) + jnp.sum(s, axis=-1)
        h = num / jnp.maximum(jnp.abs(den), jnp.exp(-m_j))[..., None]
        b_last = b[..., -1]
        g = b_last[..., None] - b + ic
        m_new = jnp.maximum(b_last + m, jnp.max(g, axis=-1))
        w = jnp.exp(g - m_new[..., None])
        decay = jnp.exp(b_last + m - m_new)
        C_new = decay[..., None, None] * C + jnp.einsum('bhs,bhsd,bhse->bhde', w, kc, vc)
        n_new = decay[..., None] * n + jnp.einsum('bhs,bhsd->bhd', w, kc)
        return (C_new, n_new, m_new), h

    init = (jnp.zeros((B, H, dk, dv), jnp.float32), jnp.zeros((B, H, dk), jnp.float32),
            jnp.zeros((B, H), jnp.float32))
    _, hs = lax.scan(step, init, (to_chunks(q), to_chunks(k), to_chunks(v),
                                  to_chunks(i_pre), to_chunks(logf)))
    return jnp.moveaxis(hs, 0, 2).reshape(B, H, S, dv)


def causal_conv(x, w, b):
    C = x.shape[-1]
    y = lax.conv_general_dilated(x, w[:, None, :].astype(x.dtype), window_strides=(1,),
                                 padding=[(CONV_WIDTH - 1, 0)],
                                 dimension_numbers=('NWC', 'WIO', 'NWC'),
                                 feature_group_count=C)
    return y + b


def hybrid_mixer(h, w_in, conv_w, conv_b, kv_norm, w_uk, w_uv, gate_bias, head_norm,
                 rel_bias, w_branch_attn, w_branch_mlstm, w_out):
    B, S, _ = h.shape
    split_points = np.cumsum(SPLITS)[:-1].tolist()
    (q_a, c_kv, q_idx, k_idx, w_idx, q_m, k_m, v_m, i_pre, f_pre, o_pre,
     gate_a, gate_m) = jnp.split(h @ w_in, split_points, axis=-1)

    c_kv = rms_norm(c_kv, kv_norm)
    y_a = dsa_attention(q_a.reshape(B, S, N_HEADS_A, HEAD_DIM_A), c_kv,
                        q_idx.reshape(B, S, N_HEADS_IDX, HEAD_DIM_IDX), k_idx,
                        w_idx * IDX_SCALE, w_uk, w_uv, rel_bias)

    qk = jax.nn.silu(causal_conv(jnp.concatenate([q_m, k_m], axis=-1), conv_w, conv_b))
    q_m, k_m = jnp.split(qk, 2, axis=-1)

    def heads(a):
        return a.reshape(B, S, N_HEADS_M, -1).transpose(0, 2, 1, 3).astype(jnp.float32)

    i_g = (i_pre + gate_bias[:N_HEADS_M]).astype(jnp.float32).transpose(0, 2, 1)
    logf = jax.nn.log_sigmoid((f_pre + gate_bias[N_HEADS_M:]).astype(jnp.float32)).transpose(0, 2, 1)
    h_m = mlstm_chunkwise(heads(q_m), heads(k_m) * (HEAD_DIM_M ** -0.5), heads(v_m), i_g, logf)
    mu = jnp.mean(h_m, axis=-1, keepdims=True)
    var = jnp.mean(jnp.square(h_m - mu), axis=-1, keepdims=True)
    h_m = (h_m - mu) * lax.rsqrt(var + EPS)
    h_m = h_m.transpose(0, 2, 1, 3).reshape(B, S, W_M) * head_norm.astype(jnp.float32)
    h_m = h_m.astype(h.dtype) * jax.nn.sigmoid(o_pre)

    merged = (jax.nn.sigmoid(gate_a) * (y_a @ w_branch_attn)
              + jax.nn.sigmoid(gate_m) * (h_m @ w_branch_mlstm))
    return merged @ w_out


def setup_inputs(seed: int = 0) -> dict:
    key = jax.random.key(seed)
    ks = iter(jax.random.split(key, 40))
    nrm = lambda shape, s: jax.random.normal(next(ks), shape, jnp.float32) * s
    gain = lambda shape: 1.0 + nrm(shape, 0.05)
    L = DEPTH
    gate_bias = jnp.concatenate([nrm((L, N_HEADS_M), 0.1),
                                 F_BIAS_INIT + nrm((L, N_HEADS_M), 0.5)], axis=-1)
    return {
        "x": nrm((BATCH, SEQ, D_MODEL), 1.0),
        "c": nrm((BATCH, D_MODEL), 1.0),
        "ada_w": nrm((L, D_MODEL, 9 * D_MODEL), 0.5 * D_MODEL ** -0.5),
        "ada_b": nrm((L, 9 * D_MODEL), 0.01),
        "ffn1_norm": gain((L, D_MODEL)),
        "ffn1_w1": nrm((L, D_MODEL, D_FF), D_MODEL ** -0.5),
        "ffn1_w3": nrm((L, D_MODEL, D_FF), D_MODEL ** -0.5),
        "ffn1_w2": nrm((L, D_FF, D_MODEL), D_FF ** -0.5),
        "mix_norm": gain((L, D_MODEL)),
        "w_in": nrm((L, D_MODEL, D_IN), D_MODEL ** -0.5),
        "conv_w": nrm((L, CONV_WIDTH, 2 * W_M), CONV_WIDTH ** -0.5),
        "conv_b": nrm((L, 2 * W_M), 0.01),
        "kv_norm": gain((L, D_LATENT)),
        "w_uk": nrm((L, N_HEADS_A, D_LATENT, HEAD_DIM_A), D_LATENT ** -0.5),
        "w_uv": nrm((L, N_HEADS_A, D_LATENT, HEAD_DIM_A), D_LATENT ** -0.5),
        "mlstm_gate_bias": gate_bias,
        "mlstm_head_norm": gain((L, W_M)),
        "rel_bias": nrm((N_BUCKETS, N_HEADS_A), 0.5),
        "w_branch_attn": nrm((L, W_A, D_MODEL), W_A ** -0.5),
        "w_branch_mlstm": nrm((L, W_M, D_MODEL), W_M ** -0.5),
        "w_out": nrm((L, D_MODEL, D_MODEL), D_MODEL ** -0.5),
        "ffn2_norm": gain((L, D_MODEL)),
        "ffn2_w1": nrm((L, D_MODEL, D_FF), D_MODEL ** -0.5),
        "ffn2_w3": nrm((L, D_MODEL, D_FF), D_MODEL ** -0.5),
        "ffn2_w2": nrm((L, D_FF, D_MODEL), D_FF ** -0.5),
        "final_norm": gain((D_MODEL,)),
    }


def reference(x, c, ada_w, ada_b, ffn1_norm, ffn1_w1, ffn1_w3, ffn1_w2, mix_norm, w_in,
              conv_w, conv_b, kv_norm, w_uk, w_uv, mlstm_gate_bias, mlstm_head_norm,
              rel_bias, w_branch_attn, w_branch_mlstm, w_out, ffn2_norm, ffn2_w1, ffn2_w3,
              ffn2_w2, final_norm):
    cond = jax.nn.silu(c)
    for l in range(DEPTH):
        mod = cond @ ada_w[l] + ada_b[l]
        sh1, sc1, g1, sh2, sc2, g2, sh3, sc3, g3 = jnp.split(mod, 9, axis=-1)
        h = modulate(rms_norm(x, ffn1_norm[l]), sh1, sc1)
        x = x + 0.5 * g1[:, None, :] * swiglu(h, ffn1_w1[l], ffn1_w3[l], ffn1_w2[l])
        h = modulate(rms_norm(x, mix_norm[l]), sh2, sc2)
        x = x + g2[:, None, :] * hybrid_mixer(h, w_in[l], conv_w[l], conv_b[l], kv_norm[l],
                                              w_uk[l], w_uv[l], mlstm_gate_bias[l],
                                              mlstm_head_norm[l], rel_bias, w_branch_attn[l],
                                              w_branch_mlstm[l], w_out[l])
        h = modulate(rms_norm(x, ffn2_norm[l]), sh3, sc3)
        x = x + 0.5 * g3[:, None, :] * swiglu(h, ffn2_w1[l], ffn2_w3[l], ffn2_w2[l])
    return rms_norm(x, final_norm)
```

```python
import functools
import math

import jax
import jax.numpy as jnp
from jax import lax
from jax.experimental import pallas as pl
from jax.experimental.pallas import tpu as pltpu

F32 = jnp.float32
BF16 = jnp.bfloat16
I32 = jnp.int32

LANES = 128
MXU_DIM = 256
VMEM_LIMIT = 56 * 1024 * 1024

N_HEADS_A = 8
HEAD_DIM_A = 64
D_LATENT = 256
N_HEADS_IDX = 8
HEAD_DIM_IDX = 64
TOPK_MAX = 256
Q_BLOCK = 128
N_BUCKETS = 32
MAX_DISTANCE = 128
N_HEADS_M = 4
HEAD_DIM_M = 128
CONV_WIDTH = 4
EPS = 1e-6
IDX_SCALE = (N_HEADS_IDX ** -0.5) * (HEAD_DIM_IDX ** -0.5)
W_A = N_HEADS_A * HEAD_DIM_A
W_M = N_HEADS_M * HEAD_DIM_M

FFN_TM = 512
FFN_CHUNK = 256
MIX_TM = 512
KEY_BLOCK = 256
MLSTM_CHUNK = 256
MERGE_TM = 512
NEG_BIG = -1e30
INT_MIN = -2 ** 31

BIAS_PAD = 384
BIAS_ROWS = KEY_BLOCK + BIAS_PAD


def _sigmoid(x):
    return 1.0 / (1.0 + jnp.exp(-x))


def _log_sigmoid(x):
    return jnp.minimum(x, 0.0) - jnp.log(1.0 + jnp.exp(-jnp.abs(x)))


def _rms_norm(x, gain):
    ms = jnp.mean(x * x, axis=-1, keepdims=True)
    return x * lax.rsqrt(ms + EPS) * gain


def _split3(x):
    hi = x.astype(BF16)
    r1 = x - hi.astype(F32)
    mid = r1.astype(BF16)
    lo = (r1 - mid.astype(F32)).astype(BF16)
    return hi, mid, lo


def _resident(shape):
    nd = len(shape)
    return pl.BlockSpec(shape, lambda *_: (0,) * nd, pipeline_mode=pl.Buffered(1))


def _adaln_kernel(c_ref, w_ref, b_ref, o_ref):
    c = c_ref[...]
    cond = c * _sigmoid(c)
    o_ref[...] = jnp.dot(cond.astype(BF16), w_ref[...].astype(BF16),
                         preferred_element_type=F32) + b_ref[...]


def _adaln(c, ada_w, ada_b):
    b, d = c.shape
    n = ada_w.shape[1]
    rows = 8
    c_pad = jnp.zeros((rows, d), F32).at[:b].set(c)
    tn = 1024
    out = pl.pallas_call(
        _adaln_kernel,
        out_shape=jax.ShapeDtypeStruct((rows, n), F32),
        grid=(n // tn,),
        in_specs=[pl.BlockSpec((rows, d), lambda j: (0, 0)),
                  pl.BlockSpec((d, tn), lambda j: (0, j)),
                  pl.BlockSpec((1, tn), lambda j: (0, j))],
        out_specs=pl.BlockSpec((rows, tn), lambda j: (0, j)),
        compiler_params=pltpu.CompilerParams(dimension_semantics=("arbitrary",),
                                             vmem_limit_bytes=VMEM_LIMIT),
        name="adaln",
    )(c_pad, ada_w, ada_b.reshape(1, n))
    return out[:b]


def _t5_bucket(dist):
    n = jnp.maximum(dist, 0)
    max_exact = N_BUCKETS // 2
    nf = jnp.maximum(n, 1).astype(F32)
    large = max_exact + (jnp.log(nf / max_exact) / math.log(MAX_DISTANCE / max_exact)
                         * (N_BUCKETS - max_exact)).astype(I32)
    large = jnp.minimum(large, N_BUCKETS - 1)
    return jnp.where(n < max_exact, n, large)


def _bias_kernel(rel_ref, tile_ref, far_ref, *, far_dist):
    r = lax.broadcasted_iota(I32, (BIAS_ROWS, LANES), 0)
    i = lax.broadcasted_iota(I32, (BIAS_ROWS, LANES), 1)
    bucket = _t5_bucket(i - r + BIAS_PAD)
    far_bucket = _t5_bucket(jnp.full((1, LANES), far_dist, I32))
    for h in range(N_HEADS_A):
        acc = jnp.zeros((BIAS_ROWS, LANES), F32)
        far = jnp.zeros((1, LANES), F32)
        for bkt in range(N_BUCKETS):
            val = rel_ref[bkt, h]
            acc = jnp.where(bucket == bkt, val, acc)
            far = jnp.where(far_bucket == bkt, val, far)
        tile_ref[h] = acc
        far_ref[:, h * LANES:(h + 1) * LANES] = far


def _bias_tiles(rel_bias, seq):
    return pl.pallas_call(
        functools.partial(_bias_kernel, far_dist=seq - 1),
        out_shape=(jax.ShapeDtypeStruct((N_HEADS_A, BIAS_ROWS, LANES), F32),
                   jax.ShapeDtypeStruct((1, N_HEADS_A * LANES), F32)),
        in_specs=[pl.BlockSpec(memory_space=pltpu.SMEM)],
        out_specs=(pl.BlockSpec(memory_space=pltpu.VMEM), pl.BlockSpec(memory_space=pltpu.VMEM)),
        name="bias_tiles",
    )(rel_bias)


def _ffn_kernel(x_ref, gain_ref, sh_ref, sc_ref, g_ref, w1_ref, w3_ref, w2_ref, fin_ref, o_ref,
                h_scr, acc_scr, *, n_chunks, final_norm):
    x = x_ref[...]
    h = _rms_norm(x, gain_ref[...]) * (1.0 + sc_ref[0]) + sh_ref[0]
    h_scr[...] = h.astype(BF16)
    for j in range(n_chunks):
        hb = h_scr[...]
        u1 = jnp.dot(hb, w1_ref[j], preferred_element_type=F32)
        u3 = jnp.dot(hb, w3_ref[j], preferred_element_type=F32)
        a = (u1 * _sigmoid(u1)) * u3
        part = jnp.dot(a.astype(BF16), w2_ref[j], preferred_element_type=F32)
        if j == 0:
            acc_scr[...] = part
        else:
            acc_scr[...] += part
    out = x + (0.5 * g_ref[0]) * acc_scr[...]
    if final_norm:
        out = _rms_norm(out, fin_ref[...])
    o_ref[...] = out


def _ffn(x, gain, sh, sc, g, w1, w3, w2, fin, *, seq, final_norm):
    t, d = x.shape
    dff = w1.shape[1]
    nch = dff // FFN_CHUNK
    w1c = w1.astype(BF16).reshape(d, nch, FFN_CHUNK).transpose(1, 0, 2)
    w3c = w3.astype(BF16).reshape(d, nch, FFN_CHUNK).transpose(1, 0, 2)
    w2c = w2.astype(BF16).reshape(nch, FFN_CHUNK, d)
    tm = FFN_TM
    per_b = seq // tm
    mod_spec = pl.BlockSpec((1, 1, d), lambda i: (i // per_b, 0, 0))
    return pl.pallas_call(
        functools.partial(_ffn_kernel, n_chunks=nch, final_norm=final_norm),
        out_shape=jax.ShapeDtypeStruct((t, d), F32),
        grid=(t // tm,),
        in_specs=[pl.BlockSpec((tm, d), lambda i: (i, 0)),
                  _resident((1, d)), mod_spec, mod_spec, mod_spec,
                  _resident((nch, d, FFN_CHUNK)), _resident((nch, d, FFN_CHUNK)),
                  _resident((nch, FFN_CHUNK, d)), _resident((1, d))],
        out_specs=pl.BlockSpec((tm, d), lambda i: (i, 0)),
        scratch_shapes=[pltpu.VMEM((tm, d), BF16), pltpu.VMEM((tm, d), F32)],
        compiler_params=pltpu.CompilerParams(dimension_semantics=("arbitrary",),
                                             vmem_limit_bytes=VMEM_LIMIT),
        name="ffn_final" if final_norm else "ffn",
    )(x, gain.reshape(1, d), sh, sc, g, w1c, w3c, w2c, fin.reshape(1, d))


_C_QA = 0
_C_CKV = _C_QA + N_HEADS_A * LANES
_C_QI = _C_CKV + D_LATENT
_C_KI = _C_QI + N_HEADS_IDX * LANES
_C_SM = _C_KI + LANES
_C_QK = _C_SM + LANES
_C_V = _C_QK + 2 * W_M
_C_O = _C_V + W_M
_C_GA = _C_O + W_M
_C_GM = _C_GA + 1024
_C_END = _C_GM + 1024
_SM_W = 0
_SM_I = N_HEADS_IDX
_SM_F = N_HEADS_IDX + N_HEADS_M


def _pack_w_in(w_in, d_model):
    splits = (W_A, D_LATENT, N_HEADS_IDX * HEAD_DIM_IDX, HEAD_DIM_IDX, N_HEADS_IDX,
              W_M, W_M, W_M, N_HEADS_M, N_HEADS_M, W_M, d_model, d_model)
    offs = [0]
    for s in splits:
        offs.append(offs[-1] + s)
    (q_a, c_kv, q_i, k_i, w_i, q_m, k_m, v_m, i_p, f_p, o_p, g_a, g_m) = [
        w_in[:, offs[n]:offs[n + 1]] for n in range(len(splits))]
    d = w_in.shape[0]

    def pad_heads(w, nh, hd):
        w = w.reshape(d, nh, hd)
        return jnp.pad(w, ((0, 0), (0, 0), (0, LANES - hd))).reshape(d, nh * LANES)

    small = jnp.concatenate([w_i, i_p, f_p], axis=1)
    small = jnp.pad(small, ((0, 0), (0, LANES - small.shape[1])))
    packed = jnp.concatenate([
        pad_heads(q_a, N_HEADS_A, HEAD_DIM_A), c_kv, pad_heads(q_i, N_HEADS_IDX, HEAD_DIM_IDX),
        jnp.pad(k_i, ((0, 0), (0, LANES - HEAD_DIM_IDX))), small, q_m, k_m, v_m, o_p, g_a, g_m], axis=1)
    assert packed.shape[1] == _C_END
    return packed.astype(BF16)


def _mixin_kernel(x_ref, gain_ref, sh_ref, sc_ref, w_ref, kvn_ref, wuk_ref,
                  qabs_ref, qidx_ref, kidx_ref, ckv_ref, ckvt_ref, wt_ref, ifc_ref, ift_ref,
                  qk_ref, v_ref, o_ref, ga_ref, gm_ref, h_scr, *, tm):
    nqb = tm // Q_BLOCK
    x = x_ref[...]
    h = _rms_norm(x, gain_ref[...]) * (1.0 + sc_ref[0]) + sh_ref[0]
    h_scr[...] = h.astype(BF16)

    def proj(lo, hi):
        return jnp.dot(h_scr[...], w_ref[:, lo:hi], preferred_element_type=F32)

    qa = proj(_C_QA, _C_CKV)
    scale = HEAD_DIM_A ** -0.5
    for hh in range(N_HEADS_A):
        q_h = qa[:, hh * LANES:(hh + 1) * LANES].astype(BF16)
        q_abs = jnp.dot(q_h, wuk_ref[hh], preferred_element_type=F32) * scale
        qabs_ref[:, hh] = q_abs.astype(BF16).reshape(nqb, Q_BLOCK, D_LATENT)
    ckv = _rms_norm(proj(_C_CKV, _C_QI), kvn_ref[...])
    ckv_ref[...] = ckv.astype(BF16)
    ckv_t = ckv.T
    for j in range(tm // KEY_BLOCK):
        ckvt_ref[j] = ckv_t[:, j * KEY_BLOCK:(j + 1) * KEY_BLOCK].astype(BF16)
    qi = proj(_C_QI, _C_KI)
    for hh in range(N_HEADS_IDX):
        qidx_ref[:, hh] = qi[:, hh * LANES:(hh + 1) * LANES].astype(BF16).reshape(nqb, Q_BLOCK, LANES)
    kidx_ref[...] = proj(_C_KI, _C_SM).astype(BF16)
    small = proj(_C_SM, _C_QK)
    ifc_ref[...] = small
    small_t = small.T
    wt_ref[...] = small_t[_SM_W:_SM_W + N_HEADS_IDX] * IDX_SCALE
    ift_ref[...] = small_t[_SM_I:_SM_I + 2 * N_HEADS_M]
    qk_ref[...] = proj(_C_QK, _C_V)
    v_ref[...] = proj(_C_V, _C_O).astype(BF16)
    o_ref[...] = proj(_C_O, _C_GA)
    ga_ref[...] = proj(_C_GA, _C_GM)
    gm_ref[...] = proj(_C_GM, _C_END)


def _mixin(x, gain, sh, sc, w_packed, kv_norm, wuk_t, *, seq):
    t, d = x.shape
    tm = MIX_TM
    per_b = seq // tm
    nqb = tm // Q_BLOCK
    row = lambda w: pl.BlockSpec((tm, w), lambda i: (i, 0))
    mod_spec = pl.BlockSpec((1, 1, d), lambda i: (i // per_b, 0, 0))
    out_shape = (
        jax.ShapeDtypeStruct((t // Q_BLOCK, N_HEADS_A, Q_BLOCK, D_LATENT), BF16),
        jax.ShapeDtypeStruct((t // Q_BLOCK, N_HEADS_IDX, Q_BLOCK, LANES), BF16),
        jax.ShapeDtypeStruct((t, LANES), BF16),
        jax.ShapeDtypeStruct((t, D_LATENT), BF16),
        jax.ShapeDtypeStruct((t // KEY_BLOCK, D_LATENT, KEY_BLOCK), BF16),
        jax.ShapeDtypeStruct((N_HEADS_IDX, t), F32),
        jax.ShapeDtypeStruct((t, LANES), F32),
        jax.ShapeDtypeStruct((2 * N_HEADS_M, t), F32),
        jax.ShapeDtypeStruct((t, 2 * W_M), F32),
        jax.ShapeDtypeStruct((t, W_M), BF16),
        jax.ShapeDtypeStruct((t, W_M), F32),
        jax.ShapeDtypeStruct((t, d), F32),
        jax.ShapeDtypeStruct((t, d), F32),
    )
    out_specs = (
        pl.BlockSpec((nqb, N_HEADS_A, Q_BLOCK, D_LATENT), lambda i: (i, 0, 0, 0)),
        pl.BlockSpec((nqb, N_HEADS_IDX, Q_BLOCK, LANES), lambda i: (i, 0, 0, 0)),
        row(LANES), row(D_LATENT),
        pl.BlockSpec((tm // KEY_BLOCK, D_LATENT, KEY_BLOCK), lambda i: (i, 0, 0)),
        pl.BlockSpec((N_HEADS_IDX, tm), lambda i: (0, i)),
        row(LANES),
        pl.BlockSpec((2 * N_HEADS_M, tm), lambda i: (0, i)),
        row(2 * W_M), row(W_M), row(W_M), row(d), row(d),
    )
    return pl.pallas_call(
        functools.partial(_mixin_kernel, tm=tm),
        out_shape=out_shape,
        grid=(t // tm,),
        in_specs=[pl.BlockSpec((tm, d), lambda i: (i, 0)), _resident((1, d)), mod_spec, mod_spec,
                  _resident((d, _C_END)), _resident((1, D_LATENT)),
                  _resident((N_HEADS_A, LANES, D_LATENT))],
        out_specs=out_specs,
        scratch_shapes=[pltpu.VMEM((tm, d), BF16)],
        compiler_params=pltpu.CompilerParams(dimension_semantics=("arbitrary",),
                                             vmem_limit_bytes=VMEM_LIMIT),
        name="mixin",
    )(x, gain.reshape(1, d), sh, sc, w_packed, kv_norm.reshape(1, D_LATENT), wuk_t)


def _sortable_key(score):
    bits = pltpu.bitcast(score, I32)
    bits = jnp.where(bits == INT_MIN, 0, bits)
    return jnp.where(bits < 0, bits ^ 0x7FFFFFFF, bits)


def _dsa_kernel(qidx_ref, qabs_ref, wt_ref, kidx_ref, ckv_ref, ckvt_ref, btile_ref, bfar_ref, wuvt_ref,
                out_ref, keys_scr, acc_scr, m_scr, l_scr, *, topk):
    kb_sz = KEY_BLOCK
    qb = pl.program_id(1)
    n_kb = qb // (kb_sz // Q_BLOCK) + 1
    q0 = qb * Q_BLOCK
    row_id = lax.broadcasted_iota(I32, (kb_sz, LANES), 0)
    lane_id = lax.broadcasted_iota(I32, (kb_sz, LANES), 1)

    q_idx = qidx_ref[0].reshape(N_HEADS_IDX * Q_BLOCK, LANES)
    w_t = wt_ref[...]

    def score_body(kb, carry):
        k_blk = kidx_ref[kb]
        s_t = lax.dot_general(k_blk, q_idx, (((1,), (1,)), ((), ())),
                              preferred_element_type=F32)
        score = jnp.zeros((kb_sz, LANES), F32)
        for hh in range(N_HEADS_IDX):
            score = score + jnp.maximum(s_t[:, hh * LANES:(hh + 1) * LANES], 0.0) * w_t[hh:hh + 1, :]
        valid = (kb * kb_sz + row_id) <= (q0 + lane_id)
        keys_scr[kb] = jnp.where(valid, _sortable_key(score), INT_MIN)
        return carry

    lax.fori_loop(0, n_kb, score_body, 0)

    def count(pred_fn):
        def body(kb, acc):
            hit = jnp.where(pred_fn(keys_scr[kb], kb), 1, 0)
            return acc + jnp.sum(hit.reshape(kb_sz // 8, 8, LANES), axis=0)
        acc = lax.fori_loop(0, n_kb, body, jnp.zeros((8, LANES), I32))
        return jnp.sum(acc, axis=0, keepdims=True)

    def ge_count(trial):
        return count(lambda k, kb: k >= trial)

    cand0 = jnp.full((1, LANES), INT_MIN, I32)
    zero = jnp.zeros((1, LANES), I32)
    cand = jnp.where(ge_count(zero) >= topk, zero, cand0)

    def bit_body(it, cand):
        trial = cand | (jnp.int32(1) << (30 - it))
        return jnp.where(ge_count(trial) >= topk, trial, cand)

    tau = lax.fori_loop(0, 31, bit_body, cand)

    n_gt = count(lambda k, kb: k > tau)
    n_eq = count(lambda k, kb: k == tau)
    need = topk - n_gt
    overflow = n_eq > need
    seq_bits = max(1, (keys_scr.shape[0] * kb_sz - 1).bit_length())

    def idx_body(it, jc):
        trial = jc | (jnp.int32(1) << (seq_bits - 1 - it))
        c = count(lambda k, kb: (k == tau) & ((kb * kb_sz + row_id) < trial))
        return jnp.where(c < need, trial, jc)

    any_overflow = jnp.max(jnp.where(overflow, 1, 0)) > 0
    j_cut = lax.cond(any_overflow,
                     lambda: lax.fori_loop(0, seq_bits, idx_body, jnp.zeros((1, LANES), I32)),
                     lambda: jnp.zeros((1, LANES), I32))
    j_lim = jnp.where(overflow, j_cut, jnp.int32(2 ** 30))

    q_abs = qabs_ref[0].reshape(N_HEADS_A * Q_BLOCK, D_LATENT)
    m_scr[...] = jnp.full(m_scr.shape, NEG_BIG, F32)
    l_scr[...] = jnp.zeros(l_scr.shape, F32)
    acc_scr[...] = jnp.zeros(acc_scr.shape, F32)

    def attend(kb, near):
        keys = keys_scr[kb]
        s_idx = kb * kb_sz + row_id
        sel = (keys > tau) | ((keys == tau) & (s_idx <= j_lim))
        sel = sel & (s_idx <= (q0 + lane_id))
        lt = lax.dot_general(ckv_ref[kb], q_abs, (((1,), (1,)), ((), ())),
                             preferred_element_type=F32)
        if near:
            start = pl.multiple_of(BIAS_PAD - (q0 - kb * kb_sz), LANES)
        parts = []
        for hh in range(N_HEADS_A):
            sl = slice(hh * LANES, (hh + 1) * LANES)
            if near:
                bias = btile_ref[hh, pl.ds(start, kb_sz), :]
            else:
                bias = bfar_ref[:, sl]
            parts.append(jnp.where(sel, lt[:, sl] + bias, NEG_BIG))
        lt = jnp.concatenate(parts, axis=1)
        m_old = m_scr[...]
        m_new = jnp.maximum(m_old, jnp.max(lt, axis=0, keepdims=True))
        alpha = jnp.exp(m_old - m_new)
        p = jnp.exp(lt - m_new)
        l_scr[...] = alpha * l_scr[...] + jnp.sum(p, axis=0, keepdims=True)
        m_scr[...] = m_new
        pv = jnp.dot(ckvt_ref[kb], p.astype(BF16), preferred_element_type=F32)
        acc_scr[...] = alpha * acc_scr[...] + pv

    def far_body(kb, carry):
        attend(kb, False)
        return carry

    lax.fori_loop(0, jnp.maximum(n_kb - 2, 0), far_body, 0)

    @pl.when(n_kb >= 2)
    def _():
        attend(n_kb - 2, True)

    attend(n_kb - 1, True)

    o_lat = acc_scr[...] * (1.0 / l_scr[...])
    ys = []
    for hh in range(N_HEADS_A):
        o_h = o_lat[:, hh * LANES:(hh + 1) * LANES].astype(BF16)
        ys.append(jnp.dot(wuvt_ref[hh], o_h, preferred_element_type=F32))
    y_t = jnp.concatenate(ys, axis=0)
    out_ref[...] = y_t.T.astype(BF16)


def _dsa(q_idx, q_abs, w_t, k_idx, ckv, ckv_t, btile, bfar, wuv_t, *, batch, seq):
    t = batch * seq
    nqb = seq // Q_BLOCK
    nkb = seq // KEY_BLOCK
    topk = min(TOPK_MAX, seq // 4)
    k_idx3 = k_idx.reshape(t // KEY_BLOCK, KEY_BLOCK, LANES)
    ckv3 = ckv.reshape(t // KEY_BLOCK, KEY_BLOCK, D_LATENT)
    per_batch = lambda shape: pl.BlockSpec(shape, lambda b, q: (b,) + (0,) * (len(shape) - 1),
                                           pipeline_mode=pl.Buffered(1))
    return pl.pallas_call(
        functools.partial(_dsa_kernel, topk=topk),
        out_shape=jax.ShapeDtypeStruct((t, W_A), BF16),
        grid=(batch, nqb),
        in_specs=[pl.BlockSpec((1, N_HEADS_IDX, Q_BLOCK, LANES), lambda b, q: (b * nqb + q, 0, 0, 0)),
                  pl.BlockSpec((1, N_HEADS_A, Q_BLOCK, D_LATENT), lambda b, q: (b * nqb + q, 0, 0, 0)),
                  pl.BlockSpec((N_HEADS_IDX, Q_BLOCK), lambda b, q: (0, b * nqb + q)),
                  per_batch((nkb, KEY_BLOCK, LANES)),
                  per_batch((nkb, KEY_BLOCK, D_LATENT)),
                  per_batch((nkb, D_LATENT, KEY_BLOCK)),
                  _resident((N_HEADS_A, BIAS_ROWS, LANES)),
                  _resident((1, N_HEADS_A * LANES)),
                  _resident((N_HEADS_A, HEAD_DIM_A, D_LATENT))],
        out_specs=pl.BlockSpec((Q_BLOCK, W_A), lambda b, q: (b * nqb + q, 0)),
        scratch_shapes=[pltpu.VMEM((nkb, KEY_BLOCK, LANES), I32),
                        pltpu.VMEM((D_LATENT, N_HEADS_A * LANES), F32),
                        pltpu.VMEM((1, N_HEADS_A * LANES), F32),
                        pltpu.VMEM((1, N_HEADS_A * LANES), F32)],
        compiler_params=pltpu.CompilerParams(dimension_semantics=("arbitrary", "arbitrary"),
                                             vmem_limit_bytes=VMEM_LIMIT),
        name="dsa",
    )(q_idx, q_abs, w_t, k_idx3, ckv3, ckv_t, btile, bfar, wuv_t)


def _mlstm_kernel(qk_ref, v_ref, o_ref, ifc_ref, ift_ref, cw_ref, cb_ref, gbc_ref, gbt_ref, hn_ref,
                  out_ref, c_scr, n_scr, m_scr, tail_scr, *, chunk):
    L = chunk
    step = pl.program_id(1)

    @pl.when(step == 0)
    def _():
        c_scr[...] = jnp.zeros(c_scr.shape, F32)
        n_scr[...] = jnp.zeros(n_scr.shape, F32)
        m_scr[...] = jnp.zeros(m_scr.shape, F32)
        tail_scr[...] = jnp.zeros(tail_scr.shape, F32)

    x = qk_ref[...]
    xe = jnp.concatenate([tail_scr[...], x], axis=0)
    conv = x * cw_ref[CONV_WIDTH - 1:CONV_WIDTH, :] + cb_ref[...]
    for d in range(1, CONV_WIDTH):
        shifted = pltpu.roll(xe, d, axis=0)[8:]
        conv = conv + shifted * cw_ref[CONV_WIDTH - 1 - d:CONV_WIDTH - d, :]
    tail_scr[...] = x[L - 8:]
    qk = conv * _sigmoid(conv)

    a_c = ifc_ref[...] + gbc_ref[...]
    ls_c = _log_sigmoid(a_c)
    a_t = ift_ref[...] + gbt_ref[...]
    ls_t = _log_sigmoid(a_t)
    rr = lax.broadcasted_iota(I32, (L, L), 0)
    cc = lax.broadcasted_iota(I32, (L, L), 1)
    causal = cc <= rr
    tril = jnp.where(causal, 1.0, 0.0).astype(BF16)
    triu = jnp.where(rr <= cc, 1.0, 0.0).astype(BF16)
    b_c = sum(jnp.dot(tril, piece, preferred_element_type=F32) for piece in _split3(ls_c))
    b_t = sum(jnp.dot(piece, triu, preferred_element_type=F32) for piece in _split3(ls_t))

    o_gate = _sigmoid(o_ref[...])
    for hh in range(N_HEADS_M):
        hs = slice(hh * HEAD_DIM_M, (hh + 1) * HEAD_DIM_M)
        q = qk[:, hs]
        k = qk[:, W_M + hh * HEAD_DIM_M:W_M + (hh + 1) * HEAD_DIM_M] * (HEAD_DIM_M ** -0.5)
        v = v_ref[:, hs]
        ci, cf = _SM_I + hh, _SM_F + hh
        i_c = a_c[:, ci:ci + 1]
        bc = b_c[:, cf:cf + 1]
        i_t = a_t[hh:hh + 1, :]
        bt = b_t[N_HEADS_M + hh:N_HEADS_M + hh + 1, :]
        b_last = bc[L - 1:L, :]
        m_prev = m_scr[hh]
        c_prev = c_scr[hh]
        n_prev = n_scr[hh]

        d_log = jnp.where(causal, bc - bt + i_t, NEG_BIG)
        inter_log = bc + m_prev
        m_j = jnp.maximum(inter_log, jnp.max(d_log, axis=1, keepdims=True))
        qb16 = q.astype(BF16)
        s = lax.dot_general(qb16, k.astype(BF16), (((1,), (1,)), ((), ())),
                            preferred_element_type=F32) * jnp.exp(d_log - m_j)
        w_inter = jnp.exp(inter_log - m_j)
        num = (w_inter * jnp.dot(qb16, c_prev.astype(BF16), preferred_element_type=F32)
               + jnp.dot(s.astype(BF16), v, preferred_element_type=F32))
        den = w_inter * jnp.sum(q * n_prev, axis=1, keepdims=True) + jnp.sum(s, axis=1, keepdims=True)
        hval = num / jnp.maximum(jnp.abs(den), jnp.exp(-m_j))

        g_t = b_last - bt + i_t
        g_c = b_last - bc + i_c
        m_new = jnp.maximum(b_last + m_prev, jnp.max(g_t, axis=1, keepdims=True))
        decay = jnp.exp(b_last + m_prev - m_new)
        kw = k * jnp.exp(g_c - m_new)
        c_scr[hh] = decay * c_prev + jnp.dot(kw.T.astype(BF16), v, preferred_element_type=F32)
        n_scr[hh] = decay * n_prev + jnp.sum(kw, axis=0, keepdims=True)
        m_scr[hh] = m_new

        mu = jnp.mean(hval, axis=1, keepdims=True)
        cen = hval - mu
        var = jnp.mean(cen * cen, axis=1, keepdims=True)
        hn = cen * lax.rsqrt(var + EPS) * hn_ref[:, hs]
        out_ref[:, hs] = (hn * o_gate[:, hs]).astype(BF16)


def _mlstm(qk_raw, v, o_pre, ifc, ift, conv_w, conv_b, gate_bias, head_norm, *, batch, seq):
    t = batch * seq
    L = MLSTM_CHUNK
    nc = seq // L
    gbc = jnp.zeros((1, LANES), F32).at[0, _SM_I:_SM_I + 2 * N_HEADS_M].set(gate_bias)
    gbt = jnp.broadcast_to(gate_bias.reshape(2 * N_HEADS_M, 1), (2 * N_HEADS_M, L))
    row = lambda w: pl.BlockSpec((L, w), lambda b, c: (b * nc + c, 0))
    return pl.pallas_call(
        functools.partial(_mlstm_kernel, chunk=L),
        out_shape=jax.ShapeDtypeStruct((t, W_M), BF16),
        grid=(batch, nc),
        in_specs=[row(2 * W_M), row(W_M), row(W_M), row(LANES),
                  pl.BlockSpec((2 * N_HEADS_M, L), lambda b, c: (0, b * nc + c)),
                  _resident((CONV_WIDTH, 2 * W_M)), _resident((1, 2 * W_M)),
                  _resident((1, LANES)), _resident((2 * N_HEADS_M, L)), _resident((1, W_M))],
        out_specs=row(W_M),
        scratch_shapes=[pltpu.VMEM((N_HEADS_M, HEAD_DIM_M, HEAD_DIM_M), F32),
                        pltpu.VMEM((N_HEADS_M, 1, HEAD_DIM_M), F32),
                        pltpu.VMEM((N_HEADS_M, 1, 1), F32),
                        pltpu.VMEM((8, 2 * W_M), F32)],
        compiler_params=pltpu.CompilerParams(dimension_semantics=("arbitrary", "arbitrary"),
                                             vmem_limit_bytes=VMEM_LIMIT),
        name="mlstm",
    )(qk_raw, v, o_pre, ifc, ift, conv_w, conv_b.reshape(1, -1), gbc, gbt, head_norm.reshape(1, -1))


def _merge_kernel(x_ref, ya_ref, hm_ref, ga_ref, gm_ref, g_ref, wa_ref, wm_ref, wo_ref, o_ref):
    pa = jnp.dot(ya_ref[...], wa_ref[...], preferred_element_type=F32)
    pm = jnp.dot(hm_ref[...], wm_ref[...], preferred_element_type=F32)
    merged = _sigmoid(ga_ref[...]) * pa + _sigmoid(gm_ref[...]) * pm
    out = jnp.dot(merged.astype(BF16), wo_ref[...], preferred_element_type=F32)
    o_ref[...] = x_ref[...] + g_ref[0] * out


def _merge(x, y_a, h_m, gate_a, gate_m, g, w_a, w_m, w_o, *, seq):
    t, d = x.shape
    tm = MERGE_TM
    per_b = seq // tm
    row = lambda w: pl.BlockSpec((tm, w), lambda i: (i, 0))
    return pl.pallas_call(
        _merge_kernel,
        out_shape=jax.ShapeDtypeStruct((t, d), F32),
        grid=(t // tm,),
        in_specs=[row(d), row(W_A), row(W_M), row(d), row(d),
                  pl.BlockSpec((1, 1, d), lambda i: (i // per_b, 0, 0)),
                  _resident((W_A, d)), _resident((W_M, d)), _resident((d, d))],
        out_specs=row(d),
        compiler_params=pltpu.CompilerParams(dimension_semantics=("arbitrary",),
                                             vmem_limit_bytes=VMEM_LIMIT),
        name="merge",
    )(x, y_a, h_m, gate_a, gate_m, g, w_a.astype(BF16), w_m.astype(BF16), w_o.astype(BF16))


def kernel(x, c, ada_w, ada_b, ffn1_norm, ffn1_w1, ffn1_w3, ffn1_w2, mix_norm, w_in, conv_w, conv_b,
           kv_norm, w_uk, w_uv, mlstm_gate_bias, mlstm_head_norm, rel_bias, w_branch_attn,
           w_branch_mlstm, w_out, ffn2_norm, ffn2_w1, ffn2_w3, ffn2_w2, final_norm):
    batch, seq, d = x.shape
    depth = ada_w.shape[0]
    assert seq % max(FFN_TM, MIX_TM, MERGE_TM, MLSTM_CHUNK, KEY_BLOCK) == 0
    t = batch * seq
    xf = x.reshape(t, d)
    btile, bfar = _bias_tiles(rel_bias, seq)
    for l in range(depth):
        mod = _adaln(c, ada_w[l], ada_b[l]).reshape(batch, 9, 1, d)
        sh1, sc1, g1, sh2, sc2, g2, sh3, sc3, g3 = [mod[:, n] for n in range(9)]
        xf = _ffn(xf, ffn1_norm[l], sh1, sc1, g1, ffn1_w1[l], ffn1_w3[l], ffn1_w2[l], final_norm,
                  seq=seq, final_norm=False)
        wuk_t = jnp.pad(w_uk[l].transpose(0, 2, 1), ((0, 0), (0, LANES - HEAD_DIM_A), (0, 0))).astype(BF16)
        (q_abs, q_idx, k_idx, ckv, ckv_t, w_t, ifc, ift, qk_raw, v_m, o_pre, gate_a, gate_m) = _mixin(
            xf, mix_norm[l], sh2, sc2, _pack_w_in(w_in[l], d), kv_norm[l], wuk_t, seq=seq)
        wuv_t = w_uv[l].transpose(0, 2, 1).astype(BF16)
        y_a = _dsa(q_idx, q_abs, w_t, k_idx, ckv, ckv_t, btile, bfar, wuv_t, batch=batch, seq=seq)
        h_m = _mlstm(qk_raw, v_m, o_pre, ifc, ift, conv_w[l], conv_b[l], mlstm_gate_bias[l],
                     mlstm_head_norm[l], batch=batch, seq=seq)
        xf = _merge(xf, y_a, h_m, gate_a, gate_m, g2, w_branch_attn[l], w_branch_mlstm[l], w_out[l],
                    seq=seq)
        xf = _ffn(xf, ffn2_norm[l], sh3, sc3, g3, ffn2_w1[l], ffn2_w3[l], ffn2_w2[l], final_norm,
                  seq=seq, final_norm=(l == depth - 1))
    return xf.reshape(batch, seq, d)
```

```python
import functools
import math

import jax
import jax.numpy as jnp
from jax import lax
from jax.experimental import pallas as pl
from jax.experimental.pallas import tpu as pltpu

F32 = jnp.float32
BF16 = jnp.bfloat16
I32 = jnp.int32

LANES = 128
MXU_DIM = 256
VMEM_LIMIT = 56 * 1024 * 1024

N_HEADS_A = 8
HEAD_DIM_A = 64
D_LATENT = 256
N_HEADS_IDX = 8
HEAD_DIM_IDX = 64
TOPK_MAX = 256
Q_BLOCK = 128
N_BUCKETS = 32
MAX_DISTANCE = 128
N_HEADS_M = 4
HEAD_DIM_M = 128
CONV_WIDTH = 4
EPS = 1e-6
IDX_SCALE = (N_HEADS_IDX ** -0.5) * (HEAD_DIM_IDX ** -0.5)
W_A = N_HEADS_A * HEAD_DIM_A
W_M = N_HEADS_M * HEAD_DIM_M

FFN_TM = 512
FFN_CHUNK = 256
MIX_TM = 512
KEY_BLOCK = 256
MLSTM_CHUNK = 256
MERGE_TM = 512
NEG_BIG = -1e30
INT_MIN = -2 ** 31

BIAS_PAD = 384
BIAS_ROWS = KEY_BLOCK + BIAS_PAD


def _sigmoid(x):
    return 1.0 / (1.0 + jnp.exp(-x))


def _log_sigmoid(x):
    return jnp.minimum(x, 0.0) - jnp.log(1.0 + jnp.exp(-jnp.abs(x)))


def _rms_norm(x, gain):
    ms = jnp.mean(x * x, axis=-1, keepdims=True)
    return x * lax.rsqrt(ms + EPS) * gain


def _split3(x):
    hi = x.astype(BF16)
    r1 = x - hi.astype(F32)
    mid = r1.astype(BF16)
    lo = (r1 - mid.astype(F32)).astype(BF16)
    return hi, mid, lo


def _resident(shape):
    nd = len(shape)
    return pl.BlockSpec(shape, lambda *_: (0,) * nd, pipeline_mode=pl.Buffered(1))


def _adaln_kernel(c_ref, w_ref, b_ref, o_ref):
    c = c_ref[...]
    cond = c * _sigmoid(c)
    o_ref[...] = jnp.dot(cond.astype(BF16), w_ref[...].astype(BF16),
                         preferred_element_type=F32) + b_ref[...]


def _adaln(c, ada_w, ada_b):
    b, d = c.shape
    n = ada_w.shape[1]
    rows = 8
    c_pad = jnp.zeros((rows, d), F32).at[:b].set(c)
    tn = 1024
    out = pl.pallas_call(
        _adaln_kernel,
        out_shape=jax.ShapeDtypeStruct((rows, n), F32),
        grid=(n // tn,),
        in_specs=[pl.BlockSpec((rows, d), lambda j: (0, 0)),
                  pl.BlockSpec((d, tn), lambda j: (0, j)),
                  pl.BlockSpec((1, tn), lambda j: (0, j))],
        out_specs=pl.BlockSpec((rows, tn), lambda j: (0, j)),
        compiler_params=pltpu.CompilerParams(dimension_semantics=("arbitrary",),
                                             vmem_limit_bytes=VMEM_LIMIT),
        name="adaln",
    )(c_pad, ada_w, ada_b.reshape(1, n))
    return out[:b]


def _t5_bucket(dist):
    n = jnp.maximum(dist, 0)
    max_exact = N_BUCKETS // 2
    nf = jnp.maximum(n, 1).astype(F32)
    large = max_exact + (jnp.log(nf / max_exact) / math.log(MAX_DISTANCE / max_exact)
                         * (N_BUCKETS - max_exact)).astype(I32)
    large = jnp.minimum(large, N_BUCKETS - 1)
    return jnp.where(n < max_exact, n, large)


def _bias_kernel(rel_ref, tile_ref, far_ref, *, far_dist):
    r = lax.broadcasted_iota(I32, (BIAS_ROWS, LANES), 0)
    i = lax.broadcasted_iota(I32, (BIAS_ROWS, LANES), 1)
    bucket = _t5_bucket(i - r + BIAS_PAD)
    far_bucket = _t5_bucket(jnp.full((1, LANES), far_dist, I32))
    for h in range(N_HEADS_A):
        acc = jnp.zeros((BIAS_ROWS, LANES), F32)
        far = jnp.zeros((1, LANES), F32)
        for bkt in range(N_BUCKETS):
            val = rel_ref[bkt, h]
            acc = jnp.where(bucket == bkt, val, acc)
            far = jnp.where(far_bucket == bkt, val, far)
        tile_ref[h] = acc
        far_ref[:, h * LANES:(h + 1) * LANES] = far


def _bias_tiles(rel_bias, seq):
    return pl.pallas_call(
        functools.partial(_bias_kernel, far_dist=seq - 1),
        out_shape=(jax.ShapeDtypeStruct((N_HEADS_A, BIAS_ROWS, LANES), F32),
                   jax.ShapeDtypeStruct((1, N_HEADS_A * LANES), F32)),
        in_specs=[pl.BlockSpec(memory_space=pltpu.SMEM)],
        out_specs=(pl.BlockSpec(memory_space=pltpu.VMEM), pl.BlockSpec(memory_space=pltpu.VMEM)),
        name="bias_tiles",
    )(rel_bias)


def _ffn_kernel(x_ref, gain_ref, sh_ref, sc_ref, g_ref, w1_ref, w3_ref, w2_ref, fin_ref, o_ref,
                h_scr, acc_scr, *, n_chunks, final_norm):
    x = x_ref[...]
    h = _rms_norm(x, gain_ref[...]) * (1.0 + sc_ref[0]) + sh_ref[0]
    h_scr[...] = h.astype(BF16)
    for j in range(n_chunks):
        hb = h_scr[...]
        u1 = jnp.dot(hb, w1_ref[j], preferred_element_type=F32)
        u3 = jnp.dot(hb, w3_ref[j], preferred_element_type=F32)
        a = (u1 * _sigmoid(u1)) * u3
        part = jnp.dot(a.astype(BF16), w2_ref[j], preferred_element_type=F32)
        if j == 0:
            acc_scr[...] = part
        else:
            acc_scr[...] += part
    out = x + (0.5 * g_ref[0]) * acc_scr[...]
    if final_norm:
        out = _rms_norm(out, fin_ref[...])
    o_ref[...] = out


def _ffn(x, gain, sh, sc, g, w1, w3, w2, fin, *, seq, final_norm):
    t, d = x.shape
    dff = w1.shape[1]
    nch = dff // FFN_CHUNK
    w1c = w1.astype(BF16).reshape(d, nch, FFN_CHUNK).transpose(1, 0, 2)
    w3c = w3.astype(BF16).reshape(d, nch, FFN_CHUNK).transpose(1, 0, 2)
    w2c = w2.astype(BF16).reshape(nch, FFN_CHUNK, d)
    tm = FFN_TM
    per_b = seq // tm
    mod_spec = pl.BlockSpec((1, 1, d), lambda i: (i // per_b, 0, 0))
    return pl.pallas_call(
        functools.partial(_ffn_kernel, n_chunks=nch, final_norm=final_norm),
        out_shape=jax.ShapeDtypeStruct((t, d), F32),
        grid=(t // tm,),
        in_specs=[pl.BlockSpec((tm, d), lambda i: (i, 0)),
                  _resident((1, d)), mod_spec, mod_spec, mod_spec,
                  _resident((nch, d, FFN_CHUNK)), _resident((nch, d, FFN_CHUNK)),
                  _resident((nch, FFN_CHUNK, d)), _resident((1, d))],
        out_specs=pl.BlockSpec((tm, d), lambda i: (i, 0)),
        scratch_shapes=[pltpu.VMEM((tm, d), BF16), pltpu.VMEM((tm, d), F32)],
        compiler_params=pltpu.CompilerParams(dimension_semantics=("arbitrary",),
                                             vmem_limit_bytes=VMEM_LIMIT),
        name="ffn_final" if final_norm else "ffn",
    )(x, gain.reshape(1, d), sh, sc, g, w1c, w3c, w2c, fin.reshape(1, d))


_C_QA = 0
_C_CKV = _C_QA + N_HEADS_A * LANES
_C_QI = _C_CKV + D_LATENT
_C_KI = _C_QI + N_HEADS_IDX * LANES
_C_SM = _C_KI + LANES
_C_QK = _C_SM + LANES
_C_V = _C_QK + 2 * W_M
_C_O = _C_V + W_M
_C_GA = _C_O + W_M
_C_GM = _C_GA + 1024
_C_END = _C_GM + 1024
_SM_W = 0
_SM_I = N_HEADS_IDX
_SM_F = N_HEADS_IDX + N_HEADS_M


def _pack_w_in(w_in, d_model):
    splits = (W_A, D_LATENT, N_HEADS_IDX * HEAD_DIM_IDX, HEAD_DIM_IDX, N_HEADS_IDX,
              W_M, W_M, W_M, N_HEADS_M, N_HEADS_M, W_M, d_model, d_model)
    offs = [0]
    for s in splits:
        offs.append(offs[-1] + s)
    (q_a, c_kv, q_i, k_i, w_i, q_m, k_m, v_m, i_p, f_p, o_p, g_a, g_m) = [
        w_in[:, offs[n]:offs[n + 1]] for n in range(len(splits))]
    d = w_in.shape[0]

    def pad_heads(w, nh, hd):
        w = w.reshape(d, nh, hd)
        return jnp.pad(w, ((0, 0), (0, 0), (0, LANES - hd))).reshape(d, nh * LANES)

    small = jnp.concatenate([w_i, i_p, f_p], axis=1)
    small = jnp.pad(small, ((0, 0), (0, LANES - small.shape[1])))
    packed = jnp.concatenate([
        pad_heads(q_a, N_HEADS_A, HEAD_DIM_A), c_kv, pad_heads(q_i, N_HEADS_IDX, HEAD_DIM_IDX),
        jnp.pad(k_i, ((0, 0), (0, LANES - HEAD_DIM_IDX))), small, q_m, k_m, v_m, o_p, g_a, g_m], axis=1)
    assert packed.shape[1] == _C_END
    return packed.astype(BF16)


def _mixin_kernel(x_ref, gain_ref, sh_ref, sc_ref, w_ref, kvn_ref, wuk_ref,
                  qabs_ref, qidx_ref, kidx_ref, ckv_ref, ckvt_ref, wt_ref, ifc_ref, ift_ref,
                  qk_ref, v_ref, o_ref, ga_ref, gm_ref, h_scr, *, tm):
    nqb = tm // Q_BLOCK
    x = x_ref[...]
    h = _rms_norm(x, gain_ref[...]) * (1.0 + sc_ref[0]) + sh_ref[0]
    h_scr[...] = h.astype(BF16)

    def proj(lo, hi):
        return jnp.dot(h_scr[...], w_ref[:, lo:hi], preferred_element_type=F32)

    qa = proj(_C_QA, _C_CKV)
    scale = HEAD_DIM_A ** -0.5
    for hh in range(N_HEADS_A):
        q_h = qa[:, hh * LANES:(hh + 1) * LANES].astype(BF16)
        q_abs = jnp.dot(q_h, wuk_ref[hh], preferred_element_type=F32) * scale
        qabs_ref[:, hh] = q_abs.astype(BF16).reshape(nqb, Q_BLOCK, D_LATENT)
    ckv = _rms_norm(proj(_C_CKV, _C_QI), kvn_ref[...])
    ckv_ref[...] = ckv.astype(BF16)
    ckv_t = ckv.T
    for j in range(tm // KEY_BLOCK):
        ckvt_ref[j] = ckv_t[:, j * KEY_BLOCK:(j + 1) * KEY_BLOCK].astype(BF16)
    qi = proj(_C_QI, _C_KI)
    for hh in range(N_HEADS_IDX):
        qidx_ref[:, hh] = qi[:, hh * LANES:(hh + 1) * LANES].astype(BF16).reshape(nqb, Q_BLOCK, LANES)
    kidx_ref[...] = proj(_C_KI, _C_SM).astype(BF16)
    small = proj(_C_SM, _C_QK)
    ifc_ref[...] = small
    small_t = small.T
    wt_ref[...] = small_t[_SM_W:_SM_W + N_HEADS_IDX] * IDX_SCALE
    ift_ref[...] = small_t[_SM_I:_SM_I + 2 * N_HEADS_M]
    qk_ref[...] = proj(_C_QK, _C_V)
    v_ref[...] = proj(_C_V, _C_O).astype(BF16)
    o_ref[...] = proj(_C_O, _C_GA)
    ga_ref[...] = proj(_C_GA, _C_GM)
    gm_ref[...] = proj(_C_GM, _C_END)


def _mixin(x, gain, sh, sc, w_packed, kv_norm, wuk_t, *, seq):
    t, d = x.shape
    tm = MIX_TM
    per_b = seq // tm
    nqb = tm // Q_BLOCK
    row = lambda w: pl.BlockSpec((tm, w), lambda i: (i, 0))
    mod_spec = pl.BlockSpec((1, 1, d), lambda i: (i // per_b, 0, 0))
    out_shape = (
        jax.ShapeDtypeStruct((t // Q_BLOCK, N_HEADS_A, Q_BLOCK, D_LATENT), BF16),
        jax.ShapeDtypeStruct((t // Q_BLOCK, N_HEADS_IDX, Q_BLOCK, LANES), BF16),
        jax.ShapeDtypeStruct((t, LANES), BF16),
        jax.ShapeDtypeStruct((t, D_LATENT), BF16),
        jax.ShapeDtypeStruct((t // KEY_BLOCK, D_LATENT, KEY_BLOCK), BF16),
        jax.ShapeDtypeStruct((N_HEADS_IDX, t), F32),
        jax.ShapeDtypeStruct((t, LANES), F32),
        jax.ShapeDtypeStruct((2 * N_HEADS_M, t), F32),
        jax.ShapeDtypeStruct((t, 2 * W_M), F32),
        jax.ShapeDtypeStruct((t, W_M), BF16),
        jax.ShapeDtypeStruct((t, W_M), F32),
        jax.ShapeDtypeStruct((t, d), F32),
        jax.ShapeDtypeStruct((t, d), F32),
    )
    out_specs = (
        pl.BlockSpec((nqb, N_HEADS_A, Q_BLOCK, D_LATENT), lambda i: (i, 0, 0, 0)),
        pl.BlockSpec((nqb, N_HEADS_IDX, Q_BLOCK, LANES), lambda i: (i, 0, 0, 0)),
        row(LANES), row(D_LATENT),
        pl.BlockSpec((tm // KEY_BLOCK, D_LATENT, KEY_BLOCK), lambda i: (i, 0, 0)),
        pl.BlockSpec((N_HEADS_IDX, tm), lambda i: (0, i)),
        row(LANES),
        pl.BlockSpec((2 * N_HEADS_M, tm), lambda i: (0, i)),
        row(2 * W_M), row(W_M), row(W_M), row(d), row(d),
    )
    return pl.pallas_call(
        functools.partial(_mixin_kernel, tm=tm),
        out_shape=out_shape,
        grid=(t // tm,),
        in_specs=[pl.BlockSpec((tm, d), lambda i: (i, 0)), _resident((1, d)), mod_spec, mod_spec,
                  _resident((d, _C_END)), _resident((1, D_LATENT)),
                  _resident((N_HEADS_A, LANES, D_LATENT))],
        out_specs=out_specs,
        scratch_shapes=[pltpu.VMEM((tm, d), BF16)],
        compiler_params=pltpu.CompilerParams(dimension_semantics=("arbitrary",),
                                             vmem_limit_bytes=VMEM_LIMIT),
        name="mixin",
    )(x, gain.reshape(1, d), sh, sc, w_packed, kv_norm.reshape(1, D_LATENT), wuk_t)


def _sortable_key(score):
    bits = pltpu.bitcast(score, I32)
    bits = jnp.where(bits == INT_MIN, 0, bits)
    return jnp.where(bits < 0, bits ^ 0x7FFFFFFF, bits)


def _bit_transpose32(words):
    v = list(words)
    j, m = 16, 0x0000FFFF
    while j:
        k = 0
        while k < 32:
            t = (v[k] ^ lax.shift_right_logical(v[k + j], jnp.int32(j))) & m
            v[k] = v[k] ^ t
            v[k + j] = v[k + j] ^ (t << j)
            k = (k + j + 1) & ~j
        j >>= 1
        m = (m ^ (m << j)) & 0x7FFFFFFF
    return v


def _dsa_kernel(qidx_ref, qabs_ref, wt_ref, kidx_ref, ckv_ref, ckvt_ref, btile_ref, bfar_ref, wuvt_ref,
                out_ref, keys_scr, planes_scr, cand_scr, acc_scr, m_scr, l_scr, *, topk):
    kb_sz = KEY_BLOCK
    qb = pl.program_id(1)
    n_kb = qb // (kb_sz // Q_BLOCK) + 1
    q0 = qb * Q_BLOCK
    row_id = lax.broadcasted_iota(I32, (kb_sz, LANES), 0)
    lane_id = lax.broadcasted_iota(I32, (kb_sz, LANES), 1)

    q_idx = qidx_ref[0].reshape(N_HEADS_IDX * Q_BLOCK, LANES)
    w_t = wt_ref[...]

    def score_body(kb, carry):
        k_blk = kidx_ref[kb]
        s_t = lax.dot_general(k_blk, q_idx, (((1,), (1,)), ((), ())),
                              preferred_element_type=F32)
        score = jnp.zeros((kb_sz, LANES), F32)
        for hh in range(N_HEADS_IDX):
            score = score + jnp.maximum(s_t[:, hh * LANES:(hh + 1) * LANES], 0.0) * w_t[hh:hh + 1, :]
        valid = (kb * kb_sz + row_id) <= (q0 + lane_id)
        keys_scr[kb] = jnp.where(valid, _sortable_key(score), INT_MIN)
        return carry

    lax.fori_loop(0, n_kb, score_body, 0)

    n_blocks = keys_scr.shape[0]

    def planes_body(kb, carry):
        v = keys_scr[kb] ^ INT_MIN
        words = _bit_transpose32([v[8 * i:8 * (i + 1), :] for i in range(32)])
        for bit in range(32):
            planes_scr[bit, kb] = words[31 - bit]
        return carry

    lax.fori_loop(0, n_kb, planes_body, 0)

    def clear_body(kb, carry):
        for bit in range(32):
            planes_scr[bit, kb] = jnp.zeros((8, LANES), I32)
        return carry

    lax.fori_loop(n_kb, n_blocks, clear_body, 0)
    blk_id = lax.broadcasted_iota(I32, cand_scr.shape, 0)
    cand_scr[...] = jnp.where(blk_id < n_kb, -1, 0)

    def bit_body(it, carry):
        above, tau_u = carry
        bit = 31 - it
        ones = cand_scr[...] & planes_scr[bit]
        c1 = jnp.sum(jnp.sum(lax.population_count(ones), axis=0), axis=0, keepdims=True)
        take = (above + c1) >= topk
        cand_scr[...] = jnp.where(take, ones, cand_scr[...] ^ ones)
        above = jnp.where(take, above, above + c1)
        tau_u = jnp.where(take, tau_u | (jnp.int32(1) << bit), tau_u)
        return above, tau_u

    zero = jnp.zeros((1, LANES), I32)
    n_gt, tau_u = lax.fori_loop(0, 32, bit_body, (zero, zero))
    tau = tau_u ^ INT_MIN
    n_eq = jnp.sum(jnp.sum(lax.population_count(cand_scr[...]), axis=0), axis=0, keepdims=True)

    def count(pred_fn):
        def body(kb, acc):
            hit = jnp.where(pred_fn(keys_scr[kb], kb), 1, 0)
            return acc + jnp.sum(hit.reshape(kb_sz // 8, 8, LANES), axis=0)
        acc = lax.fori_loop(0, n_kb, body, jnp.zeros((8, LANES), I32))
        return jnp.sum(acc, axis=0, keepdims=True)

    need = topk - n_gt
    overflow = n_eq > need
    seq_bits = max(1, (keys_scr.shape[0] * kb_sz - 1).bit_length())

    def idx_body(it, jc):
        trial = jc | (jnp.int32(1) << (seq_bits - 1 - it))
        c = count(lambda k, kb: (k == tau) & ((kb * kb_sz + row_id) < trial))
        return jnp.where(c < need, trial, jc)

    any_overflow = jnp.max(jnp.where(overflow, 1, 0)) > 0
    j_cut = lax.cond(any_overflow,
                     lambda: lax.fori_loop(0, seq_bits, idx_body, jnp.zeros((1, LANES), I32)),
                     lambda: jnp.zeros((1, LANES), I32))
    j_lim = jnp.where(overflow, j_cut, jnp.int32(2 ** 30))

    q_abs = qabs_ref[0].reshape(N_HEADS_A * Q_BLOCK, D_LATENT)
    m_scr[...] = jnp.full(m_scr.shape, NEG_BIG, F32)
    l_scr[...] = jnp.zeros(l_scr.shape, F32)
    acc_scr[...] = jnp.zeros(acc_scr.shape, F32)

    def attend(kb, near):
        keys = keys_scr[kb]
        s_idx = kb * kb_sz + row_id
        sel = (keys > tau) | ((keys == tau) & (s_idx <= j_lim))
        sel = sel & (s_idx <= (q0 + lane_id))
        lt = lax.dot_general(ckv_ref[kb], q_abs, (((1,), (1,)), ((), ())),
                             preferred_element_type=F32)
        if near:
            start = pl.multiple_of(BIAS_PAD - (q0 - kb * kb_sz), LANES)
        parts = []
        for hh in range(N_HEADS_A):
            sl = slice(hh * LANES, (hh + 1) * LANES)
            if near:
                bias = btile_ref[hh, pl.ds(start, kb_sz), :]
            else:
                bias = bfar_ref[:, sl]
            parts.append(jnp.where(sel, lt[:, sl] + bias, NEG_BIG))
        lt = jnp.concatenate(parts, axis=1)
        m_old = m_scr[...]
        m_new = jnp.maximum(m_old, jnp.max(lt, axis=0, keepdims=True))
        alpha = jnp.exp(m_old - m_new)
        p = jnp.exp(lt - m_new)
        l_scr[...] = alpha * l_scr[...] + jnp.sum(p, axis=0, keepdims=True)
        m_scr[...] = m_new
        pv = jnp.dot(ckvt_ref[kb], p.astype(BF16), preferred_element_type=F32)
        acc_scr[...] = alpha * acc_scr[...] + pv

    def far_body(kb, carry):
        attend(kb, False)
        return carry

    lax.fori_loop(0, jnp.maximum(n_kb - 2, 0), far_body, 0)

    @pl.when(n_kb >= 2)
    def _():
        attend(n_kb - 2, True)

    attend(n_kb - 1, True)

    o_lat = acc_scr[...] * (1.0 / l_scr[...])
    ys = []
    for hh in range(N_HEADS_A):
        o_h = o_lat[:, hh * LANES:(hh + 1) * LANES].astype(BF16)
        ys.append(jnp.dot(wuvt_ref[hh], o_h, preferred_element_type=F32))
    y_t = jnp.concatenate(ys, axis=0)
    out_ref[...] = y_t.T.astype(BF16)


def _dsa(q_idx, q_abs, w_t, k_idx, ckv, ckv_t, btile, bfar, wuv_t, *, batch, seq):
    t = batch * seq
    nqb = seq // Q_BLOCK
    nkb = seq // KEY_BLOCK
    topk = min(TOPK_MAX, seq // 4)
    k_idx3 = k_idx.reshape(t // KEY_BLOCK, KEY_BLOCK, LANES)
    ckv3 = ckv.reshape(t // KEY_BLOCK, KEY_BLOCK, D_LATENT)
    per_batch = lambda shape: pl.BlockSpec(shape, lambda b, q: (b,) + (0,) * (len(shape) - 1),
                                           pipeline_mode=pl.Buffered(1))
    return pl.pallas_call(
        functools.partial(_dsa_kernel, topk=topk),
        out_shape=jax.ShapeDtypeStruct((t, W_A), BF16),
        grid=(batch, nqb),
        in_specs=[pl.BlockSpec((1, N_HEADS_IDX, Q_BLOCK, LANES), lambda b, q: (b * nqb + q, 0, 0, 0)),
                  pl.BlockSpec((1, N_HEADS_A, Q_BLOCK, D_LATENT), lambda b, q: (b * nqb + q, 0, 0, 0)),
                  pl.BlockSpec((N_HEADS_IDX, Q_BLOCK), lambda b, q: (0, b * nqb + q)),
                  per_batch((nkb, KEY_BLOCK, LANES)),
                  per_batch((nkb, KEY_BLOCK, D_LATENT)),
                  per_batch((nkb, D_LATENT, KEY_BLOCK)),
                  _resident((N_HEADS_A, BIAS_ROWS, LANES)),
                  _resident((1, N_HEADS_A * LANES)),
                  _resident((N_HEADS_A, HEAD_DIM_A, D_LATENT))],
        out_specs=pl.BlockSpec((Q_BLOCK, W_A), lambda b, q: (b * nqb + q, 0)),
        scratch_shapes=[pltpu.VMEM((nkb, KEY_BLOCK, LANES), I32),
                        pltpu.VMEM((32, nkb, 8, LANES), I32),
                        pltpu.VMEM((nkb, 8, LANES), I32),
                        pltpu.VMEM((D_LATENT, N_HEADS_A * LANES), F32),
                        pltpu.VMEM((1, N_HEADS_A * LANES), F32),
                        pltpu.VMEM((1, N_HEADS_A * LANES), F32)],
        compiler_params=pltpu.CompilerParams(dimension_semantics=("arbitrary", "arbitrary"),
                                             vmem_limit_bytes=VMEM_LIMIT),
        name="dsa",
    )(q_idx, q_abs, w_t, k_idx3, ckv3, ckv_t, btile, bfar, wuv_t)


def _mlstm_kernel(qk_ref, v_ref, o_ref, ifc_ref, ift_ref, cw_ref, cb_ref, gbc_ref, gbt_ref, hn_ref,
                  out_ref, c_scr, n_scr, m_scr, tail_scr, *, chunk):
    L = chunk
    step = pl.program_id(1)

    @pl.when(step == 0)
    def _():
        c_scr[...] = jnp.zeros(c_scr.shape, F32)
        n_scr[...] = jnp.zeros(n_scr.shape, F32)
        m_scr[...] = jnp.zeros(m_scr.shape, F32)
        tail_scr[...] = jnp.zeros(tail_scr.shape, F32)

    x = qk_ref[...]
    xe = jnp.concatenate([tail_scr[...], x], axis=0)
    conv = x * cw_ref[CONV_WIDTH - 1:CONV_WIDTH, :] + cb_ref[...]
    for d in range(1, CONV_WIDTH):
        shifted = pltpu.roll(xe, d, axis=0)[8:]
        conv = conv + shifted * cw_ref[CONV_WIDTH - 1 - d:CONV_WIDTH - d, :]
    tail_scr[...] = x[L - 8:]
    qk = conv * _sigmoid(conv)

    a_c = ifc_ref[...] + gbc_ref[...]
    ls_c = _log_sigmoid(a_c)
    a_t = ift_ref[...] + gbt_ref[...]
    ls_t = _log_sigmoid(a_t)
    rr = lax.broadcasted_iota(I32, (L, L), 0)
    cc = lax.broadcasted_iota(I32, (L, L), 1)
    causal = cc <= rr
    tril = jnp.where(causal, 1.0, 0.0).astype(BF16)
    triu = jnp.where(rr <= cc, 1.0, 0.0).astype(BF16)
    b_c = sum(jnp.dot(tril, piece, preferred_element_type=F32) for piece in _split3(ls_c))
    b_t = sum(jnp.dot(piece, triu, preferred_element_type=F32) for piece in _split3(ls_t))

    o_gate = _sigmoid(o_ref[...])
    for hh in range(N_HEADS_M):
        hs = slice(hh * HEAD_DIM_M, (hh + 1) * HEAD_DIM_M)
        q = qk[:, hs]
        k = qk[:, W_M + hh * HEAD_DIM_M:W_M + (hh + 1) * HEAD_DIM_M] * (HEAD_DIM_M ** -0.5)
        v = v_ref[:, hs]
        ci, cf = _SM_I + hh, _SM_F + hh
        i_c = a_c[:, ci:ci + 1]
        bc = b_c[:, cf:cf + 1]
        i_t = a_t[hh:hh + 1, :]
        bt = b_t[N_HEADS_M + hh:N_HEADS_M + hh + 1, :]
        b_last = bc[L - 1:L, :]
        m_prev = m_scr[hh]
        c_prev = c_scr[hh]
        n_prev = n_scr[hh]

        d_log = jnp.where(causal, bc - bt + i_t, NEG_BIG)
        inter_log = bc + m_prev
        m_j = jnp.maximum(inter_log, jnp.max(d_log, axis=1, keepdims=True))
        qb16 = q.astype(BF16)
        s = lax.dot_general(qb16, k.astype(BF16), (((1,), (1,)), ((), ())),
                            preferred_element_type=F32) * jnp.exp(d_log - m_j)
        w_inter = jnp.exp(inter_log - m_j)
        num = (w_inter * jnp.dot(qb16, c_prev.astype(BF16), preferred_element_type=F32)
               + jnp.dot(s.astype(BF16), v, preferred_element_type=F32))
        den = w_inter * jnp.sum(q * n_prev, axis=1, keepdims=True) + jnp.sum(s, axis=1, keepdims=True)
        hval = num / jnp.maximum(jnp.abs(den), jnp.exp(-m_j))

        g_t = b_last - bt + i_t
        g_c = b_last - bc + i_c
        m_new = jnp.maximum(b_last + m_prev, jnp.max(g_t, axis=1, keepdims=True))
        decay = jnp.exp(b_last + m_prev - m_new)
        kw = k * jnp.exp(g_c - m_new)
        c_scr[hh] = decay * c_prev + jnp.dot(kw.T.astype(BF16), v, preferred_element_type=F32)
        n_scr[hh] = decay * n_prev + jnp.sum(kw, axis=0, keepdims=True)
        m_scr[hh] = m_new

        mu = jnp.mean(hval, axis=1, keepdims=True)
        cen = hval - mu
        var = jnp.mean(cen * cen, axis=1, keepdims=True)
        hn = cen * lax.rsqrt(var + EPS) * hn_ref[:, hs]
        out_ref[:, hs] = (hn * o_gate[:, hs]).astype(BF16)


def _mlstm(qk_raw, v, o_pre, ifc, ift, conv_w, conv_b, gate_bias, head_norm, *, batch, seq):
    t = batch * seq
    L = MLSTM_CHUNK
    nc = seq // L
    gbc = jnp.zeros((1, LANES), F32).at[0, _SM_I:_SM_I + 2 * N_HEADS_M].set(gate_bias)
    gbt = jnp.broadcast_to(gate_bias.reshape(2 * N_HEADS_M, 1), (2 * N_HEADS_M, L))
    row = lambda w: pl.BlockSpec((L, w), lambda b, c: (b * nc + c, 0))
    return pl.pallas_call(
        functools.partial(_mlstm_kernel, chunk=L),
        out_shape=jax.ShapeDtypeStruct((t, W_M), BF16),
        grid=(batch, nc),
        in_specs=[row(2 * W_M), row(W_M), row(W_M), row(LANES),
                  pl.BlockSpec((2 * N_HEADS_M, L), lambda b, c: (0, b * nc + c)),
                  _resident((CONV_WIDTH, 2 * W_M)), _resident((1, 2 * W_M)),
                  _resident((1, LANES)), _resident((2 * N_HEADS_M, L)), _resident((1, W_M))],
        out_specs=row(W_M),
        scratch_shapes=[pltpu.VMEM((N_HEADS_M, HEAD_DIM_M, HEAD_DIM_M), F32),
                        pltpu.VMEM((N_HEADS_M, 1, HEAD_DIM_M), F32),
                        pltpu.VMEM((N_HEADS_M, 1, 1), F32),
                        pltpu.VMEM((8, 2 * W_M), F32)],
        compiler_params=pltpu.CompilerParams(dimension_semantics=("arbitrary", "arbitrary"),
                                             vmem_limit_bytes=VMEM_LIMIT),
        name="mlstm",
    )(qk_raw, v, o_pre, ifc, ift, conv_w, conv_b.reshape(1, -1), gbc, gbt, head_norm.reshape(1, -1))


def _merge_kernel(x_ref, ya_ref, hm_ref, ga_ref, gm_ref, g_ref, wa_ref, wm_ref, wo_ref, o_ref):
    pa = jnp.dot(ya_ref[...], wa_ref[...], preferred_element_type=F32)
    pm = jnp.dot(hm_ref[...], wm_ref[...], preferred_element_type=F32)
    merged = _sigmoid(ga_ref[...]) * pa + _sigmoid(gm_ref[...]) * pm
    out = jnp.dot(merged.astype(BF16), wo_ref[...], preferred_element_type=F32)
    o_ref[...] = x_ref[...] + g_ref[0] * out


def _merge(x, y_a, h_m, gate_a, gate_m, g, w_a, w_m, w_o, *, seq):
    t, d = x.shape
    tm = MERGE_TM
    per_b = seq // tm
    row = lambda w: pl.BlockSpec((tm, w), lambda i: (i, 0))
    return pl.pallas_call(
        _merge_kernel,
        out_shape=jax.ShapeDtypeStruct((t, d), F32),
        grid=(t // tm,),
        in_specs=[row(d), row(W_A), row(W_M), row(d), row(d),
                  pl.BlockSpec((1, 1, d), lambda i: (i // per_b, 0, 0)),
                  _resident((W_A, d)), _resident((W_M, d)), _resident((d, d))],
        out_specs=row(d),
        compiler_params=pltpu.CompilerParams(dimension_semantics=("arbitrary",),
                                             vmem_limit_bytes=VMEM_LIMIT),
        name="merge",
    )(x, y_a, h_m, gate_a, gate_m, g, w_a.astype(BF16), w_m.astype(BF16), w_o.astype(BF16))


def kernel(x, c, ada_w, ada_b, ffn1_norm, ffn1_w1, ffn1_w3, ffn1_w2, mix_norm, w_in, conv_w, conv_b,
           kv_norm, w_uk, w_uv, mlstm_gate_bias, mlstm_head_norm, rel_bias, w_branch_attn,
           w_branch_mlstm, w_out, ffn2_norm, ffn2_w1, ffn2_w3, ffn2_w2, final_norm):
    batch, seq, d = x.shape
    depth = ada_w.shape[0]
    assert seq % max(FFN_TM, MIX_TM, MERGE_TM, MLSTM_CHUNK, KEY_BLOCK) == 0
    t = batch * seq
    xf = x.reshape(t, d)
    btile, bfar = _bias_tiles(rel_bias, seq)
    for l in range(depth):
        mod = _adaln(c, ada_w[l], ada_b[l]).reshape(batch, 9, 1, d)
        sh1, sc1, g1, sh2, sc2, g2, sh3, sc3, g3 = [mod[:, n] for n in range(9)]
        xf = _ffn(xf, ffn1_norm[l], sh1, sc1, g1, ffn1_w1[l], ffn1_w3[l], ffn1_w2[l], final_norm,
                  seq=seq, final_norm=False)
        wuk_t = jnp.pad(w_uk[l].transpose(0, 2, 1), ((0, 0), (0, LANES - HEAD_DIM_A), (0, 0))).astype(BF16)
        (q_abs, q_idx, k_idx, ckv, ckv_t, w_t, ifc, ift, qk_raw, v_m, o_pre, gate_a, gate_m) = _mixin(
            xf, mix_norm[l], sh2, sc2, _pack_w_in(w_in[l], d), kv_norm[l], wuk_t, seq=seq)
        wuv_t = w_uv[l].transpose(0, 2, 1).astype(BF16)
        y_a = _dsa(q_idx, q_abs, w_t, k_idx, ckv, ckv_t, btile, bfar, wuv_t, batch=batch, seq=seq)
        h_m = _mlstm(qk_raw, v_m, o_pre, ifc, ift, conv_w[l], conv_b[l], mlstm_gate_bias[l],
                     mlstm_head_norm[l], batch=batch, seq=seq)
        xf = _merge(xf, y_a, h_m, gate_a, gate_m, g2, w_branch_attn[l], w_branch_mlstm[l], w_out[l],
                    seq=seq)
        xf = _ffn(xf, ffn2_norm[l], sh3, sc3, g3, ffn2_w1[l], ffn2_w3[l], ffn2_w2[l], final_norm,
                  seq=seq, final_norm=(l == depth - 1))
    return xf.reshape(batch, seq, d)
```

```python
import functools
import math

import jax
import jax.numpy as jnp
from jax import lax
from jax.experimental import pallas as pl
from jax.experimental.pallas import tpu as pltpu

F32 = jnp.float32
BF16 = jnp.bfloat16
I32 = jnp.int32

LANES = 128
MXU_DIM = 256
VMEM_LIMIT = 56 * 1024 * 1024

N_HEADS_A = 8
HEAD_DIM_A = 64
D_LATENT = 256
N_HEADS_IDX = 8
HEAD_DIM_IDX = 64
TOPK_MAX = 256
Q_BLOCK = 128
N_BUCKETS = 32
MAX_DISTANCE = 128
N_HEADS_M = 4
HEAD_DIM_M = 128
CONV_WIDTH = 4
EPS = 1e-6
IDX_SCALE = (N_HEADS_IDX ** -0.5) * (HEAD_DIM_IDX ** -0.5)
W_A = N_HEADS_A * HEAD_DIM_A
W_M = N_HEADS_M * HEAD_DIM_M

FFN_TM = 512
FFN_CHUNK = 256
MIX_TM = 512
KEY_BLOCK = 256
ATTN_UNROLL = 2
MLSTM_CHUNK = 256
MERGE_TM = 512
NEG_BIG = -1e30
INT_MIN = -2 ** 31

BIAS_PAD = 384
BIAS_ROWS = KEY_BLOCK + BIAS_PAD
CKVT_ROWS = D_LATENT + 16
LOG2E = math.log2(math.e)


def _sigmoid(x):
    return 1.0 / (1.0 + jnp.exp(-x))


def _log_sigmoid(x):
    return jnp.minimum(x, 0.0) - jnp.log(1.0 + jnp.exp(-jnp.abs(x)))


def _rms_norm(x, gain):
    ms = jnp.mean(x * x, axis=-1, keepdims=True)
    return x * lax.rsqrt(ms + EPS) * gain


def _split3(x):
    hi = x.astype(BF16)
    r1 = x - hi.astype(F32)
    mid = r1.astype(BF16)
    lo = (r1 - mid.astype(F32)).astype(BF16)
    return hi, mid, lo


def _resident(shape):
    nd = len(shape)
    return pl.BlockSpec(shape, lambda *_: (0,) * nd, pipeline_mode=pl.Buffered(1))


def _adaln_kernel(c_ref, w_ref, b_ref, o_ref):
    c = c_ref[...]
    cond = c * _sigmoid(c)
    o_ref[...] = jnp.dot(cond.astype(BF16), w_ref[...].astype(BF16),
                         preferred_element_type=F32) + b_ref[...]


def _adaln(c, ada_w, ada_b):
    b, d = c.shape
    n = ada_w.shape[1]
    rows = 8
    c_pad = jnp.zeros((rows, d), F32).at[:b].set(c)
    tn = 1024
    out = pl.pallas_call(
        _adaln_kernel,
        out_shape=jax.ShapeDtypeStruct((rows, n), F32),
        grid=(n // tn,),
        in_specs=[pl.BlockSpec((rows, d), lambda j: (0, 0)),
                  pl.BlockSpec((d, tn), lambda j: (0, j)),
                  pl.BlockSpec((1, tn), lambda j: (0, j))],
        out_specs=pl.BlockSpec((rows, tn), lambda j: (0, j)),
        compiler_params=pltpu.CompilerParams(dimension_semantics=("arbitrary",),
                                             vmem_limit_bytes=VMEM_LIMIT),
        name="adaln",
    )(c_pad, ada_w, ada_b.reshape(1, n))
    return out[:b]


def _t5_bucket(dist):
    n = jnp.maximum(dist, 0)
    max_exact = N_BUCKETS // 2
    nf = jnp.maximum(n, 1).astype(F32)
    large = max_exact + (jnp.log(nf / max_exact) / math.log(MAX_DISTANCE / max_exact)
                         * (N_BUCKETS - max_exact)).astype(I32)
    large = jnp.minimum(large, N_BUCKETS - 1)
    return jnp.where(n < max_exact, n, large)


def _bias_kernel(rel_ref, tile_ref, max_ref):
    r = lax.broadcasted_iota(I32, (BIAS_ROWS, LANES), 0)
    i = lax.broadcasted_iota(I32, (BIAS_ROWS, LANES), 1)
    bucket = _t5_bucket(i - r + BIAS_PAD)
    for h in range(N_HEADS_A):
        acc = jnp.zeros((BIAS_ROWS, LANES), F32)
        top = rel_ref[0, h] * LOG2E
        for bkt in range(N_BUCKETS):
            val = rel_ref[bkt, h] * LOG2E
            acc = jnp.where(bucket == bkt, val, acc)
            top = jnp.maximum(top, val)
        tile_ref[h] = acc
        max_ref[:, h * LANES:(h + 1) * LANES] = jnp.full((1, LANES), top, F32)


def _bias_tiles(rel_bias):
    return pl.pallas_call(
        _bias_kernel,
        out_shape=(jax.ShapeDtypeStruct((N_HEADS_A, BIAS_ROWS, LANES), F32),
                   jax.ShapeDtypeStruct((1, N_HEADS_A * LANES), F32)),
        in_specs=[pl.BlockSpec(memory_space=pltpu.SMEM)],
        out_specs=(pl.BlockSpec(memory_space=pltpu.VMEM), pl.BlockSpec(memory_space=pltpu.VMEM)),
        name="bias_tiles",
    )(rel_bias)


def _ffn_kernel(x_ref, gain_ref, sh_ref, sc_ref, g_ref, w1_ref, w3_ref, w2_ref, fin_ref, o_ref,
                h_scr, acc_scr, *, n_chunks, final_norm):
    x = x_ref[...]
    h = _rms_norm(x, gain_ref[...]) * (1.0 + sc_ref[0]) + sh_ref[0]
    h_scr[...] = h.astype(BF16)
    for j in range(n_chunks):
        hb = h_scr[...]
        u1 = jnp.dot(hb, w1_ref[j], preferred_element_type=F32)
        u3 = jnp.dot(hb, w3_ref[j], preferred_element_type=F32)
        a = (u1 * _sigmoid(u1)) * u3
        part = jnp.dot(a.astype(BF16), w2_ref[j], preferred_element_type=F32)
        if j == 0:
            acc_scr[...] = part
        else:
            acc_scr[...] += part
    out = x + (0.5 * g_ref[0]) * acc_scr[...]
    if final_norm:
        out = _rms_norm(out, fin_ref[...])
    o_ref[...] = out


def _ffn(x, gain, sh, sc, g, w1, w3, w2, fin, *, seq, final_norm):
    t, d = x.shape
    dff = w1.shape[1]
    nch = dff // FFN_CHUNK
    w1c = w1.astype(BF16).reshape(d, nch, FFN_CHUNK).transpose(1, 0, 2)
    w3c = w3.astype(BF16).reshape(d, nch, FFN_CHUNK).transpose(1, 0, 2)
    w2c = w2.astype(BF16).reshape(nch, FFN_CHUNK, d)
    tm = FFN_TM
    per_b = seq // tm
    mod_spec = pl.BlockSpec((1, 1, d), lambda i: (i // per_b, 0, 0))
    return pl.pallas_call(
        functools.partial(_ffn_kernel, n_chunks=nch, final_norm=final_norm),
        out_shape=jax.ShapeDtypeStruct((t, d), F32),
        grid=(t // tm,),
        in_specs=[pl.BlockSpec((tm, d), lambda i: (i, 0)),
                  _resident((1, d)), mod_spec, mod_spec, mod_spec,
                  _resident((nch, d, FFN_CHUNK)), _resident((nch, d, FFN_CHUNK)),
                  _resident((nch, FFN_CHUNK, d)), _resident((1, d))],
        out_specs=pl.BlockSpec((tm, d), lambda i: (i, 0)),
        scratch_shapes=[pltpu.VMEM((tm, d), BF16), pltpu.VMEM((tm, d), F32)],
        compiler_params=pltpu.CompilerParams(dimension_semantics=("arbitrary",),
                                             vmem_limit_bytes=VMEM_LIMIT),
        name="ffn_final" if final_norm else "ffn",
    )(x, gain.reshape(1, d), sh, sc, g, w1c, w3c, w2c, fin.reshape(1, d))


_C_QA = 0
_C_CKV = _C_QA + N_HEADS_A * LANES
_C_QI = _C_CKV + D_LATENT
_C_KI = _C_QI + N_HEADS_IDX * LANES
_C_SM = _C_KI + LANES
_C_QK = _C_SM + LANES
_C_V = _C_QK + 2 * W_M
_C_O = _C_V + W_M
_C_GA = _C_O + W_M
_C_GM = _C_GA + 1024
_C_END = _C_GM + 1024
_SM_W = 0
_SM_I = N_HEADS_IDX
_SM_F = N_HEADS_IDX + N_HEADS_M


def _pack_w_in(w_in, d_model):
    splits = (W_A, D_LATENT, N_HEADS_IDX * HEAD_DIM_IDX, HEAD_DIM_IDX, N_HEADS_IDX,
              W_M, W_M, W_M, N_HEADS_M, N_HEADS_M, W_M, d_model, d_model)
    offs = [0]
    for s in splits:
        offs.append(offs[-1] + s)
    (q_a, c_kv, q_i, k_i, w_i, q_m, k_m, v_m, i_p, f_p, o_p, g_a, g_m) = [
        w_in[:, offs[n]:offs[n + 1]] for n in range(len(splits))]
    d = w_in.shape[0]

    def pad_heads(w, nh, hd):
        w = w.reshape(d, nh, hd)
        return jnp.pad(w, ((0, 0), (0, 0), (0, LANES - hd))).reshape(d, nh * LANES)

    small = jnp.concatenate([w_i, i_p, f_p], axis=1)
    small = jnp.pad(small, ((0, 0), (0, LANES - small.shape[1])))
    packed = jnp.concatenate([
        pad_heads(q_a, N_HEADS_A, HEAD_DIM_A), c_kv, pad_heads(q_i, N_HEADS_IDX, HEAD_DIM_IDX),
        jnp.pad(k_i, ((0, 0), (0, LANES - HEAD_DIM_IDX))), small, q_m, k_m, v_m, o_p, g_a, g_m], axis=1)
    assert packed.shape[1] == _C_END
    return packed.astype(BF16)


def _mixin_kernel(x_ref, gain_ref, sh_ref, sc_ref, w_ref, kvn_ref, wuk_ref,
                  qabs_ref, qidx_ref, kidx_ref, ckv_ref, ckvt_ref, wt_ref, ifc_ref, ift_ref,
                  qk_ref, v_ref, o_ref, ga_ref, gm_ref, h_scr, *, tm):
    nqb = tm // Q_BLOCK
    x = x_ref[...]
    h = _rms_norm(x, gain_ref[...]) * (1.0 + sc_ref[0]) + sh_ref[0]
    h_scr[...] = h.astype(BF16)

    def proj(lo, hi):
        return jnp.dot(h_scr[...], w_ref[:, lo:hi], preferred_element_type=F32)

    qa = proj(_C_QA, _C_CKV)
    scale = HEAD_DIM_A ** -0.5 * LOG2E
    for hh in range(N_HEADS_A):
        q_h = qa[:, hh * LANES:(hh + 1) * LANES].astype(BF16)
        q_abs = jnp.dot(q_h, wuk_ref[hh], preferred_element_type=F32) * scale
        qabs_ref[:, hh] = q_abs.astype(BF16).reshape(nqb, Q_BLOCK, D_LATENT)
    ckv = _rms_norm(proj(_C_CKV, _C_QI), kvn_ref[...])
    ckv_ref[...] = ckv.astype(BF16)
    ckv_t = ckv.T
    ones_row = jnp.where(lax.broadcasted_iota(I32, (CKVT_ROWS - D_LATENT, KEY_BLOCK), 0) == 0, 1.0, 0.0)
    for j in range(tm // KEY_BLOCK):
        ckvt_ref[j, :D_LATENT] = ckv_t[:, j * KEY_BLOCK:(j + 1) * KEY_BLOCK].astype(BF16)
        ckvt_ref[j, D_LATENT:] = ones_row.astype(BF16)
    qi = proj(_C_QI, _C_KI)
    for hh in range(N_HEADS_IDX):
        qidx_ref[:, hh] = qi[:, hh * LANES:(hh + 1) * LANES].astype(BF16).reshape(nqb, Q_BLOCK, LANES)
    kidx_ref[...] = proj(_C_KI, _C_SM).astype(BF16)
    small = proj(_C_SM, _C_QK)
    ifc_ref[...] = small
    small_t = small.T
    wt_ref[...] = small_t[_SM_W:_SM_W + N_HEADS_IDX] * IDX_SCALE
    ift_ref[...] = small_t[_SM_I:_SM_I + 2 * N_HEADS_M]
    qk_ref[...] = proj(_C_QK, _C_V)
    v_ref[...] = proj(_C_V, _C_O).astype(BF16)
    o_ref[...] = proj(_C_O, _C_GA)
    ga_ref[...] = proj(_C_GA, _C_GM)
    gm_ref[...] = proj(_C_GM, _C_END)


def _mixin(x, gain, sh, sc, w_packed, kv_norm, wuk_t, *, seq):
    t, d = x.shape
    tm = MIX_TM
    per_b = seq // tm
    nqb = tm // Q_BLOCK
    row = lambda w: pl.BlockSpec((tm, w), lambda i: (i, 0))
    mod_spec = pl.BlockSpec((1, 1, d), lambda i: (i // per_b, 0, 0))
    out_shape = (
        jax.ShapeDtypeStruct((t // Q_BLOCK, N_HEADS_A, Q_BLOCK, D_LATENT), BF16),
        jax.ShapeDtypeStruct((t // Q_BLOCK, N_HEADS_IDX, Q_BLOCK, LANES), BF16),
        jax.ShapeDtypeStruct((t, LANES), BF16),
        jax.ShapeDtypeStruct((t, D_LATENT), BF16),
        jax.ShapeDtypeStruct((t // KEY_BLOCK, CKVT_ROWS, KEY_BLOCK), BF16),
        jax.ShapeDtypeStruct((N_HEADS_IDX, t), F32),
        jax.ShapeDtypeStruct((t, LANES), F32),
        jax.ShapeDtypeStruct((2 * N_HEADS_M, t), F32),
        jax.ShapeDtypeStruct((t, 2 * W_M), F32),
        jax.ShapeDtypeStruct((t, W_M), BF16),
        jax.ShapeDtypeStruct((t, W_M), F32),
        jax.ShapeDtypeStruct((t, d), F32),
        jax.ShapeDtypeStruct((t, d), F32),
    )
    out_specs = (
        pl.BlockSpec((nqb, N_HEADS_A, Q_BLOCK, D_LATENT), lambda i: (i, 0, 0, 0)),
        pl.BlockSpec((nqb, N_HEADS_IDX, Q_BLOCK, LANES), lambda i: (i, 0, 0, 0)),
        row(LANES), row(D_LATENT),
        pl.BlockSpec((tm // KEY_BLOCK, CKVT_ROWS, KEY_BLOCK), lambda i: (i, 0, 0)),
        pl.BlockSpec((N_HEADS_IDX, tm), lambda i: (0, i)),
        row(LANES),
        pl.BlockSpec((2 * N_HEADS_M, tm), lambda i: (0, i)),
        row(2 * W_M), row(W_M), row(W_M), row(d), row(d),
    )
    return pl.pallas_call(
        functools.partial(_mixin_kernel, tm=tm),
        out_shape=out_shape,
        grid=(t // tm,),
        in_specs=[pl.BlockSpec((tm, d), lambda i: (i, 0)), _resident((1, d)), mod_spec, mod_spec,
                  _resident((d, _C_END)), _resident((1, D_LATENT)),
                  _resident((N_HEADS_A, LANES, D_LATENT))],
        out_specs=out_specs,
        scratch_shapes=[pltpu.VMEM((tm, d), BF16)],
        compiler_params=pltpu.CompilerParams(dimension_semantics=("arbitrary",),
                                             vmem_limit_bytes=VMEM_LIMIT),
        name="mixin",
    )(x, gain.reshape(1, d), sh, sc, w_packed, kv_norm.reshape(1, D_LATENT), wuk_t)


def _sortable_key(score):
    bits = pltpu.bitcast(score, I32)
    bits = jnp.where(bits == INT_MIN, 0, bits)
    return jnp.where(bits < 0, bits ^ 0x7FFFFFFF, bits)


def _bit_transpose32(words):
    v = list(words)
    j, m = 16, 0x0000FFFF
    while j:
        k = 0
        while k < 32:
            t = (v[k] ^ lax.shift_right_logical(v[k + j], jnp.int32(j))) & m
            v[k] = v[k] ^ t
            v[k + j] = v[k + j] ^ (t << j)
            k = (k + j + 1) & ~j
        j >>= 1
        m = (m ^ (m << j)) & 0x7FFFFFFF
    return v


def _dsa_kernel(qidx_ref, qabs_ref, wt_ref, kidx_ref, ckv_ref, ckvt_ref, btile_ref, bmax_ref, wuvt_ref,
                out_ref, keys_scr, planes_scr, cand_scr, acc_scr, m_scr, lt_scr, kmax_scr, *, topk):
    kb_sz = KEY_BLOCK
    qb = pl.program_id(1)
    n_kb = qb // (kb_sz // Q_BLOCK) + 1
    q0 = qb * Q_BLOCK
    row_id = lax.broadcasted_iota(I32, (kb_sz, LANES), 0)
    lane_id = lax.broadcasted_iota(I32, (kb_sz, LANES), 1)

    w_t = wt_ref[...]
    n_blocks = keys_scr.shape[0]

    n_groups = N_HEADS_A // 2
    pair = 2 * LANES

    def idx_dot(kb, g):
        q_g = qidx_ref[0, 2 * g:2 * g + 2].reshape(2 * Q_BLOCK, LANES)
        return lax.dot_general(kidx_ref[kb], q_g, (((1,), (1,)), ((), ())), preferred_element_type=F32)

    for g in range(n_groups):
        lt_scr[:, g * pair:(g + 1) * pair] = idx_dot(0, g)

    def score_body(kb, carry):
        kb_next = jnp.minimum(kb + 1, n_kb - 1)
        score = jnp.zeros((kb_sz, LANES), F32)
        for g in range(n_groups):
            s_t = lt_scr[:, g * pair:(g + 1) * pair]
            for j in range(2):
                hh = 2 * g + j
                score = score + jnp.maximum(s_t[:, j * LANES:(j + 1) * LANES], 0.0) * w_t[hh:hh + 1, :]
            lt_scr[:, g * pair:(g + 1) * pair] = idx_dot(kb_next, g)
        valid = (kb * kb_sz + row_id) <= (q0 + lane_id)
        keys = jnp.where(valid, _sortable_key(score), INT_MIN)
        keys_scr[kb] = keys
        v = keys ^ INT_MIN
        words = _bit_transpose32([v[8 * i:8 * (i + 1), :] for i in range(32)])
        for bit in range(32):
            planes_scr[bit, kb] = words[31 - bit]
        return carry

    lax.fori_loop(0, n_kb, score_body, 0)


    def clear_body(kb, carry):
        for bit in range(32):
            planes_scr[bit, kb] = jnp.zeros((8, LANES), I32)
        return carry

    lax.fori_loop(n_kb, n_blocks, clear_body, 0)
    blk_id = lax.broadcasted_iota(I32, cand_scr.shape, 0)
    cand_scr[...] = jnp.where(blk_id < n_kb, -1, 0)

    def bit_body(it, carry):
        above, tau_u = carry
        bit = 31 - it
        ones = cand_scr[...] & planes_scr[bit]
        c1 = jnp.sum(jnp.sum(lax.population_count(ones), axis=0), axis=0, keepdims=True)
        take = (above + c1) >= topk
        cand_scr[...] = jnp.where(take, ones, cand_scr[...] ^ ones)
        above = jnp.where(take, above, above + c1)
        tau_u = jnp.where(take, tau_u | (jnp.int32(1) << bit), tau_u)
        return above, tau_u

    zero = jnp.zeros((1, LANES), I32)
    n_gt, tau_u = lax.fori_loop(0, 32, bit_body, (zero, zero))
    tau = tau_u ^ INT_MIN
    n_eq = jnp.sum(jnp.sum(lax.population_count(cand_scr[...]), axis=0), axis=0, keepdims=True)

    need = topk - n_gt
    overflow = n_eq > need
    seq_bits = max(1, (n_blocks * kb_sz - 1).bit_length())

    @pl.when(jnp.max(jnp.where(overflow, 1, 0)) > 0)
    def _():
        def count_ties_before(trial):
            def body(kb, acc):
                hit = jnp.where((keys_scr[kb] == tau) & ((kb * kb_sz + row_id) < trial), 1, 0)
                return acc + jnp.sum(hit.reshape(kb_sz // 8, 8, LANES), axis=0)
            acc = lax.fori_loop(0, n_kb, body, jnp.zeros((8, LANES), I32))
            return jnp.sum(acc, axis=0, keepdims=True)

        def idx_body(it, jc):
            trial = jc | (jnp.int32(1) << (seq_bits - 1 - it))
            return jnp.where(count_ties_before(trial) < need, trial, jc)

        j_cut = lax.fori_loop(0, seq_bits, idx_body, jnp.zeros((1, LANES), I32))

        def demote_body(kb, carry):
            k = keys_scr[kb]
            drop = overflow & (k == tau) & ((kb * kb_sz + row_id) > j_cut)
            keys_scr[kb] = jnp.where(drop, INT_MIN, k)
            return carry

        lax.fori_loop(0, n_kb, demote_body, 0)

    tau_sel = jnp.maximum(tau, INT_MIN + 1)

    def logits(kb, g):
        q_g = qabs_ref[0, 2 * g:2 * g + 2].reshape(2 * Q_BLOCK, D_LATENT)
        return lax.dot_general(ckv_ref[kb], q_g, (((1,), (1,)), ((), ())), preferred_element_type=F32)

    def bias_start(kb):
        delta = jnp.minimum(q0 - kb * kb_sz, BIAS_PAD)
        return pl.multiple_of(BIAS_PAD - delta, LANES)

    ones8 = jnp.ones((8, D_LATENT), BF16)

    @pl.when(qb == 0)
    def _():
        def kn_body(kb, mx):
            c = ckv_ref[kb].astype(F32)
            n2 = lax.dot_general(ones8, (c * c).astype(BF16), (((1,), (1,)), ((), ())),
                                 preferred_element_type=F32)
            return jnp.maximum(mx, n2[0:1])
        mx = lax.fori_loop(0, n_blocks, kn_body, jnp.zeros((1, kb_sz), F32))
        kmax_scr[...] = jnp.max(mx, axis=1, keepdims=True)

    qn2 = []
    for g in range(n_groups):
        q_g = qabs_ref[0, 2 * g:2 * g + 2].reshape(2 * Q_BLOCK, D_LATENT).astype(F32)
        qn2.append(lax.dot_general(ones8, (q_g * q_g).astype(BF16), (((1,), (1,)), ((), ())),
                                   preferred_element_type=F32)[0:1])
    bound = jnp.sqrt(jnp.concatenate(qn2, axis=1) * kmax_scr[...]) * 1.02 + bmax_ref[...] + 1e-3

    acc_scr[...] = jnp.zeros(acc_scr.shape, F32)
    for g in range(n_groups):
        lt_scr[:, g * pair:(g + 1) * pair] = logits(0, g)

    def fast_step(kb):
        kc = jnp.minimum(kb, n_kb - 1)
        kb_next = jnp.minimum(kb + 1, n_kb - 1)
        thr = jnp.where(kb < n_kb, tau_sel - 1, jnp.int32(2 ** 31 - 1))
        sel = keys_scr[kc] > thr
        ct_blk = ckvt_ref[kc]
        start = bias_start(kc)
        for g in range(n_groups):
            lt = lt_scr[:, g * pair:(g + 1) * pair]
            ps = []
            for j in range(2):
                hh = 2 * g + j
                piece = lt[:, j * LANES:(j + 1) * LANES] + btile_ref[hh, pl.ds(start, kb_sz), :]
                ps.append(jnp.exp2(jnp.where(sel, piece, NEG_BIG) - bound[:, hh * LANES:(hh + 1) * LANES]))
            lt_scr[:, g * pair:(g + 1) * pair] = logits(kb_next, g)
            acc_scr[g] += jnp.dot(ct_blk, jnp.concatenate(ps, axis=1).astype(BF16),
                                  preferred_element_type=F32)

    def fast_body(it, carry):
        for u in range(ATTN_UNROLL):
            fast_step(it * ATTN_UNROLL + u)
        return carry

    lax.fori_loop(0, (n_kb + ATTN_UNROLL - 1) // ATTN_UNROLL, fast_body, 0)

    l_min = jnp.min(jnp.concatenate([acc_scr[g, D_LATENT:D_LATENT + 1, :] for g in range(n_groups)], axis=1))

    @pl.when(jnp.logical_not(l_min >= 2.0 ** -80))
    def _():
        m_scr[...] = jnp.full(m_scr.shape, NEG_BIG, F32)
        acc_scr[...] = jnp.zeros(acc_scr.shape, F32)

        def exact_body(kb, carry):
            sel = keys_scr[kb] >= tau_sel
            ct_blk = ckvt_ref[kb]
            start = bias_start(kb)
            for g in range(n_groups):
                lt = logits(kb, g)
                ps, alphas = [], []
                for j in range(2):
                    hh = 2 * g + j
                    sl = slice(hh * LANES, (hh + 1) * LANES)
                    piece = lt[:, j * LANES:(j + 1) * LANES] + btile_ref[hh, pl.ds(start, kb_sz), :]
                    masked = jnp.where(sel, piece, NEG_BIG)
                    m_old = m_scr[:, sl]
                    m_new = jnp.maximum(m_old, jnp.max(masked, axis=0, keepdims=True))
                    m_scr[:, sl] = m_new
                    alphas.append(jnp.exp2(m_old - m_new))
                    ps.append(jnp.exp2(masked - m_new))
                pv = jnp.dot(ct_blk, jnp.concatenate(ps, axis=1).astype(BF16), preferred_element_type=F32)
                acc_scr[g] = jnp.concatenate(alphas, axis=1) * acc_scr[g] + pv
            return carry

        lax.fori_loop(0, n_kb, exact_body, 0)

    ys = []
    for hh in range(N_HEADS_A):
        acc_h = acc_scr[hh // 2, :, (hh % 2) * LANES:(hh % 2 + 1) * LANES]
        o_h = acc_h[:D_LATENT] * (1.0 / acc_h[D_LATENT:D_LATENT + 1])
        ys.append(jnp.dot(wuvt_ref[hh], o_h.astype(BF16), preferred_element_type=F32))
    y_t = jnp.concatenate(ys, axis=0)
    out_ref[...] = y_t.T.astype(BF16)


def _dsa(q_idx, q_abs, w_t, k_idx, ckv, ckv_t, btile, bmax, wuv_t, *, batch, seq):
    t = batch * seq
    nqb = seq // Q_BLOCK
    nkb = seq // KEY_BLOCK
    topk = min(TOPK_MAX, seq // 4)
    k_idx3 = k_idx.reshape(t // KEY_BLOCK, KEY_BLOCK, LANES)
    ckv3 = ckv.reshape(t // KEY_BLOCK, KEY_BLOCK, D_LATENT)
    per_batch = lambda shape: pl.BlockSpec(shape, lambda b, q: (b,) + (0,) * (len(shape) - 1),
                                           pipeline_mode=pl.Buffered(1))
    return pl.pallas_call(
        functools.partial(_dsa_kernel, topk=topk),
        out_shape=jax.ShapeDtypeStruct((t, W_A), BF16),
        grid=(batch, nqb),
        in_specs=[pl.BlockSpec((1, N_HEADS_IDX, Q_BLOCK, LANES), lambda b, q: (b * nqb + q, 0, 0, 0)),
                  pl.BlockSpec((1, N_HEADS_A, Q_BLOCK, D_LATENT), lambda b, q: (b * nqb + q, 0, 0, 0)),
                  pl.BlockSpec((N_HEADS_IDX, Q_BLOCK), lambda b, q: (0, b * nqb + q)),
                  per_batch((nkb, KEY_BLOCK, LANES)),
                  per_batch((nkb, KEY_BLOCK, D_LATENT)),
                  per_batch((nkb, CKVT_ROWS, KEY_BLOCK)),
                  _resident((N_HEADS_A, BIAS_ROWS, LANES)),
                  _resident((1, N_HEADS_A * LANES)),
                  _resident((N_HEADS_A, HEAD_DIM_A, D_LATENT))],
        out_specs=pl.BlockSpec((Q_BLOCK, W_A), lambda b, q: (b * nqb + q, 0)),
        scratch_shapes=[pltpu.VMEM((nkb, KEY_BLOCK, LANES), I32),
                        pltpu.VMEM((32, nkb, 8, LANES), I32),
                        pltpu.VMEM((nkb, 8, LANES), I32),
                        pltpu.VMEM((N_HEADS_A // 2, CKVT_ROWS, 2 * LANES), F32),
                        pltpu.VMEM((1, N_HEADS_A * LANES), F32),
                        pltpu.VMEM((KEY_BLOCK, N_HEADS_A * LANES), F32),
                        pltpu.VMEM((1, 1), F32)],
        compiler_params=pltpu.CompilerParams(dimension_semantics=("arbitrary", "arbitrary"),
                                             vmem_limit_bytes=VMEM_LIMIT),
        name="dsa",
    )(q_idx, q_abs, w_t, k_idx3, ckv3, ckv_t, btile, bmax, wuv_t)


def _mlstm_kernel(qk_ref, v_ref, o_ref, ifc_ref, ift_ref, cw_ref, cb_ref, gbc_ref, gbt_ref, hn_ref,
                  out_ref, c_scr, n_scr, m_scr, tail_scr, *, chunk):
    L = chunk
    step = pl.program_id(1)

    @pl.when(step == 0)
    def _():
        c_scr[...] = jnp.zeros(c_scr.shape, F32)
        n_scr[...] = jnp.zeros(n_scr.shape, F32)
        m_scr[...] = jnp.zeros(m_scr.shape, F32)
        tail_scr[...] = jnp.zeros(tail_scr.shape, F32)

    x = qk_ref[...]
    xe = jnp.concatenate([tail_scr[...], x], axis=0)
    conv = x * cw_ref[CONV_WIDTH - 1:CONV_WIDTH, :] + cb_ref[...]
    for d in range(1, CONV_WIDTH):
        shifted = pltpu.roll(xe, d, axis=0)[8:]
        conv = conv + shifted * cw_ref[CONV_WIDTH - 1 - d:CONV_WIDTH - d, :]
    tail_scr[...] = x[L - 8:]
    qk = conv * _sigmoid(conv)

    a_c = ifc_ref[...] + gbc_ref[...]
    ls_c = _log_sigmoid(a_c)
    a_t = ift_ref[...] + gbt_ref[...]
    ls_t = _log_sigmoid(a_t)
    rr = lax.broadcasted_iota(I32, (L, L), 0)
    cc = lax.broadcasted_iota(I32, (L, L), 1)
    causal = cc <= rr
    tril = jnp.where(causal, 1.0, 0.0).astype(BF16)
    triu = jnp.where(rr <= cc, 1.0, 0.0).astype(BF16)
    b_c = sum(jnp.dot(tril, piece, preferred_element_type=F32) for piece in _split3(ls_c))
    b_t = sum(jnp.dot(piece, triu, preferred_element_type=F32) for piece in _split3(ls_t))

    o_gate = _sigmoid(o_ref[...])
    for hh in range(N_HEADS_M):
        hs = slice(hh * HEAD_DIM_M, (hh + 1) * HEAD_DIM_M)
        q = qk[:, hs]
        k = qk[:, W_M + hh * HEAD_DIM_M:W_M + (hh + 1) * HEAD_DIM_M] * (HEAD_DIM_M ** -0.5)
        v = v_ref[:, hs]
        ci, cf = _SM_I + hh, _SM_F + hh
        i_c = a_c[:, ci:ci + 1]
        bc = b_c[:, cf:cf + 1]
        i_t = a_t[hh:hh + 1, :]
        bt = b_t[N_HEADS_M + hh:N_HEADS_M + hh + 1, :]
        b_last = bc[L - 1:L, :]
        m_prev = m_scr[hh]
        c_prev = c_scr[hh]
        n_prev = n_scr[hh]

        d_log = jnp.where(causal, bc - bt + i_t, NEG_BIG)
        inter_log = bc + m_prev
        m_j = jnp.maximum(inter_log, jnp.max(d_log, axis=1, keepdims=True))
        qb16 = q.astype(BF16)
        s = lax.dot_general(qb16, k.astype(BF16), (((1,), (1,)), ((), ())),
                            preferred_element_type=F32) * jnp.exp(d_log - m_j)
        w_inter = jnp.exp(inter_log - m_j)
        num = (w_inter * jnp.dot(qb16, c_prev.astype(BF16), preferred_element_type=F32)
               + jnp.dot(s.astype(BF16), v, preferred_element_type=F32))
        den = w_inter * jnp.sum(q * n_prev, axis=1, keepdims=True) + jnp.sum(s, axis=1, keepdims=True)
        hval = num / jnp.maximum(jnp.abs(den), jnp.exp(-m_j))

        g_t = b_last - bt + i_t
        g_c = b_last - bc + i_c
        m_new = jnp.maximum(b_last + m_prev, jnp.max(g_t, axis=1, keepdims=True))
        decay = jnp.exp(b_last + m_prev - m_new)
        kw = k * jnp.exp(g_c - m_new)
        c_scr[hh] = decay * c_prev + jnp.dot(kw.T.astype(BF16), v, preferred_element_type=F32)
        n_scr[hh] = decay * n_prev + jnp.sum(kw, axis=0, keepdims=True)
        m_scr[hh] = m_new

        mu = jnp.mean(hval, axis=1, keepdims=True)
        cen = hval - mu
        var = jnp.mean(cen * cen, axis=1, keepdims=True)
        hn = cen * lax.rsqrt(var + EPS) * hn_ref[:, hs]
        out_ref[:, hs] = (hn * o_gate[:, hs]).astype(BF16)


def _mlstm(qk_raw, v, o_pre, ifc, ift, conv_w, conv_b, gate_bias, head_norm, *, batch, seq):
    t = batch * seq
    L = MLSTM_CHUNK
    nc = seq // L
    gbc = jnp.zeros((1, LANES), F32).at[0, _SM_I:_SM_I + 2 * N_HEADS_M].set(gate_bias)
    gbt = jnp.broadcast_to(gate_bias.reshape(2 * N_HEADS_M, 1), (2 * N_HEADS_M, L))
    row = lambda w: pl.BlockSpec((L, w), lambda b, c: (b * nc + c, 0))
    return pl.pallas_call(
        functools.partial(_mlstm_kernel, chunk=L),
        out_shape=jax.ShapeDtypeStruct((t, W_M), BF16),
        grid=(batch, nc),
        in_specs=[row(2 * W_M), row(W_M), row(W_M), row(LANES),
                  pl.BlockSpec((2 * N_HEADS_M, L), lambda b, c: (0, b * nc + c)),
                  _resident((CONV_WIDTH, 2 * W_M)), _resident((1, 2 * W_M)),
                  _resident((1, LANES)), _resident((2 * N_HEADS_M, L)), _resident((1, W_M))],
        out_specs=row(W_M),
        scratch_shapes=[pltpu.VMEM((N_HEADS_M, HEAD_DIM_M, HEAD_DIM_M), F32),
                        pltpu.VMEM((N_HEADS_M, 1, HEAD_DIM_M), F32),
                        pltpu.VMEM((N_HEADS_M, 1, 1), F32),
                        pltpu.VMEM((8, 2 * W_M), F32)],
        compiler_params=pltpu.CompilerParams(dimension_semantics=("arbitrary", "arbitrary"),
                                             vmem_limit_bytes=VMEM_LIMIT),
        name="mlstm",
    )(qk_raw, v, o_pre, ifc, ift, conv_w, conv_b.reshape(1, -1), gbc, gbt, head_norm.reshape(1, -1))


def _merge_kernel(x_ref, ya_ref, hm_ref, ga_ref, gm_ref, g_ref, wa_ref, wm_ref, wo_ref, o_ref):
    pa = jnp.dot(ya_ref[...], wa_ref[...], preferred_element_type=F32)
    pm = jnp.dot(hm_ref[...], wm_ref[...], preferred_element_type=F32)
    merged = _sigmoid(ga_ref[...]) * pa + _sigmoid(gm_ref[...]) * pm
    out = jnp.dot(merged.astype(BF16), wo_ref[...], preferred_element_type=F32)
    o_ref[...] = x_ref[...] + g_ref[0] * out


def _merge(x, y_a, h_m, gate_a, gate_m, g, w_a, w_m, w_o, *, seq):
    t, d = x.shape
    tm = MERGE_TM
    per_b = seq // tm
    row = lambda w: pl.BlockSpec((tm, w), lambda i: (i, 0))
    return pl.pallas_call(
        _merge_kernel,
        out_shape=jax.ShapeDtypeStruct((t, d), F32),
        grid=(t // tm,),
        in_specs=[row(d), row(W_A), row(W_M), row(d), row(d),
                  pl.BlockSpec((1, 1, d), lambda i: (i // per_b, 0, 0)),
                  _resident((W_A, d)), _resident((W_M, d)), _resident((d, d))],
        out_specs=row(d),
        compiler_params=pltpu.CompilerParams(dimension_semantics=("arbitrary",),
                                             vmem_limit_bytes=VMEM_LIMIT),
        name="merge",
    )(x, y_a, h_m, gate_a, gate_m, g, w_a.astype(BF16), w_m.astype(BF16), w_o.astype(BF16))


def kernel(x, c, ada_w, ada_b, ffn1_norm, ffn1_w1, ffn1_w3, ffn1_w2, mix_norm, w_in, conv_w, conv_b,
           kv_norm, w_uk, w_uv, mlstm_gate_bias, mlstm_head_norm, rel_bias, w_branch_attn,
           w_branch_mlstm, w_out, ffn2_norm, ffn2_w1, ffn2_w3, ffn2_w2, final_norm):
    batch, seq, d = x.shape
    depth = ada_w.shape[0]
    assert seq % max(FFN_TM, MIX_TM, MERGE_TM, MLSTM_CHUNK, KEY_BLOCK) == 0
    t = batch * seq
    xf = x.reshape(t, d)
    btile, bmax = _bias_tiles(rel_bias)
    for l in range(depth):
        mod = _adaln(c, ada_w[l], ada_b[l]).reshape(batch, 9, 1, d)
        sh1, sc1, g1, sh2, sc2, g2, sh3, sc3, g3 = [mod[:, n] for n in range(9)]
        xf = _ffn(xf, ffn1_norm[l], sh1, sc1, g1, ffn1_w1[l], ffn1_w3[l], ffn1_w2[l], final_norm,
                  seq=seq, final_norm=False)
        wuk_t = jnp.pad(w_uk[l].transpose(0, 2, 1), ((0, 0), (0, LANES - HEAD_DIM_A), (0, 0))).astype(BF16)
        (q_abs, q_idx, k_idx, ckv, ckv_t, w_t, ifc, ift, qk_raw, v_m, o_pre, gate_a, gate_m) = _mixin(
            xf, mix_norm[l], sh2, sc2, _pack_w_in(w_in[l], d), kv_norm[l], wuk_t, seq=seq)
        wuv_t = w_uv[l].transpose(0, 2, 1).astype(BF16)
        y_a = _dsa(q_idx, q_abs, w_t, k_idx, ckv, ckv_t, btile, bmax, wuv_t, batch=batch, seq=seq)
        h_m = _mlstm(qk_raw, v_m, o_pre, ifc, ift, conv_w[l], conv_b[l], mlstm_gate_bias[l],
                     mlstm_head_norm[l], batch=batch, seq=seq)
        xf = _merge(xf, y_a, h_m, gate_a, gate_m, g2, w_branch_attn[l], w_branch_mlstm[l], w_out[l],
                    seq=seq)
        xf = _ffn(xf, ffn2_norm[l], sh3, sc3, g3, ffn2_w1[l], ffn2_w3[l], ffn2_w2[l], final_norm,
                  seq=seq, final_norm=(l == depth - 1))
    return xf.reshape(batch, seq, d)
```

```python
import functools
import math

import jax
import jax.numpy as jnp
from jax import lax
from jax.experimental import pallas as pl
from jax.experimental.pallas import tpu as pltpu

F32 = jnp.float32
BF16 = jnp.bfloat16
I32 = jnp.int32

LANES = 128
MXU_DIM = 256
VMEM_LIMIT = 56 * 1024 * 1024

N_HEADS_A = 8
HEAD_DIM_A = 64
D_LATENT = 256
N_HEADS_IDX = 8
HEAD_DIM_IDX = 64
TOPK_MAX = 256
Q_BLOCK = 128
N_BUCKETS = 32
MAX_DISTANCE = 128
N_HEADS_M = 4
HEAD_DIM_M = 128
CONV_WIDTH = 4
EPS = 1e-6
IDX_SCALE = (N_HEADS_IDX ** -0.5) * (HEAD_DIM_IDX ** -0.5)
W_A = N_HEADS_A * HEAD_DIM_A
W_M = N_HEADS_M * HEAD_DIM_M

FFN_TM = 512
FFN_CHUNK = 256
MIX_TM = 512
KEY_BLOCK = 256
BLOCK_UNROLL = 2
MLSTM_CHUNK = 256
MERGE_TM = 512
NEG_BIG = -1e30
INT_MIN = -2 ** 31

BIAS_PAD = 384
BIAS_ROWS = KEY_BLOCK + BIAS_PAD
CKVT_ROWS = D_LATENT + 16
LOG2E = math.log2(math.e)


def _sigmoid(x):
    return 1.0 / (1.0 + jnp.exp(-x))


def _log_sigmoid(x):
    return jnp.minimum(x, 0.0) - jnp.log(1.0 + jnp.exp(-jnp.abs(x)))


def _rms_norm(x, gain):
    ms = jnp.mean(x * x, axis=-1, keepdims=True)
    return x * lax.rsqrt(ms + EPS) * gain


def _split3(x):
    hi = x.astype(BF16)
    r1 = x - hi.astype(F32)
    mid = r1.astype(BF16)
    lo = (r1 - mid.astype(F32)).astype(BF16)
    return hi, mid, lo


def _resident(shape):
    nd = len(shape)
    return pl.BlockSpec(shape, lambda *_: (0,) * nd, pipeline_mode=pl.Buffered(1))


def _adaln_kernel(c_ref, w_ref, b_ref, o_ref):
    c = c_ref[...]
    cond = c * _sigmoid(c)
    o_ref[...] = jnp.dot(cond.astype(BF16), w_ref[...].astype(BF16),
                         preferred_element_type=F32) + b_ref[...]


def _adaln(c, ada_w, ada_b):
    b, d = c.shape
    n = ada_w.shape[1]
    rows = 8
    c_pad = jnp.zeros((rows, d), F32).at[:b].set(c)
    tn = 1024
    out = pl.pallas_call(
        _adaln_kernel,
        out_shape=jax.ShapeDtypeStruct((rows, n), F32),
        grid=(n // tn,),
        in_specs=[pl.BlockSpec((rows, d), lambda j: (0, 0)),
                  pl.BlockSpec((d, tn), lambda j: (0, j)),
                  pl.BlockSpec((1, tn), lambda j: (0, j))],
        out_specs=pl.BlockSpec((rows, tn), lambda j: (0, j)),
        compiler_params=pltpu.CompilerParams(dimension_semantics=("arbitrary",),
                                             vmem_limit_bytes=VMEM_LIMIT),
        name="adaln",
    )(c_pad, ada_w, ada_b.reshape(1, n))
    return out[:b]


def _t5_bucket(dist):
    n = jnp.maximum(dist, 0)
    max_exact = N_BUCKETS // 2
    nf = jnp.maximum(n, 1).astype(F32)
    large = max_exact + (jnp.log(nf / max_exact) / math.log(MAX_DISTANCE / max_exact)
                         * (N_BUCKETS - max_exact)).astype(I32)
    large = jnp.minimum(large, N_BUCKETS - 1)
    return jnp.where(n < max_exact, n, large)


def _bias_kernel(rel_ref, tile_ref, max_ref):
    r = lax.broadcasted_iota(I32, (BIAS_ROWS, LANES), 0)
    i = lax.broadcasted_iota(I32, (BIAS_ROWS, LANES), 1)
    bucket = _t5_bucket(i - r + BIAS_PAD)
    for h in range(N_HEADS_A):
        acc = jnp.zeros((BIAS_ROWS, LANES), F32)
        top = rel_ref[0, h] * LOG2E
        for bkt in range(N_BUCKETS):
            val = rel_ref[bkt, h] * LOG2E
            acc = jnp.where(bucket == bkt, val, acc)
            top = jnp.maximum(top, val)
        tile_ref[h] = acc
        max_ref[:, h * LANES:(h + 1) * LANES] = jnp.full((1, LANES), top, F32)


def _bias_tiles(rel_bias):
    return pl.pallas_call(
        _bias_kernel,
        out_shape=(jax.ShapeDtypeStruct((N_HEADS_A, BIAS_ROWS, LANES), F32),
                   jax.ShapeDtypeStruct((1, N_HEADS_A * LANES), F32)),
        in_specs=[pl.BlockSpec(memory_space=pltpu.SMEM)],
        out_specs=(pl.BlockSpec(memory_space=pltpu.VMEM), pl.BlockSpec(memory_space=pltpu.VMEM)),
        name="bias_tiles",
    )(rel_bias)


def _ffn_kernel(x_ref, gain_ref, sh_ref, sc_ref, g_ref, w1_ref, w3_ref, w2_ref, fin_ref, o_ref,
                h_scr, acc_scr, *, n_chunks, final_norm):
    x = x_ref[...]
    h = _rms_norm(x, gain_ref[...]) * (1.0 + sc_ref[0]) + sh_ref[0]
    h_scr[...] = h.astype(BF16)
    for j in range(n_chunks):
        hb = h_scr[...]
        u1 = jnp.dot(hb, w1_ref[j], preferred_element_type=F32)
        u3 = jnp.dot(hb, w3_ref[j], preferred_element_type=F32)
        a = (u1 * _sigmoid(u1)) * u3
        part = jnp.dot(a.astype(BF16), w2_ref[j], preferred_element_type=F32)
        if j == 0:
            acc_scr[...] = part
        else:
            acc_scr[...] += part
    out = x + (0.5 * g_ref[0]) * acc_scr[...]
    if final_norm:
        out = _rms_norm(out, fin_ref[...])
    o_ref[...] = out


def _ffn(x, gain, sh, sc, g, w1, w3, w2, fin, *, seq, final_norm):
    t, d = x.shape
    dff = w1.shape[1]
    nch = dff // FFN_CHUNK
    w1c = w1.astype(BF16).reshape(d, nch, FFN_CHUNK).transpose(1, 0, 2)
    w3c = w3.astype(BF16).reshape(d, nch, FFN_CHUNK).transpose(1, 0, 2)
    w2c = w2.astype(BF16).reshape(nch, FFN_CHUNK, d)
    tm = FFN_TM
    per_b = seq // tm
    mod_spec = pl.BlockSpec((1, 1, d), lambda i: (i // per_b, 0, 0))
    return pl.pallas_call(
        functools.partial(_ffn_kernel, n_chunks=nch, final_norm=final_norm),
        out_shape=jax.ShapeDtypeStruct((t, d), F32),
        grid=(t // tm,),
        in_specs=[pl.BlockSpec((tm, d), lambda i: (i, 0)),
                  _resident((1, d)), mod_spec, mod_spec, mod_spec,
                  _resident((nch, d, FFN_CHUNK)), _resident((nch, d, FFN_CHUNK)),
                  _resident((nch, FFN_CHUNK, d)), _resident((1, d))],
        out_specs=pl.BlockSpec((tm, d), lambda i: (i, 0)),
        scratch_shapes=[pltpu.VMEM((tm, d), BF16), pltpu.VMEM((tm, d), F32)],
        compiler_params=pltpu.CompilerParams(dimension_semantics=("arbitrary",),
                                             vmem_limit_bytes=VMEM_LIMIT),
        name="ffn_final" if final_norm else "ffn",
    )(x, gain.reshape(1, d), sh, sc, g, w1c, w3c, w2c, fin.reshape(1, d))


_C_QA = 0
_C_CKV = _C_QA + N_HEADS_A * LANES
_C_QI = _C_CKV + D_LATENT
_C_KI = _C_QI + N_HEADS_IDX * LANES
_C_SM = _C_KI + LANES
_C_QK = _C_SM + LANES
_C_V = _C_QK + 2 * W_M
_C_O = _C_V + W_M
_C_GA = _C_O + W_M
_C_GM = _C_GA + 1024
_C_END = _C_GM + 1024
_SM_W = 0
_SM_I = N_HEADS_IDX
_SM_F = N_HEADS_IDX + N_HEADS_M


def _pack_w_in(w_in, d_model):
    splits = (W_A, D_LATENT, N_HEADS_IDX * HEAD_DIM_IDX, HEAD_DIM_IDX, N_HEADS_IDX,
              W_M, W_M, W_M, N_HEADS_M, N_HEADS_M, W_M, d_model, d_model)
    offs = [0]
    for s in splits:
        offs.append(offs[-1] + s)
    (q_a, c_kv, q_i, k_i, w_i, q_m, k_m, v_m, i_p, f_p, o_p, g_a, g_m) = [
        w_in[:, offs[n]:offs[n + 1]] for n in range(len(splits))]
    d = w_in.shape[0]

    def pad_heads(w, nh, hd):
        w = w.reshape(d, nh, hd)
        return jnp.pad(w, ((0, 0), (0, 0), (0, LANES - hd))).reshape(d, nh * LANES)

    small = jnp.concatenate([w_i, i_p, f_p], axis=1)
    small = jnp.pad(small, ((0, 0), (0, LANES - small.shape[1])))
    packed = jnp.concatenate([
        pad_heads(q_a, N_HEADS_A, HEAD_DIM_A), c_kv, pad_heads(q_i, N_HEADS_IDX, HEAD_DIM_IDX),
        jnp.pad(k_i, ((0, 0), (0, LANES - HEAD_DIM_IDX))), small, q_m, k_m, v_m, o_p, g_a, g_m], axis=1)
    assert packed.shape[1] == _C_END
    return packed.astype(BF16)


def _mixin_kernel(x_ref, gain_ref, sh_ref, sc_ref, w_ref, kvn_ref, wuk_ref,
                  qabs_ref, qidx_ref, kidx_ref, ckv_ref, ckvt_ref, wt_ref, ifc_ref, ift_ref,
                  qk_ref, v_ref, o_ref, ga_ref, gm_ref, h_scr, *, tm):
    nqb = tm // Q_BLOCK
    x = x_ref[...]
    h = _rms_norm(x, gain_ref[...]) * (1.0 + sc_ref[0]) + sh_ref[0]
    h_scr[...] = h.astype(BF16)

    def proj(lo, hi):
        return jnp.dot(h_scr[...], w_ref[:, lo:hi], preferred_element_type=F32)

    qa = proj(_C_QA, _C_CKV)
    scale = HEAD_DIM_A ** -0.5 * LOG2E
    for hh in range(N_HEADS_A):
        q_h = qa[:, hh * LANES:(hh + 1) * LANES].astype(BF16)
        q_abs = jnp.dot(q_h, wuk_ref[hh], preferred_element_type=F32) * scale
        qabs_ref[:, hh] = q_abs.astype(BF16).reshape(nqb, Q_BLOCK, D_LATENT)
    ckv = _rms_norm(proj(_C_CKV, _C_QI), kvn_ref[...])
    ckv_ref[...] = ckv.astype(BF16)
    ckv_t = ckv.T
    ones_row = jnp.where(lax.broadcasted_iota(I32, (CKVT_ROWS - D_LATENT, KEY_BLOCK), 0) == 0, 1.0, 0.0)
    for j in range(tm // KEY_BLOCK):
        ckvt_ref[j, :D_LATENT] = ckv_t[:, j * KEY_BLOCK:(j + 1) * KEY_BLOCK].astype(BF16)
        ckvt_ref[j, D_LATENT:] = ones_row.astype(BF16)
    qi = proj(_C_QI, _C_KI)
    for hh in range(N_HEADS_IDX):
        qidx_ref[:, hh] = qi[:, hh * LANES:(hh + 1) * LANES].astype(BF16).reshape(nqb, Q_BLOCK, LANES)
    kidx_ref[...] = proj(_C_KI, _C_SM).astype(BF16)
    small = proj(_C_SM, _C_QK)
    ifc_ref[...] = small
    small_t = small.T
    wt_ref[...] = small_t[_SM_W:_SM_W + N_HEADS_IDX] * IDX_SCALE
    ift_ref[...] = small_t[_SM_I:_SM_I + 2 * N_HEADS_M]
    qk_ref[...] = proj(_C_QK, _C_V)
    v_ref[...] = proj(_C_V, _C_O).astype(BF16)
    o_ref[...] = proj(_C_O, _C_GA)
    ga_ref[...] = proj(_C_GA, _C_GM)
    gm_ref[...] = proj(_C_GM, _C_END)


def _mixin(x, gain, sh, sc, w_packed, kv_norm, wuk_t, *, seq):
    t, d = x.shape
    tm = MIX_TM
    per_b = seq // tm
    nqb = tm // Q_BLOCK
    row = lambda w: pl.BlockSpec((tm, w), lambda i: (i, 0))
    mod_spec = pl.BlockSpec((1, 1, d), lambda i: (i // per_b, 0, 0))
    out_shape = (
        jax.ShapeDtypeStruct((t // Q_BLOCK, N_HEADS_A, Q_BLOCK, D_LATENT), BF16),
        jax.ShapeDtypeStruct((t // Q_BLOCK, N_HEADS_IDX, Q_BLOCK, LANES), BF16),
        jax.ShapeDtypeStruct((t, LANES), BF16),
        jax.ShapeDtypeStruct((t, D_LATENT), BF16),
        jax.ShapeDtypeStruct((t // KEY_BLOCK, CKVT_ROWS, KEY_BLOCK), BF16),
        jax.ShapeDtypeStruct((N_HEADS_IDX, t), F32),
        jax.ShapeDtypeStruct((t, LANES), F32),
        jax.ShapeDtypeStruct((2 * N_HEADS_M, t), F32),
        jax.ShapeDtypeStruct((t, 2 * W_M), F32),
        jax.ShapeDtypeStruct((t, W_M), BF16),
        jax.ShapeDtypeStruct((t, W_M), F32),
        jax.ShapeDtypeStruct((t, d), F32),
        jax.ShapeDtypeStruct((t, d), F32),
    )
    out_specs = (
        pl.BlockSpec((nqb, N_HEADS_A, Q_BLOCK, D_LATENT), lambda i: (i, 0, 0, 0)),
        pl.BlockSpec((nqb, N_HEADS_IDX, Q_BLOCK, LANES), lambda i: (i, 0, 0, 0)),
        row(LANES), row(D_LATENT),
        pl.BlockSpec((tm // KEY_BLOCK, CKVT_ROWS, KEY_BLOCK), lambda i: (i, 0, 0)),
        pl.BlockSpec((N_HEADS_IDX, tm), lambda i: (0, i)),
        row(LANES),
        pl.BlockSpec((2 * N_HEADS_M, tm), lambda i: (0, i)),
        row(2 * W_M), row(W_M), row(W_M), row(d), row(d),
    )
    return pl.pallas_call(
        functools.partial(_mixin_kernel, tm=tm),
        out_shape=out_shape,
        grid=(t // tm,),
        in_specs=[pl.BlockSpec((tm, d), lambda i: (i, 0)), _resident((1, d)), mod_spec, mod_spec,
                  _resident((d, _C_END)), _resident((1, D_LATENT)),
                  _resident((N_HEADS_A, LANES, D_LATENT))],
        out_specs=out_specs,
        scratch_shapes=[pltpu.VMEM((tm, d), BF16)],
        compiler_params=pltpu.CompilerParams(dimension_semantics=("arbitrary",),
                                             vmem_limit_bytes=VMEM_LIMIT),
        name="mixin",
    )(x, gain.reshape(1, d), sh, sc, w_packed, kv_norm.reshape(1, D_LATENT), wuk_t)


def _sortable_key(score):
    bits = pltpu.bitcast(score, I32)
    bits = jnp.where(bits == INT_MIN, 0, bits)
    return jnp.where(bits < 0, bits ^ 0x7FFFFFFF, bits)


def _bit_transpose32(words):
    v = list(words)
    j, m = 16, 0x0000FFFF
    while j:
        k = 0
        while k < 32:
            t = (v[k] ^ lax.shift_right_logical(v[k + j], jnp.int32(j))) & m
            v[k] = v[k] ^ t
            v[k + j] = v[k + j] ^ (t << j)
            k = (k + j + 1) & ~j
        j >>= 1
        m = (m ^ (m << j)) & 0x7FFFFFFF
    return v


def _dsa_kernel(qidx_ref, qabs_ref, wt_ref, kidx_ref, ckv_ref, ckvt_ref, btile_ref, bmax_ref, wuvt_ref,
                out_ref, keys_scr, planes_scr, cand_scr, tau_scr, acc_scr, m_scr, lta_scr, ltc_scr, kmax_scr,
                *, topk, n_qb):
    kb_sz = KEY_BLOCK
    step = pl.program_id(1)
    has_c = step >= 1
    qa = jnp.minimum(step, n_qb - 1)
    qc = jnp.maximum(step - 1, 0)
    slot_a = step & 1
    slot_c = 1 - slot_a
    n_a = qa // (kb_sz // Q_BLOCK) + 1
    n_c = qc // (kb_sz // Q_BLOCK) + 1
    qa0 = qa * Q_BLOCK
    qc0 = qc * Q_BLOCK
    row_id = lax.broadcasted_iota(I32, (kb_sz, LANES), 0)
    lane_id = lax.broadcasted_iota(I32, (kb_sz, LANES), 1)
    n_blocks = keys_scr.shape[1]
    n_groups = N_HEADS_A // 2
    pair = 2 * LANES
    ones8 = jnp.ones((8, D_LATENT), BF16)

    @pl.when(step == 0)
    def _():
        keys_scr[1, 0] = jnp.full((kb_sz, LANES), INT_MIN, I32)
        tau_scr[1] = jnp.zeros((1, LANES), I32)

        def kn_body(kb, mx):
            c = ckv_ref[kb].astype(F32)
            n2 = lax.dot_general(ones8, (c * c).astype(BF16), (((1,), (1,)), ((), ())),
                                 preferred_element_type=F32)
            return jnp.maximum(mx, n2[0:1])
        mx = lax.fori_loop(0, n_blocks, kn_body, jnp.zeros((1, kb_sz), F32))
        kmax_scr[...] = jnp.max(mx, axis=1, keepdims=True)

    def idx_dot(kb, g):
        q_g = qidx_ref[0, 2 * g:2 * g + 2].reshape(2 * Q_BLOCK, LANES)
        return lax.dot_general(kidx_ref[kb], q_g, (((1,), (1,)), ((), ())), preferred_element_type=F32)

    def logits(kb, g):
        q_g = qabs_ref[0, 2 * g:2 * g + 2].reshape(2 * Q_BLOCK, D_LATENT)
        return lax.dot_general(ckv_ref[kb], q_g, (((1,), (1,)), ((), ())), preferred_element_type=F32)

    def bias_start(kb):
        delta = jnp.minimum(qc0 - kb * kb_sz, BIAS_PAD)
        return pl.multiple_of(BIAS_PAD - delta, LANES)

    qn2 = []
    for g in range(n_groups):
        q_g = qabs_ref[0, 2 * g:2 * g + 2].reshape(2 * Q_BLOCK, D_LATENT).astype(F32)
        qn2.append(lax.dot_general(ones8, (q_g * q_g).astype(BF16), (((1,), (1,)), ((), ())),
                                   preferred_element_type=F32)[0:1])
    bound = jnp.sqrt(jnp.concatenate(qn2, axis=1) * kmax_scr[...]) * 1.02 + bmax_ref[...] + 1e-3
    tau_c = tau_scr[slot_c]
    w_t = wt_ref[...]

    acc_scr[...] = jnp.zeros(acc_scr.shape, F32)
    for g in range(n_groups):
        lta_scr[:, g * pair:(g + 1) * pair] = idx_dot(0, g)
        ltc_scr[:, g * pair:(g + 1) * pair] = logits(0, g)

    def block_step(kb_raw):
        kb = jnp.minimum(kb_raw, n_a - 1)
        kb_next = jnp.minimum(kb_raw + 1, n_a - 1)
        kc = jnp.minimum(kb_raw, n_c - 1)
        kc_next = jnp.minimum(kb_raw + 1, n_c - 1)
        thr = jnp.where(has_c & (kb_raw < n_c), tau_c - 1, jnp.int32(2 ** 31 - 1))
        sel = keys_scr[slot_c, kc] > thr
        ct_blk = ckvt_ref[kc]
        start = bias_start(kc)
        score = jnp.zeros((kb_sz, LANES), F32)
        for g in range(n_groups):
            s_t = lta_scr[:, g * pair:(g + 1) * pair]
            for j in range(2):
                hh = 2 * g + j
                score = score + jnp.maximum(s_t[:, j * LANES:(j + 1) * LANES], 0.0) * w_t[hh:hh + 1, :]
            lta_scr[:, g * pair:(g + 1) * pair] = idx_dot(kb_next, g)
            lt = ltc_scr[:, g * pair:(g + 1) * pair]
            ps = []
            for j in range(2):
                hh = 2 * g + j
                piece = lt[:, j * LANES:(j + 1) * LANES] + btile_ref[hh, pl.ds(start, kb_sz), :]
                ps.append(jnp.exp2(jnp.where(sel, piece, NEG_BIG) - bound[:, hh * LANES:(hh + 1) * LANES]))
            ltc_scr[:, g * pair:(g + 1) * pair] = logits(kc_next, g)
            acc_scr[g] += jnp.dot(ct_blk, jnp.concatenate(ps, axis=1).astype(BF16),
                                  preferred_element_type=F32)
        valid = (kb * kb_sz + row_id) <= (qa0 + lane_id)
        keys = jnp.where(valid, _sortable_key(score), INT_MIN)
        keys_scr[slot_a, kb] = keys
        v = keys ^ INT_MIN
        words = _bit_transpose32([v[8 * i:8 * (i + 1), :] for i in range(32)])
        for bit in range(32):
            planes_scr[bit, kb] = words[31 - bit]

    def block_body(it, carry):
        for u in range(BLOCK_UNROLL):
            block_step(it * BLOCK_UNROLL + u)
        return carry

    lax.fori_loop(0, (n_a + BLOCK_UNROLL - 1) // BLOCK_UNROLL, block_body, 0)
    n_kb = n_a


    def clear_body(kb, carry):
        for bit in range(32):
            planes_scr[bit, kb] = jnp.zeros((8, LANES), I32)
        return carry

    lax.fori_loop(n_kb, n_blocks, clear_body, 0)
    blk_id = lax.broadcasted_iota(I32, cand_scr.shape, 0)
    cand_scr[...] = jnp.where(blk_id < n_kb, -1, 0)

    def bit_body(it, carry):
        above, tau_u = carry
        bit = 31 - it
        ones = cand_scr[...] & planes_scr[bit]
        c1 = jnp.sum(jnp.sum(lax.population_count(ones), axis=0), axis=0, keepdims=True)
        take = (above + c1) >= topk
        cand_scr[...] = jnp.where(take, ones, cand_scr[...] ^ ones)
        above = jnp.where(take, above, above + c1)
        tau_u = jnp.where(take, tau_u | (jnp.int32(1) << bit), tau_u)
        return above, tau_u

    zero = jnp.zeros((1, LANES), I32)
    n_gt, tau_u = lax.fori_loop(0, 32, bit_body, (zero, zero))
    tau = tau_u ^ INT_MIN
    n_eq = jnp.sum(jnp.sum(lax.population_count(cand_scr[...]), axis=0), axis=0, keepdims=True)

    need = topk - n_gt
    overflow = n_eq > need
    seq_bits = max(1, (n_blocks * kb_sz - 1).bit_length())

    @pl.when(jnp.max(jnp.where(overflow, 1, 0)) > 0)
    def _():
        def count_ties_before(trial):
            def body(kb, acc):
                hit = jnp.where((keys_scr[slot_a, kb] == tau) & ((kb * kb_sz + row_id) < trial), 1, 0)
                return acc + jnp.sum(hit.reshape(kb_sz // 8, 8, LANES), axis=0)
            acc = lax.fori_loop(0, n_kb, body, jnp.zeros((8, LANES), I32))
            return jnp.sum(acc, axis=0, keepdims=True)

        def idx_body(it, jc):
            trial = jc | (jnp.int32(1) << (seq_bits - 1 - it))
            return jnp.where(count_ties_before(trial) < need, trial, jc)

        j_cut = lax.fori_loop(0, seq_bits, idx_body, jnp.zeros((1, LANES), I32))

        def demote_body(kb, carry):
            k = keys_scr[slot_a, kb]
            drop = overflow & (k == tau) & ((kb * kb_sz + row_id) > j_cut)
            keys_scr[slot_a, kb] = jnp.where(drop, INT_MIN, k)
            return carry

        lax.fori_loop(0, n_kb, demote_body, 0)

    tau_scr[slot_a] = jnp.maximum(tau, INT_MIN + 1)

    l_min = jnp.min(jnp.concatenate([acc_scr[g, D_LATENT:D_LATENT + 1, :] for g in range(n_groups)], axis=1))

    @pl.when(has_c & jnp.logical_not(l_min >= 2.0 ** -80))
    def _():
        m_scr[...] = jnp.full(m_scr.shape, NEG_BIG, F32)
        acc_scr[...] = jnp.zeros(acc_scr.shape, F32)

        def exact_body(kb, carry):
            sel = keys_scr[slot_c, kb] >= tau_c
            ct_blk = ckvt_ref[kb]
            start = bias_start(kb)
            for g in range(n_groups):
                lt = logits(kb, g)
                ps, alphas = [], []
                for j in range(2):
                    hh = 2 * g + j
                    sl = slice(hh * LANES, (hh + 1) * LANES)
                    piece = lt[:, j * LANES:(j + 1) * LANES] + btile_ref[hh, pl.ds(start, kb_sz), :]
                    masked = jnp.where(sel, piece, NEG_BIG)
                    m_old = m_scr[:, sl]
                    m_new = jnp.maximum(m_old, jnp.max(masked, axis=0, keepdims=True))
                    m_scr[:, sl] = m_new
                    alphas.append(jnp.exp2(m_old - m_new))
                    ps.append(jnp.exp2(masked - m_new))
                pv = jnp.dot(ct_blk, jnp.concatenate(ps, axis=1).astype(BF16), preferred_element_type=F32)
                acc_scr[g] = jnp.concatenate(alphas, axis=1) * acc_scr[g] + pv
            return carry

        lax.fori_loop(0, n_c, exact_body, 0)

    @pl.when(has_c)
    def _():
        ys = []
        for hh in range(N_HEADS_A):
            acc_h = acc_scr[hh // 2, :, (hh % 2) * LANES:(hh % 2 + 1) * LANES]
            o_h = acc_h[:D_LATENT] * (1.0 / acc_h[D_LATENT:D_LATENT + 1])
            ys.append(jnp.dot(wuvt_ref[hh], o_h.astype(BF16), preferred_element_type=F32))
        y_t = jnp.concatenate(ys, axis=0)
        out_ref[...] = y_t.T.astype(BF16)


def _dsa(q_idx, q_abs, w_t, k_idx, ckv, ckv_t, btile, bmax, wuv_t, *, batch, seq):
    t = batch * seq
    nqb = seq // Q_BLOCK
    nkb = seq // KEY_BLOCK
    topk = min(TOPK_MAX, seq // 4)
    k_idx3 = k_idx.reshape(t // KEY_BLOCK, KEY_BLOCK, LANES)
    ckv3 = ckv.reshape(t // KEY_BLOCK, KEY_BLOCK, D_LATENT)
    per_batch = lambda shape: pl.BlockSpec(shape, lambda b, q: (b,) + (0,) * (len(shape) - 1),
                                           pipeline_mode=pl.Buffered(1))
    scored = lambda b, s: b * nqb + jnp.minimum(s, nqb - 1)
    attended = lambda b, s: b * nqb + jnp.maximum(s - 1, 0)
    return pl.pallas_call(
        functools.partial(_dsa_kernel, topk=topk, n_qb=nqb),
        out_shape=jax.ShapeDtypeStruct((t, W_A), BF16),
        grid=(batch, nqb + 1),
        in_specs=[pl.BlockSpec((1, N_HEADS_IDX, Q_BLOCK, LANES), lambda b, s: (scored(b, s), 0, 0, 0)),
                  pl.BlockSpec((1, N_HEADS_A, Q_BLOCK, D_LATENT), lambda b, s: (attended(b, s), 0, 0, 0)),
                  pl.BlockSpec((N_HEADS_IDX, Q_BLOCK), lambda b, s: (0, scored(b, s))),
                  per_batch((nkb, KEY_BLOCK, LANES)),
                  per_batch((nkb, KEY_BLOCK, D_LATENT)),
                  per_batch((nkb, CKVT_ROWS, KEY_BLOCK)),
                  _resident((N_HEADS_A, BIAS_ROWS, LANES)),
                  _resident((1, N_HEADS_A * LANES)),
                  _resident((N_HEADS_A, HEAD_DIM_A, D_LATENT))],
        out_specs=pl.BlockSpec((Q_BLOCK, W_A), lambda b, s: (attended(b, s), 0)),
        scratch_shapes=[pltpu.VMEM((2, nkb, KEY_BLOCK, LANES), I32),
                        pltpu.VMEM((32, nkb, 8, LANES), I32),
                        pltpu.VMEM((nkb, 8, LANES), I32),
                        pltpu.VMEM((2, 1, LANES), I32),
                        pltpu.VMEM((N_HEADS_A // 2, CKVT_ROWS, 2 * LANES), F32),
                        pltpu.VMEM((1, N_HEADS_A * LANES), F32),
                        pltpu.VMEM((KEY_BLOCK, N_HEADS_A * LANES), F32),
                        pltpu.VMEM((KEY_BLOCK, N_HEADS_A * LANES), F32),
                        pltpu.VMEM((1, 1), F32)],
        compiler_params=pltpu.CompilerParams(dimension_semantics=("arbitrary", "arbitrary"),
                                             vmem_limit_bytes=VMEM_LIMIT),
        name="dsa",
    )(q_idx, q_abs, w_t, k_idx3, ckv3, ckv_t, btile, bmax, wuv_t)


def _mlstm_kernel(qk_ref, v_ref, o_ref, ifc_ref, ift_ref, cw_ref, cb_ref, gbc_ref, gbt_ref, hn_ref,
                  out_ref, c_scr, n_scr, m_scr, tail_scr, *, chunk):
    L = chunk
    step = pl.program_id(1)

    @pl.when(step == 0)
    def _():
        c_scr[...] = jnp.zeros(c_scr.shape, F32)
        n_scr[...] = jnp.zeros(n_scr.shape, F32)
        m_scr[...] = jnp.zeros(m_scr.shape, F32)
        tail_scr[...] = jnp.zeros(tail_scr.shape, F32)

    x = qk_ref[...]
    xe = jnp.concatenate([tail_scr[...], x], axis=0)
    conv = x * cw_ref[CONV_WIDTH - 1:CONV_WIDTH, :] + cb_ref[...]
    for d in range(1, CONV_WIDTH):
        shifted = pltpu.roll(xe, d, axis=0)[8:]
        conv = conv + shifted * cw_ref[CONV_WIDTH - 1 - d:CONV_WIDTH - d, :]
    tail_scr[...] = x[L - 8:]
    qk = conv * _sigmoid(conv)

    a_c = ifc_ref[...] + gbc_ref[...]
    ls_c = _log_sigmoid(a_c)
    a_t = ift_ref[...] + gbt_ref[...]
    ls_t = _log_sigmoid(a_t)
    rr = lax.broadcasted_iota(I32, (L, L), 0)
    cc = lax.broadcasted_iota(I32, (L, L), 1)
    causal = cc <= rr
    tril = jnp.where(causal, 1.0, 0.0).astype(BF16)
    triu = jnp.where(rr <= cc, 1.0, 0.0).astype(BF16)
    b_c = sum(jnp.dot(tril, piece, preferred_element_type=F32) for piece in _split3(ls_c))
    b_t = sum(jnp.dot(piece, triu, preferred_element_type=F32) for piece in _split3(ls_t))

    o_gate = _sigmoid(o_ref[...])
    for hh in range(N_HEADS_M):
        hs = slice(hh * HEAD_DIM_M, (hh + 1) * HEAD_DIM_M)
        q = qk[:, hs]
        k = qk[:, W_M + hh * HEAD_DIM_M:W_M + (hh + 1) * HEAD_DIM_M] * (HEAD_DIM_M ** -0.5)
        v = v_ref[:, hs]
        ci, cf = _SM_I + hh, _SM_F + hh
        i_c = a_c[:, ci:ci + 1]
        bc = b_c[:, cf:cf + 1]
        i_t = a_t[hh:hh + 1, :]
        bt = b_t[N_HEADS_M + hh:N_HEADS_M + hh + 1, :]
        b_last = bc[L - 1:L, :]
        m_prev = m_scr[hh]
        c_prev = c_scr[hh]
        n_prev = n_scr[hh]

        d_log = jnp.where(causal, bc - bt + i_t, NEG_BIG)
        inter_log = bc + m_prev
        m_j = jnp.maximum(inter_log, jnp.max(d_log, axis=1, keepdims=True))
        qb16 = q.astype(BF16)
        s = lax.dot_general(qb16, k.astype(BF16), (((1,), (1,)), ((), ())),
                            preferred_element_type=F32) * jnp.exp(d_log - m_j)
        w_inter = jnp.exp(inter_log - m_j)
        num = (w_inter * jnp.dot(qb16, c_prev.astype(BF16), preferred_element_type=F32)
               + jnp.dot(s.astype(BF16), v, preferred_element_type=F32))
        den = w_inter * jnp.sum(q * n_prev, axis=1, keepdims=True) + jnp.sum(s, axis=1, keepdims=True)
        hval = num / jnp.maximum(jnp.abs(den), jnp.exp(-m_j))

        g_t = b_last - bt + i_t
        g_c = b_last - bc + i_c
        m_new = jnp.maximum(b_last + m_prev, jnp.max(g_t, axis=1, keepdims=True))
        decay = jnp.exp(b_last + m_prev - m_new)
        kw = k * jnp.exp(g_c - m_new)
        c_scr[hh] = decay * c_prev + jnp.dot(kw.T.astype(BF16), v, preferred_element_type=F32)
        n_scr[hh] = decay * n_prev + jnp.sum(kw, axis=0, keepdims=True)
        m_scr[hh] = m_new

        mu = jnp.mean(hval, axis=1, keepdims=True)
        cen = hval - mu
        var = jnp.mean(cen * cen, axis=1, keepdims=True)
        hn = cen * lax.rsqrt(var + EPS) * hn_ref[:, hs]
        out_ref[:, hs] = (hn * o_gate[:, hs]).astype(BF16)


def _mlstm(qk_raw, v, o_pre, ifc, ift, conv_w, conv_b, gate_bias, head_norm, *, batch, seq):
    t = batch * seq
    L = MLSTM_CHUNK
    nc = seq // L
    gbc = jnp.zeros((1, LANES), F32).at[0, _SM_I:_SM_I + 2 * N_HEADS_M].set(gate_bias)
    gbt = jnp.broadcast_to(gate_bias.reshape(2 * N_HEADS_M, 1), (2 * N_HEADS_M, L))
    row = lambda w: pl.BlockSpec((L, w), lambda b, c: (b * nc + c, 0))
    return pl.pallas_call(
        functools.partial(_mlstm_kernel, chunk=L),
        out_shape=jax.ShapeDtypeStruct((t, W_M), BF16),
        grid=(batch, nc),
        in_specs=[row(2 * W_M), row(W_M), row(W_M), row(LANES),
                  pl.BlockSpec((2 * N_HEADS_M, L), lambda b, c: (0, b * nc + c)),
                  _resident((CONV_WIDTH, 2 * W_M)), _resident((1, 2 * W_M)),
                  _resident((1, LANES)), _resident((2 * N_HEADS_M, L)), _resident((1, W_M))],
        out_specs=row(W_M),
        scratch_shapes=[pltpu.VMEM((N_HEADS_M, HEAD_DIM_M, HEAD_DIM_M), F32),
                        pltpu.VMEM((N_HEADS_M, 1, HEAD_DIM_M), F32),
                        pltpu.VMEM((N_HEADS_M, 1, 1), F32),
                        pltpu.VMEM((8, 2 * W_M), F32)],
        compiler_params=pltpu.CompilerParams(dimension_semantics=("arbitrary", "arbitrary"),
                                             vmem_limit_bytes=VMEM_LIMIT),
        name="mlstm",
    )(qk_raw, v, o_pre, ifc, ift, conv_w, conv_b.reshape(1, -1), gbc, gbt, head_norm.reshape(1, -1))


def _merge_kernel(x_ref, ya_ref, hm_ref, ga_ref, gm_ref, g_ref, wa_ref, wm_ref, wo_ref, o_ref):
    pa = jnp.dot(ya_ref[...], wa_ref[...], preferred_element_type=F32)
    pm = jnp.dot(hm_ref[...], wm_ref[...], preferred_element_type=F32)
    merged = _sigmoid(ga_ref[...]) * pa + _sigmoid(gm_ref[...]) * pm
    out = jnp.dot(merged.astype(BF16), wo_ref[...], preferred_element_type=F32)
    o_ref[...] = x_ref[...] + g_ref[0] * out


def _merge(x, y_a, h_m, gate_a, gate_m, g, w_a, w_m, w_o, *, seq):
    t, d = x.shape
    tm = MERGE_TM
    per_b = seq // tm
    row = lambda w: pl.BlockSpec((tm, w), lambda i: (i, 0))
    return pl.pallas_call(
        _merge_kernel,
        out_shape=jax.ShapeDtypeStruct((t, d), F32),
        grid=(t // tm,),
        in_specs=[row(d), row(W_A), row(W_M), row(d), row(d),
                  pl.BlockSpec((1, 1, d), lambda i: (i // per_b, 0, 0)),
                  _resident((W_A, d)), _resident((W_M, d)), _resident((d, d))],
        out_specs=row(d),
        compiler_params=pltpu.CompilerParams(dimension_semantics=("arbitrary",),
                                             vmem_limit_bytes=VMEM_LIMIT),
        name="merge",
    )(x, y_a, h_m, gate_a, gate_m, g, w_a.astype(BF16), w_m.astype(BF16), w_o.astype(BF16))


def kernel(x, c, ada_w, ada_b, ffn1_norm, ffn1_w1, ffn1_w3, ffn1_w2, mix_norm, w_in, conv_w, conv_b,
           kv_norm, w_uk, w_uv, mlstm_gate_bias, mlstm_head_norm, rel_bias, w_branch_attn,
           w_branch_mlstm, w_out, ffn2_norm, ffn2_w1, ffn2_w3, ffn2_w2, final_norm):
    batch, seq, d = x.shape
    depth = ada_w.shape[0]
    assert seq % max(FFN_TM, MIX_TM, MERGE_TM, MLSTM_CHUNK, KEY_BLOCK) == 0
    t = batch * seq
    xf = x.reshape(t, d)
    btile, bmax = _bias_tiles(rel_bias)
    for l in range(depth):
        mod = _adaln(c, ada_w[l], ada_b[l]).reshape(batch, 9, 1, d)
        sh1, sc1, g1, sh2, sc2, g2, sh3, sc3, g3 = [mod[:, n] for n in range(9)]
        xf = _ffn(xf, ffn1_norm[l], sh1, sc1, g1, ffn1_w1[l], ffn1_w3[l], ffn1_w2[l], final_norm,
                  seq=seq, final_norm=False)
        wuk_t = jnp.pad(w_uk[l].transpose(0, 2, 1), ((0, 0), (0, LANES - HEAD_DIM_A), (0, 0))).astype(BF16)
        (q_abs, q_idx, k_idx, ckv, ckv_t, w_t, ifc, ift, qk_raw, v_m, o_pre, gate_a, gate_m) = _mixin(
            xf, mix_norm[l], sh2, sc2, _pack_w_in(w_in[l], d), kv_norm[l], wuk_t, seq=seq)
        wuv_t = w_uv[l].transpose(0, 2, 1).astype(BF16)
        y_a = _dsa(q_idx, q_abs, w_t, k_idx, ckv, ckv_t, btile, bmax, wuv_t, batch=batch, seq=seq)
        h_m = _mlstm(qk_raw, v_m, o_pre, ifc, ift, conv_w[l], conv_b[l], mlstm_gate_bias[l],
                     mlstm_head_norm[l], batch=batch, seq=seq)
        xf = _merge(xf, y_a, h_m, gate_a, gate_m, g2, w_branch_attn[l], w_branch_mlstm[l], w_out[l],
                    seq=seq)
        xf = _ffn(xf, ffn2_norm[l], sh3, sc3, g3, ffn2_w1[l], ffn2_w3[l], ffn2_w2[l], final_norm,
                  seq=seq, final_norm=(l == depth - 1))
    return xf.reshape(batch, seq, d)
```

```python
import functools
import math

import jax
import jax.numpy as jnp
from jax import lax
from jax.experimental import pallas as pl
from jax.experimental.pallas import tpu as pltpu

F32 = jnp.float32
BF16 = jnp.bfloat16
I32 = jnp.int32

LANES = 128
MXU_DIM = 256
VMEM_LIMIT = 56 * 1024 * 1024

N_HEADS_A = 8
HEAD_DIM_A = 64
D_LATENT = 256
N_HEADS_IDX = 8
HEAD_DIM_IDX = 64
TOPK_MAX = 256
Q_BLOCK = 128
N_BUCKETS = 32
MAX_DISTANCE = 128
N_HEADS_M = 4
HEAD_DIM_M = 128
CONV_WIDTH = 4
EPS = 1e-6
IDX_SCALE = (N_HEADS_IDX ** -0.5) * (HEAD_DIM_IDX ** -0.5)
W_A = N_HEADS_A * HEAD_DIM_A
W_M = N_HEADS_M * HEAD_DIM_M

FFN_TM = 512
FFN_CHUNK = 256
MIX_TM = 512
KEY_BLOCK = 256
BLOCK_UNROLL = 2
MLSTM_CHUNK = 256
MERGE_TM = 512
NEG_BIG = -1e30
INT_MIN = -2 ** 31

BIAS_PAD = 384
BIAS_ROWS = KEY_BLOCK + BIAS_PAD
CKVT_ROWS = D_LATENT + 16
LOG2E = math.log2(math.e)


def _sigmoid(x):
    return 1.0 / (1.0 + jnp.exp(-x))


def _log_sigmoid(x):
    return jnp.minimum(x, 0.0) - jnp.log(1.0 + jnp.exp(-jnp.abs(x)))


def _rms_norm(x, gain):
    ms = jnp.mean(x * x, axis=-1, keepdims=True)
    return x * lax.rsqrt(ms + EPS) * gain


def _split3(x):
    hi = x.astype(BF16)
    r1 = x - hi.astype(F32)
    mid = r1.astype(BF16)
    lo = (r1 - mid.astype(F32)).astype(BF16)
    return hi, mid, lo


def _resident(shape):
    nd = len(shape)
    return pl.BlockSpec(shape, lambda *_: (0,) * nd, pipeline_mode=pl.Buffered(1))


def _adaln_kernel(c_ref, w_ref, b_ref, o_ref):
    c = c_ref[...]
    cond = c * _sigmoid(c)
    o_ref[...] = jnp.dot(cond.astype(BF16), w_ref[...].astype(BF16),
                         preferred_element_type=F32) + b_ref[...]


def _adaln(c, ada_w, ada_b):
    b, d = c.shape
    n = ada_w.shape[1]
    rows = 8
    c_pad = jnp.zeros((rows, d), F32).at[:b].set(c)
    tn = 1024
    out = pl.pallas_call(
        _adaln_kernel,
        out_shape=jax.ShapeDtypeStruct((rows, n), F32),
        grid=(n // tn,),
        in_specs=[pl.BlockSpec((rows, d), lambda j: (0, 0)),
                  pl.BlockSpec((d, tn), lambda j: (0, j)),
                  pl.BlockSpec((1, tn), lambda j: (0, j))],
        out_specs=pl.BlockSpec((rows, tn), lambda j: (0, j)),
        compiler_params=pltpu.CompilerParams(dimension_semantics=("arbitrary",),
                                             vmem_limit_bytes=VMEM_LIMIT),
        name="adaln",
    )(c_pad, ada_w, ada_b.reshape(1, n))
    return out[:b]


def _t5_bucket(dist):
    n = jnp.maximum(dist, 0)
    max_exact = N_BUCKETS // 2
    nf = jnp.maximum(n, 1).astype(F32)
    large = max_exact + (jnp.log(nf / max_exact) / math.log(MAX_DISTANCE / max_exact)
                         * (N_BUCKETS - max_exact)).astype(I32)
    large = jnp.minimum(large, N_BUCKETS - 1)
    return jnp.where(n < max_exact, n, large)


def _bias_kernel(rel_ref, tile_ref, max_ref):
    r = lax.broadcasted_iota(I32, (BIAS_ROWS, LANES), 0)
    i = lax.broadcasted_iota(I32, (BIAS_ROWS, LANES), 1)
    bucket = _t5_bucket(i - r + BIAS_PAD)
    for h in range(N_HEADS_A):
        acc = jnp.zeros((BIAS_ROWS, LANES), F32)
        top = rel_ref[0, h] * LOG2E
        for bkt in range(N_BUCKETS):
            val = rel_ref[bkt, h] * LOG2E
            acc = jnp.where(bucket == bkt, val, acc)
            top = jnp.maximum(top, val)
        tile_ref[h] = acc
        max_ref[:, h * LANES:(h + 1) * LANES] = jnp.full((1, LANES), top, F32)


def _bias_tiles(rel_bias):
    return pl.pallas_call(
        _bias_kernel,
        out_shape=(jax.ShapeDtypeStruct((N_HEADS_A, BIAS_ROWS, LANES), F32),
                   jax.ShapeDtypeStruct((1, N_HEADS_A * LANES), F32)),
        in_specs=[pl.BlockSpec(memory_space=pltpu.SMEM)],
        out_specs=(pl.BlockSpec(memory_space=pltpu.VMEM), pl.BlockSpec(memory_space=pltpu.VMEM)),
        name="bias_tiles",
    )(rel_bias)


def _ffn_kernel(x_ref, gain_ref, sh_ref, sc_ref, g_ref, w1_ref, w3_ref, w2_ref, fin_ref, o_ref,
                h_scr, acc_scr, *, n_chunks, final_norm):
    x = x_ref[...]
    h = _rms_norm(x, gain_ref[...]) * (1.0 + sc_ref[0]) + sh_ref[0]
    h_scr[...] = h.astype(BF16)
    for j in range(n_chunks):
        hb = h_scr[...]
        cols = slice(j * FFN_CHUNK, (j + 1) * FFN_CHUNK)
        u1 = jnp.dot(hb, w1_ref[:, cols], preferred_element_type=F32)
        u3 = jnp.dot(hb, w3_ref[:, cols], preferred_element_type=F32)
        a = (u1 * _sigmoid(u1)) * u3
        part = jnp.dot(a.astype(BF16), w2_ref[j], preferred_element_type=F32)
        if j == 0:
            acc_scr[...] = part
        else:
            acc_scr[...] += part
    out = x + (0.5 * g_ref[0]) * acc_scr[...]
    if final_norm:
        out = _rms_norm(out, fin_ref[...])
    o_ref[...] = out


def _ffn(x, gain, sh, sc, g, w1, w3, w2, fin, *, seq, final_norm):
    t, d = x.shape
    dff = w1.shape[1]
    nch = dff // FFN_CHUNK
    w1c = w1.astype(BF16)
    w3c = w3.astype(BF16)
    w2c = w2.astype(BF16).reshape(nch, FFN_CHUNK, d)
    tm = FFN_TM
    per_b = seq // tm
    mod_spec = pl.BlockSpec((1, 1, d), lambda i: (i // per_b, 0, 0))
    return pl.pallas_call(
        functools.partial(_ffn_kernel, n_chunks=nch, final_norm=final_norm),
        out_shape=jax.ShapeDtypeStruct((t, d), F32),
        grid=(t // tm,),
        in_specs=[pl.BlockSpec((tm, d), lambda i: (i, 0)),
                  _resident((1, d)), mod_spec, mod_spec, mod_spec,
                  _resident((d, dff)), _resident((d, dff)),
                  _resident((nch, FFN_CHUNK, d)), _resident((1, d))],
        out_specs=pl.BlockSpec((tm, d), lambda i: (i, 0)),
        scratch_shapes=[pltpu.VMEM((tm, d), BF16), pltpu.VMEM((tm, d), F32)],
        compiler_params=pltpu.CompilerParams(dimension_semantics=("arbitrary",),
                                             vmem_limit_bytes=VMEM_LIMIT),
        name="ffn_final" if final_norm else "ffn",
    )(x, gain.reshape(1, d), sh, sc, g, w1c, w3c, w2c, fin.reshape(1, d))


_C_QA = 0
_C_CKV = _C_QA + N_HEADS_A * LANES
_C_QI = _C_CKV + D_LATENT
_C_KI = _C_QI + N_HEADS_IDX * LANES
_C_SM = _C_KI + LANES
_C_QK = _C_SM + LANES
_C_V = _C_QK + 2 * W_M
_C_O = _C_V + W_M
_C_GA = _C_O + W_M
_C_GM = _C_GA + 1024
_C_END = _C_GM + 1024
_SM_W = 0
_SM_I = N_HEADS_IDX
_SM_F = N_HEADS_IDX + N_HEADS_M


def _pack_w_in(w_in, d_model):
    splits = (W_A, D_LATENT, N_HEADS_IDX * HEAD_DIM_IDX, HEAD_DIM_IDX, N_HEADS_IDX,
              W_M, W_M, W_M, N_HEADS_M, N_HEADS_M, W_M, d_model, d_model)
    offs = [0]
    for s in splits:
        offs.append(offs[-1] + s)
    (q_a, c_kv, q_i, k_i, w_i, q_m, k_m, v_m, i_p, f_p, o_p, g_a, g_m) = [
        w_in[:, offs[n]:offs[n + 1]] for n in range(len(splits))]
    d = w_in.shape[0]

    def pad_heads(w, nh, hd):
        w = w.reshape(d, nh, hd)
        return jnp.pad(w, ((0, 0), (0, 0), (0, LANES - hd))).reshape(d, nh * LANES)

    small = jnp.concatenate([w_i, i_p, f_p], axis=1)
    small = jnp.pad(small, ((0, 0), (0, LANES - small.shape[1])))
    packed = jnp.concatenate([
        pad_heads(q_a, N_HEADS_A, HEAD_DIM_A), c_kv, pad_heads(q_i, N_HEADS_IDX, HEAD_DIM_IDX),
        jnp.pad(k_i, ((0, 0), (0, LANES - HEAD_DIM_IDX))), small, q_m, k_m, v_m, o_p, g_a, g_m], axis=1)
    assert packed.shape[1] == _C_END
    return packed.astype(BF16)


def _mixin_kernel(x_ref, gain_ref, sh_ref, sc_ref, w_ref, kvn_ref, wuk_ref, cw_ref, cb_ref,
                  qabs_ref, qidx_ref, kidx_ref, ckv_ref, ckvt_ref, wt_ref, ift_ref,
                  qk_ref, v_ref, o_ref, ga_ref, gm_ref, h_scr, tail_scr, *, tm, tiles_per_seq):
    nqb = tm // Q_BLOCK

    @pl.when(pl.program_id(0) % tiles_per_seq == 0)
    def _():
        tail_scr[...] = jnp.zeros(tail_scr.shape, F32)

    x = x_ref[...]
    h = _rms_norm(x, gain_ref[...]) * (1.0 + sc_ref[0]) + sh_ref[0]
    h_scr[...] = h.astype(BF16)

    def proj(lo, hi):
        return jnp.dot(h_scr[...], w_ref[:, lo:hi], preferred_element_type=F32)

    qa = proj(_C_QA, _C_CKV)
    scale = HEAD_DIM_A ** -0.5 * LOG2E
    for hh in range(N_HEADS_A):
        q_h = qa[:, hh * LANES:(hh + 1) * LANES].astype(BF16)
        q_abs = jnp.dot(q_h, wuk_ref[hh], preferred_element_type=F32) * scale
        qabs_ref[:, hh] = q_abs.astype(BF16).reshape(nqb, Q_BLOCK, D_LATENT)
    ckv = _rms_norm(proj(_C_CKV, _C_QI), kvn_ref[...])
    ckv_ref[...] = ckv.astype(BF16)
    ckv_t = ckv.T
    ones_row = jnp.where(lax.broadcasted_iota(I32, (CKVT_ROWS - D_LATENT, KEY_BLOCK), 0) == 0, 1.0, 0.0)
    for j in range(tm // KEY_BLOCK):
        ckvt_ref[j, :D_LATENT] = ckv_t[:, j * KEY_BLOCK:(j + 1) * KEY_BLOCK].astype(BF16)
        ckvt_ref[j, D_LATENT:] = ones_row.astype(BF16)
    qi = proj(_C_QI, _C_KI)
    for hh in range(N_HEADS_IDX):
        qidx_ref[:, hh] = qi[:, hh * LANES:(hh + 1) * LANES].astype(BF16).reshape(nqb, Q_BLOCK, LANES)
    kidx_ref[...] = proj(_C_KI, _C_SM).astype(BF16)
    small_t = proj(_C_SM, _C_QK).T
    wt_ref[...] = small_t[_SM_W:_SM_W + N_HEADS_IDX] * IDX_SCALE
    ift_ref[...] = small_t[_SM_I:_SM_I + 2 * N_HEADS_M]
    xq = proj(_C_QK, _C_V)
    xe = jnp.concatenate([tail_scr[...], xq], axis=0)
    conv = xq * cw_ref[CONV_WIDTH - 1:CONV_WIDTH, :] + cb_ref[...]
    for d in range(1, CONV_WIDTH):
        conv = conv + pltpu.roll(xe, d, axis=0)[8:] * cw_ref[CONV_WIDTH - 1 - d:CONV_WIDTH - d, :]
    tail_scr[...] = xq[tm - 8:]
    qk = conv * _sigmoid(conv)
    qk_ref[:, :W_M] = qk[:, :W_M].astype(BF16)
    qk_ref[:, W_M:] = (qk[:, W_M:] * (HEAD_DIM_M ** -0.5)).astype(BF16)
    v_ref[...] = proj(_C_V, _C_O).astype(BF16)
    o_ref[...] = proj(_C_O, _C_GA)
    ga_ref[...] = proj(_C_GA, _C_GM)
    gm_ref[...] = proj(_C_GM, _C_END)


def _mixin(x, gain, sh, sc, w_packed, kv_norm, wuk_t, conv_w, conv_b, *, seq):
    t, d = x.shape
    tm = MIX_TM
    per_b = seq // tm
    nqb = tm // Q_BLOCK
    row = lambda w: pl.BlockSpec((tm, w), lambda i: (i, 0))
    mod_spec = pl.BlockSpec((1, 1, d), lambda i: (i // per_b, 0, 0))
    out_shape = (
        jax.ShapeDtypeStruct((t // Q_BLOCK, N_HEADS_A, Q_BLOCK, D_LATENT), BF16),
        jax.ShapeDtypeStruct((t // Q_BLOCK, N_HEADS_IDX, Q_BLOCK, LANES), BF16),
        jax.ShapeDtypeStruct((t, LANES), BF16),
        jax.ShapeDtypeStruct((t, D_LATENT), BF16),
        jax.ShapeDtypeStruct((t // KEY_BLOCK, CKVT_ROWS, KEY_BLOCK), BF16),
        jax.ShapeDtypeStruct((N_HEADS_IDX, t), F32),
        jax.ShapeDtypeStruct((2 * N_HEADS_M, t), F32),
        jax.ShapeDtypeStruct((t, 2 * W_M), BF16),
        jax.ShapeDtypeStruct((t, W_M), BF16),
        jax.ShapeDtypeStruct((t, W_M), F32),
        jax.ShapeDtypeStruct((t, d), F32),
        jax.ShapeDtypeStruct((t, d), F32),
    )
    out_specs = (
        pl.BlockSpec((nqb, N_HEADS_A, Q_BLOCK, D_LATENT), lambda i: (i, 0, 0, 0)),
        pl.BlockSpec((nqb, N_HEADS_IDX, Q_BLOCK, LANES), lambda i: (i, 0, 0, 0)),
        row(LANES), row(D_LATENT),
        pl.BlockSpec((tm // KEY_BLOCK, CKVT_ROWS, KEY_BLOCK), lambda i: (i, 0, 0)),
        pl.BlockSpec((N_HEADS_IDX, tm), lambda i: (0, i)),
        pl.BlockSpec((2 * N_HEADS_M, tm), lambda i: (0, i)),
        row(2 * W_M), row(W_M), row(W_M), row(d), row(d),
    )
    return pl.pallas_call(
        functools.partial(_mixin_kernel, tm=tm, tiles_per_seq=per_b),
        out_shape=out_shape,
        grid=(t // tm,),
        in_specs=[pl.BlockSpec((tm, d), lambda i: (i, 0)), _resident((1, d)), mod_spec, mod_spec,
                  _resident((d, _C_END)), _resident((1, D_LATENT)),
                  _resident((N_HEADS_A, LANES, D_LATENT)),
                  _resident((CONV_WIDTH, 2 * W_M)), _resident((1, 2 * W_M))],
        out_specs=out_specs,
        scratch_shapes=[pltpu.VMEM((tm, d), BF16), pltpu.VMEM((8, 2 * W_M), F32)],
        compiler_params=pltpu.CompilerParams(dimension_semantics=("arbitrary",),
                                             vmem_limit_bytes=VMEM_LIMIT),
        name="mixin",
    )(x, gain.reshape(1, d), sh, sc, w_packed, kv_norm.reshape(1, D_LATENT), wuk_t,
      conv_w, conv_b.reshape(1, -1))


def _sortable_key(score):
    bits = pltpu.bitcast(score, I32)
    bits = jnp.where(bits == INT_MIN, 0, bits)
    return jnp.where(bits < 0, bits ^ 0x7FFFFFFF, bits)


def _bit_transpose32(words):
    v = list(words)
    j, m = 16, 0x0000FFFF
    while j:
        k = 0
        while k < 32:
            t = (v[k] ^ lax.shift_right_logical(v[k + j], jnp.int32(j))) & m
            v[k] = v[k] ^ t
            v[k + j] = v[k + j] ^ (t << j)
            k = (k + j + 1) & ~j
        j >>= 1
        m = (m ^ (m << j)) & 0x7FFFFFFF
    return v


def _dsa_kernel(qidx_ref, qabs_ref, wt_ref, kidx_ref, ckv_ref, ckvt_ref, btile_ref, bmax_ref, wuvt_ref,
                out_ref, keys_scr, planes_scr, cand_scr, tau_scr, acc_scr, m_scr, lta_scr, ltc_scr, kmax_scr,
                *, topk, n_qb):
    kb_sz = KEY_BLOCK
    step = pl.program_id(1)
    has_c = step >= 1
    qa = jnp.minimum(step, n_qb - 1)
    qc = jnp.maximum(step - 1, 0)
    slot_a = step & 1
    slot_c = 1 - slot_a
    n_a = qa // (kb_sz // Q_BLOCK) + 1
    n_c = qc // (kb_sz // Q_BLOCK) + 1
    qa0 = qa * Q_BLOCK
    qc0 = qc * Q_BLOCK
    row_id = lax.broadcasted_iota(I32, (kb_sz, LANES), 0)
    lane_id = lax.broadcasted_iota(I32, (kb_sz, LANES), 1)
    n_blocks = keys_scr.shape[1]
    n_groups = N_HEADS_A // 2
    pair = 2 * LANES
    ones8 = jnp.ones((8, D_LATENT), BF16)

    @pl.when(step == 0)
    def _():
        keys_scr[1, 0] = jnp.full((kb_sz, LANES), INT_MIN, I32)
        tau_scr[1] = jnp.zeros((1, LANES), I32)
        planes_scr[...] = jnp.zeros(planes_scr.shape, I32)

        def kn_body(kb, mx):
            c = ckv_ref[kb].astype(F32)
            n2 = lax.dot_general(ones8, (c * c).astype(BF16), (((1,), (1,)), ((), ())),
                                 preferred_element_type=F32)
            return jnp.maximum(mx, n2[0:1])
        mx = lax.fori_loop(0, n_blocks, kn_body, jnp.zeros((1, kb_sz), F32))
        kmax_scr[...] = jnp.max(mx, axis=1, keepdims=True)

    def idx_dot(kb, g):
        q_g = qidx_ref[0, 2 * g:2 * g + 2].reshape(2 * Q_BLOCK, LANES)
        return lax.dot_general(kidx_ref[kb], q_g, (((1,), (1,)), ((), ())), preferred_element_type=F32)

    def logits(kb, g):
        q_g = qabs_ref[0, 2 * g:2 * g + 2].reshape(2 * Q_BLOCK, D_LATENT)
        return lax.dot_general(ckv_ref[kb], q_g, (((1,), (1,)), ((), ())), preferred_element_type=F32)

    def bias_start(kb):
        delta = jnp.minimum(qc0 - kb * kb_sz, BIAS_PAD)
        return pl.multiple_of(BIAS_PAD - delta, LANES)

    qn2 = []
    for g in range(n_groups):
        q_g = qabs_ref[0, 2 * g:2 * g + 2].reshape(2 * Q_BLOCK, D_LATENT).astype(F32)
        qn2.append(lax.dot_general(ones8, (q_g * q_g).astype(BF16), (((1,), (1,)), ((), ())),
                                   preferred_element_type=F32)[0:1])
    bound = jnp.sqrt(jnp.concatenate(qn2, axis=1) * kmax_scr[...]) * 1.02 + bmax_ref[...] + 1e-3
    tau_c = tau_scr[slot_c]
    w_t = wt_ref[...]

    acc_scr[...] = jnp.zeros(acc_scr.shape, F32)
    for g in range(n_groups):
        lta_scr[:, g * pair:(g + 1) * pair] = idx_dot(0, g)
        ltc_scr[:, g * pair:(g + 1) * pair] = logits(0, g)

    def block_step(kb_raw):
        kb = jnp.minimum(kb_raw, n_a - 1)
        kb_next = jnp.minimum(kb_raw + 1, n_a - 1)
        kc = jnp.minimum(kb_raw, n_c - 1)
        kc_next = jnp.minimum(kb_raw + 1, n_c - 1)
        thr = jnp.where(has_c & (kb_raw < n_c), tau_c - 1, jnp.int32(2 ** 31 - 1))
        sel = keys_scr[slot_c, kc] > thr
        ct_blk = ckvt_ref[kc]
        start = bias_start(kc)
        score = jnp.zeros((kb_sz, LANES), F32)
        for g in range(n_groups):
            s_t = lta_scr[:, g * pair:(g + 1) * pair]
            for j in range(2):
                hh = 2 * g + j
                score = score + jnp.maximum(s_t[:, j * LANES:(j + 1) * LANES], 0.0) * w_t[hh:hh + 1, :]
            lta_scr[:, g * pair:(g + 1) * pair] = idx_dot(kb_next, g)
            lt = ltc_scr[:, g * pair:(g + 1) * pair]
            ps = []
            for j in range(2):
                hh = 2 * g + j
                piece = lt[:, j * LANES:(j + 1) * LANES] + btile_ref[hh, pl.ds(start, kb_sz), :]
                ps.append(jnp.exp2(jnp.where(sel, piece, NEG_BIG) - bound[:, hh * LANES:(hh + 1) * LANES]))
            ltc_scr[:, g * pair:(g + 1) * pair] = logits(kc_next, g)
            acc_scr[g] += jnp.dot(ct_blk, jnp.concatenate(ps, axis=1).astype(BF16),
                                  preferred_element_type=F32)
        valid = (kb * kb_sz + row_id) <= (qa0 + lane_id)
        keys = jnp.where(valid, _sortable_key(score), INT_MIN)
        keys_scr[slot_a, kb] = keys
        v = keys ^ INT_MIN
        words = _bit_transpose32([v[8 * i:8 * (i + 1), :] for i in range(32)])
        for bit in range(32):
            planes_scr[bit, kb] = words[31 - bit]

    def block_body(it, carry):
        for u in range(BLOCK_UNROLL):
            block_step(it * BLOCK_UNROLL + u)
        return carry

    lax.fori_loop(0, (n_a + BLOCK_UNROLL - 1) // BLOCK_UNROLL, block_body, 0)
    n_kb = n_a


    blk_id = lax.broadcasted_iota(I32, cand_scr.shape, 0)
    cand_scr[...] = jnp.where(blk_id < n_kb, -1, 0)

    def bit_body(it, carry):
        above, tau_u = carry
        bit = 31 - it
        ones = cand_scr[...] & planes_scr[bit]
        c1 = jnp.sum(jnp.sum(lax.population_count(ones), axis=0), axis=0, keepdims=True)
        take = (above + c1) >= topk
        cand_scr[...] = jnp.where(take, ones, cand_scr[...] ^ ones)
        above = jnp.where(take, above, above + c1)
        tau_u = jnp.where(take, tau_u | (jnp.int32(1) << bit), tau_u)
        return above, tau_u

    zero = jnp.zeros((1, LANES), I32)
    n_gt, tau_u = lax.fori_loop(0, 32, bit_body, (zero, zero))
    tau = tau_u ^ INT_MIN
    n_eq = jnp.sum(jnp.sum(lax.population_count(cand_scr[...]), axis=0), axis=0, keepdims=True)

    need = topk - n_gt
    overflow = n_eq > need
    seq_bits = max(1, (n_blocks * kb_sz - 1).bit_length())

    @pl.when(jnp.max(jnp.where(overflow, 1, 0)) > 0)
    def _():
        def count_ties_before(trial):
            def body(kb, acc):
                hit = jnp.where((keys_scr[slot_a, kb] == tau) & ((kb * kb_sz + row_id) < trial), 1, 0)
                return acc + jnp.sum(hit.reshape(kb_sz // 8, 8, LANES), axis=0)
            acc = lax.fori_loop(0, n_kb, body, jnp.zeros((8, LANES), I32))
            return jnp.sum(acc, axis=0, keepdims=True)

        def idx_body(it, jc):
            trial = jc | (jnp.int32(1) << (seq_bits - 1 - it))
            return jnp.where(count_ties_before(trial) < need, trial, jc)

        j_cut = lax.fori_loop(0, seq_bits, idx_body, jnp.zeros((1, LANES), I32))

        def demote_body(kb, carry):
            k = keys_scr[slot_a, kb]
            drop = overflow & (k == tau) & ((kb * kb_sz + row_id) > j_cut)
            keys_scr[slot_a, kb] = jnp.where(drop, INT_MIN, k)
            return carry

        lax.fori_loop(0, n_kb, demote_body, 0)

    tau_scr[slot_a] = jnp.maximum(tau, INT_MIN + 1)

    l_min = jnp.min(jnp.concatenate([acc_scr[g, D_LATENT:D_LATENT + 1, :] for g in range(n_groups)], axis=1))

    @pl.when(has_c & jnp.logical_not(l_min >= 2.0 ** -80))
    def _():
        m_scr[...] = jnp.full(m_scr.shape, NEG_BIG, F32)
        acc_scr[...] = jnp.zeros(acc_scr.shape, F32)

        def exact_body(kb, carry):
            sel = keys_scr[slot_c, kb] >= tau_c
            ct_blk = ckvt_ref[kb]
            start = bias_start(kb)
            for g in range(n_groups):
                lt = logits(kb, g)
                ps, alphas = [], []
                for j in range(2):
                    hh = 2 * g + j
                    sl = slice(hh * LANES, (hh + 1) * LANES)
                    piece = lt[:, j * LANES:(j + 1) * LANES] + btile_ref[hh, pl.ds(start, kb_sz), :]
                    masked = jnp.where(sel, piece, NEG_BIG)
                    m_old = m_scr[:, sl]
                    m_new = jnp.maximum(m_old, jnp.max(masked, axis=0, keepdims=True))
                    m_scr[:, sl] = m_new
                    alphas.append(jnp.exp2(m_old - m_new))
                    ps.append(jnp.exp2(masked - m_new))
                pv = jnp.dot(ct_blk, jnp.concatenate(ps, axis=1).astype(BF16), preferred_element_type=F32)
                acc_scr[g] = jnp.concatenate(alphas, axis=1) * acc_scr[g] + pv
            return carry

        lax.fori_loop(0, n_c, exact_body, 0)

    @pl.when(has_c)
    def _():
        ys = []
        for hh in range(N_HEADS_A):
            acc_h = acc_scr[hh // 2, :, (hh % 2) * LANES:(hh % 2 + 1) * LANES]
            o_h = acc_h[:D_LATENT] * (1.0 / acc_h[D_LATENT:D_LATENT + 1])
            ys.append(jnp.dot(wuvt_ref[hh], o_h.astype(BF16), preferred_element_type=F32))
        y_t = jnp.concatenate(ys, axis=0)
        out_ref[...] = y_t.T.astype(BF16)


def _dsa(q_idx, q_abs, w_t, k_idx, ckv, ckv_t, btile, bmax, wuv_t, *, batch, seq):
    t = batch * seq
    nqb = seq // Q_BLOCK
    nkb = seq // KEY_BLOCK
    topk = min(TOPK_MAX, seq // 4)
    k_idx3 = k_idx.reshape(t // KEY_BLOCK, KEY_BLOCK, LANES)
    ckv3 = ckv.reshape(t // KEY_BLOCK, KEY_BLOCK, D_LATENT)
    per_batch = lambda shape: pl.BlockSpec(shape, lambda b, q: (b,) + (0,) * (len(shape) - 1),
                                           pipeline_mode=pl.Buffered(1))
    scored = lambda b, s: b * nqb + jnp.minimum(s, nqb - 1)
    attended = lambda b, s: b * nqb + jnp.maximum(s - 1, 0)
    return pl.pallas_call(
        functools.partial(_dsa_kernel, topk=topk, n_qb=nqb),
        out_shape=jax.ShapeDtypeStruct((t, W_A), BF16),
        grid=(batch, nqb + 1),
        in_specs=[pl.BlockSpec((1, N_HEADS_IDX, Q_BLOCK, LANES), lambda b, s: (scored(b, s), 0, 0, 0)),
                  pl.BlockSpec((1, N_HEADS_A, Q_BLOCK, D_LATENT), lambda b, s: (attended(b, s), 0, 0, 0)),
                  pl.BlockSpec((N_HEADS_IDX, Q_BLOCK), lambda b, s: (0, scored(b, s))),
                  per_batch((nkb, KEY_BLOCK, LANES)),
                  per_batch((nkb, KEY_BLOCK, D_LATENT)),
                  per_batch((nkb, CKVT_ROWS, KEY_BLOCK)),
                  _resident((N_HEADS_A, BIAS_ROWS, LANES)),
                  _resident((1, N_HEADS_A * LANES)),
                  _resident((N_HEADS_A, HEAD_DIM_A, D_LATENT))],
        out_specs=pl.BlockSpec((Q_BLOCK, W_A), lambda b, s: (attended(b, s), 0)),
        scratch_shapes=[pltpu.VMEM((2, nkb, KEY_BLOCK, LANES), I32),
                        pltpu.VMEM((32, nkb, 8, LANES), I32),
                        pltpu.VMEM((nkb, 8, LANES), I32),
                        pltpu.VMEM((2, 1, LANES), I32),
                        pltpu.VMEM((N_HEADS_A // 2, CKVT_ROWS, 2 * LANES), F32),
                        pltpu.VMEM((1, N_HEADS_A * LANES), F32),
                        pltpu.VMEM((KEY_BLOCK, N_HEADS_A * LANES), F32),
                        pltpu.VMEM((KEY_BLOCK, N_HEADS_A * LANES), F32),
                        pltpu.VMEM((1, 1), F32)],
        compiler_params=pltpu.CompilerParams(dimension_semantics=("arbitrary", "arbitrary"),
                                             vmem_limit_bytes=VMEM_LIMIT),
        name="dsa",
    )(q_idx, q_abs, w_t, k_idx3, ckv3, ckv_t, btile, bmax, wuv_t)


def _mlstm_kernel(qk_ref, v_ref, o_ref, ift_ref, gbt_ref, hn_ref,
                  out_ref, c_scr, n_scr, m_scr, *, chunk):
    L = chunk
    step = pl.program_id(1)

    @pl.when(step == 0)
    def _():
        c_scr[...] = jnp.zeros(c_scr.shape, F32)
        n_scr[...] = jnp.zeros(n_scr.shape, F32)
        m_scr[...] = jnp.zeros(m_scr.shape, F32)

    a_t = ift_ref[...] + gbt_ref[...]
    ls_t = _log_sigmoid(a_t)
    rr = lax.broadcasted_iota(I32, (L, L), 0)
    cc = lax.broadcasted_iota(I32, (L, L), 1)
    causal = cc <= rr
    triu = jnp.where(rr <= cc, 1.0, 0.0).astype(BF16)
    b_t = sum(jnp.dot(piece, triu, preferred_element_type=F32) for piece in _split3(ls_t))
    gates_t = jnp.concatenate([a_t[:N_HEADS_M], b_t[N_HEADS_M:],
                               jnp.zeros((LANES - 2 * N_HEADS_M, L), F32)], axis=0)
    gates_c = gates_t.T

    o_gate = _sigmoid(o_ref[...])
    for hh in range(N_HEADS_M):
        hs = slice(hh * HEAD_DIM_M, (hh + 1) * HEAD_DIM_M)
        qb16 = qk_ref[:, hs]
        kb16 = qk_ref[:, W_M + hh * HEAD_DIM_M:W_M + (hh + 1) * HEAD_DIM_M]
        q = qb16.astype(F32)
        k = kb16.astype(F32)
        v = v_ref[:, hs]
        i_c = gates_c[:, hh:hh + 1]
        bc = gates_c[:, N_HEADS_M + hh:N_HEADS_M + hh + 1]
        i_t = a_t[hh:hh + 1, :]
        bt = b_t[N_HEADS_M + hh:N_HEADS_M + hh + 1, :]
        b_last = bc[L - 1:L, :]
        m_prev = m_scr[hh]
        c_prev = c_scr[hh]
        n_prev = n_scr[hh]

        d_log = jnp.where(causal, bc - bt + i_t, NEG_BIG)
        inter_log = bc + m_prev
        m_j = jnp.maximum(inter_log, jnp.max(d_log, axis=1, keepdims=True))
        s = lax.dot_general(qb16, kb16, (((1,), (1,)), ((), ())),
                            preferred_element_type=F32) * jnp.exp(d_log - m_j)
        w_inter = jnp.exp(inter_log - m_j)
        num = (w_inter * jnp.dot(qb16, c_prev.astype(BF16), preferred_element_type=F32)
               + jnp.dot(s.astype(BF16), v, preferred_element_type=F32))
        den = w_inter * jnp.sum(q * n_prev, axis=1, keepdims=True) + jnp.sum(s, axis=1, keepdims=True)
        hval = num / jnp.maximum(jnp.abs(den), jnp.exp(-m_j))

        g_t = b_last - bt + i_t
        g_c = b_last - bc + i_c
        m_new = jnp.maximum(b_last + m_prev, jnp.max(g_t, axis=1, keepdims=True))
        decay = jnp.exp(b_last + m_prev - m_new)
        kw = k * jnp.exp(g_c - m_new)
        c_scr[hh] = decay * c_prev + jnp.dot(kw.T.astype(BF16), v, preferred_element_type=F32)
        n_scr[hh] = decay * n_prev + jnp.sum(kw, axis=0, keepdims=True)
        m_scr[hh] = m_new

        mu = jnp.mean(hval, axis=1, keepdims=True)
        cen = hval - mu
        var = jnp.mean(cen * cen, axis=1, keepdims=True)
        hn = cen * lax.rsqrt(var + EPS) * hn_ref[:, hs]
        out_ref[:, hs] = (hn * o_gate[:, hs]).astype(BF16)


def _mlstm(qk, v, o_pre, ift, gate_bias, head_norm, *, batch, seq):
    t = batch * seq
    L = MLSTM_CHUNK
    nc = seq // L
    gbt = jnp.broadcast_to(gate_bias.reshape(2 * N_HEADS_M, 1), (2 * N_HEADS_M, L))
    row = lambda w: pl.BlockSpec((L, w), lambda b, c: (b * nc + c, 0))
    return pl.pallas_call(
        functools.partial(_mlstm_kernel, chunk=L),
        out_shape=jax.ShapeDtypeStruct((t, W_M), BF16),
        grid=(batch, nc),
        in_specs=[row(2 * W_M), row(W_M), row(W_M),
                  pl.BlockSpec((2 * N_HEADS_M, L), lambda b, c: (0, b * nc + c)),
                  _resident((2 * N_HEADS_M, L)), _resident((1, W_M))],
        out_specs=row(W_M),
        scratch_shapes=[pltpu.VMEM((N_HEADS_M, HEAD_DIM_M, HEAD_DIM_M), F32),
                        pltpu.VMEM((N_HEADS_M, 1, HEAD_DIM_M), F32),
                        pltpu.VMEM((N_HEADS_M, 1, 1), F32)],
        compiler_params=pltpu.CompilerParams(dimension_semantics=("arbitrary", "arbitrary"),
                                             vmem_limit_bytes=VMEM_LIMIT),
        name="mlstm",
    )(qk, v, o_pre, ift, gbt, head_norm.reshape(1, -1))


def _merge_kernel(x_ref, ya_ref, hm_ref, ga_ref, gm_ref, g_ref, wa_ref, wm_ref, wo_ref, o_ref):
    pa = jnp.dot(ya_ref[...], wa_ref[...], preferred_element_type=F32)
    pm = jnp.dot(hm_ref[...], wm_ref[...], preferred_element_type=F32)
    merged = _sigmoid(ga_ref[...]) * pa + _sigmoid(gm_ref[...]) * pm
    out = jnp.dot(merged.astype(BF16), wo_ref[...], preferred_element_type=F32)
    o_ref[...] = x_ref[...] + g_ref[0] * out


def _merge(x, y_a, h_m, gate_a, gate_m, g, w_a, w_m, w_o, *, seq):
    t, d = x.shape
    tm = MERGE_TM
    per_b = seq // tm
    row = lambda w: pl.BlockSpec((tm, w), lambda i: (i, 0))
    return pl.pallas_call(
        _merge_kernel,
        out_shape=jax.ShapeDtypeStruct((t, d), F32),
        grid=(t // tm,),
        in_specs=[row(d), row(W_A), row(W_M), row(d), row(d),
                  pl.BlockSpec((1, 1, d), lambda i: (i // per_b, 0, 0)),
                  _resident((W_A, d)), _resident((W_M, d)), _resident((d, d))],
        out_specs=row(d),
        compiler_params=pltpu.CompilerParams(dimension_semantics=("arbitrary",),
                                             vmem_limit_bytes=VMEM_LIMIT),
        name="merge",
    )(x, y_a, h_m, gate_a, gate_m, g, w_a.astype(BF16), w_m.astype(BF16), w_o.astype(BF16))


def kernel(x, c, ada_w, ada_b, ffn1_norm, ffn1_w1, ffn1_w3, ffn1_w2, mix_norm, w_in, conv_w, conv_b,
           kv_norm, w_uk, w_uv, mlstm_gate_bias, mlstm_head_norm, rel_bias, w_branch_attn,
           w_branch_mlstm, w_out, ffn2_norm, ffn2_w1, ffn2_w3, ffn2_w2, final_norm):
    batch, seq, d = x.shape
    depth = ada_w.shape[0]
    assert seq % max(FFN_TM, MIX_TM, MERGE_TM, MLSTM_CHUNK, KEY_BLOCK) == 0
    t = batch * seq
    xf = x.reshape(t, d)
    btile, bmax = _bias_tiles(rel_bias)
    for l in range(depth):
        mod = _adaln(c, ada_w[l], ada_b[l]).reshape(batch, 9, 1, d)
        sh1, sc1, g1, sh2, sc2, g2, sh3, sc3, g3 = [mod[:, n] for n in range(9)]
        xf = _ffn(xf, ffn1_norm[l], sh1, sc1, g1, ffn1_w1[l], ffn1_w3[l], ffn1_w2[l], final_norm,
                  seq=seq, final_norm=False)
        wuk_t = jnp.pad(w_uk[l].transpose(0, 2, 1), ((0, 0), (0, LANES - HEAD_DIM_A), (0, 0))).astype(BF16)
        (q_abs, q_idx, k_idx, ckv, ckv_t, w_t, ift, qk_m, v_m, o_pre, gate_a, gate_m) = _mixin(
            xf, mix_norm[l], sh2, sc2, _pack_w_in(w_in[l], d), kv_norm[l], wuk_t, conv_w[l], conv_b[l],
            seq=seq)
        wuv_t = w_uv[l].transpose(0, 2, 1).astype(BF16)
        y_a = _dsa(q_idx, q_abs, w_t, k_idx, ckv, ckv_t, btile, bmax, wuv_t, batch=batch, seq=seq)
        h_m = _mlstm(qk_m, v_m, o_pre, ift, mlstm_gate_bias[l], mlstm_head_norm[l], batch=batch, seq=seq)
        xf = _merge(xf, y_a, h_m, gate_a, gate_m, g2, w_branch_attn[l], w_branch_mlstm[l], w_out[l],
                    seq=seq)
        xf = _ffn(xf, ffn2_norm[l], sh3, sc3, g3, ffn2_w1[l], ffn2_w3[l], ffn2_w2[l], final_norm,
                  seq=seq, final_norm=(l == depth - 1))
    return xf.reshape(batch, seq, d)
```

```python
import functools
import math

import jax
import jax.numpy as jnp
from jax import lax
from jax.experimental import pallas as pl
from jax.experimental.pallas import tpu as pltpu

F32 = jnp.float32
BF16 = jnp.bfloat16
I32 = jnp.int32

LANES = 128
MXU_DIM = 256
VMEM_LIMIT = 56 * 1024 * 1024

N_HEADS_A = 8
HEAD_DIM_A = 64
D_LATENT = 256
N_HEADS_IDX = 8
HEAD_DIM_IDX = 64
TOPK_MAX = 256
Q_BLOCK = 128
N_BUCKETS = 32
MAX_DISTANCE = 128
N_HEADS_M = 4
HEAD_DIM_M = 128
CONV_WIDTH = 4
EPS = 1e-6
IDX_SCALE = (N_HEADS_IDX ** -0.5) * (HEAD_DIM_IDX ** -0.5)
W_A = N_HEADS_A * HEAD_DIM_A
W_M = N_HEADS_M * HEAD_DIM_M

FFN_TM = 512
FFN_CHUNK = 256
MIX_TM = 512
KEY_BLOCK = 256
BLOCK_UNROLL = 2
MLSTM_CHUNK = 256
MERGE_TM = 512
NEG_BIG = -1e30
INT_MIN = -2 ** 31

BIAS_PAD = 384
BIAS_ROWS = KEY_BLOCK + BIAS_PAD
CKVT_ROWS = D_LATENT + 16
LOG2E = math.log2(math.e)


def _sigmoid(x):
    return 1.0 / (1.0 + jnp.exp(-x))


def _log_sigmoid(x):
    return jnp.minimum(x, 0.0) - jnp.log(1.0 + jnp.exp(-jnp.abs(x)))


def _rms_norm(x, gain):
    ms = jnp.mean(x * x, axis=-1, keepdims=True)
    return x * lax.rsqrt(ms + EPS) * gain


def _split3(x):
    hi = x.astype(BF16)
    r1 = x - hi.astype(F32)
    mid = r1.astype(BF16)
    lo = (r1 - mid.astype(F32)).astype(BF16)
    return hi, mid, lo


def _resident(shape):
    nd = len(shape)
    return pl.BlockSpec(shape, lambda *_: (0,) * nd, pipeline_mode=pl.Buffered(1))


def _adaln_kernel(c_ref, w_ref, b_ref, o_ref):
    c = c_ref[...]
    cond = c * _sigmoid(c)
    o_ref[...] = jnp.dot(cond.astype(BF16), w_ref[...].astype(BF16),
                         preferred_element_type=F32) + b_ref[...]


def _adaln(c, ada_w, ada_b):
    b, d = c.shape
    n = ada_w.shape[1]
    rows = 8
    c_pad = jnp.zeros((rows, d), F32).at[:b].set(c)
    tn = 1024
    out = pl.pallas_call(
        _adaln_kernel,
        out_shape=jax.ShapeDtypeStruct((rows, n), F32),
        grid=(n // tn,),
        in_specs=[pl.BlockSpec((rows, d), lambda j: (0, 0)),
                  pl.BlockSpec((d, tn), lambda j: (0, j)),
                  pl.BlockSpec((1, tn), lambda j: (0, j))],
        out_specs=pl.BlockSpec((rows, tn), lambda j: (0, j)),
        compiler_params=pltpu.CompilerParams(dimension_semantics=("arbitrary",),
                                             vmem_limit_bytes=VMEM_LIMIT),
        name="adaln",
    )(c_pad, ada_w, ada_b.reshape(1, n))
    return out[:b]


def _t5_bucket(dist):
    n = jnp.maximum(dist, 0)
    max_exact = N_BUCKETS // 2
    nf = jnp.maximum(n, 1).astype(F32)
    large = max_exact + (jnp.log(nf / max_exact) / math.log(MAX_DISTANCE / max_exact)
                         * (N_BUCKETS - max_exact)).astype(I32)
    large = jnp.minimum(large, N_BUCKETS - 1)
    return jnp.where(n < max_exact, n, large)


def _bias_kernel(rel_ref, tile_ref, max_ref):
    r = lax.broadcasted_iota(I32, (BIAS_ROWS, LANES), 0)
    i = lax.broadcasted_iota(I32, (BIAS_ROWS, LANES), 1)
    bucket = _t5_bucket(i - r + BIAS_PAD)
    for h in range(N_HEADS_A):
        acc = jnp.zeros((BIAS_ROWS, LANES), F32)
        top = rel_ref[0, h] * LOG2E
        for bkt in range(N_BUCKETS):
            val = rel_ref[bkt, h] * LOG2E
            acc = jnp.where(bucket == bkt, val, acc)
            top = jnp.maximum(top, val)
        tile_ref[h] = acc
        max_ref[:, h * LANES:(h + 1) * LANES] = jnp.full((1, LANES), top, F32)


def _bias_tiles(rel_bias):
    return pl.pallas_call(
        _bias_kernel,
        out_shape=(jax.ShapeDtypeStruct((N_HEADS_A, BIAS_ROWS, LANES), F32),
                   jax.ShapeDtypeStruct((1, N_HEADS_A * LANES), F32)),
        in_specs=[pl.BlockSpec(memory_space=pltpu.SMEM)],
        out_specs=(pl.BlockSpec(memory_space=pltpu.VMEM), pl.BlockSpec(memory_space=pltpu.VMEM)),
        name="bias_tiles",
    )(rel_bias)


def _ffn_kernel(x_ref, gain_ref, sh_ref, sc_ref, g_ref, w1_ref, w3_ref, w2_ref, fin_ref, o_ref,
                h_scr, acc_scr, *, n_chunks, final_norm):
    x = x_ref[...]
    h = _rms_norm(x, gain_ref[...]) * (1.0 + sc_ref[0]) + sh_ref[0]
    h_scr[...] = h.astype(BF16)
    for j in range(n_chunks):
        hb = h_scr[...]
        cols = slice(j * FFN_CHUNK, (j + 1) * FFN_CHUNK)
        u1 = jnp.dot(hb, w1_ref[:, cols], preferred_element_type=F32)
        u3 = jnp.dot(hb, w3_ref[:, cols], preferred_element_type=F32)
        a = (u1 * _sigmoid(u1)) * u3
        part = jnp.dot(a.astype(BF16), w2_ref[j], preferred_element_type=F32)
        if j == 0:
            acc_scr[...] = part
        else:
            acc_scr[...] += part
    out = x + (0.5 * g_ref[0]) * acc_scr[...]
    if final_norm:
        out = _rms_norm(out, fin_ref[...])
    o_ref[...] = out


def _ffn(x, gain, sh, sc, g, w1, w3, w2, fin, *, seq, final_norm):
    t, d = x.shape
    dff = w1.shape[1]
    nch = dff // FFN_CHUNK
    w1c = w1.astype(BF16)
    w3c = w3.astype(BF16)
    w2c = w2.astype(BF16).reshape(nch, FFN_CHUNK, d)
    tm = FFN_TM
    per_b = seq // tm
    mod_spec = pl.BlockSpec((1, 1, d), lambda i: (i // per_b, 0, 0))
    return pl.pallas_call(
        functools.partial(_ffn_kernel, n_chunks=nch, final_norm=final_norm),
        out_shape=jax.ShapeDtypeStruct((t, d), F32),
        grid=(t // tm,),
        in_specs=[pl.BlockSpec((tm, d), lambda i: (i, 0)),
                  _resident((1, d)), mod_spec, mod_spec, mod_spec,
                  _resident((d, dff)), _resident((d, dff)),
                  _resident((nch, FFN_CHUNK, d)), _resident((1, d))],
        out_specs=pl.BlockSpec((tm, d), lambda i: (i, 0)),
        scratch_shapes=[pltpu.VMEM((tm, d), BF16), pltpu.VMEM((tm, d), F32)],
        compiler_params=pltpu.CompilerParams(dimension_semantics=("arbitrary",),
                                             vmem_limit_bytes=VMEM_LIMIT),
        name="ffn_final" if final_norm else "ffn",
    )(x, gain.reshape(1, d), sh, sc, g, w1c, w3c, w2c, fin.reshape(1, d))


_C_QA = 0
_C_CKV = _C_QA + N_HEADS_A * LANES
_C_QI = _C_CKV + D_LATENT
_C_KI = _C_QI + N_HEADS_IDX * LANES
_C_SM = _C_KI + LANES
_C_QK = _C_SM + LANES
_C_V = _C_QK + 2 * W_M
_C_O = _C_V + W_M
_C_GA = _C_O + W_M
_C_GM = _C_GA + 1024
_C_END = _C_GM + 1024
_SM_W = 0
_SM_I = N_HEADS_IDX
_SM_F = N_HEADS_IDX + N_HEADS_M


def _pack_w_in(w_in, d_model):
    splits = (W_A, D_LATENT, N_HEADS_IDX * HEAD_DIM_IDX, HEAD_DIM_IDX, N_HEADS_IDX,
              W_M, W_M, W_M, N_HEADS_M, N_HEADS_M, W_M, d_model, d_model)
    offs = [0]
    for s in splits:
        offs.append(offs[-1] + s)
    (q_a, c_kv, q_i, k_i, w_i, q_m, k_m, v_m, i_p, f_p, o_p, g_a, g_m) = [
        w_in[:, offs[n]:offs[n + 1]] for n in range(len(splits))]
    d = w_in.shape[0]

    def pad_heads(w, nh, hd):
        w = w.reshape(d, nh, hd)
        return jnp.pad(w, ((0, 0), (0, 0), (0, LANES - hd))).reshape(d, nh * LANES)

    small = jnp.concatenate([w_i, i_p, f_p], axis=1)
    small = jnp.pad(small, ((0, 0), (0, LANES - small.shape[1])))
    packed = jnp.concatenate([
        pad_heads(q_a, N_HEADS_A, HEAD_DIM_A), c_kv, pad_heads(q_i, N_HEADS_IDX, HEAD_DIM_IDX),
        jnp.pad(k_i, ((0, 0), (0, LANES - HEAD_DIM_IDX))), small, q_m, k_m, v_m, o_p, g_a, g_m], axis=1)
    assert packed.shape[1] == _C_END
    return packed.astype(BF16)


def _mixin_kernel(x_ref, gain_ref, sh_ref, sc_ref, w_ref, kvn_ref, wuk_ref, cw_ref, cb_ref,
                  qabs_ref, qidx_ref, kidx_ref, ckv_ref, ckvt_ref, wt_ref, ift_ref,
                  qk_ref, v_ref, o_ref, ga_ref, gm_ref, h_scr, xe_scr, *, tm, tiles_per_seq):
    nqb = tm // Q_BLOCK

    @pl.when(pl.program_id(0) % tiles_per_seq == 0)
    def _():
        xe_scr[:8] = jnp.zeros((8, xe_scr.shape[1]), F32)

    x = x_ref[...]
    h = _rms_norm(x, gain_ref[...]) * (1.0 + sc_ref[0]) + sh_ref[0]
    h_scr[...] = h.astype(BF16)

    def proj(lo, hi):
        return jnp.dot(h_scr[...], w_ref[:, lo:hi], preferred_element_type=F32)

    qa = proj(_C_QA, _C_CKV)
    scale = HEAD_DIM_A ** -0.5 * LOG2E
    for hh in range(N_HEADS_A):
        q_h = qa[:, hh * LANES:(hh + 1) * LANES].astype(BF16)
        q_abs = jnp.dot(q_h, wuk_ref[hh], preferred_element_type=F32) * scale
        qabs_ref[:, hh] = q_abs.astype(BF16).reshape(nqb, Q_BLOCK, D_LATENT)
    ckv = _rms_norm(proj(_C_CKV, _C_QI), kvn_ref[...])
    ckv_ref[...] = ckv.astype(BF16)
    ckv_t = ckv.T
    ones_row = jnp.where(lax.broadcasted_iota(I32, (CKVT_ROWS - D_LATENT, KEY_BLOCK), 0) == 0, 1.0, 0.0)
    for j in range(tm // KEY_BLOCK):
        ckvt_ref[j, :D_LATENT] = ckv_t[:, j * KEY_BLOCK:(j + 1) * KEY_BLOCK].astype(BF16)
        ckvt_ref[j, D_LATENT:] = ones_row.astype(BF16)
    qi = proj(_C_QI, _C_KI)
    for hh in range(N_HEADS_IDX):
        qidx_ref[:, hh] = qi[:, hh * LANES:(hh + 1) * LANES].astype(BF16).reshape(nqb, Q_BLOCK, LANES)
    kidx_ref[...] = proj(_C_KI, _C_SM).astype(BF16)
    small_t = proj(_C_SM, _C_QK).T
    wt_ref[...] = small_t[_SM_W:_SM_W + N_HEADS_IDX] * IDX_SCALE
    ift_ref[...] = small_t[_SM_I:_SM_I + 2 * N_HEADS_M]
    xe_scr[8:] = proj(_C_QK, _C_V)
    xe = xe_scr[...]
    xq = xe[8:]
    conv = xq * cw_ref[CONV_WIDTH - 1:CONV_WIDTH, :] + cb_ref[...]
    for d in range(1, CONV_WIDTH):
        conv = conv + pltpu.roll(xe, d, axis=0)[8:] * cw_ref[CONV_WIDTH - 1 - d:CONV_WIDTH - d, :]
    xe_scr[:8] = xe_scr[tm:]
    qk = conv * _sigmoid(conv)
    qk_ref[:, :W_M] = qk[:, :W_M].astype(BF16)
    qk_ref[:, W_M:] = (qk[:, W_M:] * (HEAD_DIM_M ** -0.5)).astype(BF16)
    v_ref[...] = proj(_C_V, _C_O).astype(BF16)
    o_ref[...] = proj(_C_O, _C_GA)
    ga_ref[...] = proj(_C_GA, _C_GM)
    gm_ref[...] = proj(_C_GM, _C_END)


def _mixin(x, gain, sh, sc, w_packed, kv_norm, wuk_t, conv_w, conv_b, *, seq):
    t, d = x.shape
    tm = MIX_TM
    per_b = seq // tm
    nqb = tm // Q_BLOCK
    row = lambda w: pl.BlockSpec((tm, w), lambda i: (i, 0))
    mod_spec = pl.BlockSpec((1, 1, d), lambda i: (i // per_b, 0, 0))
    out_shape = (
        jax.ShapeDtypeStruct((t // Q_BLOCK, N_HEADS_A, Q_BLOCK, D_LATENT), BF16),
        jax.ShapeDtypeStruct((t // Q_BLOCK, N_HEADS_IDX, Q_BLOCK, LANES), BF16),
        jax.ShapeDtypeStruct((t, LANES), BF16),
        jax.ShapeDtypeStruct((t, D_LATENT), BF16),
        jax.ShapeDtypeStruct((t // KEY_BLOCK, CKVT_ROWS, KEY_BLOCK), BF16),
        jax.ShapeDtypeStruct((N_HEADS_IDX, t), F32),
        jax.ShapeDtypeStruct((2 * N_HEADS_M, t), F32),
        jax.ShapeDtypeStruct((t, 2 * W_M), BF16),
        jax.ShapeDtypeStruct((t, W_M), BF16),
        jax.ShapeDtypeStruct((t, W_M), F32),
        jax.ShapeDtypeStruct((t, d), F32),
        jax.ShapeDtypeStruct((t, d), F32),
    )
    out_specs = (
        pl.BlockSpec((nqb, N_HEADS_A, Q_BLOCK, D_LATENT), lambda i: (i, 0, 0, 0)),
        pl.BlockSpec((nqb, N_HEADS_IDX, Q_BLOCK, LANES), lambda i: (i, 0, 0, 0)),
        row(LANES), row(D_LATENT),
        pl.BlockSpec((tm // KEY_BLOCK, CKVT_ROWS, KEY_BLOCK), lambda i: (i, 0, 0)),
        pl.BlockSpec((N_HEADS_IDX, tm), lambda i: (0, i)),
        pl.BlockSpec((2 * N_HEADS_M, tm), lambda i: (0, i)),
        row(2 * W_M), row(W_M), row(W_M), row(d), row(d),
    )
    return pl.pallas_call(
        functools.partial(_mixin_kernel, tm=tm, tiles_per_seq=per_b),
        out_shape=out_shape,
        grid=(t // tm,),
        in_specs=[pl.BlockSpec((tm, d), lambda i: (i, 0)), _resident((1, d)), mod_spec, mod_spec,
                  _resident((d, _C_END)), _resident((1, D_LATENT)),
                  _resident((N_HEADS_A, LANES, D_LATENT)),
                  _resident((CONV_WIDTH, 2 * W_M)), _resident((1, 2 * W_M))],
        out_specs=out_specs,
        scratch_shapes=[pltpu.VMEM((tm, d), BF16), pltpu.VMEM((tm + 8, 2 * W_M), F32)],
        compiler_params=pltpu.CompilerParams(dimension_semantics=("arbitrary",),
                                             vmem_limit_bytes=VMEM_LIMIT),
        name="mixin",
    )(x, gain.reshape(1, d), sh, sc, w_packed, kv_norm.reshape(1, D_LATENT), wuk_t,
      conv_w, conv_b.reshape(1, -1))


def _sortable_key(score):
    bits = pltpu.bitcast(score, I32)
    bits = jnp.where(bits == INT_MIN, 0, bits)
    return jnp.where(bits < 0, bits ^ 0x7FFFFFFF, bits)


def _bit_transpose32(words):
    v = list(words)
    j, m = 16, 0x0000FFFF
    while j:
        k = 0
        while k < 32:
            t = (v[k] ^ lax.shift_right_logical(v[k + j], jnp.int32(j))) & m
            v[k] = v[k] ^ t
            v[k + j] = v[k + j] ^ (t << j)
            k = (k + j + 1) & ~j
        j >>= 1
        m = (m ^ (m << j)) & 0x7FFFFFFF
    return v


def _dsa_kernel(qidx_ref, qabs_ref, wt_ref, kidx_ref, ckv_ref, ckvt_ref, btile_ref, bmax_ref, wuvt_ref,
                out_ref, keys_scr, planes_scr, cand_scr, tau_scr, acc_scr, m_scr, lta_scr, ltc_scr, kmax_scr,
                *, topk, n_qb):
    kb_sz = KEY_BLOCK
    step = pl.program_id(1)
    has_c = step >= 1
    qa = jnp.minimum(step, n_qb - 1)
    qc = jnp.maximum(step - 1, 0)
    slot_a = step & 1
    slot_c = 1 - slot_a
    n_a = qa // (kb_sz // Q_BLOCK) + 1
    n_c = qc // (kb_sz // Q_BLOCK) + 1
    qa0 = qa * Q_BLOCK
    qc0 = qc * Q_BLOCK
    row_id = lax.broadcasted_iota(I32, (kb_sz, LANES), 0)
    lane_id = lax.broadcasted_iota(I32, (kb_sz, LANES), 1)
    n_blocks = keys_scr.shape[1]
    n_groups = N_HEADS_A // 2
    pair = 2 * LANES
    ones8 = jnp.ones((8, D_LATENT), BF16)

    @pl.when(step == 0)
    def _():
        keys_scr[1, 0] = jnp.full((kb_sz, LANES), INT_MIN, I32)
        tau_scr[1] = jnp.zeros((1, LANES), I32)
        planes_scr[...] = jnp.zeros(planes_scr.shape, I32)

        def kn_body(kb, mx):
            c = ckv_ref[kb].astype(F32)
            n2 = lax.dot_general(ones8, (c * c).astype(BF16), (((1,), (1,)), ((), ())),
                                 preferred_element_type=F32)
            return jnp.maximum(mx, n2[0:1])
        mx = lax.fori_loop(0, n_blocks, kn_body, jnp.zeros((1, kb_sz), F32))
        kmax_scr[...] = jnp.max(mx, axis=1, keepdims=True)

    def idx_dot(kb, g):
        q_g = qidx_ref[0, 2 * g:2 * g + 2].reshape(2 * Q_BLOCK, LANES)
        return lax.dot_general(kidx_ref[kb], q_g, (((1,), (1,)), ((), ())), preferred_element_type=F32)

    def logits(kb, g):
        q_g = qabs_ref[0, 2 * g:2 * g + 2].reshape(2 * Q_BLOCK, D_LATENT)
        return lax.dot_general(ckv_ref[kb], q_g, (((1,), (1,)), ((), ())), preferred_element_type=F32)

    def bias_start(kb):
        delta = jnp.minimum(qc0 - kb * kb_sz, BIAS_PAD)
        return pl.multiple_of(BIAS_PAD - delta, LANES)

    qn2 = []
    for g in range(n_groups):
        q_g = qabs_ref[0, 2 * g:2 * g + 2].reshape(2 * Q_BLOCK, D_LATENT).astype(F32)
        qn2.append(lax.dot_general(ones8, (q_g * q_g).astype(BF16), (((1,), (1,)), ((), ())),
                                   preferred_element_type=F32)[0:1])
    bound = jnp.sqrt(jnp.concatenate(qn2, axis=1) * kmax_scr[...]) * 1.02 + bmax_ref[...] + 1e-3
    tau_c = tau_scr[slot_c]
    w_t = wt_ref[...]

    acc_scr[...] = jnp.zeros(acc_scr.shape, F32)
    for g in range(n_groups):
        lta_scr[:, g * pair:(g + 1) * pair] = idx_dot(0, g)
        ltc_scr[:, g * pair:(g + 1) * pair] = logits(0, g)

    bound_far = bound - jnp.concatenate([btile_ref[hh, 0:1, :] for hh in range(N_HEADS_A)], axis=1)

    def block_step(kb_raw, far):
        kb = jnp.minimum(kb_raw, n_a - 1)
        kb_next = jnp.minimum(kb_raw + 1, n_a - 1)
        kc = jnp.minimum(kb_raw, n_c - 1)
        kc_next = jnp.minimum(kb_raw + 1, n_c - 1)
        thr = jnp.where(has_c & (kb_raw < n_c), tau_c - 1, jnp.int32(2 ** 31 - 1))
        sel = keys_scr[slot_c, kc] > thr
        ct_blk = ckvt_ref[kc]
        start = bias_start(kc)
        ref_pt = bound_far if far else bound
        score = jnp.zeros((kb_sz, LANES), F32)
        for g in range(n_groups):
            s_t = lta_scr[:, g * pair:(g + 1) * pair]
            for j in range(2):
                hh = 2 * g + j
                score = score + jnp.maximum(s_t[:, j * LANES:(j + 1) * LANES], 0.0) * w_t[hh:hh + 1, :]
            lta_scr[:, g * pair:(g + 1) * pair] = idx_dot(kb_next, g)
            lt = ltc_scr[:, g * pair:(g + 1) * pair]
            ps = []
            for j in range(2):
                hh = 2 * g + j
                piece = lt[:, j * LANES:(j + 1) * LANES]
                if not far:
                    piece = piece + btile_ref[hh, pl.ds(start, kb_sz), :]
                ps.append(jnp.exp2(jnp.where(sel, piece, NEG_BIG) - ref_pt[:, hh * LANES:(hh + 1) * LANES]))
            ltc_scr[:, g * pair:(g + 1) * pair] = logits(kc_next, g)
            acc_scr[g] += jnp.dot(ct_blk, jnp.concatenate(ps, axis=1).astype(BF16),
                                  preferred_element_type=F32)
        valid = (kb * kb_sz + row_id) <= (qa0 + lane_id)
        keys = jnp.where(valid, _sortable_key(score), INT_MIN)
        keys_scr[slot_a, kb] = keys
        v = keys ^ INT_MIN
        words = _bit_transpose32([v[8 * i:8 * (i + 1), :] for i in range(32)])
        for bit in range(32):
            planes_scr[bit, kb] = words[31 - bit]

    def block_body(far, it, carry):
        for u in range(BLOCK_UNROLL):
            block_step(it * BLOCK_UNROLL + u, far)
        return carry

    n_far_iters = jnp.maximum(n_c - 2, 0) // BLOCK_UNROLL
    lax.fori_loop(0, n_far_iters, functools.partial(block_body, True), 0)
    lax.fori_loop(n_far_iters, (n_a + BLOCK_UNROLL - 1) // BLOCK_UNROLL,
                  functools.partial(block_body, False), 0)
    n_kb = n_a


    blk_id = lax.broadcasted_iota(I32, cand_scr.shape, 0)
    cand_scr[...] = jnp.where(blk_id < n_kb, -1, 0)

    def bit_body(it, carry):
        above, tau_u = carry
        bit = 31 - it
        ones = cand_scr[...] & planes_scr[bit]
        c1 = jnp.sum(jnp.sum(lax.population_count(ones), axis=0), axis=0, keepdims=True)
        take = (above + c1) >= topk
        cand_scr[...] = jnp.where(take, ones, cand_scr[...] ^ ones)
        above = jnp.where(take, above, above + c1)
        tau_u = jnp.where(take, tau_u | (jnp.int32(1) << bit), tau_u)
        return above, tau_u

    zero = jnp.zeros((1, LANES), I32)
    n_gt, tau_u = lax.fori_loop(0, 32, bit_body, (zero, zero))
    tau = tau_u ^ INT_MIN
    n_eq = jnp.sum(jnp.sum(lax.population_count(cand_scr[...]), axis=0), axis=0, keepdims=True)

    need = topk - n_gt
    overflow = n_eq > need
    seq_bits = max(1, (n_blocks * kb_sz - 1).bit_length())

    @pl.when(jnp.max(jnp.where(overflow, 1, 0)) > 0)
    def _():
        def count_ties_before(trial):
            def body(kb, acc):
                hit = jnp.where((keys_scr[slot_a, kb] == tau) & ((kb * kb_sz + row_id) < trial), 1, 0)
                return acc + jnp.sum(hit.reshape(kb_sz // 8, 8, LANES), axis=0)
            acc = lax.fori_loop(0, n_kb, body, jnp.zeros((8, LANES), I32))
            return jnp.sum(acc, axis=0, keepdims=True)

        def idx_body(it, jc):
            trial = jc | (jnp.int32(1) << (seq_bits - 1 - it))
            return jnp.where(count_ties_before(trial) < need, trial, jc)

        j_cut = lax.fori_loop(0, seq_bits, idx_body, jnp.zeros((1, LANES), I32))

        def demote_body(kb, carry):
            k = keys_scr[slot_a, kb]
            drop = overflow & (k == tau) & ((kb * kb_sz + row_id) > j_cut)
            keys_scr[slot_a, kb] = jnp.where(drop, INT_MIN, k)
            return carry

        lax.fori_loop(0, n_kb, demote_body, 0)

    tau_scr[slot_a] = jnp.maximum(tau, INT_MIN + 1)

    l_min = jnp.min(jnp.concatenate([acc_scr[g, D_LATENT:D_LATENT + 1, :] for g in range(n_groups)], axis=1))

    @pl.when(has_c & jnp.logical_not(l_min >= 2.0 ** -80))
    def _():
        m_scr[...] = jnp.full(m_scr.shape, NEG_BIG, F32)
        acc_scr[...] = jnp.zeros(acc_scr.shape, F32)

        def exact_body(kb, carry):
            sel = keys_scr[slot_c, kb] >= tau_c
            ct_blk = ckvt_ref[kb]
            start = bias_start(kb)
            for g in range(n_groups):
                lt = logits(kb, g)
                ps, alphas = [], []
                for j in range(2):
                    hh = 2 * g + j
                    sl = slice(hh * LANES, (hh + 1) * LANES)
                    piece = lt[:, j * LANES:(j + 1) * LANES] + btile_ref[hh, pl.ds(start, kb_sz), :]
                    masked = jnp.where(sel, piece, NEG_BIG)
                    m_old = m_scr[:, sl]
                    m_new = jnp.maximum(m_old, jnp.max(masked, axis=0, keepdims=True))
                    m_scr[:, sl] = m_new
                    alphas.append(jnp.exp2(m_old - m_new))
                    ps.append(jnp.exp2(masked - m_new))
                pv = jnp.dot(ct_blk, jnp.concatenate(ps, axis=1).astype(BF16), preferred_element_type=F32)
                acc_scr[g] = jnp.concatenate(alphas, axis=1) * acc_scr[g] + pv
            return carry

        lax.fori_loop(0, n_c, exact_body, 0)

    @pl.when(has_c)
    def _():
        ys = []
        for hh in range(N_HEADS_A):
            acc_h = acc_scr[hh // 2, :, (hh % 2) * LANES:(hh % 2 + 1) * LANES]
            o_h = acc_h[:D_LATENT] * (1.0 / acc_h[D_LATENT:D_LATENT + 1])
            ys.append(jnp.dot(wuvt_ref[hh], o_h.astype(BF16), preferred_element_type=F32))
        y_t = jnp.concatenate(ys, axis=0)
        out_ref[...] = y_t.T.astype(BF16)


def _dsa(q_idx, q_abs, w_t, k_idx, ckv, ckv_t, btile, bmax, wuv_t, *, batch, seq):
    t = batch * seq
    nqb = seq // Q_BLOCK
    nkb = seq // KEY_BLOCK
    topk = min(TOPK_MAX, seq // 4)
    k_idx3 = k_idx.reshape(t // KEY_BLOCK, KEY_BLOCK, LANES)
    ckv3 = ckv.reshape(t // KEY_BLOCK, KEY_BLOCK, D_LATENT)
    per_batch = lambda shape: pl.BlockSpec(shape, lambda b, q: (b,) + (0,) * (len(shape) - 1),
                                           pipeline_mode=pl.Buffered(1))
    scored = lambda b, s: b * nqb + jnp.minimum(s, nqb - 1)
    attended = lambda b, s: b * nqb + jnp.maximum(s - 1, 0)
    return pl.pallas_call(
        functools.partial(_dsa_kernel, topk=topk, n_qb=nqb),
        out_shape=jax.ShapeDtypeStruct((t, W_A), BF16),
        grid=(batch, nqb + 1),
        in_specs=[pl.BlockSpec((1, N_HEADS_IDX, Q_BLOCK, LANES), lambda b, s: (scored(b, s), 0, 0, 0)),
                  pl.BlockSpec((1, N_HEADS_A, Q_BLOCK, D_LATENT), lambda b, s: (attended(b, s), 0, 0, 0)),
                  pl.BlockSpec((N_HEADS_IDX, Q_BLOCK), lambda b, s: (0, scored(b, s))),
                  per_batch((nkb, KEY_BLOCK, LANES)),
                  per_batch((nkb, KEY_BLOCK, D_LATENT)),
                  per_batch((nkb, CKVT_ROWS, KEY_BLOCK)),
                  _resident((N_HEADS_A, BIAS_ROWS, LANES)),
                  _resident((1, N_HEADS_A * LANES)),
                  _resident((N_HEADS_A, HEAD_DIM_A, D_LATENT))],
        out_specs=pl.BlockSpec((Q_BLOCK, W_A), lambda b, s: (attended(b, s), 0)),
        scratch_shapes=[pltpu.VMEM((2, nkb, KEY_BLOCK, LANES), I32),
                        pltpu.VMEM((32, nkb, 8, LANES), I32),
                        pltpu.VMEM((nkb, 8, LANES), I32),
                        pltpu.VMEM((2, 1, LANES), I32),
                        pltpu.VMEM((N_HEADS_A // 2, CKVT_ROWS, 2 * LANES), F32),
                        pltpu.VMEM((1, N_HEADS_A * LANES), F32),
                        pltpu.VMEM((KEY_BLOCK, N_HEADS_A * LANES), F32),
                        pltpu.VMEM((KEY_BLOCK, N_HEADS_A * LANES), F32),
                        pltpu.VMEM((1, 1), F32)],
        compiler_params=pltpu.CompilerParams(dimension_semantics=("arbitrary", "arbitrary"),
                                             vmem_limit_bytes=VMEM_LIMIT),
        name="dsa",
    )(q_idx, q_abs, w_t, k_idx3, ckv3, ckv_t, btile, bmax, wuv_t)


def _mlstm_kernel(qk_ref, v_ref, o_ref, ift_ref, gbt_ref, hn_ref,
                  out_ref, c_scr, n_scr, m_scr, *, chunk):
    L = chunk
    step = pl.program_id(1)

    @pl.when(step == 0)
    def _():
        c_scr[...] = jnp.zeros(c_scr.shape, F32)
        n_scr[...] = jnp.zeros(n_scr.shape, F32)
        m_scr[...] = jnp.zeros(m_scr.shape, F32)

    a_t = ift_ref[...] + gbt_ref[...]
    ls_t = _log_sigmoid(a_t)
    rr = lax.broadcasted_iota(I32, (L, L), 0)
    cc = lax.broadcasted_iota(I32, (L, L), 1)
    causal = cc <= rr
    triu = jnp.where(rr <= cc, 1.0, 0.0).astype(BF16)
    b_t = sum(jnp.dot(piece, triu, preferred_element_type=F32) for piece in _split3(ls_t))
    gates_t = jnp.concatenate([a_t[:N_HEADS_M], b_t[N_HEADS_M:],
                               jnp.zeros((LANES - 2 * N_HEADS_M, L), F32)], axis=0)
    gates_c = gates_t.T

    o_gate = _sigmoid(o_ref[...])
    for hh in range(N_HEADS_M):
        hs = slice(hh * HEAD_DIM_M, (hh + 1) * HEAD_DIM_M)
        qb16 = qk_ref[:, hs]
        kb16 = qk_ref[:, W_M + hh * HEAD_DIM_M:W_M + (hh + 1) * HEAD_DIM_M]
        q = qb16.astype(F32)
        k = kb16.astype(F32)
        v = v_ref[:, hs]
        i_c = gates_c[:, hh:hh + 1]
        bc = gates_c[:, N_HEADS_M + hh:N_HEADS_M + hh + 1]
        i_t = a_t[hh:hh + 1, :]
        bt = b_t[N_HEADS_M + hh:N_HEADS_M + hh + 1, :]
        b_last = bc[L - 1:L, :]
        m_prev = m_scr[hh]
        c_prev = c_scr[hh]
        n_prev = n_scr[hh]

        d_log = jnp.where(causal, bc - bt + i_t, NEG_BIG)
        inter_log = bc + m_prev
        m_j = jnp.maximum(inter_log, jnp.max(d_log, axis=1, keepdims=True))
        s = lax.dot_general(qb16, kb16, (((1,), (1,)), ((), ())),
                            preferred_element_type=F32) * jnp.exp(d_log - m_j)
        w_inter = jnp.exp(inter_log - m_j)
        num = (w_inter * jnp.dot(qb16, c_prev.astype(BF16), preferred_element_type=F32)
               + jnp.dot(s.astype(BF16), v, preferred_element_type=F32))
        den = w_inter * jnp.sum(q * n_prev, axis=1, keepdims=True) + jnp.sum(s, axis=1, keepdims=True)
        hval = num / jnp.maximum(jnp.abs(den), jnp.exp(-m_j))

        g_t = b_last - bt + i_t
        g_c = b_last - bc + i_c
        m_new = jnp.maximum(b_last + m_prev, jnp.max(g_t, axis=1, keepdims=True))
        decay = jnp.exp(b_last + m_prev - m_new)
        kw = k * jnp.exp(g_c - m_new)
        c_scr[hh] = decay * c_prev + jnp.dot(kw.T.astype(BF16), v, preferred_element_type=F32)
        n_scr[hh] = decay * n_prev + jnp.sum(kw, axis=0, keepdims=True)
        m_scr[hh] = m_new

        mu = jnp.mean(hval, axis=1, keepdims=True)
        cen = hval - mu
        var = jnp.mean(cen * cen, axis=1, keepdims=True)
        hn = cen * lax.rsqrt(var + EPS) * hn_ref[:, hs]
        out_ref[:, hs] = (hn * o_gate[:, hs]).astype(BF16)


def _mlstm(qk, v, o_pre, ift, gate_bias, head_norm, *, batch, seq):
    t = batch * seq
    L = MLSTM_CHUNK
    nc = seq // L
    gbt = jnp.broadcast_to(gate_bias.reshape(2 * N_HEADS_M, 1), (2 * N_HEADS_M, L))
    row = lambda w: pl.BlockSpec((L, w), lambda b, c: (b * nc + c, 0))
    return pl.pallas_call(
        functools.partial(_mlstm_kernel, chunk=L),
        out_shape=jax.ShapeDtypeStruct((t, W_M), BF16),
        grid=(batch, nc),
        in_specs=[row(2 * W_M), row(W_M), row(W_M),
                  pl.BlockSpec((2 * N_HEADS_M, L), lambda b, c: (0, b * nc + c)),
                  _resident((2 * N_HEADS_M, L)), _resident((1, W_M))],
        out_specs=row(W_M),
        scratch_shapes=[pltpu.VMEM((N_HEADS_M, HEAD_DIM_M, HEAD_DIM_M), F32),
                        pltpu.VMEM((N_HEADS_M, 1, HEAD_DIM_M), F32),
                        pltpu.VMEM((N_HEADS_M, 1, 1), F32)],
        compiler_params=pltpu.CompilerParams(dimension_semantics=("arbitrary", "arbitrary"),
                                             vmem_limit_bytes=VMEM_LIMIT),
        name="mlstm",
    )(qk, v, o_pre, ift, gbt, head_norm.reshape(1, -1))


def _merge_kernel(x_ref, ya_ref, hm_ref, ga_ref, gm_ref, g_ref, wa_ref, wm_ref, wo_ref, o_ref):
    pa = jnp.dot(ya_ref[...], wa_ref[...], preferred_element_type=F32)
    pm = jnp.dot(hm_ref[...], wm_ref[...], preferred_element_type=F32)
    merged = _sigmoid(ga_ref[...]) * pa + _sigmoid(gm_ref[...]) * pm
    out = jnp.dot(merged.astype(BF16), wo_ref[...], preferred_element_type=F32)
    o_ref[...] = x_ref[...] + g_ref[0] * out


def _merge(x, y_a, h_m, gate_a, gate_m, g, w_a, w_m, w_o, *, seq):
    t, d = x.shape
    tm = MERGE_TM
    per_b = seq // tm
    row = lambda w: pl.BlockSpec((tm, w), lambda i: (i, 0))
    return pl.pallas_call(
        _merge_kernel,
        out_shape=jax.ShapeDtypeStruct((t, d), F32),
        grid=(t // tm,),
        in_specs=[row(d), row(W_A), row(W_M), row(d), row(d),
                  pl.BlockSpec((1, 1, d), lambda i: (i // per_b, 0, 0)),
                  _resident((W_A, d)), _resident((W_M, d)), _resident((d, d))],
        out_specs=row(d),
        compiler_params=pltpu.CompilerParams(dimension_semantics=("arbitrary",),
                                             vmem_limit_bytes=VMEM_LIMIT),
        name="merge",
    )(x, y_a, h_m, gate_a, gate_m, g, w_a.astype(BF16), w_m.astype(BF16), w_o.astype(BF16))


def kernel(x, c, ada_w, ada_b, ffn1_norm, ffn1_w1, ffn1_w3, ffn1_w2, mix_norm, w_in, conv_w, conv_b,
           kv_norm, w_uk, w_uv, mlstm_gate_bias, mlstm_head_norm, rel_bias, w_branch_attn,
           w_branch_mlstm, w_out, ffn2_norm, ffn2_w1, ffn2_w3, ffn2_w2, final_norm):
    batch, seq, d = x.shape
    depth = ada_w.shape[0]
    assert seq % max(FFN_TM, MIX_TM, MERGE_TM, MLSTM_CHUNK, KEY_BLOCK) == 0
    t = batch * seq
    xf = x.reshape(t, d)
    btile, bmax = _bias_tiles(rel_bias)
    for l in range(depth):
        mod = _adaln(c, ada_w[l], ada_b[l]).reshape(batch, 9, 1, d)
        sh1, sc1, g1, sh2, sc2, g2, sh3, sc3, g3 = [mod[:, n] for n in range(9)]
        xf = _ffn(xf, ffn1_norm[l], sh1, sc1, g1, ffn1_w1[l], ffn1_w3[l], ffn1_w2[l], final_norm,
                  seq=seq, final_norm=False)
        wuk_t = jnp.pad(w_uk[l].transpose(0, 2, 1), ((0, 0), (0, LANES - HEAD_DIM_A), (0, 0))).astype(BF16)
        (q_abs, q_idx, k_idx, ckv, ckv_t, w_t, ift, qk_m, v_m, o_pre, gate_a, gate_m) = _mixin(
            xf, mix_norm[l], sh2, sc2, _pack_w_in(w_in[l], d), kv_norm[l], wuk_t, conv_w[l], conv_b[l],
            seq=seq)
        wuv_t = w_uv[l].transpose(0, 2, 1).astype(BF16)
        y_a = _dsa(q_idx, q_abs, w_t, k_idx, ckv, ckv_t, btile, bmax, wuv_t, batch=batch, seq=seq)
        h_m = _mlstm(qk_m, v_m, o_pre, ift, mlstm_gate_bias[l], mlstm_head_norm[l], batch=batch, seq=seq)
        xf = _merge(xf, y_a, h_m, gate_a, gate_m, g2, w_branch_attn[l], w_branch_mlstm[l], w_out[l],
                    seq=seq)
        xf = _ffn(xf, ffn2_norm[l], sh3, sc3, g3, ffn2_w1[l], ffn2_w3[l], ffn2_w2[l], final_norm,
                  seq=seq, final_norm=(l == depth - 1))
    return xf.reshape(batch, seq, d)
```

```python
import functools
import math

import jax
import jax.numpy as jnp
from jax import lax
from jax.experimental import pallas as pl
from jax.experimental.pallas import tpu as pltpu

F32 = jnp.float32
BF16 = jnp.bfloat16
I32 = jnp.int32

LANES = 128
MXU_DIM = 256
VMEM_LIMIT = 56 * 1024 * 1024

N_HEADS_A = 8
HEAD_DIM_A = 64
D_LATENT = 256
N_HEADS_IDX = 8
HEAD_DIM_IDX = 64
TOPK_MAX = 256
Q_BLOCK = 128
N_BUCKETS = 32
MAX_DISTANCE = 128
N_HEADS_M = 4
HEAD_DIM_M = 128
CONV_WIDTH = 4
EPS = 1e-6
IDX_SCALE = (N_HEADS_IDX ** -0.5) * (HEAD_DIM_IDX ** -0.5)
W_A = N_HEADS_A * HEAD_DIM_A
W_M = N_HEADS_M * HEAD_DIM_M

FFN_TM = 512
FFN_CHUNK = 256
MIX_TM = 512
KEY_BLOCK = 256
BLOCK_UNROLL = 2
MLSTM_CHUNK = 256
MERGE_TM = 512
NEG_BIG = -1e30
INT_MIN = -2 ** 31

BIAS_PAD = 384
BIAS_ROWS = KEY_BLOCK + BIAS_PAD
CKVT_ROWS = D_LATENT + 16
LOG2E = math.log2(math.e)


def _sigmoid(x):
    return 1.0 / (1.0 + jnp.exp(-x))


def _log_sigmoid(x):
    return jnp.minimum(x, 0.0) - jnp.log(1.0 + jnp.exp(-jnp.abs(x)))


def _rms_norm(x, gain):
    ms = jnp.mean(x * x, axis=-1, keepdims=True)
    return x * lax.rsqrt(ms + EPS) * gain


def _split3(x):
    hi = x.astype(BF16)
    r1 = x - hi.astype(F32)
    mid = r1.astype(BF16)
    lo = (r1 - mid.astype(F32)).astype(BF16)
    return hi, mid, lo


def _resident(shape):
    nd = len(shape)
    return pl.BlockSpec(shape, lambda *_: (0,) * nd, pipeline_mode=pl.Buffered(1))


def _adaln_kernel(c_ref, w_ref, b_ref, o_ref):
    c = c_ref[...]
    cond = c * _sigmoid(c)
    o_ref[...] = jnp.dot(cond.astype(BF16), w_ref[...].astype(BF16),
                         preferred_element_type=F32) + b_ref[...]


def _adaln(c, ada_w, ada_b):
    b, d = c.shape
    n = ada_w.shape[1]
    rows = 8
    c_pad = jnp.zeros((rows, d), F32).at[:b].set(c)
    tn = 1024
    out = pl.pallas_call(
        _adaln_kernel,
        out_shape=jax.ShapeDtypeStruct((rows, n), F32),
        grid=(n // tn,),
        in_specs=[pl.BlockSpec((rows, d), lambda j: (0, 0)),
                  pl.BlockSpec((d, tn), lambda j: (0, j)),
                  pl.BlockSpec((1, tn), lambda j: (0, j))],
        out_specs=pl.BlockSpec((rows, tn), lambda j: (0, j)),
        compiler_params=pltpu.CompilerParams(dimension_semantics=("arbitrary",),
                                             vmem_limit_bytes=VMEM_LIMIT),
        name="adaln",
    )(c_pad, ada_w, ada_b.reshape(1, n))
    return out[:b]


def _t5_bucket(dist):
    n = jnp.maximum(dist, 0)
    max_exact = N_BUCKETS // 2
    nf = jnp.maximum(n, 1).astype(F32)
    large = max_exact + (jnp.log(nf / max_exact) / math.log(MAX_DISTANCE / max_exact)
                         * (N_BUCKETS - max_exact)).astype(I32)
    large = jnp.minimum(large, N_BUCKETS - 1)
    return jnp.where(n < max_exact, n, large)


def _bias_kernel(rel_ref, tile_ref, max_ref):
    r = lax.broadcasted_iota(I32, (BIAS_ROWS, LANES), 0)
    i = lax.broadcasted_iota(I32, (BIAS_ROWS, LANES), 1)
    bucket = _t5_bucket(i - r + BIAS_PAD)
    for h in range(N_HEADS_A):
        acc = jnp.zeros((BIAS_ROWS, LANES), F32)
        top = rel_ref[0, h] * LOG2E
        for bkt in range(N_BUCKETS):
            val = rel_ref[bkt, h] * LOG2E
            acc = jnp.where(bucket == bkt, val, acc)
            top = jnp.maximum(top, val)
        tile_ref[h] = acc
        max_ref[:, h * LANES:(h + 1) * LANES] = jnp.full((1, LANES), top, F32)


def _bias_tiles(rel_bias):
    return pl.pallas_call(
        _bias_kernel,
        out_shape=(jax.ShapeDtypeStruct((N_HEADS_A, BIAS_ROWS, LANES), F32),
                   jax.ShapeDtypeStruct((1, N_HEADS_A * LANES), F32)),
        in_specs=[pl.BlockSpec(memory_space=pltpu.SMEM)],
        out_specs=(pl.BlockSpec(memory_space=pltpu.VMEM), pl.BlockSpec(memory_space=pltpu.VMEM)),
        name="bias_tiles",
    )(rel_bias)


def _ffn_kernel(x_ref, gain_ref, sh_ref, sc_ref, g_ref, w1_ref, w3_ref, w2_ref, fin_ref, o_ref,
                h_scr, acc_scr, *, n_chunks, final_norm):
    x = x_ref[...]
    h = _rms_norm(x, gain_ref[...]) * (1.0 + sc_ref[0]) + sh_ref[0]
    h_scr[...] = h.astype(BF16)
    for j in range(n_chunks):
        hb = h_scr[...]
        cols = slice(j * FFN_CHUNK, (j + 1) * FFN_CHUNK)
        u1 = jnp.dot(hb, w1_ref[:, cols], preferred_element_type=F32)
        u3 = jnp.dot(hb, w3_ref[:, cols], preferred_element_type=F32)
        a = (u1 * _sigmoid(u1)) * u3
        part = jnp.dot(a.astype(BF16), w2_ref[j], preferred_element_type=F32)
        if j == 0:
            acc_scr[...] = part
        else:
            acc_scr[...] += part
    out = x + (0.5 * g_ref[0]) * acc_scr[...]
    if final_norm:
        out = _rms_norm(out, fin_ref[...])
    o_ref[...] = out


def _ffn(x, gain, sh, sc, g, w1, w3, w2, fin, *, seq, final_norm):
    t, d = x.shape
    dff = w1.shape[1]
    nch = dff // FFN_CHUNK
    w1c = w1.astype(BF16)
    w3c = w3.astype(BF16)
    w2c = w2.astype(BF16).reshape(nch, FFN_CHUNK, d)
    tm = FFN_TM
    per_b = seq // tm
    mod_spec = pl.BlockSpec((1, 1, d), lambda i: (i // per_b, 0, 0))
    return pl.pallas_call(
        functools.partial(_ffn_kernel, n_chunks=nch, final_norm=final_norm),
        out_shape=jax.ShapeDtypeStruct((t, d), F32),
        grid=(t // tm,),
        in_specs=[pl.BlockSpec((tm, d), lambda i: (i, 0)),
                  _resident((1, d)), mod_spec, mod_spec, mod_spec,
                  _resident((d, dff)), _resident((d, dff)),
                  _resident((nch, FFN_CHUNK, d)), _resident((1, d))],
        out_specs=pl.BlockSpec((tm, d), lambda i: (i, 0)),
        scratch_shapes=[pltpu.VMEM((tm, d), BF16), pltpu.VMEM((tm, d), F32)],
        compiler_params=pltpu.CompilerParams(dimension_semantics=("arbitrary",),
                                             vmem_limit_bytes=VMEM_LIMIT),
        name="ffn_final" if final_norm else "ffn",
    )(x, gain.reshape(1, d), sh, sc, g, w1c, w3c, w2c, fin.reshape(1, d))


_C_QA = 0
_C_CKV = _C_QA + N_HEADS_A * LANES
_C_QI = _C_CKV + D_LATENT
_C_KI = _C_QI + N_HEADS_IDX * LANES
_C_SM = _C_KI + LANES
_C_QK = _C_SM + LANES
_C_V = _C_QK + 2 * W_M
_C_O = _C_V + W_M
_C_GA = _C_O + W_M
_C_GM = _C_GA + 1024
_C_END = _C_GM + 1024
_SM_W = 0
_SM_I = N_HEADS_IDX
_SM_F = N_HEADS_IDX + N_HEADS_M


def _pack_w_in(w_in, d_model):
    splits = (W_A, D_LATENT, N_HEADS_IDX * HEAD_DIM_IDX, HEAD_DIM_IDX, N_HEADS_IDX,
              W_M, W_M, W_M, N_HEADS_M, N_HEADS_M, W_M, d_model, d_model)
    offs = [0]
    for s in splits:
        offs.append(offs[-1] + s)
    (q_a, c_kv, q_i, k_i, w_i, q_m, k_m, v_m, i_p, f_p, o_p, g_a, g_m) = [
        w_in[:, offs[n]:offs[n + 1]] for n in range(len(splits))]
    d = w_in.shape[0]

    def pad_heads(w, nh, hd):
        w = w.reshape(d, nh, hd)
        return jnp.pad(w, ((0, 0), (0, 0), (0, LANES - hd))).reshape(d, nh * LANES)

    small = jnp.concatenate([w_i, i_p, f_p], axis=1)
    small = jnp.pad(small, ((0, 0), (0, LANES - small.shape[1])))
    packed = jnp.concatenate([
        pad_heads(q_a, N_HEADS_A, HEAD_DIM_A), c_kv, pad_heads(q_i, N_HEADS_IDX, HEAD_DIM_IDX),
        jnp.pad(k_i, ((0, 0), (0, LANES - HEAD_DIM_IDX))), small, q_m, k_m, v_m, o_p, g_a, g_m], axis=1)
    assert packed.shape[1] == _C_END
    return packed.astype(BF16)


def _mixin_kernel(x_ref, gain_ref, sh_ref, sc_ref, w_ref, kvn_ref, wuk_ref, cw_ref, cb_ref,
                  qabs_ref, qidx_ref, kidx_ref, ckv_ref, ckvt_ref, wt_ref, ift_ref,
                  qk_ref, v_ref, o_ref, ga_ref, gm_ref, h_scr, xe_scr, *, tm, tiles_per_seq):
    nqb = tm // Q_BLOCK

    @pl.when(pl.program_id(0) % tiles_per_seq == 0)
    def _():
        xe_scr[:8] = jnp.zeros((8, xe_scr.shape[1]), F32)

    x = x_ref[...]
    h = _rms_norm(x, gain_ref[...]) * (1.0 + sc_ref[0]) + sh_ref[0]
    h_scr[...] = h.astype(BF16)

    def proj(lo, hi):
        return jnp.dot(h_scr[...], w_ref[:, lo:hi], preferred_element_type=F32)

    qa = proj(_C_QA, _C_CKV)
    scale = HEAD_DIM_A ** -0.5 * LOG2E
    for hh in range(N_HEADS_A):
        q_h = qa[:, hh * LANES:(hh + 1) * LANES].astype(BF16)
        q_abs = jnp.dot(q_h, wuk_ref[hh], preferred_element_type=F32) * scale
        qabs_ref[:, hh] = q_abs.astype(BF16).reshape(nqb, Q_BLOCK, D_LATENT)
    ckv = _rms_norm(proj(_C_CKV, _C_QI), kvn_ref[...])
    ckv_ref[...] = ckv.astype(BF16)
    ckv_t = ckv.T
    ones_row = jnp.where(lax.broadcasted_iota(I32, (CKVT_ROWS - D_LATENT, KEY_BLOCK), 0) == 0, 1.0, 0.0)
    for j in range(tm // KEY_BLOCK):
        ckvt_ref[j, :D_LATENT] = ckv_t[:, j * KEY_BLOCK:(j + 1) * KEY_BLOCK].astype(BF16)
        ckvt_ref[j, D_LATENT:] = ones_row.astype(BF16)
    qi = proj(_C_QI, _C_KI)
    for hh in range(N_HEADS_IDX):
        qidx_ref[:, hh] = qi[:, hh * LANES:(hh + 1) * LANES].astype(BF16).reshape(nqb, Q_BLOCK, LANES)
    kidx_ref[...] = proj(_C_KI, _C_SM).astype(BF16)
    small_t = proj(_C_SM, _C_QK).T
    wt_ref[...] = small_t[_SM_W:_SM_W + N_HEADS_IDX] * IDX_SCALE
    ift_ref[...] = small_t[_SM_I:_SM_I + 2 * N_HEADS_M]
    xe_scr[8:] = proj(_C_QK, _C_V)
    xe = xe_scr[...]
    xq = xe[8:]
    conv = xq * cw_ref[CONV_WIDTH - 1:CONV_WIDTH, :] + cb_ref[...]
    for d in range(1, CONV_WIDTH):
        conv = conv + pltpu.roll(xe, d, axis=0)[8:] * cw_ref[CONV_WIDTH - 1 - d:CONV_WIDTH - d, :]
    xe_scr[:8] = xe_scr[tm:]
    qk = conv * _sigmoid(conv)
    qk_ref[:, :W_M] = qk[:, :W_M].astype(BF16)
    qk_ref[:, W_M:] = (qk[:, W_M:] * (HEAD_DIM_M ** -0.5)).astype(BF16)
    v_ref[...] = proj(_C_V, _C_O).astype(BF16)
    o_ref[...] = proj(_C_O, _C_GA)
    ga_ref[...] = proj(_C_GA, _C_GM)
    gm_ref[...] = proj(_C_GM, _C_END)


def _mixin(x, gain, sh, sc, w_packed, kv_norm, wuk_t, conv_w, conv_b, *, seq):
    t, d = x.shape
    tm = MIX_TM
    per_b = seq // tm
    nqb = tm // Q_BLOCK
    row = lambda w: pl.BlockSpec((tm, w), lambda i: (i, 0))
    mod_spec = pl.BlockSpec((1, 1, d), lambda i: (i // per_b, 0, 0))
    out_shape = (
        jax.ShapeDtypeStruct((t // Q_BLOCK, N_HEADS_A, Q_BLOCK, D_LATENT), BF16),
        jax.ShapeDtypeStruct((t // Q_BLOCK, N_HEADS_IDX, Q_BLOCK, LANES), BF16),
        jax.ShapeDtypeStruct((t, LANES), BF16),
        jax.ShapeDtypeStruct((t, D_LATENT), BF16),
        jax.ShapeDtypeStruct((t // KEY_BLOCK, CKVT_ROWS, KEY_BLOCK), BF16),
        jax.ShapeDtypeStruct((N_HEADS_IDX, t), F32),
        jax.ShapeDtypeStruct((2 * N_HEADS_M, t), F32),
        jax.ShapeDtypeStruct((t, 2 * W_M), BF16),
        jax.ShapeDtypeStruct((t, W_M), BF16),
        jax.ShapeDtypeStruct((t, W_M), F32),
        jax.ShapeDtypeStruct((t, d), F32),
        jax.ShapeDtypeStruct((t, d), F32),
    )
    out_specs = (
        pl.BlockSpec((nqb, N_HEADS_A, Q_BLOCK, D_LATENT), lambda i: (i, 0, 0, 0)),
        pl.BlockSpec((nqb, N_HEADS_IDX, Q_BLOCK, LANES), lambda i: (i, 0, 0, 0)),
        row(LANES), row(D_LATENT),
        pl.BlockSpec((tm // KEY_BLOCK, CKVT_ROWS, KEY_BLOCK), lambda i: (i, 0, 0)),
        pl.BlockSpec((N_HEADS_IDX, tm), lambda i: (0, i)),
        pl.BlockSpec((2 * N_HEADS_M, tm), lambda i: (0, i)),
        row(2 * W_M), row(W_M), row(W_M), row(d), row(d),
    )
    return pl.pallas_call(
        functools.partial(_mixin_kernel, tm=tm, tiles_per_seq=per_b),
        out_shape=out_shape,
        grid=(t // tm,),
        in_specs=[pl.BlockSpec((tm, d), lambda i: (i, 0)), _resident((1, d)), mod_spec, mod_spec,
                  _resident((d, _C_END)), _resident((1, D_LATENT)),
                  _resident((N_HEADS_A, LANES, D_LATENT)),
                  _resident((CONV_WIDTH, 2 * W_M)), _resident((1, 2 * W_M))],
        out_specs=out_specs,
        scratch_shapes=[pltpu.VMEM((tm, d), BF16), pltpu.VMEM((tm + 8, 2 * W_M), F32)],
        compiler_params=pltpu.CompilerParams(dimension_semantics=("arbitrary",),
                                             vmem_limit_bytes=VMEM_LIMIT),
        name="mixin",
    )(x, gain.reshape(1, d), sh, sc, w_packed, kv_norm.reshape(1, D_LATENT), wuk_t,
      conv_w, conv_b.reshape(1, -1))


def _sortable_key(score):
    bits = pltpu.bitcast(score, I32)
    bits = jnp.where(bits == INT_MIN, 0, bits)
    return jnp.where(bits < 0, bits ^ 0x7FFFFFFF, bits)


def _bit_transpose32(words):
    v = list(words)
    j, m = 16, 0x0000FFFF
    while j:
        k = 0
        while k < 32:
            t = (v[k] ^ lax.shift_right_logical(v[k + j], jnp.int32(j))) & m
            v[k] = v[k] ^ t
            v[k + j] = v[k + j] ^ (t << j)
            k = (k + j + 1) & ~j
        j >>= 1
        m = (m ^ (m << j)) & 0x7FFFFFFF
    return v


def _dsa_kernel(qidx_ref, qabs_ref, wt_ref, kidx_ref, ckv_ref, ckvt_ref, btile_ref, bmax_ref, wuvt_ref,
                out_ref, keys_scr, planes_scr, cand_scr, tau_scr, acc_scr, m_scr, lta_scr, ltc_scr, kmax_scr,
                *, topk, n_qb):
    kb_sz = KEY_BLOCK
    step = pl.program_id(1)
    has_c = step >= 1
    qa = jnp.minimum(step, n_qb - 1)
    qc = jnp.maximum(step - 1, 0)
    slot_a = step & 1
    slot_c = 1 - slot_a
    n_a = qa // (kb_sz // Q_BLOCK) + 1
    n_c = qc // (kb_sz // Q_BLOCK) + 1
    qa0 = qa * Q_BLOCK
    qc0 = qc * Q_BLOCK
    row_id = lax.broadcasted_iota(I32, (kb_sz, LANES), 0)
    lane_id = lax.broadcasted_iota(I32, (kb_sz, LANES), 1)
    n_blocks = keys_scr.shape[1]
    n_groups = N_HEADS_A // 2
    pair = 2 * LANES
    ones8 = jnp.ones((8, D_LATENT), BF16)

    @pl.when(step == 0)
    def _():
        keys_scr[1, 0] = jnp.full((kb_sz, LANES), INT_MIN, I32)
        tau_scr[1] = jnp.zeros((1, LANES), I32)
        planes_scr[...] = jnp.zeros(planes_scr.shape, I32)

        def kn_body(kb, mx):
            c = ckv_ref[kb].astype(F32)
            n2 = lax.dot_general(ones8, (c * c).astype(BF16), (((1,), (1,)), ((), ())),
                                 preferred_element_type=F32)
            return jnp.maximum(mx, n2[0:1])
        mx = lax.fori_loop(0, n_blocks, kn_body, jnp.zeros((1, kb_sz), F32))
        kmax_scr[...] = jnp.max(mx, axis=1, keepdims=True)

    def idx_dot(kb, g):
        q_g = qidx_ref[0, 2 * g:2 * g + 2].reshape(2 * Q_BLOCK, LANES)
        return lax.dot_general(kidx_ref[kb], q_g, (((1,), (1,)), ((), ())), preferred_element_type=F32)

    def logits(kb, g):
        q_g = qabs_ref[0, 2 * g:2 * g + 2].reshape(2 * Q_BLOCK, D_LATENT)
        return lax.dot_general(ckv_ref[kb], q_g, (((1,), (1,)), ((), ())), preferred_element_type=F32)

    def bias_start(kb):
        delta = jnp.minimum(qc0 - kb * kb_sz, BIAS_PAD)
        return pl.multiple_of(BIAS_PAD - delta, LANES)

    qn2 = []
    for g in range(n_groups):
        q_g = qabs_ref[0, 2 * g:2 * g + 2].reshape(2 * Q_BLOCK, D_LATENT).astype(F32)
        qn2.append(lax.dot_general(ones8, (q_g * q_g).astype(BF16), (((1,), (1,)), ((), ())),
                                   preferred_element_type=F32)[0:1])
    bound = jnp.sqrt(jnp.concatenate(qn2, axis=1) * kmax_scr[...]) * 1.02 + bmax_ref[...] + 1e-3
    tau_c = tau_scr[slot_c]
    w_t = wt_ref[...]

    def put_pair(ref, g, val):
        ref[2 * g] = val[:, :LANES]
        ref[2 * g + 1] = val[:, LANES:]

    acc_scr[...] = jnp.zeros(acc_scr.shape, F32)
    for g in range(n_groups):
        put_pair(lta_scr, g, idx_dot(0, g))
        put_pair(ltc_scr, g, logits(0, g))

    bound_far = bound - jnp.concatenate([btile_ref[hh, 0:1, :] for hh in range(N_HEADS_A)], axis=1)

    def block_step(kb_raw, far):
        kb = jnp.minimum(kb_raw, n_a - 1)
        kb_next = jnp.minimum(kb_raw + 1, n_a - 1)
        kc = jnp.minimum(kb_raw, n_c - 1)
        kc_next = jnp.minimum(kb_raw + 1, n_c - 1)
        thr = jnp.where(has_c & (kb_raw < n_c), tau_c - 1, jnp.int32(2 ** 31 - 1))
        sel = keys_scr[slot_c, kc] > thr
        ct_blk = ckvt_ref[kc]
        start = bias_start(kc)
        ref_pt = bound_far if far else bound
        score = jnp.zeros((kb_sz, LANES), F32)
        for g in range(n_groups):
            for hh in (2 * g, 2 * g + 1):
                score = score + jnp.maximum(lta_scr[hh], 0.0) * w_t[hh:hh + 1, :]
            put_pair(lta_scr, g, idx_dot(kb_next, g))
            ps = []
            for hh in (2 * g, 2 * g + 1):
                piece = ltc_scr[hh]
                if not far:
                    piece = piece + btile_ref[hh, pl.ds(start, kb_sz), :]
                ps.append(jnp.exp2(jnp.where(sel, piece, NEG_BIG) - ref_pt[:, hh * LANES:(hh + 1) * LANES]))
            put_pair(ltc_scr, g, logits(kc_next, g))
            acc_scr[g] += jnp.dot(ct_blk, jnp.concatenate(ps, axis=1).astype(BF16),
                                  preferred_element_type=F32)
        valid = (kb * kb_sz + row_id) <= (qa0 + lane_id)
        keys = jnp.where(valid, _sortable_key(score), INT_MIN)
        keys_scr[slot_a, kb] = keys
        v = keys ^ INT_MIN
        words = _bit_transpose32([v[8 * i:8 * (i + 1), :] for i in range(32)])
        for bit in range(32):
            planes_scr[bit, kb] = words[31 - bit]

    def block_body(far, it, carry):
        for u in range(BLOCK_UNROLL):
            block_step(it * BLOCK_UNROLL + u, far)
        return carry

    n_far_iters = jnp.maximum(n_c - 2, 0) // BLOCK_UNROLL
    lax.fori_loop(0, n_far_iters, functools.partial(block_body, True), 0)
    lax.fori_loop(n_far_iters, (n_a + BLOCK_UNROLL - 1) // BLOCK_UNROLL,
                  functools.partial(block_body, False), 0)
    n_kb = n_a


    blk_id = lax.broadcasted_iota(I32, cand_scr.shape, 0)
    cand_scr[...] = jnp.where(blk_id < n_kb, -1, 0)

    def bit_body(it, carry):
        above, tau_u = carry
        bit = 31 - it
        ones = cand_scr[...] & planes_scr[bit]
        c1 = jnp.sum(jnp.sum(lax.population_count(ones), axis=0), axis=0, keepdims=True)
        take = (above + c1) >= topk
        cand_scr[...] = jnp.where(take, ones, cand_scr[...] ^ ones)
        above = jnp.where(take, above, above + c1)
        tau_u = jnp.where(take, tau_u | (jnp.int32(1) << bit), tau_u)
        return above, tau_u

    zero = jnp.zeros((1, LANES), I32)
    n_gt, tau_u = lax.fori_loop(0, 32, bit_body, (zero, zero))
    tau = tau_u ^ INT_MIN
    n_eq = jnp.sum(jnp.sum(lax.population_count(cand_scr[...]), axis=0), axis=0, keepdims=True)

    need = topk - n_gt
    overflow = n_eq > need
    seq_bits = max(1, (n_blocks * kb_sz - 1).bit_length())

    @pl.when(jnp.max(jnp.where(overflow, 1, 0)) > 0)
    def _():
        def count_ties_before(trial):
            def body(kb, acc):
                hit = jnp.where((keys_scr[slot_a, kb] == tau) & ((kb * kb_sz + row_id) < trial), 1, 0)
                return acc + jnp.sum(hit.reshape(kb_sz // 8, 8, LANES), axis=0)
            acc = lax.fori_loop(0, n_kb, body, jnp.zeros((8, LANES), I32))
            return jnp.sum(acc, axis=0, keepdims=True)

        def idx_body(it, jc):
            trial = jc | (jnp.int32(1) << (seq_bits - 1 - it))
            return jnp.where(count_ties_before(trial) < need, trial, jc)

        j_cut = lax.fori_loop(0, seq_bits, idx_body, jnp.zeros((1, LANES), I32))

        def demote_body(kb, carry):
            k = keys_scr[slot_a, kb]
            drop = overflow & (k == tau) & ((kb * kb_sz + row_id) > j_cut)
            keys_scr[slot_a, kb] = jnp.where(drop, INT_MIN, k)
            return carry

        lax.fori_loop(0, n_kb, demote_body, 0)

    tau_scr[slot_a] = jnp.maximum(tau, INT_MIN + 1)

    l_min = jnp.min(jnp.concatenate([acc_scr[g, D_LATENT:D_LATENT + 1, :] for g in range(n_groups)], axis=1))

    @pl.when(has_c & jnp.logical_not(l_min >= 2.0 ** -80))
    def _():
        m_scr[...] = jnp.full(m_scr.shape, NEG_BIG, F32)
        acc_scr[...] = jnp.zeros(acc_scr.shape, F32)

        def exact_body(kb, carry):
            sel = keys_scr[slot_c, kb] >= tau_c
            ct_blk = ckvt_ref[kb]
            start = bias_start(kb)
            for g in range(n_groups):
                lt = logits(kb, g)
                ps, alphas = [], []
                for j in range(2):
                    hh = 2 * g + j
                    sl = slice(hh * LANES, (hh + 1) * LANES)
                    piece = lt[:, j * LANES:(j + 1) * LANES] + btile_ref[hh, pl.ds(start, kb_sz), :]
                    masked = jnp.where(sel, piece, NEG_BIG)
                    m_old = m_scr[:, sl]
                    m_new = jnp.maximum(m_old, jnp.max(masked, axis=0, keepdims=True))
                    m_scr[:, sl] = m_new
                    alphas.append(jnp.exp2(m_old - m_new))
                    ps.append(jnp.exp2(masked - m_new))
                pv = jnp.dot(ct_blk, jnp.concatenate(ps, axis=1).astype(BF16), preferred_element_type=F32)
                acc_scr[g] = jnp.concatenate(alphas, axis=1) * acc_scr[g] + pv
            return carry

        lax.fori_loop(0, n_c, exact_body, 0)

    @pl.when(has_c)
    def _():
        ys = []
        for hh in range(N_HEADS_A):
            acc_h = acc_scr[hh // 2, :, (hh % 2) * LANES:(hh % 2 + 1) * LANES]
            o_h = acc_h[:D_LATENT] * (1.0 / acc_h[D_LATENT:D_LATENT + 1])
            ys.append(jnp.dot(wuvt_ref[hh], o_h.astype(BF16), preferred_element_type=F32))
        y_t = jnp.concatenate(ys, axis=0)
        out_ref[...] = y_t.T.astype(BF16)


def _dsa(q_idx, q_abs, w_t, k_idx, ckv, ckv_t, btile, bmax, wuv_t, *, batch, seq):
    t = batch * seq
    nqb = seq // Q_BLOCK
    nkb = seq // KEY_BLOCK
    topk = min(TOPK_MAX, seq // 4)
    k_idx3 = k_idx.reshape(t // KEY_BLOCK, KEY_BLOCK, LANES)
    ckv3 = ckv.reshape(t // KEY_BLOCK, KEY_BLOCK, D_LATENT)
    per_batch = lambda shape: pl.BlockSpec(shape, lambda b, q: (b,) + (0,) * (len(shape) - 1),
                                           pipeline_mode=pl.Buffered(1))
    scored = lambda b, s: b * nqb + jnp.minimum(s, nqb - 1)
    attended = lambda b, s: b * nqb + jnp.maximum(s - 1, 0)
    return pl.pallas_call(
        functools.partial(_dsa_kernel, topk=topk, n_qb=nqb),
        out_shape=jax.ShapeDtypeStruct((t, W_A), BF16),
        grid=(batch, nqb + 1),
        in_specs=[pl.BlockSpec((1, N_HEADS_IDX, Q_BLOCK, LANES), lambda b, s: (scored(b, s), 0, 0, 0)),
                  pl.BlockSpec((1, N_HEADS_A, Q_BLOCK, D_LATENT), lambda b, s: (attended(b, s), 0, 0, 0)),
                  pl.BlockSpec((N_HEADS_IDX, Q_BLOCK), lambda b, s: (0, scored(b, s))),
                  per_batch((nkb, KEY_BLOCK, LANES)),
                  per_batch((nkb, KEY_BLOCK, D_LATENT)),
                  per_batch((nkb, CKVT_ROWS, KEY_BLOCK)),
                  _resident((N_HEADS_A, BIAS_ROWS, LANES)),
                  _resident((1, N_HEADS_A * LANES)),
                  _resident((N_HEADS_A, HEAD_DIM_A, D_LATENT))],
        out_specs=pl.BlockSpec((Q_BLOCK, W_A), lambda b, s: (attended(b, s), 0)),
        scratch_shapes=[pltpu.VMEM((2, nkb, KEY_BLOCK, LANES), I32),
                        pltpu.VMEM((32, nkb + 1, 8, LANES), I32),
                        pltpu.VMEM((nkb + 1, 8, LANES), I32),
                        pltpu.VMEM((2, 1, LANES), I32),
                        pltpu.VMEM((N_HEADS_A // 2, CKVT_ROWS, 2 * LANES), F32),
                        pltpu.VMEM((1, N_HEADS_A * LANES), F32),
                        pltpu.VMEM((N_HEADS_IDX, KEY_BLOCK, LANES), F32),
                        pltpu.VMEM((N_HEADS_A, KEY_BLOCK, LANES), F32),
                        pltpu.VMEM((1, 1), F32)],
        compiler_params=pltpu.CompilerParams(dimension_semantics=("arbitrary", "arbitrary"),
                                             vmem_limit_bytes=VMEM_LIMIT),
        name="dsa",
    )(q_idx, q_abs, w_t, k_idx3, ckv3, ckv_t, btile, bmax, wuv_t)


def _mlstm_kernel(qk_ref, v_ref, o_ref, ift_ref, gbt_ref, hn_ref,
                  out_ref, c_scr, n_scr, m_scr, *, chunk):
    L = chunk
    step = pl.program_id(1)

    @pl.when(step == 0)
    def _():
        c_scr[...] = jnp.zeros(c_scr.shape, F32)
        n_scr[...] = jnp.zeros(n_scr.shape, F32)
        m_scr[...] = jnp.zeros(m_scr.shape, F32)

    a_t = ift_ref[...] + gbt_ref[...]
    ls_t = _log_sigmoid(a_t)
    rr = lax.broadcasted_iota(I32, (L, L), 0)
    cc = lax.broadcasted_iota(I32, (L, L), 1)
    causal = cc <= rr
    triu = jnp.where(rr <= cc, 1.0, 0.0).astype(BF16)
    b_t = sum(jnp.dot(piece, triu, preferred_element_type=F32) for piece in _split3(ls_t))
    gates_t = jnp.concatenate([a_t[:N_HEADS_M], b_t[N_HEADS_M:],
                               jnp.zeros((LANES - 2 * N_HEADS_M, L), F32)], axis=0)
    gates_c = gates_t.T

    o_gate = _sigmoid(o_ref[...])
    for hh in range(N_HEADS_M):
        hs = slice(hh * HEAD_DIM_M, (hh + 1) * HEAD_DIM_M)
        qb16 = qk_ref[:, hs]
        kb16 = qk_ref[:, W_M + hh * HEAD_DIM_M:W_M + (hh + 1) * HEAD_DIM_M]
        q = qb16.astype(F32)
        k = kb16.astype(F32)
        v = v_ref[:, hs]
        i_c = gates_c[:, hh:hh + 1]
        bc = gates_c[:, N_HEADS_M + hh:N_HEADS_M + hh + 1]
        i_t = a_t[hh:hh + 1, :]
        bt = b_t[N_HEADS_M + hh:N_HEADS_M + hh + 1, :]
        b_last = bc[L - 1:L, :]
        m_prev = m_scr[hh]
        c_prev = c_scr[hh]
        n_prev = n_scr[hh]

        d_log = jnp.where(causal, bc - bt + i_t, NEG_BIG)
        inter_log = bc + m_prev
        m_j = jnp.maximum(inter_log, jnp.max(d_log, axis=1, keepdims=True))
        s = lax.dot_general(qb16, kb16, (((1,), (1,)), ((), ())),
                            preferred_element_type=F32) * jnp.exp(d_log - m_j)
        w_inter = jnp.exp(inter_log - m_j)
        num = (w_inter * jnp.dot(qb16, c_prev.astype(BF16), preferred_element_type=F32)
               + jnp.dot(s.astype(BF16), v, preferred_element_type=F32))
        den = w_inter * jnp.sum(q * n_prev, axis=1, keepdims=True) + jnp.sum(s, axis=1, keepdims=True)
        hval = num / jnp.maximum(jnp.abs(den), jnp.exp(-m_j))

        g_t = b_last - bt + i_t
        g_c = b_last - bc + i_c
        m_new = jnp.maximum(b_last + m_prev, jnp.max(g_t, axis=1, keepdims=True))
        decay = jnp.exp(b_last + m_prev - m_new)
        kw = k * jnp.exp(g_c - m_new)
        c_scr[hh] = decay * c_prev + jnp.dot(kw.T.astype(BF16), v, preferred_element_type=F32)
        n_scr[hh] = decay * n_prev + jnp.sum(kw, axis=0, keepdims=True)
        m_scr[hh] = m_new

        mu = jnp.mean(hval, axis=1, keepdims=True)
        cen = hval - mu
        var = jnp.mean(cen * cen, axis=1, keepdims=True)
        hn = cen * lax.rsqrt(var + EPS) * hn_ref[:, hs]
        out_ref[:, hs] = (hn * o_gate[:, hs]).astype(BF16)


def _mlstm(qk, v, o_pre, ift, gate_bias, head_norm, *, batch, seq):
    t = batch * seq
    L = MLSTM_CHUNK
    nc = seq // L
    gbt = jnp.broadcast_to(gate_bias.reshape(2 * N_HEADS_M, 1), (2 * N_HEADS_M, L))
    row = lambda w: pl.BlockSpec((L, w), lambda b, c: (b * nc + c, 0))
    return pl.pallas_call(
        functools.partial(_mlstm_kernel, chunk=L),
        out_shape=jax.ShapeDtypeStruct((t, W_M), BF16),
        grid=(batch, nc),
        in_specs=[row(2 * W_M), row(W_M), row(W_M),
                  pl.BlockSpec((2 * N_HEADS_M, L), lambda b, c: (0, b * nc + c)),
                  _resident((2 * N_HEADS_M, L)), _resident((1, W_M))],
        out_specs=row(W_M),
        scratch_shapes=[pltpu.VMEM((N_HEADS_M, HEAD_DIM_M, HEAD_DIM_M), F32),
                        pltpu.VMEM((N_HEADS_M, 1, HEAD_DIM_M), F32),
                        pltpu.VMEM((N_HEADS_M, 1, 1), F32)],
        compiler_params=pltpu.CompilerParams(dimension_semantics=("arbitrary", "arbitrary"),
                                             vmem_limit_bytes=VMEM_LIMIT),
        name="mlstm",
    )(qk, v, o_pre, ift, gbt, head_norm.reshape(1, -1))


def _merge_kernel(x_ref, ya_ref, hm_ref, ga_ref, gm_ref, g_ref, wa_ref, wm_ref, wo_ref, o_ref):
    pa = jnp.dot(ya_ref[...], wa_ref[...], preferred_element_type=F32)
    pm = jnp.dot(hm_ref[...], wm_ref[...], preferred_element_type=F32)
    merged = _sigmoid(ga_ref[...]) * pa + _sigmoid(gm_ref[...]) * pm
    out = jnp.dot(merged.astype(BF16), wo_ref[...], preferred_element_type=F32)
    o_ref[...] = x_ref[...] + g_ref[0] * out


def _merge(x, y_a, h_m, gate_a, gate_m, g, w_a, w_m, w_o, *, seq):
    t, d = x.shape
    tm = MERGE_TM
    per_b = seq // tm
    row = lambda w: pl.BlockSpec((tm, w), lambda i: (i, 0))
    return pl.pallas_call(
        _merge_kernel,
        out_shape=jax.ShapeDtypeStruct((t, d), F32),
        grid=(t // tm,),
        in_specs=[row(d), row(W_A), row(W_M), row(d), row(d),
                  pl.BlockSpec((1, 1, d), lambda i: (i // per_b, 0, 0)),
                  _resident((W_A, d)), _resident((W_M, d)), _resident((d, d))],
        out_specs=row(d),
        compiler_params=pltpu.CompilerParams(dimension_semantics=("arbitrary",),
                                             vmem_limit_bytes=VMEM_LIMIT),
        name="merge",
    )(x, y_a, h_m, gate_a, gate_m, g, w_a.astype(BF16), w_m.astype(BF16), w_o.astype(BF16))


def kernel(x, c, ada_w, ada_b, ffn1_norm, ffn1_w1, ffn1_w3, ffn1_w2, mix_norm, w_in, conv_w, conv_b,
           kv_norm, w_uk, w_uv, mlstm_gate_bias, mlstm_head_norm, rel_bias, w_branch_attn,
           w_branch_mlstm, w_out, ffn2_norm, ffn2_w1, ffn2_w3, ffn2_w2, final_norm):
    batch, seq, d = x.shape
    depth = ada_w.shape[0]
    assert seq % max(FFN_TM, MIX_TM, MERGE_TM, MLSTM_CHUNK, KEY_BLOCK) == 0
    t = batch * seq
    xf = x.reshape(t, d)
    btile, bmax = _bias_tiles(rel_bias)
    for l in range(depth):
        mod = _adaln(c, ada_w[l], ada_b[l]).reshape(batch, 9, 1, d)
        sh1, sc1, g1, sh2, sc2, g2, sh3, sc3, g3 = [mod[:, n] for n in range(9)]
        xf = _ffn(xf, ffn1_norm[l], sh1, sc1, g1, ffn1_w1[l], ffn1_w3[l], ffn1_w2[l], final_norm,
                  seq=seq, final_norm=False)
        wuk_t = jnp.pad(w_uk[l].transpose(0, 2, 1), ((0, 0), (0, LANES - HEAD_DIM_A), (0, 0))).astype(BF16)
        (q_abs, q_idx, k_idx, ckv, ckv_t, w_t, ift, qk_m, v_m, o_pre, gate_a, gate_m) = _mixin(
            xf, mix_norm[l], sh2, sc2, _pack_w_in(w_in[l], d), kv_norm[l], wuk_t, conv_w[l], conv_b[l],
            seq=seq)
        wuv_t = w_uv[l].transpose(0, 2, 1).astype(BF16)
        y_a = _dsa(q_idx, q_abs, w_t, k_idx, ckv, ckv_t, btile, bmax, wuv_t, batch=batch, seq=seq)
        h_m = _mlstm(qk_m, v_m, o_pre, ift, mlstm_gate_bias[l], mlstm_head_norm[l], batch=batch, seq=seq)
        xf = _merge(xf, y_a, h_m, gate_a, gate_m, g2, w_branch_attn[l], w_branch_mlstm[l], w_out[l],
                    seq=seq)
        xf = _ffn(xf, ffn2_norm[l], sh3, sc3, g3, ffn2_w1[l], ffn2_w3[l], ffn2_w2[l], final_norm,
                  seq=seq, final_norm=(l == depth - 1))
    return xf.reshape(batch, seq, d)
```

```python
import functools
import math

import jax
import jax.numpy as jnp
from jax import lax
from jax.experimental import pallas as pl
from jax.experimental.pallas import tpu as pltpu

F32 = jnp.float32
BF16 = jnp.bfloat16
I32 = jnp.int32

LANES = 128
MXU_DIM = 256
VMEM_LIMIT = 56 * 1024 * 1024

N_HEADS_A = 8
HEAD_DIM_A = 64
D_LATENT = 256
N_HEADS_IDX = 8
HEAD_DIM_IDX = 64
TOPK_MAX = 256
Q_BLOCK = 128
N_BUCKETS = 32
MAX_DISTANCE = 128
N_HEADS_M = 4
HEAD_DIM_M = 128
CONV_WIDTH = 4
EPS = 1e-6
IDX_SCALE = (N_HEADS_IDX ** -0.5) * (HEAD_DIM_IDX ** -0.5)
W_A = N_HEADS_A * HEAD_DIM_A
W_M = N_HEADS_M * HEAD_DIM_M

FFN_TM = 512
FFN_CHUNK = 256
MIX_TM = 512
KEY_BLOCK = 256
BLOCK_UNROLL = 2
MLSTM_CHUNK = 256
MERGE_TM = 512
NEG_BIG = -1e30
INT_MIN = -2 ** 31

BIAS_PAD = 384
BIAS_ROWS = KEY_BLOCK + BIAS_PAD
CKVT_ROWS = D_LATENT + 16
LOG2E = math.log2(math.e)


def _sigmoid(x):
    return 1.0 / (1.0 + jnp.exp(-x))


def _log_sigmoid(x):
    return jnp.minimum(x, 0.0) - jnp.log(1.0 + jnp.exp(-jnp.abs(x)))


def _rms_norm(x, gain):
    ms = jnp.mean(x * x, axis=-1, keepdims=True)
    return x * lax.rsqrt(ms + EPS) * gain


def _split3(x):
    hi = x.astype(BF16)
    r1 = x - hi.astype(F32)
    mid = r1.astype(BF16)
    lo = (r1 - mid.astype(F32)).astype(BF16)
    return hi, mid, lo


def _resident(shape):
    nd = len(shape)
    return pl.BlockSpec(shape, lambda *_: (0,) * nd, pipeline_mode=pl.Buffered(1))


def _adaln_kernel(c_ref, w_ref, b_ref, o_ref):
    c = c_ref[...]
    cond = c * _sigmoid(c)
    o_ref[...] = jnp.dot(cond.astype(BF16), w_ref[...].astype(BF16),
                         preferred_element_type=F32) + b_ref[...]


def _adaln(c, ada_w, ada_b):
    b, d = c.shape
    n = ada_w.shape[1]
    rows = 8
    c_pad = jnp.zeros((rows, d), F32).at[:b].set(c)
    tn = 1024
    out = pl.pallas_call(
        _adaln_kernel,
        out_shape=jax.ShapeDtypeStruct((rows, n), F32),
        grid=(n // tn,),
        in_specs=[pl.BlockSpec((rows, d), lambda j: (0, 0)),
                  pl.BlockSpec((d, tn), lambda j: (0, j)),
                  pl.BlockSpec((1, tn), lambda j: (0, j))],
        out_specs=pl.BlockSpec((rows, tn), lambda j: (0, j)),
        compiler_params=pltpu.CompilerParams(dimension_semantics=("arbitrary",),
                                             vmem_limit_bytes=VMEM_LIMIT),
        name="adaln",
    )(c_pad, ada_w, ada_b.reshape(1, n))
    return out[:b]


def _t5_bucket(dist):
    n = jnp.maximum(dist, 0)
    max_exact = N_BUCKETS // 2
    nf = jnp.maximum(n, 1).astype(F32)
    large = max_exact + (jnp.log(nf / max_exact) / math.log(MAX_DISTANCE / max_exact)
                         * (N_BUCKETS - max_exact)).astype(I32)
    large = jnp.minimum(large, N_BUCKETS - 1)
    return jnp.where(n < max_exact, n, large)


def _bias_kernel(rel_ref, tile_ref, max_ref):
    r = lax.broadcasted_iota(I32, (BIAS_ROWS, LANES), 0)
    i = lax.broadcasted_iota(I32, (BIAS_ROWS, LANES), 1)
    bucket = _t5_bucket(i - r + BIAS_PAD)
    for h in range(N_HEADS_A):
        acc = jnp.zeros((BIAS_ROWS, LANES), F32)
        top = rel_ref[0, h] * LOG2E
        for bkt in range(N_BUCKETS):
            val = rel_ref[bkt, h] * LOG2E
            acc = jnp.where(bucket == bkt, val, acc)
            top = jnp.maximum(top, val)
        tile_ref[h] = acc
        max_ref[:, h * LANES:(h + 1) * LANES] = jnp.full((1, LANES), top, F32)


def _bias_tiles(rel_bias):
    return pl.pallas_call(
        _bias_kernel,
        out_shape=(jax.ShapeDtypeStruct((N_HEADS_A, BIAS_ROWS, LANES), F32),
                   jax.ShapeDtypeStruct((1, N_HEADS_A * LANES), F32)),
        in_specs=[pl.BlockSpec(memory_space=pltpu.SMEM)],
        out_specs=(pl.BlockSpec(memory_space=pltpu.VMEM), pl.BlockSpec(memory_space=pltpu.VMEM)),
        name="bias_tiles",
    )(rel_bias)


def _ffn_kernel(x_ref, gain_ref, sh_ref, sc_ref, g_ref, w1_ref, w3_ref, w2_ref, fin_ref, o_ref,
                h_scr, acc_scr, *, n_chunks, final_norm):
    x = x_ref[...]
    h = _rms_norm(x, gain_ref[...]) * (1.0 + sc_ref[0]) + sh_ref[0]
    h_scr[...] = h.astype(BF16)
    for j in range(n_chunks):
        hb = h_scr[...]
        cols = slice(j * FFN_CHUNK, (j + 1) * FFN_CHUNK)
        u1 = jnp.dot(hb, w1_ref[:, cols], preferred_element_type=F32)
        u3 = jnp.dot(hb, w3_ref[:, cols], preferred_element_type=F32)
        a = (u1 * _sigmoid(u1)) * u3
        part = jnp.dot(a.astype(BF16), w2_ref[j], preferred_element_type=F32)
        if j == 0:
            acc_scr[...] = part
        else:
            acc_scr[...] += part
    out = x + (0.5 * g_ref[0]) * acc_scr[...]
    if final_norm:
        out = _rms_norm(out, fin_ref[...])
    o_ref[...] = out


def _ffn(x, gain, sh, sc, g, w1, w3, w2, fin, *, seq, final_norm):
    t, d = x.shape
    dff = w1.shape[1]
    nch = dff // FFN_CHUNK
    w1c = w1.astype(BF16)
    w3c = w3.astype(BF16)
    w2c = w2.astype(BF16).reshape(nch, FFN_CHUNK, d)
    tm = FFN_TM
    per_b = seq // tm
    mod_spec = pl.BlockSpec((1, 1, d), lambda i: (i // per_b, 0, 0))
    return pl.pallas_call(
        functools.partial(_ffn_kernel, n_chunks=nch, final_norm=final_norm),
        out_shape=jax.ShapeDtypeStruct((t, d), F32),
        grid=(t // tm,),
        in_specs=[pl.BlockSpec((tm, d), lambda i: (i, 0)),
                  _resident((1, d)), mod_spec, mod_spec, mod_spec,
                  _resident((d, dff)), _resident((d, dff)),
                  _resident((nch, FFN_CHUNK, d)), _resident((1, d))],
        out_specs=pl.BlockSpec((tm, d), lambda i: (i, 0)),
        scratch_shapes=[pltpu.VMEM((tm, d), BF16), pltpu.VMEM((tm, d), F32)],
        compiler_params=pltpu.CompilerParams(dimension_semantics=("arbitrary",),
                                             vmem_limit_bytes=VMEM_LIMIT),
        name="ffn_final" if final_norm else "ffn",
    )(x, gain.reshape(1, d), sh, sc, g, w1c, w3c, w2c, fin.reshape(1, d))


_C_QA = 0
_C_CKV = _C_QA + N_HEADS_A * LANES
_C_QI = _C_CKV + D_LATENT
_C_KI = _C_QI + N_HEADS_IDX * LANES
_C_SM = _C_KI + LANES
_C_QK = _C_SM + LANES
_C_V = _C_QK + 2 * W_M
_C_O = _C_V + W_M
_C_GA = _C_O + W_M
_C_GM = _C_GA + 1024
_C_END = _C_GM + 1024
_SM_W = 0
_SM_I = N_HEADS_IDX
_SM_F = N_HEADS_IDX + N_HEADS_M


def _pack_w_in(w_in, d_model):
    splits = (W_A, D_LATENT, N_HEADS_IDX * HEAD_DIM_IDX, HEAD_DIM_IDX, N_HEADS_IDX,
              W_M, W_M, W_M, N_HEADS_M, N_HEADS_M, W_M, d_model, d_model)
    offs = [0]
    for s in splits:
        offs.append(offs[-1] + s)
    (q_a, c_kv, q_i, k_i, w_i, q_m, k_m, v_m, i_p, f_p, o_p, g_a, g_m) = [
        w_in[:, offs[n]:offs[n + 1]] for n in range(len(splits))]
    d = w_in.shape[0]

    def pad_heads(w, nh, hd):
        w = w.reshape(d, nh, hd)
        return jnp.pad(w, ((0, 0), (0, 0), (0, LANES - hd))).reshape(d, nh * LANES)

    small = jnp.concatenate([w_i, i_p, f_p], axis=1)
    small = jnp.pad(small, ((0, 0), (0, LANES - small.shape[1])))
    packed = jnp.concatenate([
        pad_heads(q_a, N_HEADS_A, HEAD_DIM_A), c_kv, pad_heads(q_i, N_HEADS_IDX, HEAD_DIM_IDX),
        jnp.pad(k_i, ((0, 0), (0, LANES - HEAD_DIM_IDX))), small, q_m, k_m, v_m, o_p, g_a, g_m], axis=1)
    assert packed.shape[1] == _C_END
    return packed.astype(BF16)


def _mixin_kernel(x_ref, gain_ref, sh_ref, sc_ref, w_ref, kvn_ref, wuk_ref, cw_ref, cb_ref,
                  qabs_ref, qidx_ref, kidx_ref, ckv_ref, ckvt_ref, wt_ref, ift_ref,
                  qk_ref, v_ref, o_ref, ga_ref, gm_ref, h_scr, xe_scr, *, tm, tiles_per_seq):
    nqb = tm // Q_BLOCK

    @pl.when(pl.program_id(0) % tiles_per_seq == 0)
    def _():
        xe_scr[:8] = jnp.zeros((8, xe_scr.shape[1]), F32)

    x = x_ref[...]
    h = _rms_norm(x, gain_ref[...]) * (1.0 + sc_ref[0]) + sh_ref[0]
    h_scr[...] = h.astype(BF16)

    def proj(lo, hi):
        return jnp.dot(h_scr[...], w_ref[:, lo:hi], preferred_element_type=F32)

    qa = proj(_C_QA, _C_CKV)
    scale = HEAD_DIM_A ** -0.5 * LOG2E
    for hh in range(N_HEADS_A):
        q_h = qa[:, hh * LANES:(hh + 1) * LANES].astype(BF16)
        q_abs = jnp.dot(q_h, wuk_ref[hh], preferred_element_type=F32) * scale
        qabs_ref[:, hh] = q_abs.astype(BF16).reshape(nqb, Q_BLOCK, D_LATENT)
    ckv = _rms_norm(proj(_C_CKV, _C_QI), kvn_ref[...])
    ckv_ref[...] = ckv.astype(BF16)
    ckv_t = ckv.T
    ones_row = jnp.where(lax.broadcasted_iota(I32, (CKVT_ROWS - D_LATENT, KEY_BLOCK), 0) == 0, 1.0, 0.0)
    for j in range(tm // KEY_BLOCK):
        ckvt_ref[j, :D_LATENT] = ckv_t[:, j * KEY_BLOCK:(j + 1) * KEY_BLOCK].astype(BF16)
        ckvt_ref[j, D_LATENT:] = ones_row.astype(BF16)
    qi = proj(_C_QI, _C_KI)
    for hh in range(N_HEADS_IDX):
        qidx_ref[:, hh] = qi[:, hh * LANES:(hh + 1) * LANES].astype(BF16).reshape(nqb, Q_BLOCK, LANES)
    kidx_ref[...] = proj(_C_KI, _C_SM).astype(BF16)
    small_t = proj(_C_SM, _C_QK).T
    wt_ref[...] = small_t[_SM_W:_SM_W + N_HEADS_IDX] * IDX_SCALE
    ift_ref[...] = small_t[_SM_I:_SM_I + 2 * N_HEADS_M]
    xe_scr[8:] = proj(_C_QK, _C_V)
    xe = xe_scr[...]
    xq = xe[8:]
    conv = xq * cw_ref[CONV_WIDTH - 1:CONV_WIDTH, :] + cb_ref[...]
    for d in range(1, CONV_WIDTH):
        conv = conv + pltpu.roll(xe, d, axis=0)[8:] * cw_ref[CONV_WIDTH - 1 - d:CONV_WIDTH - d, :]
    xe_scr[:8] = xe_scr[tm:]
    qk = conv * _sigmoid(conv)
    qk_ref[:, :W_M] = qk[:, :W_M].astype(BF16)
    qk_ref[:, W_M:] = (qk[:, W_M:] * (HEAD_DIM_M ** -0.5)).astype(BF16)
    v_ref[...] = proj(_C_V, _C_O).astype(BF16)
    o_ref[...] = proj(_C_O, _C_GA)
    ga_ref[...] = proj(_C_GA, _C_GM)
    gm_ref[...] = proj(_C_GM, _C_END)


def _mixin(x, gain, sh, sc, w_packed, kv_norm, wuk_t, conv_w, conv_b, *, seq):
    t, d = x.shape
    tm = MIX_TM
    per_b = seq // tm
    nqb = tm // Q_BLOCK
    row = lambda w: pl.BlockSpec((tm, w), lambda i: (i, 0))
    mod_spec = pl.BlockSpec((1, 1, d), lambda i: (i // per_b, 0, 0))
    out_shape = (
        jax.ShapeDtypeStruct((t // Q_BLOCK, N_HEADS_A, Q_BLOCK, D_LATENT), BF16),
        jax.ShapeDtypeStruct((t // Q_BLOCK, N_HEADS_IDX, Q_BLOCK, LANES), BF16),
        jax.ShapeDtypeStruct((t, LANES), BF16),
        jax.ShapeDtypeStruct((t, D_LATENT), BF16),
        jax.ShapeDtypeStruct((t // KEY_BLOCK, CKVT_ROWS, KEY_BLOCK), BF16),
        jax.ShapeDtypeStruct((N_HEADS_IDX, t), F32),
        jax.ShapeDtypeStruct((2 * N_HEADS_M, t), F32),
        jax.ShapeDtypeStruct((t, 2 * W_M), BF16),
        jax.ShapeDtypeStruct((t, W_M), BF16),
        jax.ShapeDtypeStruct((t, W_M), F32),
        jax.ShapeDtypeStruct((t, d), F32),
        jax.ShapeDtypeStruct((t, d), F32),
    )
    out_specs = (
        pl.BlockSpec((nqb, N_HEADS_A, Q_BLOCK, D_LATENT), lambda i: (i, 0, 0, 0)),
        pl.BlockSpec((nqb, N_HEADS_IDX, Q_BLOCK, LANES), lambda i: (i, 0, 0, 0)),
        row(LANES), row(D_LATENT),
        pl.BlockSpec((tm // KEY_BLOCK, CKVT_ROWS, KEY_BLOCK), lambda i: (i, 0, 0)),
        pl.BlockSpec((N_HEADS_IDX, tm), lambda i: (0, i)),
        pl.BlockSpec((2 * N_HEADS_M, tm), lambda i: (0, i)),
        row(2 * W_M), row(W_M), row(W_M), row(d), row(d),
    )
    return pl.pallas_call(
        functools.partial(_mixin_kernel, tm=tm, tiles_per_seq=per_b),
        out_shape=out_shape,
        grid=(t // tm,),
        in_specs=[pl.BlockSpec((tm, d), lambda i: (i, 0)), _resident((1, d)), mod_spec, mod_spec,
                  _resident((d, _C_END)), _resident((1, D_LATENT)),
                  _resident((N_HEADS_A, LANES, D_LATENT)),
                  _resident((CONV_WIDTH, 2 * W_M)), _resident((1, 2 * W_M))],
        out_specs=out_specs,
        scratch_shapes=[pltpu.VMEM((tm, d), BF16), pltpu.VMEM((tm + 8, 2 * W_M), F32)],
        compiler_params=pltpu.CompilerParams(dimension_semantics=("arbitrary",),
                                             vmem_limit_bytes=VMEM_LIMIT),
        name="mixin",
    )(x, gain.reshape(1, d), sh, sc, w_packed, kv_norm.reshape(1, D_LATENT), wuk_t,
      conv_w, conv_b.reshape(1, -1))


def _sortable_key(score):
    bits = pltpu.bitcast(score, I32)
    bits = jnp.where(bits == INT_MIN, 0, bits)
    return jnp.where(bits < 0, bits ^ 0x7FFFFFFF, bits)


def _bit_transpose32(words):
    v = list(words)
    j, m = 16, 0x0000FFFF
    while j:
        k = 0
        while k < 32:
            t = (v[k] ^ lax.shift_right_logical(v[k + j], jnp.int32(j))) & m
            v[k] = v[k] ^ t
            v[k + j] = v[k + j] ^ (t << j)
            k = (k + j + 1) & ~j
        j >>= 1
        m = (m ^ (m << j)) & 0x7FFFFFFF
    return v


def _dsa_kernel(qidx_ref, qabs_ref, wt_ref, kidx_ref, ckv_ref, ckvt_ref, btile_ref, bmax_ref, wuvt_ref,
                out_ref, keys_scr, planes_scr, cand_scr, tau_scr, acc_scr, m_scr, lta_scr, ltc_scr, kmax_scr,
                qta_scr, qtc_scr, *, topk, n_qb):
    kb_sz = KEY_BLOCK
    step = pl.program_id(1)
    has_c = step >= 1
    qa = jnp.minimum(step, n_qb - 1)
    qc = jnp.maximum(step - 1, 0)
    slot_a = step & 1
    slot_c = 1 - slot_a
    n_a = qa // (kb_sz // Q_BLOCK) + 1
    n_c = qc // (kb_sz // Q_BLOCK) + 1
    qa0 = qa * Q_BLOCK
    qc0 = qc * Q_BLOCK
    row_id = lax.broadcasted_iota(I32, (kb_sz, LANES), 0)
    lane_id = lax.broadcasted_iota(I32, (kb_sz, LANES), 1)
    n_blocks = keys_scr.shape[1]
    n_groups = N_HEADS_A // 2
    pair = 2 * LANES
    ones8 = jnp.ones((8, D_LATENT), BF16)

    @pl.when(step == 0)
    def _():
        keys_scr[1, 0] = jnp.full((kb_sz, LANES), INT_MIN, I32)
        tau_scr[1] = jnp.zeros((1, LANES), I32)
        planes_scr[...] = jnp.zeros(planes_scr.shape, I32)

        def kn_body(kb, mx):
            c = ckv_ref[kb].astype(F32)
            n2 = lax.dot_general(ones8, (c * c).astype(BF16), (((1,), (1,)), ((), ())),
                                 preferred_element_type=F32)
            return jnp.maximum(mx, n2[0:1])
        mx = lax.fori_loop(0, n_blocks, kn_body, jnp.zeros((1, kb_sz), F32))
        kmax_scr[...] = jnp.max(mx, axis=1, keepdims=True)

    for g in range(n_groups):
        qi_g = qidx_ref[0, 2 * g:2 * g + 2].reshape(2 * Q_BLOCK, LANES).astype(F32)
        qta_scr[g] = qi_g.T.astype(BF16)
        qa_g = qabs_ref[0, 2 * g:2 * g + 2].reshape(2 * Q_BLOCK, D_LATENT).astype(F32)
        qtc_scr[g] = qa_g.T.astype(BF16)

    def idx_dot(kb, g):
        return jnp.dot(kidx_ref[kb], qta_scr[g], preferred_element_type=F32)

    def logits(kb, g):
        return jnp.dot(ckv_ref[kb], qtc_scr[g], preferred_element_type=F32)

    def bias_start(kb):
        delta = jnp.minimum(qc0 - kb * kb_sz, BIAS_PAD)
        return pl.multiple_of(BIAS_PAD - delta, LANES)

    qn2 = []
    for g in range(n_groups):
        q_g = qabs_ref[0, 2 * g:2 * g + 2].reshape(2 * Q_BLOCK, D_LATENT).astype(F32)
        qn2.append(lax.dot_general(ones8, (q_g * q_g).astype(BF16), (((1,), (1,)), ((), ())),
                                   preferred_element_type=F32)[0:1])
    bound = jnp.sqrt(jnp.concatenate(qn2, axis=1) * kmax_scr[...]) * 1.02 + bmax_ref[...] + 1e-3
    tau_c = tau_scr[slot_c]
    w_t = wt_ref[...]

    acc_scr[...] = jnp.zeros(acc_scr.shape, F32)
    for g in range(n_groups):
        lta_scr[:, g * pair:(g + 1) * pair] = idx_dot(0, g)
        ltc_scr[:, g * pair:(g + 1) * pair] = logits(0, g)

    bound_far = bound - jnp.concatenate([btile_ref[hh, 0:1, :] for hh in range(N_HEADS_A)], axis=1)

    def block_step(kb_raw, far):
        kb = jnp.minimum(kb_raw, n_a - 1)
        kb_next = jnp.minimum(kb_raw + 1, n_a - 1)
        kc = jnp.minimum(kb_raw, n_c - 1)
        kc_next = jnp.minimum(kb_raw + 1, n_c - 1)
        thr = jnp.where(has_c & (kb_raw < n_c), tau_c - 1, jnp.int32(2 ** 31 - 1))
        sel = keys_scr[slot_c, kc] > thr
        ct_blk = ckvt_ref[kc]
        start = bias_start(kc)
        ref_pt = bound_far if far else bound
        score = jnp.zeros((kb_sz, LANES), F32)
        for g in range(n_groups):
            s_t = lta_scr[:, g * pair:(g + 1) * pair]
            for j in range(2):
                hh = 2 * g + j
                score = score + jnp.maximum(s_t[:, j * LANES:(j + 1) * LANES], 0.0) * w_t[hh:hh + 1, :]
            lta_scr[:, g * pair:(g + 1) * pair] = idx_dot(kb_next, g)
            lt = ltc_scr[:, g * pair:(g + 1) * pair]
            ps = []
            for j in range(2):
                hh = 2 * g + j
                piece = lt[:, j * LANES:(j + 1) * LANES]
                if not far:
                    piece = piece + btile_ref[hh, pl.ds(start, kb_sz), :]
                ps.append(jnp.exp2(jnp.where(sel, piece, NEG_BIG) - ref_pt[:, hh * LANES:(hh + 1) * LANES]))
            ltc_scr[:, g * pair:(g + 1) * pair] = logits(kc_next, g)
            acc_scr[g] += jnp.dot(ct_blk, jnp.concatenate(ps, axis=1).astype(BF16),
                                  preferred_element_type=F32)
        valid = (kb * kb_sz + row_id) <= (qa0 + lane_id)
        keys = jnp.where(valid, _sortable_key(score), INT_MIN)
        keys_scr[slot_a, kb] = keys
        v = keys ^ INT_MIN
        words = _bit_transpose32([v[8 * i:8 * (i + 1), :] for i in range(32)])
        for bit in range(32):
            planes_scr[bit, kb] = words[31 - bit]

    def block_body(far, it, carry):
        for u in range(BLOCK_UNROLL):
            block_step(it * BLOCK_UNROLL + u, far)
        return carry

    n_far_iters = jnp.maximum(n_c - 2, 0) // BLOCK_UNROLL
    lax.fori_loop(0, n_far_iters, functools.partial(block_body, True), 0)
    lax.fori_loop(n_far_iters, (n_a + BLOCK_UNROLL - 1) // BLOCK_UNROLL,
                  functools.partial(block_body, False), 0)
    n_kb = n_a


    blk_id = lax.broadcasted_iota(I32, cand_scr.shape, 0)
    cand_scr[...] = jnp.where(blk_id < n_kb, -1, 0)

    def bit_body(it, carry):
        above, tau_u = carry
        bit = 31 - it
        ones = cand_scr[...] & planes_scr[bit]
        c1 = jnp.sum(jnp.sum(lax.population_count(ones), axis=0), axis=0, keepdims=True)
        take = (above + c1) >= topk
        cand_scr[...] = jnp.where(take, ones, cand_scr[...] ^ ones)
        above = jnp.where(take, above, above + c1)
        tau_u = jnp.where(take, tau_u | (jnp.int32(1) << bit), tau_u)
        return above, tau_u

    zero = jnp.zeros((1, LANES), I32)
    n_gt, tau_u = lax.fori_loop(0, 32, bit_body, (zero, zero))
    tau = tau_u ^ INT_MIN
    n_eq = jnp.sum(jnp.sum(lax.population_count(cand_scr[...]), axis=0), axis=0, keepdims=True)

    need = topk - n_gt
    overflow = n_eq > need
    seq_bits = max(1, (n_blocks * kb_sz - 1).bit_length())

    @pl.when(jnp.max(jnp.where(overflow, 1, 0)) > 0)
    def _():
        def count_ties_before(trial):
            def body(kb, acc):
                hit = jnp.where((keys_scr[slot_a, kb] == tau) & ((kb * kb_sz + row_id) < trial), 1, 0)
                return acc + jnp.sum(hit.reshape(kb_sz // 8, 8, LANES), axis=0)
            acc = lax.fori_loop(0, n_kb, body, jnp.zeros((8, LANES), I32))
            return jnp.sum(acc, axis=0, keepdims=True)

        def idx_body(it, jc):
            trial = jc | (jnp.int32(1) << (seq_bits - 1 - it))
            return jnp.where(count_ties_before(trial) < need, trial, jc)

        j_cut = lax.fori_loop(0, seq_bits, idx_body, jnp.zeros((1, LANES), I32))

        def demote_body(kb, carry):
            k = keys_scr[slot_a, kb]
            drop = overflow & (k == tau) & ((kb * kb_sz + row_id) > j_cut)
            keys_scr[slot_a, kb] = jnp.where(drop, INT_MIN, k)
            return carry

        lax.fori_loop(0, n_kb, demote_body, 0)

    tau_scr[slot_a] = jnp.maximum(tau, INT_MIN + 1)

    l_min = jnp.min(jnp.concatenate([acc_scr[g, D_LATENT:D_LATENT + 1, :] for g in range(n_groups)], axis=1))

    @pl.when(has_c & jnp.logical_not(l_min >= 2.0 ** -80))
    def _():
        m_scr[...] = jnp.full(m_scr.shape, NEG_BIG, F32)
        acc_scr[...] = jnp.zeros(acc_scr.shape, F32)

        def exact_body(kb, carry):
            sel = keys_scr[slot_c, kb] >= tau_c
            ct_blk = ckvt_ref[kb]
            start = bias_start(kb)
            for g in range(n_groups):
                lt = logits(kb, g)
                ps, alphas = [], []
                for j in range(2):
                    hh = 2 * g + j
                    sl = slice(hh * LANES, (hh + 1) * LANES)
                    piece = lt[:, j * LANES:(j + 1) * LANES] + btile_ref[hh, pl.ds(start, kb_sz), :]
                    masked = jnp.where(sel, piece, NEG_BIG)
                    m_old = m_scr[:, sl]
                    m_new = jnp.maximum(m_old, jnp.max(masked, axis=0, keepdims=True))
                    m_scr[:, sl] = m_new
                    alphas.append(jnp.exp2(m_old - m_new))
                    ps.append(jnp.exp2(masked - m_new))
                pv = jnp.dot(ct_blk, jnp.concatenate(ps, axis=1).astype(BF16), preferred_element_type=F32)
                acc_scr[g] = jnp.concatenate(alphas, axis=1) * acc_scr[g] + pv
            return carry

        lax.fori_loop(0, n_c, exact_body, 0)

    @pl.when(has_c)
    def _():
        ys = []
        for hh in range(N_HEADS_A):
            acc_h = acc_scr[hh // 2, :, (hh % 2) * LANES:(hh % 2 + 1) * LANES]
            o_h = acc_h[:D_LATENT] * (1.0 / acc_h[D_LATENT:D_LATENT + 1])
            ys.append(jnp.dot(wuvt_ref[hh], o_h.astype(BF16), preferred_element_type=F32))
        y_t = jnp.concatenate(ys, axis=0)
        out_ref[...] = y_t.T.astype(BF16)


def _dsa(q_idx, q_abs, w_t, k_idx, ckv, ckv_t, btile, bmax, wuv_t, *, batch, seq):
    t = batch * seq
    nqb = seq // Q_BLOCK
    nkb = seq // KEY_BLOCK
    topk = min(TOPK_MAX, seq // 4)
    k_idx3 = k_idx.reshape(t // KEY_BLOCK, KEY_BLOCK, LANES)
    ckv3 = ckv.reshape(t // KEY_BLOCK, KEY_BLOCK, D_LATENT)
    per_batch = lambda shape: pl.BlockSpec(shape, lambda b, q: (b,) + (0,) * (len(shape) - 1),
                                           pipeline_mode=pl.Buffered(1))
    scored = lambda b, s: b * nqb + jnp.minimum(s, nqb - 1)
    attended = lambda b, s: b * nqb + jnp.maximum(s - 1, 0)
    return pl.pallas_call(
        functools.partial(_dsa_kernel, topk=topk, n_qb=nqb),
        out_shape=jax.ShapeDtypeStruct((t, W_A), BF16),
        grid=(batch, nqb + 1),
        in_specs=[pl.BlockSpec((1, N_HEADS_IDX, Q_BLOCK, LANES), lambda b, s: (scored(b, s), 0, 0, 0)),
                  pl.BlockSpec((1, N_HEADS_A, Q_BLOCK, D_LATENT), lambda b, s: (attended(b, s), 0, 0, 0)),
                  pl.BlockSpec((N_HEADS_IDX, Q_BLOCK), lambda b, s: (0, scored(b, s))),
                  per_batch((nkb, KEY_BLOCK, LANES)),
                  per_batch((nkb, KEY_BLOCK, D_LATENT)),
                  per_batch((nkb, CKVT_ROWS, KEY_BLOCK)),
                  _resident((N_HEADS_A, BIAS_ROWS, LANES)),
                  _resident((1, N_HEADS_A * LANES)),
                  _resident((N_HEADS_A, HEAD_DIM_A, D_LATENT))],
        out_specs=pl.BlockSpec((Q_BLOCK, W_A), lambda b, s: (attended(b, s), 0)),
        scratch_shapes=[pltpu.VMEM((2, nkb, KEY_BLOCK, LANES), I32),
                        pltpu.VMEM((32, nkb, 8, LANES), I32),
                        pltpu.VMEM((nkb, 8, LANES), I32),
                        pltpu.VMEM((2, 1, LANES), I32),
                        pltpu.VMEM((N_HEADS_A // 2, CKVT_ROWS, 2 * LANES), F32),
                        pltpu.VMEM((1, N_HEADS_A * LANES), F32),
                        pltpu.VMEM((KEY_BLOCK, N_HEADS_A * LANES), F32),
                        pltpu.VMEM((KEY_BLOCK, N_HEADS_A * LANES), F32),
                        pltpu.VMEM((1, 1), F32),
                        pltpu.VMEM((N_HEADS_IDX // 2, LANES, 2 * LANES), BF16),
                        pltpu.VMEM((N_HEADS_A // 2, D_LATENT, 2 * LANES), BF16)],
        compiler_params=pltpu.CompilerParams(dimension_semantics=("arbitrary", "arbitrary"),
                                             vmem_limit_bytes=VMEM_LIMIT),
        name="dsa",
    )(q_idx, q_abs, w_t, k_idx3, ckv3, ckv_t, btile, bmax, wuv_t)


def _mlstm_kernel(qk_ref, v_ref, o_ref, ift_ref, gbt_ref, hn_ref,
                  out_ref, c_scr, n_scr, m_scr, *, chunk):
    L = chunk
    step = pl.program_id(1)

    @pl.when(step == 0)
    def _():
        c_scr[...] = jnp.zeros(c_scr.shape, F32)
        n_scr[...] = jnp.zeros(n_scr.shape, F32)
        m_scr[...] = jnp.zeros(m_scr.shape, F32)

    a_t = ift_ref[...] + gbt_ref[...]
    ls_t = _log_sigmoid(a_t)
    rr = lax.broadcasted_iota(I32, (L, L), 0)
    cc = lax.broadcasted_iota(I32, (L, L), 1)
    causal = cc <= rr
    triu = jnp.where(rr <= cc, 1.0, 0.0).astype(BF16)
    b_t = sum(jnp.dot(piece, triu, preferred_element_type=F32) for piece in _split3(ls_t))
    gates_t = jnp.concatenate([a_t[:N_HEADS_M], b_t[N_HEADS_M:],
                               jnp.zeros((LANES - 2 * N_HEADS_M, L), F32)], axis=0)
    gates_c = gates_t.T

    o_gate = _sigmoid(o_ref[...])
    for hh in range(N_HEADS_M):
        hs = slice(hh * HEAD_DIM_M, (hh + 1) * HEAD_DIM_M)
        qb16 = qk_ref[:, hs]
        kb16 = qk_ref[:, W_M + hh * HEAD_DIM_M:W_M + (hh + 1) * HEAD_DIM_M]
        q = qb16.astype(F32)
        k = kb16.astype(F32)
        v = v_ref[:, hs]
        i_c = gates_c[:, hh:hh + 1]
        bc = gates_c[:, N_HEADS_M + hh:N_HEADS_M + hh + 1]
        i_t = a_t[hh:hh + 1, :]
        bt = b_t[N_HEADS_M + hh:N_HEADS_M + hh + 1, :]
        b_last = bc[L - 1:L, :]
        m_prev = m_scr[hh]
        c_prev = c_scr[hh]
        n_prev = n_scr[hh]

        d_log = jnp.where(causal, bc - bt + i_t, NEG_BIG)
        inter_log = bc + m_prev
        m_j = jnp.maximum(inter_log, jnp.max(d_log, axis=1, keepdims=True))
        s = lax.dot_general(qb16, kb16, (((1,), (1,)), ((), ())),
                            preferred_element_type=F32) * jnp.exp(d_log - m_j)
        w_inter = jnp.exp(inter_log - m_j)
        num = (w_inter * jnp.dot(qb16, c_prev.astype(BF16), preferred_element_type=F32)
               + jnp.dot(s.astype(BF16), v, preferred_element_type=F32))
        den = w_inter * jnp.sum(q * n_prev, axis=1, keepdims=True) + jnp.sum(s, axis=1, keepdims=True)
        hval = num / jnp.maximum(jnp.abs(den), jnp.exp(-m_j))

        g_t = b_last - bt + i_t
        g_c = b_last - bc + i_c
        m_new = jnp.maximum(b_last + m_prev, jnp.max(g_t, axis=1, keepdims=True))
        decay = jnp.exp(b_last + m_prev - m_new)
        kw = k * jnp.exp(g_c - m_new)
        c_scr[hh] = decay * c_prev + jnp.dot(kw.T.astype(BF16), v, preferred_element_type=F32)
        n_scr[hh] = decay * n_prev + jnp.sum(kw, axis=0, keepdims=True)
        m_scr[hh] = m_new

        mu = jnp.mean(hval, axis=1, keepdims=True)
        cen = hval - mu
        var = jnp.mean(cen * cen, axis=1, keepdims=True)
        hn = cen * lax.rsqrt(var + EPS) * hn_ref[:, hs]
        out_ref[:, hs] = (hn * o_gate[:, hs]).astype(BF16)


def _mlstm(qk, v, o_pre, ift, gate_bias, head_norm, *, batch, seq):
    t = batch * seq
    L = MLSTM_CHUNK
    nc = seq // L
    gbt = jnp.broadcast_to(gate_bias.reshape(2 * N_HEADS_M, 1), (2 * N_HEADS_M, L))
    row = lambda w: pl.BlockSpec((L, w), lambda b, c: (b * nc + c, 0))
    return pl.pallas_call(
        functools.partial(_mlstm_kernel, chunk=L),
        out_shape=jax.ShapeDtypeStruct((t, W_M), BF16),
        grid=(batch, nc),
        in_specs=[row(2 * W_M), row(W_M), row(W_M),
                  pl.BlockSpec((2 * N_HEADS_M, L), lambda b, c: (0, b * nc + c)),
                  _resident((2 * N_HEADS_M, L)), _resident((1, W_M))],
        out_specs=row(W_M),
        scratch_shapes=[pltpu.VMEM((N_HEADS_M, HEAD_DIM_M, HEAD_DIM_M), F32),
                        pltpu.VMEM((N_HEADS_M, 1, HEAD_DIM_M), F32),
                        pltpu.VMEM((N_HEADS_M, 1, 1), F32)],
        compiler_params=pltpu.CompilerParams(dimension_semantics=("arbitrary", "arbitrary"),
                                             vmem_limit_bytes=VMEM_LIMIT),
        name="mlstm",
    )(qk, v, o_pre, ift, gbt, head_norm.reshape(1, -1))


def _merge_kernel(x_ref, ya_ref, hm_ref, ga_ref, gm_ref, g_ref, wa_ref, wm_ref, wo_ref, o_ref):
    pa = jnp.dot(ya_ref[...], wa_ref[...], preferred_element_type=F32)
    pm = jnp.dot(hm_ref[...], wm_ref[...], preferred_element_type=F32)
    merged = _sigmoid(ga_ref[...]) * pa + _sigmoid(gm_ref[...]) * pm
    out = jnp.dot(merged.astype(BF16), wo_ref[...], preferred_element_type=F32)
    o_ref[...] = x_ref[...] + g_ref[0] * out


def _merge(x, y_a, h_m, gate_a, gate_m, g, w_a, w_m, w_o, *, seq):
    t, d = x.shape
    tm = MERGE_TM
    per_b = seq // tm
    row = lambda w: pl.BlockSpec((tm, w), lambda i: (i, 0))
    return pl.pallas_call(
        _merge_kernel,
        out_shape=jax.ShapeDtypeStruct((t, d), F32),
        grid=(t // tm,),
        in_specs=[row(d), row(W_A), row(W_M), row(d), row(d),
                  pl.BlockSpec((1, 1, d), lambda i: (i // per_b, 0, 0)),
                  _resident((W_A, d)), _resident((W_M, d)), _resident((d, d))],
        out_specs=row(d),
        compiler_params=pltpu.CompilerParams(dimension_semantics=("arbitrary",),
                                             vmem_limit_bytes=VMEM_LIMIT),
        name="merge",
    )(x, y_a, h_m, gate_a, gate_m, g, w_a.astype(BF16), w_m.astype(BF16), w_o.astype(BF16))


def kernel(x, c, ada_w, ada_b, ffn1_norm, ffn1_w1, ffn1_w3, ffn1_w2, mix_norm, w_in, conv_w, conv_b,
           kv_norm, w_uk, w_uv, mlstm_gate_bias, mlstm_head_norm, rel_bias, w_branch_attn,
           w_branch_mlstm, w_out, ffn2_norm, ffn2_w1, ffn2_w3, ffn2_w2, final_norm):
    batch, seq, d = x.shape
    depth = ada_w.shape[0]
    assert seq % max(FFN_TM, MIX_TM, MERGE_TM, MLSTM_CHUNK, KEY_BLOCK) == 0
    t = batch * seq
    xf = x.reshape(t, d)
    btile, bmax = _bias_tiles(rel_bias)
    for l in range(depth):
        mod = _adaln(c, ada_w[l], ada_b[l]).reshape(batch, 9, 1, d)
        sh1, sc1, g1, sh2, sc2, g2, sh3, sc3, g3 = [mod[:, n] for n in range(9)]
        xf = _ffn(xf, ffn1_norm[l], sh1, sc1, g1, ffn1_w1[l], ffn1_w3[l], ffn1_w2[l], final_norm,
                  seq=seq, final_norm=False)
        wuk_t = jnp.pad(w_uk[l].transpose(0, 2, 1), ((0, 0), (0, LANES - HEAD_DIM_A), (0, 0))).astype(BF16)
        (q_abs, q_idx, k_idx, ckv, ckv_t, w_t, ift, qk_m, v_m, o_pre, gate_a, gate_m) = _mixin(
            xf, mix_norm[l], sh2, sc2, _pack_w_in(w_in[l], d), kv_norm[l], wuk_t, conv_w[l], conv_b[l],
            seq=seq)
        wuv_t = w_uv[l].transpose(0, 2, 1).astype(BF16)
        y_a = _dsa(q_idx, q_abs, w_t, k_idx, ckv, ckv_t, btile, bmax, wuv_t, batch=batch, seq=seq)
        h_m = _mlstm(qk_m, v_m, o_pre, ift, mlstm_gate_bias[l], mlstm_head_norm[l], batch=batch, seq=seq)
        xf = _merge(xf, y_a, h_m, gate_a, gate_m, g2, w_branch_attn[l], w_branch_mlstm[l], w_out[l],
                    seq=seq)
        xf = _ffn(xf, ffn2_norm[l], sh3, sc3, g3, ffn2_w1[l], ffn2_w3[l], ffn2_w2[l], final_norm,
                  seq=seq, final_norm=(l == depth - 1))
    return xf.reshape(batch, seq, d)
```

```python
import functools
import math

import jax
import jax.numpy as jnp
from jax import lax
from jax.experimental import pallas as pl
from jax.experimental.pallas import tpu as pltpu

F32 = jnp.float32
BF16 = jnp.bfloat16
I32 = jnp.int32

LANES = 128
MXU_DIM = 256
VMEM_LIMIT = 56 * 1024 * 1024

N_HEADS_A = 8
HEAD_DIM_A = 64
D_LATENT = 256
N_HEADS_IDX = 8
HEAD_DIM_IDX = 64
TOPK_MAX = 256
Q_BLOCK = 128
N_BUCKETS = 32
MAX_DISTANCE = 128
N_HEADS_M = 4
HEAD_DIM_M = 128
CONV_WIDTH = 4
EPS = 1e-6
IDX_SCALE = (N_HEADS_IDX ** -0.5) * (HEAD_DIM_IDX ** -0.5)
W_A = N_HEADS_A * HEAD_DIM_A
W_M = N_HEADS_M * HEAD_DIM_M

FFN_TM = 512
FFN_CHUNK = 256
MIX_TM = 512
KEY_BLOCK = 256
BLOCK_UNROLL = 2
MLSTM_CHUNK = 256
MERGE_TM = 512
NEG_BIG = -1e30
INT_MIN = -2 ** 31

BIAS_PAD = 384
BIAS_ROWS = KEY_BLOCK + BIAS_PAD
CKVT_ROWS = D_LATENT + 16
LOG2E = math.log2(math.e)


def _sigmoid(x):
    return 1.0 / (1.0 + jnp.exp(-x))


def _log_sigmoid(x):
    return jnp.minimum(x, 0.0) - jnp.log(1.0 + jnp.exp(-jnp.abs(x)))


def _rms_norm(x, gain):
    ms = jnp.mean(x * x, axis=-1, keepdims=True)
    return x * lax.rsqrt(ms + EPS) * gain


def _split3(x):
    hi = x.astype(BF16)
    r1 = x - hi.astype(F32)
    mid = r1.astype(BF16)
    lo = (r1 - mid.astype(F32)).astype(BF16)
    return hi, mid, lo


def _resident(shape):
    nd = len(shape)
    return pl.BlockSpec(shape, lambda *_: (0,) * nd, pipeline_mode=pl.Buffered(1))


def _adaln_kernel(c_ref, w_ref, b_ref, o_ref):
    c = c_ref[...]
    cond = c * _sigmoid(c)
    o_ref[...] = jnp.dot(cond.astype(BF16), w_ref[...].astype(BF16),
                         preferred_element_type=F32) + b_ref[...]


def _adaln(c, ada_w, ada_b):
    b, d = c.shape
    n = ada_w.shape[1]
    rows = 8
    c_pad = jnp.zeros((rows, d), F32).at[:b].set(c)
    tn = 1024
    out = pl.pallas_call(
        _adaln_kernel,
        out_shape=jax.ShapeDtypeStruct((rows, n), F32),
        grid=(n // tn,),
        in_specs=[pl.BlockSpec((rows, d), lambda j: (0, 0)),
                  pl.BlockSpec((d, tn), lambda j: (0, j)),
                  pl.BlockSpec((1, tn), lambda j: (0, j))],
        out_specs=pl.BlockSpec((rows, tn), lambda j: (0, j)),
        compiler_params=pltpu.CompilerParams(dimension_semantics=("arbitrary",),
                                             vmem_limit_bytes=VMEM_LIMIT),
        name="adaln",
    )(c_pad, ada_w, ada_b.reshape(1, n))
    return out[:b]


def _t5_bucket(dist):
    n = jnp.maximum(dist, 0)
    max_exact = N_BUCKETS // 2
    nf = jnp.maximum(n, 1).astype(F32)
    large = max_exact + (jnp.log(nf / max_exact) / math.log(MAX_DISTANCE / max_exact)
                         * (N_BUCKETS - max_exact)).astype(I32)
    large = jnp.minimum(large, N_BUCKETS - 1)
    return jnp.where(n < max_exact, n, large)


def _bias_kernel(rel_ref, tile_ref, max_ref):
    r = lax.broadcasted_iota(I32, (BIAS_ROWS, LANES), 0)
    i = lax.broadcasted_iota(I32, (BIAS_ROWS, LANES), 1)
    bucket = _t5_bucket(i - r + BIAS_PAD)
    for h in range(N_HEADS_A):
        acc = jnp.zeros((BIAS_ROWS, LANES), F32)
        top = rel_ref[0, h] * LOG2E
        for bkt in range(N_BUCKETS):
            val = rel_ref[bkt, h] * LOG2E
            acc = jnp.where(bucket == bkt, val, acc)
            top = jnp.maximum(top, val)
        tile_ref[h] = acc
        max_ref[:, h * LANES:(h + 1) * LANES] = jnp.full((1, LANES), top, F32)


def _bias_tiles(rel_bias):
    return pl.pallas_call(
        _bias_kernel,
        out_shape=(jax.ShapeDtypeStruct((N_HEADS_A, BIAS_ROWS, LANES), F32),
                   jax.ShapeDtypeStruct((1, N_HEADS_A * LANES), F32)),
        in_specs=[pl.BlockSpec(memory_space=pltpu.SMEM)],
        out_specs=(pl.BlockSpec(memory_space=pltpu.VMEM), pl.BlockSpec(memory_space=pltpu.VMEM)),
        name="bias_tiles",
    )(rel_bias)


def _ffn_kernel(x_ref, gain_ref, sh_ref, sc_ref, g_ref, w1_ref, w3_ref, w2_ref, fin_ref, o_ref,
                h_scr, acc_scr, *, n_chunks, final_norm):
    x = x_ref[...]
    h = _rms_norm(x, gain_ref[...]) * (1.0 + sc_ref[0]) + sh_ref[0]
    h_scr[...] = h.astype(BF16)
    for j in range(n_chunks):
        hb = h_scr[...]
        cols = slice(j * FFN_CHUNK, (j + 1) * FFN_CHUNK)
        u1 = jnp.dot(hb, w1_ref[:, cols], preferred_element_type=F32)
        u3 = jnp.dot(hb, w3_ref[:, cols], preferred_element_type=F32)
        a = (u1 * _sigmoid(u1)) * u3
        part = jnp.dot(a.astype(BF16), w2_ref[j], preferred_element_type=F32)
        if j == 0:
            acc_scr[...] = part
        else:
            acc_scr[...] += part
    out = x + (0.5 * g_ref[0]) * acc_scr[...]
    if final_norm:
        out = _rms_norm(out, fin_ref[...])
    o_ref[...] = out


def _ffn(x, gain, sh, sc, g, w1, w3, w2, fin, *, seq, final_norm):
    t, d = x.shape
    dff = w1.shape[1]
    nch = dff // FFN_CHUNK
    w1c = w1.astype(BF16)
    w3c = w3.astype(BF16)
    w2c = w2.astype(BF16).reshape(nch, FFN_CHUNK, d)
    tm = FFN_TM
    per_b = seq // tm
    mod_spec = pl.BlockSpec((1, 1, d), lambda i: (i // per_b, 0, 0))
    return pl.pallas_call(
        functools.partial(_ffn_kernel, n_chunks=nch, final_norm=final_norm),
        out_shape=jax.ShapeDtypeStruct((t, d), F32),
        grid=(t // tm,),
        in_specs=[pl.BlockSpec((tm, d), lambda i: (i, 0)),
                  _resident((1, d)), mod_spec, mod_spec, mod_spec,
                  _resident((d, dff)), _resident((d, dff)),
                  _resident((nch, FFN_CHUNK, d)), _resident((1, d))],
        out_specs=pl.BlockSpec((tm, d), lambda i: (i, 0)),
        scratch_shapes=[pltpu.VMEM((tm, d), BF16), pltpu.VMEM((tm, d), F32)],
        compiler_params=pltpu.CompilerParams(dimension_semantics=("arbitrary",),
                                             vmem_limit_bytes=VMEM_LIMIT),
        name="ffn_final" if final_norm else "ffn",
    )(x, gain.reshape(1, d), sh, sc, g, w1c, w3c, w2c, fin.reshape(1, d))


_C_QA = 0
_C_CKV = _C_QA + N_HEADS_A * LANES
_C_QI = _C_CKV + D_LATENT
_C_KI = _C_QI + N_HEADS_IDX * LANES
_C_SM = _C_KI + LANES
_C_QK = _C_SM + LANES
_C_V = _C_QK + 2 * W_M
_C_O = _C_V + W_M
_C_GA = _C_O + W_M
_C_GM = _C_GA + 1024
_C_END = _C_GM + 1024
_SM_W = 0
_SM_I = N_HEADS_IDX
_SM_F = N_HEADS_IDX + N_HEADS_M


def _pack_w_in(w_in, d_model):
    splits = (W_A, D_LATENT, N_HEADS_IDX * HEAD_DIM_IDX, HEAD_DIM_IDX, N_HEADS_IDX,
              W_M, W_M, W_M, N_HEADS_M, N_HEADS_M, W_M, d_model, d_model)
    offs = [0]
    for s in splits:
        offs.append(offs[-1] + s)
    (q_a, c_kv, q_i, k_i, w_i, q_m, k_m, v_m, i_p, f_p, o_p, g_a, g_m) = [
        w_in[:, offs[n]:offs[n + 1]] for n in range(len(splits))]
    d = w_in.shape[0]

    def pad_heads(w, nh, hd):
        w = w.reshape(d, nh, hd)
        return jnp.pad(w, ((0, 0), (0, 0), (0, LANES - hd))).reshape(d, nh * LANES)

    small = jnp.concatenate([w_i, i_p, f_p], axis=1)
    small = jnp.pad(small, ((0, 0), (0, LANES - small.shape[1])))
    packed = jnp.concatenate([
        pad_heads(q_a, N_HEADS_A, HEAD_DIM_A), c_kv, pad_heads(q_i, N_HEADS_IDX, HEAD_DIM_IDX),
        jnp.pad(k_i, ((0, 0), (0, LANES - HEAD_DIM_IDX))), small, q_m, k_m, v_m, o_p, g_a, g_m], axis=1)
    assert packed.shape[1] == _C_END
    return packed.astype(BF16)


def _mixin_kernel(x_ref, gain_ref, sh_ref, sc_ref, w_ref, kvn_ref, wuk_ref, cw_ref, cb_ref,
                  qabs_ref, qidx_ref, kidx_ref, ckv_ref, ckvt_ref, wt_ref, ift_ref,
                  qk_ref, v_ref, o_ref, ga_ref, gm_ref, h_scr, xe_scr, *, tm, tiles_per_seq):
    nqb = tm // Q_BLOCK

    @pl.when(pl.program_id(0) % tiles_per_seq == 0)
    def _():
        xe_scr[:8] = jnp.zeros((8, xe_scr.shape[1]), F32)

    x = x_ref[...]
    h = _rms_norm(x, gain_ref[...]) * (1.0 + sc_ref[0]) + sh_ref[0]
    h_scr[...] = h.astype(BF16)

    def proj(lo, hi):
        return jnp.dot(h_scr[...], w_ref[:, lo:hi], preferred_element_type=F32)

    qa = proj(_C_QA, _C_CKV)
    scale = HEAD_DIM_A ** -0.5 * LOG2E
    for hh in range(N_HEADS_A):
        q_h = qa[:, hh * LANES:(hh + 1) * LANES].astype(BF16)
        q_abs = jnp.dot(q_h, wuk_ref[hh], preferred_element_type=F32) * scale
        qabs_ref[:, hh] = q_abs.astype(BF16).reshape(nqb, Q_BLOCK, D_LATENT)
    ckv = _rms_norm(proj(_C_CKV, _C_QI), kvn_ref[...])
    ckv_ref[...] = ckv.astype(BF16)
    ckv_t = ckv.T
    ones_row = jnp.where(lax.broadcasted_iota(I32, (CKVT_ROWS - D_LATENT, KEY_BLOCK), 0) == 0, 1.0, 0.0)
    for j in range(tm // KEY_BLOCK):
        ckvt_ref[j, :D_LATENT] = ckv_t[:, j * KEY_BLOCK:(j + 1) * KEY_BLOCK].astype(BF16)
        ckvt_ref[j, D_LATENT:] = ones_row.astype(BF16)
    qi = proj(_C_QI, _C_KI)
    for hh in range(N_HEADS_IDX):
        qidx_ref[:, hh] = qi[:, hh * LANES:(hh + 1) * LANES].astype(BF16).reshape(nqb, Q_BLOCK, LANES)
    kidx_ref[...] = proj(_C_KI, _C_SM).astype(BF16)
    small_t = proj(_C_SM, _C_QK).T
    wt_ref[...] = small_t[_SM_W:_SM_W + N_HEADS_IDX] * IDX_SCALE
    ift_ref[...] = small_t[_SM_I:_SM_I + 2 * N_HEADS_M]
    xe_scr[8:] = proj(_C_QK, _C_V)
    xe = xe_scr[...]
    xq = xe[8:]
    conv = xq * cw_ref[CONV_WIDTH - 1:CONV_WIDTH, :] + cb_ref[...]
    for d in range(1, CONV_WIDTH):
        conv = conv + pltpu.roll(xe, d, axis=0)[8:] * cw_ref[CONV_WIDTH - 1 - d:CONV_WIDTH - d, :]
    xe_scr[:8] = xe_scr[tm:]
    qk = conv * _sigmoid(conv)
    qk_ref[:, :W_M] = qk[:, :W_M].astype(BF16)
    qk_ref[:, W_M:] = (qk[:, W_M:] * (HEAD_DIM_M ** -0.5)).astype(BF16)
    v_ref[...] = proj(_C_V, _C_O).astype(BF16)
    o_ref[...] = proj(_C_O, _C_GA)
    ga_ref[...] = proj(_C_GA, _C_GM)
    gm_ref[...] = proj(_C_GM, _C_END)


def _mixin(x, gain, sh, sc, w_packed, kv_norm, wuk_t, conv_w, conv_b, *, seq):
    t, d = x.shape
    tm = MIX_TM
    per_b = seq // tm
    nqb = tm // Q_BLOCK
    row = lambda w: pl.BlockSpec((tm, w), lambda i: (i, 0))
    mod_spec = pl.BlockSpec((1, 1, d), lambda i: (i // per_b, 0, 0))
    out_shape = (
        jax.ShapeDtypeStruct((t // Q_BLOCK, N_HEADS_A, Q_BLOCK, D_LATENT), BF16),
        jax.ShapeDtypeStruct((t // Q_BLOCK, N_HEADS_IDX, Q_BLOCK, LANES), BF16),
        jax.ShapeDtypeStruct((t, LANES), BF16),
        jax.ShapeDtypeStruct((t, D_LATENT), BF16),
        jax.ShapeDtypeStruct((t // KEY_BLOCK, CKVT_ROWS, KEY_BLOCK), BF16),
        jax.ShapeDtypeStruct((N_HEADS_IDX, t), F32),
        jax.ShapeDtypeStruct((2 * N_HEADS_M, t), F32),
        jax.ShapeDtypeStruct((t, 2 * W_M), BF16),
        jax.ShapeDtypeStruct((t, W_M), BF16),
        jax.ShapeDtypeStruct((t, W_M), F32),
        jax.ShapeDtypeStruct((t, d), F32),
        jax.ShapeDtypeStruct((t, d), F32),
    )
    out_specs = (
        pl.BlockSpec((nqb, N_HEADS_A, Q_BLOCK, D_LATENT), lambda i: (i, 0, 0, 0)),
        pl.BlockSpec((nqb, N_HEADS_IDX, Q_BLOCK, LANES), lambda i: (i, 0, 0, 0)),
        row(LANES), row(D_LATENT),
        pl.BlockSpec((tm // KEY_BLOCK, CKVT_ROWS, KEY_BLOCK), lambda i: (i, 0, 0)),
        pl.BlockSpec((N_HEADS_IDX, tm), lambda i: (0, i)),
        pl.BlockSpec((2 * N_HEADS_M, tm), lambda i: (0, i)),
        row(2 * W_M), row(W_M), row(W_M), row(d), row(d),
    )
    return pl.pallas_call(
        functools.partial(_mixin_kernel, tm=tm, tiles_per_seq=per_b),
        out_shape=out_shape,
        grid=(t // tm,),
        in_specs=[pl.BlockSpec((tm, d), lambda i: (i, 0)), _resident((1, d)), mod_spec, mod_spec,
                  _resident((d, _C_END)), _resident((1, D_LATENT)),
                  _resident((N_HEADS_A, LANES, D_LATENT)),
                  _resident((CONV_WIDTH, 2 * W_M)), _resident((1, 2 * W_M))],
        out_specs=out_specs,
        scratch_shapes=[pltpu.VMEM((tm, d), BF16), pltpu.VMEM((tm + 8, 2 * W_M), F32)],
        compiler_params=pltpu.CompilerParams(dimension_semantics=("arbitrary",),
                                             vmem_limit_bytes=VMEM_LIMIT),
        name="mixin",
    )(x, gain.reshape(1, d), sh, sc, w_packed, kv_norm.reshape(1, D_LATENT), wuk_t,
      conv_w, conv_b.reshape(1, -1))


def _sortable_key(score):
    bits = pltpu.bitcast(score, I32)
    bits = jnp.where(bits == INT_MIN, 0, bits)
    return jnp.where(bits < 0, bits ^ 0x7FFFFFFF, bits)


def _bit_transpose32(words):
    v = list(words)
    j, m = 16, 0x0000FFFF
    while j:
        k = 0
        while k < 32:
            t = (v[k] ^ lax.shift_right_logical(v[k + j], jnp.int32(j))) & m
            v[k] = v[k] ^ t
            v[k + j] = v[k + j] ^ (t << j)
            k = (k + j + 1) & ~j
        j >>= 1
        m = (m ^ (m << j)) & 0x7FFFFFFF
    return v


def _dsa_kernel(qidx_ref, qabs_ref, wt_ref, kidx_ref, ckv_ref, ckvt_ref, btile_ref, bmax_ref, wuvt_ref,
                out_ref, keys_scr, planes_scr, cand_scr, tau_scr, acc_scr, m_scr, lta_scr, ltc_scr, kmax_scr,
                qta_scr, qtc_scr, *, topk, n_qb):
    kb_sz = KEY_BLOCK
    step = pl.program_id(1)
    has_c = step >= 1
    qa = jnp.minimum(step, n_qb - 1)
    qc = jnp.maximum(step - 1, 0)
    slot_a = step & 1
    slot_c = 1 - slot_a
    n_a = qa // (kb_sz // Q_BLOCK) + 1
    n_c = qc // (kb_sz // Q_BLOCK) + 1
    qa0 = qa * Q_BLOCK
    qc0 = qc * Q_BLOCK
    row_id = lax.broadcasted_iota(I32, (kb_sz, LANES), 0)
    lane_id = lax.broadcasted_iota(I32, (kb_sz, LANES), 1)
    n_blocks = keys_scr.shape[1]
    n_groups = N_HEADS_A // 2
    pair = 2 * LANES
    ones8 = jnp.ones((8, D_LATENT), BF16)

    @pl.when(step == 0)
    def _():
        keys_scr[1, 0] = jnp.full((kb_sz, LANES), INT_MIN, I32)
        tau_scr[1] = jnp.zeros((1, LANES), I32)
        planes_scr[...] = jnp.zeros(planes_scr.shape, I32)

        def kn_body(kb, mx):
            c = ckv_ref[kb].astype(F32)
            n2 = lax.dot_general(ones8, (c * c).astype(BF16), (((1,), (1,)), ((), ())),
                                 preferred_element_type=F32)
            return jnp.maximum(mx, n2[0:1])
        mx = lax.fori_loop(0, n_blocks, kn_body, jnp.zeros((1, kb_sz), F32))
        kmax_scr[...] = jnp.max(mx, axis=1, keepdims=True)

    for g in range(n_groups):
        qi_g = qidx_ref[0, 2 * g:2 * g + 2].reshape(2 * Q_BLOCK, LANES).astype(F32)
        qta_scr[g] = qi_g.T.astype(BF16)
        qa_g = qabs_ref[0, 2 * g:2 * g + 2].reshape(2 * Q_BLOCK, D_LATENT).astype(F32)
        qtc_scr[g] = qa_g.T.astype(BF16)

    def idx_dot(kb, g):
        return jnp.dot(kidx_ref[kb], qta_scr[g], preferred_element_type=F32)

    def logits(kb, g):
        return jnp.dot(ckv_ref[kb], qtc_scr[g], preferred_element_type=F32)

    def bias_start(kb):
        delta = jnp.minimum(qc0 - kb * kb_sz, BIAS_PAD)
        return pl.multiple_of(BIAS_PAD - delta, LANES)

    qn2 = []
    for g in range(n_groups):
        q_g = qabs_ref[0, 2 * g:2 * g + 2].reshape(2 * Q_BLOCK, D_LATENT).astype(F32)
        qn2.append(lax.dot_general(ones8, (q_g * q_g).astype(BF16), (((1,), (1,)), ((), ())),
                                   preferred_element_type=F32)[0:1])
    bound = jnp.sqrt(jnp.concatenate(qn2, axis=1) * kmax_scr[...]) * 1.02 + bmax_ref[...] + 1e-3
    tau_c = tau_scr[slot_c]
    w_t = wt_ref[...]

    acc_scr[...] = jnp.zeros(acc_scr.shape, F32)
    for g in range(n_groups):
        lta_scr[:, g * pair:(g + 1) * pair] = idx_dot(0, g)
        ltc_scr[:, g * pair:(g + 1) * pair] = logits(0, g)

    bound_far = bound - jnp.concatenate([btile_ref[hh, 0:1, :] for hh in range(N_HEADS_A)], axis=1)

    def block_step(kb_raw, far):
        kb = jnp.minimum(kb_raw, n_a - 1)
        kb_next = jnp.minimum(kb_raw + 1, n_a - 1)
        kc = jnp.minimum(kb_raw, n_c - 1)
        kc_next = jnp.minimum(kb_raw + 1, n_c - 1)
        thr = jnp.where(has_c & (kb_raw < n_c), tau_c - 1, jnp.int32(2 ** 31 - 1))
        sel = keys_scr[slot_c, kc] > thr
        ct_blk = ckvt_ref[kc]
        start = bias_start(kc)
        ref_pt = bound_far if far else bound
        score = jnp.zeros((kb_sz, LANES), F32)
        for g in range(n_groups):
            s_t = lta_scr[:, g * pair:(g + 1) * pair]
            for j in range(2):
                hh = 2 * g + j
                score = score + jnp.maximum(s_t[:, j * LANES:(j + 1) * LANES], 0.0) * w_t[hh:hh + 1, :]
            lta_scr[:, g * pair:(g + 1) * pair] = idx_dot(kb_next, g)
            lt = ltc_scr[:, g * pair:(g + 1) * pair]
            ps = []
            for j in range(2):
                hh = 2 * g + j
                piece = lt[:, j * LANES:(j + 1) * LANES]
                if not far:
                    piece = piece + btile_ref[hh, pl.ds(start, kb_sz), :]
                ps.append(jnp.exp2(jnp.where(sel, piece, NEG_BIG) - ref_pt[:, hh * LANES:(hh + 1) * LANES]))
            ltc_scr[:, g * pair:(g + 1) * pair] = logits(kc_next, g)
            acc_scr[g] += jnp.dot(ct_blk, jnp.concatenate(ps, axis=1).astype(BF16),
                                  preferred_element_type=F32)
        valid = (kb * kb_sz + row_id) <= (qa0 + lane_id)
        keys = jnp.where(valid, _sortable_key(score), INT_MIN)
        keys_scr[slot_a, kb] = keys
        v = keys ^ INT_MIN
        words = _bit_transpose32([v[8 * i:8 * (i + 1), :] for i in range(32)])
        for bit in range(32):
            planes_scr[bit, kb] = words[31 - bit]

    def block_body(far, it, carry):
        for u in range(BLOCK_UNROLL):
            block_step(it * BLOCK_UNROLL + u, far)
        return carry

    n_far_iters = jnp.maximum(n_c - 2, 0) // BLOCK_UNROLL
    lax.fori_loop(0, n_far_iters, functools.partial(block_body, True), 0)
    lax.fori_loop(n_far_iters, (n_a + BLOCK_UNROLL - 1) // BLOCK_UNROLL,
                  functools.partial(block_body, False), 0)
    n_kb = n_a


    blk_id = lax.broadcasted_iota(I32, cand_scr.shape, 0)
    cand_scr[...] = jnp.where(blk_id < n_kb, -1, 0)

    def bit_body(it, carry):
        above, tau_u = carry
        bit = 31 - it
        ones = cand_scr[...] & planes_scr[bit]
        c1 = jnp.sum(jnp.sum(lax.population_count(ones), axis=0), axis=0, keepdims=True)
        take = (above + c1) >= topk
        cand_scr[...] = jnp.where(take, ones, cand_scr[...] ^ ones)
        above = jnp.where(take, above, above + c1)
        tau_u = jnp.where(take, tau_u | (jnp.int32(1) << bit), tau_u)
        return above, tau_u

    zero = jnp.zeros((1, LANES), I32)
    n_gt, tau_u = lax.fori_loop(0, 32, bit_body, (zero, zero))
    tau = tau_u ^ INT_MIN
    n_eq = jnp.sum(jnp.sum(lax.population_count(cand_scr[...]), axis=0), axis=0, keepdims=True)

    need = topk - n_gt
    overflow = n_eq > need
    seq_bits = max(1, (n_blocks * kb_sz - 1).bit_length())

    @pl.when(jnp.max(jnp.where(overflow, 1, 0)) > 0)
    def _():
        def count_ties_before(trial):
            def body(kb, acc):
                hit = jnp.where((keys_scr[slot_a, kb] == tau) & ((kb * kb_sz + row_id) < trial), 1, 0)
                return acc + jnp.sum(hit.reshape(kb_sz // 8, 8, LANES), axis=0)
            acc = lax.fori_loop(0, n_kb, body, jnp.zeros((8, LANES), I32))
            return jnp.sum(acc, axis=0, keepdims=True)

        def idx_body(it, jc):
            trial = jc | (jnp.int32(1) << (seq_bits - 1 - it))
            return jnp.where(count_ties_before(trial) < need, trial, jc)

        j_cut = lax.fori_loop(0, seq_bits, idx_body, jnp.zeros((1, LANES), I32))

        def demote_body(kb, carry):
            k = keys_scr[slot_a, kb]
            drop = overflow & (k == tau) & ((kb * kb_sz + row_id) > j_cut)
            keys_scr[slot_a, kb] = jnp.where(drop, INT_MIN, k)
            return carry

        lax.fori_loop(0, n_kb, demote_body, 0)

    tau_scr[slot_a] = jnp.maximum(tau, INT_MIN + 1)

    l_min = jnp.min(jnp.concatenate([acc_scr[g, D_LATENT:D_LATENT + 1, :] for g in range(n_groups)], axis=1))

    @pl.when(has_c & jnp.logical_not(l_min >= 2.0 ** -80))
    def _():
        m_scr[...] = jnp.full(m_scr.shape, NEG_BIG, F32)
        acc_scr[...] = jnp.zeros(acc_scr.shape, F32)

        def exact_body(kb, carry):
            sel = keys_scr[slot_c, kb] >= tau_c
            ct_blk = ckvt_ref[kb]
            start = bias_start(kb)
            for g in range(n_groups):
                lt = logits(kb, g)
                ps, alphas = [], []
                for j in range(2):
                    hh = 2 * g + j
                    sl = slice(hh * LANES, (hh + 1) * LANES)
                    piece = lt[:, j * LANES:(j + 1) * LANES] + btile_ref[hh, pl.ds(start, kb_sz), :]
                    masked = jnp.where(sel, piece, NEG_BIG)
                    m_old = m_scr[:, sl]
                    m_new = jnp.maximum(m_old, jnp.max(masked, axis=0, keepdims=True))
                    m_scr[:, sl] = m_new
                    alphas.append(jnp.exp2(m_old - m_new))
                    ps.append(jnp.exp2(masked - m_new))
                pv = jnp.dot(ct_blk, jnp.concatenate(ps, axis=1).astype(BF16), preferred_element_type=F32)
                acc_scr[g] = jnp.concatenate(alphas, axis=1) * acc_scr[g] + pv
            return carry

        lax.fori_loop(0, n_c, exact_body, 0)

    @pl.when(has_c)
    def _():
        ys = []
        for hh in range(N_HEADS_A):
            acc_h = acc_scr[hh // 2, :, (hh % 2) * LANES:(hh % 2 + 1) * LANES]
            o_h = acc_h[:D_LATENT] * (1.0 / acc_h[D_LATENT:D_LATENT + 1])
            ys.append(jnp.dot(wuvt_ref[hh], o_h.astype(BF16), preferred_element_type=F32))
        y_t = jnp.concatenate(ys, axis=0)
        out_ref[...] = y_t.T.astype(BF16)


def _dsa(q_idx, q_abs, w_t, k_idx, ckv, ckv_t, btile, bmax, wuv_t, *, batch, seq):
    t = batch * seq
    nqb = seq // Q_BLOCK
    nkb = seq // KEY_BLOCK
    topk = min(TOPK_MAX, seq // 4)
    k_idx3 = k_idx.reshape(t // KEY_BLOCK, KEY_BLOCK, LANES)
    ckv3 = ckv.reshape(t // KEY_BLOCK, KEY_BLOCK, D_LATENT)
    per_batch = lambda shape: pl.BlockSpec(shape, lambda b, q: (b,) + (0,) * (len(shape) - 1),
                                           pipeline_mode=pl.Buffered(1))
    scored = lambda b, s: b * nqb + jnp.minimum(s, nqb - 1)
    attended = lambda b, s: b * nqb + jnp.maximum(s - 1, 0)
    return pl.pallas_call(
        functools.partial(_dsa_kernel, topk=topk, n_qb=nqb),
        out_shape=jax.ShapeDtypeStruct((t, W_A), BF16),
        grid=(batch, nqb + 1),
        in_specs=[pl.BlockSpec((1, N_HEADS_IDX, Q_BLOCK, LANES), lambda b, s: (scored(b, s), 0, 0, 0)),
                  pl.BlockSpec((1, N_HEADS_A, Q_BLOCK, D_LATENT), lambda b, s: (attended(b, s), 0, 0, 0)),
                  pl.BlockSpec((N_HEADS_IDX, Q_BLOCK), lambda b, s: (0, scored(b, s))),
                  per_batch((nkb, KEY_BLOCK, LANES)),
                  per_batch((nkb, KEY_BLOCK, D_LATENT)),
                  per_batch((nkb, CKVT_ROWS, KEY_BLOCK)),
                  _resident((N_HEADS_A, BIAS_ROWS, LANES)),
                  _resident((1, N_HEADS_A * LANES)),
                  _resident((N_HEADS_A, HEAD_DIM_A, D_LATENT))],
        out_specs=pl.BlockSpec((Q_BLOCK, W_A), lambda b, s: (attended(b, s), 0)),
        scratch_shapes=[pltpu.VMEM((2, nkb, KEY_BLOCK, LANES), I32),
                        pltpu.VMEM((32, nkb, 8, LANES), I32),
                        pltpu.VMEM((nkb, 8, LANES), I32),
                        pltpu.VMEM((2, 1, LANES), I32),
                        pltpu.VMEM((N_HEADS_A // 2, CKVT_ROWS, 2 * LANES), F32),
                        pltpu.VMEM((1, N_HEADS_A * LANES), F32),
                        pltpu.VMEM((KEY_BLOCK, N_HEADS_A * LANES), F32),
                        pltpu.VMEM((KEY_BLOCK, N_HEADS_A * LANES), F32),
                        pltpu.VMEM((1, 1), F32),
                        pltpu.VMEM((N_HEADS_IDX // 2, LANES, 2 * LANES), BF16),
                        pltpu.VMEM((N_HEADS_A // 2, D_LATENT, 2 * LANES), BF16)],
        compiler_params=pltpu.CompilerParams(dimension_semantics=("arbitrary", "arbitrary"),
                                             vmem_limit_bytes=VMEM_LIMIT),
        name="dsa",
    )(q_idx, q_abs, w_t, k_idx3, ckv3, ckv_t, btile, bmax, wuv_t)


def _mlstm_kernel(qk_ref, v_ref, o_ref, ift_ref, gbt_ref, hn_ref,
                  out_ref, cx_scr, m_scr, *, chunk, n_batch):
    L = chunk

    @pl.when(pl.program_id(0) == 0)
    def _():
        cx_scr[...] = jnp.zeros(cx_scr.shape, F32)
        m_scr[...] = jnp.zeros(m_scr.shape, F32)

    rr = lax.broadcasted_iota(I32, (L, L), 0)
    cc = lax.broadcasted_iota(I32, (L, L), 1)
    causal = cc <= rr
    triu = jnp.where(rr <= cc, 1.0, 0.0).astype(BF16)
    lane = lax.broadcasted_iota(I32, (8, L), 1)
    ones_col = jnp.where(lax.broadcasted_iota(I32, (L, HEAD_DIM_M), 1) == 0, 1.0, 0.0).astype(BF16)
    for bi in range(n_batch):
        _mlstm_chunk(qk_ref.at[bi], v_ref.at[bi], o_ref.at[bi], ift_ref.at[bi], gbt_ref, hn_ref,
                     out_ref.at[bi], cx_scr.at[bi], m_scr.at[bi], causal, triu, lane, ones_col, L)


def _mlstm_chunk(qk_ref, v_ref, o_ref, ift_ref, gbt_ref, hn_ref, out_ref, cx_scr, m_scr,
                 causal, triu, lane, ones_col, L):
    g_t = ift_ref[...] + gbt_ref[...]
    b_all = sum(jnp.dot(piece, triu, preferred_element_type=F32) for piece in _split3(_log_sigmoid(g_t)))
    b8 = pltpu.roll(b_all, N_HEADS_M, axis=0)
    a8 = g_t - b8
    cm = a8
    shift = 1
    while shift < L:
        cm = jnp.maximum(cm, jnp.where(lane >= shift, pltpu.roll(cm, shift, axis=1), NEG_BIG))
        shift *= 2
    m_prev = m_scr[...]
    mx = jnp.maximum(m_prev, cm)
    mx_last = mx[:, L - 1:L]
    decay8 = jnp.exp(m_prev - mx_last)
    m_scr[...] = b8[:, L - 1:L] + mx_last
    rows = jnp.concatenate([-mx,
                            jnp.exp(m_prev - mx),
                            jnp.exp(-(b8 + mx)),
                            jnp.exp(a8 - mx_last),
                            jnp.zeros((LANES - 32, L), F32)], axis=0)
    cols = rows.T

    o_gate = _sigmoid(o_ref[...])
    for hh in range(N_HEADS_M):
        hs = slice(hh * HEAD_DIM_M, (hh + 1) * HEAD_DIM_M)
        qb16 = qk_ref[:, hs]
        kb16 = qk_ref[:, W_M + hh * HEAD_DIM_M:W_M + (hh + 1) * HEAD_DIM_M]
        v_ext = jnp.concatenate([v_ref[:, hs], ones_col], axis=1)
        u_c = cols[:, hh:hh + 1]
        w_inter = cols[:, 8 + hh:9 + hh]
        em_c = cols[:, 16 + hh:17 + hh]
        wgt_c = cols[:, 24 + hh:25 + hh]
        cx_prev = cx_scr[hh]

        d_mat = jnp.where(causal, jnp.exp(u_c + a8[hh:hh + 1, :]), 0.0)
        s = lax.dot_general(qb16, kb16, (((1,), (1,)), ((), ())), preferred_element_type=F32) * d_mat
        intra = jnp.dot(s.astype(BF16), v_ext, preferred_element_type=F32)
        inter = jnp.dot(qb16, cx_prev.astype(BF16), preferred_element_type=F32)
        both = w_inter * inter + intra
        num = both[:, :HEAD_DIM_M]
        den = both[:, HEAD_DIM_M:HEAD_DIM_M + 1]
        hval = num / jnp.maximum(jnp.abs(den), em_c)

        kw = kb16.astype(F32) * wgt_c
        cx_scr[hh] = decay8[hh:hh + 1] * cx_prev + jnp.dot(kw.T.astype(BF16), v_ext,
                                                           preferred_element_type=F32)

        mu = jnp.mean(hval, axis=1, keepdims=True)
        cen = hval - mu
        var = jnp.mean(cen * cen, axis=1, keepdims=True)
        hn = cen * lax.rsqrt(var + EPS) * hn_ref[:, hs]
        out_ref[:, hs] = (hn * o_gate[:, hs]).astype(BF16)


def _mlstm(qk, v, o_pre, ift, gate_bias, head_norm, *, batch, seq):
    t = batch * seq
    L = MLSTM_CHUNK
    nc = seq // L
    gbt = jnp.broadcast_to(gate_bias.reshape(2 * N_HEADS_M, 1), (2 * N_HEADS_M, L))
    ift_b = ift.reshape(2 * N_HEADS_M, batch, seq).transpose(1, 0, 2)
    row = lambda w: pl.BlockSpec((batch, L, w), lambda c: (0, c, 0))
    out = pl.pallas_call(
        functools.partial(_mlstm_kernel, chunk=L, n_batch=batch),
        out_shape=jax.ShapeDtypeStruct((batch, seq, W_M), BF16),
        grid=(nc,),
        in_specs=[row(2 * W_M), row(W_M), row(W_M),
                  pl.BlockSpec((batch, 2 * N_HEADS_M, L), lambda c: (0, 0, c)),
                  _resident((2 * N_HEADS_M, L)), _resident((1, W_M))],
        out_specs=row(W_M),
        scratch_shapes=[pltpu.VMEM((batch, N_HEADS_M, HEAD_DIM_M, 2 * HEAD_DIM_M), F32),
                        pltpu.VMEM((batch, 8, 1), F32)],
        compiler_params=pltpu.CompilerParams(dimension_semantics=("arbitrary",),
                                             vmem_limit_bytes=VMEM_LIMIT),
        name="mlstm",
    )(qk.reshape(batch, seq, 2 * W_M), v.reshape(batch, seq, W_M), o_pre.reshape(batch, seq, W_M),
      ift_b, gbt, head_norm.reshape(1, -1))
    return out.reshape(t, W_M)


def _merge_kernel(x_ref, ya_ref, hm_ref, ga_ref, gm_ref, g_ref, wa_ref, wm_ref, wo_ref, o_ref):
    pa = jnp.dot(ya_ref[...], wa_ref[...], preferred_element_type=F32)
    pm = jnp.dot(hm_ref[...], wm_ref[...], preferred_element_type=F32)
    merged = _sigmoid(ga_ref[...]) * pa + _sigmoid(gm_ref[...]) * pm
    out = jnp.dot(merged.astype(BF16), wo_ref[...], preferred_element_type=F32)
    o_ref[...] = x_ref[...] + g_ref[0] * out


def _merge(x, y_a, h_m, gate_a, gate_m, g, w_a, w_m, w_o, *, seq):
    t, d = x.shape
    tm = MERGE_TM
    per_b = seq // tm
    row = lambda w: pl.BlockSpec((tm, w), lambda i: (i, 0))
    return pl.pallas_call(
        _merge_kernel,
        out_shape=jax.ShapeDtypeStruct((t, d), F32),
        grid=(t // tm,),
        in_specs=[row(d), row(W_A), row(W_M), row(d), row(d),
                  pl.BlockSpec((1, 1, d), lambda i: (i // per_b, 0, 0)),
                  _resident((W_A, d)), _resident((W_M, d)), _resident((d, d))],
        out_specs=row(d),
        compiler_params=pltpu.CompilerParams(dimension_semantics=("arbitrary",),
                                             vmem_limit_bytes=VMEM_LIMIT),
        name="merge",
    )(x, y_a, h_m, gate_a, gate_m, g, w_a.astype(BF16), w_m.astype(BF16), w_o.astype(BF16))


def kernel(x, c, ada_w, ada_b, ffn1_norm, ffn1_w1, ffn1_w3, ffn1_w2, mix_norm, w_in, conv_w, conv_b,
           kv_norm, w_uk, w_uv, mlstm_gate_bias, mlstm_head_norm, rel_bias, w_branch_attn,
           w_branch_mlstm, w_out, ffn2_norm, ffn2_w1, ffn2_w3, ffn2_w2, final_norm):
    batch, seq, d = x.shape
    depth = ada_w.shape[0]
    assert seq % max(FFN_TM, MIX_TM, MERGE_TM, MLSTM_CHUNK, KEY_BLOCK) == 0
    t = batch * seq
    xf = x.reshape(t, d)
    btile, bmax = _bias_tiles(rel_bias)
    for l in range(depth):
        mod = _adaln(c, ada_w[l], ada_b[l]).reshape(batch, 9, 1, d)
        sh1, sc1, g1, sh2, sc2, g2, sh3, sc3, g3 = [mod[:, n] for n in range(9)]
        xf = _ffn(xf, ffn1_norm[l], sh1, sc1, g1, ffn1_w1[l], ffn1_w3[l], ffn1_w2[l], final_norm,
                  seq=seq, final_norm=False)
        wuk_t = jnp.pad(w_uk[l].transpose(0, 2, 1), ((0, 0), (0, LANES - HEAD_DIM_A), (0, 0))).astype(BF16)
        (q_abs, q_idx, k_idx, ckv, ckv_t, w_t, ift, qk_m, v_m, o_pre, gate_a, gate_m) = _mixin(
            xf, mix_norm[l], sh2, sc2, _pack_w_in(w_in[l], d), kv_norm[l], wuk_t, conv_w[l], conv_b[l],
            seq=seq)
        wuv_t = w_uv[l].transpose(0, 2, 1).astype(BF16)
        y_a = _dsa(q_idx, q_abs, w_t, k_idx, ckv, ckv_t, btile, bmax, wuv_t, batch=batch, seq=seq)
        h_m = _mlstm(qk_m, v_m, o_pre, ift, mlstm_gate_bias[l], mlstm_head_norm[l], batch=batch, seq=seq)
        xf = _merge(xf, y_a, h_m, gate_a, gate_m, g2, w_branch_attn[l], w_branch_mlstm[l], w_out[l],
                    seq=seq)
        xf = _ffn(xf, ffn2_norm[l], sh3, sc3, g3, ffn2_w1[l], ffn2_w3[l], ffn2_w2[l], final_norm,
                  seq=seq, final_norm=(l == depth - 1))
    return xf.reshape(batch, seq, d)
```

```python
import functools
import math

import jax
import jax.numpy as jnp
from jax import lax
from jax.experimental import pallas as pl
from jax.experimental.pallas import tpu as pltpu

F32 = jnp.float32
BF16 = jnp.bfloat16
I32 = jnp.int32

LANES = 128
MXU_DIM = 256
VMEM_LIMIT = 56 * 1024 * 1024

N_HEADS_A = 8
HEAD_DIM_A = 64
D_LATENT = 256
N_HEADS_IDX = 8
HEAD_DIM_IDX = 64
TOPK_MAX = 256
Q_BLOCK = 128
N_BUCKETS = 32
MAX_DISTANCE = 128
N_HEADS_M = 4
HEAD_DIM_M = 128
CONV_WIDTH = 4
EPS = 1e-6
IDX_SCALE = (N_HEADS_IDX ** -0.5) * (HEAD_DIM_IDX ** -0.5)
W_A = N_HEADS_A * HEAD_DIM_A
W_M = N_HEADS_M * HEAD_DIM_M

FFN_TM = 512
FFN_CHUNK = 256
MIX_TM = 512
KEY_BLOCK = 256
BLOCK_UNROLL = 2
MLSTM_CHUNK = 256
NEG_BIG = -1e30
INT_MIN = -2 ** 31

BIAS_PAD = 384
BIAS_ROWS = KEY_BLOCK + BIAS_PAD
CKVT_ROWS = D_LATENT + 16
LOG2E = math.log2(math.e)


def _sigmoid(x):
    return 1.0 / (1.0 + jnp.exp(-x))


def _log_sigmoid(x):
    return jnp.minimum(x, 0.0) - jnp.log(1.0 + jnp.exp(-jnp.abs(x)))


def _rms_norm(x, gain):
    ms = jnp.mean(x * x, axis=-1, keepdims=True)
    return x * lax.rsqrt(ms + EPS) * gain


def _split3(x):
    hi = x.astype(BF16)
    r1 = x - hi.astype(F32)
    mid = r1.astype(BF16)
    lo = (r1 - mid.astype(F32)).astype(BF16)
    return hi, mid, lo


def _resident(shape):
    nd = len(shape)
    return pl.BlockSpec(shape, lambda *_: (0,) * nd, pipeline_mode=pl.Buffered(1))


def _adaln_kernel(c_ref, w_ref, b_ref, o_ref):
    c = c_ref[...]
    cond = c * _sigmoid(c)
    o_ref[...] = jnp.dot(cond.astype(BF16), w_ref[...].astype(BF16),
                         preferred_element_type=F32) + b_ref[...]


def _adaln(c, ada_w, ada_b):
    b, d = c.shape
    n = ada_w.shape[1]
    rows = 8
    c_pad = jnp.zeros((rows, d), F32).at[:b].set(c)
    tn = 1024
    out = pl.pallas_call(
        _adaln_kernel,
        out_shape=jax.ShapeDtypeStruct((rows, n), F32),
        grid=(n // tn,),
        in_specs=[pl.BlockSpec((rows, d), lambda j: (0, 0)),
                  pl.BlockSpec((d, tn), lambda j: (0, j)),
                  pl.BlockSpec((1, tn), lambda j: (0, j))],
        out_specs=pl.BlockSpec((rows, tn), lambda j: (0, j)),
        compiler_params=pltpu.CompilerParams(dimension_semantics=("arbitrary",),
                                             vmem_limit_bytes=VMEM_LIMIT),
        name="adaln",
    )(c_pad, ada_w, ada_b.reshape(1, n))
    return out[:b]


def _t5_bucket(dist):
    n = jnp.maximum(dist, 0)
    max_exact = N_BUCKETS // 2
    nf = jnp.maximum(n, 1).astype(F32)
    large = max_exact + (jnp.log(nf / max_exact) / math.log(MAX_DISTANCE / max_exact)
                         * (N_BUCKETS - max_exact)).astype(I32)
    large = jnp.minimum(large, N_BUCKETS - 1)
    return jnp.where(n < max_exact, n, large)


def _bias_kernel(rel_ref, tile_ref, max_ref):
    r = lax.broadcasted_iota(I32, (BIAS_ROWS, LANES), 0)
    i = lax.broadcasted_iota(I32, (BIAS_ROWS, LANES), 1)
    bucket = _t5_bucket(i - r + BIAS_PAD)
    for h in range(N_HEADS_A):
        acc = jnp.zeros((BIAS_ROWS, LANES), F32)
        top = rel_ref[0, h] * LOG2E
        for bkt in range(N_BUCKETS):
            val = rel_ref[bkt, h] * LOG2E
            acc = jnp.where(bucket == bkt, val, acc)
            top = jnp.maximum(top, val)
        tile_ref[h] = acc
        max_ref[:, h * LANES:(h + 1) * LANES] = jnp.full((1, LANES), top, F32)


def _bias_tiles(rel_bias):
    return pl.pallas_call(
        _bias_kernel,
        out_shape=(jax.ShapeDtypeStruct((N_HEADS_A, BIAS_ROWS, LANES), F32),
                   jax.ShapeDtypeStruct((1, N_HEADS_A * LANES), F32)),
        in_specs=[pl.BlockSpec(memory_space=pltpu.SMEM)],
        out_specs=(pl.BlockSpec(memory_space=pltpu.VMEM), pl.BlockSpec(memory_space=pltpu.VMEM)),
        name="bias_tiles",
    )(rel_bias)


def _ffn_kernel(*refs, n_chunks, final_norm, merge):
    if merge:
        (x_ref, ya_ref, hm_ref, ga_ref, gm_ref, gmix_ref, wa_ref, wm_ref, wo_ref), refs = refs[:9], refs[9:]
    else:
        x_ref, refs = refs[0], refs[1:]
    gain_ref, sh_ref, sc_ref, g_ref, w1_ref, w3_ref, w2_ref, fin_ref, o_ref, h_scr, acc_scr = refs
    x = x_ref[...]
    if merge:
        pa = jnp.dot(ya_ref[...], wa_ref[...], preferred_element_type=F32)
        pm = jnp.dot(hm_ref[...], wm_ref[...], preferred_element_type=F32)
        merged = _sigmoid(ga_ref[...]) * pa + _sigmoid(gm_ref[...]) * pm
        x = x + gmix_ref[0] * jnp.dot(merged.astype(BF16), wo_ref[...], preferred_element_type=F32)
    h = _rms_norm(x, gain_ref[...]) * (1.0 + sc_ref[0]) + sh_ref[0]
    h_scr[...] = h.astype(BF16)
    for j in range(n_chunks):
        hb = h_scr[...]
        cols = slice(j * FFN_CHUNK, (j + 1) * FFN_CHUNK)
        u1 = jnp.dot(hb, w1_ref[:, cols], preferred_element_type=F32)
        u3 = jnp.dot(hb, w3_ref[:, cols], preferred_element_type=F32)
        a = (u1 * _sigmoid(u1)) * u3
        part = jnp.dot(a.astype(BF16), w2_ref[j], preferred_element_type=F32)
        if j == 0:
            acc_scr[...] = part
        else:
            acc_scr[...] += part
    out = x + (0.5 * g_ref[0]) * acc_scr[...]
    if final_norm:
        out = _rms_norm(out, fin_ref[...])
    o_ref[...] = out


def _ffn(x, gain, sh, sc, g, w1, w3, w2, fin, *, seq, final_norm, merge=None):
    t, d = x.shape
    dff = w1.shape[1]
    nch = dff // FFN_CHUNK
    w1c = w1.astype(BF16)
    w3c = w3.astype(BF16)
    w2c = w2.astype(BF16).reshape(nch, FFN_CHUNK, d)
    tm = FFN_TM
    per_b = seq // tm
    row = lambda w: pl.BlockSpec((tm, w), lambda i: (i, 0))
    mod_spec = pl.BlockSpec((1, 1, d), lambda i: (i // per_b, 0, 0))
    merge_specs, merge_args = [], []
    if merge is not None:
        y_a, h_m, gate_a, gate_m, g_mix, w_a, w_m, w_o = merge
        merge_specs = [row(W_A), row(W_M), row(d), row(d), mod_spec,
                       _resident((W_A, d)), _resident((W_M, d)), _resident((d, d))]
        merge_args = [y_a, h_m, gate_a, gate_m, g_mix, w_a.astype(BF16), w_m.astype(BF16), w_o.astype(BF16)]
    return pl.pallas_call(
        functools.partial(_ffn_kernel, n_chunks=nch, final_norm=final_norm, merge=merge is not None),
        out_shape=jax.ShapeDtypeStruct((t, d), F32),
        grid=(t // tm,),
        in_specs=[row(d)] + merge_specs + [
                  _resident((1, d)), mod_spec, mod_spec, mod_spec,
                  _resident((d, dff)), _resident((d, dff)),
                  _resident((nch, FFN_CHUNK, d)), _resident((1, d))],
        out_specs=row(d),
        scratch_shapes=[pltpu.VMEM((tm, d), BF16), pltpu.VMEM((tm, d), F32)],
        compiler_params=pltpu.CompilerParams(dimension_semantics=("arbitrary",),
                                             vmem_limit_bytes=VMEM_LIMIT),
        name="ffn_final" if final_norm else "ffn",
    )(x, *merge_args, gain.reshape(1, d), sh, sc, g, w1c, w3c, w2c, fin.reshape(1, d))


_C_QA = 0
_C_CKV = _C_QA + W_A
_C_QI = _C_CKV + D_LATENT
_C_KI = _C_QI + N_HEADS_IDX * LANES
_C_SM = _C_KI + LANES
_C_QK = _C_SM + LANES
_C_V = _C_QK + 2 * W_M
_C_O = _C_V + W_M
_C_GA = _C_O + W_M
_C_GM = _C_GA + 1024
_C_END = _C_GM + 1024
_SM_W = 0
_SM_I = N_HEADS_IDX
_SM_F = N_HEADS_IDX + N_HEADS_M


def _pack_w_in(w_in, d_model):
    splits = (W_A, D_LATENT, N_HEADS_IDX * HEAD_DIM_IDX, HEAD_DIM_IDX, N_HEADS_IDX,
              W_M, W_M, W_M, N_HEADS_M, N_HEADS_M, W_M, d_model, d_model)
    offs = [0]
    for s in splits:
        offs.append(offs[-1] + s)
    (q_a, c_kv, q_i, k_i, w_i, q_m, k_m, v_m, i_p, f_p, o_p, g_a, g_m) = [
        w_in[:, offs[n]:offs[n + 1]] for n in range(len(splits))]
    d = w_in.shape[0]

    def pad_heads(w, nh, hd):
        w = w.reshape(d, nh, hd)
        return jnp.pad(w, ((0, 0), (0, 0), (0, LANES - hd))).reshape(d, nh * LANES)

    small = jnp.concatenate([w_i, i_p, f_p], axis=1)
    small = jnp.pad(small, ((0, 0), (0, LANES - small.shape[1])))
    packed = jnp.concatenate([
        q_a, c_kv, pad_heads(q_i, N_HEADS_IDX, HEAD_DIM_IDX),
        jnp.pad(k_i, ((0, 0), (0, LANES - HEAD_DIM_IDX))), small, q_m, k_m, v_m, o_p, g_a, g_m], axis=1)
    assert packed.shape[1] == _C_END
    return packed.astype(BF16)


def _mixin_kernel(x_ref, gain_ref, sh_ref, sc_ref, w_ref, kvn_ref, wuk_ref, cw_ref, cb_ref,
                  qabs_ref, qidx_ref, kidx_ref, ckv_ref, ckvt_ref, wt_ref, ift_ref,
                  qk_ref, v_ref, o_ref, ga_ref, gm_ref, h_scr, xe_scr, *, tm, tiles_per_seq):
    nqb = tm // Q_BLOCK

    @pl.when(pl.program_id(0) % tiles_per_seq == 0)
    def _():
        xe_scr[:8] = jnp.zeros((8, xe_scr.shape[1]), F32)

    x = x_ref[...]
    h = _rms_norm(x, gain_ref[...]) * (1.0 + sc_ref[0]) + sh_ref[0]
    h_scr[...] = h.astype(BF16)

    def proj(lo, hi):
        return jnp.dot(h_scr[...], w_ref[:, lo:hi], preferred_element_type=F32)

    qa = proj(_C_QA, _C_CKV)
    scale = HEAD_DIM_A ** -0.5 * LOG2E
    for g in range(N_HEADS_A // 2):
        q_pair = qa[:, g * LANES:(g + 1) * LANES].astype(BF16)
        q_abs = jnp.dot(q_pair, wuk_ref[g], preferred_element_type=F32) * scale
        for j in range(2):
            qabs_ref[:, 2 * g + j] = (q_abs[:, j * D_LATENT:(j + 1) * D_LATENT]
                                      .astype(BF16).reshape(nqb, Q_BLOCK, D_LATENT))
    ckv = _rms_norm(proj(_C_CKV, _C_QI), kvn_ref[...])
    ckv_ref[...] = ckv.astype(BF16)
    ckv_t = ckv.T
    ones_row = jnp.where(lax.broadcasted_iota(I32, (CKVT_ROWS - D_LATENT, KEY_BLOCK), 0) == 0, 1.0, 0.0)
    for j in range(tm // KEY_BLOCK):
        ckvt_ref[j, :D_LATENT] = ckv_t[:, j * KEY_BLOCK:(j + 1) * KEY_BLOCK].astype(BF16)
        ckvt_ref[j, D_LATENT:] = ones_row.astype(BF16)
    qi = proj(_C_QI, _C_KI)
    for hh in range(N_HEADS_IDX):
        qidx_ref[:, hh] = qi[:, hh * LANES:(hh + 1) * LANES].astype(BF16).reshape(nqb, Q_BLOCK, LANES)
    kidx_ref[...] = proj(_C_KI, _C_SM).astype(BF16)
    small_t = proj(_C_SM, _C_QK).T
    wt_ref[...] = small_t[_SM_W:_SM_W + N_HEADS_IDX] * IDX_SCALE
    ift_ref[...] = small_t[_SM_I:_SM_I + 2 * N_HEADS_M]
    xe_scr[8:] = proj(_C_QK, _C_V)
    xe = xe_scr[...]
    xq = xe[8:]
    conv = xq * cw_ref[CONV_WIDTH - 1:CONV_WIDTH, :] + cb_ref[...]
    for d in range(1, CONV_WIDTH):
        conv = conv + pltpu.roll(xe, d, axis=0)[8:] * cw_ref[CONV_WIDTH - 1 - d:CONV_WIDTH - d, :]
    xe_scr[:8] = xe_scr[tm:]
    qk = conv * _sigmoid(conv)
    qk_ref[:, :W_M] = qk[:, :W_M].astype(BF16)
    qk_ref[:, W_M:] = (qk[:, W_M:] * (HEAD_DIM_M ** -0.5)).astype(BF16)
    v_ref[...] = proj(_C_V, _C_O).astype(BF16)
    o_ref[...] = proj(_C_O, _C_GA)
    ga_ref[...] = proj(_C_GA, _C_GM)
    gm_ref[...] = proj(_C_GM, _C_END)


def _mixin(x, gain, sh, sc, w_packed, kv_norm, wuk_t, conv_w, conv_b, *, seq):
    t, d = x.shape
    tm = MIX_TM
    per_b = seq // tm
    nqb = tm // Q_BLOCK
    row = lambda w: pl.BlockSpec((tm, w), lambda i: (i, 0))
    mod_spec = pl.BlockSpec((1, 1, d), lambda i: (i // per_b, 0, 0))
    out_shape = (
        jax.ShapeDtypeStruct((t // Q_BLOCK, N_HEADS_A, Q_BLOCK, D_LATENT), BF16),
        jax.ShapeDtypeStruct((t // Q_BLOCK, N_HEADS_IDX, Q_BLOCK, LANES), BF16),
        jax.ShapeDtypeStruct((t, LANES), BF16),
        jax.ShapeDtypeStruct((t, D_LATENT), BF16),
        jax.ShapeDtypeStruct((t // KEY_BLOCK, CKVT_ROWS, KEY_BLOCK), BF16),
        jax.ShapeDtypeStruct((N_HEADS_IDX, t), F32),
        jax.ShapeDtypeStruct((2 * N_HEADS_M, t), F32),
        jax.ShapeDtypeStruct((t, 2 * W_M), BF16),
        jax.ShapeDtypeStruct((t, W_M), BF16),
        jax.ShapeDtypeStruct((t, W_M), F32),
        jax.ShapeDtypeStruct((t, d), F32),
        jax.ShapeDtypeStruct((t, d), F32),
    )
    out_specs = (
        pl.BlockSpec((nqb, N_HEADS_A, Q_BLOCK, D_LATENT), lambda i: (i, 0, 0, 0)),
        pl.BlockSpec((nqb, N_HEADS_IDX, Q_BLOCK, LANES), lambda i: (i, 0, 0, 0)),
        row(LANES), row(D_LATENT),
        pl.BlockSpec((tm // KEY_BLOCK, CKVT_ROWS, KEY_BLOCK), lambda i: (i, 0, 0)),
        pl.BlockSpec((N_HEADS_IDX, tm), lambda i: (0, i)),
        pl.BlockSpec((2 * N_HEADS_M, tm), lambda i: (0, i)),
        row(2 * W_M), row(W_M), row(W_M), row(d), row(d),
    )
    return pl.pallas_call(
        functools.partial(_mixin_kernel, tm=tm, tiles_per_seq=per_b),
        out_shape=out_shape,
        grid=(t // tm,),
        in_specs=[pl.BlockSpec((tm, d), lambda i: (i, 0)), _resident((1, d)), mod_spec, mod_spec,
                  _resident((d, _C_END)), _resident((1, D_LATENT)),
                  _resident((N_HEADS_A // 2, LANES, 2 * D_LATENT)),
                  _resident((CONV_WIDTH, 2 * W_M)), _resident((1, 2 * W_M))],
        out_specs=out_specs,
        scratch_shapes=[pltpu.VMEM((tm, d), BF16), pltpu.VMEM((tm + 8, 2 * W_M), F32)],
        compiler_params=pltpu.CompilerParams(dimension_semantics=("arbitrary",),
                                             vmem_limit_bytes=VMEM_LIMIT),
        name="mixin",
    )(x, gain.reshape(1, d), sh, sc, w_packed, kv_norm.reshape(1, D_LATENT), wuk_t,
      conv_w, conv_b.reshape(1, -1))


def _sortable_key(score):
    bits = pltpu.bitcast(score, I32)
    bits = jnp.where(bits == INT_MIN, 0, bits)
    return jnp.where(bits < 0, bits ^ 0x7FFFFFFF, bits)


def _bit_transpose32(words):
    v = list(words)
    j, m = 16, 0x0000FFFF
    while j:
        k = 0
        while k < 32:
            t = (v[k] ^ lax.shift_right_logical(v[k + j], jnp.int32(j))) & m
            v[k] = v[k] ^ t
            v[k + j] = v[k + j] ^ (t << j)
            k = (k + j + 1) & ~j
        j >>= 1
        m = (m ^ (m << j)) & 0x7FFFFFFF
    return v


def _dsa_kernel(qidx_ref, qabs_ref, wt_ref, kidx_ref, ckv_ref, ckvt_ref, btile_ref, bmax_ref, wuvt_ref,
                out_ref, keys_scr, planes_scr, cand_scr, tau_scr, acc_scr, m_scr, lta_scr, ltc_scr, kmax_scr,
                qta_scr, qtc_scr, *, topk, n_qb):
    kb_sz = KEY_BLOCK
    step = pl.program_id(1)
    has_c = step >= 1
    qa = jnp.minimum(step, n_qb - 1)
    qc = jnp.maximum(step - 1, 0)
    slot_a = step & 1
    slot_c = 1 - slot_a
    n_a = qa // (kb_sz // Q_BLOCK) + 1
    n_c = qc // (kb_sz // Q_BLOCK) + 1
    qa0 = qa * Q_BLOCK
    qc0 = qc * Q_BLOCK
    row_id = lax.broadcasted_iota(I32, (kb_sz, LANES), 0)
    lane_id = lax.broadcasted_iota(I32, (kb_sz, LANES), 1)
    n_blocks = keys_scr.shape[1]
    n_groups = N_HEADS_A // 2
    pair = 2 * LANES
    ones8 = jnp.ones((8, D_LATENT), BF16)

    @pl.when(step == 0)
    def _():
        keys_scr[1, 0] = jnp.full((kb_sz, LANES), INT_MIN, I32)
        tau_scr[1] = jnp.zeros((1, LANES), I32)
        planes_scr[...] = jnp.zeros(planes_scr.shape, I32)

        def kn_body(kb, mx):
            c = ckv_ref[kb].astype(F32)
            n2 = lax.dot_general(ones8, (c * c).astype(BF16), (((1,), (1,)), ((), ())),
                                 preferred_element_type=F32)
            return jnp.maximum(mx, n2[0:1])
        mx = lax.fori_loop(0, n_blocks, kn_body, jnp.zeros((1, kb_sz), F32))
        kmax_scr[...] = jnp.max(mx, axis=1, keepdims=True)

    for g in range(n_groups):
        qi_g = qidx_ref[0, 2 * g:2 * g + 2].reshape(2 * Q_BLOCK, LANES).astype(F32)
        qta_scr[g] = qi_g.T.astype(BF16)
        qa_g = qabs_ref[0, 2 * g:2 * g + 2].reshape(2 * Q_BLOCK, D_LATENT).astype(F32)
        qtc_scr[g] = qa_g.T.astype(BF16)

    def idx_dot(kb, g):
        return jnp.dot(kidx_ref[kb], qta_scr[g], preferred_element_type=F32)

    def logits(kb, g):
        return jnp.dot(ckv_ref[kb], qtc_scr[g], preferred_element_type=F32)

    def bias_start(kb):
        delta = jnp.minimum(qc0 - kb * kb_sz, BIAS_PAD)
        return pl.multiple_of(BIAS_PAD - delta, LANES)

    qn2 = []
    for g in range(n_groups):
        q_g = qabs_ref[0, 2 * g:2 * g + 2].reshape(2 * Q_BLOCK, D_LATENT).astype(F32)
        qn2.append(lax.dot_general(ones8, (q_g * q_g).astype(BF16), (((1,), (1,)), ((), ())),
                                   preferred_element_type=F32)[0:1])
    bound = jnp.sqrt(jnp.concatenate(qn2, axis=1) * kmax_scr[...]) * 1.02 + bmax_ref[...] + 1e-3
    tau_c = tau_scr[slot_c]
    w_t = wt_ref[...]

    acc_scr[...] = jnp.zeros(acc_scr.shape, F32)
    for g in range(n_groups):
        lta_scr[:, g * pair:(g + 1) * pair] = idx_dot(0, g)
        ltc_scr[:, g * pair:(g + 1) * pair] = logits(0, g)

    bound_far = bound - jnp.concatenate([btile_ref[hh, 0:1, :] for hh in range(N_HEADS_A)], axis=1)

    def block_step(kb_raw, far):
        kb = jnp.minimum(kb_raw, n_a - 1)
        kb_next = jnp.minimum(kb_raw + 1, n_a - 1)
        kc = jnp.minimum(kb_raw, n_c - 1)
        kc_next = jnp.minimum(kb_raw + 1, n_c - 1)
        thr = jnp.where(has_c & (kb_raw < n_c), tau_c - 1, jnp.int32(2 ** 31 - 1))
        sel = keys_scr[slot_c, kc] > thr
        ct_blk = ckvt_ref[kc]
        start = bias_start(kc)
        ref_pt = bound_far if far else bound
        score = jnp.zeros((kb_sz, LANES), F32)
        for g in range(n_groups):
            s_t = lta_scr[:, g * pair:(g + 1) * pair]
            for j in range(2):
                hh = 2 * g + j
                score = score + jnp.maximum(s_t[:, j * LANES:(j + 1) * LANES], 0.0) * w_t[hh:hh + 1, :]
            lta_scr[:, g * pair:(g + 1) * pair] = idx_dot(kb_next, g)
            lt = ltc_scr[:, g * pair:(g + 1) * pair]
            ps = []
            for j in range(2):
                hh = 2 * g + j
                piece = lt[:, j * LANES:(j + 1) * LANES]
                if not far:
                    piece = piece + btile_ref[hh, pl.ds(start, kb_sz), :]
                ps.append(jnp.exp2(jnp.where(sel, piece, NEG_BIG) - ref_pt[:, hh * LANES:(hh + 1) * LANES]))
            ltc_scr[:, g * pair:(g + 1) * pair] = logits(kc_next, g)
            acc_scr[g] += jnp.dot(ct_blk, jnp.concatenate(ps, axis=1).astype(BF16),
                                  preferred_element_type=F32)
        valid = (kb * kb_sz + row_id) <= (qa0 + lane_id)
        keys = jnp.where(valid, _sortable_key(score), INT_MIN)
        keys_scr[slot_a, kb] = keys
        v = keys ^ INT_MIN
        words = _bit_transpose32([v[8 * i:8 * (i + 1), :] for i in range(32)])
        for bit in range(32):
            planes_scr[bit, kb] = words[31 - bit]

    def block_body(far, it, carry):
        for u in range(BLOCK_UNROLL):
            block_step(it * BLOCK_UNROLL + u, far)
        return carry

    n_far_iters = jnp.maximum(n_c - 2, 0) // BLOCK_UNROLL
    lax.fori_loop(0, n_far_iters, functools.partial(block_body, True), 0)
    lax.fori_loop(n_far_iters, (n_a + BLOCK_UNROLL - 1) // BLOCK_UNROLL,
                  functools.partial(block_body, False), 0)
    n_kb = n_a


    blk_id = lax.broadcasted_iota(I32, cand_scr.shape, 0)
    cand_scr[...] = jnp.where(blk_id < n_kb, -1, 0)

    def bit_body(it, carry):
        above, tau_u = carry
        bit = 31 - it
        ones = cand_scr[...] & planes_scr[bit]
        c1 = jnp.sum(jnp.sum(lax.population_count(ones), axis=0), axis=0, keepdims=True)
        take = (above + c1) >= topk
        cand_scr[...] = jnp.where(take, ones, cand_scr[...] ^ ones)
        above = jnp.where(take, above, above + c1)
        tau_u = jnp.where(take, tau_u | (jnp.int32(1) << bit), tau_u)
        return above, tau_u

    zero = jnp.zeros((1, LANES), I32)
    n_gt, tau_u = lax.fori_loop(0, 32, bit_body, (zero, zero))
    tau = tau_u ^ INT_MIN
    n_eq = jnp.sum(jnp.sum(lax.population_count(cand_scr[...]), axis=0), axis=0, keepdims=True)

    need = topk - n_gt
    overflow = n_eq > need
    seq_bits = max(1, (n_blocks * kb_sz - 1).bit_length())

    @pl.when(jnp.max(jnp.where(overflow, 1, 0)) > 0)
    def _():
        def count_ties_before(trial):
            def body(kb, acc):
                hit = jnp.where((keys_scr[slot_a, kb] == tau) & ((kb * kb_sz + row_id) < trial), 1, 0)
                return acc + jnp.sum(hit.reshape(kb_sz // 8, 8, LANES), axis=0)
            acc = lax.fori_loop(0, n_kb, body, jnp.zeros((8, LANES), I32))
            return jnp.sum(acc, axis=0, keepdims=True)

        def idx_body(it, jc):
            trial = jc | (jnp.int32(1) << (seq_bits - 1 - it))
            return jnp.where(count_ties_before(trial) < need, trial, jc)

        j_cut = lax.fori_loop(0, seq_bits, idx_body, jnp.zeros((1, LANES), I32))

        def demote_body(kb, carry):
            k = keys_scr[slot_a, kb]
            drop = overflow & (k == tau) & ((kb * kb_sz + row_id) > j_cut)
            keys_scr[slot_a, kb] = jnp.where(drop, INT_MIN, k)
            return carry

        lax.fori_loop(0, n_kb, demote_body, 0)

    tau_scr[slot_a] = jnp.maximum(tau, INT_MIN + 1)

    l_min = jnp.min(jnp.concatenate([acc_scr[g, D_LATENT:D_LATENT + 1, :] for g in range(n_groups)], axis=1))

    @pl.when(has_c & jnp.logical_not(l_min >= 2.0 ** -80))
    def _():
        m_scr[...] = jnp.full(m_scr.shape, NEG_BIG, F32)
        acc_scr[...] = jnp.zeros(acc_scr.shape, F32)

        def exact_body(kb, carry):
            sel = keys_scr[slot_c, kb] >= tau_c
            ct_blk = ckvt_ref[kb]
            start = bias_start(kb)
            for g in range(n_groups):
                lt = logits(kb, g)
                ps, alphas = [], []
                for j in range(2):
                    hh = 2 * g + j
                    sl = slice(hh * LANES, (hh + 1) * LANES)
                    piece = lt[:, j * LANES:(j + 1) * LANES] + btile_ref[hh, pl.ds(start, kb_sz), :]
                    masked = jnp.where(sel, piece, NEG_BIG)
                    m_old = m_scr[:, sl]
                    m_new = jnp.maximum(m_old, jnp.max(masked, axis=0, keepdims=True))
                    m_scr[:, sl] = m_new
                    alphas.append(jnp.exp2(m_old - m_new))
                    ps.append(jnp.exp2(masked - m_new))
                pv = jnp.dot(ct_blk, jnp.concatenate(ps, axis=1).astype(BF16), preferred_element_type=F32)
                acc_scr[g] = jnp.concatenate(alphas, axis=1) * acc_scr[g] + pv
            return carry

        lax.fori_loop(0, n_c, exact_body, 0)

    @pl.when(has_c)
    def _():
        ys = []
        for hh in range(N_HEADS_A):
            acc_h = acc_scr[hh // 2, :, (hh % 2) * LANES:(hh % 2 + 1) * LANES]
            o_h = acc_h[:D_LATENT] * (1.0 / acc_h[D_LATENT:D_LATENT + 1])
            ys.append(jnp.dot(wuvt_ref[hh], o_h.astype(BF16), preferred_element_type=F32))
        y_t = jnp.concatenate(ys, axis=0)
        out_ref[...] = y_t.T.astype(BF16)


def _dsa(q_idx, q_abs, w_t, k_idx, ckv, ckv_t, btile, bmax, wuv_t, *, batch, seq):
    t = batch * seq
    nqb = seq // Q_BLOCK
    nkb = seq // KEY_BLOCK
    topk = min(TOPK_MAX, seq // 4)
    k_idx3 = k_idx.reshape(t // KEY_BLOCK, KEY_BLOCK, LANES)
    ckv3 = ckv.reshape(t // KEY_BLOCK, KEY_BLOCK, D_LATENT)
    per_batch = lambda shape: pl.BlockSpec(shape, lambda b, q: (b,) + (0,) * (len(shape) - 1),
                                           pipeline_mode=pl.Buffered(1))
    scored = lambda b, s: b * nqb + jnp.minimum(s, nqb - 1)
    attended = lambda b, s: b * nqb + jnp.maximum(s - 1, 0)
    return pl.pallas_call(
        functools.partial(_dsa_kernel, topk=topk, n_qb=nqb),
        out_shape=jax.ShapeDtypeStruct((t, W_A), BF16),
        grid=(batch, nqb + 1),
        in_specs=[pl.BlockSpec((1, N_HEADS_IDX, Q_BLOCK, LANES), lambda b, s: (scored(b, s), 0, 0, 0)),
                  pl.BlockSpec((1, N_HEADS_A, Q_BLOCK, D_LATENT), lambda b, s: (attended(b, s), 0, 0, 0)),
                  pl.BlockSpec((N_HEADS_IDX, Q_BLOCK), lambda b, s: (0, scored(b, s))),
                  per_batch((nkb, KEY_BLOCK, LANES)),
                  per_batch((nkb, KEY_BLOCK, D_LATENT)),
                  per_batch((nkb, CKVT_ROWS, KEY_BLOCK)),
                  _resident((N_HEADS_A, BIAS_ROWS, LANES)),
                  _resident((1, N_HEADS_A * LANES)),
                  _resident((N_HEADS_A, HEAD_DIM_A, D_LATENT))],
        out_specs=pl.BlockSpec((Q_BLOCK, W_A), lambda b, s: (attended(b, s), 0)),
        scratch_shapes=[pltpu.VMEM((2, nkb, KEY_BLOCK, LANES), I32),
                        pltpu.VMEM((32, nkb, 8, LANES), I32),
                        pltpu.VMEM((nkb, 8, LANES), I32),
                        pltpu.VMEM((2, 1, LANES), I32),
                        pltpu.VMEM((N_HEADS_A // 2, CKVT_ROWS, 2 * LANES), F32),
                        pltpu.VMEM((1, N_HEADS_A * LANES), F32),
                        pltpu.VMEM((KEY_BLOCK, N_HEADS_A * LANES), F32),
                        pltpu.VMEM((KEY_BLOCK, N_HEADS_A * LANES), F32),
                        pltpu.VMEM((1, 1), F32),
                        pltpu.VMEM((N_HEADS_IDX // 2, LANES, 2 * LANES), BF16),
                        pltpu.VMEM((N_HEADS_A // 2, D_LATENT, 2 * LANES), BF16)],
        compiler_params=pltpu.CompilerParams(dimension_semantics=("arbitrary", "arbitrary"),
                                             vmem_limit_bytes=VMEM_LIMIT),
        name="dsa",
    )(q_idx, q_abs, w_t, k_idx3, ckv3, ckv_t, btile, bmax, wuv_t)


def _mlstm_kernel(qk_ref, v_ref, o_ref, ift_ref, gbt_ref, hn_ref,
                  out_ref, cx_scr, m_scr, *, chunk, n_batch):
    L = chunk

    @pl.when(pl.program_id(0) == 0)
    def _():
        cx_scr[...] = jnp.zeros(cx_scr.shape, F32)
        m_scr[...] = jnp.zeros(m_scr.shape, F32)

    rr = lax.broadcasted_iota(I32, (L, L), 0)
    cc = lax.broadcasted_iota(I32, (L, L), 1)
    causal = cc <= rr
    triu = jnp.where(rr <= cc, 1.0, 0.0).astype(BF16)
    lane = lax.broadcasted_iota(I32, (8, L), 1)
    ones_col = jnp.where(lax.broadcasted_iota(I32, (L, HEAD_DIM_M), 1) == 0, 1.0, 0.0).astype(BF16)
    for bi in range(n_batch):
        _mlstm_chunk(qk_ref.at[bi], v_ref.at[bi], o_ref.at[bi], ift_ref.at[bi], gbt_ref, hn_ref,
                     out_ref.at[bi], cx_scr.at[bi], m_scr.at[bi], causal, triu, lane, ones_col, L)


def _mlstm_chunk(qk_ref, v_ref, o_ref, ift_ref, gbt_ref, hn_ref, out_ref, cx_scr, m_scr,
                 causal, triu, lane, ones_col, L):
    g_t = ift_ref[...] + gbt_ref[...]
    b_all = sum(jnp.dot(piece, triu, preferred_element_type=F32) for piece in _split3(_log_sigmoid(g_t)))
    b8 = pltpu.roll(b_all, N_HEADS_M, axis=0)
    a8 = g_t - b8
    cm = a8
    shift = 1
    while shift < L:
        cm = jnp.maximum(cm, jnp.where(lane >= shift, pltpu.roll(cm, shift, axis=1), NEG_BIG))
        shift *= 2
    m_prev = m_scr[...]
    mx = jnp.maximum(m_prev, cm)
    mx_last = mx[:, L - 1:L]
    decay8 = jnp.exp(m_prev - mx_last)
    m_scr[...] = b8[:, L - 1:L] + mx_last
    rows = jnp.concatenate([-mx,
                            jnp.exp(m_prev - mx),
                            jnp.exp(-(b8 + mx)),
                            jnp.exp(a8 - mx_last),
                            jnp.zeros((LANES - 32, L), F32)], axis=0)
    cols = rows.T

    o_gate = _sigmoid(o_ref[...])
    for hh in range(N_HEADS_M):
        hs = slice(hh * HEAD_DIM_M, (hh + 1) * HEAD_DIM_M)
        qb16 = qk_ref[:, hs]
        kb16 = qk_ref[:, W_M + hh * HEAD_DIM_M:W_M + (hh + 1) * HEAD_DIM_M]
        v_ext = jnp.concatenate([v_ref[:, hs], ones_col], axis=1)
        u_c = cols[:, hh:hh + 1]
        w_inter = cols[:, 8 + hh:9 + hh]
        em_c = cols[:, 16 + hh:17 + hh]
        wgt_c = cols[:, 24 + hh:25 + hh]
        cx_prev = cx_scr[hh]

        d_mat = jnp.where(causal, jnp.exp(u_c + a8[hh:hh + 1, :]), 0.0)
        s = lax.dot_general(qb16, kb16, (((1,), (1,)), ((), ())), preferred_element_type=F32) * d_mat
        intra = jnp.dot(s.astype(BF16), v_ext, preferred_element_type=F32)
        inter = jnp.dot(qb16, cx_prev.astype(BF16), preferred_element_type=F32)
        both = w_inter * inter + intra
        num = both[:, :HEAD_DIM_M]
        den = both[:, HEAD_DIM_M:HEAD_DIM_M + 1]
        hval = num / jnp.maximum(jnp.abs(den), em_c)

        kw = kb16.astype(F32) * wgt_c
        cx_scr[hh] = decay8[hh:hh + 1] * cx_prev + jnp.dot(kw.T.astype(BF16), v_ext,
                                                           preferred_element_type=F32)

        mu = jnp.mean(hval, axis=1, keepdims=True)
        cen = hval - mu
        var = jnp.mean(cen * cen, axis=1, keepdims=True)
        hn = cen * lax.rsqrt(var + EPS) * hn_ref[:, hs]
        out_ref[:, hs] = (hn * o_gate[:, hs]).astype(BF16)


def _mlstm(qk, v, o_pre, ift, gate_bias, head_norm, *, batch, seq):
    t = batch * seq
    L = MLSTM_CHUNK
    nc = seq // L
    gbt = jnp.broadcast_to(gate_bias.reshape(2 * N_HEADS_M, 1), (2 * N_HEADS_M, L))
    ift_b = ift.reshape(2 * N_HEADS_M, batch, seq).transpose(1, 0, 2)
    row = lambda w: pl.BlockSpec((batch, L, w), lambda c: (0, c, 0))
    out = pl.pallas_call(
        functools.partial(_mlstm_kernel, chunk=L, n_batch=batch),
        out_shape=jax.ShapeDtypeStruct((batch, seq, W_M), BF16),
        grid=(nc,),
        in_specs=[row(2 * W_M), row(W_M), row(W_M),
                  pl.BlockSpec((batch, 2 * N_HEADS_M, L), lambda c: (0, 0, c)),
                  _resident((2 * N_HEADS_M, L)), _resident((1, W_M))],
        out_specs=row(W_M),
        scratch_shapes=[pltpu.VMEM((batch, N_HEADS_M, HEAD_DIM_M, 2 * HEAD_DIM_M), F32),
                        pltpu.VMEM((batch, 8, 1), F32)],
        compiler_params=pltpu.CompilerParams(dimension_semantics=("arbitrary",),
                                             vmem_limit_bytes=VMEM_LIMIT),
        name="mlstm",
    )(qk.reshape(batch, seq, 2 * W_M), v.reshape(batch, seq, W_M), o_pre.reshape(batch, seq, W_M),
      ift_b, gbt, head_norm.reshape(1, -1))
    return out.reshape(t, W_M)


def kernel(x, c, ada_w, ada_b, ffn1_norm, ffn1_w1, ffn1_w3, ffn1_w2, mix_norm, w_in, conv_w, conv_b,
           kv_norm, w_uk, w_uv, mlstm_gate_bias, mlstm_head_norm, rel_bias, w_branch_attn,
           w_branch_mlstm, w_out, ffn2_norm, ffn2_w1, ffn2_w3, ffn2_w2, final_norm):
    batch, seq, d = x.shape
    depth = ada_w.shape[0]
    assert seq % max(FFN_TM, MIX_TM, MLSTM_CHUNK, KEY_BLOCK) == 0
    t = batch * seq
    xf = x.reshape(t, d)
    btile, bmax = _bias_tiles(rel_bias)
    for l in range(depth):
        mod = _adaln(c, ada_w[l], ada_b[l]).reshape(batch, 9, 1, d)
        sh1, sc1, g1, sh2, sc2, g2, sh3, sc3, g3 = [mod[:, n] for n in range(9)]
        xf = _ffn(xf, ffn1_norm[l], sh1, sc1, g1, ffn1_w1[l], ffn1_w3[l], ffn1_w2[l], final_norm,
                  seq=seq, final_norm=False)
        wuk_hdc = w_uk[l].transpose(0, 2, 1).reshape(N_HEADS_A // 2, 2, HEAD_DIM_A, D_LATENT)
        zeros = jnp.zeros_like(wuk_hdc[:, 0])
        wuk_t = jnp.concatenate([jnp.concatenate([wuk_hdc[:, 0], zeros], axis=2),
                                 jnp.concatenate([zeros, wuk_hdc[:, 1]], axis=2)], axis=1).astype(BF16)
        (q_abs, q_idx, k_idx, ckv, ckv_t, w_t, ift, qk_m, v_m, o_pre, gate_a, gate_m) = _mixin(
            xf, mix_norm[l], sh2, sc2, _pack_w_in(w_in[l], d), kv_norm[l], wuk_t, conv_w[l], conv_b[l],
            seq=seq)
        wuv_t = w_uv[l].transpose(0, 2, 1).astype(BF16)
        y_a = _dsa(q_idx, q_abs, w_t, k_idx, ckv, ckv_t, btile, bmax, wuv_t, batch=batch, seq=seq)
        h_m = _mlstm(qk_m, v_m, o_pre, ift, mlstm_gate_bias[l], mlstm_head_norm[l], batch=batch, seq=seq)
        xf = _ffn(xf, ffn2_norm[l], sh3, sc3, g3, ffn2_w1[l], ffn2_w3[l], ffn2_w2[l], final_norm,
                  seq=seq, final_norm=(l == depth - 1),
                  merge=(y_a, h_m, gate_a, gate_m, g2, w_branch_attn[l], w_branch_mlstm[l], w_out[l]))
    return xf.reshape(batch, seq, d)
```

```python
import functools
import math

import jax
import jax.numpy as jnp
from jax import lax
from jax.experimental import pallas as pl
from jax.experimental.pallas import tpu as pltpu

F32 = jnp.float32
BF16 = jnp.bfloat16
I32 = jnp.int32

LANES = 128
MXU_DIM = 256
VMEM_LIMIT = 56 * 1024 * 1024

N_HEADS_A = 8
HEAD_DIM_A = 64
D_LATENT = 256
N_HEADS_IDX = 8
HEAD_DIM_IDX = 64
TOPK_MAX = 256
Q_BLOCK = 128
N_BUCKETS = 32
MAX_DISTANCE = 128
N_HEADS_M = 4
HEAD_DIM_M = 128
CONV_WIDTH = 4
EPS = 1e-6
IDX_SCALE = (N_HEADS_IDX ** -0.5) * (HEAD_DIM_IDX ** -0.5)
W_A = N_HEADS_A * HEAD_DIM_A
W_M = N_HEADS_M * HEAD_DIM_M

FFN_TM = 512
FFN_CHUNK = 256
MIX_TM = 512
KEY_BLOCK = 512
BLOCK_UNROLL = 1
PLANE_KEYS = 256
MLSTM_CHUNK = 256
NEG_BIG = -1e30
INT_MIN = -2 ** 31

BIAS_PAD = 2 * KEY_BLOCK - Q_BLOCK
BIAS_ROWS = KEY_BLOCK + BIAS_PAD
CKVT_ROWS = D_LATENT + 16
LOG2E = math.log2(math.e)


def _sigmoid(x):
    return 1.0 / (1.0 + jnp.exp(-x))


def _log_sigmoid(x):
    return jnp.minimum(x, 0.0) - jnp.log(1.0 + jnp.exp(-jnp.abs(x)))


def _rms_norm(x, gain):
    ms = jnp.mean(x * x, axis=-1, keepdims=True)
    return x * lax.rsqrt(ms + EPS) * gain


def _split3(x):
    hi = x.astype(BF16)
    r1 = x - hi.astype(F32)
    mid = r1.astype(BF16)
    lo = (r1 - mid.astype(F32)).astype(BF16)
    return hi, mid, lo


def _resident(shape):
    nd = len(shape)
    return pl.BlockSpec(shape, lambda *_: (0,) * nd, pipeline_mode=pl.Buffered(1))


def _adaln_kernel(c_ref, w_ref, b_ref, o_ref):
    c = c_ref[...]
    cond = c * _sigmoid(c)
    o_ref[...] = jnp.dot(cond.astype(BF16), w_ref[...].astype(BF16),
                         preferred_element_type=F32) + b_ref[...]


def _adaln(c, ada_w, ada_b):
    b, d = c.shape
    n = ada_w.shape[1]
    rows = 8
    c_pad = jnp.zeros((rows, d), F32).at[:b].set(c)
    tn = 1024
    out = pl.pallas_call(
        _adaln_kernel,
        out_shape=jax.ShapeDtypeStruct((rows, n), F32),
        grid=(n // tn,),
        in_specs=[pl.BlockSpec((rows, d), lambda j: (0, 0)),
                  pl.BlockSpec((d, tn), lambda j: (0, j)),
                  pl.BlockSpec((1, tn), lambda j: (0, j))],
        out_specs=pl.BlockSpec((rows, tn), lambda j: (0, j)),
        compiler_params=pltpu.CompilerParams(dimension_semantics=("arbitrary",),
                                             vmem_limit_bytes=VMEM_LIMIT),
        name="adaln",
    )(c_pad, ada_w, ada_b.reshape(1, n))
    return out[:b]


def _t5_bucket(dist):
    n = jnp.maximum(dist, 0)
    max_exact = N_BUCKETS // 2
    nf = jnp.maximum(n, 1).astype(F32)
    large = max_exact + (jnp.log(nf / max_exact) / math.log(MAX_DISTANCE / max_exact)
                         * (N_BUCKETS - max_exact)).astype(I32)
    large = jnp.minimum(large, N_BUCKETS - 1)
    return jnp.where(n < max_exact, n, large)


def _bias_kernel(rel_ref, tile_ref, max_ref):
    r = lax.broadcasted_iota(I32, (BIAS_ROWS, LANES), 0)
    i = lax.broadcasted_iota(I32, (BIAS_ROWS, LANES), 1)
    bucket = _t5_bucket(i - r + BIAS_PAD)
    for h in range(N_HEADS_A):
        acc = jnp.zeros((BIAS_ROWS, LANES), F32)
        top = rel_ref[0, h] * LOG2E
        for bkt in range(N_BUCKETS):
            val = rel_ref[bkt, h] * LOG2E
            acc = jnp.where(bucket == bkt, val, acc)
            top = jnp.maximum(top, val)
        tile_ref[h] = acc
        max_ref[:, h * LANES:(h + 1) * LANES] = jnp.full((1, LANES), top, F32)


def _bias_tiles(rel_bias):
    return pl.pallas_call(
        _bias_kernel,
        out_shape=(jax.ShapeDtypeStruct((N_HEADS_A, BIAS_ROWS, LANES), F32),
                   jax.ShapeDtypeStruct((1, N_HEADS_A * LANES), F32)),
        in_specs=[pl.BlockSpec(memory_space=pltpu.SMEM)],
        out_specs=(pl.BlockSpec(memory_space=pltpu.VMEM), pl.BlockSpec(memory_space=pltpu.VMEM)),
        name="bias_tiles",
    )(rel_bias)


def _ffn_kernel(*refs, n_chunks, final_norm, merge):
    if merge:
        (x_ref, ya_ref, hm_ref, ga_ref, gm_ref, gmix_ref, wa_ref, wm_ref, wo_ref), refs = refs[:9], refs[9:]
    else:
        x_ref, refs = refs[0], refs[1:]
    gain_ref, sh_ref, sc_ref, g_ref, w1_ref, w3_ref, w2_ref, fin_ref, o_ref, h_scr, acc_scr = refs
    x = x_ref[...]
    if merge:
        pa = jnp.dot(ya_ref[...], wa_ref[...], preferred_element_type=F32)
        pm = jnp.dot(hm_ref[...], wm_ref[...], preferred_element_type=F32)
        merged = _sigmoid(ga_ref[...]) * pa + _sigmoid(gm_ref[...]) * pm
        x = x + gmix_ref[0] * jnp.dot(merged.astype(BF16), wo_ref[...], preferred_element_type=F32)
    h = _rms_norm(x, gain_ref[...]) * (1.0 + sc_ref[0]) + sh_ref[0]
    h_scr[...] = h.astype(BF16)
    for j in range(n_chunks):
        hb = h_scr[...]
        cols = slice(j * FFN_CHUNK, (j + 1) * FFN_CHUNK)
        u1 = jnp.dot(hb, w1_ref[:, cols], preferred_element_type=F32)
        u3 = jnp.dot(hb, w3_ref[:, cols], preferred_element_type=F32)
        a = (u1 * _sigmoid(u1)) * u3
        part = jnp.dot(a.astype(BF16), w2_ref[j], preferred_element_type=F32)
        if j == 0:
            acc_scr[...] = part
        else:
            acc_scr[...] += part
    out = x + (0.5 * g_ref[0]) * acc_scr[...]
    if final_norm:
        out = _rms_norm(out, fin_ref[...])
    o_ref[...] = out


def _ffn(x, gain, sh, sc, g, w1, w3, w2, fin, *, seq, final_norm, merge=None):
    t, d = x.shape
    dff = w1.shape[1]
    nch = dff // FFN_CHUNK
    w1c = w1.astype(BF16)
    w3c = w3.astype(BF16)
    w2c = w2.astype(BF16).reshape(nch, FFN_CHUNK, d)
    tm = FFN_TM
    per_b = seq // tm
    row = lambda w: pl.BlockSpec((tm, w), lambda i: (i, 0))
    mod_spec = pl.BlockSpec((1, 1, d), lambda i: (i // per_b, 0, 0))
    merge_specs, merge_args = [], []
    if merge is not None:
        y_a, h_m, gate_a, gate_m, g_mix, w_a, w_m, w_o = merge
        merge_specs = [row(W_A), row(W_M), row(d), row(d), mod_spec,
                       _resident((W_A, d)), _resident((W_M, d)), _resident((d, d))]
        merge_args = [y_a, h_m, gate_a, gate_m, g_mix, w_a.astype(BF16), w_m.astype(BF16), w_o.astype(BF16)]
    return pl.pallas_call(
        functools.partial(_ffn_kernel, n_chunks=nch, final_norm=final_norm, merge=merge is not None),
        out_shape=jax.ShapeDtypeStruct((t, d), F32),
        grid=(t // tm,),
        in_specs=[row(d)] + merge_specs + [
                  _resident((1, d)), mod_spec, mod_spec, mod_spec,
                  _resident((d, dff)), _resident((d, dff)),
                  _resident((nch, FFN_CHUNK, d)), _resident((1, d))],
        out_specs=row(d),
        scratch_shapes=[pltpu.VMEM((tm, d), BF16), pltpu.VMEM((tm, d), F32)],
        compiler_params=pltpu.CompilerParams(dimension_semantics=("arbitrary",),
                                             vmem_limit_bytes=VMEM_LIMIT),
        name="ffn_final" if final_norm else "ffn",
    )(x, *merge_args, gain.reshape(1, d), sh, sc, g, w1c, w3c, w2c, fin.reshape(1, d))


_C_QA = 0
_C_CKV = _C_QA + W_A
_C_QI = _C_CKV + D_LATENT
_C_KI = _C_QI + N_HEADS_IDX * LANES
_C_SM = _C_KI + LANES
_C_QK = _C_SM + LANES
_C_V = _C_QK + 2 * W_M
_C_O = _C_V + W_M
_C_GA = _C_O + W_M
_C_GM = _C_GA + 1024
_C_END = _C_GM + 1024
_SM_W = 0
_SM_I = N_HEADS_IDX
_SM_F = N_HEADS_IDX + N_HEADS_M


def _pack_w_in(w_in, d_model):
    splits = (W_A, D_LATENT, N_HEADS_IDX * HEAD_DIM_IDX, HEAD_DIM_IDX, N_HEADS_IDX,
              W_M, W_M, W_M, N_HEADS_M, N_HEADS_M, W_M, d_model, d_model)
    offs = [0]
    for s in splits:
        offs.append(offs[-1] + s)
    (q_a, c_kv, q_i, k_i, w_i, q_m, k_m, v_m, i_p, f_p, o_p, g_a, g_m) = [
        w_in[:, offs[n]:offs[n + 1]] for n in range(len(splits))]
    d = w_in.shape[0]

    def pad_heads(w, nh, hd):
        w = w.reshape(d, nh, hd)
        return jnp.pad(w, ((0, 0), (0, 0), (0, LANES - hd))).reshape(d, nh * LANES)

    small = jnp.concatenate([w_i, i_p, f_p], axis=1)
    small = jnp.pad(small, ((0, 0), (0, LANES - small.shape[1])))
    packed = jnp.concatenate([
        q_a, c_kv, pad_heads(q_i, N_HEADS_IDX, HEAD_DIM_IDX),
        jnp.pad(k_i, ((0, 0), (0, LANES - HEAD_DIM_IDX))), small, q_m, k_m, v_m, o_p, g_a, g_m], axis=1)
    assert packed.shape[1] == _C_END
    return packed.astype(BF16)


def _mixin_kernel(x_ref, gain_ref, sh_ref, sc_ref, w_ref, kvn_ref, wuk_ref, cw_ref, cb_ref,
                  qabs_ref, qidx_ref, kidx_ref, ckv_ref, ckvt_ref, wt_ref, ift_ref,
                  qk_ref, v_ref, o_ref, ga_ref, gm_ref, h_scr, xe_scr, *, tm, tiles_per_seq):
    nqb = tm // Q_BLOCK

    @pl.when(pl.program_id(0) % tiles_per_seq == 0)
    def _():
        xe_scr[:8] = jnp.zeros((8, xe_scr.shape[1]), F32)

    x = x_ref[...]
    h = _rms_norm(x, gain_ref[...]) * (1.0 + sc_ref[0]) + sh_ref[0]
    h_scr[...] = h.astype(BF16)

    def proj(lo, hi):
        return jnp.dot(h_scr[...], w_ref[:, lo:hi], preferred_element_type=F32)

    qa = proj(_C_QA, _C_CKV)
    scale = HEAD_DIM_A ** -0.5 * LOG2E
    for g in range(N_HEADS_A // 2):
        q_pair = qa[:, g * LANES:(g + 1) * LANES].astype(BF16)
        q_abs = jnp.dot(q_pair, wuk_ref[g], preferred_element_type=F32) * scale
        for j in range(2):
            qabs_ref[:, 2 * g + j] = (q_abs[:, j * D_LATENT:(j + 1) * D_LATENT]
                                      .astype(BF16).reshape(nqb, Q_BLOCK, D_LATENT))
    ckv = _rms_norm(proj(_C_CKV, _C_QI), kvn_ref[...])
    ckv_ref[...] = ckv.astype(BF16)
    ckv_t = ckv.T
    ones_row = jnp.where(lax.broadcasted_iota(I32, (CKVT_ROWS - D_LATENT, KEY_BLOCK), 0) == 0, 1.0, 0.0)
    for j in range(tm // KEY_BLOCK):
        ckvt_ref[j, :D_LATENT] = ckv_t[:, j * KEY_BLOCK:(j + 1) * KEY_BLOCK].astype(BF16)
        ckvt_ref[j, D_LATENT:] = ones_row.astype(BF16)
    qi = proj(_C_QI, _C_KI)
    for hh in range(N_HEADS_IDX):
        qidx_ref[:, hh] = qi[:, hh * LANES:(hh + 1) * LANES].astype(BF16).reshape(nqb, Q_BLOCK, LANES)
    kidx_ref[...] = proj(_C_KI, _C_SM).astype(BF16)
    small_t = proj(_C_SM, _C_QK).T
    wt_ref[...] = small_t[_SM_W:_SM_W + N_HEADS_IDX] * IDX_SCALE
    ift_ref[...] = small_t[_SM_I:_SM_I + 2 * N_HEADS_M]
    xe_scr[8:] = proj(_C_QK, _C_V)
    xe = xe_scr[...]
    xq = xe[8:]
    conv = xq * cw_ref[CONV_WIDTH - 1:CONV_WIDTH, :] + cb_ref[...]
    for d in range(1, CONV_WIDTH):
        conv = conv + pltpu.roll(xe, d, axis=0)[8:] * cw_ref[CONV_WIDTH - 1 - d:CONV_WIDTH - d, :]
    xe_scr[:8] = xe_scr[tm:]
    qk = conv * _sigmoid(conv)
    qk_ref[:, :W_M] = qk[:, :W_M].astype(BF16)
    qk_ref[:, W_M:] = (qk[:, W_M:] * (HEAD_DIM_M ** -0.5)).astype(BF16)
    v_ref[...] = proj(_C_V, _C_O).astype(BF16)
    o_ref[...] = proj(_C_O, _C_GA)
    ga_ref[...] = proj(_C_GA, _C_GM)
    gm_ref[...] = proj(_C_GM, _C_END)


def _mixin(x, gain, sh, sc, w_packed, kv_norm, wuk_t, conv_w, conv_b, *, seq):
    t, d = x.shape
    tm = MIX_TM
    per_b = seq // tm
    nqb = tm // Q_BLOCK
    row = lambda w: pl.BlockSpec((tm, w), lambda i: (i, 0))
    mod_spec = pl.BlockSpec((1, 1, d), lambda i: (i // per_b, 0, 0))
    out_shape = (
        jax.ShapeDtypeStruct((t // Q_BLOCK, N_HEADS_A, Q_BLOCK, D_LATENT), BF16),
        jax.ShapeDtypeStruct((t // Q_BLOCK, N_HEADS_IDX, Q_BLOCK, LANES), BF16),
        jax.ShapeDtypeStruct((t, LANES), BF16),
        jax.ShapeDtypeStruct((t, D_LATENT), BF16),
        jax.ShapeDtypeStruct((t // KEY_BLOCK, CKVT_ROWS, KEY_BLOCK), BF16),
        jax.ShapeDtypeStruct((N_HEADS_IDX, t), F32),
        jax.ShapeDtypeStruct((2 * N_HEADS_M, t), F32),
        jax.ShapeDtypeStruct((t, 2 * W_M), BF16),
        jax.ShapeDtypeStruct((t, W_M), BF16),
        jax.ShapeDtypeStruct((t, W_M), F32),
        jax.ShapeDtypeStruct((t, d), F32),
        jax.ShapeDtypeStruct((t, d), F32),
    )
    out_specs = (
        pl.BlockSpec((nqb, N_HEADS_A, Q_BLOCK, D_LATENT), lambda i: (i, 0, 0, 0)),
        pl.BlockSpec((nqb, N_HEADS_IDX, Q_BLOCK, LANES), lambda i: (i, 0, 0, 0)),
        row(LANES), row(D_LATENT),
        pl.BlockSpec((tm // KEY_BLOCK, CKVT_ROWS, KEY_BLOCK), lambda i: (i, 0, 0)),
        pl.BlockSpec((N_HEADS_IDX, tm), lambda i: (0, i)),
        pl.BlockSpec((2 * N_HEADS_M, tm), lambda i: (0, i)),
        row(2 * W_M), row(W_M), row(W_M), row(d), row(d),
    )
    return pl.pallas_call(
        functools.partial(_mixin_kernel, tm=tm, tiles_per_seq=per_b),
        out_shape=out_shape,
        grid=(t // tm,),
        in_specs=[pl.BlockSpec((tm, d), lambda i: (i, 0)), _resident((1, d)), mod_spec, mod_spec,
                  _resident((d, _C_END)), _resident((1, D_LATENT)),
                  _resident((N_HEADS_A // 2, LANES, 2 * D_LATENT)),
                  _resident((CONV_WIDTH, 2 * W_M)), _resident((1, 2 * W_M))],
        out_specs=out_specs,
        scratch_shapes=[pltpu.VMEM((tm, d), BF16), pltpu.VMEM((tm + 8, 2 * W_M), F32)],
        compiler_params=pltpu.CompilerParams(dimension_semantics=("arbitrary",),
                                             vmem_limit_bytes=VMEM_LIMIT),
        name="mixin",
    )(x, gain.reshape(1, d), sh, sc, w_packed, kv_norm.reshape(1, D_LATENT), wuk_t,
      conv_w, conv_b.reshape(1, -1))


def _sortable_key(score):
    bits = pltpu.bitcast(score, I32)
    bits = jnp.where(bits == INT_MIN, 0, bits)
    return jnp.where(bits < 0, bits ^ 0x7FFFFFFF, bits)


def _bit_transpose32(words):
    v = list(words)
    j, m = 16, 0x0000FFFF
    while j:
        k = 0
        while k < 32:
            t = (v[k] ^ lax.shift_right_logical(v[k + j], jnp.int32(j))) & m
            v[k] = v[k] ^ t
            v[k + j] = v[k + j] ^ (t << j)
            k = (k + j + 1) & ~j
        j >>= 1
        m = (m ^ (m << j)) & 0x7FFFFFFF
    return v


def _dsa_kernel(qidx_ref, qabs_ref, wt_ref, kidx_ref, ckv_ref, ckvt_ref, btile_ref, bmax_ref, wuvt_ref,
                out_ref, keys_scr, planes_scr, cand_scr, tau_scr, acc_scr, m_scr, lta_scr, ltc_scr, kmax_scr,
                qta_scr, qtc_scr, *, topk, n_qb):
    kb_sz = KEY_BLOCK
    step = pl.program_id(1)
    has_c = step >= 1
    qa = jnp.minimum(step, n_qb - 1)
    qc = jnp.maximum(step - 1, 0)
    slot_a = step & 1
    slot_c = 1 - slot_a
    n_a = qa // (kb_sz // Q_BLOCK) + 1
    n_c = qc // (kb_sz // Q_BLOCK) + 1
    qa0 = qa * Q_BLOCK
    qc0 = qc * Q_BLOCK
    row_id = lax.broadcasted_iota(I32, (kb_sz, LANES), 0)
    lane_id = lax.broadcasted_iota(I32, (kb_sz, LANES), 1)
    n_blocks = keys_scr.shape[1]
    n_groups = N_HEADS_A // 2
    pair = 2 * LANES
    ones8 = jnp.ones((8, D_LATENT), BF16)

    @pl.when(step == 0)
    def _():
        keys_scr[1, 0] = jnp.full((kb_sz, LANES), INT_MIN, I32)
        tau_scr[1] = jnp.zeros((1, LANES), I32)
        planes_scr[...] = jnp.zeros(planes_scr.shape, I32)

        def kn_body(kb, mx):
            c = ckv_ref[kb].astype(F32)
            n2 = lax.dot_general(ones8, (c * c).astype(BF16), (((1,), (1,)), ((), ())),
                                 preferred_element_type=F32)
            return jnp.maximum(mx, n2[0:1])
        mx = lax.fori_loop(0, n_blocks, kn_body, jnp.zeros((1, kb_sz), F32))
        kmax_scr[...] = jnp.max(mx, axis=1, keepdims=True)

    for g in range(n_groups):
        qi_g = qidx_ref[0, 2 * g:2 * g + 2].reshape(2 * Q_BLOCK, LANES).astype(F32)
        qta_scr[g] = qi_g.T.astype(BF16)
        qa_g = qabs_ref[0, 2 * g:2 * g + 2].reshape(2 * Q_BLOCK, D_LATENT).astype(F32)
        qtc_scr[g] = qa_g.T.astype(BF16)

    def idx_dot(kb, g):
        return jnp.dot(kidx_ref[kb], qta_scr[g], preferred_element_type=F32)

    def logits(kb, g):
        return jnp.dot(ckv_ref[kb], qtc_scr[g], preferred_element_type=F32)

    def bias_start(kb):
        delta = jnp.minimum(qc0 - kb * kb_sz, BIAS_PAD)
        return pl.multiple_of(BIAS_PAD - delta, LANES)

    qn2 = []
    for g in range(n_groups):
        q_g = qabs_ref[0, 2 * g:2 * g + 2].reshape(2 * Q_BLOCK, D_LATENT).astype(F32)
        qn2.append(lax.dot_general(ones8, (q_g * q_g).astype(BF16), (((1,), (1,)), ((), ())),
                                   preferred_element_type=F32)[0:1])
    bound = jnp.sqrt(jnp.concatenate(qn2, axis=1) * kmax_scr[...]) * 1.02 + bmax_ref[...] + 1e-3
    tau_c = tau_scr[slot_c]
    w_t = wt_ref[...]

    acc_scr[...] = jnp.zeros(acc_scr.shape, F32)
    for g in range(n_groups):
        lta_scr[:, g * pair:(g + 1) * pair] = idx_dot(0, g)
        ltc_scr[:, g * pair:(g + 1) * pair] = logits(0, g)

    bound_far = bound - jnp.concatenate([btile_ref[hh, 0:1, :] for hh in range(N_HEADS_A)], axis=1)

    def block_step(kb_raw, far):
        kb = jnp.minimum(kb_raw, n_a - 1)
        kb_next = jnp.minimum(kb_raw + 1, n_a - 1)
        kc = jnp.minimum(kb_raw, n_c - 1)
        kc_next = jnp.minimum(kb_raw + 1, n_c - 1)
        thr = jnp.where(has_c & (kb_raw < n_c), tau_c - 1, jnp.int32(2 ** 31 - 1))
        sel = keys_scr[slot_c, kc] > thr
        ct_blk = ckvt_ref[kc]
        start = bias_start(kc)
        ref_pt = bound_far if far else bound
        score = jnp.zeros((kb_sz, LANES), F32)
        for g in range(n_groups):
            s_t = lta_scr[:, g * pair:(g + 1) * pair]
            for j in range(2):
                hh = 2 * g + j
                score = score + jnp.maximum(s_t[:, j * LANES:(j + 1) * LANES], 0.0) * w_t[hh:hh + 1, :]
            lta_scr[:, g * pair:(g + 1) * pair] = idx_dot(kb_next, g)
            lt = ltc_scr[:, g * pair:(g + 1) * pair]
            ps = []
            for j in range(2):
                hh = 2 * g + j
                piece = lt[:, j * LANES:(j + 1) * LANES]
                if not far:
                    piece = piece + btile_ref[hh, pl.ds(start, kb_sz), :]
                ps.append(jnp.exp2(jnp.where(sel, piece, NEG_BIG) - ref_pt[:, hh * LANES:(hh + 1) * LANES]))
            ltc_scr[:, g * pair:(g + 1) * pair] = logits(kc_next, g)
            acc_scr[g] += jnp.dot(ct_blk, jnp.concatenate(ps, axis=1).astype(BF16),
                                  preferred_element_type=F32)
        valid = (kb * kb_sz + row_id) <= (qa0 + lane_id)
        keys = jnp.where(valid, _sortable_key(score), INT_MIN)
        keys_scr[slot_a, kb] = keys
        v = keys ^ INT_MIN
        for sub in range(kb_sz // PLANE_KEYS):
            r0 = sub * PLANE_KEYS
            words = _bit_transpose32([v[r0 + 8 * i:r0 + 8 * (i + 1), :] for i in range(32)])
            for bit in range(32):
                planes_scr[bit, kb * (kb_sz // PLANE_KEYS) + sub] = words[31 - bit]

    def block_body(far, it, carry):
        for u in range(BLOCK_UNROLL):
            block_step(it * BLOCK_UNROLL + u, far)
        return carry

    n_far_iters = jnp.maximum(n_c - 2, 0) // BLOCK_UNROLL
    lax.fori_loop(0, n_far_iters, functools.partial(block_body, True), 0)
    lax.fori_loop(n_far_iters, (n_a + BLOCK_UNROLL - 1) // BLOCK_UNROLL,
                  functools.partial(block_body, False), 0)
    n_kb = n_a


    blk_id = lax.broadcasted_iota(I32, cand_scr.shape, 0)
    cand_scr[...] = jnp.where(blk_id < n_kb * (kb_sz // PLANE_KEYS), -1, 0)

    def bit_body(it, carry):
        above, tau_u = carry
        bit = 31 - it
        ones = cand_scr[...] & planes_scr[bit]
        c1 = jnp.sum(jnp.sum(lax.population_count(ones), axis=0), axis=0, keepdims=True)
        take = (above + c1) >= topk
        cand_scr[...] = jnp.where(take, ones, cand_scr[...] ^ ones)
        above = jnp.where(take, above, above + c1)
        tau_u = jnp.where(take, tau_u | (jnp.int32(1) << bit), tau_u)
        return above, tau_u

    zero = jnp.zeros((1, LANES), I32)
    n_gt, tau_u = lax.fori_loop(0, 32, bit_body, (zero, zero))
    tau = tau_u ^ INT_MIN
    n_eq = jnp.sum(jnp.sum(lax.population_count(cand_scr[...]), axis=0), axis=0, keepdims=True)

    need = topk - n_gt
    overflow = n_eq > need
    seq_bits = max(1, (n_blocks * kb_sz - 1).bit_length())

    @pl.when(jnp.max(jnp.where(overflow, 1, 0)) > 0)
    def _():
        def count_ties_before(trial):
            def body(kb, acc):
                hit = jnp.where((keys_scr[slot_a, kb] == tau) & ((kb * kb_sz + row_id) < trial), 1, 0)
                return acc + jnp.sum(hit.reshape(kb_sz // 8, 8, LANES), axis=0)
            acc = lax.fori_loop(0, n_kb, body, jnp.zeros((8, LANES), I32))
            return jnp.sum(acc, axis=0, keepdims=True)

        def idx_body(it, jc):
            trial = jc | (jnp.int32(1) << (seq_bits - 1 - it))
            return jnp.where(count_ties_before(trial) < need, trial, jc)

        j_cut = lax.fori_loop(0, seq_bits, idx_body, jnp.zeros((1, LANES), I32))

        def demote_body(kb, carry):
            k = keys_scr[slot_a, kb]
            drop = overflow & (k == tau) & ((kb * kb_sz + row_id) > j_cut)
            keys_scr[slot_a, kb] = jnp.where(drop, INT_MIN, k)
            return carry

        lax.fori_loop(0, n_kb, demote_body, 0)

    tau_scr[slot_a] = jnp.maximum(tau, INT_MIN + 1)

    l_min = jnp.min(jnp.concatenate([acc_scr[g, D_LATENT:D_LATENT + 1, :] for g in range(n_groups)], axis=1))

    @pl.when(has_c & jnp.logical_not(l_min >= 2.0 ** -80))
    def _():
        m_scr[...] = jnp.full(m_scr.shape, NEG_BIG, F32)
        acc_scr[...] = jnp.zeros(acc_scr.shape, F32)

        def exact_body(kb, carry):
            sel = keys_scr[slot_c, kb] >= tau_c
            ct_blk = ckvt_ref[kb]
            start = bias_start(kb)
            for g in range(n_groups):
                lt = logits(kb, g)
                ps, alphas = [], []
                for j in range(2):
                    hh = 2 * g + j
                    sl = slice(hh * LANES, (hh + 1) * LANES)
                    piece = lt[:, j * LANES:(j + 1) * LANES] + btile_ref[hh, pl.ds(start, kb_sz), :]
                    masked = jnp.where(sel, piece, NEG_BIG)
                    m_old = m_scr[:, sl]
                    m_new = jnp.maximum(m_old, jnp.max(masked, axis=0, keepdims=True))
                    m_scr[:, sl] = m_new
                    alphas.append(jnp.exp2(m_old - m_new))
                    ps.append(jnp.exp2(masked - m_new))
                pv = jnp.dot(ct_blk, jnp.concatenate(ps, axis=1).astype(BF16), preferred_element_type=F32)
                acc_scr[g] = jnp.concatenate(alphas, axis=1) * acc_scr[g] + pv
            return carry

        lax.fori_loop(0, n_c, exact_body, 0)

    @pl.when(has_c)
    def _():
        ys = []
        for hh in range(N_HEADS_A):
            acc_h = acc_scr[hh // 2, :, (hh % 2) * LANES:(hh % 2 + 1) * LANES]
            o_h = acc_h[:D_LATENT] * (1.0 / acc_h[D_LATENT:D_LATENT + 1])
            ys.append(jnp.dot(wuvt_ref[hh], o_h.astype(BF16), preferred_element_type=F32))
        y_t = jnp.concatenate(ys, axis=0)
        out_ref[...] = y_t.T.astype(BF16)


def _dsa(q_idx, q_abs, w_t, k_idx, ckv, ckv_t, btile, bmax, wuv_t, *, batch, seq):
    t = batch * seq
    nqb = seq // Q_BLOCK
    nkb = seq // KEY_BLOCK
    topk = min(TOPK_MAX, seq // 4)
    k_idx3 = k_idx.reshape(t // KEY_BLOCK, KEY_BLOCK, LANES)
    ckv3 = ckv.reshape(t // KEY_BLOCK, KEY_BLOCK, D_LATENT)
    per_batch = lambda shape: pl.BlockSpec(shape, lambda b, q: (b,) + (0,) * (len(shape) - 1),
                                           pipeline_mode=pl.Buffered(1))
    scored = lambda b, s: b * nqb + jnp.minimum(s, nqb - 1)
    attended = lambda b, s: b * nqb + jnp.maximum(s - 1, 0)
    return pl.pallas_call(
        functools.partial(_dsa_kernel, topk=topk, n_qb=nqb),
        out_shape=jax.ShapeDtypeStruct((t, W_A), BF16),
        grid=(batch, nqb + 1),
        in_specs=[pl.BlockSpec((1, N_HEADS_IDX, Q_BLOCK, LANES), lambda b, s: (scored(b, s), 0, 0, 0)),
                  pl.BlockSpec((1, N_HEADS_A, Q_BLOCK, D_LATENT), lambda b, s: (attended(b, s), 0, 0, 0)),
                  pl.BlockSpec((N_HEADS_IDX, Q_BLOCK), lambda b, s: (0, scored(b, s))),
                  per_batch((nkb, KEY_BLOCK, LANES)),
                  per_batch((nkb, KEY_BLOCK, D_LATENT)),
                  per_batch((nkb, CKVT_ROWS, KEY_BLOCK)),
                  _resident((N_HEADS_A, BIAS_ROWS, LANES)),
                  _resident((1, N_HEADS_A * LANES)),
                  _resident((N_HEADS_A, HEAD_DIM_A, D_LATENT))],
        out_specs=pl.BlockSpec((Q_BLOCK, W_A), lambda b, s: (attended(b, s), 0)),
        scratch_shapes=[pltpu.VMEM((2, nkb, KEY_BLOCK, LANES), I32),
                        pltpu.VMEM((32, seq // PLANE_KEYS, 8, LANES), I32),
                        pltpu.VMEM((seq // PLANE_KEYS, 8, LANES), I32),
                        pltpu.VMEM((2, 1, LANES), I32),
                        pltpu.VMEM((N_HEADS_A // 2, CKVT_ROWS, 2 * LANES), F32),
                        pltpu.VMEM((1, N_HEADS_A * LANES), F32),
                        pltpu.VMEM((KEY_BLOCK, N_HEADS_A * LANES), F32),
                        pltpu.VMEM((KEY_BLOCK, N_HEADS_A * LANES), F32),
                        pltpu.VMEM((1, 1), F32),
                        pltpu.VMEM((N_HEADS_IDX // 2, LANES, 2 * LANES), BF16),
                        pltpu.VMEM((N_HEADS_A // 2, D_LATENT, 2 * LANES), BF16)],
        compiler_params=pltpu.CompilerParams(dimension_semantics=("arbitrary", "arbitrary"),
                                             vmem_limit_bytes=VMEM_LIMIT),
        name="dsa",
    )(q_idx, q_abs, w_t, k_idx3, ckv3, ckv_t, btile, bmax, wuv_t)


def _mlstm_kernel(qk_ref, v_ref, o_ref, ift_ref, gbt_ref, hn_ref,
                  out_ref, cx_scr, m_scr, *, chunk, n_batch):
    L = chunk

    @pl.when(pl.program_id(0) == 0)
    def _():
        cx_scr[...] = jnp.zeros(cx_scr.shape, F32)
        m_scr[...] = jnp.zeros(m_scr.shape, F32)

    rr = lax.broadcasted_iota(I32, (L, L), 0)
    cc = lax.broadcasted_iota(I32, (L, L), 1)
    causal = cc <= rr
    triu = jnp.where(rr <= cc, 1.0, 0.0).astype(BF16)
    lane = lax.broadcasted_iota(I32, (8, L), 1)
    ones_col = jnp.where(lax.broadcasted_iota(I32, (L, HEAD_DIM_M), 1) == 0, 1.0, 0.0).astype(BF16)
    for bi in range(n_batch):
        _mlstm_chunk(qk_ref.at[bi], v_ref.at[bi], o_ref.at[bi], ift_ref.at[bi], gbt_ref, hn_ref,
                     out_ref.at[bi], cx_scr.at[bi], m_scr.at[bi], causal, triu, lane, ones_col, L)


def _mlstm_chunk(qk_ref, v_ref, o_ref, ift_ref, gbt_ref, hn_ref, out_ref, cx_scr, m_scr,
                 causal, triu, lane, ones_col, L):
    g_t = ift_ref[...] + gbt_ref[...]
    b_all = sum(jnp.dot(piece, triu, preferred_element_type=F32) for piece in _split3(_log_sigmoid(g_t)))
    b8 = pltpu.roll(b_all, N_HEADS_M, axis=0)
    a8 = g_t - b8
    cm = a8
    shift = 1
    while shift < L:
        cm = jnp.maximum(cm, jnp.where(lane >= shift, pltpu.roll(cm, shift, axis=1), NEG_BIG))
        shift *= 2
    m_prev = m_scr[...]
    mx = jnp.maximum(m_prev, cm)
    mx_last = mx[:, L - 1:L]
    decay8 = jnp.exp(m_prev - mx_last)
    m_scr[...] = b8[:, L - 1:L] + mx_last
    rows = jnp.concatenate([-mx,
                            jnp.exp(m_prev - mx),
                            jnp.exp(-(b8 + mx)),
                            jnp.exp(a8 - mx_last),
                            jnp.zeros((LANES - 32, L), F32)], axis=0)
    cols = rows.T

    o_gate = _sigmoid(o_ref[...])
    for hh in range(N_HEADS_M):
        hs = slice(hh * HEAD_DIM_M, (hh + 1) * HEAD_DIM_M)
        qb16 = qk_ref[:, hs]
        kb16 = qk_ref[:, W_M + hh * HEAD_DIM_M:W_M + (hh + 1) * HEAD_DIM_M]
        v_ext = jnp.concatenate([v_ref[:, hs], ones_col], axis=1)
        u_c = cols[:, hh:hh + 1]
        w_inter = cols[:, 8 + hh:9 + hh]
        em_c = cols[:, 16 + hh:17 + hh]
        wgt_c = cols[:, 24 + hh:25 + hh]
        cx_prev = cx_scr[hh]

        d_mat = jnp.where(causal, jnp.exp(u_c + a8[hh:hh + 1, :]), 0.0)
        s = lax.dot_general(qb16, kb16, (((1,), (1,)), ((), ())), preferred_element_type=F32) * d_mat
        intra = jnp.dot(s.astype(BF16), v_ext, preferred_element_type=F32)
        inter = jnp.dot(qb16, cx_prev.astype(BF16), preferred_element_type=F32)
        both = w_inter * inter + intra
        num = both[:, :HEAD_DIM_M]
        den = both[:, HEAD_DIM_M:HEAD_DIM_M + 1]
        hval = num / jnp.maximum(jnp.abs(den), em_c)

        kw = kb16.astype(F32) * wgt_c
        cx_scr[hh] = decay8[hh:hh + 1] * cx_prev + jnp.dot(kw.T.astype(BF16), v_ext,
                                                           preferred_element_type=F32)

        mu = jnp.mean(hval, axis=1, keepdims=True)
        cen = hval - mu
        var = jnp.mean(cen * cen, axis=1, keepdims=True)
        hn = cen * lax.rsqrt(var + EPS) * hn_ref[:, hs]
        out_ref[:, hs] = (hn * o_gate[:, hs]).astype(BF16)


def _mlstm(qk, v, o_pre, ift, gate_bias, head_norm, *, batch, seq):
    t = batch * seq
    L = MLSTM_CHUNK
    nc = seq // L
    gbt = jnp.broadcast_to(gate_bias.reshape(2 * N_HEADS_M, 1), (2 * N_HEADS_M, L))
    ift_b = ift.reshape(2 * N_HEADS_M, batch, seq).transpose(1, 0, 2)
    row = lambda w: pl.BlockSpec((batch, L, w), lambda c: (0, c, 0))
    out = pl.pallas_call(
        functools.partial(_mlstm_kernel, chunk=L, n_batch=batch),
        out_shape=jax.ShapeDtypeStruct((batch, seq, W_M), BF16),
        grid=(nc,),
        in_specs=[row(2 * W_M), row(W_M), row(W_M),
                  pl.BlockSpec((batch, 2 * N_HEADS_M, L), lambda c: (0, 0, c)),
                  _resident((2 * N_HEADS_M, L)), _resident((1, W_M))],
        out_specs=row(W_M),
        scratch_shapes=[pltpu.VMEM((batch, N_HEADS_M, HEAD_DIM_M, 2 * HEAD_DIM_M), F32),
                        pltpu.VMEM((batch, 8, 1), F32)],
        compiler_params=pltpu.CompilerParams(dimension_semantics=("arbitrary",),
                                             vmem_limit_bytes=VMEM_LIMIT),
        name="mlstm",
    )(qk.reshape(batch, seq, 2 * W_M), v.reshape(batch, seq, W_M), o_pre.reshape(batch, seq, W_M),
      ift_b, gbt, head_norm.reshape(1, -1))
    return out.reshape(t, W_M)


def kernel(x, c, ada_w, ada_b, ffn1_norm, ffn1_w1, ffn1_w3, ffn1_w2, mix_norm, w_in, conv_w, conv_b,
           kv_norm, w_uk, w_uv, mlstm_gate_bias, mlstm_head_norm, rel_bias, w_branch_attn,
           w_branch_mlstm, w_out, ffn2_norm, ffn2_w1, ffn2_w3, ffn2_w2, final_norm):
    batch, seq, d = x.shape
    depth = ada_w.shape[0]
    assert seq % max(FFN_TM, MIX_TM, MLSTM_CHUNK, KEY_BLOCK) == 0
    t = batch * seq
    xf = x.reshape(t, d)
    btile, bmax = _bias_tiles(rel_bias)
    for l in range(depth):
        mod = _adaln(c, ada_w[l], ada_b[l]).reshape(batch, 9, 1, d)
        sh1, sc1, g1, sh2, sc2, g2, sh3, sc3, g3 = [mod[:, n] for n in range(9)]
        xf = _ffn(xf, ffn1_norm[l], sh1, sc1, g1, ffn1_w1[l], ffn1_w3[l], ffn1_w2[l], final_norm,
                  seq=seq, final_norm=False)
        wuk_hdc = w_uk[l].transpose(0, 2, 1).reshape(N_HEADS_A // 2, 2, HEAD_DIM_A, D_LATENT)
        zeros = jnp.zeros_like(wuk_hdc[:, 0])
        wuk_t = jnp.concatenate([jnp.concatenate([wuk_hdc[:, 0], zeros], axis=2),
                                 jnp.concatenate([zeros, wuk_hdc[:, 1]], axis=2)], axis=1).astype(BF16)
        (q_abs, q_idx, k_idx, ckv, ckv_t, w_t, ift, qk_m, v_m, o_pre, gate_a, gate_m) = _mixin(
            xf, mix_norm[l], sh2, sc2, _pack_w_in(w_in[l], d), kv_norm[l], wuk_t, conv_w[l], conv_b[l],
            seq=seq)
        wuv_t = w_uv[l].transpose(0, 2, 1).astype(BF16)
        y_a = _dsa(q_idx, q_abs, w_t, k_idx, ckv, ckv_t, btile, bmax, wuv_t, batch=batch, seq=seq)
        h_m = _mlstm(qk_m, v_m, o_pre, ift, mlstm_gate_bias[l], mlstm_head_norm[l], batch=batch, seq=seq)
        xf = _ffn(xf, ffn2_norm[l], sh3, sc3, g3, ffn2_w1[l], ffn2_w3[l], ffn2_w2[l], final_norm,
                  seq=seq, final_norm=(l == depth - 1),
                  merge=(y_a, h_m, gate_a, gate_m, g2, w_branch_attn[l], w_branch_mlstm[l], w_out[l]))
    return xf.reshape(batch, seq, d)
```

```python
import functools
import math

import jax
import jax.numpy as jnp
from jax import lax
from jax.experimental import pallas as pl
from jax.experimental.pallas import tpu as pltpu

F32 = jnp.float32
BF16 = jnp.bfloat16
I32 = jnp.int32

LANES = 128
MXU_DIM = 256
VMEM_LIMIT = 56 * 1024 * 1024

N_HEADS_A = 8
HEAD_DIM_A = 64
D_LATENT = 256
N_HEADS_IDX = 8
HEAD_DIM_IDX = 64
TOPK_MAX = 256
Q_BLOCK = 128
N_BUCKETS = 32
MAX_DISTANCE = 128
N_HEADS_M = 4
HEAD_DIM_M = 128
CONV_WIDTH = 4
EPS = 1e-6
IDX_SCALE = (N_HEADS_IDX ** -0.5) * (HEAD_DIM_IDX ** -0.5)
W_A = N_HEADS_A * HEAD_DIM_A
W_M = N_HEADS_M * HEAD_DIM_M

FFN_TM = 512
FFN_CHUNK = 256
MIX_TM = 512
KEY_BLOCK = 256
BLOCK_UNROLL = 2
PLANE_KEYS = 256
MLSTM_CHUNK = 256
NEG_BIG = -1e30
INT_MIN = -2 ** 31

BIAS_PAD = 2 * KEY_BLOCK - Q_BLOCK
BIAS_ROWS = KEY_BLOCK + BIAS_PAD
CKVT_ROWS = D_LATENT + 16
LOG2E = math.log2(math.e)


def _sigmoid(x):
    return 1.0 / (1.0 + jnp.exp(-x))


def _log_sigmoid(x):
    return jnp.minimum(x, 0.0) - jnp.log(1.0 + jnp.exp(-jnp.abs(x)))


def _rms_norm(x, gain):
    ms = jnp.mean(x * x, axis=-1, keepdims=True)
    return x * lax.rsqrt(ms + EPS) * gain


def _split3(x):
    hi = x.astype(BF16)
    r1 = x - hi.astype(F32)
    mid = r1.astype(BF16)
    lo = (r1 - mid.astype(F32)).astype(BF16)
    return hi, mid, lo


def _resident(shape):
    nd = len(shape)
    return pl.BlockSpec(shape, lambda *_: (0,) * nd, pipeline_mode=pl.Buffered(1))


def _adaln_kernel(c_ref, w_ref, b_ref, o_ref):
    c = c_ref[...]
    cond = c * _sigmoid(c)
    o_ref[...] = jnp.dot(cond.astype(BF16), w_ref[...].astype(BF16),
                         preferred_element_type=F32) + b_ref[...]


def _adaln(c, ada_w, ada_b):
    b, d = c.shape
    n = ada_w.shape[1]
    rows = 8
    c_pad = jnp.zeros((rows, d), F32).at[:b].set(c)
    tn = 1024
    out = pl.pallas_call(
        _adaln_kernel,
        out_shape=jax.ShapeDtypeStruct((rows, n), F32),
        grid=(n // tn,),
        in_specs=[pl.BlockSpec((rows, d), lambda j: (0, 0)),
                  pl.BlockSpec((d, tn), lambda j: (0, j)),
                  pl.BlockSpec((1, tn), lambda j: (0, j))],
        out_specs=pl.BlockSpec((rows, tn), lambda j: (0, j)),
        compiler_params=pltpu.CompilerParams(dimension_semantics=("arbitrary",),
                                             vmem_limit_bytes=VMEM_LIMIT),
        name="adaln",
    )(c_pad, ada_w, ada_b.reshape(1, n))
    return out[:b]


def _t5_bucket(dist):
    n = jnp.maximum(dist, 0)
    max_exact = N_BUCKETS // 2
    nf = jnp.maximum(n, 1).astype(F32)
    large = max_exact + (jnp.log(nf / max_exact) / math.log(MAX_DISTANCE / max_exact)
                         * (N_BUCKETS - max_exact)).astype(I32)
    large = jnp.minimum(large, N_BUCKETS - 1)
    return jnp.where(n < max_exact, n, large)


def _bias_kernel(rel_ref, tile_ref, max_ref):
    r = lax.broadcasted_iota(I32, (BIAS_ROWS, LANES), 0)
    i = lax.broadcasted_iota(I32, (BIAS_ROWS, LANES), 1)
    bucket = _t5_bucket(i - r + BIAS_PAD)
    for h in range(N_HEADS_A):
        acc = jnp.zeros((BIAS_ROWS, LANES), F32)
        top = rel_ref[0, h] * LOG2E
        for bkt in range(N_BUCKETS):
            val = rel_ref[bkt, h] * LOG2E
            acc = jnp.where(bucket == bkt, val, acc)
            top = jnp.maximum(top, val)
        tile_ref[h] = acc
        max_ref[:, h * LANES:(h + 1) * LANES] = jnp.full((1, LANES), top, F32)


def _bias_tiles(rel_bias):
    return pl.pallas_call(
        _bias_kernel,
        out_shape=(jax.ShapeDtypeStruct((N_HEADS_A, BIAS_ROWS, LANES), F32),
                   jax.ShapeDtypeStruct((1, N_HEADS_A * LANES), F32)),
        in_specs=[pl.BlockSpec(memory_space=pltpu.SMEM)],
        out_specs=(pl.BlockSpec(memory_space=pltpu.VMEM), pl.BlockSpec(memory_space=pltpu.VMEM)),
        name="bias_tiles",
    )(rel_bias)


def _ffn_kernel(*refs, n_chunks, final_norm, merge):
    if merge:
        (x_ref, ya_ref, hm_ref, ga_ref, gm_ref, gmix_ref, wa_ref, wm_ref, wo_ref), refs = refs[:9], refs[9:]
    else:
        x_ref, refs = refs[0], refs[1:]
    gain_ref, sh_ref, sc_ref, g_ref, w1_ref, w3_ref, w2_ref, fin_ref, o_ref, h_scr, acc_scr = refs
    x = x_ref[...]
    if merge:
        pa = jnp.dot(ya_ref[...], wa_ref[...], preferred_element_type=F32)
        pm = jnp.dot(hm_ref[...], wm_ref[...], preferred_element_type=F32)
        merged = _sigmoid(ga_ref[...]) * pa + _sigmoid(gm_ref[...]) * pm
        x = x + gmix_ref[0] * jnp.dot(merged.astype(BF16), wo_ref[...], preferred_element_type=F32)
    h = _rms_norm(x, gain_ref[...]) * (1.0 + sc_ref[0]) + sh_ref[0]
    h_scr[...] = h.astype(BF16)
    for j in range(n_chunks):
        hb = h_scr[...]
        cols = slice(j * FFN_CHUNK, (j + 1) * FFN_CHUNK)
        u1 = jnp.dot(hb, w1_ref[:, cols], preferred_element_type=F32)
        u3 = jnp.dot(hb, w3_ref[:, cols], preferred_element_type=F32)
        a = (u1 * _sigmoid(u1)) * u3
        part = jnp.dot(a.astype(BF16), w2_ref[j], preferred_element_type=F32)
        if j == 0:
            acc_scr[...] = part
        else:
            acc_scr[...] += part
    out = x + (0.5 * g_ref[0]) * acc_scr[...]
    if final_norm:
        out = _rms_norm(out, fin_ref[...])
    o_ref[...] = out


def _ffn(x, gain, sh, sc, g, w1, w3, w2, fin, *, seq, final_norm, merge=None):
    t, d = x.shape
    dff = w1.shape[1]
    nch = dff // FFN_CHUNK
    w1c = w1.astype(BF16)
    w3c = w3.astype(BF16)
    w2c = w2.astype(BF16).reshape(nch, FFN_CHUNK, d)
    tm = FFN_TM
    per_b = seq // tm
    row = lambda w: pl.BlockSpec((tm, w), lambda i: (i, 0))
    mod_spec = pl.BlockSpec((1, 1, d), lambda i: (i // per_b, 0, 0))
    merge_specs, merge_args = [], []
    if merge is not None:
        y_a, h_m, gate_a, gate_m, g_mix, w_a, w_m, w_o = merge
        merge_specs = [row(W_A), row(W_M), row(d), row(d), mod_spec,
                       _resident((W_A, d)), _resident((W_M, d)), _resident((d, d))]
        merge_args = [y_a, h_m, gate_a, gate_m, g_mix, w_a.astype(BF16), w_m.astype(BF16), w_o.astype(BF16)]
    return pl.pallas_call(
        functools.partial(_ffn_kernel, n_chunks=nch, final_norm=final_norm, merge=merge is not None),
        out_shape=jax.ShapeDtypeStruct((t, d), F32),
        grid=(t // tm,),
        in_specs=[row(d)] + merge_specs + [
                  _resident((1, d)), mod_spec, mod_spec, mod_spec,
                  _resident((d, dff)), _resident((d, dff)),
                  _resident((nch, FFN_CHUNK, d)), _resident((1, d))],
        out_specs=row(d),
        scratch_shapes=[pltpu.VMEM((tm, d), BF16), pltpu.VMEM((tm, d), F32)],
        compiler_params=pltpu.CompilerParams(dimension_semantics=("arbitrary",),
                                             vmem_limit_bytes=VMEM_LIMIT),
        name="ffn_final" if final_norm else "ffn",
    )(x, *merge_args, gain.reshape(1, d), sh, sc, g, w1c, w3c, w2c, fin.reshape(1, d))


_C_QA = 0
_C_CKV = _C_QA + W_A
_C_QI = _C_CKV + D_LATENT
_C_KI = _C_QI + N_HEADS_IDX * LANES
_C_SM = _C_KI + LANES
_C_QK = _C_SM + LANES
_C_V = _C_QK + 2 * W_M
_C_O = _C_V + W_M
_C_GA = _C_O + W_M
_C_GM = _C_GA + 1024
_C_END = _C_GM + 1024
_SM_W = 0
_SM_I = N_HEADS_IDX
_SM_F = N_HEADS_IDX + N_HEADS_M


def _pack_w_in(w_in, d_model):
    splits = (W_A, D_LATENT, N_HEADS_IDX * HEAD_DIM_IDX, HEAD_DIM_IDX, N_HEADS_IDX,
              W_M, W_M, W_M, N_HEADS_M, N_HEADS_M, W_M, d_model, d_model)
    offs = [0]
    for s in splits:
        offs.append(offs[-1] + s)
    (q_a, c_kv, q_i, k_i, w_i, q_m, k_m, v_m, i_p, f_p, o_p, g_a, g_m) = [
        w_in[:, offs[n]:offs[n + 1]] for n in range(len(splits))]
    d = w_in.shape[0]

    def pad_heads(w, nh, hd):
        w = w.reshape(d, nh, hd)
        return jnp.pad(w, ((0, 0), (0, 0), (0, LANES - hd))).reshape(d, nh * LANES)

    small = jnp.concatenate([w_i, i_p, f_p], axis=1)
    small = jnp.pad(small, ((0, 0), (0, LANES - small.shape[1])))
    packed = jnp.concatenate([
        q_a, c_kv, pad_heads(q_i, N_HEADS_IDX, HEAD_DIM_IDX),
        jnp.pad(k_i, ((0, 0), (0, LANES - HEAD_DIM_IDX))), small, q_m, k_m, v_m, o_p, g_a, g_m], axis=1)
    assert packed.shape[1] == _C_END
    return packed.astype(BF16)


def _mixin_kernel(x_ref, gain_ref, sh_ref, sc_ref, w_ref, kvn_ref, wuk_ref, cw_ref, cb_ref,
                  qabs_ref, qidx_ref, kidx_ref, ckv_ref, ckvt_ref, wt_ref, ift_ref,
                  qk_ref, v_ref, o_ref, ga_ref, gm_ref, h_scr, xe_scr, *, tm, tiles_per_seq):
    nqb = tm // Q_BLOCK

    @pl.when(pl.program_id(0) % tiles_per_seq == 0)
    def _():
        xe_scr[:8] = jnp.zeros((8, xe_scr.shape[1]), F32)

    x = x_ref[...]
    h = _rms_norm(x, gain_ref[...]) * (1.0 + sc_ref[0]) + sh_ref[0]
    h_scr[...] = h.astype(BF16)

    def proj(lo, hi):
        return jnp.dot(h_scr[...], w_ref[:, lo:hi], preferred_element_type=F32)

    qa = proj(_C_QA, _C_CKV)
    scale = HEAD_DIM_A ** -0.5 * LOG2E
    for g in range(N_HEADS_A // 2):
        q_pair = qa[:, g * LANES:(g + 1) * LANES].astype(BF16)
        q_abs = jnp.dot(q_pair, wuk_ref[g], preferred_element_type=F32) * scale
        for j in range(2):
            qabs_ref[:, 2 * g + j] = (q_abs[:, j * D_LATENT:(j + 1) * D_LATENT]
                                      .astype(BF16).reshape(nqb, Q_BLOCK, D_LATENT))
    ckv = _rms_norm(proj(_C_CKV, _C_QI), kvn_ref[...])
    ckv_ref[...] = ckv.astype(BF16)
    ckv_t = ckv.T
    ones_row = jnp.where(lax.broadcasted_iota(I32, (CKVT_ROWS - D_LATENT, KEY_BLOCK), 0) == 0, 1.0, 0.0)
    for j in range(tm // KEY_BLOCK):
        ckvt_ref[j, :D_LATENT] = ckv_t[:, j * KEY_BLOCK:(j + 1) * KEY_BLOCK].astype(BF16)
        ckvt_ref[j, D_LATENT:] = ones_row.astype(BF16)
    qi = proj(_C_QI, _C_KI)
    for hh in range(N_HEADS_IDX):
        qidx_ref[:, hh] = qi[:, hh * LANES:(hh + 1) * LANES].astype(BF16).reshape(nqb, Q_BLOCK, LANES)
    kidx_ref[...] = proj(_C_KI, _C_SM).astype(BF16)
    small_t = proj(_C_SM, _C_QK).T
    wt_ref[...] = small_t[_SM_W:_SM_W + N_HEADS_IDX] * IDX_SCALE
    ift_ref[...] = small_t[_SM_I:_SM_I + 2 * N_HEADS_M]
    xe_scr[8:] = proj(_C_QK, _C_V)
    xe = xe_scr[...]
    xq = xe[8:]
    conv = xq * cw_ref[CONV_WIDTH - 1:CONV_WIDTH, :] + cb_ref[...]
    for d in range(1, CONV_WIDTH):
        conv = conv + pltpu.roll(xe, d, axis=0)[8:] * cw_ref[CONV_WIDTH - 1 - d:CONV_WIDTH - d, :]
    xe_scr[:8] = xe_scr[tm:]
    qk = conv * _sigmoid(conv)
    qk_ref[:, :W_M] = qk[:, :W_M].astype(BF16)
    qk_ref[:, W_M:] = (qk[:, W_M:] * (HEAD_DIM_M ** -0.5)).astype(BF16)
    v_ref[...] = proj(_C_V, _C_O).astype(BF16)
    o_ref[...] = proj(_C_O, _C_GA)
    ga_ref[...] = proj(_C_GA, _C_GM)
    gm_ref[...] = proj(_C_GM, _C_END)


def _mixin(x, gain, sh, sc, w_packed, kv_norm, wuk_t, conv_w, conv_b, *, seq):
    t, d = x.shape
    tm = MIX_TM
    per_b = seq // tm
    nqb = tm // Q_BLOCK
    row = lambda w: pl.BlockSpec((tm, w), lambda i: (i, 0))
    mod_spec = pl.BlockSpec((1, 1, d), lambda i: (i // per_b, 0, 0))
    out_shape = (
        jax.ShapeDtypeStruct((t // Q_BLOCK, N_HEADS_A, Q_BLOCK, D_LATENT), BF16),
        jax.ShapeDtypeStruct((t // Q_BLOCK, N_HEADS_IDX, Q_BLOCK, LANES), BF16),
        jax.ShapeDtypeStruct((t, LANES), BF16),
        jax.ShapeDtypeStruct((t, D_LATENT), BF16),
        jax.ShapeDtypeStruct((t // KEY_BLOCK, CKVT_ROWS, KEY_BLOCK), BF16),
        jax.ShapeDtypeStruct((N_HEADS_IDX, t), F32),
        jax.ShapeDtypeStruct((2 * N_HEADS_M, t), F32),
        jax.ShapeDtypeStruct((t, 2 * W_M), BF16),
        jax.ShapeDtypeStruct((t, W_M), BF16),
        jax.ShapeDtypeStruct((t, W_M), F32),
        jax.ShapeDtypeStruct((t, d), F32),
        jax.ShapeDtypeStruct((t, d), F32),
    )
    out_specs = (
        pl.BlockSpec((nqb, N_HEADS_A, Q_BLOCK, D_LATENT), lambda i: (i, 0, 0, 0)),
        pl.BlockSpec((nqb, N_HEADS_IDX, Q_BLOCK, LANES), lambda i: (i, 0, 0, 0)),
        row(LANES), row(D_LATENT),
        pl.BlockSpec((tm // KEY_BLOCK, CKVT_ROWS, KEY_BLOCK), lambda i: (i, 0, 0)),
        pl.BlockSpec((N_HEADS_IDX, tm), lambda i: (0, i)),
        pl.BlockSpec((2 * N_HEADS_M, tm), lambda i: (0, i)),
        row(2 * W_M), row(W_M), row(W_M), row(d), row(d),
    )
    return pl.pallas_call(
        functools.partial(_mixin_kernel, tm=tm, tiles_per_seq=per_b),
        out_shape=out_shape,
        grid=(t // tm,),
        in_specs=[pl.BlockSpec((tm, d), lambda i: (i, 0)), _resident((1, d)), mod_spec, mod_spec,
                  _resident((d, _C_END)), _resident((1, D_LATENT)),
                  _resident((N_HEADS_A // 2, LANES, 2 * D_LATENT)),
                  _resident((CONV_WIDTH, 2 * W_M)), _resident((1, 2 * W_M))],
        out_specs=out_specs,
        scratch_shapes=[pltpu.VMEM((tm, d), BF16), pltpu.VMEM((tm + 8, 2 * W_M), F32)],
        compiler_params=pltpu.CompilerParams(dimension_semantics=("arbitrary",),
                                             vmem_limit_bytes=VMEM_LIMIT),
        name="mixin",
    )(x, gain.reshape(1, d), sh, sc, w_packed, kv_norm.reshape(1, D_LATENT), wuk_t,
      conv_w, conv_b.reshape(1, -1))


def _sortable_key(score):
    bits = pltpu.bitcast(score, I32)
    bits = jnp.where(bits == INT_MIN, 0, bits)
    return jnp.where(bits < 0, bits ^ 0x7FFFFFFF, bits)


def _bit_transpose32(words):
    v = list(words)
    j, m = 16, 0x0000FFFF
    while j:
        k = 0
        while k < 32:
            t = (v[k] ^ lax.shift_right_logical(v[k + j], jnp.int32(j))) & m
            v[k] = v[k] ^ t
            v[k + j] = v[k + j] ^ (t << j)
            k = (k + j + 1) & ~j
        j >>= 1
        m = (m ^ (m << j)) & 0x7FFFFFFF
    return v


def _dsa_kernel(qidx_ref, qabs_ref, wt_ref, kidx_ref, ckv_ref, ckvt_ref, btile_ref, bmax_ref, wuvt_ref,
                out_ref, keys_scr, planes_scr, cand_scr, tau_scr, acc_scr, m_scr, lta_scr, ltc_scr, kmax_scr,
                qta_scr, qtc_scr, *, topk, n_qb):
    kb_sz = KEY_BLOCK
    step = pl.program_id(1)
    has_c = step >= 1
    qa = jnp.minimum(step, n_qb - 1)
    qc = jnp.maximum(step - 1, 0)
    slot_a = step & 1
    slot_c = 1 - slot_a
    n_a = qa // (kb_sz // Q_BLOCK) + 1
    n_c = qc // (kb_sz // Q_BLOCK) + 1
    qa0 = qa * Q_BLOCK
    qc0 = qc * Q_BLOCK
    row_id = lax.broadcasted_iota(I32, (kb_sz, LANES), 0)
    lane_id = lax.broadcasted_iota(I32, (kb_sz, LANES), 1)
    n_blocks = keys_scr.shape[1]
    n_groups = N_HEADS_A // 2
    pair = 2 * LANES
    ones8 = jnp.ones((8, D_LATENT), BF16)

    @pl.when(step == 0)
    def _():
        keys_scr[1, 0] = jnp.full((kb_sz, LANES), INT_MIN, I32)
        tau_scr[1] = jnp.zeros((1, LANES), I32)
        planes_scr[...] = jnp.zeros(planes_scr.shape, I32)

        def kn_body(kb, mx):
            c = ckv_ref[kb].astype(F32)
            n2 = lax.dot_general(ones8, (c * c).astype(BF16), (((1,), (1,)), ((), ())),
                                 preferred_element_type=F32)
            return jnp.maximum(mx, n2[0:1])
        mx = lax.fori_loop(0, n_blocks, kn_body, jnp.zeros((1, kb_sz), F32))
        kmax_scr[...] = jnp.max(mx, axis=1, keepdims=True)

    for g in range(n_groups):
        qi_g = qidx_ref[0, 2 * g:2 * g + 2].reshape(2 * Q_BLOCK, LANES).astype(F32)
        qta_scr[g] = qi_g.T.astype(BF16)
        qa_g = qabs_ref[0, 2 * g:2 * g + 2].reshape(2 * Q_BLOCK, D_LATENT).astype(F32)
        qtc_scr[g] = qa_g.T.astype(BF16)

    def idx_dot(kb, g):
        return jnp.dot(kidx_ref[kb], qta_scr[g], preferred_element_type=F32)

    def logits(kb, g):
        return jnp.dot(ckv_ref[kb], qtc_scr[g], preferred_element_type=F32)

    def bias_start(kb):
        delta = jnp.minimum(qc0 - kb * kb_sz, BIAS_PAD)
        return pl.multiple_of(BIAS_PAD - delta, LANES)

    qn2 = []
    for g in range(n_groups):
        q_g = qabs_ref[0, 2 * g:2 * g + 2].reshape(2 * Q_BLOCK, D_LATENT).astype(F32)
        qn2.append(lax.dot_general(ones8, (q_g * q_g).astype(BF16), (((1,), (1,)), ((), ())),
                                   preferred_element_type=F32)[0:1])
    bound = jnp.sqrt(jnp.concatenate(qn2, axis=1) * kmax_scr[...]) * 1.02 + bmax_ref[...] + 1e-3
    tau_c = tau_scr[slot_c]
    w_t = wt_ref[...]

    acc_scr[...] = jnp.zeros(acc_scr.shape, F32)
    for g in range(n_groups):
        lta_scr[:, g * pair:(g + 1) * pair] = idx_dot(0, g)
        ltc_scr[:, g * pair:(g + 1) * pair] = logits(0, g)

    bound_far = bound - jnp.concatenate([btile_ref[hh, 0:1, :] for hh in range(N_HEADS_A)], axis=1)

    def block_step(kb_raw, far):
        kb = jnp.minimum(kb_raw, n_a - 1)
        kb_next = jnp.minimum(kb_raw + 1, n_a - 1)
        kc = jnp.minimum(kb_raw, n_c - 1)
        kc_next = jnp.minimum(kb_raw + 1, n_c - 1)
        thr = jnp.where(has_c & (kb_raw < n_c), tau_c - 1, jnp.int32(2 ** 31 - 1))
        sel = keys_scr[slot_c, kc] > thr
        ct_blk = ckvt_ref[kc]
        start = bias_start(kc)
        ref_pt = bound_far if far else bound
        score = jnp.zeros((kb_sz, LANES), F32)
        for g in range(n_groups):
            s_t = lta_scr[:, g * pair:(g + 1) * pair]
            for j in range(2):
                hh = 2 * g + j
                score = score + jnp.maximum(s_t[:, j * LANES:(j + 1) * LANES], 0.0) * w_t[hh:hh + 1, :]
            lta_scr[:, g * pair:(g + 1) * pair] = idx_dot(kb_next, g)
            lt = ltc_scr[:, g * pair:(g + 1) * pair]
            ps = []
            for j in range(2):
                hh = 2 * g + j
                piece = lt[:, j * LANES:(j + 1) * LANES]
                if not far:
                    piece = piece + btile_ref[hh, pl.ds(start, kb_sz), :]
                ps.append(jnp.exp2(jnp.where(sel, piece, NEG_BIG) - ref_pt[:, hh * LANES:(hh + 1) * LANES]))
            ltc_scr[:, g * pair:(g + 1) * pair] = logits(kc_next, g)
            acc_scr[g] += jnp.dot(ct_blk, jnp.concatenate(ps, axis=1).astype(BF16),
                                  preferred_element_type=F32)
        valid = (kb * kb_sz + row_id) <= (qa0 + lane_id)
        keys = jnp.where(valid, _sortable_key(score), INT_MIN)
        keys_scr[slot_a, kb] = keys
        v = keys ^ INT_MIN
        for sub in range(kb_sz // PLANE_KEYS):
            r0 = sub * PLANE_KEYS
            words = _bit_transpose32([v[r0 + 8 * i:r0 + 8 * (i + 1), :] for i in range(32)])
            for bit in range(32):
                planes_scr[bit, kb * (kb_sz // PLANE_KEYS) + sub] = words[31 - bit]

    def block_body(far, it, carry):
        for u in range(BLOCK_UNROLL):
            block_step(it * BLOCK_UNROLL + u, far)
        return carry

    n_far_iters = jnp.maximum(n_c - 2, 0) // BLOCK_UNROLL
    lax.fori_loop(0, n_far_iters, functools.partial(block_body, True), 0)
    lax.fori_loop(n_far_iters, (n_a + BLOCK_UNROLL - 1) // BLOCK_UNROLL,
                  functools.partial(block_body, False), 0)
    n_kb = n_a


    n_planes = planes_scr.shape[1]
    live = n_kb * (kb_sz // PLANE_KEYS)

    def radix_select(width):
        blk_id = lax.broadcasted_iota(I32, (width, 8, LANES), 0)
        cand_scr[:width] = jnp.where(blk_id < live, -1, 0)

        def bit_body(it, carry):
            above, tau_u = carry
            bit = 31 - it
            ones = cand_scr[:width] & planes_scr[bit, :width]
            c1 = jnp.sum(jnp.sum(lax.population_count(ones), axis=0), axis=0, keepdims=True)
            take = (above + c1) >= topk
            cand_scr[:width] = jnp.where(take, ones, cand_scr[:width] ^ ones)
            above = jnp.where(take, above, above + c1)
            tau_u = jnp.where(take, tau_u | (jnp.int32(1) << bit), tau_u)
            return above, tau_u

        zero = jnp.zeros((1, LANES), I32)
        above, tau_u = lax.fori_loop(0, 32, bit_body, (zero, zero))
        n_eq = jnp.sum(jnp.sum(lax.population_count(cand_scr[:width]), axis=0), axis=0, keepdims=True)
        return above, tau_u, n_eq

    widths = [n_planes * (i + 1) // 4 for i in range(4)]
    select = functools.partial(radix_select, widths[-1])
    for width in reversed(widths[:-1]):
        select = functools.partial(lax.cond, live <= width, functools.partial(radix_select, width), select)
    n_gt, tau_u, n_eq = select()
    tau = tau_u ^ INT_MIN

    need = topk - n_gt
    overflow = n_eq > need
    seq_bits = max(1, (n_blocks * kb_sz - 1).bit_length())

    @pl.when(jnp.max(jnp.where(overflow, 1, 0)) > 0)
    def _():
        def count_ties_before(trial):
            def body(kb, acc):
                hit = jnp.where((keys_scr[slot_a, kb] == tau) & ((kb * kb_sz + row_id) < trial), 1, 0)
                return acc + jnp.sum(hit.reshape(kb_sz // 8, 8, LANES), axis=0)
            acc = lax.fori_loop(0, n_kb, body, jnp.zeros((8, LANES), I32))
            return jnp.sum(acc, axis=0, keepdims=True)

        def idx_body(it, jc):
            trial = jc | (jnp.int32(1) << (seq_bits - 1 - it))
            return jnp.where(count_ties_before(trial) < need, trial, jc)

        j_cut = lax.fori_loop(0, seq_bits, idx_body, jnp.zeros((1, LANES), I32))

        def demote_body(kb, carry):
            k = keys_scr[slot_a, kb]
            drop = overflow & (k == tau) & ((kb * kb_sz + row_id) > j_cut)
            keys_scr[slot_a, kb] = jnp.where(drop, INT_MIN, k)
            return carry

        lax.fori_loop(0, n_kb, demote_body, 0)

    tau_scr[slot_a] = jnp.maximum(tau, INT_MIN + 1)

    l_min = jnp.min(jnp.concatenate([acc_scr[g, D_LATENT:D_LATENT + 1, :] for g in range(n_groups)], axis=1))

    @pl.when(has_c & jnp.logical_not(l_min >= 2.0 ** -80))
    def _():
        m_scr[...] = jnp.full(m_scr.shape, NEG_BIG, F32)
        acc_scr[...] = jnp.zeros(acc_scr.shape, F32)

        def exact_body(kb, carry):
            sel = keys_scr[slot_c, kb] >= tau_c
            ct_blk = ckvt_ref[kb]
            start = bias_start(kb)
            for g in range(n_groups):
                lt = logits(kb, g)
                ps, alphas = [], []
                for j in range(2):
                    hh = 2 * g + j
                    sl = slice(hh * LANES, (hh + 1) * LANES)
                    piece = lt[:, j * LANES:(j + 1) * LANES] + btile_ref[hh, pl.ds(start, kb_sz), :]
                    masked = jnp.where(sel, piece, NEG_BIG)
                    m_old = m_scr[:, sl]
                    m_new = jnp.maximum(m_old, jnp.max(masked, axis=0, keepdims=True))
                    m_scr[:, sl] = m_new
                    alphas.append(jnp.exp2(m_old - m_new))
                    ps.append(jnp.exp2(masked - m_new))
                pv = jnp.dot(ct_blk, jnp.concatenate(ps, axis=1).astype(BF16), preferred_element_type=F32)
                acc_scr[g] = jnp.concatenate(alphas, axis=1) * acc_scr[g] + pv
            return carry

        lax.fori_loop(0, n_c, exact_body, 0)

    @pl.when(has_c)
    def _():
        ys = []
        for hh in range(N_HEADS_A):
            acc_h = acc_scr[hh // 2, :, (hh % 2) * LANES:(hh % 2 + 1) * LANES]
            o_h = acc_h[:D_LATENT] * (1.0 / acc_h[D_LATENT:D_LATENT + 1])
            ys.append(jnp.dot(wuvt_ref[hh], o_h.astype(BF16), preferred_element_type=F32))
        y_t = jnp.concatenate(ys, axis=0)
        out_ref[...] = y_t.T.astype(BF16)


def _dsa(q_idx, q_abs, w_t, k_idx, ckv, ckv_t, btile, bmax, wuv_t, *, batch, seq):
    t = batch * seq
    nqb = seq // Q_BLOCK
    nkb = seq // KEY_BLOCK
    topk = min(TOPK_MAX, seq // 4)
    k_idx3 = k_idx.reshape(t // KEY_BLOCK, KEY_BLOCK, LANES)
    ckv3 = ckv.reshape(t // KEY_BLOCK, KEY_BLOCK, D_LATENT)
    per_batch = lambda shape: pl.BlockSpec(shape, lambda b, q: (b,) + (0,) * (len(shape) - 1),
                                           pipeline_mode=pl.Buffered(1))
    scored = lambda b, s: b * nqb + jnp.minimum(s, nqb - 1)
    attended = lambda b, s: b * nqb + jnp.maximum(s - 1, 0)
    return pl.pallas_call(
        functools.partial(_dsa_kernel, topk=topk, n_qb=nqb),
        out_shape=jax.ShapeDtypeStruct((t, W_A), BF16),
        grid=(batch, nqb + 1),
        in_specs=[pl.BlockSpec((1, N_HEADS_IDX, Q_BLOCK, LANES), lambda b, s: (scored(b, s), 0, 0, 0)),
                  pl.BlockSpec((1, N_HEADS_A, Q_BLOCK, D_LATENT), lambda b, s: (attended(b, s), 0, 0, 0)),
                  pl.BlockSpec((N_HEADS_IDX, Q_BLOCK), lambda b, s: (0, scored(b, s))),
                  per_batch((nkb, KEY_BLOCK, LANES)),
                  per_batch((nkb, KEY_BLOCK, D_LATENT)),
                  per_batch((nkb, CKVT_ROWS, KEY_BLOCK)),
                  _resident((N_HEADS_A, BIAS_ROWS, LANES)),
                  _resident((1, N_HEADS_A * LANES)),
                  _resident((N_HEADS_A, HEAD_DIM_A, D_LATENT))],
        out_specs=pl.BlockSpec((Q_BLOCK, W_A), lambda b, s: (attended(b, s), 0)),
        scratch_shapes=[pltpu.VMEM((2, nkb, KEY_BLOCK, LANES), I32),
                        pltpu.VMEM((32, seq // PLANE_KEYS, 8, LANES), I32),
                        pltpu.VMEM((seq // PLANE_KEYS, 8, LANES), I32),
                        pltpu.VMEM((2, 1, LANES), I32),
                        pltpu.VMEM((N_HEADS_A // 2, CKVT_ROWS, 2 * LANES), F32),
                        pltpu.VMEM((1, N_HEADS_A * LANES), F32),
                        pltpu.VMEM((KEY_BLOCK, N_HEADS_A * LANES), F32),
                        pltpu.VMEM((KEY_BLOCK, N_HEADS_A * LANES), F32),
                        pltpu.VMEM((1, 1), F32),
                        pltpu.VMEM((N_HEADS_IDX // 2, LANES, 2 * LANES), BF16),
                        pltpu.VMEM((N_HEADS_A // 2, D_LATENT, 2 * LANES), BF16)],
        compiler_params=pltpu.CompilerParams(dimension_semantics=("arbitrary", "arbitrary"),
                                             vmem_limit_bytes=VMEM_LIMIT),
        name="dsa",
    )(q_idx, q_abs, w_t, k_idx3, ckv3, ckv_t, btile, bmax, wuv_t)


def _mlstm_kernel(qk_ref, v_ref, o_ref, ift_ref, gbt_ref, hn_ref,
                  out_ref, cx_scr, m_scr, *, chunk, n_batch):
    L = chunk

    @pl.when(pl.program_id(0) == 0)
    def _():
        cx_scr[...] = jnp.zeros(cx_scr.shape, F32)
        m_scr[...] = jnp.zeros(m_scr.shape, F32)

    rr = lax.broadcasted_iota(I32, (L, L), 0)
    cc = lax.broadcasted_iota(I32, (L, L), 1)
    causal = cc <= rr
    triu = jnp.where(rr <= cc, 1.0, 0.0).astype(BF16)
    lane = lax.broadcasted_iota(I32, (8, L), 1)
    ones_col = jnp.where(lax.broadcasted_iota(I32, (L, HEAD_DIM_M), 1) == 0, 1.0, 0.0).astype(BF16)
    for bi in range(n_batch):
        _mlstm_chunk(qk_ref.at[bi], v_ref.at[bi], o_ref.at[bi], ift_ref.at[bi], gbt_ref, hn_ref,
                     out_ref.at[bi], cx_scr.at[bi], m_scr.at[bi], causal, triu, lane, ones_col, L)


def _mlstm_chunk(qk_ref, v_ref, o_ref, ift_ref, gbt_ref, hn_ref, out_ref, cx_scr, m_scr,
                 causal, triu, lane, ones_col, L):
    g_t = ift_ref[...] + gbt_ref[...]
    b_all = sum(jnp.dot(piece, triu, preferred_element_type=F32) for piece in _split3(_log_sigmoid(g_t)))
    b8 = pltpu.roll(b_all, N_HEADS_M, axis=0)
    a8 = g_t - b8
    cm = a8
    shift = 1
    while shift < L:
        cm = jnp.maximum(cm, jnp.where(lane >= shift, pltpu.roll(cm, shift, axis=1), NEG_BIG))
        shift *= 2
    m_prev = m_scr[...]
    mx = jnp.maximum(m_prev, cm)
    mx_last = mx[:, L - 1:L]
    decay8 = jnp.exp(m_prev - mx_last)
    m_scr[...] = b8[:, L - 1:L] + mx_last
    rows = jnp.concatenate([-mx,
                            jnp.exp(m_prev - mx),
                            jnp.exp(-(b8 + mx)),
                            jnp.exp(a8 - mx_last),
                            jnp.zeros((LANES - 32, L), F32)], axis=0)
    cols = rows.T

    o_gate = _sigmoid(o_ref[...])
    for hh in range(N_HEADS_M):
        hs = slice(hh * HEAD_DIM_M, (hh + 1) * HEAD_DIM_M)
        qb16 = qk_ref[:, hs]
        kb16 = qk_ref[:, W_M + hh * HEAD_DIM_M:W_M + (hh + 1) * HEAD_DIM_M]
        v_ext = jnp.concatenate([v_ref[:, hs], ones_col], axis=1)
        u_c = cols[:, hh:hh + 1]
        w_inter = cols[:, 8 + hh:9 + hh]
        em_c = cols[:, 16 + hh:17 + hh]
        wgt_c = cols[:, 24 + hh:25 + hh]
        cx_prev = cx_scr[hh]

        d_mat = jnp.where(causal, jnp.exp(u_c + a8[hh:hh + 1, :]), 0.0)
        s = lax.dot_general(qb16, kb16, (((1,), (1,)), ((), ())), preferred_element_type=F32) * d_mat
        intra = jnp.dot(s.astype(BF16), v_ext, preferred_element_type=F32)
        inter = jnp.dot(qb16, cx_prev.astype(BF16), preferred_element_type=F32)
        both = w_inter * inter + intra
        num = both[:, :HEAD_DIM_M]
        den = both[:, HEAD_DIM_M:HEAD_DIM_M + 1]
        hval = num / jnp.maximum(jnp.abs(den), em_c)

        kw = kb16.astype(F32) * wgt_c
        cx_scr[hh] = decay8[hh:hh + 1] * cx_prev + jnp.dot(kw.T.astype(BF16), v_ext,
                                                           preferred_element_type=F32)

        mu = jnp.mean(hval, axis=1, keepdims=True)
        cen = hval - mu
        var = jnp.mean(cen * cen, axis=1, keepdims=True)
        hn = cen * lax.rsqrt(var + EPS) * hn_ref[:, hs]
        out_ref[:, hs] = (hn * o_gate[:, hs]).astype(BF16)


def _mlstm(qk, v, o_pre, ift, gate_bias, head_norm, *, batch, seq):
    t = batch * seq
    L = MLSTM_CHUNK
    nc = seq // L
    gbt = jnp.broadcast_to(gate_bias.reshape(2 * N_HEADS_M, 1), (2 * N_HEADS_M, L))
    ift_b = ift.reshape(2 * N_HEADS_M, batch, seq).transpose(1, 0, 2)
    row = lambda w: pl.BlockSpec((batch, L, w), lambda c: (0, c, 0))
    out = pl.pallas_call(
        functools.partial(_mlstm_kernel, chunk=L, n_batch=batch),
        out_shape=jax.ShapeDtypeStruct((batch, seq, W_M), BF16),
        grid=(nc,),
        in_specs=[row(2 * W_M), row(W_M), row(W_M),
                  pl.BlockSpec((batch, 2 * N_HEADS_M, L), lambda c: (0, 0, c)),
                  _resident((2 * N_HEADS_M, L)), _resident((1, W_M))],
        out_specs=row(W_M),
        scratch_shapes=[pltpu.VMEM((batch, N_HEADS_M, HEAD_DIM_M, 2 * HEAD_DIM_M), F32),
                        pltpu.VMEM((batch, 8, 1), F32)],
        compiler_params=pltpu.CompilerParams(dimension_semantics=("arbitrary",),
                                             vmem_limit_bytes=VMEM_LIMIT),
        name="mlstm",
    )(qk.reshape(batch, seq, 2 * W_M), v.reshape(batch, seq, W_M), o_pre.reshape(batch, seq, W_M),
      ift_b, gbt, head_norm.reshape(1, -1))
    return out.reshape(t, W_M)


def kernel(x, c, ada_w, ada_b, ffn1_norm, ffn1_w1, ffn1_w3, ffn1_w2, mix_norm, w_in, conv_w, conv_b,
           kv_norm, w_uk, w_uv, mlstm_gate_bias, mlstm_head_norm, rel_bias, w_branch_attn,
           w_branch_mlstm, w_out, ffn2_norm, ffn2_w1, ffn2_w3, ffn2_w2, final_norm):
    batch, seq, d = x.shape
    depth = ada_w.shape[0]
    assert seq % max(FFN_TM, MIX_TM, MLSTM_CHUNK, KEY_BLOCK) == 0
    t = batch * seq
    xf = x.reshape(t, d)
    btile, bmax = _bias_tiles(rel_bias)
    for l in range(depth):
        mod = _adaln(c, ada_w[l], ada_b[l]).reshape(batch, 9, 1, d)
        sh1, sc1, g1, sh2, sc2, g2, sh3, sc3, g3 = [mod[:, n] for n in range(9)]
        xf = _ffn(xf, ffn1_norm[l], sh1, sc1, g1, ffn1_w1[l], ffn1_w3[l], ffn1_w2[l], final_norm,
                  seq=seq, final_norm=False)
        wuk_hdc = w_uk[l].transpose(0, 2, 1).reshape(N_HEADS_A // 2, 2, HEAD_DIM_A, D_LATENT)
        zeros = jnp.zeros_like(wuk_hdc[:, 0])
        wuk_t = jnp.concatenate([jnp.concatenate([wuk_hdc[:, 0], zeros], axis=2),
                                 jnp.concatenate([zeros, wuk_hdc[:, 1]], axis=2)], axis=1).astype(BF16)
        (q_abs, q_idx, k_idx, ckv, ckv_t, w_t, ift, qk_m, v_m, o_pre, gate_a, gate_m) = _mixin(
            xf, mix_norm[l], sh2, sc2, _pack_w_in(w_in[l], d), kv_norm[l], wuk_t, conv_w[l], conv_b[l],
            seq=seq)
        wuv_t = w_uv[l].transpose(0, 2, 1).astype(BF16)
        y_a = _dsa(q_idx, q_abs, w_t, k_idx, ckv, ckv_t, btile, bmax, wuv_t, batch=batch, seq=seq)
        h_m = _mlstm(qk_m, v_m, o_pre, ift, mlstm_gate_bias[l], mlstm_head_norm[l], batch=batch, seq=seq)
        xf = _ffn(xf, ffn2_norm[l], sh3, sc3, g3, ffn2_w1[l], ffn2_w3[l], ffn2_w2[l], final_norm,
                  seq=seq, final_norm=(l == depth - 1),
                  merge=(y_a, h_m, gate_a, gate_m, g2, w_branch_attn[l], w_branch_mlstm[l], w_out[l]))
    return xf.reshape(batch, seq, d)
```

```python
import functools
import math

import jax
import jax.numpy as jnp
from jax import lax
from jax.experimental import pallas as pl
from jax.experimental.pallas import tpu as pltpu

F32 = jnp.float32
BF16 = jnp.bfloat16
I32 = jnp.int32

LANES = 128
MXU_DIM = 256
VMEM_LIMIT = 56 * 1024 * 1024

N_HEADS_A = 8
HEAD_DIM_A = 64
D_LATENT = 256
N_HEADS_IDX = 8
HEAD_DIM_IDX = 64
TOPK_MAX = 256
Q_BLOCK = 128
N_BUCKETS = 32
MAX_DISTANCE = 128
N_HEADS_M = 4
HEAD_DIM_M = 128
CONV_WIDTH = 4
EPS = 1e-6
IDX_SCALE = (N_HEADS_IDX ** -0.5) * (HEAD_DIM_IDX ** -0.5)
W_A = N_HEADS_A * HEAD_DIM_A
W_M = N_HEADS_M * HEAD_DIM_M

FFN_TM = 512
FFN_CHUNK = 256
MIX_TM = 512
KEY_BLOCK = 256
BLOCK_UNROLL = 2
PLANE_KEYS = 256
MLSTM_CHUNK = 256
NEG_BIG = -1e30
INT_MIN = -2 ** 31

BIAS_PAD = 2 * KEY_BLOCK - Q_BLOCK
BIAS_ROWS = KEY_BLOCK + BIAS_PAD
CKVT_ROWS = D_LATENT + 16
LOG2E = math.log2(math.e)


def _sigmoid(x):
    return 1.0 / (1.0 + jnp.exp(-x))


def _log_sigmoid(x):
    return jnp.minimum(x, 0.0) - jnp.log(1.0 + jnp.exp(-jnp.abs(x)))


def _rms_norm(x, gain):
    ms = jnp.mean(x * x, axis=-1, keepdims=True)
    return x * lax.rsqrt(ms + EPS) * gain


def _split3(x):
    hi = x.astype(BF16)
    r1 = x - hi.astype(F32)
    mid = r1.astype(BF16)
    lo = (r1 - mid.astype(F32)).astype(BF16)
    return hi, mid, lo


def _resident(shape):
    nd = len(shape)
    return pl.BlockSpec(shape, lambda *_: (0,) * nd, pipeline_mode=pl.Buffered(1))


def _adaln_kernel(c_ref, w_ref, b_ref, o_ref):
    c = c_ref[...]
    cond = c * _sigmoid(c)
    o_ref[...] = jnp.dot(cond.astype(BF16), w_ref[...].astype(BF16),
                         preferred_element_type=F32) + b_ref[...]


def _adaln(c, ada_w, ada_b):
    b, d = c.shape
    n = ada_w.shape[1]
    rows = 8
    c_pad = jnp.zeros((rows, d), F32).at[:b].set(c)
    tn = 1024
    out = pl.pallas_call(
        _adaln_kernel,
        out_shape=jax.ShapeDtypeStruct((rows, n), F32),
        grid=(n // tn,),
        in_specs=[pl.BlockSpec((rows, d), lambda j: (0, 0)),
                  pl.BlockSpec((d, tn), lambda j: (0, j)),
                  pl.BlockSpec((1, tn), lambda j: (0, j))],
        out_specs=pl.BlockSpec((rows, tn), lambda j: (0, j)),
        compiler_params=pltpu.CompilerParams(dimension_semantics=("arbitrary",),
                                             vmem_limit_bytes=VMEM_LIMIT),
        name="adaln",
    )(c_pad, ada_w, ada_b.reshape(1, n))
    return out[:b]


def _t5_bucket(dist):
    n = jnp.maximum(dist, 0)
    max_exact = N_BUCKETS // 2
    nf = jnp.maximum(n, 1).astype(F32)
    large = max_exact + (jnp.log(nf / max_exact) / math.log(MAX_DISTANCE / max_exact)
                         * (N_BUCKETS - max_exact)).astype(I32)
    large = jnp.minimum(large, N_BUCKETS - 1)
    return jnp.where(n < max_exact, n, large)


def _bias_kernel(rel_ref, tile_ref, max_ref):
    r = lax.broadcasted_iota(I32, (BIAS_ROWS, LANES), 0)
    i = lax.broadcasted_iota(I32, (BIAS_ROWS, LANES), 1)
    bucket = _t5_bucket(i - r + BIAS_PAD)
    for h in range(N_HEADS_A):
        acc = jnp.zeros((BIAS_ROWS, LANES), F32)
        top = rel_ref[0, h] * LOG2E
        for bkt in range(N_BUCKETS):
            val = rel_ref[bkt, h] * LOG2E
            acc = jnp.where(bucket == bkt, val, acc)
            top = jnp.maximum(top, val)
        tile_ref[h] = acc
        max_ref[:, h * LANES:(h + 1) * LANES] = jnp.full((1, LANES), top, F32)


def _bias_tiles(rel_bias):
    return pl.pallas_call(
        _bias_kernel,
        out_shape=(jax.ShapeDtypeStruct((N_HEADS_A, BIAS_ROWS, LANES), F32),
                   jax.ShapeDtypeStruct((1, N_HEADS_A * LANES), F32)),
        in_specs=[pl.BlockSpec(memory_space=pltpu.SMEM)],
        out_specs=(pl.BlockSpec(memory_space=pltpu.VMEM), pl.BlockSpec(memory_space=pltpu.VMEM)),
        name="bias_tiles",
    )(rel_bias)


def _ffn_kernel(*refs, n_chunks, final_norm, merge):
    if merge:
        (x_ref, ya_ref, hm_ref, ga_ref, gm_ref, gmix_ref, wa_ref, wm_ref, wo_ref), refs = refs[:9], refs[9:]
    else:
        x_ref, refs = refs[0], refs[1:]
    gain_ref, sh_ref, sc_ref, g_ref, w1_ref, w3_ref, w2_ref, fin_ref, o_ref, h_scr, acc_scr = refs
    x = x_ref[...]
    if merge:
        pa = jnp.dot(ya_ref[...], wa_ref[...], preferred_element_type=F32)
        pm = jnp.dot(hm_ref[...], wm_ref[...], preferred_element_type=F32)
        merged = _sigmoid(ga_ref[...]) * pa + _sigmoid(gm_ref[...]) * pm
        x = x + gmix_ref[0] * jnp.dot(merged.astype(BF16), wo_ref[...], preferred_element_type=F32)
    h = _rms_norm(x, gain_ref[...]) * (1.0 + sc_ref[0]) + sh_ref[0]
    h_scr[...] = h.astype(BF16)
    for j in range(n_chunks):
        hb = h_scr[...]
        cols = slice(j * FFN_CHUNK, (j + 1) * FFN_CHUNK)
        u1 = jnp.dot(hb, w1_ref[:, cols], preferred_element_type=F32)
        u3 = jnp.dot(hb, w3_ref[:, cols], preferred_element_type=F32)
        a = (u1 * _sigmoid(u1)) * u3
        part = jnp.dot(a.astype(BF16), w2_ref[j], preferred_element_type=F32)
        if j == 0:
            acc_scr[...] = part
        else:
            acc_scr[...] += part
    out = x + (0.5 * g_ref[0]) * acc_scr[...]
    if final_norm:
        out = _rms_norm(out, fin_ref[...])
    o_ref[...] = out


def _ffn(x, gain, sh, sc, g, w1, w3, w2, fin, *, seq, final_norm, merge=None):
    t, d = x.shape
    dff = w1.shape[1]
    nch = dff // FFN_CHUNK
    w1c = w1.astype(BF16)
    w3c = w3.astype(BF16)
    w2c = w2.astype(BF16).reshape(nch, FFN_CHUNK, d)
    tm = FFN_TM
    per_b = seq // tm
    row = lambda w: pl.BlockSpec((tm, w), lambda i: (i, 0))
    mod_spec = pl.BlockSpec((1, 1, d), lambda i: (i // per_b, 0, 0))
    merge_specs, merge_args = [], []
    if merge is not None:
        y_a, h_m, gate_a, gate_m, g_mix, w_a, w_m, w_o = merge
        merge_specs = [row(W_A), row(W_M), row(d), row(d), mod_spec,
                       _resident((W_A, d)), _resident((W_M, d)), _resident((d, d))]
        merge_args = [y_a, h_m, gate_a, gate_m, g_mix, w_a.astype(BF16), w_m.astype(BF16), w_o.astype(BF16)]
    return pl.pallas_call(
        functools.partial(_ffn_kernel, n_chunks=nch, final_norm=final_norm, merge=merge is not None),
        out_shape=jax.ShapeDtypeStruct((t, d), F32),
        grid=(t // tm,),
        in_specs=[row(d)] + merge_specs + [
                  _resident((1, d)), mod_spec, mod_spec, mod_spec,
                  _resident((d, dff)), _resident((d, dff)),
                  _resident((nch, FFN_CHUNK, d)), _resident((1, d))],
        out_specs=row(d),
        scratch_shapes=[pltpu.VMEM((tm, d), BF16), pltpu.VMEM((tm, d), F32)],
        compiler_params=pltpu.CompilerParams(dimension_semantics=("arbitrary",),
                                             vmem_limit_bytes=VMEM_LIMIT),
        name="ffn_final" if final_norm else "ffn",
    )(x, *merge_args, gain.reshape(1, d), sh, sc, g, w1c, w3c, w2c, fin.reshape(1, d))


_C_QA = 0
_C_CKV = _C_QA + W_A
_C_QI = _C_CKV + D_LATENT
_C_KI = _C_QI + N_HEADS_IDX * LANES
_C_SM = _C_KI + LANES
_C_QK = _C_SM + LANES
_C_V = _C_QK + 2 * W_M
_C_O = _C_V + W_M
_C_GA = _C_O + W_M
_C_GM = _C_GA + 1024
_C_END = _C_GM + 1024
_SM_W = 0
_SM_I = N_HEADS_IDX
_SM_F = N_HEADS_IDX + N_HEADS_M


def _pack_w_in(w_in, d_model):
    splits = (W_A, D_LATENT, N_HEADS_IDX * HEAD_DIM_IDX, HEAD_DIM_IDX, N_HEADS_IDX,
              W_M, W_M, W_M, N_HEADS_M, N_HEADS_M, W_M, d_model, d_model)
    offs = [0]
    for s in splits:
        offs.append(offs[-1] + s)
    (q_a, c_kv, q_i, k_i, w_i, q_m, k_m, v_m, i_p, f_p, o_p, g_a, g_m) = [
        w_in[:, offs[n]:offs[n + 1]] for n in range(len(splits))]
    d = w_in.shape[0]

    def pad_heads(w, nh, hd):
        w = w.reshape(d, nh, hd)
        return jnp.pad(w, ((0, 0), (0, 0), (0, LANES - hd))).reshape(d, nh * LANES)

    small = jnp.concatenate([w_i, i_p, f_p], axis=1)
    small = jnp.pad(small, ((0, 0), (0, LANES - small.shape[1])))
    packed = jnp.concatenate([
        q_a, c_kv, pad_heads(q_i, N_HEADS_IDX, HEAD_DIM_IDX),
        jnp.pad(k_i, ((0, 0), (0, LANES - HEAD_DIM_IDX))), small, q_m, k_m, v_m, o_p, g_a, g_m], axis=1)
    assert packed.shape[1] == _C_END
    return packed.astype(BF16)


def _mixin_kernel(x_ref, gain_ref, sh_ref, sc_ref, w_ref, kvn_ref, wuk_ref, cw_ref, cb_ref,
                  qabs_ref, qidx_ref, kidx_ref, ckv_ref, ckvt_ref, wt_ref, ift_ref,
                  qk_ref, v_ref, o_ref, ga_ref, gm_ref, h_scr, xe_scr, *, tm, tiles_per_seq):
    nqb = tm // Q_BLOCK

    @pl.when(pl.program_id(0) % tiles_per_seq == 0)
    def _():
        xe_scr[:8] = jnp.zeros((8, xe_scr.shape[1]), F32)

    x = x_ref[...]
    h = _rms_norm(x, gain_ref[...]) * (1.0 + sc_ref[0]) + sh_ref[0]
    h_scr[...] = h.astype(BF16)

    def proj(lo, hi):
        return jnp.dot(h_scr[...], w_ref[:, lo:hi], preferred_element_type=F32)

    qa = proj(_C_QA, _C_CKV)
    scale = HEAD_DIM_A ** -0.5 * LOG2E
    for g in range(N_HEADS_A // 2):
        q_pair = qa[:, g * LANES:(g + 1) * LANES].astype(BF16)
        q_abs = jnp.dot(q_pair, wuk_ref[g], preferred_element_type=F32) * scale
        for j in range(2):
            qabs_ref[:, 2 * g + j] = (q_abs[:, j * D_LATENT:(j + 1) * D_LATENT]
                                      .astype(BF16).reshape(nqb, Q_BLOCK, D_LATENT))
    ckv = _rms_norm(proj(_C_CKV, _C_QI), kvn_ref[...])
    ckv_ref[...] = ckv.astype(BF16)
    ckv_t = ckv.T
    ones_row = jnp.where(lax.broadcasted_iota(I32, (CKVT_ROWS - D_LATENT, KEY_BLOCK), 0) == 0, 1.0, 0.0)
    for j in range(tm // KEY_BLOCK):
        ckvt_ref[j, :D_LATENT] = ckv_t[:, j * KEY_BLOCK:(j + 1) * KEY_BLOCK].astype(BF16)
        ckvt_ref[j, D_LATENT:] = ones_row.astype(BF16)
    qi = proj(_C_QI, _C_KI)
    for hh in range(N_HEADS_IDX):
        qidx_ref[:, hh] = qi[:, hh * LANES:(hh + 1) * LANES].astype(BF16).reshape(nqb, Q_BLOCK, LANES)
    kidx_ref[...] = proj(_C_KI, _C_SM).astype(BF16)
    small_t = proj(_C_SM, _C_QK).T
    wt_ref[...] = small_t[_SM_W:_SM_W + N_HEADS_IDX] * IDX_SCALE
    ift_ref[...] = small_t[_SM_I:_SM_I + 2 * N_HEADS_M]
    xe_scr[8:] = proj(_C_QK, _C_V)
    xe = xe_scr[...]
    xq = xe[8:]
    conv = xq * cw_ref[CONV_WIDTH - 1:CONV_WIDTH, :] + cb_ref[...]
    for d in range(1, CONV_WIDTH):
        conv = conv + pltpu.roll(xe, d, axis=0)[8:] * cw_ref[CONV_WIDTH - 1 - d:CONV_WIDTH - d, :]
    xe_scr[:8] = xe_scr[tm:]
    qk = conv * _sigmoid(conv)
    qk_ref[:, :W_M] = qk[:, :W_M].astype(BF16)
    qk_ref[:, W_M:] = (qk[:, W_M:] * (HEAD_DIM_M ** -0.5)).astype(BF16)
    v_ref[...] = proj(_C_V, _C_O).astype(BF16)
    o_ref[...] = proj(_C_O, _C_GA)
    ga_ref[...] = proj(_C_GA, _C_GM)
    gm_ref[...] = proj(_C_GM, _C_END)


def _mixin(x, gain, sh, sc, w_packed, kv_norm, wuk_t, conv_w, conv_b, *, seq):
    t, d = x.shape
    tm = MIX_TM
    per_b = seq // tm
    nqb = tm // Q_BLOCK
    row = lambda w: pl.BlockSpec((tm, w), lambda i: (i, 0))
    mod_spec = pl.BlockSpec((1, 1, d), lambda i: (i // per_b, 0, 0))
    out_shape = (
        jax.ShapeDtypeStruct((t // Q_BLOCK, N_HEADS_A, Q_BLOCK, D_LATENT), BF16),
        jax.ShapeDtypeStruct((t // Q_BLOCK, N_HEADS_IDX, Q_BLOCK, LANES), BF16),
        jax.ShapeDtypeStruct((t, LANES), BF16),
        jax.ShapeDtypeStruct((t, D_LATENT), BF16),
        jax.ShapeDtypeStruct((t // KEY_BLOCK, CKVT_ROWS, KEY_BLOCK), BF16),
        jax.ShapeDtypeStruct((N_HEADS_IDX, t), F32),
        jax.ShapeDtypeStruct((2 * N_HEADS_M, t), F32),
        jax.ShapeDtypeStruct((t, 2 * W_M), BF16),
        jax.ShapeDtypeStruct((t, W_M), BF16),
        jax.ShapeDtypeStruct((t, W_M), F32),
        jax.ShapeDtypeStruct((t, d), F32),
        jax.ShapeDtypeStruct((t, d), F32),
    )
    out_specs = (
        pl.BlockSpec((nqb, N_HEADS_A, Q_BLOCK, D_LATENT), lambda i: (i, 0, 0, 0)),
        pl.BlockSpec((nqb, N_HEADS_IDX, Q_BLOCK, LANES), lambda i: (i, 0, 0, 0)),
        row(LANES), row(D_LATENT),
        pl.BlockSpec((tm // KEY_BLOCK, CKVT_ROWS, KEY_BLOCK), lambda i: (i, 0, 0)),
        pl.BlockSpec((N_HEADS_IDX, tm), lambda i: (0, i)),
        pl.BlockSpec((2 * N_HEADS_M, tm), lambda i: (0, i)),
        row(2 * W_M), row(W_M), row(W_M), row(d), row(d),
    )
    return pl.pallas_call(
        functools.partial(_mixin_kernel, tm=tm, tiles_per_seq=per_b),
        out_shape=out_shape,
        grid=(t // tm,),
        in_specs=[pl.BlockSpec((tm, d), lambda i: (i, 0)), _resident((1, d)), mod_spec, mod_spec,
                  _resident((d, _C_END)), _resident((1, D_LATENT)),
                  _resident((N_HEADS_A // 2, LANES, 2 * D_LATENT)),
                  _resident((CONV_WIDTH, 2 * W_M)), _resident((1, 2 * W_M))],
        out_specs=out_specs,
        scratch_shapes=[pltpu.VMEM((tm, d), BF16), pltpu.VMEM((tm + 8, 2 * W_M), F32)],
        compiler_params=pltpu.CompilerParams(dimension_semantics=("arbitrary",),
                                             vmem_limit_bytes=VMEM_LIMIT),
        name="mixin",
    )(x, gain.reshape(1, d), sh, sc, w_packed, kv_norm.reshape(1, D_LATENT), wuk_t,
      conv_w, conv_b.reshape(1, -1))


def _sortable_key(score):
    bits = pltpu.bitcast(score, I32)
    bits = jnp.where(bits == INT_MIN, 0, bits)
    return jnp.where(bits < 0, bits ^ 0x7FFFFFFF, bits)


def _bit_transpose32(words):
    v = list(words)
    j, m = 16, 0x0000FFFF
    while j:
        k = 0
        while k < 32:
            t = (v[k] ^ lax.shift_right_logical(v[k + j], jnp.int32(j))) & m
            v[k] = v[k] ^ t
            v[k + j] = v[k + j] ^ (t << j)
            k = (k + j + 1) & ~j
        j >>= 1
        m = (m ^ (m << j)) & 0x7FFFFFFF
    return v


def _dsa_kernel(qidx_ref, qabs_ref, wt_ref, kidx_ref, ckv_ref, ckvt_ref, btile_ref, bmax_ref, wuvt_ref,
                out_ref, keys_scr, planes_scr, cand_scr, tau_scr, acc_scr, m_scr, ltc_scr, kmax_scr,
                qta_scr, qtc_scr, *, topk, n_qb):
    kb_sz = KEY_BLOCK
    step = pl.program_id(1)
    has_c = step >= 1
    qa = jnp.minimum(step, n_qb - 1)
    qc = jnp.maximum(step - 1, 0)
    slot_a = step & 1
    slot_c = 1 - slot_a
    n_a = qa // (kb_sz // Q_BLOCK) + 1
    n_c = qc // (kb_sz // Q_BLOCK) + 1
    qa0 = qa * Q_BLOCK
    qc0 = qc * Q_BLOCK
    row_id = lax.broadcasted_iota(I32, (kb_sz, LANES), 0)
    lane_id = lax.broadcasted_iota(I32, (kb_sz, LANES), 1)
    n_blocks = keys_scr.shape[1]
    n_groups = N_HEADS_A // 2
    pair = 2 * LANES
    ones8 = jnp.ones((8, D_LATENT), BF16)

    @pl.when(step == 0)
    def _():
        keys_scr[1, 0] = jnp.full((kb_sz, LANES), INT_MIN, I32)
        tau_scr[1] = jnp.zeros((1, LANES), I32)
        planes_scr[...] = jnp.zeros(planes_scr.shape, I32)

        def kn_body(kb, mx):
            c = ckv_ref[kb].astype(F32)
            n2 = lax.dot_general(ones8, (c * c).astype(BF16), (((1,), (1,)), ((), ())),
                                 preferred_element_type=F32)
            return jnp.maximum(mx, n2[0:1])
        mx = lax.fori_loop(0, n_blocks, kn_body, jnp.zeros((1, kb_sz), F32))
        kmax_scr[...] = jnp.max(mx, axis=1, keepdims=True)

    for g in range(n_groups):
        qi_g = qidx_ref[0, 2 * g:2 * g + 2].reshape(2 * Q_BLOCK, LANES).astype(F32)
        qta_scr[g] = qi_g.T.astype(BF16)
        qa_g = qabs_ref[0, 2 * g:2 * g + 2].reshape(2 * Q_BLOCK, D_LATENT).astype(F32)
        qtc_scr[g] = qa_g.T.astype(BF16)

    def idx_dot(kb, g):
        return jnp.dot(kidx_ref[kb], qta_scr[g], preferred_element_type=F32)

    def logits(kb, g):
        return jnp.dot(ckv_ref[kb], qtc_scr[g], preferred_element_type=F32)

    def bias_start(kb):
        delta = jnp.minimum(qc0 - kb * kb_sz, BIAS_PAD)
        return pl.multiple_of(BIAS_PAD - delta, LANES)

    qn2 = []
    for g in range(n_groups):
        q_g = qabs_ref[0, 2 * g:2 * g + 2].reshape(2 * Q_BLOCK, D_LATENT).astype(F32)
        qn2.append(lax.dot_general(ones8, (q_g * q_g).astype(BF16), (((1,), (1,)), ((), ())),
                                   preferred_element_type=F32)[0:1])
    bound = jnp.sqrt(jnp.concatenate(qn2, axis=1) * kmax_scr[...]) * 1.02 + bmax_ref[...] + 1e-3
    tau_c = tau_scr[slot_c]
    w_t = wt_ref[...]

    acc_scr[...] = jnp.zeros(acc_scr.shape, F32)
    for g in range(n_groups):
        ltc_scr[:, g * pair:(g + 1) * pair] = logits(0, g)

    bound_far = bound - jnp.concatenate([btile_ref[hh, 0:1, :] for hh in range(N_HEADS_A)], axis=1)

    def block_step(kb_raw, far):
        kb = jnp.minimum(kb_raw, n_a - 1)
        kc = jnp.minimum(kb_raw, n_c - 1)
        kc_next = jnp.minimum(kb_raw + 1, n_c - 1)
        thr = jnp.where(has_c & (kb_raw < n_c), tau_c - 1, jnp.int32(2 ** 31 - 1))
        sel = keys_scr[slot_c, kc] > thr
        ct_blk = ckvt_ref[kc]
        start = bias_start(kc)
        ref_pt = bound_far if far else bound
        score = jnp.zeros((kb_sz, LANES), F32)
        for g in range(n_groups):
            s_t = idx_dot(kb, g)
            for j in range(2):
                hh = 2 * g + j
                score = score + jnp.maximum(s_t[:, j * LANES:(j + 1) * LANES], 0.0) * w_t[hh:hh + 1, :]
        for g in range(n_groups):
            lt = ltc_scr[:, g * pair:(g + 1) * pair]
            ps = []
            for j in range(2):
                hh = 2 * g + j
                piece = lt[:, j * LANES:(j + 1) * LANES]
                if not far:
                    piece = piece + btile_ref[hh, pl.ds(start, kb_sz), :]
                ps.append(jnp.exp2(jnp.where(sel, piece, NEG_BIG) - ref_pt[:, hh * LANES:(hh + 1) * LANES]))
            ltc_scr[:, g * pair:(g + 1) * pair] = logits(kc_next, g)
            acc_scr[g] += jnp.dot(ct_blk, jnp.concatenate(ps, axis=1).astype(BF16),
                                  preferred_element_type=F32)
        valid = (kb * kb_sz + row_id) <= (qa0 + lane_id)
        keys = jnp.where(valid, _sortable_key(score), INT_MIN)
        keys_scr[slot_a, kb] = keys
        v = keys ^ INT_MIN
        for sub in range(kb_sz // PLANE_KEYS):
            r0 = sub * PLANE_KEYS
            words = _bit_transpose32([v[r0 + 8 * i:r0 + 8 * (i + 1), :] for i in range(32)])
            for bit in range(32):
                planes_scr[bit, kb * (kb_sz // PLANE_KEYS) + sub] = words[31 - bit]

    def block_body(far, it, carry):
        for u in range(BLOCK_UNROLL):
            block_step(it * BLOCK_UNROLL + u, far)
        return carry

    n_far_iters = jnp.maximum(n_c - 2, 0) // BLOCK_UNROLL
    lax.fori_loop(0, n_far_iters, functools.partial(block_body, True), 0)
    lax.fori_loop(n_far_iters, (n_a + BLOCK_UNROLL - 1) // BLOCK_UNROLL,
                  functools.partial(block_body, False), 0)
    n_kb = n_a


    n_planes = planes_scr.shape[1]
    live = n_kb * (kb_sz // PLANE_KEYS)

    def radix_select(width):
        blk_id = lax.broadcasted_iota(I32, (width, 8, LANES), 0)
        cand_scr[:width] = jnp.where(blk_id < live, -1, 0)

        def bit_body(it, carry):
            above, tau_u = carry
            bit = 31 - it
            ones = cand_scr[:width] & planes_scr[bit, :width]
            c1 = jnp.sum(jnp.sum(lax.population_count(ones), axis=0), axis=0, keepdims=True)
            take = (above + c1) >= topk
            cand_scr[:width] = jnp.where(take, ones, cand_scr[:width] ^ ones)
            above = jnp.where(take, above, above + c1)
            tau_u = jnp.where(take, tau_u | (jnp.int32(1) << bit), tau_u)
            return above, tau_u

        zero = jnp.zeros((1, LANES), I32)
        above, tau_u = lax.fori_loop(0, 32, bit_body, (zero, zero))
        n_eq = jnp.sum(jnp.sum(lax.population_count(cand_scr[:width]), axis=0), axis=0, keepdims=True)
        return above, tau_u, n_eq

    widths = [n_planes * (i + 1) // 4 for i in range(4)]
    select = functools.partial(radix_select, widths[-1])
    for width in reversed(widths[:-1]):
        select = functools.partial(lax.cond, live <= width, functools.partial(radix_select, width), select)
    n_gt, tau_u, n_eq = select()
    tau = tau_u ^ INT_MIN

    need = topk - n_gt
    overflow = n_eq > need
    seq_bits = max(1, (n_blocks * kb_sz - 1).bit_length())

    @pl.when(jnp.max(jnp.where(overflow, 1, 0)) > 0)
    def _():
        def count_ties_before(trial):
            def body(kb, acc):
                hit = jnp.where((keys_scr[slot_a, kb] == tau) & ((kb * kb_sz + row_id) < trial), 1, 0)
                return acc + jnp.sum(hit.reshape(kb_sz // 8, 8, LANES), axis=0)
            acc = lax.fori_loop(0, n_kb, body, jnp.zeros((8, LANES), I32))
            return jnp.sum(acc, axis=0, keepdims=True)

        def idx_body(it, jc):
            trial = jc | (jnp.int32(1) << (seq_bits - 1 - it))
            return jnp.where(count_ties_before(trial) < need, trial, jc)

        j_cut = lax.fori_loop(0, seq_bits, idx_body, jnp.zeros((1, LANES), I32))

        def demote_body(kb, carry):
            k = keys_scr[slot_a, kb]
            drop = overflow & (k == tau) & ((kb * kb_sz + row_id) > j_cut)
            keys_scr[slot_a, kb] = jnp.where(drop, INT_MIN, k)
            return carry

        lax.fori_loop(0, n_kb, demote_body, 0)

    tau_scr[slot_a] = jnp.maximum(tau, INT_MIN + 1)

    l_min = jnp.min(jnp.concatenate([acc_scr[g, D_LATENT:D_LATENT + 1, :] for g in range(n_groups)], axis=1))

    @pl.when(has_c & jnp.logical_not(l_min >= 2.0 ** -80))
    def _():
        m_scr[...] = jnp.full(m_scr.shape, NEG_BIG, F32)
        acc_scr[...] = jnp.zeros(acc_scr.shape, F32)

        def exact_body(kb, carry):
            sel = keys_scr[slot_c, kb] >= tau_c
            ct_blk = ckvt_ref[kb]
            start = bias_start(kb)
            for g in range(n_groups):
                lt = logits(kb, g)
                ps, alphas = [], []
                for j in range(2):
                    hh = 2 * g + j
                    sl = slice(hh * LANES, (hh + 1) * LANES)
                    piece = lt[:, j * LANES:(j + 1) * LANES] + btile_ref[hh, pl.ds(start, kb_sz), :]
                    masked = jnp.where(sel, piece, NEG_BIG)
                    m_old = m_scr[:, sl]
                    m_new = jnp.maximum(m_old, jnp.max(masked, axis=0, keepdims=True))
                    m_scr[:, sl] = m_new
                    alphas.append(jnp.exp2(m_old - m_new))
                    ps.append(jnp.exp2(masked - m_new))
                pv = jnp.dot(ct_blk, jnp.concatenate(ps, axis=1).astype(BF16), preferred_element_type=F32)
                acc_scr[g] = jnp.concatenate(alphas, axis=1) * acc_scr[g] + pv
            return carry

        lax.fori_loop(0, n_c, exact_body, 0)

    @pl.when(has_c)
    def _():
        ys = []
        for hh in range(N_HEADS_A):
            acc_h = acc_scr[hh // 2, :, (hh % 2) * LANES:(hh % 2 + 1) * LANES]
            o_h = acc_h[:D_LATENT] * (1.0 / acc_h[D_LATENT:D_LATENT + 1])
            ys.append(jnp.dot(wuvt_ref[hh], o_h.astype(BF16), preferred_element_type=F32))
        y_t = jnp.concatenate(ys, axis=0)
        out_ref[...] = y_t.T.astype(BF16)


def _dsa(q_idx, q_abs, w_t, k_idx, ckv, ckv_t, btile, bmax, wuv_t, *, batch, seq):
    t = batch * seq
    nqb = seq // Q_BLOCK
    nkb = seq // KEY_BLOCK
    topk = min(TOPK_MAX, seq // 4)
    k_idx3 = k_idx.reshape(t // KEY_BLOCK, KEY_BLOCK, LANES)
    ckv3 = ckv.reshape(t // KEY_BLOCK, KEY_BLOCK, D_LATENT)
    per_batch = lambda shape: pl.BlockSpec(shape, lambda b, q: (b,) + (0,) * (len(shape) - 1),
                                           pipeline_mode=pl.Buffered(1))
    scored = lambda b, s: b * nqb + jnp.minimum(s, nqb - 1)
    attended = lambda b, s: b * nqb + jnp.maximum(s - 1, 0)
    return pl.pallas_call(
        functools.partial(_dsa_kernel, topk=topk, n_qb=nqb),
        out_shape=jax.ShapeDtypeStruct((t, W_A), BF16),
        grid=(batch, nqb + 1),
        in_specs=[pl.BlockSpec((1, N_HEADS_IDX, Q_BLOCK, LANES), lambda b, s: (scored(b, s), 0, 0, 0)),
                  pl.BlockSpec((1, N_HEADS_A, Q_BLOCK, D_LATENT), lambda b, s: (attended(b, s), 0, 0, 0)),
                  pl.BlockSpec((N_HEADS_IDX, Q_BLOCK), lambda b, s: (0, scored(b, s))),
                  per_batch((nkb, KEY_BLOCK, LANES)),
                  per_batch((nkb, KEY_BLOCK, D_LATENT)),
                  per_batch((nkb, CKVT_ROWS, KEY_BLOCK)),
                  _resident((N_HEADS_A, BIAS_ROWS, LANES)),
                  _resident((1, N_HEADS_A * LANES)),
                  _resident((N_HEADS_A, HEAD_DIM_A, D_LATENT))],
        out_specs=pl.BlockSpec((Q_BLOCK, W_A), lambda b, s: (attended(b, s), 0)),
        scratch_shapes=[pltpu.VMEM((2, nkb, KEY_BLOCK, LANES), I32),
                        pltpu.VMEM((32, seq // PLANE_KEYS, 8, LANES), I32),
                        pltpu.VMEM((seq // PLANE_KEYS, 8, LANES), I32),
                        pltpu.VMEM((2, 1, LANES), I32),
                        pltpu.VMEM((N_HEADS_A // 2, CKVT_ROWS, 2 * LANES), F32),
                        pltpu.VMEM((1, N_HEADS_A * LANES), F32),
                        pltpu.VMEM((KEY_BLOCK, N_HEADS_A * LANES), F32),
                        pltpu.VMEM((1, 1), F32),
                        pltpu.VMEM((N_HEADS_IDX // 2, LANES, 2 * LANES), BF16),
                        pltpu.VMEM((N_HEADS_A // 2, D_LATENT, 2 * LANES), BF16)],
        compiler_params=pltpu.CompilerParams(dimension_semantics=("arbitrary", "arbitrary"),
                                             vmem_limit_bytes=VMEM_LIMIT),
        name="dsa",
    )(q_idx, q_abs, w_t, k_idx3, ckv3, ckv_t, btile, bmax, wuv_t)


def _mlstm_kernel(qk_ref, v_ref, o_ref, ift_ref, gbt_ref, hn_ref,
                  out_ref, cx_scr, m_scr, *, chunk, n_batch):
    L = chunk

    @pl.when(pl.program_id(0) == 0)
    def _():
        cx_scr[...] = jnp.zeros(cx_scr.shape, F32)
        m_scr[...] = jnp.zeros(m_scr.shape, F32)

    rr = lax.broadcasted_iota(I32, (L, L), 0)
    cc = lax.broadcasted_iota(I32, (L, L), 1)
    causal = cc <= rr
    triu = jnp.where(rr <= cc, 1.0, 0.0).astype(BF16)
    lane = lax.broadcasted_iota(I32, (8, L), 1)
    ones_col = jnp.where(lax.broadcasted_iota(I32, (L, HEAD_DIM_M), 1) == 0, 1.0, 0.0).astype(BF16)
    for bi in range(n_batch):
        _mlstm_chunk(qk_ref.at[bi], v_ref.at[bi], o_ref.at[bi], ift_ref.at[bi], gbt_ref, hn_ref,
                     out_ref.at[bi], cx_scr.at[bi], m_scr.at[bi], causal, triu, lane, ones_col, L)


def _mlstm_chunk(qk_ref, v_ref, o_ref, ift_ref, gbt_ref, hn_ref, out_ref, cx_scr, m_scr,
                 causal, triu, lane, ones_col, L):
    g_t = ift_ref[...] + gbt_ref[...]
    b_all = sum(jnp.dot(piece, triu, preferred_element_type=F32) for piece in _split3(_log_sigmoid(g_t)))
    b8 = pltpu.roll(b_all, N_HEADS_M, axis=0)
    a8 = g_t - b8
    cm = a8
    shift = 1
    while shift < L:
        cm = jnp.maximum(cm, jnp.where(lane >= shift, pltpu.roll(cm, shift, axis=1), NEG_BIG))
        shift *= 2
    m_prev = m_scr[...]
    mx = jnp.maximum(m_prev, cm)
    mx_last = mx[:, L - 1:L]
    decay8 = jnp.exp(m_prev - mx_last)
    m_scr[...] = b8[:, L - 1:L] + mx_last
    rows = jnp.concatenate([-mx,
                            jnp.exp(m_prev - mx),
                            jnp.exp(-(b8 + mx)),
                            jnp.exp(a8 - mx_last),
                            jnp.zeros((LANES - 32, L), F32)], axis=0)
    cols = rows.T

    o_gate = _sigmoid(o_ref[...])
    for hh in range(N_HEADS_M):
        hs = slice(hh * HEAD_DIM_M, (hh + 1) * HEAD_DIM_M)
        qb16 = qk_ref[:, hs]
        kb16 = qk_ref[:, W_M + hh * HEAD_DIM_M:W_M + (hh + 1) * HEAD_DIM_M]
        v_ext = jnp.concatenate([v_ref[:, hs], ones_col], axis=1)
        u_c = cols[:, hh:hh + 1]
        w_inter = cols[:, 8 + hh:9 + hh]
        em_c = cols[:, 16 + hh:17 + hh]
        wgt_c = cols[:, 24 + hh:25 + hh]
        cx_prev = cx_scr[hh]

        d_mat = jnp.where(causal, jnp.exp(u_c + a8[hh:hh + 1, :]), 0.0)
        s = lax.dot_general(qb16, kb16, (((1,), (1,)), ((), ())), preferred_element_type=F32) * d_mat
        intra = jnp.dot(s.astype(BF16), v_ext, preferred_element_type=F32)
        inter = jnp.dot(qb16, cx_prev.astype(BF16), preferred_element_type=F32)
        both = w_inter * inter + intra
        num = both[:, :HEAD_DIM_M]
        den = both[:, HEAD_DIM_M:HEAD_DIM_M + 1]
        hval = num / jnp.maximum(jnp.abs(den), em_c)

        kw = kb16.astype(F32) * wgt_c
        cx_scr[hh] = decay8[hh:hh + 1] * cx_prev + jnp.dot(kw.T.astype(BF16), v_ext,
                                                           preferred_element_type=F32)

        mu = jnp.mean(hval, axis=1, keepdims=True)
        cen = hval - mu
        var = jnp.mean(cen * cen, axis=1, keepdims=True)
        hn = cen * lax.rsqrt(var + EPS) * hn_ref[:, hs]
        out_ref[:, hs] = (hn * o_gate[:, hs]).astype(BF16)


def _mlstm(qk, v, o_pre, ift, gate_bias, head_norm, *, batch, seq):
    t = batch * seq
    L = MLSTM_CHUNK
    nc = seq // L
    gbt = jnp.broadcast_to(gate_bias.reshape(2 * N_HEADS_M, 1), (2 * N_HEADS_M, L))
    ift_b = ift.reshape(2 * N_HEADS_M, batch, seq).transpose(1, 0, 2)
    row = lambda w: pl.BlockSpec((batch, L, w), lambda c: (0, c, 0))
    out = pl.pallas_call(
        functools.partial(_mlstm_kernel, chunk=L, n_batch=batch),
        out_shape=jax.ShapeDtypeStruct((batch, seq, W_M), BF16),
        grid=(nc,),
        in_specs=[row(2 * W_M), row(W_M), row(W_M),
                  pl.BlockSpec((batch, 2 * N_HEADS_M, L), lambda c: (0, 0, c)),
                  _resident((2 * N_HEADS_M, L)), _resident((1, W_M))],
        out_specs=row(W_M),
        scratch_shapes=[pltpu.VMEM((batch, N_HEADS_M, HEAD_DIM_M, 2 * HEAD_DIM_M), F32),
                        pltpu.VMEM((batch, 8, 1), F32)],
        compiler_params=pltpu.CompilerParams(dimension_semantics=("arbitrary",),
                                             vmem_limit_bytes=VMEM_LIMIT),
        name="mlstm",
    )(qk.reshape(batch, seq, 2 * W_M), v.reshape(batch, seq, W_M), o_pre.reshape(batch, seq, W_M),
      ift_b, gbt, head_norm.reshape(1, -1))
    return out.reshape(t, W_M)


def kernel(x, c, ada_w, ada_b, ffn1_norm, ffn1_w1, ffn1_w3, ffn1_w2, mix_norm, w_in, conv_w, conv_b,
           kv_norm, w_uk, w_uv, mlstm_gate_bias, mlstm_head_norm, rel_bias, w_branch_attn,
           w_branch_mlstm, w_out, ffn2_norm, ffn2_w1, ffn2_w3, ffn2_w2, final_norm):
    batch, seq, d = x.shape
    depth = ada_w.shape[0]
    assert seq % max(FFN_TM, MIX_TM, MLSTM_CHUNK, KEY_BLOCK) == 0
    t = batch * seq
    xf = x.reshape(t, d)
    btile, bmax = _bias_tiles(rel_bias)
    for l in range(depth):
        mod = _adaln(c, ada_w[l], ada_b[l]).reshape(batch, 9, 1, d)
        sh1, sc1, g1, sh2, sc2, g2, sh3, sc3, g3 = [mod[:, n] for n in range(9)]
        xf = _ffn(xf, ffn1_norm[l], sh1, sc1, g1, ffn1_w1[l], ffn1_w3[l], ffn1_w2[l], final_norm,
                  seq=seq, final_norm=False)
        wuk_hdc = w_uk[l].transpose(0, 2, 1).reshape(N_HEADS_A // 2, 2, HEAD_DIM_A, D_LATENT)
        zeros = jnp.zeros_like(wuk_hdc[:, 0])
        wuk_t = jnp.concatenate([jnp.concatenate([wuk_hdc[:, 0], zeros], axis=2),
                                 jnp.concatenate([zeros, wuk_hdc[:, 1]], axis=2)], axis=1).astype(BF16)
        (q_abs, q_idx, k_idx, ckv, ckv_t, w_t, ift, qk_m, v_m, o_pre, gate_a, gate_m) = _mixin(
            xf, mix_norm[l], sh2, sc2, _pack_w_in(w_in[l], d), kv_norm[l], wuk_t, conv_w[l], conv_b[l],
            seq=seq)
        wuv_t = w_uv[l].transpose(0, 2, 1).astype(BF16)
        y_a = _dsa(q_idx, q_abs, w_t, k_idx, ckv, ckv_t, btile, bmax, wuv_t, batch=batch, seq=seq)
        h_m = _mlstm(qk_m, v_m, o_pre, ift, mlstm_gate_bias[l], mlstm_head_norm[l], batch=batch, seq=seq)
        xf = _ffn(xf, ffn2_norm[l], sh3, sc3, g3, ffn2_w1[l], ffn2_w3[l], ffn2_w2[l], final_norm,
                  seq=seq, final_norm=(l == depth - 1),
                  merge=(y_a, h_m, gate_a, gate_m, g2, w_branch_attn[l], w_branch_mlstm[l], w_out[l]))
    return xf.reshape(batch, seq, d)
```

```python
import functools
import math

import jax
import jax.numpy as jnp
from jax import lax
from jax.experimental import pallas as pl
from jax.experimental.pallas import tpu as pltpu

F32 = jnp.float32
BF16 = jnp.bfloat16
I32 = jnp.int32

LANES = 128
VMEM_LIMIT = 56 * 1024 * 1024

N_HEADS_A = 8
HEAD_DIM_A = 64
D_LATENT = 256
N_HEADS_IDX = 8
HEAD_DIM_IDX = 64
TOPK_MAX = 256
Q_BLOCK = 128
N_BUCKETS = 32
MAX_DISTANCE = 128
N_HEADS_M = 4
HEAD_DIM_M = 128
CONV_WIDTH = 4
EPS = 1e-6
IDX_SCALE = (N_HEADS_IDX ** -0.5) * (HEAD_DIM_IDX ** -0.5)
W_A = N_HEADS_A * HEAD_DIM_A
W_M = N_HEADS_M * HEAD_DIM_M

FFN_TM = 512
FFN_CHUNK = 256
MIX_TM = 512
KEY_BLOCK = 256
BLOCK_UNROLL = 2
PLANE_KEYS = 256
MLSTM_CHUNK = 256
NEG_BIG = -1e30
INT_MIN = -2 ** 31

BIAS_PAD = 2 * KEY_BLOCK - Q_BLOCK
BIAS_ROWS = KEY_BLOCK + BIAS_PAD
CKVT_ROWS = D_LATENT + 16
LOG2E = math.log2(math.e)


def _sigmoid(x):
    return 1.0 / (1.0 + jnp.exp(-x))


def _log_sigmoid(x):
    return jnp.minimum(x, 0.0) - jnp.log(1.0 + jnp.exp(-jnp.abs(x)))


def _rms_norm(x, gain):
    ms = jnp.mean(x * x, axis=-1, keepdims=True)
    return x * lax.rsqrt(ms + EPS) * gain


def _split3(x):
    hi = x.astype(BF16)
    r1 = x - hi.astype(F32)
    mid = r1.astype(BF16)
    lo = (r1 - mid.astype(F32)).astype(BF16)
    return hi, mid, lo


def _resident(shape):
    nd = len(shape)
    return pl.BlockSpec(shape, lambda *_: (0,) * nd, pipeline_mode=pl.Buffered(1))


def _adaln_kernel(c_ref, w_ref, b_ref, o_ref):
    c = c_ref[...]
    cond = c * _sigmoid(c)
    o_ref[...] = jnp.dot(cond.astype(BF16), w_ref[...].astype(BF16),
                         preferred_element_type=F32) + b_ref[...]


def _adaln(c, ada_w, ada_b):
    b, d = c.shape
    n = ada_w.shape[1]
    rows = 8
    c_pad = jnp.zeros((rows, d), F32).at[:b].set(c)
    tn = 1024
    out = pl.pallas_call(
        _adaln_kernel,
        out_shape=jax.ShapeDtypeStruct((rows, n), F32),
        grid=(n // tn,),
        in_specs=[pl.BlockSpec((rows, d), lambda j: (0, 0)),
                  pl.BlockSpec((d, tn), lambda j: (0, j)),
                  pl.BlockSpec((1, tn), lambda j: (0, j))],
        out_specs=pl.BlockSpec((rows, tn), lambda j: (0, j)),
        compiler_params=pltpu.CompilerParams(dimension_semantics=("arbitrary",),
                                             vmem_limit_bytes=VMEM_LIMIT),
        name="adaln",
    )(c_pad, ada_w, ada_b.reshape(1, n))
    return out[:b]


def _t5_bucket(dist):
    n = jnp.maximum(dist, 0)
    max_exact = N_BUCKETS // 2
    nf = jnp.maximum(n, 1).astype(F32)
    large = max_exact + (jnp.log(nf / max_exact) / math.log(MAX_DISTANCE / max_exact)
                         * (N_BUCKETS - max_exact)).astype(I32)
    large = jnp.minimum(large, N_BUCKETS - 1)
    return jnp.where(n < max_exact, n, large)


def _bias_kernel(rel_ref, tile_ref, max_ref):
    r = lax.broadcasted_iota(I32, (BIAS_ROWS, LANES), 0)
    i = lax.broadcasted_iota(I32, (BIAS_ROWS, LANES), 1)
    bucket = _t5_bucket(i - r + BIAS_PAD)
    for h in range(N_HEADS_A):
        acc = jnp.zeros((BIAS_ROWS, LANES), F32)
        top = rel_ref[0, h] * LOG2E
        for bkt in range(N_BUCKETS):
            val = rel_ref[bkt, h] * LOG2E
            acc = jnp.where(bucket == bkt, val, acc)
            top = jnp.maximum(top, val)
        tile_ref[h] = acc
        max_ref[:, h * LANES:(h + 1) * LANES] = jnp.full((1, LANES), top, F32)


def _bias_tiles(rel_bias):
    return pl.pallas_call(
        _bias_kernel,
        out_shape=(jax.ShapeDtypeStruct((N_HEADS_A, BIAS_ROWS, LANES), F32),
                   jax.ShapeDtypeStruct((1, N_HEADS_A * LANES), F32)),
        in_specs=[pl.BlockSpec(memory_space=pltpu.SMEM)],
        out_specs=(pl.BlockSpec(memory_space=pltpu.VMEM), pl.BlockSpec(memory_space=pltpu.VMEM)),
        name="bias_tiles",
    )(rel_bias)


def _ffn_kernel(*refs, n_chunks, final_norm, merge):
    if merge:
        (x_ref, ya_ref, hm_ref, ga_ref, gm_ref, gmix_ref, wa_ref, wm_ref, wo_ref), refs = refs[:9], refs[9:]
    else:
        x_ref, refs = refs[0], refs[1:]
    gain_ref, sh_ref, sc_ref, g_ref, w1_ref, w3_ref, w2_ref, fin_ref, o_ref, h_scr, acc_scr = refs
    x = x_ref[...]
    if merge:
        pa = jnp.dot(ya_ref[...], wa_ref[...], preferred_element_type=F32)
        pm = jnp.dot(hm_ref[...], wm_ref[...], preferred_element_type=F32)
        merged = _sigmoid(ga_ref[...]) * pa + _sigmoid(gm_ref[...]) * pm
        x = x + gmix_ref[0] * jnp.dot(merged.astype(BF16), wo_ref[...], preferred_element_type=F32)
    h = _rms_norm(x, gain_ref[...]) * (1.0 + sc_ref[0]) + sh_ref[0]
    h_scr[...] = h.astype(BF16)
    for j in range(n_chunks):
        hb = h_scr[...]
        cols = slice(j * FFN_CHUNK, (j + 1) * FFN_CHUNK)
        u1 = jnp.dot(hb, w1_ref[:, cols], preferred_element_type=F32)
        u3 = jnp.dot(hb, w3_ref[:, cols], preferred_element_type=F32)
        a = (u1 * _sigmoid(u1)) * u3
        part = jnp.dot(a.astype(BF16), w2_ref[j], preferred_element_type=F32)
        if j == 0:
            acc_scr[...] = part
        else:
            acc_scr[...] += part
    out = x + (0.5 * g_ref[0]) * acc_scr[...]
    if final_norm:
        out = _rms_norm(out, fin_ref[...])
    o_ref[...] = out


def _ffn(x, gain, sh, sc, g, w1, w3, w2, fin, *, seq, final_norm, merge=None):
    t, d = x.shape
    dff = w1.shape[1]
    nch = dff // FFN_CHUNK
    w1c = w1.astype(BF16)
    w3c = w3.astype(BF16)
    w2c = w2.astype(BF16).reshape(nch, FFN_CHUNK, d)
    tm = FFN_TM
    per_b = seq // tm
    row = lambda w: pl.BlockSpec((tm, w), lambda i: (i, 0))
    mod_spec = pl.BlockSpec((1, 1, d), lambda i: (i // per_b, 0, 0))
    merge_specs, merge_args = [], []
    if merge is not None:
        y_a, h_m, gate_a, gate_m, g_mix, w_a, w_m, w_o = merge
        merge_specs = [row(W_A), row(W_M), row(d), row(d), mod_spec,
                       _resident((W_A, d)), _resident((W_M, d)), _resident((d, d))]
        merge_args = [y_a, h_m, gate_a, gate_m, g_mix, w_a.astype(BF16), w_m.astype(BF16), w_o.astype(BF16)]
    return pl.pallas_call(
        functools.partial(_ffn_kernel, n_chunks=nch, final_norm=final_norm, merge=merge is not None),
        out_shape=jax.ShapeDtypeStruct((t, d), F32),
        grid=(t // tm,),
        in_specs=[row(d)] + merge_specs + [
                  _resident((1, d)), mod_spec, mod_spec, mod_spec,
                  _resident((d, dff)), _resident((d, dff)),
                  _resident((nch, FFN_CHUNK, d)), _resident((1, d))],
        out_specs=row(d),
        scratch_shapes=[pltpu.VMEM((tm, d), BF16), pltpu.VMEM((tm, d), F32)],
        compiler_params=pltpu.CompilerParams(dimension_semantics=("arbitrary",),
                                             vmem_limit_bytes=VMEM_LIMIT),
        name="ffn_final" if final_norm else "ffn",
    )(x, *merge_args, gain.reshape(1, d), sh, sc, g, w1c, w3c, w2c, fin.reshape(1, d))


_C_QA = 0
_C_CKV = _C_QA + W_A
_C_QI = _C_CKV + D_LATENT
_C_KI = _C_QI + N_HEADS_IDX * LANES
_C_SM = _C_KI + LANES
_C_QK = _C_SM + LANES
_C_V = _C_QK + 2 * W_M
_C_O = _C_V + W_M
_C_GA = _C_O + W_M
_C_GM = _C_GA + 1024
_C_END = _C_GM + 1024
_SM_W = 0
_SM_I = N_HEADS_IDX


def _pack_w_in(w_in, d_model):
    splits = (W_A, D_LATENT, N_HEADS_IDX * HEAD_DIM_IDX, HEAD_DIM_IDX, N_HEADS_IDX,
              W_M, W_M, W_M, N_HEADS_M, N_HEADS_M, W_M, d_model, d_model)
    offs = [0]
    for s in splits:
        offs.append(offs[-1] + s)
    (q_a, c_kv, q_i, k_i, w_i, q_m, k_m, v_m, i_p, f_p, o_p, g_a, g_m) = [
        w_in[:, offs[n]:offs[n + 1]] for n in range(len(splits))]
    d = w_in.shape[0]

    def pad_heads(w, nh, hd):
        w = w.reshape(d, nh, hd)
        return jnp.pad(w, ((0, 0), (0, 0), (0, LANES - hd))).reshape(d, nh * LANES)

    small = jnp.concatenate([w_i, i_p, f_p], axis=1)
    small = jnp.pad(small, ((0, 0), (0, LANES - small.shape[1])))
    packed = jnp.concatenate([
        q_a, c_kv, pad_heads(q_i, N_HEADS_IDX, HEAD_DIM_IDX),
        jnp.pad(k_i, ((0, 0), (0, LANES - HEAD_DIM_IDX))), small, q_m, k_m, v_m, o_p, g_a, g_m], axis=1)
    assert packed.shape[1] == _C_END
    return packed.astype(BF16)


def _mixin_kernel(x_ref, gain_ref, sh_ref, sc_ref, w_ref, kvn_ref, wuk_ref, cw_ref, cb_ref,
                  qabs_ref, qidx_ref, kidx_ref, ckv_ref, ckvt_ref, wt_ref, ift_ref,
                  qk_ref, v_ref, o_ref, ga_ref, gm_ref, h_scr, xe_scr, *, tm, tiles_per_seq):
    nqb = tm // Q_BLOCK

    @pl.when(pl.program_id(0) % tiles_per_seq == 0)
    def _():
        xe_scr[:8] = jnp.zeros((8, xe_scr.shape[1]), F32)

    x = x_ref[...]
    h = _rms_norm(x, gain_ref[...]) * (1.0 + sc_ref[0]) + sh_ref[0]
    h_scr[...] = h.astype(BF16)

    def proj(lo, hi):
        return jnp.dot(h_scr[...], w_ref[:, lo:hi], preferred_element_type=F32)

    qa = proj(_C_QA, _C_CKV)
    scale = HEAD_DIM_A ** -0.5 * LOG2E
    for g in range(N_HEADS_A // 2):
        q_pair = qa[:, g * LANES:(g + 1) * LANES].astype(BF16)
        q_abs = jnp.dot(q_pair, wuk_ref[g], preferred_element_type=F32) * scale
        for j in range(2):
            qabs_ref[:, 2 * g + j] = (q_abs[:, j * D_LATENT:(j + 1) * D_LATENT]
                                      .astype(BF16).reshape(nqb, Q_BLOCK, D_LATENT))
    ckv = _rms_norm(proj(_C_CKV, _C_QI), kvn_ref[...])
    ckv_ref[...] = ckv.astype(BF16)
    ckv_t = ckv.T
    ones_row = jnp.where(lax.broadcasted_iota(I32, (CKVT_ROWS - D_LATENT, KEY_BLOCK), 0) == 0, 1.0, 0.0)
    for j in range(tm // KEY_BLOCK):
        ckvt_ref[j, :D_LATENT] = ckv_t[:, j * KEY_BLOCK:(j + 1) * KEY_BLOCK].astype(BF16)
        ckvt_ref[j, D_LATENT:] = ones_row.astype(BF16)
    qi = proj(_C_QI, _C_KI)
    for hh in range(N_HEADS_IDX):
        qidx_ref[:, hh] = qi[:, hh * LANES:(hh + 1) * LANES].astype(BF16).reshape(nqb, Q_BLOCK, LANES)
    kidx_ref[...] = proj(_C_KI, _C_SM).astype(BF16)
    small_t = proj(_C_SM, _C_QK).T
    wt_ref[...] = small_t[_SM_W:_SM_W + N_HEADS_IDX] * IDX_SCALE
    ift_ref[...] = small_t[_SM_I:_SM_I + 2 * N_HEADS_M]
    xe_scr[8:] = proj(_C_QK, _C_V)
    xe = xe_scr[...]
    xq = xe[8:]
    conv = xq * cw_ref[CONV_WIDTH - 1:CONV_WIDTH, :] + cb_ref[...]
    for d in range(1, CONV_WIDTH):
        conv = conv + pltpu.roll(xe, d, axis=0)[8:] * cw_ref[CONV_WIDTH - 1 - d:CONV_WIDTH - d, :]
    xe_scr[:8] = xe_scr[tm:]
    qk = conv * _sigmoid(conv)
    qk_ref[:, :W_M] = qk[:, :W_M].astype(BF16)
    qk_ref[:, W_M:] = (qk[:, W_M:] * (HEAD_DIM_M ** -0.5)).astype(BF16)
    v_ref[...] = proj(_C_V, _C_O).astype(BF16)
    o_ref[...] = proj(_C_O, _C_GA)
    ga_ref[...] = proj(_C_GA, _C_GM)
    gm_ref[...] = proj(_C_GM, _C_END)


def _mixin(x, gain, sh, sc, w_packed, kv_norm, wuk_t, conv_w, conv_b, *, seq):
    t, d = x.shape
    tm = MIX_TM
    per_b = seq // tm
    nqb = tm // Q_BLOCK
    row = lambda w: pl.BlockSpec((tm, w), lambda i: (i, 0))
    mod_spec = pl.BlockSpec((1, 1, d), lambda i: (i // per_b, 0, 0))
    out_shape = (
        jax.ShapeDtypeStruct((t // Q_BLOCK, N_HEADS_A, Q_BLOCK, D_LATENT), BF16),
        jax.ShapeDtypeStruct((t // Q_BLOCK, N_HEADS_IDX, Q_BLOCK, LANES), BF16),
        jax.ShapeDtypeStruct((t, LANES), BF16),
        jax.ShapeDtypeStruct((t, D_LATENT), BF16),
        jax.ShapeDtypeStruct((t // KEY_BLOCK, CKVT_ROWS, KEY_BLOCK), BF16),
        jax.ShapeDtypeStruct((N_HEADS_IDX, t), F32),
        jax.ShapeDtypeStruct((2 * N_HEADS_M, t), F32),
        jax.ShapeDtypeStruct((t, 2 * W_M), BF16),
        jax.ShapeDtypeStruct((t, W_M), BF16),
        jax.ShapeDtypeStruct((t, W_M), F32),
        jax.ShapeDtypeStruct((t, d), F32),
        jax.ShapeDtypeStruct((t, d), F32),
    )
    out_specs = (
        pl.BlockSpec((nqb, N_HEADS_A, Q_BLOCK, D_LATENT), lambda i: (i, 0, 0, 0)),
        pl.BlockSpec((nqb, N_HEADS_IDX, Q_BLOCK, LANES), lambda i: (i, 0, 0, 0)),
        row(LANES), row(D_LATENT),
        pl.BlockSpec((tm // KEY_BLOCK, CKVT_ROWS, KEY_BLOCK), lambda i: (i, 0, 0)),
        pl.BlockSpec((N_HEADS_IDX, tm), lambda i: (0, i)),
        pl.BlockSpec((2 * N_HEADS_M, tm), lambda i: (0, i)),
        row(2 * W_M), row(W_M), row(W_M), row(d), row(d),
    )
    return pl.pallas_call(
        functools.partial(_mixin_kernel, tm=tm, tiles_per_seq=per_b),
        out_shape=out_shape,
        grid=(t // tm,),
        in_specs=[pl.BlockSpec((tm, d), lambda i: (i, 0)), _resident((1, d)), mod_spec, mod_spec,
                  _resident((d, _C_END)), _resident((1, D_LATENT)),
                  _resident((N_HEADS_A // 2, LANES, 2 * D_LATENT)),
                  _resident((CONV_WIDTH, 2 * W_M)), _resident((1, 2 * W_M))],
        out_specs=out_specs,
        scratch_shapes=[pltpu.VMEM((tm, d), BF16), pltpu.VMEM((tm + 8, 2 * W_M), F32)],
        compiler_params=pltpu.CompilerParams(dimension_semantics=("arbitrary",),
                                             vmem_limit_bytes=VMEM_LIMIT),
        name="mixin",
    )(x, gain.reshape(1, d), sh, sc, w_packed, kv_norm.reshape(1, D_LATENT), wuk_t,
      conv_w, conv_b.reshape(1, -1))


def _sortable_key(score):
    bits = pltpu.bitcast(score, I32)
    bits = jnp.where(bits == INT_MIN, 0, bits)
    return jnp.where(bits < 0, bits ^ 0x7FFFFFFF, bits)


def _bit_transpose32(words):
    v = list(words)
    j, m = 16, 0x0000FFFF
    while j:
        k = 0
        while k < 32:
            t = (v[k] ^ lax.shift_right_logical(v[k + j], jnp.int32(j))) & m
            v[k] = v[k] ^ t
            v[k + j] = v[k + j] ^ (t << j)
            k = (k + j + 1) & ~j
        j >>= 1
        m = (m ^ (m << j)) & 0x7FFFFFFF
    return v


def _dsa_kernel(qidx_ref, qabs_ref, wt_ref, kidx_ref, ckv_ref, ckvt_ref, btile_ref, bmax_ref, wuvt_ref,
                out_ref, keys_scr, planes_scr, cand_scr, tau_scr, acc_scr, m_scr, ltc_scr, kmax_scr,
                qta_scr, qtc_scr, *, topk, n_qb):
    kb_sz = KEY_BLOCK
    step = pl.program_id(1)
    has_c = step >= 1
    qa = jnp.minimum(step, n_qb - 1)
    qc = jnp.maximum(step - 1, 0)
    slot_a = step & 1
    slot_c = 1 - slot_a
    n_a = qa // (kb_sz // Q_BLOCK) + 1
    n_c = qc // (kb_sz // Q_BLOCK) + 1
    qa0 = qa * Q_BLOCK
    qc0 = qc * Q_BLOCK
    row_id = lax.broadcasted_iota(I32, (kb_sz, LANES), 0)
    lane_id = lax.broadcasted_iota(I32, (kb_sz, LANES), 1)
    n_blocks = keys_scr.shape[1]
    n_groups = N_HEADS_A // 2
    pair = 2 * LANES
    ones8 = jnp.ones((8, D_LATENT), BF16)

    @pl.when(step == 0)
    def _():
        keys_scr[1, 0] = jnp.full((kb_sz, LANES), INT_MIN, I32)
        tau_scr[1] = jnp.zeros((1, LANES), I32)
        planes_scr[...] = jnp.zeros(planes_scr.shape, I32)

        def kn_body(kb, mx):
            c = ckv_ref[kb].astype(F32)
            n2 = lax.dot_general(ones8, (c * c).astype(BF16), (((1,), (1,)), ((), ())),
                                 preferred_element_type=F32)
            return jnp.maximum(mx, n2[0:1])
        mx = lax.fori_loop(0, n_blocks, kn_body, jnp.zeros((1, kb_sz), F32))
        kmax_scr[...] = jnp.max(mx, axis=1, keepdims=True)

    for g in range(n_groups):
        qi_g = qidx_ref[0, 2 * g:2 * g + 2].reshape(2 * Q_BLOCK, LANES).astype(F32)
        qta_scr[g] = qi_g.T.astype(BF16)
        qa_g = qabs_ref[0, 2 * g:2 * g + 2].reshape(2 * Q_BLOCK, D_LATENT).astype(F32)
        qtc_scr[g] = qa_g.T.astype(BF16)

    def idx_dot(kb, g):
        return jnp.dot(kidx_ref[kb], qta_scr[g], preferred_element_type=F32)

    def logits(kb, g):
        return jnp.dot(ckv_ref[kb], qtc_scr[g], preferred_element_type=F32)

    def bias_start(kb):
        delta = jnp.minimum(qc0 - kb * kb_sz, BIAS_PAD)
        return pl.multiple_of(BIAS_PAD - delta, LANES)

    qn2 = []
    for g in range(n_groups):
        q_g = qabs_ref[0, 2 * g:2 * g + 2].reshape(2 * Q_BLOCK, D_LATENT).astype(F32)
        qn2.append(lax.dot_general(ones8, (q_g * q_g).astype(BF16), (((1,), (1,)), ((), ())),
                                   preferred_element_type=F32)[0:1])
    bound = jnp.sqrt(jnp.concatenate(qn2, axis=1) * kmax_scr[...]) * 1.02 + bmax_ref[...] + 1e-3
    tau_c = tau_scr[slot_c]
    w_t = wt_ref[...]

    acc_scr[...] = jnp.zeros(acc_scr.shape, F32)
    for g in range(n_groups):
        ltc_scr[:, g * pair:(g + 1) * pair] = logits(0, g)

    bound_far = bound - jnp.concatenate([btile_ref[hh, 0:1, :] for hh in range(N_HEADS_A)], axis=1)

    def block_step(kb_raw, far):
        kb = jnp.minimum(kb_raw, n_a - 1)
        kc = jnp.minimum(kb_raw, n_c - 1)
        kc_next = jnp.minimum(kb_raw + 1, n_c - 1)
        thr = jnp.where(has_c & (kb_raw < n_c), tau_c - 1, jnp.int32(2 ** 31 - 1))
        sel = keys_scr[slot_c, kc] > thr
        ct_blk = ckvt_ref[kc]
        start = bias_start(kc)
        ref_pt = bound_far if far else bound
        score = jnp.zeros((kb_sz, LANES), F32)
        for g in range(n_groups):
            s_t = idx_dot(kb, g)
            for j in range(2):
                hh = 2 * g + j
                score = score + jnp.maximum(s_t[:, j * LANES:(j + 1) * LANES], 0.0) * w_t[hh:hh + 1, :]
        for g in range(n_groups):
            lt = ltc_scr[:, g * pair:(g + 1) * pair]
            ps = []
            for j in range(2):
                hh = 2 * g + j
                piece = lt[:, j * LANES:(j + 1) * LANES]
                if not far:
                    piece = piece + btile_ref[hh, pl.ds(start, kb_sz), :]
                ps.append(jnp.exp2(jnp.where(sel, piece, NEG_BIG) - ref_pt[:, hh * LANES:(hh + 1) * LANES]))
            ltc_scr[:, g * pair:(g + 1) * pair] = logits(kc_next, g)
            acc_scr[g] += jnp.dot(ct_blk, jnp.concatenate(ps, axis=1).astype(BF16),
                                  preferred_element_type=F32)
        valid = (kb * kb_sz + row_id) <= (qa0 + lane_id)
        keys = jnp.where(valid, _sortable_key(score), INT_MIN)
        keys_scr[slot_a, kb] = keys
        v = keys ^ INT_MIN
        for sub in range(kb_sz // PLANE_KEYS):
            r0 = sub * PLANE_KEYS
            words = _bit_transpose32([v[r0 + 8 * i:r0 + 8 * (i + 1), :] for i in range(32)])
            for bit in range(32):
                planes_scr[bit, kb * (kb_sz // PLANE_KEYS) + sub] = words[31 - bit]

    def block_body(far, it, carry):
        for u in range(BLOCK_UNROLL):
            block_step(it * BLOCK_UNROLL + u, far)
        return carry

    n_far_iters = jnp.maximum(n_c - 2, 0) // BLOCK_UNROLL
    lax.fori_loop(0, n_far_iters, functools.partial(block_body, True), 0)
    lax.fori_loop(n_far_iters, (n_a + BLOCK_UNROLL - 1) // BLOCK_UNROLL,
                  functools.partial(block_body, False), 0)
    n_kb = n_a


    n_planes = planes_scr.shape[1]
    live = n_kb * (kb_sz // PLANE_KEYS)

    def radix_select(width):
        blk_id = lax.broadcasted_iota(I32, (width, 8, LANES), 0)
        cand_scr[:width] = jnp.where(blk_id < live, -1, 0)

        def bit_body(it, carry):
            above, tau_u = carry
            bit = 31 - it
            ones = cand_scr[:width] & planes_scr[bit, :width]
            c1 = jnp.sum(jnp.sum(lax.population_count(ones), axis=0), axis=0, keepdims=True)
            take = (above + c1) >= topk
            cand_scr[:width] = jnp.where(take, ones, cand_scr[:width] ^ ones)
            above = jnp.where(take, above, above + c1)
            tau_u = jnp.where(take, tau_u | (jnp.int32(1) << bit), tau_u)
            return above, tau_u

        zero = jnp.zeros((1, LANES), I32)
        above, tau_u = lax.fori_loop(0, 32, bit_body, (zero, zero))
        n_eq = jnp.sum(jnp.sum(lax.population_count(cand_scr[:width]), axis=0), axis=0, keepdims=True)
        return above, tau_u, n_eq

    widths = [n_planes * (i + 1) // 4 for i in range(4)]
    select = functools.partial(radix_select, widths[-1])
    for width in reversed(widths[:-1]):
        select = functools.partial(lax.cond, live <= width, functools.partial(radix_select, width), select)
    n_gt, tau_u, n_eq = select()
    tau = tau_u ^ INT_MIN

    need = topk - n_gt
    overflow = n_eq > need
    seq_bits = max(1, (n_blocks * kb_sz - 1).bit_length())

    @pl.when(jnp.max(jnp.where(overflow, 1, 0)) > 0)
    def _():
        def count_ties_before(trial):
            def body(kb, acc):
                hit = jnp.where((keys_scr[slot_a, kb] == tau) & ((kb * kb_sz + row_id) < trial), 1, 0)
                return acc + jnp.sum(hit.reshape(kb_sz // 8, 8, LANES), axis=0)
            acc = lax.fori_loop(0, n_kb, body, jnp.zeros((8, LANES), I32))
            return jnp.sum(acc, axis=0, keepdims=True)

        def idx_body(it, jc):
            trial = jc | (jnp.int32(1) << (seq_bits - 1 - it))
            return jnp.where(count_ties_before(trial) < need, trial, jc)

        j_cut = lax.fori_loop(0, seq_bits, idx_body, jnp.zeros((1, LANES), I32))

        def demote_body(kb, carry):
            k = keys_scr[slot_a, kb]
            drop = overflow & (k == tau) & ((kb * kb_sz + row_id) > j_cut)
            keys_scr[slot_a, kb] = jnp.where(drop, INT_MIN, k)
            return carry

        lax.fori_loop(0, n_kb, demote_body, 0)

    tau_scr[slot_a] = jnp.maximum(tau, INT_MIN + 1)

    l_min = jnp.min(jnp.concatenate([acc_scr[g, D_LATENT:D_LATENT + 1, :] for g in range(n_groups)], axis=1))

    @pl.when(has_c & jnp.logical_not(l_min >= 2.0 ** -80))
    def _():
        m_scr[...] = jnp.full(m_scr.shape, NEG_BIG, F32)
        acc_scr[...] = jnp.zeros(acc_scr.shape, F32)

        def exact_body(kb, carry):
            sel = keys_scr[slot_c, kb] >= tau_c
            ct_blk = ckvt_ref[kb]
            start = bias_start(kb)
            for g in range(n_groups):
                lt = logits(kb, g)
                ps, alphas = [], []
                for j in range(2):
                    hh = 2 * g + j
                    sl = slice(hh * LANES, (hh + 1) * LANES)
                    piece = lt[:, j * LANES:(j + 1) * LANES] + btile_ref[hh, pl.ds(start, kb_sz), :]
                    masked = jnp.where(sel, piece, NEG_BIG)
                    m_old = m_scr[:, sl]
                    m_new = jnp.maximum(m_old, jnp.max(masked, axis=0, keepdims=True))
                    m_scr[:, sl] = m_new
                    alphas.append(jnp.exp2(m_old - m_new))
                    ps.append(jnp.exp2(masked - m_new))
                pv = jnp.dot(ct_blk, jnp.concatenate(ps, axis=1).astype(BF16), preferred_element_type=F32)
                acc_scr[g] = jnp.concatenate(alphas, axis=1) * acc_scr[g] + pv
            return carry

        lax.fori_loop(0, n_c, exact_body, 0)

    @pl.when(has_c)
    def _():
        ys = []
        for hh in range(N_HEADS_A):
            acc_h = acc_scr[hh // 2, :, (hh % 2) * LANES:(hh % 2 + 1) * LANES]
            o_h = acc_h[:D_LATENT] * (1.0 / acc_h[D_LATENT:D_LATENT + 1])
            ys.append(jnp.dot(wuvt_ref[hh], o_h.astype(BF16), preferred_element_type=F32))
        y_t = jnp.concatenate(ys, axis=0)
        out_ref[...] = y_t.T.astype(BF16)


def _dsa(q_idx, q_abs, w_t, k_idx, ckv, ckv_t, btile, bmax, wuv_t, *, batch, seq):
    t = batch * seq
    nqb = seq // Q_BLOCK
    nkb = seq // KEY_BLOCK
    topk = min(TOPK_MAX, seq // 4)
    k_idx3 = k_idx.reshape(t // KEY_BLOCK, KEY_BLOCK, LANES)
    ckv3 = ckv.reshape(t // KEY_BLOCK, KEY_BLOCK, D_LATENT)
    per_batch = lambda shape: pl.BlockSpec(shape, lambda b, q: (b,) + (0,) * (len(shape) - 1),
                                           pipeline_mode=pl.Buffered(1))
    scored = lambda b, s: b * nqb + jnp.minimum(s, nqb - 1)
    attended = lambda b, s: b * nqb + jnp.maximum(s - 1, 0)
    return pl.pallas_call(
        functools.partial(_dsa_kernel, topk=topk, n_qb=nqb),
        out_shape=jax.ShapeDtypeStruct((t, W_A), BF16),
        grid=(batch, nqb + 1),
        in_specs=[pl.BlockSpec((1, N_HEADS_IDX, Q_BLOCK, LANES), lambda b, s: (scored(b, s), 0, 0, 0)),
                  pl.BlockSpec((1, N_HEADS_A, Q_BLOCK, D_LATENT), lambda b, s: (attended(b, s), 0, 0, 0)),
                  pl.BlockSpec((N_HEADS_IDX, Q_BLOCK), lambda b, s: (0, scored(b, s))),
                  per_batch((nkb, KEY_BLOCK, LANES)),
                  per_batch((nkb, KEY_BLOCK, D_LATENT)),
                  per_batch((nkb, CKVT_ROWS, KEY_BLOCK)),
                  _resident((N_HEADS_A, BIAS_ROWS, LANES)),
                  _resident((1, N_HEADS_A * LANES)),
                  _resident((N_HEADS_A, HEAD_DIM_A, D_LATENT))],
        out_specs=pl.BlockSpec((Q_BLOCK, W_A), lambda b, s: (attended(b, s), 0)),
        scratch_shapes=[pltpu.VMEM((2, nkb, KEY_BLOCK, LANES), I32),
                        pltpu.VMEM((32, seq // PLANE_KEYS, 8, LANES), I32),
                        pltpu.VMEM((seq // PLANE_KEYS, 8, LANES), I32),
                        pltpu.VMEM((2, 1, LANES), I32),
                        pltpu.VMEM((N_HEADS_A // 2, CKVT_ROWS, 2 * LANES), F32),
                        pltpu.VMEM((1, N_HEADS_A * LANES), F32),
                        pltpu.VMEM((KEY_BLOCK, N_HEADS_A * LANES), F32),
                        pltpu.VMEM((1, 1), F32),
                        pltpu.VMEM((N_HEADS_IDX // 2, LANES, 2 * LANES), BF16),
                        pltpu.VMEM((N_HEADS_A // 2, D_LATENT, 2 * LANES), BF16)],
        compiler_params=pltpu.CompilerParams(dimension_semantics=("arbitrary", "arbitrary"),
                                             vmem_limit_bytes=VMEM_LIMIT),
        name="dsa",
    )(q_idx, q_abs, w_t, k_idx3, ckv3, ckv_t, btile, bmax, wuv_t)


def _mlstm_kernel(qk_ref, v_ref, o_ref, ift_ref, gbt_ref, hn_ref,
                  out_ref, cx_scr, m_scr, *, chunk, n_batch):
    L = chunk

    @pl.when(pl.program_id(0) == 0)
    def _():
        cx_scr[...] = jnp.zeros(cx_scr.shape, F32)
        m_scr[...] = jnp.zeros(m_scr.shape, F32)

    rr = lax.broadcasted_iota(I32, (L, L), 0)
    cc = lax.broadcasted_iota(I32, (L, L), 1)
    causal = cc <= rr
    triu = jnp.where(rr <= cc, 1.0, 0.0).astype(BF16)
    lane = lax.broadcasted_iota(I32, (8, L), 1)
    ones_col = jnp.where(lax.broadcasted_iota(I32, (L, HEAD_DIM_M), 1) == 0, 1.0, 0.0).astype(BF16)
    for bi in range(n_batch):
        _mlstm_chunk(qk_ref.at[bi], v_ref.at[bi], o_ref.at[bi], ift_ref.at[bi], gbt_ref, hn_ref,
                     out_ref.at[bi], cx_scr.at[bi], m_scr.at[bi], causal, triu, lane, ones_col, L)


def _mlstm_chunk(qk_ref, v_ref, o_ref, ift_ref, gbt_ref, hn_ref, out_ref, cx_scr, m_scr,
                 causal, triu, lane, ones_col, L):
    g_t = ift_ref[...] + gbt_ref[...]
    b_all = sum(jnp.dot(piece, triu, preferred_element_type=F32) for piece in _split3(_log_sigmoid(g_t)))
    b8 = pltpu.roll(b_all, N_HEADS_M, axis=0)
    a8 = g_t - b8
    cm = a8
    shift = 1
    while shift < L:
        cm = jnp.maximum(cm, jnp.where(lane >= shift, pltpu.roll(cm, shift, axis=1), NEG_BIG))
        shift *= 2
    m_prev = m_scr[...]
    mx = jnp.maximum(m_prev, cm)
    mx_last = mx[:, L - 1:L]
    decay8 = jnp.exp(m_prev - mx_last)
    m_scr[...] = b8[:, L - 1:L] + mx_last
    rows = jnp.concatenate([-mx,
                            jnp.exp(m_prev - mx),
                            jnp.exp(-(b8 + mx)),
                            jnp.exp(a8 - mx_last),
                            jnp.zeros((LANES - 32, L), F32)], axis=0)
    cols = rows.T

    o_gate = _sigmoid(o_ref[...])
    for hh in range(N_HEADS_M):
        hs = slice(hh * HEAD_DIM_M, (hh + 1) * HEAD_DIM_M)
        qb16 = qk_ref[:, hs]
        kb16 = qk_ref[:, W_M + hh * HEAD_DIM_M:W_M + (hh + 1) * HEAD_DIM_M]
        v_ext = jnp.concatenate([v_ref[:, hs], ones_col], axis=1)
        u_c = cols[:, hh:hh + 1]
        w_inter = cols[:, 8 + hh:9 + hh]
        em_c = cols[:, 16 + hh:17 + hh]
        wgt_c = cols[:, 24 + hh:25 + hh]
        cx_prev = cx_scr[hh]

        d_mat = jnp.where(causal, jnp.exp(u_c + a8[hh:hh + 1, :]), 0.0)
        s = lax.dot_general(qb16, kb16, (((1,), (1,)), ((), ())), preferred_element_type=F32) * d_mat
        intra = jnp.dot(s.astype(BF16), v_ext, preferred_element_type=F32)
        inter = jnp.dot(qb16, cx_prev.astype(BF16), preferred_element_type=F32)
        both = w_inter * inter + intra
        num = both[:, :HEAD_DIM_M]
        den = both[:, HEAD_DIM_M:HEAD_DIM_M + 1]
        hval = num / jnp.maximum(jnp.abs(den), em_c)

        kw = kb16.astype(F32) * wgt_c
        cx_scr[hh] = decay8[hh:hh + 1] * cx_prev + jnp.dot(kw.T.astype(BF16), v_ext,
                                                           preferred_element_type=F32)

        mu = jnp.mean(hval, axis=1, keepdims=True)
        cen = hval - mu
        var = jnp.mean(cen * cen, axis=1, keepdims=True)
        hn = cen * lax.rsqrt(var + EPS) * hn_ref[:, hs]
        out_ref[:, hs] = (hn * o_gate[:, hs]).astype(BF16)


def _mlstm(qk, v, o_pre, ift, gate_bias, head_norm, *, batch, seq):
    t = batch * seq
    L = MLSTM_CHUNK
    nc = seq // L
    gbt = jnp.broadcast_to(gate_bias.reshape(2 * N_HEADS_M, 1), (2 * N_HEADS_M, L))
    ift_b = ift.reshape(2 * N_HEADS_M, batch, seq).transpose(1, 0, 2)
    row = lambda w: pl.BlockSpec((batch, L, w), lambda c: (0, c, 0))
    out = pl.pallas_call(
        functools.partial(_mlstm_kernel, chunk=L, n_batch=batch),
        out_shape=jax.ShapeDtypeStruct((batch, seq, W_M), BF16),
        grid=(nc,),
        in_specs=[row(2 * W_M), row(W_M), row(W_M),
                  pl.BlockSpec((batch, 2 * N_HEADS_M, L), lambda c: (0, 0, c)),
                  _resident((2 * N_HEADS_M, L)), _resident((1, W_M))],
        out_specs=row(W_M),
        scratch_shapes=[pltpu.VMEM((batch, N_HEADS_M, HEAD_DIM_M, 2 * HEAD_DIM_M), F32),
                        pltpu.VMEM((batch, 8, 1), F32)],
        compiler_params=pltpu.CompilerParams(dimension_semantics=("arbitrary",),
                                             vmem_limit_bytes=VMEM_LIMIT),
        name="mlstm",
    )(qk.reshape(batch, seq, 2 * W_M), v.reshape(batch, seq, W_M), o_pre.reshape(batch, seq, W_M),
      ift_b, gbt, head_norm.reshape(1, -1))
    return out.reshape(t, W_M)


def kernel(x, c, ada_w, ada_b, ffn1_norm, ffn1_w1, ffn1_w3, ffn1_w2, mix_norm, w_in, conv_w, conv_b,
           kv_norm, w_uk, w_uv, mlstm_gate_bias, mlstm_head_norm, rel_bias, w_branch_attn,
           w_branch_mlstm, w_out, ffn2_norm, ffn2_w1, ffn2_w3, ffn2_w2, final_norm):
    batch, seq, d = x.shape
    depth = ada_w.shape[0]
    assert seq % max(FFN_TM, MIX_TM, MLSTM_CHUNK, KEY_BLOCK) == 0
    t = batch * seq
    xf = x.reshape(t, d)
    btile, bmax = _bias_tiles(rel_bias)
    for l in range(depth):
        mod = _adaln(c, ada_w[l], ada_b[l]).reshape(batch, 9, 1, d)
        sh1, sc1, g1, sh2, sc2, g2, sh3, sc3, g3 = [mod[:, n] for n in range(9)]
        xf = _ffn(xf, ffn1_norm[l], sh1, sc1, g1, ffn1_w1[l], ffn1_w3[l], ffn1_w2[l], final_norm,
                  seq=seq, final_norm=False)
        wuk_hdc = w_uk[l].transpose(0, 2, 1).reshape(N_HEADS_A // 2, 2, HEAD_DIM_A, D_LATENT)
        zeros = jnp.zeros_like(wuk_hdc[:, 0])
        wuk_t = jnp.concatenate([jnp.concatenate([wuk_hdc[:, 0], zeros], axis=2),
                                 jnp.concatenate([zeros, wuk_hdc[:, 1]], axis=2)], axis=1).astype(BF16)
        (q_abs, q_idx, k_idx, ckv, ckv_t, w_t, ift, qk_m, v_m, o_pre, gate_a, gate_m) = _mixin(
            xf, mix_norm[l], sh2, sc2, _pack_w_in(w_in[l], d), kv_norm[l], wuk_t, conv_w[l], conv_b[l],
            seq=seq)
        wuv_t = w_uv[l].transpose(0, 2, 1).astype(BF16)
        y_a = _dsa(q_idx, q_abs, w_t, k_idx, ckv, ckv_t, btile, bmax, wuv_t, batch=batch, seq=seq)
        h_m = _mlstm(qk_m, v_m, o_pre, ift, mlstm_gate_bias[l], mlstm_head_norm[l], batch=batch, seq=seq)
        xf = _ffn(xf, ffn2_norm[l], sh3, sc3, g3, ffn2_w1[l], ffn2_w3[l], ffn2_w2[l], final_norm,
                  seq=seq, final_norm=(l == depth - 1),
                  merge=(y_a, h_m, gate_a, gate_m, g2, w_branch_attn[l], w_branch_mlstm[l], w_out[l]))
    return xf.reshape(batch, seq, d)
```

```python
import functools
import math

import jax
import jax.numpy as jnp
from jax import lax
from jax.experimental import pallas as pl
from jax.experimental.pallas import tpu as pltpu

F32 = jnp.float32
BF16 = jnp.bfloat16
I32 = jnp.int32

LANES = 128
VMEM_LIMIT = 56 * 1024 * 1024

N_HEADS_A = 8
HEAD_DIM_A = 64
D_LATENT = 256
N_HEADS_IDX = 8
HEAD_DIM_IDX = 64
TOPK_MAX = 256
Q_BLOCK = 128
N_BUCKETS = 32
MAX_DISTANCE = 128
N_HEADS_M = 4
HEAD_DIM_M = 128
CONV_WIDTH = 4
EPS = 1e-6
IDX_SCALE = (N_HEADS_IDX ** -0.5) * (HEAD_DIM_IDX ** -0.5)
W_A = N_HEADS_A * HEAD_DIM_A
W_M = N_HEADS_M * HEAD_DIM_M

FFN_TM = 512
FFN_CHUNK = 256
MIX_TM = 512
KEY_BLOCK = 256
BLOCK_UNROLL = 2
PLANE_KEYS = 256
MLSTM_CHUNK = 256
NEG_BIG = -1e30
INT_MIN = -2 ** 31

BIAS_PAD = 2 * KEY_BLOCK - Q_BLOCK
BIAS_ROWS = KEY_BLOCK + BIAS_PAD
CKVT_ROWS = D_LATENT + 16
LOG2E = math.log2(math.e)


def _sigmoid(x):
    return 1.0 / (1.0 + jnp.exp(-x))


def _log_sigmoid(x):
    return jnp.minimum(x, 0.0) - jnp.log(1.0 + jnp.exp(-jnp.abs(x)))


def _rms_norm(x, gain):
    ms = jnp.mean(x * x, axis=-1, keepdims=True)
    return x * lax.rsqrt(ms + EPS) * gain


def _split3(x):
    hi = x.astype(BF16)
    r1 = x - hi.astype(F32)
    mid = r1.astype(BF16)
    lo = (r1 - mid.astype(F32)).astype(BF16)
    return hi, mid, lo


def _resident(shape):
    nd = len(shape)
    return pl.BlockSpec(shape, lambda *_: (0,) * nd, pipeline_mode=pl.Buffered(1))


def _adaln_kernel(c_ref, w_ref, b_ref, o_ref):
    c = c_ref[...]
    cond = c * _sigmoid(c)
    o_ref[...] = jnp.dot(cond.astype(BF16), w_ref[...].astype(BF16),
                         preferred_element_type=F32) + b_ref[...]


def _adaln(c, ada_w, ada_b):
    b, d = c.shape
    n = ada_w.shape[1]
    rows = 8
    c_pad = jnp.zeros((rows, d), F32).at[:b].set(c)
    tn = 1024
    out = pl.pallas_call(
        _adaln_kernel,
        out_shape=jax.ShapeDtypeStruct((rows, n), F32),
        grid=(n // tn,),
        in_specs=[pl.BlockSpec((rows, d), lambda j: (0, 0)),
                  pl.BlockSpec((d, tn), lambda j: (0, j)),
                  pl.BlockSpec((1, tn), lambda j: (0, j))],
        out_specs=pl.BlockSpec((rows, tn), lambda j: (0, j)),
        compiler_params=pltpu.CompilerParams(dimension_semantics=("arbitrary",),
                                             vmem_limit_bytes=VMEM_LIMIT),
        name="adaln",
    )(c_pad, ada_w, ada_b.reshape(1, n))
    return out[:b]


def _t5_bucket(dist):
    n = jnp.maximum(dist, 0)
    max_exact = N_BUCKETS // 2
    nf = jnp.maximum(n, 1).astype(F32)
    large = max_exact + (jnp.log(nf / max_exact) / math.log(MAX_DISTANCE / max_exact)
                         * (N_BUCKETS - max_exact)).astype(I32)
    large = jnp.minimum(large, N_BUCKETS - 1)
    return jnp.where(n < max_exact, n, large)


def _bias_kernel(rel_ref, tile_ref, max_ref):
    r = lax.broadcasted_iota(I32, (BIAS_ROWS, LANES), 0)
    i = lax.broadcasted_iota(I32, (BIAS_ROWS, LANES), 1)
    bucket = _t5_bucket(i - r + BIAS_PAD)
    for h in range(N_HEADS_A):
        acc = jnp.zeros((BIAS_ROWS, LANES), F32)
        top = rel_ref[0, h] * LOG2E
        for bkt in range(N_BUCKETS):
            val = rel_ref[bkt, h] * LOG2E
            acc = jnp.where(bucket == bkt, val, acc)
            top = jnp.maximum(top, val)
        tile_ref[h] = acc
        max_ref[:, h * LANES:(h + 1) * LANES] = jnp.full((1, LANES), top, F32)


def _bias_tiles(rel_bias):
    return pl.pallas_call(
        _bias_kernel,
        out_shape=(jax.ShapeDtypeStruct((N_HEADS_A, BIAS_ROWS, LANES), F32),
                   jax.ShapeDtypeStruct((1, N_HEADS_A * LANES), F32)),
        in_specs=[pl.BlockSpec(memory_space=pltpu.SMEM)],
        out_specs=(pl.BlockSpec(memory_space=pltpu.VMEM), pl.BlockSpec(memory_space=pltpu.VMEM)),
        name="bias_tiles",
    )(rel_bias)


def _ffn_kernel(*refs, n_chunks, final_norm, merge):
    if merge:
        (x_ref, ya_ref, hm_ref, ga_ref, gm_ref, gmix_ref, wa_ref, wm_ref, wo_ref), refs = refs[:9], refs[9:]
    else:
        x_ref, refs = refs[0], refs[1:]
    gain_ref, sh_ref, sc_ref, g_ref, w1_ref, w3_ref, w2_ref, fin_ref, o_ref, h_scr, acc_scr = refs
    x = x_ref[...]
    if merge:
        pa = jnp.dot(ya_ref[...], wa_ref[...], preferred_element_type=F32)
        pm = jnp.dot(hm_ref[...], wm_ref[...], preferred_element_type=F32)
        merged = _sigmoid(ga_ref[...]) * pa + _sigmoid(gm_ref[...]) * pm
        x = x + gmix_ref[0] * jnp.dot(merged.astype(BF16), wo_ref[...], preferred_element_type=F32)
    h = _rms_norm(x, gain_ref[...]) * (1.0 + sc_ref[0]) + sh_ref[0]
    h_scr[...] = h.astype(BF16)
    for j in range(n_chunks):
        hb = h_scr[...]
        cols = slice(j * FFN_CHUNK, (j + 1) * FFN_CHUNK)
        u1 = jnp.dot(hb, w1_ref[:, cols], preferred_element_type=F32)
        u3 = jnp.dot(hb, w3_ref[:, cols], preferred_element_type=F32)
        a = (u1 * _sigmoid(u1)) * u3
        part = jnp.dot(a.astype(BF16), w2_ref[j], preferred_element_type=F32)
        if j == 0:
            acc_scr[...] = part
        else:
            acc_scr[...] += part
    out = x + (0.5 * g_ref[0]) * acc_scr[...]
    if final_norm:
        out = _rms_norm(out, fin_ref[...])
    o_ref[...] = out


def _ffn(x, gain, sh, sc, g, w1, w3, w2, fin, *, seq, final_norm, merge=None):
    t, d = x.shape
    dff = w1.shape[1]
    nch = dff // FFN_CHUNK
    w1c = w1.astype(BF16)
    w3c = w3.astype(BF16)
    w2c = w2.astype(BF16).reshape(nch, FFN_CHUNK, d)
    tm = FFN_TM
    per_b = seq // tm
    row = lambda w: pl.BlockSpec((tm, w), lambda i: (i, 0))
    mod_spec = pl.BlockSpec((1, 1, d), lambda i: (i // per_b, 0, 0))
    merge_specs, merge_args = [], []
    if merge is not None:
        y_a, h_m, gate_a, gate_m, g_mix, w_a, w_m, w_o = merge
        merge_specs = [row(W_A), row(W_M), row(d), row(d), mod_spec,
                       _resident((W_A, d)), _resident((W_M, d)), _resident((d, d))]
        merge_args = [y_a, h_m, gate_a, gate_m, g_mix, w_a.astype(BF16), w_m.astype(BF16), w_o.astype(BF16)]
    return pl.pallas_call(
        functools.partial(_ffn_kernel, n_chunks=nch, final_norm=final_norm, merge=merge is not None),
        out_shape=jax.ShapeDtypeStruct((t, d), F32),
        grid=(t // tm,),
        in_specs=[row(d)] + merge_specs + [
                  _resident((1, d)), mod_spec, mod_spec, mod_spec,
                  _resident((d, dff)), _resident((d, dff)),
                  _resident((nch, FFN_CHUNK, d)), _resident((1, d))],
        out_specs=row(d),
        scratch_shapes=[pltpu.VMEM((tm, d), BF16), pltpu.VMEM((tm, d), F32)],
        compiler_params=pltpu.CompilerParams(dimension_semantics=("arbitrary",),
                                             vmem_limit_bytes=VMEM_LIMIT),
        name="ffn_final" if final_norm else "ffn",
    )(x, *merge_args, gain.reshape(1, d), sh, sc, g, w1c, w3c, w2c, fin.reshape(1, d))


_C_QA = 0
_C_CKV = _C_QA + W_A
_C_QI = _C_CKV + D_LATENT
_C_KI = _C_QI + N_HEADS_IDX * LANES
_C_SM = _C_KI + LANES
_C_QK = _C_SM + LANES
_C_V = _C_QK + 2 * W_M
_C_O = _C_V + W_M
_C_GA = _C_O + W_M
_C_GM = _C_GA + 1024
_C_END = _C_GM + 1024
_SM_W = 0
_SM_I = N_HEADS_IDX


def _pack_w_in(w_in, d_model):
    splits = (W_A, D_LATENT, N_HEADS_IDX * HEAD_DIM_IDX, HEAD_DIM_IDX, N_HEADS_IDX,
              W_M, W_M, W_M, N_HEADS_M, N_HEADS_M, W_M, d_model, d_model)
    offs = [0]
    for s in splits:
        offs.append(offs[-1] + s)
    (q_a, c_kv, q_i, k_i, w_i, q_m, k_m, v_m, i_p, f_p, o_p, g_a, g_m) = [
        w_in[:, offs[n]:offs[n + 1]] for n in range(len(splits))]
    d = w_in.shape[0]

    def pad_heads(w, nh, hd):
        w = w.reshape(d, nh, hd)
        return jnp.pad(w, ((0, 0), (0, 0), (0, LANES - hd))).reshape(d, nh * LANES)

    small = jnp.concatenate([w_i, i_p, f_p], axis=1)
    small = jnp.pad(small, ((0, 0), (0, LANES - small.shape[1])))
    packed = jnp.concatenate([
        q_a, c_kv, pad_heads(q_i, N_HEADS_IDX, HEAD_DIM_IDX),
        jnp.pad(k_i, ((0, 0), (0, LANES - HEAD_DIM_IDX))), small, q_m, k_m, v_m, o_p, g_a, g_m], axis=1)
    assert packed.shape[1] == _C_END
    return packed.astype(BF16)


def _mixin_kernel(x_ref, gain_ref, sh_ref, sc_ref, w_ref, kvn_ref, wuk_ref, cw_ref, cb_ref,
                  qabs_ref, qidx_ref, kidx_ref, ckv_ref, ckvt_ref, wt_ref, ift_ref,
                  qk_ref, v_ref, o_ref, ga_ref, gm_ref, h_scr, xe_scr, *, tm, tiles_per_seq):
    nqb = tm // Q_BLOCK

    @pl.when(pl.program_id(0) % tiles_per_seq == 0)
    def _():
        xe_scr[:8] = jnp.zeros((8, xe_scr.shape[1]), F32)

    x = x_ref[...]
    h = _rms_norm(x, gain_ref[...]) * (1.0 + sc_ref[0]) + sh_ref[0]
    h_scr[...] = h.astype(BF16)

    def proj(lo, hi):
        return jnp.dot(h_scr[...], w_ref[:, lo:hi], preferred_element_type=F32)

    qa = proj(_C_QA, _C_CKV)
    scale = HEAD_DIM_A ** -0.5 * LOG2E
    for g in range(N_HEADS_A // 2):
        q_pair = qa[:, g * LANES:(g + 1) * LANES].astype(BF16)
        q_abs = jnp.dot(q_pair, wuk_ref[g], preferred_element_type=F32) * scale
        for j in range(2):
            qabs_ref[:, 2 * g + j] = (q_abs[:, j * D_LATENT:(j + 1) * D_LATENT]
                                      .astype(BF16).reshape(nqb, Q_BLOCK, D_LATENT))
    ckv = _rms_norm(proj(_C_CKV, _C_QI), kvn_ref[...])
    ckv_ref[...] = ckv.astype(BF16)
    ckv_t = ckv.T
    ones_row = jnp.where(lax.broadcasted_iota(I32, (CKVT_ROWS - D_LATENT, KEY_BLOCK), 0) == 0, 1.0, 0.0)
    for j in range(tm // KEY_BLOCK):
        ckvt_ref[j, :D_LATENT] = ckv_t[:, j * KEY_BLOCK:(j + 1) * KEY_BLOCK].astype(BF16)
        ckvt_ref[j, D_LATENT:] = ones_row.astype(BF16)
    qi = proj(_C_QI, _C_KI)
    for hh in range(N_HEADS_IDX):
        qidx_ref[:, hh] = qi[:, hh * LANES:(hh + 1) * LANES].astype(BF16).reshape(nqb, Q_BLOCK, LANES)
    kidx_ref[...] = proj(_C_KI, _C_SM).astype(BF16)
    small_t = proj(_C_SM, _C_QK).T
    wt_ref[...] = small_t[_SM_W:_SM_W + N_HEADS_IDX] * IDX_SCALE
    ift_ref[...] = small_t[_SM_I:_SM_I + 2 * N_HEADS_M]
    xe_scr[8:] = proj(_C_QK, _C_V)
    xe = xe_scr[...]
    xq = xe[8:]
    conv = xq * cw_ref[CONV_WIDTH - 1:CONV_WIDTH, :] + cb_ref[...]
    for d in range(1, CONV_WIDTH):
        conv = conv + pltpu.roll(xe, d, axis=0)[8:] * cw_ref[CONV_WIDTH - 1 - d:CONV_WIDTH - d, :]
    xe_scr[:8] = xe_scr[tm:]
    qk = conv * _sigmoid(conv)
    qk_ref[:, :W_M] = qk[:, :W_M].astype(BF16)
    qk_ref[:, W_M:] = (qk[:, W_M:] * (HEAD_DIM_M ** -0.5)).astype(BF16)
    v_ref[...] = proj(_C_V, _C_O).astype(BF16)
    o_ref[...] = proj(_C_O, _C_GA)
    ga_ref[...] = proj(_C_GA, _C_GM)
    gm_ref[...] = proj(_C_GM, _C_END)


def _mixin(x, gain, sh, sc, w_packed, kv_norm, wuk_t, conv_w, conv_b, *, seq):
    t, d = x.shape
    tm = MIX_TM
    per_b = seq // tm
    nqb = tm // Q_BLOCK
    row = lambda w: pl.BlockSpec((tm, w), lambda i: (i, 0))
    mod_spec = pl.BlockSpec((1, 1, d), lambda i: (i // per_b, 0, 0))
    out_shape = (
        jax.ShapeDtypeStruct((t // Q_BLOCK, N_HEADS_A, Q_BLOCK, D_LATENT), BF16),
        jax.ShapeDtypeStruct((t // Q_BLOCK, N_HEADS_IDX, Q_BLOCK, LANES), BF16),
        jax.ShapeDtypeStruct((t, LANES), BF16),
        jax.ShapeDtypeStruct((t, D_LATENT), BF16),
        jax.ShapeDtypeStruct((t // KEY_BLOCK, CKVT_ROWS, KEY_BLOCK), BF16),
        jax.ShapeDtypeStruct((N_HEADS_IDX, t), F32),
        jax.ShapeDtypeStruct((2 * N_HEADS_M, t), F32),
        jax.ShapeDtypeStruct((t, 2 * W_M), BF16),
        jax.ShapeDtypeStruct((t, W_M), BF16),
        jax.ShapeDtypeStruct((t, W_M), F32),
        jax.ShapeDtypeStruct((t, d), F32),
        jax.ShapeDtypeStruct((t, d), F32),
    )
    out_specs = (
        pl.BlockSpec((nqb, N_HEADS_A, Q_BLOCK, D_LATENT), lambda i: (i, 0, 0, 0)),
        pl.BlockSpec((nqb, N_HEADS_IDX, Q_BLOCK, LANES), lambda i: (i, 0, 0, 0)),
        row(LANES), row(D_LATENT),
        pl.BlockSpec((tm // KEY_BLOCK, CKVT_ROWS, KEY_BLOCK), lambda i: (i, 0, 0)),
        pl.BlockSpec((N_HEADS_IDX, tm), lambda i: (0, i)),
        pl.BlockSpec((2 * N_HEADS_M, tm), lambda i: (0, i)),
        row(2 * W_M), row(W_M), row(W_M), row(d), row(d),
    )
    return pl.pallas_call(
        functools.partial(_mixin_kernel, tm=tm, tiles_per_seq=per_b),
        out_shape=out_shape,
        grid=(t // tm,),
        in_specs=[pl.BlockSpec((tm, d), lambda i: (i, 0)), _resident((1, d)), mod_spec, mod_spec,
                  _resident((d, _C_END)), _resident((1, D_LATENT)),
                  _resident((N_HEADS_A // 2, LANES, 2 * D_LATENT)),
                  _resident((CONV_WIDTH, 2 * W_M)), _resident((1, 2 * W_M))],
        out_specs=out_specs,
        scratch_shapes=[pltpu.VMEM((tm, d), BF16), pltpu.VMEM((tm + 8, 2 * W_M), F32)],
        compiler_params=pltpu.CompilerParams(dimension_semantics=("arbitrary",),
                                             vmem_limit_bytes=VMEM_LIMIT),
        name="mixin",
    )(x, gain.reshape(1, d), sh, sc, w_packed, kv_norm.reshape(1, D_LATENT), wuk_t,
      conv_w, conv_b.reshape(1, -1))


def _sortable_key(score):
    bits = pltpu.bitcast(score, I32)
    bits = jnp.where(bits == INT_MIN, 0, bits)
    return jnp.where(bits < 0, bits ^ 0x7FFFFFFF, bits)


def _bit_transpose32(words):
    v = list(words)
    j, m = 16, 0x0000FFFF
    while j:
        k = 0
        while k < 32:
            t = (v[k] ^ lax.shift_right_logical(v[k + j], jnp.int32(j))) & m
            v[k] = v[k] ^ t
            v[k + j] = v[k + j] ^ (t << j)
            k = (k + j + 1) & ~j
        j >>= 1
        m = (m ^ (m << j)) & 0x7FFFFFFF
    return v


def _dsa_kernel(qidx_ref, qabs_ref, wt_ref, kidx_ref, ckv_ref, ckvt_ref, btile_ref, bmax_ref, wuvt_ref,
                out_ref, keys_scr, planes_scr, cand_scr, tau_scr, acc_scr, m_scr, ltc_scr, kmax_scr,
                qta_scr, qtc_scr, *, topk, n_qb):
    kb_sz = KEY_BLOCK
    step = pl.program_id(1)
    has_c = step >= 1
    qa = jnp.minimum(step, n_qb - 1)
    qc = jnp.maximum(step - 1, 0)
    slot_a = step & 1
    slot_c = 1 - slot_a
    n_a = qa // (kb_sz // Q_BLOCK) + 1
    n_c = qc // (kb_sz // Q_BLOCK) + 1
    qa0 = qa * Q_BLOCK
    qc0 = qc * Q_BLOCK
    row_id = lax.broadcasted_iota(I32, (kb_sz, LANES), 0)
    lane_id = lax.broadcasted_iota(I32, (kb_sz, LANES), 1)
    n_blocks = keys_scr.shape[1]
    n_groups = N_HEADS_A // 2
    pair = 2 * LANES
    ones8 = jnp.ones((8, D_LATENT), BF16)

    @pl.when(step == 0)
    def _():
        keys_scr[1, 0] = jnp.full((kb_sz, LANES), INT_MIN, I32)
        tau_scr[1] = jnp.zeros((1, LANES), I32)
        planes_scr[...] = jnp.zeros(planes_scr.shape, I32)

        def kn_body(kb, mx):
            c = ckv_ref[kb].astype(F32)
            n2 = lax.dot_general(ones8, (c * c).astype(BF16), (((1,), (1,)), ((), ())),
                                 preferred_element_type=F32)
            return jnp.maximum(mx, n2[0:1])
        mx = lax.fori_loop(0, n_blocks, kn_body, jnp.zeros((1, kb_sz), F32))
        kmax_scr[...] = jnp.max(mx, axis=1, keepdims=True)

    for g in range(n_groups):
        qi_g = qidx_ref[0, 2 * g:2 * g + 2].reshape(2 * Q_BLOCK, LANES).astype(F32)
        qta_scr[g] = qi_g.T.astype(BF16)
        qa_g = qabs_ref[0, 2 * g:2 * g + 2].reshape(2 * Q_BLOCK, D_LATENT).astype(F32)
        qtc_scr[g] = qa_g.T.astype(BF16)

    def idx_dot(kb, g):
        return jnp.dot(kidx_ref[kb], qta_scr[g], preferred_element_type=F32)

    def logits(kb, g):
        return jnp.dot(ckv_ref[kb], qtc_scr[g], preferred_element_type=F32)

    def bias_start(kb):
        delta = jnp.minimum(qc0 - kb * kb_sz, BIAS_PAD)
        return pl.multiple_of(BIAS_PAD - delta, LANES)

    qn2 = []
    for g in range(n_groups):
        q_g = qabs_ref[0, 2 * g:2 * g + 2].reshape(2 * Q_BLOCK, D_LATENT).astype(F32)
        qn2.append(lax.dot_general(ones8, (q_g * q_g).astype(BF16), (((1,), (1,)), ((), ())),
                                   preferred_element_type=F32)[0:1])
    bound = jnp.sqrt(jnp.concatenate(qn2, axis=1) * kmax_scr[...]) * 1.02 + bmax_ref[...] + 1e-3
    tau_c = tau_scr[slot_c]
    w_t = wt_ref[...]

    acc_scr[...] = jnp.zeros(acc_scr.shape, F32)
    for g in range(n_groups):
        ltc_scr[:, g * pair:(g + 1) * pair] = logits(0, g)

    bound_far = bound - jnp.concatenate([btile_ref[hh, 0:1, :] for hh in range(N_HEADS_A)], axis=1)

    def block_step(kb_raw, far):
        kb = jnp.minimum(kb_raw, n_a - 1)
        kc = jnp.minimum(kb_raw, n_c - 1)
        kc_next = jnp.minimum(kb_raw + 1, n_c - 1)
        thr = jnp.where(has_c & (kb_raw < n_c), tau_c - 1, jnp.int32(2 ** 31 - 1))
        sel = keys_scr[slot_c, kc] > thr
        ct_blk = ckvt_ref[kc]
        start = bias_start(kc)
        ref_pt = bound_far if far else bound
        score = jnp.zeros((kb_sz, LANES), F32)
        for g in range(n_groups):
            s_t = idx_dot(kb, g)
            for j in range(2):
                hh = 2 * g + j
                score = score + jnp.maximum(s_t[:, j * LANES:(j + 1) * LANES], 0.0) * w_t[hh:hh + 1, :]
        for g in range(n_groups):
            lt = ltc_scr[:, g * pair:(g + 1) * pair]
            ps = []
            for j in range(2):
                hh = 2 * g + j
                piece = lt[:, j * LANES:(j + 1) * LANES]
                if not far:
                    piece = piece + btile_ref[hh, pl.ds(start, kb_sz), :]
                ps.append(jnp.exp2(jnp.where(sel, piece, NEG_BIG) - ref_pt[:, hh * LANES:(hh + 1) * LANES]))
            ltc_scr[:, g * pair:(g + 1) * pair] = logits(kc_next, g)
            acc_scr[g] += jnp.dot(ct_blk, jnp.concatenate(ps, axis=1).astype(BF16),
                                  preferred_element_type=F32)
        valid = (kb * kb_sz + row_id) <= (qa0 + lane_id)
        keys = jnp.where(valid, _sortable_key(score), INT_MIN)
        keys_scr[slot_a, kb] = keys
        v = keys ^ INT_MIN
        for sub in range(kb_sz // PLANE_KEYS):
            r0 = sub * PLANE_KEYS
            words = _bit_transpose32([v[r0 + 8 * i:r0 + 8 * (i + 1), :] for i in range(32)])
            for bit in range(32):
                planes_scr[bit, kb * (kb_sz // PLANE_KEYS) + sub] = words[31 - bit]

    def block_body(far, it, carry):
        for u in range(BLOCK_UNROLL):
            block_step(it * BLOCK_UNROLL + u, far)
        return carry

    n_far_iters = jnp.maximum(n_c - 2, 0) // BLOCK_UNROLL
    lax.fori_loop(0, n_far_iters, functools.partial(block_body, True), 0)
    lax.fori_loop(n_far_iters, (n_a + BLOCK_UNROLL - 1) // BLOCK_UNROLL,
                  functools.partial(block_body, False), 0)
    n_kb = n_a


    n_planes = planes_scr.shape[1]
    live = n_kb * (kb_sz // PLANE_KEYS)

    def radix_select(width):
        blk_id = lax.broadcasted_iota(I32, (width, 8, LANES), 0)
        cand_scr[:width] = jnp.where(blk_id < live, -1, 0)

        def bit_body(it, carry):
            above, tau_u = carry
            bit = 31 - it
            ones = cand_scr[:width] & planes_scr[bit, :width]
            c1 = jnp.sum(jnp.sum(lax.population_count(ones), axis=0), axis=0, keepdims=True)
            take = (above + c1) >= topk
            cand_scr[:width] = jnp.where(take, ones, cand_scr[:width] ^ ones)
            above = jnp.where(take, above, above + c1)
            tau_u = jnp.where(take, tau_u | (jnp.int32(1) << bit), tau_u)
            return above, tau_u

        zero = jnp.zeros((1, LANES), I32)
        above, tau_u = lax.fori_loop(0, 32, bit_body, (zero, zero))
        n_eq = jnp.sum(jnp.sum(lax.population_count(cand_scr[:width]), axis=0), axis=0, keepdims=True)
        return above, tau_u, n_eq

    widths = [n_planes * (i + 1) // 4 for i in range(4)]
    select = functools.partial(radix_select, widths[-1])
    for width in reversed(widths[:-1]):
        select = functools.partial(lax.cond, live <= width, functools.partial(radix_select, width), select)
    n_gt, tau_u, n_eq = select()
    tau = tau_u ^ INT_MIN

    need = topk - n_gt
    overflow = n_eq > need
    seq_bits = max(1, (n_blocks * kb_sz - 1).bit_length())

    @pl.when(jnp.max(jnp.where(overflow, 1, 0)) > 0)
    def _():
        def count_ties_before(trial):
            def body(kb, acc):
                hit = jnp.where((keys_scr[slot_a, kb] == tau) & ((kb * kb_sz + row_id) < trial), 1, 0)
                return acc + jnp.sum(hit.reshape(kb_sz // 8, 8, LANES), axis=0)
            acc = lax.fori_loop(0, n_kb, body, jnp.zeros((8, LANES), I32))
            return jnp.sum(acc, axis=0, keepdims=True)

        def idx_body(it, jc):
            trial = jc | (jnp.int32(1) << (seq_bits - 1 - it))
            return jnp.where(count_ties_before(trial) < need, trial, jc)

        j_cut = lax.fori_loop(0, seq_bits, idx_body, jnp.zeros((1, LANES), I32))

        def demote_body(kb, carry):
            k = keys_scr[slot_a, kb]
            drop = overflow & (k == tau) & ((kb * kb_sz + row_id) > j_cut)
            keys_scr[slot_a, kb] = jnp.where(drop, INT_MIN, k)
            return carry

        lax.fori_loop(0, n_kb, demote_body, 0)

    tau_scr[slot_a] = jnp.maximum(tau, INT_MIN + 1)

    l_min = jnp.min(jnp.concatenate([acc_scr[g, D_LATENT:D_LATENT + 1, :] for g in range(n_groups)], axis=1))

    @pl.when(has_c & jnp.logical_not(l_min >= 2.0 ** -80))
    def _():
        m_scr[...] = jnp.full(m_scr.shape, NEG_BIG, F32)
        acc_scr[...] = jnp.zeros(acc_scr.shape, F32)

        def exact_body(kb, carry):
            sel = keys_scr[slot_c, kb] >= tau_c
            ct_blk = ckvt_ref[kb]
            start = bias_start(kb)
            for g in range(n_groups):
                lt = logits(kb, g)
                ps, alphas = [], []
                for j in range(2):
                    hh = 2 * g + j
                    sl = slice(hh * LANES, (hh + 1) * LANES)
                    piece = lt[:, j * LANES:(j + 1) * LANES] + btile_ref[hh, pl.ds(start, kb_sz), :]
                    masked = jnp.where(sel, piece, NEG_BIG)
                    m_old = m_scr[:, sl]
                    m_new = jnp.maximum(m_old, jnp.max(masked, axis=0, keepdims=True))
                    m_scr[:, sl] = m_new
                    alphas.append(jnp.exp2(m_old - m_new))
                    ps.append(jnp.exp2(masked - m_new))
                pv = jnp.dot(ct_blk, jnp.concatenate(ps, axis=1).astype(BF16), preferred_element_type=F32)
                acc_scr[g] = jnp.concatenate(alphas, axis=1) * acc_scr[g] + pv
            return carry

        lax.fori_loop(0, n_c, exact_body, 0)

    @pl.when(has_c)
    def _():
        ys = []
        for hh in range(N_HEADS_A):
            acc_h = acc_scr[hh // 2, :, (hh % 2) * LANES:(hh % 2 + 1) * LANES]
            o_h = acc_h[:D_LATENT] * (1.0 / acc_h[D_LATENT:D_LATENT + 1])
            ys.append(jnp.dot(wuvt_ref[hh], o_h.astype(BF16), preferred_element_type=F32))
        y_t = jnp.concatenate(ys, axis=0)
        out_ref[...] = y_t.T.astype(BF16)


def _dsa(q_idx, q_abs, w_t, k_idx, ckv, ckv_t, btile, bmax, wuv_t, *, batch, seq):
    t = batch * seq
    nqb = seq // Q_BLOCK
    nkb = seq // KEY_BLOCK
    topk = min(TOPK_MAX, seq // 4)
    k_idx3 = k_idx.reshape(t // KEY_BLOCK, KEY_BLOCK, LANES)
    ckv3 = ckv.reshape(t // KEY_BLOCK, KEY_BLOCK, D_LATENT)
    per_batch = lambda shape: pl.BlockSpec(shape, lambda b, q: (b,) + (0,) * (len(shape) - 1),
                                           pipeline_mode=pl.Buffered(1))
    scored = lambda b, s: b * nqb + jnp.minimum(s, nqb - 1)
    attended = lambda b, s: b * nqb + jnp.maximum(s - 1, 0)
    return pl.pallas_call(
        functools.partial(_dsa_kernel, topk=topk, n_qb=nqb),
        out_shape=jax.ShapeDtypeStruct((t, W_A), BF16),
        grid=(batch, nqb + 1),
        in_specs=[pl.BlockSpec((1, N_HEADS_IDX, Q_BLOCK, LANES), lambda b, s: (scored(b, s), 0, 0, 0)),
                  pl.BlockSpec((1, N_HEADS_A, Q_BLOCK, D_LATENT), lambda b, s: (attended(b, s), 0, 0, 0)),
                  pl.BlockSpec((N_HEADS_IDX, Q_BLOCK), lambda b, s: (0, scored(b, s))),
                  per_batch((nkb, KEY_BLOCK, LANES)),
                  per_batch((nkb, KEY_BLOCK, D_LATENT)),
                  per_batch((nkb, CKVT_ROWS, KEY_BLOCK)),
                  _resident((N_HEADS_A, BIAS_ROWS, LANES)),
                  _resident((1, N_HEADS_A * LANES)),
                  _resident((N_HEADS_A, HEAD_DIM_A, D_LATENT))],
        out_specs=pl.BlockSpec((Q_BLOCK, W_A), lambda b, s: (attended(b, s), 0)),
        scratch_shapes=[pltpu.VMEM((2, nkb, KEY_BLOCK, LANES), I32),
                        pltpu.VMEM((32, seq // PLANE_KEYS + 1, 8, LANES), I32),
                        pltpu.VMEM((seq // PLANE_KEYS + 1, 8, LANES), I32),
                        pltpu.VMEM((2, 1, LANES), I32),
                        pltpu.VMEM((N_HEADS_A // 2, CKVT_ROWS, 2 * LANES), F32),
                        pltpu.VMEM((1, N_HEADS_A * LANES), F32),
                        pltpu.VMEM((KEY_BLOCK, N_HEADS_A * LANES), F32),
                        pltpu.VMEM((1, 1), F32),
                        pltpu.VMEM((N_HEADS_IDX // 2, LANES, 2 * LANES), BF16),
                        pltpu.VMEM((N_HEADS_A // 2, D_LATENT, 2 * LANES), BF16)],
        compiler_params=pltpu.CompilerParams(dimension_semantics=("arbitrary", "arbitrary"),
                                             vmem_limit_bytes=VMEM_LIMIT),
        name="dsa",
    )(q_idx, q_abs, w_t, k_idx3, ckv3, ckv_t, btile, bmax, wuv_t)


def _mlstm_kernel(qk_ref, v_ref, o_ref, ift_ref, gbt_ref, hn_ref,
                  out_ref, cx_scr, m_scr, *, chunk, n_batch):
    L = chunk

    @pl.when(pl.program_id(0) == 0)
    def _():
        cx_scr[...] = jnp.zeros(cx_scr.shape, F32)
        m_scr[...] = jnp.zeros(m_scr.shape, F32)

    rr = lax.broadcasted_iota(I32, (L, L), 0)
    cc = lax.broadcasted_iota(I32, (L, L), 1)
    causal = cc <= rr
    triu = jnp.where(rr <= cc, 1.0, 0.0).astype(BF16)
    lane = lax.broadcasted_iota(I32, (8, L), 1)
    ones_col = jnp.where(lax.broadcasted_iota(I32, (L, HEAD_DIM_M), 1) == 0, 1.0, 0.0).astype(BF16)
    for bi in range(n_batch):
        _mlstm_chunk(qk_ref.at[bi], v_ref.at[bi], o_ref.at[bi], ift_ref.at[bi], gbt_ref, hn_ref,
                     out_ref.at[bi], cx_scr.at[bi], m_scr.at[bi], causal, triu, lane, ones_col, L)


def _mlstm_chunk(qk_ref, v_ref, o_ref, ift_ref, gbt_ref, hn_ref, out_ref, cx_scr, m_scr,
                 causal, triu, lane, ones_col, L):
    g_t = ift_ref[...] + gbt_ref[...]
    b_all = sum(jnp.dot(piece, triu, preferred_element_type=F32) for piece in _split3(_log_sigmoid(g_t)))
    b8 = pltpu.roll(b_all, N_HEADS_M, axis=0)
    a8 = g_t - b8
    cm = a8
    shift = 1
    while shift < L:
        cm = jnp.maximum(cm, jnp.where(lane >= shift, pltpu.roll(cm, shift, axis=1), NEG_BIG))
        shift *= 2
    m_prev = m_scr[...]
    mx = jnp.maximum(m_prev, cm)
    mx_last = mx[:, L - 1:L]
    decay8 = jnp.exp(m_prev - mx_last)
    m_scr[...] = b8[:, L - 1:L] + mx_last
    rows = jnp.concatenate([-mx,
                            jnp.exp(m_prev - mx),
                            jnp.exp(-(b8 + mx)),
                            jnp.exp(a8 - mx_last),
                            jnp.zeros((LANES - 32, L), F32)], axis=0)
    cols = rows.T

    o_gate = _sigmoid(o_ref[...])
    for hh in range(N_HEADS_M):
        hs = slice(hh * HEAD_DIM_M, (hh + 1) * HEAD_DIM_M)
        qb16 = qk_ref[:, hs]
        kb16 = qk_ref[:, W_M + hh * HEAD_DIM_M:W_M + (hh + 1) * HEAD_DIM_M]
        v_ext = jnp.concatenate([v_ref[:, hs], ones_col], axis=1)
        u_c = cols[:, hh:hh + 1]
        w_inter = cols[:, 8 + hh:9 + hh]
        em_c = cols[:, 16 + hh:17 + hh]
        wgt_c = cols[:, 24 + hh:25 + hh]
        cx_prev = cx_scr[hh]

        d_mat = jnp.where(causal, jnp.exp(u_c + a8[hh:hh + 1, :]), 0.0)
        s = lax.dot_general(qb16, kb16, (((1,), (1,)), ((), ())), preferred_element_type=F32) * d_mat
        intra = jnp.dot(s.astype(BF16), v_ext, preferred_element_type=F32)
        inter = jnp.dot(qb16, cx_prev.astype(BF16), preferred_element_type=F32)
        both = w_inter * inter + intra
        num = both[:, :HEAD_DIM_M]
        den = both[:, HEAD_DIM_M:HEAD_DIM_M + 1]
        hval = num / jnp.maximum(jnp.abs(den), em_c)

        kw = kb16.astype(F32) * wgt_c
        cx_scr[hh] = decay8[hh:hh + 1] * cx_prev + jnp.dot(kw.T.astype(BF16), v_ext,
                                                           preferred_element_type=F32)

        mu = jnp.mean(hval, axis=1, keepdims=True)
        cen = hval - mu
        var = jnp.mean(cen * cen, axis=1, keepdims=True)
        hn = cen * lax.rsqrt(var + EPS) * hn_ref[:, hs]
        out_ref[:, hs] = (hn * o_gate[:, hs]).astype(BF16)


def _mlstm(qk, v, o_pre, ift, gate_bias, head_norm, *, batch, seq):
    t = batch * seq
    L = MLSTM_CHUNK
    nc = seq // L
    gbt = jnp.broadcast_to(gate_bias.reshape(2 * N_HEADS_M, 1), (2 * N_HEADS_M, L))
    ift_b = ift.reshape(2 * N_HEADS_M, batch, seq).transpose(1, 0, 2)
    row = lambda w: pl.BlockSpec((batch, L, w), lambda c: (0, c, 0))
    out = pl.pallas_call(
        functools.partial(_mlstm_kernel, chunk=L, n_batch=batch),
        out_shape=jax.ShapeDtypeStruct((batch, seq, W_M), BF16),
        grid=(nc,),
        in_specs=[row(2 * W_M), row(W_M), row(W_M),
                  pl.BlockSpec((batch, 2 * N_HEADS_M, L), lambda c: (0, 0, c)),
                  _resident((2 * N_HEADS_M, L)), _resident((1, W_M))],
        out_specs=row(W_M),
        scratch_shapes=[pltpu.VMEM((batch, N_HEADS_M, HEAD_DIM_M, 2 * HEAD_DIM_M), F32),
                        pltpu.VMEM((batch, 8, 1), F32)],
        compiler_params=pltpu.CompilerParams(dimension_semantics=("arbitrary",),
                                             vmem_limit_bytes=VMEM_LIMIT),
        name="mlstm",
    )(qk.reshape(batch, seq, 2 * W_M), v.reshape(batch, seq, W_M), o_pre.reshape(batch, seq, W_M),
      ift_b, gbt, head_norm.reshape(1, -1))
    return out.reshape(t, W_M)


def kernel(x, c, ada_w, ada_b, ffn1_norm, ffn1_w1, ffn1_w3, ffn1_w2, mix_norm, w_in, conv_w, conv_b,
           kv_norm, w_uk, w_uv, mlstm_gate_bias, mlstm_head_norm, rel_bias, w_branch_attn,
           w_branch_mlstm, w_out, ffn2_norm, ffn2_w1, ffn2_w3, ffn2_w2, final_norm):
    batch, seq, d = x.shape
    depth = ada_w.shape[0]
    assert seq % max(FFN_TM, MIX_TM, MLSTM_CHUNK, KEY_BLOCK) == 0
    t = batch * seq
    xf = x.reshape(t, d)
    btile, bmax = _bias_tiles(rel_bias)
    for l in range(depth):
        mod = _adaln(c, ada_w[l], ada_b[l]).reshape(batch, 9, 1, d)
        sh1, sc1, g1, sh2, sc2, g2, sh3, sc3, g3 = [mod[:, n] for n in range(9)]
        xf = _ffn(xf, ffn1_norm[l], sh1, sc1, g1, ffn1_w1[l], ffn1_w3[l], ffn1_w2[l], final_norm,
                  seq=seq, final_norm=False)
        wuk_hdc = w_uk[l].transpose(0, 2, 1).reshape(N_HEADS_A // 2, 2, HEAD_DIM_A, D_LATENT)
        zeros = jnp.zeros_like(wuk_hdc[:, 0])
        wuk_t = jnp.concatenate([jnp.concatenate([wuk_hdc[:, 0], zeros], axis=2),
                                 jnp.concatenate([zeros, wuk_hdc[:, 1]], axis=2)], axis=1).astype(BF16)
        (q_abs, q_idx, k_idx, ckv, ckv_t, w_t, ift, qk_m, v_m, o_pre, gate_a, gate_m) = _mixin(
            xf, mix_norm[l], sh2, sc2, _pack_w_in(w_in[l], d), kv_norm[l], wuk_t, conv_w[l], conv_b[l],
            seq=seq)
        wuv_t = w_uv[l].transpose(0, 2, 1).astype(BF16)
        y_a = _dsa(q_idx, q_abs, w_t, k_idx, ckv, ckv_t, btile, bmax, wuv_t, batch=batch, seq=seq)
        h_m = _mlstm(qk_m, v_m, o_pre, ift, mlstm_gate_bias[l], mlstm_head_norm[l], batch=batch, seq=seq)
        xf = _ffn(xf, ffn2_norm[l], sh3, sc3, g3, ffn2_w1[l], ffn2_w3[l], ffn2_w2[l], final_norm,
                  seq=seq, final_norm=(l == depth - 1),
                  merge=(y_a, h_m, gate_a, gate_m, g2, w_branch_attn[l], w_branch_mlstm[l], w_out[l]))
    return xf.reshape(batch, seq, d)
```

```python
import functools
import math

import jax
import jax.numpy as jnp
from jax import lax
from jax.experimental import pallas as pl
from jax.experimental.pallas import tpu as pltpu

F32 = jnp.float32
BF16 = jnp.bfloat16
I32 = jnp.int32

LANES = 128
VMEM_LIMIT = 56 * 1024 * 1024

N_HEADS_A = 8
HEAD_DIM_A = 64
D_LATENT = 256
N_HEADS_IDX = 8
HEAD_DIM_IDX = 64
TOPK_MAX = 256
Q_BLOCK = 128
N_BUCKETS = 32
MAX_DISTANCE = 128
N_HEADS_M = 4
HEAD_DIM_M = 128
CONV_WIDTH = 4
EPS = 1e-6
IDX_SCALE = (N_HEADS_IDX ** -0.5) * (HEAD_DIM_IDX ** -0.5)
W_A = N_HEADS_A * HEAD_DIM_A
W_M = N_HEADS_M * HEAD_DIM_M

FFN_TM = 512
FFN_CHUNK = 256
MIX_TM = 512
KEY_BLOCK = 256
BLOCK_UNROLL = 2
PLANE_KEYS = 256
MLSTM_CHUNK = 256
NEG_BIG = -1e30
INT_MIN = -2 ** 31

BIAS_PAD = 2 * KEY_BLOCK - Q_BLOCK
BIAS_ROWS = KEY_BLOCK + BIAS_PAD
CKVT_ROWS = D_LATENT + 16
LOG2E = math.log2(math.e)


def _sigmoid(x):
    return 1.0 / (1.0 + jnp.exp(-x))


def _log_sigmoid(x):
    return jnp.minimum(x, 0.0) - jnp.log(1.0 + jnp.exp(-jnp.abs(x)))


def _rms_norm(x, gain):
    ms = jnp.mean(x * x, axis=-1, keepdims=True)
    return x * lax.rsqrt(ms + EPS) * gain


def _split3(x):
    hi = x.astype(BF16)
    r1 = x - hi.astype(F32)
    mid = r1.astype(BF16)
    lo = (r1 - mid.astype(F32)).astype(BF16)
    return hi, mid, lo


def _resident(shape):
    nd = len(shape)
    return pl.BlockSpec(shape, lambda *_: (0,) * nd, pipeline_mode=pl.Buffered(1))


def _adaln_kernel(c_ref, w_ref, b_ref, o_ref):
    c = c_ref[...]
    cond = c * _sigmoid(c)
    o_ref[...] = jnp.dot(cond.astype(BF16), w_ref[...].astype(BF16),
                         preferred_element_type=F32) + b_ref[...]


def _adaln(c, ada_w, ada_b):
    b, d = c.shape
    n = ada_w.shape[1]
    rows = 8
    c_pad = jnp.zeros((rows, d), F32).at[:b].set(c)
    tn = 1024
    out = pl.pallas_call(
        _adaln_kernel,
        out_shape=jax.ShapeDtypeStruct((rows, n), F32),
        grid=(n // tn,),
        in_specs=[pl.BlockSpec((rows, d), lambda j: (0, 0)),
                  pl.BlockSpec((d, tn), lambda j: (0, j)),
                  pl.BlockSpec((1, tn), lambda j: (0, j))],
        out_specs=pl.BlockSpec((rows, tn), lambda j: (0, j)),
        compiler_params=pltpu.CompilerParams(dimension_semantics=("arbitrary",),
                                             vmem_limit_bytes=VMEM_LIMIT),
        name="adaln",
    )(c_pad, ada_w, ada_b.reshape(1, n))
    return out[:b]


def _t5_bucket(dist):
    n = jnp.maximum(dist, 0)
    max_exact = N_BUCKETS // 2
    nf = jnp.maximum(n, 1).astype(F32)
    large = max_exact + (jnp.log(nf / max_exact) / math.log(MAX_DISTANCE / max_exact)
                         * (N_BUCKETS - max_exact)).astype(I32)
    large = jnp.minimum(large, N_BUCKETS - 1)
    return jnp.where(n < max_exact, n, large)


def _bias_kernel(rel_ref, tile_ref, max_ref):
    r = lax.broadcasted_iota(I32, (BIAS_ROWS, LANES), 0)
    i = lax.broadcasted_iota(I32, (BIAS_ROWS, LANES), 1)
    bucket = _t5_bucket(i - r + BIAS_PAD)
    for h in range(N_HEADS_A):
        acc = jnp.zeros((BIAS_ROWS, LANES), F32)
        top = rel_ref[0, h] * LOG2E
        for bkt in range(N_BUCKETS):
            val = rel_ref[bkt, h] * LOG2E
            acc = jnp.where(bucket == bkt, val, acc)
            top = jnp.maximum(top, val)
        tile_ref[h] = acc
        max_ref[:, h * LANES:(h + 1) * LANES] = jnp.full((1, LANES), top, F32)


def _bias_tiles(rel_bias):
    return pl.pallas_call(
        _bias_kernel,
        out_shape=(jax.ShapeDtypeStruct((N_HEADS_A, BIAS_ROWS, LANES), F32),
                   jax.ShapeDtypeStruct((1, N_HEADS_A * LANES), F32)),
        in_specs=[pl.BlockSpec(memory_space=pltpu.SMEM)],
        out_specs=(pl.BlockSpec(memory_space=pltpu.VMEM), pl.BlockSpec(memory_space=pltpu.VMEM)),
        name="bias_tiles",
    )(rel_bias)


def _ffn_kernel(*refs, n_chunks, final_norm, merge):
    if merge:
        (x_ref, ya_ref, hm_ref, ga_ref, gm_ref, gmix_ref, wa_ref, wm_ref, wo_ref), refs = refs[:9], refs[9:]
    else:
        x_ref, refs = refs[0], refs[1:]
    gain_ref, sh_ref, sc_ref, g_ref, w1_ref, w3_ref, w2_ref, fin_ref, o_ref, h_scr, acc_scr = refs
    x = x_ref[...]
    if merge:
        pa = jnp.dot(ya_ref[...], wa_ref[...], preferred_element_type=F32)
        pm = jnp.dot(hm_ref[...], wm_ref[...], preferred_element_type=F32)
        merged = _sigmoid(ga_ref[...]) * pa + _sigmoid(gm_ref[...]) * pm
        x = x + gmix_ref[0] * jnp.dot(merged.astype(BF16), wo_ref[...], preferred_element_type=F32)
    h = _rms_norm(x, gain_ref[...]) * (1.0 + sc_ref[0]) + sh_ref[0]
    h_scr[...] = h.astype(BF16)
    for j in range(n_chunks):
        hb = h_scr[...]
        cols = slice(j * FFN_CHUNK, (j + 1) * FFN_CHUNK)
        u1 = jnp.dot(hb, w1_ref[:, cols], preferred_element_type=F32)
        u3 = jnp.dot(hb, w3_ref[:, cols], preferred_element_type=F32)
        a = (u1 * _sigmoid(u1)) * u3
        part = jnp.dot(a.astype(BF16), w2_ref[j], preferred_element_type=F32)
        if j == 0:
            acc_scr[...] = part
        else:
            acc_scr[...] += part
    out = x + (0.5 * g_ref[0]) * acc_scr[...]
    if final_norm:
        out = _rms_norm(out, fin_ref[...])
    o_ref[...] = out


def _ffn(x, gain, sh, sc, g, w1, w3, w2, fin, *, seq, final_norm, merge=None):
    t, d = x.shape
    dff = w1.shape[1]
    nch = dff // FFN_CHUNK
    w1c = w1.astype(BF16)
    w3c = w3.astype(BF16)
    w2c = w2.astype(BF16).reshape(nch, FFN_CHUNK, d)
    tm = FFN_TM
    per_b = seq // tm
    row = lambda w: pl.BlockSpec((tm, w), lambda i: (i, 0))
    mod_spec = pl.BlockSpec((1, 1, d), lambda i: (i // per_b, 0, 0))
    merge_specs, merge_args = [], []
    if merge is not None:
        y_a, h_m, gate_a, gate_m, g_mix, w_a, w_m, w_o = merge
        merge_specs = [row(W_A), row(W_M), row(d), row(d), mod_spec,
                       _resident((W_A, d)), _resident((W_M, d)), _resident((d, d))]
        merge_args = [y_a, h_m, gate_a, gate_m, g_mix, w_a.astype(BF16), w_m.astype(BF16), w_o.astype(BF16)]
    return pl.pallas_call(
        functools.partial(_ffn_kernel, n_chunks=nch, final_norm=final_norm, merge=merge is not None),
        out_shape=jax.ShapeDtypeStruct((t, d), F32),
        grid=(t // tm,),
        in_specs=[row(d)] + merge_specs + [
                  _resident((1, d)), mod_spec, mod_spec, mod_spec,
                  _resident((d, dff)), _resident((d, dff)),
                  _resident((nch, FFN_CHUNK, d)), _resident((1, d))],
        out_specs=row(d),
        scratch_shapes=[pltpu.VMEM((tm, d), BF16), pltpu.VMEM((tm, d), F32)],
        compiler_params=pltpu.CompilerParams(dimension_semantics=("arbitrary",),
                                             vmem_limit_bytes=VMEM_LIMIT),
        name="ffn_final" if final_norm else "ffn",
    )(x, *merge_args, gain.reshape(1, d), sh, sc, g, w1c, w3c, w2c, fin.reshape(1, d))


_C_QA = 0
_C_CKV = _C_QA + W_A
_C_QI = _C_CKV + D_LATENT
_C_KI = _C_QI + N_HEADS_IDX * LANES
_C_SM = _C_KI + LANES
_C_QK = _C_SM + LANES
_C_V = _C_QK + 2 * W_M
_C_O = _C_V + W_M
_C_GA = _C_O + W_M
_C_GM = _C_GA + 1024
_C_END = _C_GM + 1024
_SM_W = 0
_SM_I = N_HEADS_IDX


def _pack_w_in(w_in, d_model):
    splits = (W_A, D_LATENT, N_HEADS_IDX * HEAD_DIM_IDX, HEAD_DIM_IDX, N_HEADS_IDX,
              W_M, W_M, W_M, N_HEADS_M, N_HEADS_M, W_M, d_model, d_model)
    offs = [0]
    for s in splits:
        offs.append(offs[-1] + s)
    (q_a, c_kv, q_i, k_i, w_i, q_m, k_m, v_m, i_p, f_p, o_p, g_a, g_m) = [
        w_in[:, offs[n]:offs[n + 1]] for n in range(len(splits))]
    d = w_in.shape[0]

    def pad_heads(w, nh, hd):
        w = w.reshape(d, nh, hd)
        return jnp.pad(w, ((0, 0), (0, 0), (0, LANES - hd))).reshape(d, nh * LANES)

    small = jnp.concatenate([w_i, i_p, f_p], axis=1)
    small = jnp.pad(small, ((0, 0), (0, LANES - small.shape[1])))
    packed = jnp.concatenate([
        q_a, c_kv, pad_heads(q_i, N_HEADS_IDX, HEAD_DIM_IDX),
        jnp.pad(k_i, ((0, 0), (0, LANES - HEAD_DIM_IDX))), small, q_m, k_m, v_m, o_p, g_a, g_m], axis=1)
    assert packed.shape[1] == _C_END
    return packed.astype(BF16)


def _mixin_kernel(x_ref, gain_ref, sh_ref, sc_ref, w_ref, kvn_ref, wuk_ref, cw_ref, cb_ref,
                  qabs_ref, qidx_ref, kidx_ref, ckv_ref, ckvt_ref, wt_ref, ift_ref,
                  qk_ref, v_ref, o_ref, ga_ref, gm_ref, h_scr, xe_scr, *, tm, tiles_per_seq):
    nqb = tm // Q_BLOCK

    @pl.when(pl.program_id(0) % tiles_per_seq == 0)
    def _():
        xe_scr[:8] = jnp.zeros((8, xe_scr.shape[1]), F32)

    x = x_ref[...]
    h = _rms_norm(x, gain_ref[...]) * (1.0 + sc_ref[0]) + sh_ref[0]
    h_scr[...] = h.astype(BF16)

    def proj(lo, hi):
        return jnp.dot(h_scr[...], w_ref[:, lo:hi], preferred_element_type=F32)

    qa = proj(_C_QA, _C_CKV)
    scale = HEAD_DIM_A ** -0.5 * LOG2E
    for g in range(N_HEADS_A // 2):
        q_pair = qa[:, g * LANES:(g + 1) * LANES].astype(BF16)
        q_abs = jnp.dot(q_pair, wuk_ref[g], preferred_element_type=F32) * scale
        for j in range(2):
            qabs_ref[:, 2 * g + j] = (q_abs[:, j * D_LATENT:(j + 1) * D_LATENT]
                                      .astype(BF16).reshape(nqb, Q_BLOCK, D_LATENT))
    ckv = _rms_norm(proj(_C_CKV, _C_QI), kvn_ref[...])
    ckv_ref[...] = ckv.astype(BF16)
    ckv_t = ckv.T
    ones_row = jnp.where(lax.broadcasted_iota(I32, (CKVT_ROWS - D_LATENT, KEY_BLOCK), 0) == 0, 1.0, 0.0)
    for j in range(tm // KEY_BLOCK):
        ckvt_ref[j, :D_LATENT] = ckv_t[:, j * KEY_BLOCK:(j + 1) * KEY_BLOCK].astype(BF16)
        ckvt_ref[j, D_LATENT:] = ones_row.astype(BF16)
    qi = proj(_C_QI, _C_KI)
    for hh in range(N_HEADS_IDX):
        qidx_ref[:, hh] = qi[:, hh * LANES:(hh + 1) * LANES].astype(BF16).reshape(nqb, Q_BLOCK, LANES)
    kidx_ref[...] = proj(_C_KI, _C_SM).astype(BF16)
    small_t = proj(_C_SM, _C_QK).T
    wt_ref[...] = small_t[_SM_W:_SM_W + N_HEADS_IDX] * IDX_SCALE
    ift_ref[...] = small_t[_SM_I:_SM_I + 2 * N_HEADS_M]
    xe_scr[8:] = proj(_C_QK, _C_V)
    xe = xe_scr[...]
    xq = xe[8:]
    conv = xq * cw_ref[CONV_WIDTH - 1:CONV_WIDTH, :] + cb_ref[...]
    for d in range(1, CONV_WIDTH):
        conv = conv + pltpu.roll(xe, d, axis=0)[8:] * cw_ref[CONV_WIDTH - 1 - d:CONV_WIDTH - d, :]
    xe_scr[:8] = xe_scr[tm:]
    qk = conv * _sigmoid(conv)
    qk_ref[:, :W_M] = qk[:, :W_M].astype(BF16)
    qk_ref[:, W_M:] = (qk[:, W_M:] * (HEAD_DIM_M ** -0.5)).astype(BF16)
    v_ref[...] = proj(_C_V, _C_O).astype(BF16)
    o_ref[...] = proj(_C_O, _C_GA)
    ga_ref[...] = proj(_C_GA, _C_GM)
    gm_ref[...] = proj(_C_GM, _C_END)


def _mixin(x, gain, sh, sc, w_packed, kv_norm, wuk_t, conv_w, conv_b, *, seq):
    t, d = x.shape
    tm = MIX_TM
    per_b = seq // tm
    nqb = tm // Q_BLOCK
    row = lambda w: pl.BlockSpec((tm, w), lambda i: (i, 0))
    mod_spec = pl.BlockSpec((1, 1, d), lambda i: (i // per_b, 0, 0))
    out_shape = (
        jax.ShapeDtypeStruct((t // Q_BLOCK, N_HEADS_A, Q_BLOCK, D_LATENT), BF16),
        jax.ShapeDtypeStruct((t // Q_BLOCK, N_HEADS_IDX, Q_BLOCK, LANES), BF16),
        jax.ShapeDtypeStruct((t, LANES), BF16),
        jax.ShapeDtypeStruct((t, D_LATENT), BF16),
        jax.ShapeDtypeStruct((t // KEY_BLOCK, CKVT_ROWS, KEY_BLOCK), BF16),
        jax.ShapeDtypeStruct((N_HEADS_IDX, t), F32),
        jax.ShapeDtypeStruct((2 * N_HEADS_M, t), F32),
        jax.ShapeDtypeStruct((t, 2 * W_M), BF16),
        jax.ShapeDtypeStruct((t, W_M), BF16),
        jax.ShapeDtypeStruct((t, W_M), F32),
        jax.ShapeDtypeStruct((t, d), F32),
        jax.ShapeDtypeStruct((t, d), F32),
    )
    out_specs = (
        pl.BlockSpec((nqb, N_HEADS_A, Q_BLOCK, D_LATENT), lambda i: (i, 0, 0, 0)),
        pl.BlockSpec((nqb, N_HEADS_IDX, Q_BLOCK, LANES), lambda i: (i, 0, 0, 0)),
        row(LANES), row(D_LATENT),
        pl.BlockSpec((tm // KEY_BLOCK, CKVT_ROWS, KEY_BLOCK), lambda i: (i, 0, 0)),
        pl.BlockSpec((N_HEADS_IDX, tm), lambda i: (0, i)),
        pl.BlockSpec((2 * N_HEADS_M, tm), lambda i: (0, i)),
        row(2 * W_M), row(W_M), row(W_M), row(d), row(d),
    )
    return pl.pallas_call(
        functools.partial(_mixin_kernel, tm=tm, tiles_per_seq=per_b),
        out_shape=out_shape,
        grid=(t // tm,),
        in_specs=[pl.BlockSpec((tm, d), lambda i: (i, 0)), _resident((1, d)), mod_spec, mod_spec,
                  _resident((d, _C_END)), _resident((1, D_LATENT)),
                  _resident((N_HEADS_A // 2, LANES, 2 * D_LATENT)),
                  _resident((CONV_WIDTH, 2 * W_M)), _resident((1, 2 * W_M))],
        out_specs=out_specs,
        scratch_shapes=[pltpu.VMEM((tm, d), BF16), pltpu.VMEM((tm + 8, 2 * W_M), F32)],
        compiler_params=pltpu.CompilerParams(dimension_semantics=("arbitrary",),
                                             vmem_limit_bytes=VMEM_LIMIT),
        name="mixin",
    )(x, gain.reshape(1, d), sh, sc, w_packed, kv_norm.reshape(1, D_LATENT), wuk_t,
      conv_w, conv_b.reshape(1, -1))


def _sortable_key(score):
    bits = pltpu.bitcast(score, I32)
    bits = jnp.where(bits == INT_MIN, 0, bits)
    return jnp.where(bits < 0, bits ^ 0x7FFFFFFF, bits)


def _bit_transpose32(words):
    v = list(words)
    j, m = 16, 0x0000FFFF
    while j:
        k = 0
        while k < 32:
            t = (v[k] ^ lax.shift_right_logical(v[k + j], jnp.int32(j))) & m
            v[k] = v[k] ^ t
            v[k + j] = v[k + j] ^ (t << j)
            k = (k + j + 1) & ~j
        j >>= 1
        m = (m ^ (m << j)) & 0x7FFFFFFF
    return v


def _dsa_kernel(qidx_ref, qabs_ref, wt_ref, kidx_ref, ckv_ref, ckvt_ref, btile_ref, bmax_ref, wuvt_ref,
                out_ref, keys_scr, planes_scr, cand_scr, tau_scr, acc_scr, m_scr, ltc_scr, kmax_scr,
                qta_scr, qtc_scr, *, topk, n_qb):
    kb_sz = KEY_BLOCK
    step = pl.program_id(1)
    has_c = step >= 1
    qa = jnp.minimum(step, n_qb - 1)
    qc = jnp.maximum(step - 1, 0)
    slot_a = step & 1
    slot_c = 1 - slot_a
    n_a = qa // (kb_sz // Q_BLOCK) + 1
    n_c = qc // (kb_sz // Q_BLOCK) + 1
    qa0 = qa * Q_BLOCK
    qc0 = qc * Q_BLOCK
    row_id = lax.broadcasted_iota(I32, (kb_sz, LANES), 0)
    lane_id = lax.broadcasted_iota(I32, (kb_sz, LANES), 1)
    n_blocks = keys_scr.shape[1]
    n_groups = N_HEADS_A // 2
    pair = 2 * LANES
    ones8 = jnp.ones((8, D_LATENT), BF16)

    @pl.when(step == 0)
    def _():
        keys_scr[1, 0] = jnp.full((kb_sz, LANES), INT_MIN, I32)
        tau_scr[1] = jnp.zeros((1, LANES), I32)
        planes_scr[...] = jnp.zeros(planes_scr.shape, I32)

        def kn_body(kb, mx):
            c = ckv_ref[kb].astype(F32)
            n2 = lax.dot_general(ones8, (c * c).astype(BF16), (((1,), (1,)), ((), ())),
                                 preferred_element_type=F32)
            return jnp.maximum(mx, n2[0:1])
        mx = lax.fori_loop(0, n_blocks, kn_body, jnp.zeros((1, kb_sz), F32))
        kmax_scr[...] = jnp.max(mx, axis=1, keepdims=True)

    for g in range(n_groups):
        qi_g = qidx_ref[0, 2 * g:2 * g + 2].reshape(2 * Q_BLOCK, LANES).astype(F32)
        qta_scr[g] = qi_g.T.astype(BF16)
        qa_g = qabs_ref[0, 2 * g:2 * g + 2].reshape(2 * Q_BLOCK, D_LATENT).astype(F32)
        qtc_scr[g] = qa_g.T.astype(BF16)

    def idx_dot(kb, g):
        return jnp.dot(kidx_ref[kb], qta_scr[g], preferred_element_type=F32)

    def logits(kb, g):
        return jnp.dot(ckv_ref[kb], qtc_scr[g], preferred_element_type=F32)

    def bias_start(kb):
        delta = jnp.minimum(qc0 - kb * kb_sz, BIAS_PAD)
        return pl.multiple_of(BIAS_PAD - delta, LANES)

    qn2 = []
    for g in range(n_groups):
        q_g = qabs_ref[0, 2 * g:2 * g + 2].reshape(2 * Q_BLOCK, D_LATENT).astype(F32)
        qn2.append(lax.dot_general(ones8, (q_g * q_g).astype(BF16), (((1,), (1,)), ((), ())),
                                   preferred_element_type=F32)[0:1])
    bound = jnp.sqrt(jnp.concatenate(qn2, axis=1) * kmax_scr[...]) * 1.02 + bmax_ref[...] + 1e-3
    tau_c = tau_scr[slot_c]
    w_t = wt_ref[...]

    acc_scr[...] = jnp.zeros(acc_scr.shape, F32)
    for g in range(n_groups):
        ltc_scr[:, g * pair:(g + 1) * pair] = logits(0, g)

    bound_far = bound - jnp.concatenate([btile_ref[hh, 0:1, :] for hh in range(N_HEADS_A)], axis=1)

    def block_step(kb_raw, far):
        kb = jnp.minimum(kb_raw, n_a - 1)
        kc = jnp.minimum(kb_raw, n_c - 1)
        kc_next = jnp.minimum(kb_raw + 1, n_c - 1)
        thr = jnp.where(has_c & (kb_raw < n_c), tau_c - 1, jnp.int32(2 ** 31 - 1))
        sel = keys_scr[slot_c, kc] > thr
        ct_blk = ckvt_ref[kc]
        start = bias_start(kc)
        ref_pt = bound_far if far else bound
        score = jnp.zeros((kb_sz, LANES), F32)
        for g in range(n_groups):
            s_t = idx_dot(kb, g)
            for j in range(2):
                hh = 2 * g + j
                score = score + jnp.maximum(s_t[:, j * LANES:(j + 1) * LANES], 0.0) * w_t[hh:hh + 1, :]
        for g in range(n_groups):
            lt = ltc_scr[:, g * pair:(g + 1) * pair]
            ps = []
            for j in range(2):
                hh = 2 * g + j
                piece = lt[:, j * LANES:(j + 1) * LANES]
                if not far:
                    piece = piece + btile_ref[hh, pl.ds(start, kb_sz), :]
                ps.append(jnp.exp2(jnp.where(sel, piece, NEG_BIG) - ref_pt[:, hh * LANES:(hh + 1) * LANES]))
            ltc_scr[:, g * pair:(g + 1) * pair] = logits(kc_next, g)
            acc_scr[g] += jnp.dot(ct_blk, jnp.concatenate(ps, axis=1).astype(BF16),
                                  preferred_element_type=F32)
        valid = (kb * kb_sz + row_id) <= (qa0 + lane_id)
        keys = jnp.where(valid, _sortable_key(score), INT_MIN)
        keys_scr[slot_a, kb] = keys
        v = keys ^ INT_MIN
        for sub in range(kb_sz // PLANE_KEYS):
            r0 = sub * PLANE_KEYS
            words = _bit_transpose32([v[r0 + 8 * i:r0 + 8 * (i + 1), :] for i in range(32)])
            for bit in range(32):
                planes_scr[bit, kb * (kb_sz // PLANE_KEYS) + sub] = words[31 - bit]

    def block_body(far, it, carry):
        for u in range(BLOCK_UNROLL):
            block_step(it * BLOCK_UNROLL + u, far)
        return carry

    n_far_iters = jnp.maximum(n_c - 2, 0) // BLOCK_UNROLL
    lax.fori_loop(0, n_far_iters, functools.partial(block_body, True), 0)
    lax.fori_loop(n_far_iters, (n_a + BLOCK_UNROLL - 1) // BLOCK_UNROLL,
                  functools.partial(block_body, False), 0)
    n_kb = n_a


    n_planes = planes_scr.shape[1]
    live = n_kb * (kb_sz // PLANE_KEYS)

    def radix_select(width):
        blk_id = lax.broadcasted_iota(I32, (width, 8, LANES), 0)
        cand_scr[:width] = jnp.where(blk_id < live, -1, 0)

        def bit_body(it, carry):
            above, tau_u = carry
            bit = 31 - it
            ones = cand_scr[:width] & planes_scr[bit, :width]
            c1 = jnp.sum(jnp.sum(lax.population_count(ones), axis=0), axis=0, keepdims=True)
            take = (above + c1) >= topk
            cand_scr[:width] = jnp.where(take, ones, cand_scr[:width] ^ ones)
            above = jnp.where(take, above, above + c1)
            tau_u = jnp.where(take, tau_u | (jnp.int32(1) << bit), tau_u)
            return above, tau_u

        zero = jnp.zeros((1, LANES), I32)
        above, tau_u = lax.fori_loop(0, 32, bit_body, (zero, zero))
        n_eq = jnp.sum(jnp.sum(lax.population_count(cand_scr[:width]), axis=0), axis=0, keepdims=True)
        return above, tau_u, n_eq

    widths = [n_planes * (i + 1) // 4 for i in range(4)]
    select = functools.partial(radix_select, widths[-1])
    for width in reversed(widths[:-1]):
        select = functools.partial(lax.cond, live <= width, functools.partial(radix_select, width), select)
    n_gt, tau_u, n_eq = select()
    tau = tau_u ^ INT_MIN

    need = topk - n_gt
    overflow = n_eq > need
    seq_bits = max(1, (n_blocks * kb_sz - 1).bit_length())

    @pl.when(jnp.max(jnp.where(overflow, 1, 0)) > 0)
    def _():
        def count_ties_before(trial):
            def body(kb, acc):
                hit = jnp.where((keys_scr[slot_a, kb] == tau) & ((kb * kb_sz + row_id) < trial), 1, 0)
                return acc + jnp.sum(hit.reshape(kb_sz // 8, 8, LANES), axis=0)
            acc = lax.fori_loop(0, n_kb, body, jnp.zeros((8, LANES), I32))
            return jnp.sum(acc, axis=0, keepdims=True)

        def idx_body(it, jc):
            trial = jc | (jnp.int32(1) << (seq_bits - 1 - it))
            return jnp.where(count_ties_before(trial) < need, trial, jc)

        j_cut = lax.fori_loop(0, seq_bits, idx_body, jnp.zeros((1, LANES), I32))

        def demote_body(kb, carry):
            k = keys_scr[slot_a, kb]
            drop = overflow & (k == tau) & ((kb * kb_sz + row_id) > j_cut)
            keys_scr[slot_a, kb] = jnp.where(drop, INT_MIN, k)
            return carry

        lax.fori_loop(0, n_kb, demote_body, 0)

    tau_scr[slot_a] = jnp.maximum(tau, INT_MIN + 1)

    l_min = jnp.min(jnp.concatenate([acc_scr[g, D_LATENT:D_LATENT + 1, :] for g in range(n_groups)], axis=1))

    @pl.when(has_c & jnp.logical_not(l_min >= 2.0 ** -80))
    def _():
        m_scr[...] = jnp.full(m_scr.shape, NEG_BIG, F32)
        acc_scr[...] = jnp.zeros(acc_scr.shape, F32)

        def exact_body(kb, carry):
            sel = keys_scr[slot_c, kb] >= tau_c
            ct_blk = ckvt_ref[kb]
            start = bias_start(kb)
            for g in range(n_groups):
                lt = logits(kb, g)
                ps, alphas = [], []
                for j in range(2):
                    hh = 2 * g + j
                    sl = slice(hh * LANES, (hh + 1) * LANES)
                    piece = lt[:, j * LANES:(j + 1) * LANES] + btile_ref[hh, pl.ds(start, kb_sz), :]
                    masked = jnp.where(sel, piece, NEG_BIG)
                    m_old = m_scr[:, sl]
                    m_new = jnp.maximum(m_old, jnp.max(masked, axis=0, keepdims=True))
                    m_scr[:, sl] = m_new
                    alphas.append(jnp.exp2(m_old - m_new))
                    ps.append(jnp.exp2(masked - m_new))
                pv = jnp.dot(ct_blk, jnp.concatenate(ps, axis=1).astype(BF16), preferred_element_type=F32)
                acc_scr[g] = jnp.concatenate(alphas, axis=1) * acc_scr[g] + pv
            return carry

        lax.fori_loop(0, n_c, exact_body, 0)

    @pl.when(has_c)
    def _():
        ys = []
        for hh in range(N_HEADS_A):
            acc_h = acc_scr[hh // 2, :, (hh % 2) * LANES:(hh % 2 + 1) * LANES]
            o_h = acc_h[:D_LATENT] * (1.0 / acc_h[D_LATENT:D_LATENT + 1])
            ys.append(jnp.dot(wuvt_ref[hh], o_h.astype(BF16), preferred_element_type=F32))
        y_t = jnp.concatenate(ys, axis=0)
        out_ref[...] = y_t.T.astype(BF16)


def _dsa(q_idx, q_abs, w_t, k_idx, ckv, ckv_t, btile, bmax, wuv_t, *, batch, seq):
    t = batch * seq
    nqb = seq // Q_BLOCK
    nkb = seq // KEY_BLOCK
    topk = min(TOPK_MAX, seq // 4)
    k_idx3 = k_idx.reshape(t // KEY_BLOCK, KEY_BLOCK, LANES)
    ckv3 = ckv.reshape(t // KEY_BLOCK, KEY_BLOCK, D_LATENT)
    per_batch = lambda shape: pl.BlockSpec(shape, lambda b, q: (b,) + (0,) * (len(shape) - 1),
                                           pipeline_mode=pl.Buffered(1))
    scored = lambda b, s: b * nqb + jnp.minimum(s, nqb - 1)
    attended = lambda b, s: b * nqb + jnp.maximum(s - 1, 0)
    return pl.pallas_call(
        functools.partial(_dsa_kernel, topk=topk, n_qb=nqb),
        out_shape=jax.ShapeDtypeStruct((t, W_A), BF16),
        grid=(batch, nqb + 1),
        in_specs=[pl.BlockSpec((1, N_HEADS_IDX, Q_BLOCK, LANES), lambda b, s: (scored(b, s), 0, 0, 0)),
                  pl.BlockSpec((1, N_HEADS_A, Q_BLOCK, D_LATENT), lambda b, s: (attended(b, s), 0, 0, 0)),
                  pl.BlockSpec((N_HEADS_IDX, Q_BLOCK), lambda b, s: (0, scored(b, s))),
                  per_batch((nkb, KEY_BLOCK, LANES)),
                  per_batch((nkb, KEY_BLOCK, D_LATENT)),
                  per_batch((nkb, CKVT_ROWS, KEY_BLOCK)),
                  _resident((N_HEADS_A, BIAS_ROWS, LANES)),
                  _resident((1, N_HEADS_A * LANES)),
                  _resident((N_HEADS_A, HEAD_DIM_A, D_LATENT))],
        out_specs=pl.BlockSpec((Q_BLOCK, W_A), lambda b, s: (attended(b, s), 0)),
        scratch_shapes=[pltpu.VMEM((2, nkb, KEY_BLOCK, LANES), I32),
                        pltpu.VMEM((32, seq // PLANE_KEYS + 1, 8, LANES), I32),
                        pltpu.VMEM((seq // PLANE_KEYS + 1, 8, LANES), I32),
                        pltpu.VMEM((2, 1, LANES), I32),
                        pltpu.VMEM((N_HEADS_A // 2, CKVT_ROWS, 2 * LANES), F32),
                        pltpu.VMEM((1, N_HEADS_A * LANES), F32),
                        pltpu.VMEM((KEY_BLOCK, (N_HEADS_A + 1) * LANES), F32),
                        pltpu.VMEM((1, 1), F32),
                        pltpu.VMEM((N_HEADS_IDX // 2, LANES, 2 * LANES), BF16),
                        pltpu.VMEM((N_HEADS_A // 2, D_LATENT, 2 * LANES), BF16)],
        compiler_params=pltpu.CompilerParams(dimension_semantics=("arbitrary", "arbitrary"),
                                             vmem_limit_bytes=VMEM_LIMIT),
        name="dsa",
    )(q_idx, q_abs, w_t, k_idx3, ckv3, ckv_t, btile, bmax, wuv_t)


def _mlstm_kernel(qk_ref, v_ref, o_ref, ift_ref, gbt_ref, hn_ref,
                  out_ref, cx_scr, m_scr, *, chunk, n_batch):
    L = chunk

    @pl.when(pl.program_id(0) == 0)
    def _():
        cx_scr[...] = jnp.zeros(cx_scr.shape, F32)
        m_scr[...] = jnp.zeros(m_scr.shape, F32)

    rr = lax.broadcasted_iota(I32, (L, L), 0)
    cc = lax.broadcasted_iota(I32, (L, L), 1)
    causal = cc <= rr
    triu = jnp.where(rr <= cc, 1.0, 0.0).astype(BF16)
    lane = lax.broadcasted_iota(I32, (8, L), 1)
    ones_col = jnp.where(lax.broadcasted_iota(I32, (L, HEAD_DIM_M), 1) == 0, 1.0, 0.0).astype(BF16)
    for bi in range(n_batch):
        _mlstm_chunk(qk_ref.at[bi], v_ref.at[bi], o_ref.at[bi], ift_ref.at[bi], gbt_ref, hn_ref,
                     out_ref.at[bi], cx_scr.at[bi], m_scr.at[bi], causal, triu, lane, ones_col, L)


def _mlstm_chunk(qk_ref, v_ref, o_ref, ift_ref, gbt_ref, hn_ref, out_ref, cx_scr, m_scr,
                 causal, triu, lane, ones_col, L):
    g_t = ift_ref[...] + gbt_ref[...]
    b_all = sum(jnp.dot(piece, triu, preferred_element_type=F32) for piece in _split3(_log_sigmoid(g_t)))
    b8 = pltpu.roll(b_all, N_HEADS_M, axis=0)
    a8 = g_t - b8
    cm = a8
    shift = 1
    while shift < L:
        cm = jnp.maximum(cm, jnp.where(lane >= shift, pltpu.roll(cm, shift, axis=1), NEG_BIG))
        shift *= 2
    m_prev = m_scr[...]
    mx = jnp.maximum(m_prev, cm)
    mx_last = mx[:, L - 1:L]
    decay8 = jnp.exp(m_prev - mx_last)
    m_scr[...] = b8[:, L - 1:L] + mx_last
    rows = jnp.concatenate([-mx,
                            jnp.exp(m_prev - mx),
                            jnp.exp(-(b8 + mx)),
                            jnp.exp(a8 - mx_last),
                            jnp.zeros((LANES - 32, L), F32)], axis=0)
    cols = rows.T

    o_gate = _sigmoid(o_ref[...])
    for hh in range(N_HEADS_M):
        hs = slice(hh * HEAD_DIM_M, (hh + 1) * HEAD_DIM_M)
        qb16 = qk_ref[:, hs]
        kb16 = qk_ref[:, W_M + hh * HEAD_DIM_M:W_M + (hh + 1) * HEAD_DIM_M]
        v_ext = jnp.concatenate([v_ref[:, hs], ones_col], axis=1)
        u_c = cols[:, hh:hh + 1]
        w_inter = cols[:, 8 + hh:9 + hh]
        em_c = cols[:, 16 + hh:17 + hh]
        wgt_c = cols[:, 24 + hh:25 + hh]
        cx_prev = cx_scr[hh]

        d_mat = jnp.where(causal, jnp.exp(u_c + a8[hh:hh + 1, :]), 0.0)
        s = lax.dot_general(qb16, kb16, (((1,), (1,)), ((), ())), preferred_element_type=F32) * d_mat
        intra = jnp.dot(s.astype(BF16), v_ext, preferred_element_type=F32)
        inter = jnp.dot(qb16, cx_prev.astype(BF16), preferred_element_type=F32)
        both = w_inter * inter + intra
        num = both[:, :HEAD_DIM_M]
        den = both[:, HEAD_DIM_M:HEAD_DIM_M + 1]
        hval = num / jnp.maximum(jnp.abs(den), em_c)

        kw = kb16.astype(F32) * wgt_c
        cx_scr[hh] = decay8[hh:hh + 1] * cx_prev + jnp.dot(kw.T.astype(BF16), v_ext,
                                                           preferred_element_type=F32)

        mu = jnp.mean(hval, axis=1, keepdims=True)
        cen = hval - mu
        var = jnp.mean(cen * cen, axis=1, keepdims=True)
        hn = cen * lax.rsqrt(var + EPS) * hn_ref[:, hs]
        out_ref[:, hs] = (hn * o_gate[:, hs]).astype(BF16)


def _mlstm(qk, v, o_pre, ift, gate_bias, head_norm, *, batch, seq):
    t = batch * seq
    L = MLSTM_CHUNK
    nc = seq // L
    gbt = jnp.broadcast_to(gate_bias.reshape(2 * N_HEADS_M, 1), (2 * N_HEADS_M, L))
    ift_b = ift.reshape(2 * N_HEADS_M, batch, seq).transpose(1, 0, 2)
    row = lambda w: pl.BlockSpec((batch, L, w), lambda c: (0, c, 0))
    out = pl.pallas_call(
        functools.partial(_mlstm_kernel, chunk=L, n_batch=batch),
        out_shape=jax.ShapeDtypeStruct((batch, seq, W_M), BF16),
        grid=(nc,),
        in_specs=[row(2 * W_M), row(W_M), row(W_M),
                  pl.BlockSpec((batch, 2 * N_HEADS_M, L), lambda c: (0, 0, c)),
                  _resident((2 * N_HEADS_M, L)), _resident((1, W_M))],
        out_specs=row(W_M),
        scratch_shapes=[pltpu.VMEM((batch, N_HEADS_M, HEAD_DIM_M, 2 * HEAD_DIM_M), F32),
                        pltpu.VMEM((batch, 8, 1), F32)],
        compiler_params=pltpu.CompilerParams(dimension_semantics=("arbitrary",),
                                             vmem_limit_bytes=VMEM_LIMIT),
        name="mlstm",
    )(qk.reshape(batch, seq, 2 * W_M), v.reshape(batch, seq, W_M), o_pre.reshape(batch, seq, W_M),
      ift_b, gbt, head_norm.reshape(1, -1))
    return out.reshape(t, W_M)


def kernel(x, c, ada_w, ada_b, ffn1_norm, ffn1_w1, ffn1_w3, ffn1_w2, mix_norm, w_in, conv_w, conv_b,
           kv_norm, w_uk, w_uv, mlstm_gate_bias, mlstm_head_norm, rel_bias, w_branch_attn,
           w_branch_mlstm, w_out, ffn2_norm, ffn2_w1, ffn2_w3, ffn2_w2, final_norm):
    batch, seq, d = x.shape
    depth = ada_w.shape[0]
    assert seq % max(FFN_TM, MIX_TM, MLSTM_CHUNK, KEY_BLOCK) == 0
    t = batch * seq
    xf = x.reshape(t, d)
    btile, bmax = _bias_tiles(rel_bias)
    for l in range(depth):
        mod = _adaln(c, ada_w[l], ada_b[l]).reshape(batch, 9, 1, d)
        sh1, sc1, g1, sh2, sc2, g2, sh3, sc3, g3 = [mod[:, n] for n in range(9)]
        xf = _ffn(xf, ffn1_norm[l], sh1, sc1, g1, ffn1_w1[l], ffn1_w3[l], ffn1_w2[l], final_norm,
                  seq=seq, final_norm=False)
        wuk_hdc = w_uk[l].transpose(0, 2, 1).reshape(N_HEADS_A // 2, 2, HEAD_DIM_A, D_LATENT)
        zeros = jnp.zeros_like(wuk_hdc[:, 0])
        wuk_t = jnp.concatenate([jnp.concatenate([wuk_hdc[:, 0], zeros], axis=2),
                                 jnp.concatenate([zeros, wuk_hdc[:, 1]], axis=2)], axis=1).astype(BF16)
        (q_abs, q_idx, k_idx, ckv, ckv_t, w_t, ift, qk_m, v_m, o_pre, gate_a, gate_m) = _mixin(
            xf, mix_norm[l], sh2, sc2, _pack_w_in(w_in[l], d), kv_norm[l], wuk_t, conv_w[l], conv_b[l],
            seq=seq)
        wuv_t = w_uv[l].transpose(0, 2, 1).astype(BF16)
        y_a = _dsa(q_idx, q_abs, w_t, k_idx, ckv, ckv_t, btile, bmax, wuv_t, batch=batch, seq=seq)
        h_m = _mlstm(qk_m, v_m, o_pre, ift, mlstm_gate_bias[l], mlstm_head_norm[l], batch=batch, seq=seq)
        xf = _ffn(xf, ffn2_norm[l], sh3, sc3, g3, ffn2_w1[l], ffn2_w3[l], ffn2_w2[l], final_norm,
                  seq=seq, final_norm=(l == depth - 1),
                  merge=(y_a, h_m, gate_a, gate_m, g2, w_branch_attn[l], w_branch_mlstm[l], w_out[l]))
    return xf.reshape(batch, seq, d)
```

```python
import functools
import math

import jax
import jax.numpy as jnp
from jax import lax
from jax.experimental import pallas as pl
from jax.experimental.pallas import tpu as pltpu

F32 = jnp.float32
BF16 = jnp.bfloat16
I32 = jnp.int32

LANES = 128
VMEM_LIMIT = 56 * 1024 * 1024

N_HEADS_A = 8
HEAD_DIM_A = 64
D_LATENT = 256
N_HEADS_IDX = 8
HEAD_DIM_IDX = 64
TOPK_MAX = 256
Q_BLOCK = 128
N_BUCKETS = 32
MAX_DISTANCE = 128
N_HEADS_M = 4
HEAD_DIM_M = 128
CONV_WIDTH = 4
EPS = 1e-6
IDX_SCALE = (N_HEADS_IDX ** -0.5) * (HEAD_DIM_IDX ** -0.5)
W_A = N_HEADS_A * HEAD_DIM_A
W_M = N_HEADS_M * HEAD_DIM_M

FFN_TM = 512
FFN_CHUNK = 256
MIX_TM = 512
KEY_BLOCK = 256
BLOCK_UNROLL = 3
PLANE_KEYS = 256
MLSTM_CHUNK = 256
NEG_BIG = -1e30
INT_MIN = -2 ** 31

BIAS_PAD = 2 * KEY_BLOCK - Q_BLOCK
BIAS_ROWS = KEY_BLOCK + BIAS_PAD
CKVT_ROWS = D_LATENT + 16
LOG2E = math.log2(math.e)


def _sigmoid(x):
    return 1.0 / (1.0 + jnp.exp(-x))


def _log_sigmoid(x):
    return jnp.minimum(x, 0.0) - jnp.log(1.0 + jnp.exp(-jnp.abs(x)))


def _rms_norm(x, gain):
    ms = jnp.mean(x * x, axis=-1, keepdims=True)
    return x * lax.rsqrt(ms + EPS) * gain


def _split3(x):
    hi = x.astype(BF16)
    r1 = x - hi.astype(F32)
    mid = r1.astype(BF16)
    lo = (r1 - mid.astype(F32)).astype(BF16)
    return hi, mid, lo


def _resident(shape):
    nd = len(shape)
    return pl.BlockSpec(shape, lambda *_: (0,) * nd, pipeline_mode=pl.Buffered(1))


def _adaln_kernel(c_ref, w_ref, b_ref, o_ref):
    c = c_ref[...]
    cond = c * _sigmoid(c)
    o_ref[...] = jnp.dot(cond.astype(BF16), w_ref[...].astype(BF16),
                         preferred_element_type=F32) + b_ref[...]


def _adaln(c, ada_w, ada_b):
    b, d = c.shape
    n = ada_w.shape[1]
    rows = 8
    c_pad = jnp.zeros((rows, d), F32).at[:b].set(c)
    tn = 1024
    out = pl.pallas_call(
        _adaln_kernel,
        out_shape=jax.ShapeDtypeStruct((rows, n), F32),
        grid=(n // tn,),
        in_specs=[pl.BlockSpec((rows, d), lambda j: (0, 0)),
                  pl.BlockSpec((d, tn), lambda j: (0, j)),
                  pl.BlockSpec((1, tn), lambda j: (0, j))],
        out_specs=pl.BlockSpec((rows, tn), lambda j: (0, j)),
        compiler_params=pltpu.CompilerParams(dimension_semantics=("arbitrary",),
                                             vmem_limit_bytes=VMEM_LIMIT),
        name="adaln",
    )(c_pad, ada_w, ada_b.reshape(1, n))
    return out[:b]


def _t5_bucket(dist):
    n = jnp.maximum(dist, 0)
    max_exact = N_BUCKETS // 2
    nf = jnp.maximum(n, 1).astype(F32)
    large = max_exact + (jnp.log(nf / max_exact) / math.log(MAX_DISTANCE / max_exact)
                         * (N_BUCKETS - max_exact)).astype(I32)
    large = jnp.minimum(large, N_BUCKETS - 1)
    return jnp.where(n < max_exact, n, large)


def _bias_kernel(rel_ref, tile_ref, max_ref):
    r = lax.broadcasted_iota(I32, (BIAS_ROWS, LANES), 0)
    i = lax.broadcasted_iota(I32, (BIAS_ROWS, LANES), 1)
    bucket = _t5_bucket(i - r + BIAS_PAD)
    for h in range(N_HEADS_A):
        acc = jnp.zeros((BIAS_ROWS, LANES), F32)
        top = rel_ref[0, h] * LOG2E
        for bkt in range(N_BUCKETS):
            val = rel_ref[bkt, h] * LOG2E
            acc = jnp.where(bucket == bkt, val, acc)
            top = jnp.maximum(top, val)
        tile_ref[h] = acc
        max_ref[:, h * LANES:(h + 1) * LANES] = jnp.full((1, LANES), top, F32)


def _bias_tiles(rel_bias):
    return pl.pallas_call(
        _bias_kernel,
        out_shape=(jax.ShapeDtypeStruct((N_HEADS_A, BIAS_ROWS, LANES), F32),
                   jax.ShapeDtypeStruct((1, N_HEADS_A * LANES), F32)),
        in_specs=[pl.BlockSpec(memory_space=pltpu.SMEM)],
        out_specs=(pl.BlockSpec(memory_space=pltpu.VMEM), pl.BlockSpec(memory_space=pltpu.VMEM)),
        name="bias_tiles",
    )(rel_bias)


def _ffn_kernel(*refs, n_chunks, final_norm, merge):
    if merge:
        (x_ref, ya_ref, hm_ref, ga_ref, gm_ref, gmix_ref, wa_ref, wm_ref, wo_ref), refs = refs[:9], refs[9:]
    else:
        x_ref, refs = refs[0], refs[1:]
    gain_ref, sh_ref, sc_ref, g_ref, w1_ref, w3_ref, w2_ref, fin_ref, o_ref, h_scr, acc_scr = refs
    x = x_ref[...]
    if merge:
        pa = jnp.dot(ya_ref[...], wa_ref[...], preferred_element_type=F32)
        pm = jnp.dot(hm_ref[...], wm_ref[...], preferred_element_type=F32)
        merged = _sigmoid(ga_ref[...]) * pa + _sigmoid(gm_ref[...]) * pm
        x = x + gmix_ref[0] * jnp.dot(merged.astype(BF16), wo_ref[...], preferred_element_type=F32)
    h = _rms_norm(x, gain_ref[...]) * (1.0 + sc_ref[0]) + sh_ref[0]
    h_scr[...] = h.astype(BF16)
    for j in range(n_chunks):
        hb = h_scr[...]
        cols = slice(j * FFN_CHUNK, (j + 1) * FFN_CHUNK)
        u1 = jnp.dot(hb, w1_ref[:, cols], preferred_element_type=F32)
        u3 = jnp.dot(hb, w3_ref[:, cols], preferred_element_type=F32)
        a = (u1 * _sigmoid(u1)) * u3
        part = jnp.dot(a.astype(BF16), w2_ref[j], preferred_element_type=F32)
        if j == 0:
            acc_scr[...] = part
        else:
            acc_scr[...] += part
    out = x + (0.5 * g_ref[0]) * acc_scr[...]
    if final_norm:
        out = _rms_norm(out, fin_ref[...])
    o_ref[...] = out


def _ffn(x, gain, sh, sc, g, w1, w3, w2, fin, *, seq, final_norm, merge=None):
    t, d = x.shape
    dff = w1.shape[1]
    nch = dff // FFN_CHUNK
    w1c = w1.astype(BF16)
    w3c = w3.astype(BF16)
    w2c = w2.astype(BF16).reshape(nch, FFN_CHUNK, d)
    tm = FFN_TM
    per_b = seq // tm
    row = lambda w: pl.BlockSpec((tm, w), lambda i: (i, 0))
    mod_spec = pl.BlockSpec((1, 1, d), lambda i: (i // per_b, 0, 0))
    merge_specs, merge_args = [], []
    if merge is not None:
        y_a, h_m, gate_a, gate_m, g_mix, w_a, w_m, w_o = merge
        merge_specs = [row(W_A), row(W_M), row(d), row(d), mod_spec,
                       _resident((W_A, d)), _resident((W_M, d)), _resident((d, d))]
        merge_args = [y_a, h_m, gate_a, gate_m, g_mix, w_a.astype(BF16), w_m.astype(BF16), w_o.astype(BF16)]
    return pl.pallas_call(
        functools.partial(_ffn_kernel, n_chunks=nch, final_norm=final_norm, merge=merge is not None),
        out_shape=jax.ShapeDtypeStruct((t, d), F32),
        grid=(t // tm,),
        in_specs=[row(d)] + merge_specs + [
                  _resident((1, d)), mod_spec, mod_spec, mod_spec,
                  _resident((d, dff)), _resident((d, dff)),
                  _resident((nch, FFN_CHUNK, d)), _resident((1, d))],
        out_specs=row(d),
        scratch_shapes=[pltpu.VMEM((tm, d), BF16), pltpu.VMEM((tm, d), F32)],
        compiler_params=pltpu.CompilerParams(dimension_semantics=("arbitrary",),
                                             vmem_limit_bytes=VMEM_LIMIT),
        name="ffn_final" if final_norm else "ffn",
    )(x, *merge_args, gain.reshape(1, d), sh, sc, g, w1c, w3c, w2c, fin.reshape(1, d))


_C_QA = 0
_C_CKV = _C_QA + W_A
_C_QI = _C_CKV + D_LATENT
_C_KI = _C_QI + N_HEADS_IDX * LANES
_C_SM = _C_KI + LANES
_C_QK = _C_SM + LANES
_C_V = _C_QK + 2 * W_M
_C_O = _C_V + W_M
_C_GA = _C_O + W_M
_C_GM = _C_GA + 1024
_C_END = _C_GM + 1024
_SM_W = 0
_SM_I = N_HEADS_IDX


def _pack_w_in(w_in, d_model):
    splits = (W_A, D_LATENT, N_HEADS_IDX * HEAD_DIM_IDX, HEAD_DIM_IDX, N_HEADS_IDX,
              W_M, W_M, W_M, N_HEADS_M, N_HEADS_M, W_M, d_model, d_model)
    offs = [0]
    for s in splits:
        offs.append(offs[-1] + s)
    (q_a, c_kv, q_i, k_i, w_i, q_m, k_m, v_m, i_p, f_p, o_p, g_a, g_m) = [
        w_in[:, offs[n]:offs[n + 1]] for n in range(len(splits))]
    d = w_in.shape[0]

    def pad_heads(w, nh, hd):
        w = w.reshape(d, nh, hd)
        return jnp.pad(w, ((0, 0), (0, 0), (0, LANES - hd))).reshape(d, nh * LANES)

    small = jnp.concatenate([w_i, i_p, f_p], axis=1)
    small = jnp.pad(small, ((0, 0), (0, LANES - small.shape[1])))
    packed = jnp.concatenate([
        q_a, c_kv, pad_heads(q_i, N_HEADS_IDX, HEAD_DIM_IDX),
        jnp.pad(k_i, ((0, 0), (0, LANES - HEAD_DIM_IDX))), small, q_m, k_m, v_m, o_p, g_a, g_m], axis=1)
    assert packed.shape[1] == _C_END
    return packed.astype(BF16)


def _mixin_kernel(x_ref, gain_ref, sh_ref, sc_ref, w_ref, kvn_ref, wuk_ref, cw_ref, cb_ref,
                  qabs_ref, qidx_ref, kidx_ref, ckv_ref, ckvt_ref, wt_ref, ift_ref,
                  qk_ref, v_ref, o_ref, ga_ref, gm_ref, h_scr, xe_scr, *, tm, tiles_per_seq):
    nqb = tm // Q_BLOCK

    @pl.when(pl.program_id(0) % tiles_per_seq == 0)
    def _():
        xe_scr[:8] = jnp.zeros((8, xe_scr.shape[1]), F32)

    x = x_ref[...]
    h = _rms_norm(x, gain_ref[...]) * (1.0 + sc_ref[0]) + sh_ref[0]
    h_scr[...] = h.astype(BF16)

    def proj(lo, hi):
        return jnp.dot(h_scr[...], w_ref[:, lo:hi], preferred_element_type=F32)

    qa = proj(_C_QA, _C_CKV)
    scale = HEAD_DIM_A ** -0.5 * LOG2E
    for g in range(N_HEADS_A // 2):
        q_pair = qa[:, g * LANES:(g + 1) * LANES].astype(BF16)
        q_abs = jnp.dot(q_pair, wuk_ref[g], preferred_element_type=F32) * scale
        for j in range(2):
            qabs_ref[:, 2 * g + j] = (q_abs[:, j * D_LATENT:(j + 1) * D_LATENT]
                                      .astype(BF16).reshape(nqb, Q_BLOCK, D_LATENT))
    ckv = _rms_norm(proj(_C_CKV, _C_QI), kvn_ref[...])
    ckv_ref[...] = ckv.astype(BF16)
    ckv_t = ckv.T
    ones_row = jnp.where(lax.broadcasted_iota(I32, (CKVT_ROWS - D_LATENT, KEY_BLOCK), 0) == 0, 1.0, 0.0)
    for j in range(tm // KEY_BLOCK):
        ckvt_ref[j, :D_LATENT] = ckv_t[:, j * KEY_BLOCK:(j + 1) * KEY_BLOCK].astype(BF16)
        ckvt_ref[j, D_LATENT:] = ones_row.astype(BF16)
    qi = proj(_C_QI, _C_KI)
    for hh in range(N_HEADS_IDX):
        qidx_ref[:, hh] = qi[:, hh * LANES:(hh + 1) * LANES].astype(BF16).reshape(nqb, Q_BLOCK, LANES)
    kidx_ref[...] = proj(_C_KI, _C_SM).astype(BF16)
    small_t = proj(_C_SM, _C_QK).T
    wt_ref[...] = small_t[_SM_W:_SM_W + N_HEADS_IDX] * IDX_SCALE
    ift_ref[...] = small_t[_SM_I:_SM_I + 2 * N_HEADS_M]
    xe_scr[8:] = proj(_C_QK, _C_V)
    xe = xe_scr[...]
    xq = xe[8:]
    conv = xq * cw_ref[CONV_WIDTH - 1:CONV_WIDTH, :] + cb_ref[...]
    for d in range(1, CONV_WIDTH):
        conv = conv + pltpu.roll(xe, d, axis=0)[8:] * cw_ref[CONV_WIDTH - 1 - d:CONV_WIDTH - d, :]
    xe_scr[:8] = xe_scr[tm:]
    qk = conv * _sigmoid(conv)
    qk_ref[:, :W_M] = qk[:, :W_M].astype(BF16)
    qk_ref[:, W_M:] = (qk[:, W_M:] * (HEAD_DIM_M ** -0.5)).astype(BF16)
    v_ref[...] = proj(_C_V, _C_O).astype(BF16)
    o_ref[...] = proj(_C_O, _C_GA)
    ga_ref[...] = proj(_C_GA, _C_GM)
    gm_ref[...] = proj(_C_GM, _C_END)


def _mixin(x, gain, sh, sc, w_packed, kv_norm, wuk_t, conv_w, conv_b, *, seq):
    t, d = x.shape
    tm = MIX_TM
    per_b = seq // tm
    nqb = tm // Q_BLOCK
    row = lambda w: pl.BlockSpec((tm, w), lambda i: (i, 0))
    mod_spec = pl.BlockSpec((1, 1, d), lambda i: (i // per_b, 0, 0))
    out_shape = (
        jax.ShapeDtypeStruct((t // Q_BLOCK, N_HEADS_A, Q_BLOCK, D_LATENT), BF16),
        jax.ShapeDtypeStruct((t // Q_BLOCK, N_HEADS_IDX, Q_BLOCK, LANES), BF16),
        jax.ShapeDtypeStruct((t, LANES), BF16),
        jax.ShapeDtypeStruct((t, D_LATENT), BF16),
        jax.ShapeDtypeStruct((t // KEY_BLOCK, CKVT_ROWS, KEY_BLOCK), BF16),
        jax.ShapeDtypeStruct((N_HEADS_IDX, t), F32),
        jax.ShapeDtypeStruct((2 * N_HEADS_M, t), F32),
        jax.ShapeDtypeStruct((t, 2 * W_M), BF16),
        jax.ShapeDtypeStruct((t, W_M), BF16),
        jax.ShapeDtypeStruct((t, W_M), F32),
        jax.ShapeDtypeStruct((t, d), F32),
        jax.ShapeDtypeStruct((t, d), F32),
    )
    out_specs = (
        pl.BlockSpec((nqb, N_HEADS_A, Q_BLOCK, D_LATENT), lambda i: (i, 0, 0, 0)),
        pl.BlockSpec((nqb, N_HEADS_IDX, Q_BLOCK, LANES), lambda i: (i, 0, 0, 0)),
        row(LANES), row(D_LATENT),
        pl.BlockSpec((tm // KEY_BLOCK, CKVT_ROWS, KEY_BLOCK), lambda i: (i, 0, 0)),
        pl.BlockSpec((N_HEADS_IDX, tm), lambda i: (0, i)),
        pl.BlockSpec((2 * N_HEADS_M, tm), lambda i: (0, i)),
        row(2 * W_M), row(W_M), row(W_M), row(d), row(d),
    )
    return pl.pallas_call(
        functools.partial(_mixin_kernel, tm=tm, tiles_per_seq=per_b),
        out_shape=out_shape,
        grid=(t // tm,),
        in_specs=[pl.BlockSpec((tm, d), lambda i: (i, 0)), _resident((1, d)), mod_spec, mod_spec,
                  _resident((d, _C_END)), _resident((1, D_LATENT)),
                  _resident((N_HEADS_A // 2, LANES, 2 * D_LATENT)),
                  _resident((CONV_WIDTH, 2 * W_M)), _resident((1, 2 * W_M))],
        out_specs=out_specs,
        scratch_shapes=[pltpu.VMEM((tm, d), BF16), pltpu.VMEM((tm + 8, 2 * W_M), F32)],
        compiler_params=pltpu.CompilerParams(dimension_semantics=("arbitrary",),
                                             vmem_limit_bytes=VMEM_LIMIT),
        name="mixin",
    )(x, gain.reshape(1, d), sh, sc, w_packed, kv_norm.reshape(1, D_LATENT), wuk_t,
      conv_w, conv_b.reshape(1, -1))


def _sortable_key(score):
    bits = pltpu.bitcast(score, I32)
    bits = jnp.where(bits == INT_MIN, 0, bits)
    return jnp.where(bits < 0, bits ^ 0x7FFFFFFF, bits)


def _bit_transpose32(words):
    v = list(words)
    j, m = 16, 0x0000FFFF
    while j:
        k = 0
        while k < 32:
            t = (v[k] ^ lax.shift_right_logical(v[k + j], jnp.int32(j))) & m
            v[k] = v[k] ^ t
            v[k + j] = v[k + j] ^ (t << j)
            k = (k + j + 1) & ~j
        j >>= 1
        m = (m ^ (m << j)) & 0x7FFFFFFF
    return v


def _dsa_kernel(qidx_ref, qabs_ref, wt_ref, kidx_ref, ckv_ref, ckvt_ref, btile_ref, bmax_ref, wuvt_ref,
                out_ref, keys_scr, planes_scr, cand_scr, tau_scr, acc_scr, m_scr, ltc_scr, kmax_scr,
                qta_scr, qtc_scr, *, topk, n_qb):
    kb_sz = KEY_BLOCK
    step = pl.program_id(1)
    has_c = step >= 1
    qa = jnp.minimum(step, n_qb - 1)
    qc = jnp.maximum(step - 1, 0)
    slot_a = step & 1
    slot_c = 1 - slot_a
    n_a = qa // (kb_sz // Q_BLOCK) + 1
    n_c = qc // (kb_sz // Q_BLOCK) + 1
    qa0 = qa * Q_BLOCK
    qc0 = qc * Q_BLOCK
    row_id = lax.broadcasted_iota(I32, (kb_sz, LANES), 0)
    lane_id = lax.broadcasted_iota(I32, (kb_sz, LANES), 1)
    n_blocks = keys_scr.shape[1]
    n_groups = N_HEADS_A // 2
    pair = 2 * LANES
    ones8 = jnp.ones((8, D_LATENT), BF16)

    @pl.when(step == 0)
    def _():
        keys_scr[1, 0] = jnp.full((kb_sz, LANES), INT_MIN, I32)
        tau_scr[1] = jnp.zeros((1, LANES), I32)
        planes_scr[...] = jnp.zeros(planes_scr.shape, I32)

        def kn_body(kb, mx):
            c = ckv_ref[kb].astype(F32)
            n2 = lax.dot_general(ones8, (c * c).astype(BF16), (((1,), (1,)), ((), ())),
                                 preferred_element_type=F32)
            return jnp.maximum(mx, n2[0:1])
        mx = lax.fori_loop(0, n_blocks, kn_body, jnp.zeros((1, kb_sz), F32))
        kmax_scr[...] = jnp.max(mx, axis=1, keepdims=True)

    for g in range(n_groups):
        qi_g = qidx_ref[0, 2 * g:2 * g + 2].reshape(2 * Q_BLOCK, LANES).astype(F32)
        qta_scr[g] = qi_g.T.astype(BF16)
        qa_g = qabs_ref[0, 2 * g:2 * g + 2].reshape(2 * Q_BLOCK, D_LATENT).astype(F32)
        qtc_scr[g] = qa_g.T.astype(BF16)

    def idx_dot(kb, g):
        return jnp.dot(kidx_ref[kb], qta_scr[g], preferred_element_type=F32)

    def logits(kb, g):
        return jnp.dot(ckv_ref[kb], qtc_scr[g], preferred_element_type=F32)

    def bias_start(kb):
        delta = jnp.minimum(qc0 - kb * kb_sz, BIAS_PAD)
        return pl.multiple_of(BIAS_PAD - delta, LANES)

    qn2 = []
    for g in range(n_groups):
        q_g = qabs_ref[0, 2 * g:2 * g + 2].reshape(2 * Q_BLOCK, D_LATENT).astype(F32)
        qn2.append(lax.dot_general(ones8, (q_g * q_g).astype(BF16), (((1,), (1,)), ((), ())),
                                   preferred_element_type=F32)[0:1])
    bound = jnp.sqrt(jnp.concatenate(qn2, axis=1) * kmax_scr[...]) * 1.02 + bmax_ref[...] + 1e-3
    tau_c = tau_scr[slot_c]
    w_t = wt_ref[...]

    acc_scr[...] = jnp.zeros(acc_scr.shape, F32)
    for g in range(n_groups):
        ltc_scr[:, g * pair:(g + 1) * pair] = logits(0, g)

    bound_far = bound - jnp.concatenate([btile_ref[hh, 0:1, :] for hh in range(N_HEADS_A)], axis=1)

    def block_step(kb_raw, far):
        kb = jnp.minimum(kb_raw, n_a - 1)
        kc = jnp.minimum(kb_raw, n_c - 1)
        kc_next = jnp.minimum(kb_raw + 1, n_c - 1)
        thr = jnp.where(has_c & (kb_raw < n_c), tau_c - 1, jnp.int32(2 ** 31 - 1))
        sel = keys_scr[slot_c, kc] > thr
        ct_blk = ckvt_ref[kc]
        start = bias_start(kc)
        ref_pt = bound_far if far else bound
        score = jnp.zeros((kb_sz, LANES), F32)
        for g in range(n_groups):
            s_t = idx_dot(kb, g)
            for j in range(2):
                hh = 2 * g + j
                score = score + jnp.maximum(s_t[:, j * LANES:(j + 1) * LANES], 0.0) * w_t[hh:hh + 1, :]
        for g in range(n_groups):
            lt = ltc_scr[:, g * pair:(g + 1) * pair]
            ps = []
            for j in range(2):
                hh = 2 * g + j
                piece = lt[:, j * LANES:(j + 1) * LANES]
                if not far:
                    piece = piece + btile_ref[hh, pl.ds(start, kb_sz), :]
                ps.append(jnp.exp2(jnp.where(sel, piece, NEG_BIG) - ref_pt[:, hh * LANES:(hh + 1) * LANES]))
            ltc_scr[:, g * pair:(g + 1) * pair] = logits(kc_next, g)
            acc_scr[g] += jnp.dot(ct_blk, jnp.concatenate(ps, axis=1).astype(BF16),
                                  preferred_element_type=F32)
        valid = (kb * kb_sz + row_id) <= (qa0 + lane_id)
        keys = jnp.where(valid, _sortable_key(score), INT_MIN)
        keys_scr[slot_a, kb] = keys
        v = keys ^ INT_MIN
        for sub in range(kb_sz // PLANE_KEYS):
            r0 = sub * PLANE_KEYS
            words = _bit_transpose32([v[r0 + 8 * i:r0 + 8 * (i + 1), :] for i in range(32)])
            for bit in range(32):
                planes_scr[bit, kb * (kb_sz // PLANE_KEYS) + sub] = words[31 - bit]

    def block_body(far, it, carry):
        for u in range(BLOCK_UNROLL):
            block_step(it * BLOCK_UNROLL + u, far)
        return carry

    n_far_iters = jnp.maximum(n_c - 2, 0) // BLOCK_UNROLL
    lax.fori_loop(0, n_far_iters, functools.partial(block_body, True), 0)
    lax.fori_loop(n_far_iters, (n_a + BLOCK_UNROLL - 1) // BLOCK_UNROLL,
                  functools.partial(block_body, False), 0)
    n_kb = n_a


    n_planes = planes_scr.shape[1]
    live = n_kb * (kb_sz // PLANE_KEYS)

    def radix_select(width):
        blk_id = lax.broadcasted_iota(I32, (width, 8, LANES), 0)
        cand_scr[:width] = jnp.where(blk_id < live, -1, 0)

        def bit_body(it, carry):
            above, tau_u = carry
            bit = 31 - it
            ones = cand_scr[:width] & planes_scr[bit, :width]
            c1 = jnp.sum(jnp.sum(lax.population_count(ones), axis=0), axis=0, keepdims=True)
            take = (above + c1) >= topk
            cand_scr[:width] = jnp.where(take, ones, cand_scr[:width] ^ ones)
            above = jnp.where(take, above, above + c1)
            tau_u = jnp.where(take, tau_u | (jnp.int32(1) << bit), tau_u)
            return above, tau_u

        zero = jnp.zeros((1, LANES), I32)
        above, tau_u = lax.fori_loop(0, 32, bit_body, (zero, zero))
        n_eq = jnp.sum(jnp.sum(lax.population_count(cand_scr[:width]), axis=0), axis=0, keepdims=True)
        return above, tau_u, n_eq

    widths = [n_planes * (i + 1) // 4 for i in range(4)]
    select = functools.partial(radix_select, widths[-1])
    for width in reversed(widths[:-1]):
        select = functools.partial(lax.cond, live <= width, functools.partial(radix_select, width), select)
    n_gt, tau_u, n_eq = select()
    tau = tau_u ^ INT_MIN

    need = topk - n_gt
    overflow = n_eq > need
    seq_bits = max(1, (n_blocks * kb_sz - 1).bit_length())

    @pl.when(jnp.max(jnp.where(overflow, 1, 0)) > 0)
    def _():
        def count_ties_before(trial):
            def body(kb, acc):
                hit = jnp.where((keys_scr[slot_a, kb] == tau) & ((kb * kb_sz + row_id) < trial), 1, 0)
                return acc + jnp.sum(hit.reshape(kb_sz // 8, 8, LANES), axis=0)
            acc = lax.fori_loop(0, n_kb, body, jnp.zeros((8, LANES), I32))
            return jnp.sum(acc, axis=0, keepdims=True)

        def idx_body(it, jc):
            trial = jc | (jnp.int32(1) << (seq_bits - 1 - it))
            return jnp.where(count_ties_before(trial) < need, trial, jc)

        j_cut = lax.fori_loop(0, seq_bits, idx_body, jnp.zeros((1, LANES), I32))

        def demote_body(kb, carry):
            k = keys_scr[slot_a, kb]
            drop = overflow & (k == tau) & ((kb * kb_sz + row_id) > j_cut)
            keys_scr[slot_a, kb] = jnp.where(drop, INT_MIN, k)
            return carry

        lax.fori_loop(0, n_kb, demote_body, 0)

    tau_scr[slot_a] = jnp.maximum(tau, INT_MIN + 1)

    l_min = jnp.min(jnp.concatenate([acc_scr[g, D_LATENT:D_LATENT + 1, :] for g in range(n_groups)], axis=1))

    @pl.when(has_c & jnp.logical_not(l_min >= 2.0 ** -80))
    def _():
        m_scr[...] = jnp.full(m_scr.shape, NEG_BIG, F32)
        acc_scr[...] = jnp.zeros(acc_scr.shape, F32)

        def exact_body(kb, carry):
            sel = keys_scr[slot_c, kb] >= tau_c
            ct_blk = ckvt_ref[kb]
            start = bias_start(kb)
            for g in range(n_groups):
                lt = logits(kb, g)
                ps, alphas = [], []
                for j in range(2):
                    hh = 2 * g + j
                    sl = slice(hh * LANES, (hh + 1) * LANES)
                    piece = lt[:, j * LANES:(j + 1) * LANES] + btile_ref[hh, pl.ds(start, kb_sz), :]
                    masked = jnp.where(sel, piece, NEG_BIG)
                    m_old = m_scr[:, sl]
                    m_new = jnp.maximum(m_old, jnp.max(masked, axis=0, keepdims=True))
                    m_scr[:, sl] = m_new
                    alphas.append(jnp.exp2(m_old - m_new))
                    ps.append(jnp.exp2(masked - m_new))
                pv = jnp.dot(ct_blk, jnp.concatenate(ps, axis=1).astype(BF16), preferred_element_type=F32)
                acc_scr[g] = jnp.concatenate(alphas, axis=1) * acc_scr[g] + pv
            return carry

        lax.fori_loop(0, n_c, exact_body, 0)

    @pl.when(has_c)
    def _():
        ys = []
        for hh in range(N_HEADS_A):
            acc_h = acc_scr[hh // 2, :, (hh % 2) * LANES:(hh % 2 + 1) * LANES]
            o_h = acc_h[:D_LATENT] * (1.0 / acc_h[D_LATENT:D_LATENT + 1])
            ys.append(jnp.dot(wuvt_ref[hh], o_h.astype(BF16), preferred_element_type=F32))
        y_t = jnp.concatenate(ys, axis=0)
        out_ref[...] = y_t.T.astype(BF16)


def _dsa(q_idx, q_abs, w_t, k_idx, ckv, ckv_t, btile, bmax, wuv_t, *, batch, seq):
    t = batch * seq
    nqb = seq // Q_BLOCK
    nkb = seq // KEY_BLOCK
    topk = min(TOPK_MAX, seq // 4)
    k_idx3 = k_idx.reshape(t // KEY_BLOCK, KEY_BLOCK, LANES)
    ckv3 = ckv.reshape(t // KEY_BLOCK, KEY_BLOCK, D_LATENT)
    per_batch = lambda shape: pl.BlockSpec(shape, lambda b, q: (b,) + (0,) * (len(shape) - 1),
                                           pipeline_mode=pl.Buffered(1))
    scored = lambda b, s: b * nqb + jnp.minimum(s, nqb - 1)
    attended = lambda b, s: b * nqb + jnp.maximum(s - 1, 0)
    return pl.pallas_call(
        functools.partial(_dsa_kernel, topk=topk, n_qb=nqb),
        out_shape=jax.ShapeDtypeStruct((t, W_A), BF16),
        grid=(batch, nqb + 1),
        in_specs=[pl.BlockSpec((1, N_HEADS_IDX, Q_BLOCK, LANES), lambda b, s: (scored(b, s), 0, 0, 0)),
                  pl.BlockSpec((1, N_HEADS_A, Q_BLOCK, D_LATENT), lambda b, s: (attended(b, s), 0, 0, 0)),
                  pl.BlockSpec((N_HEADS_IDX, Q_BLOCK), lambda b, s: (0, scored(b, s))),
                  per_batch((nkb, KEY_BLOCK, LANES)),
                  per_batch((nkb, KEY_BLOCK, D_LATENT)),
                  per_batch((nkb, CKVT_ROWS, KEY_BLOCK)),
                  _resident((N_HEADS_A, BIAS_ROWS, LANES)),
                  _resident((1, N_HEADS_A * LANES)),
                  _resident((N_HEADS_A, HEAD_DIM_A, D_LATENT))],
        out_specs=pl.BlockSpec((Q_BLOCK, W_A), lambda b, s: (attended(b, s), 0)),
        scratch_shapes=[pltpu.VMEM((2, nkb, KEY_BLOCK, LANES), I32),
                        pltpu.VMEM((32, seq // PLANE_KEYS + 1, 8, LANES), I32),
                        pltpu.VMEM((seq // PLANE_KEYS + 1, 8, LANES), I32),
                        pltpu.VMEM((2, 1, LANES), I32),
                        pltpu.VMEM((N_HEADS_A // 2, CKVT_ROWS, 2 * LANES), F32),
                        pltpu.VMEM((1, N_HEADS_A * LANES), F32),
                        pltpu.VMEM((KEY_BLOCK, (N_HEADS_A + 1) * LANES), F32),
                        pltpu.VMEM((1, 1), F32),
                        pltpu.VMEM((N_HEADS_IDX // 2, LANES, 2 * LANES), BF16),
                        pltpu.VMEM((N_HEADS_A // 2, D_LATENT, 2 * LANES), BF16)],
        compiler_params=pltpu.CompilerParams(dimension_semantics=("arbitrary", "arbitrary"),
                                             vmem_limit_bytes=VMEM_LIMIT),
        name="dsa",
    )(q_idx, q_abs, w_t, k_idx3, ckv3, ckv_t, btile, bmax, wuv_t)


def _mlstm_kernel(qk_ref, v_ref, o_ref, ift_ref, gbt_ref, hn_ref,
                  out_ref, cx_scr, m_scr, *, chunk, n_batch):
    L = chunk

    @pl.when(pl.program_id(0) == 0)
    def _():
        cx_scr[...] = jnp.zeros(cx_scr.shape, F32)
        m_scr[...] = jnp.zeros(m_scr.shape, F32)

    rr = lax.broadcasted_iota(I32, (L, L), 0)
    cc = lax.broadcasted_iota(I32, (L, L), 1)
    causal = cc <= rr
    triu = jnp.where(rr <= cc, 1.0, 0.0).astype(BF16)
    lane = lax.broadcasted_iota(I32, (8, L), 1)
    ones_col = jnp.where(lax.broadcasted_iota(I32, (L, HEAD_DIM_M), 1) == 0, 1.0, 0.0).astype(BF16)
    for bi in range(n_batch):
        _mlstm_chunk(qk_ref.at[bi], v_ref.at[bi], o_ref.at[bi], ift_ref.at[bi], gbt_ref, hn_ref,
                     out_ref.at[bi], cx_scr.at[bi], m_scr.at[bi], causal, triu, lane, ones_col, L)


def _mlstm_chunk(qk_ref, v_ref, o_ref, ift_ref, gbt_ref, hn_ref, out_ref, cx_scr, m_scr,
                 causal, triu, lane, ones_col, L):
    g_t = ift_ref[...] + gbt_ref[...]
    b_all = sum(jnp.dot(piece, triu, preferred_element_type=F32) for piece in _split3(_log_sigmoid(g_t)))
    b8 = pltpu.roll(b_all, N_HEADS_M, axis=0)
    a8 = g_t - b8
    cm = a8
    shift = 1
    while shift < L:
        cm = jnp.maximum(cm, jnp.where(lane >= shift, pltpu.roll(cm, shift, axis=1), NEG_BIG))
        shift *= 2
    m_prev = m_scr[...]
    mx = jnp.maximum(m_prev, cm)
    mx_last = mx[:, L - 1:L]
    decay8 = jnp.exp(m_prev - mx_last)
    m_scr[...] = b8[:, L - 1:L] + mx_last
    rows = jnp.concatenate([-mx,
                            jnp.exp(m_prev - mx),
                            jnp.exp(-(b8 + mx)),
                            jnp.exp(a8 - mx_last),
                            jnp.zeros((LANES - 32, L), F32)], axis=0)
    cols = rows.T

    o_gate = _sigmoid(o_ref[...])
    for hh in range(N_HEADS_M):
        hs = slice(hh * HEAD_DIM_M, (hh + 1) * HEAD_DIM_M)
        qb16 = qk_ref[:, hs]
        kb16 = qk_ref[:, W_M + hh * HEAD_DIM_M:W_M + (hh + 1) * HEAD_DIM_M]
        v_ext = jnp.concatenate([v_ref[:, hs], ones_col], axis=1)
        u_c = cols[:, hh:hh + 1]
        w_inter = cols[:, 8 + hh:9 + hh]
        em_c = cols[:, 16 + hh:17 + hh]
        wgt_c = cols[:, 24 + hh:25 + hh]
        cx_prev = cx_scr[hh]

        d_mat = jnp.where(causal, jnp.exp(u_c + a8[hh:hh + 1, :]), 0.0)
        s = lax.dot_general(qb16, kb16, (((1,), (1,)), ((), ())), preferred_element_type=F32) * d_mat
        intra = jnp.dot(s.astype(BF16), v_ext, preferred_element_type=F32)
        inter = jnp.dot(qb16, cx_prev.astype(BF16), preferred_element_type=F32)
        both = w_inter * inter + intra
        num = both[:, :HEAD_DIM_M]
        den = both[:, HEAD_DIM_M:HEAD_DIM_M + 1]
        hval = num / jnp.maximum(jnp.abs(den), em_c)

        kw = kb16.astype(F32) * wgt_c
        cx_scr[hh] = decay8[hh:hh + 1] * cx_prev + jnp.dot(kw.T.astype(BF16), v_ext,
                                                           preferred_element_type=F32)

        mu = jnp.mean(hval, axis=1, keepdims=True)
        cen = hval - mu
        var = jnp.mean(cen * cen, axis=1, keepdims=True)
        hn = cen * lax.rsqrt(var + EPS) * hn_ref[:, hs]
        out_ref[:, hs] = (hn * o_gate[:, hs]).astype(BF16)


def _mlstm(qk, v, o_pre, ift, gate_bias, head_norm, *, batch, seq):
    t = batch * seq
    L = MLSTM_CHUNK
    nc = seq // L
    gbt = jnp.broadcast_to(gate_bias.reshape(2 * N_HEADS_M, 1), (2 * N_HEADS_M, L))
    ift_b = ift.reshape(2 * N_HEADS_M, batch, seq).transpose(1, 0, 2)
    row = lambda w: pl.BlockSpec((batch, L, w), lambda c: (0, c, 0))
    out = pl.pallas_call(
        functools.partial(_mlstm_kernel, chunk=L, n_batch=batch),
        out_shape=jax.ShapeDtypeStruct((batch, seq, W_M), BF16),
        grid=(nc,),
        in_specs=[row(2 * W_M), row(W_M), row(W_M),
                  pl.BlockSpec((batch, 2 * N_HEADS_M, L), lambda c: (0, 0, c)),
                  _resident((2 * N_HEADS_M, L)), _resident((1, W_M))],
        out_specs=row(W_M),
        scratch_shapes=[pltpu.VMEM((batch, N_HEADS_M, HEAD_DIM_M, 2 * HEAD_DIM_M), F32),
                        pltpu.VMEM((batch, 8, 1), F32)],
        compiler_params=pltpu.CompilerParams(dimension_semantics=("arbitrary",),
                                             vmem_limit_bytes=VMEM_LIMIT),
        name="mlstm",
    )(qk.reshape(batch, seq, 2 * W_M), v.reshape(batch, seq, W_M), o_pre.reshape(batch, seq, W_M),
      ift_b, gbt, head_norm.reshape(1, -1))
    return out.reshape(t, W_M)


def kernel(x, c, ada_w, ada_b, ffn1_norm, ffn1_w1, ffn1_w3, ffn1_w2, mix_norm, w_in, conv_w, conv_b,
           kv_norm, w_uk, w_uv, mlstm_gate_bias, mlstm_head_norm, rel_bias, w_branch_attn,
           w_branch_mlstm, w_out, ffn2_norm, ffn2_w1, ffn2_w3, ffn2_w2, final_norm):
    batch, seq, d = x.shape
    depth = ada_w.shape[0]
    assert seq % max(FFN_TM, MIX_TM, MLSTM_CHUNK, KEY_BLOCK) == 0
    t = batch * seq
    xf = x.reshape(t, d)
    btile, bmax = _bias_tiles(rel_bias)
    for l in range(depth):
        mod = _adaln(c, ada_w[l], ada_b[l]).reshape(batch, 9, 1, d)
        sh1, sc1, g1, sh2, sc2, g2, sh3, sc3, g3 = [mod[:, n] for n in range(9)]
        xf = _ffn(xf, ffn1_norm[l], sh1, sc1, g1, ffn1_w1[l], ffn1_w3[l], ffn1_w2[l], final_norm,
                  seq=seq, final_norm=False)
        wuk_hdc = w_uk[l].transpose(0, 2, 1).reshape(N_HEADS_A // 2, 2, HEAD_DIM_A, D_LATENT)
        zeros = jnp.zeros_like(wuk_hdc[:, 0])
        wuk_t = jnp.concatenate([jnp.concatenate([wuk_hdc[:, 0], zeros], axis=2),
                                 jnp.concatenate([zeros, wuk_hdc[:, 1]], axis=2)], axis=1).astype(BF16)
        (q_abs, q_idx, k_idx, ckv, ckv_t, w_t, ift, qk_m, v_m, o_pre, gate_a, gate_m) = _mixin(
            xf, mix_norm[l], sh2, sc2, _pack_w_in(w_in[l], d), kv_norm[l], wuk_t, conv_w[l], conv_b[l],
            seq=seq)
        wuv_t = w_uv[l].transpose(0, 2, 1).astype(BF16)
        y_a = _dsa(q_idx, q_abs, w_t, k_idx, ckv, ckv_t, btile, bmax, wuv_t, batch=batch, seq=seq)
        h_m = _mlstm(qk_m, v_m, o_pre, ift, mlstm_gate_bias[l], mlstm_head_norm[l], batch=batch, seq=seq)
        xf = _ffn(xf, ffn2_norm[l], sh3, sc3, g3, ffn2_w1[l], ffn2_w3[l], ffn2_w2[l], final_norm,
                  seq=seq, final_norm=(l == depth - 1),
                  merge=(y_a, h_m, gate_a, gate_m, g2, w_branch_attn[l], w_branch_mlstm[l], w_out[l]))
    return xf.reshape(batch, seq, d)
```

```python
import functools
import math

import jax
import jax.numpy as jnp
from jax import lax
from jax.experimental import pallas as pl
from jax.experimental.pallas import tpu as pltpu

F32 = jnp.float32
BF16 = jnp.bfloat16
I32 = jnp.int32

LANES = 128
VMEM_LIMIT = 56 * 1024 * 1024

N_HEADS_A = 8
HEAD_DIM_A = 64
D_LATENT = 256
N_HEADS_IDX = 8
HEAD_DIM_IDX = 64
TOPK_MAX = 256
Q_BLOCK = 128
N_BUCKETS = 32
MAX_DISTANCE = 128
N_HEADS_M = 4
HEAD_DIM_M = 128
CONV_WIDTH = 4
EPS = 1e-6
IDX_SCALE = (N_HEADS_IDX ** -0.5) * (HEAD_DIM_IDX ** -0.5)
W_A = N_HEADS_A * HEAD_DIM_A
W_M = N_HEADS_M * HEAD_DIM_M

FFN_TM = 512
FFN_CHUNK = 256
MIX_TM = 512
KEY_BLOCK = 256
BLOCK_UNROLL = 4
PLANE_KEYS = 256
MLSTM_CHUNK = 256
NEG_BIG = -1e30
INT_MIN = -2 ** 31

BIAS_PAD = 2 * KEY_BLOCK - Q_BLOCK
BIAS_ROWS = KEY_BLOCK + BIAS_PAD
CKVT_ROWS = D_LATENT + 16
LOG2E = math.log2(math.e)


def _sigmoid(x):
    return 1.0 / (1.0 + jnp.exp(-x))


def _log_sigmoid(x):
    return jnp.minimum(x, 0.0) - jnp.log(1.0 + jnp.exp(-jnp.abs(x)))


def _rms_norm(x, gain):
    ms = jnp.mean(x * x, axis=-1, keepdims=True)
    return x * lax.rsqrt(ms + EPS) * gain


def _split3(x):
    hi = x.astype(BF16)
    r1 = x - hi.astype(F32)
    mid = r1.astype(BF16)
    lo = (r1 - mid.astype(F32)).astype(BF16)
    return hi, mid, lo


def _resident(shape):
    nd = len(shape)
    return pl.BlockSpec(shape, lambda *_: (0,) * nd, pipeline_mode=pl.Buffered(1))


def _adaln_kernel(c_ref, w_ref, b_ref, o_ref):
    c = c_ref[...]
    cond = c * _sigmoid(c)
    o_ref[...] = jnp.dot(cond.astype(BF16), w_ref[...].astype(BF16),
                         preferred_element_type=F32) + b_ref[...]


def _adaln(c, ada_w, ada_b):
    b, d = c.shape
    n = ada_w.shape[1]
    rows = 8
    c_pad = jnp.zeros((rows, d), F32).at[:b].set(c)
    tn = 1024
    out = pl.pallas_call(
        _adaln_kernel,
        out_shape=jax.ShapeDtypeStruct((rows, n), F32),
        grid=(n // tn,),
        in_specs=[pl.BlockSpec((rows, d), lambda j: (0, 0)),
                  pl.BlockSpec((d, tn), lambda j: (0, j)),
                  pl.BlockSpec((1, tn), lambda j: (0, j))],
        out_specs=pl.BlockSpec((rows, tn), lambda j: (0, j)),
        compiler_params=pltpu.CompilerParams(dimension_semantics=("arbitrary",),
                                             vmem_limit_bytes=VMEM_LIMIT),
        name="adaln",
    )(c_pad, ada_w, ada_b.reshape(1, n))
    return out[:b]


def _t5_bucket(dist):
    n = jnp.maximum(dist, 0)
    max_exact = N_BUCKETS // 2
    nf = jnp.maximum(n, 1).astype(F32)
    large = max_exact + (jnp.log(nf / max_exact) / math.log(MAX_DISTANCE / max_exact)
                         * (N_BUCKETS - max_exact)).astype(I32)
    large = jnp.minimum(large, N_BUCKETS - 1)
    return jnp.where(n < max_exact, n, large)


def _bias_kernel(rel_ref, tile_ref, max_ref):
    r = lax.broadcasted_iota(I32, (BIAS_ROWS, LANES), 0)
    i = lax.broadcasted_iota(I32, (BIAS_ROWS, LANES), 1)
    bucket = _t5_bucket(i - r + BIAS_PAD)
    for h in range(N_HEADS_A):
        acc = jnp.zeros((BIAS_ROWS, LANES), F32)
        top = rel_ref[0, h] * LOG2E
        for bkt in range(N_BUCKETS):
            val = rel_ref[bkt, h] * LOG2E
            acc = jnp.where(bucket == bkt, val, acc)
            top = jnp.maximum(top, val)
        tile_ref[h] = acc
        max_ref[:, h * LANES:(h + 1) * LANES] = jnp.full((1, LANES), top, F32)


def _bias_tiles(rel_bias):
    return pl.pallas_call(
        _bias_kernel,
        out_shape=(jax.ShapeDtypeStruct((N_HEADS_A, BIAS_ROWS, LANES), F32),
                   jax.ShapeDtypeStruct((1, N_HEADS_A * LANES), F32)),
        in_specs=[pl.BlockSpec(memory_space=pltpu.SMEM)],
        out_specs=(pl.BlockSpec(memory_space=pltpu.VMEM), pl.BlockSpec(memory_space=pltpu.VMEM)),
        name="bias_tiles",
    )(rel_bias)


def _ffn_kernel(*refs, n_chunks, final_norm, merge):
    if merge:
        (x_ref, ya_ref, hm_ref, ga_ref, gm_ref, gmix_ref, wa_ref, wm_ref, wo_ref), refs = refs[:9], refs[9:]
    else:
        x_ref, refs = refs[0], refs[1:]
    gain_ref, sh_ref, sc_ref, g_ref, w1_ref, w3_ref, w2_ref, fin_ref, o_ref, h_scr, acc_scr = refs
    x = x_ref[...]
    if merge:
        pa = jnp.dot(ya_ref[...], wa_ref[...], preferred_element_type=F32)
        pm = jnp.dot(hm_ref[...], wm_ref[...], preferred_element_type=F32)
        merged = _sigmoid(ga_ref[...]) * pa + _sigmoid(gm_ref[...]) * pm
        x = x + gmix_ref[0] * jnp.dot(merged.astype(BF16), wo_ref[...], preferred_element_type=F32)
    h = _rms_norm(x, gain_ref[...]) * (1.0 + sc_ref[0]) + sh_ref[0]
    h_scr[...] = h.astype(BF16)
    for j in range(n_chunks):
        hb = h_scr[...]
        cols = slice(j * FFN_CHUNK, (j + 1) * FFN_CHUNK)
        u1 = jnp.dot(hb, w1_ref[:, cols], preferred_element_type=F32)
        u3 = jnp.dot(hb, w3_ref[:, cols], preferred_element_type=F32)
        a = (u1 * _sigmoid(u1)) * u3
        part = jnp.dot(a.astype(BF16), w2_ref[j], preferred_element_type=F32)
        if j == 0:
            acc_scr[...] = part
        else:
            acc_scr[...] += part
    out = x + (0.5 * g_ref[0]) * acc_scr[...]
    if final_norm:
        out = _rms_norm(out, fin_ref[...])
    o_ref[...] = out


def _ffn(x, gain, sh, sc, g, w1, w3, w2, fin, *, seq, final_norm, merge=None):
    t, d = x.shape
    dff = w1.shape[1]
    nch = dff // FFN_CHUNK
    w1c = w1.astype(BF16)
    w3c = w3.astype(BF16)
    w2c = w2.astype(BF16).reshape(nch, FFN_CHUNK, d)
    tm = FFN_TM
    per_b = seq // tm
    row = lambda w: pl.BlockSpec((tm, w), lambda i: (i, 0))
    mod_spec = pl.BlockSpec((1, 1, d), lambda i: (i // per_b, 0, 0))
    merge_specs, merge_args = [], []
    if merge is not None:
        y_a, h_m, gate_a, gate_m, g_mix, w_a, w_m, w_o = merge
        merge_specs = [row(W_A), row(W_M), row(d), row(d), mod_spec,
                       _resident((W_A, d)), _resident((W_M, d)), _resident((d, d))]
        merge_args = [y_a, h_m, gate_a, gate_m, g_mix, w_a.astype(BF16), w_m.astype(BF16), w_o.astype(BF16)]
    return pl.pallas_call(
        functools.partial(_ffn_kernel, n_chunks=nch, final_norm=final_norm, merge=merge is not None),
        out_shape=jax.ShapeDtypeStruct((t, d), F32),
        grid=(t // tm,),
        in_specs=[row(d)] + merge_specs + [
                  _resident((1, d)), mod_spec, mod_spec, mod_spec,
                  _resident((d, dff)), _resident((d, dff)),
                  _resident((nch, FFN_CHUNK, d)), _resident((1, d))],
        out_specs=row(d),
        scratch_shapes=[pltpu.VMEM((tm, d), BF16), pltpu.VMEM((tm, d), F32)],
        compiler_params=pltpu.CompilerParams(dimension_semantics=("arbitrary",),
                                             vmem_limit_bytes=VMEM_LIMIT),
        name="ffn_final" if final_norm else "ffn",
    )(x, *merge_args, gain.reshape(1, d), sh, sc, g, w1c, w3c, w2c, fin.reshape(1, d))


_C_QA = 0
_C_CKV = _C_QA + W_A
_C_QI = _C_CKV + D_LATENT
_C_KI = _C_QI + N_HEADS_IDX * LANES
_C_SM = _C_KI + LANES
_C_QK = _C_SM + LANES
_C_V = _C_QK + 2 * W_M
_C_O = _C_V + W_M
_C_GA = _C_O + W_M
_C_GM = _C_GA + 1024
_C_END = _C_GM + 1024
_SM_W = 0
_SM_I = N_HEADS_IDX


def _pack_w_in(w_in, d_model):
    splits = (W_A, D_LATENT, N_HEADS_IDX * HEAD_DIM_IDX, HEAD_DIM_IDX, N_HEADS_IDX,
              W_M, W_M, W_M, N_HEADS_M, N_HEADS_M, W_M, d_model, d_model)
    offs = [0]
    for s in splits:
        offs.append(offs[-1] + s)
    (q_a, c_kv, q_i, k_i, w_i, q_m, k_m, v_m, i_p, f_p, o_p, g_a, g_m) = [
        w_in[:, offs[n]:offs[n + 1]] for n in range(len(splits))]
    d = w_in.shape[0]

    def pad_heads(w, nh, hd):
        w = w.reshape(d, nh, hd)
        return jnp.pad(w, ((0, 0), (0, 0), (0, LANES - hd))).reshape(d, nh * LANES)

    small = jnp.concatenate([w_i, i_p, f_p], axis=1)
    small = jnp.pad(small, ((0, 0), (0, LANES - small.shape[1])))
    packed = jnp.concatenate([
        q_a, c_kv, pad_heads(q_i, N_HEADS_IDX, HEAD_DIM_IDX),
        jnp.pad(k_i, ((0, 0), (0, LANES - HEAD_DIM_IDX))), small, q_m, k_m, v_m, o_p, g_a, g_m], axis=1)
    assert packed.shape[1] == _C_END
    return packed.astype(BF16)


def _mixin_kernel(x_ref, gain_ref, sh_ref, sc_ref, w_ref, kvn_ref, wuk_ref, cw_ref, cb_ref,
                  qabs_ref, qidx_ref, kidx_ref, ckv_ref, ckvt_ref, wt_ref, ift_ref,
                  qk_ref, v_ref, o_ref, ga_ref, gm_ref, h_scr, xe_scr, *, tm, tiles_per_seq):
    nqb = tm // Q_BLOCK

    @pl.when(pl.program_id(0) % tiles_per_seq == 0)
    def _():
        xe_scr[:8] = jnp.zeros((8, xe_scr.shape[1]), F32)

    x = x_ref[...]
    h = _rms_norm(x, gain_ref[...]) * (1.0 + sc_ref[0]) + sh_ref[0]
    h_scr[...] = h.astype(BF16)

    def proj(lo, hi):
        return jnp.dot(h_scr[...], w_ref[:, lo:hi], preferred_element_type=F32)

    qa = proj(_C_QA, _C_CKV)
    scale = HEAD_DIM_A ** -0.5 * LOG2E
    for g in range(N_HEADS_A // 2):
        q_pair = qa[:, g * LANES:(g + 1) * LANES].astype(BF16)
        q_abs = jnp.dot(q_pair, wuk_ref[g], preferred_element_type=F32) * scale
        for j in range(2):
            qabs_ref[:, 2 * g + j] = (q_abs[:, j * D_LATENT:(j + 1) * D_LATENT]
                                      .astype(BF16).reshape(nqb, Q_BLOCK, D_LATENT))
    ckv = _rms_norm(proj(_C_CKV, _C_QI), kvn_ref[...])
    ckv_ref[...] = ckv.astype(BF16)
    ckv_t = ckv.T
    ones_row = jnp.where(lax.broadcasted_iota(I32, (CKVT_ROWS - D_LATENT, KEY_BLOCK), 0) == 0, 1.0, 0.0)
    for j in range(tm // KEY_BLOCK):
        ckvt_ref[j, :D_LATENT] = ckv_t[:, j * KEY_BLOCK:(j + 1) * KEY_BLOCK].astype(BF16)
        ckvt_ref[j, D_LATENT:] = ones_row.astype(BF16)
    qi = proj(_C_QI, _C_KI)
    for hh in range(N_HEADS_IDX):
        qidx_ref[:, hh] = qi[:, hh * LANES:(hh + 1) * LANES].astype(BF16).reshape(nqb, Q_BLOCK, LANES)
    kidx_ref[...] = proj(_C_KI, _C_SM).astype(BF16)
    small_t = proj(_C_SM, _C_QK).T
    wt_ref[...] = small_t[_SM_W:_SM_W + N_HEADS_IDX] * IDX_SCALE
    ift_ref[...] = small_t[_SM_I:_SM_I + 2 * N_HEADS_M]
    xe_scr[8:] = proj(_C_QK, _C_V)
    xe = xe_scr[...]
    xq = xe[8:]
    conv = xq * cw_ref[CONV_WIDTH - 1:CONV_WIDTH, :] + cb_ref[...]
    for d in range(1, CONV_WIDTH):
        conv = conv + pltpu.roll(xe, d, axis=0)[8:] * cw_ref[CONV_WIDTH - 1 - d:CONV_WIDTH - d, :]
    xe_scr[:8] = xe_scr[tm:]
    qk = conv * _sigmoid(conv)
    qk_ref[:, :W_M] = qk[:, :W_M].astype(BF16)
    qk_ref[:, W_M:] = (qk[:, W_M:] * (HEAD_DIM_M ** -0.5)).astype(BF16)
    v_ref[...] = proj(_C_V, _C_O).astype(BF16)
    o_ref[...] = proj(_C_O, _C_GA)
    ga_ref[...] = proj(_C_GA, _C_GM)
    gm_ref[...] = proj(_C_GM, _C_END)


def _mixin(x, gain, sh, sc, w_packed, kv_norm, wuk_t, conv_w, conv_b, *, seq):
    t, d = x.shape
    tm = MIX_TM
    per_b = seq // tm
    nqb = tm // Q_BLOCK
    row = lambda w: pl.BlockSpec((tm, w), lambda i: (i, 0))
    mod_spec = pl.BlockSpec((1, 1, d), lambda i: (i // per_b, 0, 0))
    out_shape = (
        jax.ShapeDtypeStruct((t // Q_BLOCK, N_HEADS_A, Q_BLOCK, D_LATENT), BF16),
        jax.ShapeDtypeStruct((t // Q_BLOCK, N_HEADS_IDX, Q_BLOCK, LANES), BF16),
        jax.ShapeDtypeStruct((t, LANES), BF16),
        jax.ShapeDtypeStruct((t, D_LATENT), BF16),
        jax.ShapeDtypeStruct((t // KEY_BLOCK, CKVT_ROWS, KEY_BLOCK), BF16),
        jax.ShapeDtypeStruct((N_HEADS_IDX, t), F32),
        jax.ShapeDtypeStruct((2 * N_HEADS_M, t), F32),
        jax.ShapeDtypeStruct((t, 2 * W_M), BF16),
        jax.ShapeDtypeStruct((t, W_M), BF16),
        jax.ShapeDtypeStruct((t, W_M), F32),
        jax.ShapeDtypeStruct((t, d), F32),
        jax.ShapeDtypeStruct((t, d), F32),
    )
    out_specs = (
        pl.BlockSpec((nqb, N_HEADS_A, Q_BLOCK, D_LATENT), lambda i: (i, 0, 0, 0)),
        pl.BlockSpec((nqb, N_HEADS_IDX, Q_BLOCK, LANES), lambda i: (i, 0, 0, 0)),
        row(LANES), row(D_LATENT),
        pl.BlockSpec((tm // KEY_BLOCK, CKVT_ROWS, KEY_BLOCK), lambda i: (i, 0, 0)),
        pl.BlockSpec((N_HEADS_IDX, tm), lambda i: (0, i)),
        pl.BlockSpec((2 * N_HEADS_M, tm), lambda i: (0, i)),
        row(2 * W_M), row(W_M), row(W_M), row(d), row(d),
    )
    return pl.pallas_call(
        functools.partial(_mixin_kernel, tm=tm, tiles_per_seq=per_b),
        out_shape=out_shape,
        grid=(t // tm,),
        in_specs=[pl.BlockSpec((tm, d), lambda i: (i, 0)), _resident((1, d)), mod_spec, mod_spec,
                  _resident((d, _C_END)), _resident((1, D_LATENT)),
                  _resident((N_HEADS_A // 2, LANES, 2 * D_LATENT)),
                  _resident((CONV_WIDTH, 2 * W_M)), _resident((1, 2 * W_M))],
        out_specs=out_specs,
        scratch_shapes=[pltpu.VMEM((tm, d), BF16), pltpu.VMEM((tm + 8, 2 * W_M), F32)],
        compiler_params=pltpu.CompilerParams(dimension_semantics=("arbitrary",),
                                             vmem_limit_bytes=VMEM_LIMIT),
        name="mixin",
    )(x, gain.reshape(1, d), sh, sc, w_packed, kv_norm.reshape(1, D_LATENT), wuk_t,
      conv_w, conv_b.reshape(1, -1))


def _sortable_key(score):
    bits = pltpu.bitcast(score, I32)
    bits = jnp.where(bits == INT_MIN, 0, bits)
    return jnp.where(bits < 0, bits ^ 0x7FFFFFFF, bits)


def _bit_transpose32(words):
    v = list(words)
    j, m = 16, 0x0000FFFF
    while j:
        k = 0
        while k < 32:
            t = (v[k] ^ lax.shift_right_logical(v[k + j], jnp.int32(j))) & m
            v[k] = v[k] ^ t
            v[k + j] = v[k + j] ^ (t << j)
            k = (k + j + 1) & ~j
        j >>= 1
        m = (m ^ (m << j)) & 0x7FFFFFFF
    return v


def _dsa_kernel(qidx_ref, qabs_ref, wt_ref, kidx_ref, ckv_ref, ckvt_ref, btile_ref, bmax_ref, wuvt_ref,
                out_ref, keys_scr, planes_scr, cand_scr, tau_scr, acc_scr, m_scr, ltc_scr, kmax_scr,
                qta_scr, qtc_scr, *, topk, n_qb):
    kb_sz = KEY_BLOCK
    step = pl.program_id(1)
    has_c = step >= 1
    qa = jnp.minimum(step, n_qb - 1)
    qc = jnp.maximum(step - 1, 0)
    slot_a = step & 1
    slot_c = 1 - slot_a
    n_a = qa // (kb_sz // Q_BLOCK) + 1
    n_c = qc // (kb_sz // Q_BLOCK) + 1
    qa0 = qa * Q_BLOCK
    qc0 = qc * Q_BLOCK
    row_id = lax.broadcasted_iota(I32, (kb_sz, LANES), 0)
    lane_id = lax.broadcasted_iota(I32, (kb_sz, LANES), 1)
    n_blocks = keys_scr.shape[1]
    n_groups = N_HEADS_A // 2
    pair = 2 * LANES
    ones8 = jnp.ones((8, D_LATENT), BF16)

    @pl.when(step == 0)
    def _():
        keys_scr[1, 0] = jnp.full((kb_sz, LANES), INT_MIN, I32)
        tau_scr[1] = jnp.zeros((1, LANES), I32)
        planes_scr[...] = jnp.zeros(planes_scr.shape, I32)

        def kn_body(kb, mx):
            c = ckv_ref[kb].astype(F32)
            n2 = lax.dot_general(ones8, (c * c).astype(BF16), (((1,), (1,)), ((), ())),
                                 preferred_element_type=F32)
            return jnp.maximum(mx, n2[0:1])
        mx = lax.fori_loop(0, n_blocks, kn_body, jnp.zeros((1, kb_sz), F32))
        kmax_scr[...] = jnp.max(mx, axis=1, keepdims=True)

    for g in range(n_groups):
        qi_g = qidx_ref[0, 2 * g:2 * g + 2].reshape(2 * Q_BLOCK, LANES).astype(F32)
        qta_scr[g] = qi_g.T.astype(BF16)
        qa_g = qabs_ref[0, 2 * g:2 * g + 2].reshape(2 * Q_BLOCK, D_LATENT).astype(F32)
        qtc_scr[g] = qa_g.T.astype(BF16)

    def idx_dot(kb, g):
        return jnp.dot(kidx_ref[kb], qta_scr[g], preferred_element_type=F32)

    def logits(kb, g):
        return jnp.dot(ckv_ref[kb], qtc_scr[g], preferred_element_type=F32)

    def bias_start(kb):
        delta = jnp.minimum(qc0 - kb * kb_sz, BIAS_PAD)
        return pl.multiple_of(BIAS_PAD - delta, LANES)

    qn2 = []
    for g in range(n_groups):
        q_g = qabs_ref[0, 2 * g:2 * g + 2].reshape(2 * Q_BLOCK, D_LATENT).astype(F32)
        qn2.append(lax.dot_general(ones8, (q_g * q_g).astype(BF16), (((1,), (1,)), ((), ())),
                                   preferred_element_type=F32)[0:1])
    bound = jnp.sqrt(jnp.concatenate(qn2, axis=1) * kmax_scr[...]) * 1.02 + bmax_ref[...] + 1e-3
    tau_c = tau_scr[slot_c]
    w_t = wt_ref[...]

    acc_scr[...] = jnp.zeros(acc_scr.shape, F32)
    for g in range(n_groups):
        ltc_scr[:, g * pair:(g + 1) * pair] = logits(0, g)

    bound_far = bound - jnp.concatenate([btile_ref[hh, 0:1, :] for hh in range(N_HEADS_A)], axis=1)

    def block_step(kb_raw, far):
        kb = jnp.minimum(kb_raw, n_a - 1)
        kc = jnp.minimum(kb_raw, n_c - 1)
        kc_next = jnp.minimum(kb_raw + 1, n_c - 1)
        thr = jnp.where(has_c & (kb_raw < n_c), tau_c - 1, jnp.int32(2 ** 31 - 1))
        sel = keys_scr[slot_c, kc] > thr
        ct_blk = ckvt_ref[kc]
        start = bias_start(kc)
        ref_pt = bound_far if far else bound
        score = jnp.zeros((kb_sz, LANES), F32)
        for g in range(n_groups):
            s_t = idx_dot(kb, g)
            for j in range(2):
                hh = 2 * g + j
                score = score + jnp.maximum(s_t[:, j * LANES:(j + 1) * LANES], 0.0) * w_t[hh:hh + 1, :]
        for g in range(n_groups):
            lt = ltc_scr[:, g * pair:(g + 1) * pair]
            ps = []
            for j in range(2):
                hh = 2 * g + j
                piece = lt[:, j * LANES:(j + 1) * LANES]
                if not far:
                    piece = piece + btile_ref[hh, pl.ds(start, kb_sz), :]
                ps.append(jnp.exp2(jnp.where(sel, piece, NEG_BIG) - ref_pt[:, hh * LANES:(hh + 1) * LANES]))
            ltc_scr[:, g * pair:(g + 1) * pair] = logits(kc_next, g)
            acc_scr[g] += jnp.dot(ct_blk, jnp.concatenate(ps, axis=1).astype(BF16),
                                  preferred_element_type=F32)
        valid = (kb * kb_sz + row_id) <= (qa0 + lane_id)
        keys = jnp.where(valid, _sortable_key(score), INT_MIN)
        keys_scr[slot_a, kb] = keys
        v = keys ^ INT_MIN
        for sub in range(kb_sz // PLANE_KEYS):
            r0 = sub * PLANE_KEYS
            words = _bit_transpose32([v[r0 + 8 * i:r0 + 8 * (i + 1), :] for i in range(32)])
            for bit in range(32):
                planes_scr[bit, kb * (kb_sz // PLANE_KEYS) + sub] = words[31 - bit]

    def block_body(far, it, carry):
        for u in range(BLOCK_UNROLL):
            block_step(it * BLOCK_UNROLL + u, far)
        return carry

    n_far_iters = jnp.maximum(n_c - 2, 0) // BLOCK_UNROLL
    lax.fori_loop(0, n_far_iters, functools.partial(block_body, True), 0)
    lax.fori_loop(n_far_iters, (n_a + BLOCK_UNROLL - 1) // BLOCK_UNROLL,
                  functools.partial(block_body, False), 0)
    n_kb = n_a


    n_planes = planes_scr.shape[1]
    live = n_kb * (kb_sz // PLANE_KEYS)

    def radix_select(width):
        blk_id = lax.broadcasted_iota(I32, (width, 8, LANES), 0)
        cand_scr[:width] = jnp.where(blk_id < live, -1, 0)

        def bit_body(it, carry):
            above, tau_u = carry
            bit = 31 - it
            ones = cand_scr[:width] & planes_scr[bit, :width]
            c1 = jnp.sum(jnp.sum(lax.population_count(ones), axis=0), axis=0, keepdims=True)
            take = (above + c1) >= topk
            cand_scr[:width] = jnp.where(take, ones, cand_scr[:width] ^ ones)
            above = jnp.where(take, above, above + c1)
            tau_u = jnp.where(take, tau_u | (jnp.int32(1) << bit), tau_u)
            return above, tau_u

        zero = jnp.zeros((1, LANES), I32)
        above, tau_u = lax.fori_loop(0, 32, bit_body, (zero, zero))
        n_eq = jnp.sum(jnp.sum(lax.population_count(cand_scr[:width]), axis=0), axis=0, keepdims=True)
        return above, tau_u, n_eq

    widths = [n_planes * (i + 1) // 4 for i in range(4)]
    select = functools.partial(radix_select, widths[-1])
    for width in reversed(widths[:-1]):
        select = functools.partial(lax.cond, live <= width, functools.partial(radix_select, width), select)
    n_gt, tau_u, n_eq = select()
    tau = tau_u ^ INT_MIN

    need = topk - n_gt
    overflow = n_eq > need
    seq_bits = max(1, (n_blocks * kb_sz - 1).bit_length())

    @pl.when(jnp.max(jnp.where(overflow, 1, 0)) > 0)
    def _():
        def count_ties_before(trial):
            def body(kb, acc):
                hit = jnp.where((keys_scr[slot_a, kb] == tau) & ((kb * kb_sz + row_id) < trial), 1, 0)
                return acc + jnp.sum(hit.reshape(kb_sz // 8, 8, LANES), axis=0)
            acc = lax.fori_loop(0, n_kb, body, jnp.zeros((8, LANES), I32))
            return jnp.sum(acc, axis=0, keepdims=True)

        def idx_body(it, jc):
            trial = jc | (jnp.int32(1) << (seq_bits - 1 - it))
            return jnp.where(count_ties_before(trial) < need, trial, jc)

        j_cut = lax.fori_loop(0, seq_bits, idx_body, jnp.zeros((1, LANES), I32))

        def demote_body(kb, carry):
            k = keys_scr[slot_a, kb]
            drop = overflow & (k == tau) & ((kb * kb_sz + row_id) > j_cut)
            keys_scr[slot_a, kb] = jnp.where(drop, INT_MIN, k)
            return carry

        lax.fori_loop(0, n_kb, demote_body, 0)

    tau_scr[slot_a] = jnp.maximum(tau, INT_MIN + 1)

    l_min = jnp.min(jnp.concatenate([acc_scr[g, D_LATENT:D_LATENT + 1, :] for g in range(n_groups)], axis=1))

    @pl.when(has_c & jnp.logical_not(l_min >= 2.0 ** -80))
    def _():
        m_scr[...] = jnp.full(m_scr.shape, NEG_BIG, F32)
        acc_scr[...] = jnp.zeros(acc_scr.shape, F32)

        def exact_body(kb, carry):
            sel = keys_scr[slot_c, kb] >= tau_c
            ct_blk = ckvt_ref[kb]
            start = bias_start(kb)
            for g in range(n_groups):
                lt = logits(kb, g)
                ps, alphas = [], []
                for j in range(2):
                    hh = 2 * g + j
                    sl = slice(hh * LANES, (hh + 1) * LANES)
                    piece = lt[:, j * LANES:(j + 1) * LANES] + btile_ref[hh, pl.ds(start, kb_sz), :]
                    masked = jnp.where(sel, piece, NEG_BIG)
                    m_old = m_scr[:, sl]
                    m_new = jnp.maximum(m_old, jnp.max(masked, axis=0, keepdims=True))
                    m_scr[:, sl] = m_new
                    alphas.append(jnp.exp2(m_old - m_new))
                    ps.append(jnp.exp2(masked - m_new))
                pv = jnp.dot(ct_blk, jnp.concatenate(ps, axis=1).astype(BF16), preferred_element_type=F32)
                acc_scr[g] = jnp.concatenate(alphas, axis=1) * acc_scr[g] + pv
            return carry

        lax.fori_loop(0, n_c, exact_body, 0)

    @pl.when(has_c)
    def _():
        ys = []
        for hh in range(N_HEADS_A):
            acc_h = acc_scr[hh // 2, :, (hh % 2) * LANES:(hh % 2 + 1) * LANES]
            o_h = acc_h[:D_LATENT] * (1.0 / acc_h[D_LATENT:D_LATENT + 1])
            ys.append(jnp.dot(wuvt_ref[hh], o_h.astype(BF16), preferred_element_type=F32))
        y_t = jnp.concatenate(ys, axis=0)
        out_ref[...] = y_t.T.astype(BF16)


def _dsa(q_idx, q_abs, w_t, k_idx, ckv, ckv_t, btile, bmax, wuv_t, *, batch, seq):
    t = batch * seq
    nqb = seq // Q_BLOCK
    nkb = seq // KEY_BLOCK
    topk = min(TOPK_MAX, seq // 4)
    k_idx3 = k_idx.reshape(t // KEY_BLOCK, KEY_BLOCK, LANES)
    ckv3 = ckv.reshape(t // KEY_BLOCK, KEY_BLOCK, D_LATENT)
    per_batch = lambda shape: pl.BlockSpec(shape, lambda b, q: (b,) + (0,) * (len(shape) - 1),
                                           pipeline_mode=pl.Buffered(1))
    scored = lambda b, s: b * nqb + jnp.minimum(s, nqb - 1)
    attended = lambda b, s: b * nqb + jnp.maximum(s - 1, 0)
    return pl.pallas_call(
        functools.partial(_dsa_kernel, topk=topk, n_qb=nqb),
        out_shape=jax.ShapeDtypeStruct((t, W_A), BF16),
        grid=(batch, nqb + 1),
        in_specs=[pl.BlockSpec((1, N_HEADS_IDX, Q_BLOCK, LANES), lambda b, s: (scored(b, s), 0, 0, 0)),
                  pl.BlockSpec((1, N_HEADS_A, Q_BLOCK, D_LATENT), lambda b, s: (attended(b, s), 0, 0, 0)),
                  pl.BlockSpec((N_HEADS_IDX, Q_BLOCK), lambda b, s: (0, scored(b, s))),
                  per_batch((nkb, KEY_BLOCK, LANES)),
                  per_batch((nkb, KEY_BLOCK, D_LATENT)),
                  per_batch((nkb, CKVT_ROWS, KEY_BLOCK)),
                  _resident((N_HEADS_A, BIAS_ROWS, LANES)),
                  _resident((1, N_HEADS_A * LANES)),
                  _resident((N_HEADS_A, HEAD_DIM_A, D_LATENT))],
        out_specs=pl.BlockSpec((Q_BLOCK, W_A), lambda b, s: (attended(b, s), 0)),
        scratch_shapes=[pltpu.VMEM((2, nkb, KEY_BLOCK, LANES), I32),
                        pltpu.VMEM((32, seq // PLANE_KEYS + 1, 8, LANES), I32),
                        pltpu.VMEM((seq // PLANE_KEYS + 1, 8, LANES), I32),
                        pltpu.VMEM((2, 1, LANES), I32),
                        pltpu.VMEM((N_HEADS_A // 2, CKVT_ROWS, 2 * LANES), F32),
                        pltpu.VMEM((1, N_HEADS_A * LANES), F32),
                        pltpu.VMEM((KEY_BLOCK, (N_HEADS_A + 1) * LANES), F32),
                        pltpu.VMEM((1, 1), F32),
                        pltpu.VMEM((N_HEADS_IDX // 2, LANES, 2 * LANES), BF16),
                        pltpu.VMEM((N_HEADS_A // 2, D_LATENT, 2 * LANES), BF16)],
        compiler_params=pltpu.CompilerParams(dimension_semantics=("arbitrary", "arbitrary"),
                                             vmem_limit_bytes=VMEM_LIMIT),
        name="dsa",
    )(q_idx, q_abs, w_t, k_idx3, ckv3, ckv_t, btile, bmax, wuv_t)


def _mlstm_kernel(qk_ref, v_ref, o_ref, ift_ref, gbt_ref, hn_ref,
                  out_ref, cx_scr, m_scr, *, chunk, n_batch):
    L = chunk

    @pl.when(pl.program_id(0) == 0)
    def _():
        cx_scr[...] = jnp.zeros(cx_scr.shape, F32)
        m_scr[...] = jnp.zeros(m_scr.shape, F32)

    rr = lax.broadcasted_iota(I32, (L, L), 0)
    cc = lax.broadcasted_iota(I32, (L, L), 1)
    causal = cc <= rr
    triu = jnp.where(rr <= cc, 1.0, 0.0).astype(BF16)
    lane = lax.broadcasted_iota(I32, (8, L), 1)
    ones_col = jnp.where(lax.broadcasted_iota(I32, (L, HEAD_DIM_M), 1) == 0, 1.0, 0.0).astype(BF16)
    for bi in range(n_batch):
        _mlstm_chunk(qk_ref.at[bi], v_ref.at[bi], o_ref.at[bi], ift_ref.at[bi], gbt_ref, hn_ref,
                     out_ref.at[bi], cx_scr.at[bi], m_scr.at[bi], causal, triu, lane, ones_col, L)


def _mlstm_chunk(qk_ref, v_ref, o_ref, ift_ref, gbt_ref, hn_ref, out_ref, cx_scr, m_scr,
                 causal, triu, lane, ones_col, L):
    g_t = ift_ref[...] + gbt_ref[...]
    b_all = sum(jnp.dot(piece, triu, preferred_element_type=F32) for piece in _split3(_log_sigmoid(g_t)))
    b8 = pltpu.roll(b_all, N_HEADS_M, axis=0)
    a8 = g_t - b8
    cm = a8
    shift = 1
    while shift < L:
        cm = jnp.maximum(cm, jnp.where(lane >= shift, pltpu.roll(cm, shift, axis=1), NEG_BIG))
        shift *= 2
    m_prev = m_scr[...]
    mx = jnp.maximum(m_prev, cm)
    mx_last = mx[:, L - 1:L]
    decay8 = jnp.exp(m_prev - mx_last)
    m_scr[...] = b8[:, L - 1:L] + mx_last
    rows = jnp.concatenate([-mx,
                            jnp.exp(m_prev - mx),
                            jnp.exp(-(b8 + mx)),
                            jnp.exp(a8 - mx_last),
                            jnp.zeros((LANES - 32, L), F32)], axis=0)
    cols = rows.T

    o_gate = _sigmoid(o_ref[...])
    for hh in range(N_HEADS_M):
        hs = slice(hh * HEAD_DIM_M, (hh + 1) * HEAD_DIM_M)
        qb16 = qk_ref[:, hs]
        kb16 = qk_ref[:, W_M + hh * HEAD_DIM_M:W_M + (hh + 1) * HEAD_DIM_M]
        v_ext = jnp.concatenate([v_ref[:, hs], ones_col], axis=1)
        u_c = cols[:, hh:hh + 1]
        w_inter = cols[:, 8 + hh:9 + hh]
        em_c = cols[:, 16 + hh:17 + hh]
        wgt_c = cols[:, 24 + hh:25 + hh]
        cx_prev = cx_scr[hh]

        d_mat = jnp.where(causal, jnp.exp(u_c + a8[hh:hh + 1, :]), 0.0)
        s = lax.dot_general(qb16, kb16, (((1,), (1,)), ((), ())), preferred_element_type=F32) * d_mat
        intra = jnp.dot(s.astype(BF16), v_ext, preferred_element_type=F32)
        inter = jnp.dot(qb16, cx_prev.astype(BF16), preferred_element_type=F32)
        both = w_inter * inter + intra
        num = both[:, :HEAD_DIM_M]
        den = both[:, HEAD_DIM_M:HEAD_DIM_M + 1]
        hval = num / jnp.maximum(jnp.abs(den), em_c)

        kw = kb16.astype(F32) * wgt_c
        cx_scr[hh] = decay8[hh:hh + 1] * cx_prev + jnp.dot(kw.T.astype(BF16), v_ext,
                                                           preferred_element_type=F32)

        mu = jnp.mean(hval, axis=1, keepdims=True)
        cen = hval - mu
        var = jnp.mean(cen * cen, axis=1, keepdims=True)
        hn = cen * lax.rsqrt(var + EPS) * hn_ref[:, hs]
        out_ref[:, hs] = (hn * o_gate[:, hs]).astype(BF16)


def _mlstm(qk, v, o_pre, ift, gate_bias, head_norm, *, batch, seq):
    t = batch * seq
    L = MLSTM_CHUNK
    nc = seq // L
    gbt = jnp.broadcast_to(gate_bias.reshape(2 * N_HEADS_M, 1), (2 * N_HEADS_M, L))
    ift_b = ift.reshape(2 * N_HEADS_M, batch, seq).transpose(1, 0, 2)
    row = lambda w: pl.BlockSpec((batch, L, w), lambda c: (0, c, 0))
    out = pl.pallas_call(
        functools.partial(_mlstm_kernel, chunk=L, n_batch=batch),
        out_shape=jax.ShapeDtypeStruct((batch, seq, W_M), BF16),
        grid=(nc,),
        in_specs=[row(2 * W_M), row(W_M), row(W_M),
                  pl.BlockSpec((batch, 2 * N_HEADS_M, L), lambda c: (0, 0, c)),
                  _resident((2 * N_HEADS_M, L)), _resident((1, W_M))],
        out_specs=row(W_M),
        scratch_shapes=[pltpu.VMEM((batch, N_HEADS_M, HEAD_DIM_M, 2 * HEAD_DIM_M), F32),
                        pltpu.VMEM((batch, 8, 1), F32)],
        compiler_params=pltpu.CompilerParams(dimension_semantics=("arbitrary",),
                                             vmem_limit_bytes=VMEM_LIMIT),
        name="mlstm",
    )(qk.reshape(batch, seq, 2 * W_M), v.reshape(batch, seq, W_M), o_pre.reshape(batch, seq, W_M),
      ift_b, gbt, head_norm.reshape(1, -1))
    return out.reshape(t, W_M)


def kernel(x, c, ada_w, ada_b, ffn1_norm, ffn1_w1, ffn1_w3, ffn1_w2, mix_norm, w_in, conv_w, conv_b,
           kv_norm, w_uk, w_uv, mlstm_gate_bias, mlstm_head_norm, rel_bias, w_branch_attn,
           w_branch_mlstm, w_out, ffn2_norm, ffn2_w1, ffn2_w3, ffn2_w2, final_norm):
    batch, seq, d = x.shape
    depth = ada_w.shape[0]
    assert seq % max(FFN_TM, MIX_TM, MLSTM_CHUNK, KEY_BLOCK) == 0
    t = batch * seq
    xf = x.reshape(t, d)
    btile, bmax = _bias_tiles(rel_bias)
    for l in range(depth):
        mod = _adaln(c, ada_w[l], ada_b[l]).reshape(batch, 9, 1, d)
        sh1, sc1, g1, sh2, sc2, g2, sh3, sc3, g3 = [mod[:, n] for n in range(9)]
        xf = _ffn(xf, ffn1_norm[l], sh1, sc1, g1, ffn1_w1[l], ffn1_w3[l], ffn1_w2[l], final_norm,
                  seq=seq, final_norm=False)
        wuk_hdc = w_uk[l].transpose(0, 2, 1).reshape(N_HEADS_A // 2, 2, HEAD_DIM_A, D_LATENT)
        zeros = jnp.zeros_like(wuk_hdc[:, 0])
        wuk_t = jnp.concatenate([jnp.concatenate([wuk_hdc[:, 0], zeros], axis=2),
                                 jnp.concatenate([zeros, wuk_hdc[:, 1]], axis=2)], axis=1).astype(BF16)
        (q_abs, q_idx, k_idx, ckv, ckv_t, w_t, ift, qk_m, v_m, o_pre, gate_a, gate_m) = _mixin(
            xf, mix_norm[l], sh2, sc2, _pack_w_in(w_in[l], d), kv_norm[l], wuk_t, conv_w[l], conv_b[l],
            seq=seq)
        wuv_t = w_uv[l].transpose(0, 2, 1).astype(BF16)
        y_a = _dsa(q_idx, q_abs, w_t, k_idx, ckv, ckv_t, btile, bmax, wuv_t, batch=batch, seq=seq)
        h_m = _mlstm(qk_m, v_m, o_pre, ift, mlstm_gate_bias[l], mlstm_head_norm[l], batch=batch, seq=seq)
        xf = _ffn(xf, ffn2_norm[l], sh3, sc3, g3, ffn2_w1[l], ffn2_w3[l], ffn2_w2[l], final_norm,
                  seq=seq, final_norm=(l == depth - 1),
                  merge=(y_a, h_m, gate_a, gate_m, g2, w_branch_attn[l], w_branch_mlstm[l], w_out[l]))
    return xf.reshape(batch, seq, d)
```

```python
import functools
import math

import jax
import jax.numpy as jnp
from jax import lax
from jax.experimental import pallas as pl
from jax.experimental.pallas import tpu as pltpu

F32 = jnp.float32
BF16 = jnp.bfloat16
I32 = jnp.int32

LANES = 128
VMEM_LIMIT = 56 * 1024 * 1024

N_HEADS_A = 8
HEAD_DIM_A = 64
D_LATENT = 256
N_HEADS_IDX = 8
HEAD_DIM_IDX = 64
TOPK_MAX = 256
Q_BLOCK = 128
N_BUCKETS = 32
MAX_DISTANCE = 128
N_HEADS_M = 4
HEAD_DIM_M = 128
CONV_WIDTH = 4
EPS = 1e-6
IDX_SCALE = (N_HEADS_IDX ** -0.5) * (HEAD_DIM_IDX ** -0.5)
W_A = N_HEADS_A * HEAD_DIM_A
W_M = N_HEADS_M * HEAD_DIM_M

FFN_TM = 512
FFN_CHUNK = 256
MIX_TM = 512
KEY_BLOCK = 256
BLOCK_UNROLL = 3
PLANE_KEYS = 256
MLSTM_CHUNK = 256
NEG_BIG = -1e30
INT_MIN = -2 ** 31

BIAS_PAD = 2 * KEY_BLOCK - Q_BLOCK
BIAS_ROWS = KEY_BLOCK + BIAS_PAD
CKVT_ROWS = D_LATENT + 16
LOG2E = math.log2(math.e)


def _sigmoid(x):
    return 1.0 / (1.0 + jnp.exp(-x))


def _log_sigmoid(x):
    return jnp.minimum(x, 0.0) - jnp.log(1.0 + jnp.exp(-jnp.abs(x)))


def _rms_norm(x, gain):
    ms = jnp.mean(x * x, axis=-1, keepdims=True)
    return x * lax.rsqrt(ms + EPS) * gain


def _split3(x):
    hi = x.astype(BF16)
    r1 = x - hi.astype(F32)
    mid = r1.astype(BF16)
    lo = (r1 - mid.astype(F32)).astype(BF16)
    return hi, mid, lo


def _resident(shape):
    nd = len(shape)
    return pl.BlockSpec(shape, lambda *_: (0,) * nd, pipeline_mode=pl.Buffered(1))


def _adaln_kernel(c_ref, w_ref, b_ref, o_ref):
    c = c_ref[...]
    cond = c * _sigmoid(c)
    o_ref[...] = jnp.dot(cond.astype(BF16), w_ref[...].astype(BF16),
                         preferred_element_type=F32) + b_ref[...]


def _adaln(c, ada_w, ada_b):
    b, d = c.shape
    n = ada_w.shape[1]
    rows = 8
    c_pad = jnp.zeros((rows, d), F32).at[:b].set(c)
    tn = 1024
    out = pl.pallas_call(
        _adaln_kernel,
        out_shape=jax.ShapeDtypeStruct((rows, n), F32),
        grid=(n // tn,),
        in_specs=[pl.BlockSpec((rows, d), lambda j: (0, 0)),
                  pl.BlockSpec((d, tn), lambda j: (0, j)),
                  pl.BlockSpec((1, tn), lambda j: (0, j))],
        out_specs=pl.BlockSpec((rows, tn), lambda j: (0, j)),
        compiler_params=pltpu.CompilerParams(dimension_semantics=("arbitrary",),
                                             vmem_limit_bytes=VMEM_LIMIT),
        name="adaln",
    )(c_pad, ada_w, ada_b.reshape(1, n))
    return out[:b]


def _t5_bucket(dist):
    n = jnp.maximum(dist, 0)
    max_exact = N_BUCKETS // 2
    nf = jnp.maximum(n, 1).astype(F32)
    large = max_exact + (jnp.log(nf / max_exact) / math.log(MAX_DISTANCE / max_exact)
                         * (N_BUCKETS - max_exact)).astype(I32)
    large = jnp.minimum(large, N_BUCKETS - 1)
    return jnp.where(n < max_exact, n, large)


def _bias_kernel(rel_ref, tile_ref, max_ref):
    r = lax.broadcasted_iota(I32, (BIAS_ROWS, LANES), 0)
    i = lax.broadcasted_iota(I32, (BIAS_ROWS, LANES), 1)
    bucket = _t5_bucket(i - r + BIAS_PAD)
    for h in range(N_HEADS_A):
        acc = jnp.zeros((BIAS_ROWS, LANES), F32)
        top = rel_ref[0, h] * LOG2E
        for bkt in range(N_BUCKETS):
            val = rel_ref[bkt, h] * LOG2E
            acc = jnp.where(bucket == bkt, val, acc)
            top = jnp.maximum(top, val)
        tile_ref[h] = acc
        max_ref[:, h * LANES:(h + 1) * LANES] = jnp.full((1, LANES), top, F32)


def _bias_tiles(rel_bias):
    return pl.pallas_call(
        _bias_kernel,
        out_shape=(jax.ShapeDtypeStruct((N_HEADS_A, BIAS_ROWS, LANES), F32),
                   jax.ShapeDtypeStruct((1, N_HEADS_A * LANES), F32)),
        in_specs=[pl.BlockSpec(memory_space=pltpu.SMEM)],
        out_specs=(pl.BlockSpec(memory_space=pltpu.VMEM), pl.BlockSpec(memory_space=pltpu.VMEM)),
        name="bias_tiles",
    )(rel_bias)


def _ffn_kernel(*refs, n_chunks, final_norm, merge):
    if merge:
        (x_ref, ya_ref, hm_ref, ga_ref, gm_ref, gmix_ref, wa_ref, wm_ref, wo_ref), refs = refs[:9], refs[9:]
    else:
        x_ref, refs = refs[0], refs[1:]
    gain_ref, sh_ref, sc_ref, g_ref, w1_ref, w3_ref, w2_ref, fin_ref, o_ref, h_scr, acc_scr = refs
    x = x_ref[...]
    if merge:
        pa = jnp.dot(ya_ref[...], wa_ref[...], preferred_element_type=F32)
        pm = jnp.dot(hm_ref[...], wm_ref[...], preferred_element_type=F32)
        merged = _sigmoid(ga_ref[...]) * pa + _sigmoid(gm_ref[...]) * pm
        x = x + gmix_ref[0] * jnp.dot(merged.astype(BF16), wo_ref[...], preferred_element_type=F32)
    h = _rms_norm(x, gain_ref[...]) * (1.0 + sc_ref[0]) + sh_ref[0]
    h_scr[...] = h.astype(BF16)
    for j in range(n_chunks):
        hb = h_scr[...]
        cols = slice(j * FFN_CHUNK, (j + 1) * FFN_CHUNK)
        u1 = jnp.dot(hb, w1_ref[:, cols], preferred_element_type=F32)
        u3 = jnp.dot(hb, w3_ref[:, cols], preferred_element_type=F32)
        a = (u1 * _sigmoid(u1)) * u3
        part = jnp.dot(a.astype(BF16), w2_ref[j], preferred_element_type=F32)
        if j == 0:
            acc_scr[...] = part
        else:
            acc_scr[...] += part
    out = x + (0.5 * g_ref[0]) * acc_scr[...]
    if final_norm:
        out = _rms_norm(out, fin_ref[...])
    o_ref[...] = out


def _ffn(x, gain, sh, sc, g, w1, w3, w2, fin, *, seq, final_norm, merge=None):
    t, d = x.shape
    dff = w1.shape[1]
    nch = dff // FFN_CHUNK
    w1c = w1.astype(BF16)
    w3c = w3.astype(BF16)
    w2c = w2.astype(BF16).reshape(nch, FFN_CHUNK, d)
    tm = FFN_TM
    per_b = seq // tm
    row = lambda w: pl.BlockSpec((tm, w), lambda i: (i, 0))
    mod_spec = pl.BlockSpec((1, 1, d), lambda i: (i // per_b, 0, 0))
    merge_specs, merge_args = [], []
    if merge is not None:
        y_a, h_m, gate_a, gate_m, g_mix, w_a, w_m, w_o = merge
        merge_specs = [row(W_A), row(W_M), row(d), row(d), mod_spec,
                       _resident((W_A, d)), _resident((W_M, d)), _resident((d, d))]
        merge_args = [y_a, h_m, gate_a, gate_m, g_mix, w_a.astype(BF16), w_m.astype(BF16), w_o.astype(BF16)]
    return pl.pallas_call(
        functools.partial(_ffn_kernel, n_chunks=nch, final_norm=final_norm, merge=merge is not None),
        out_shape=jax.ShapeDtypeStruct((t, d), F32),
        grid=(t // tm,),
        in_specs=[row(d)] + merge_specs + [
                  _resident((1, d)), mod_spec, mod_spec, mod_spec,
                  _resident((d, dff)), _resident((d, dff)),
                  _resident((nch, FFN_CHUNK, d)), _resident((1, d))],
        out_specs=row(d),
        scratch_shapes=[pltpu.VMEM((tm, d), BF16), pltpu.VMEM((tm, d), F32)],
        compiler_params=pltpu.CompilerParams(dimension_semantics=("arbitrary",),
                                             vmem_limit_bytes=VMEM_LIMIT),
        name="ffn_final" if final_norm else "ffn",
    )(x, *merge_args, gain.reshape(1, d), sh, sc, g, w1c, w3c, w2c, fin.reshape(1, d))


_C_QA = 0
_C_CKV = _C_QA + W_A
_C_QI = _C_CKV + D_LATENT
_C_KI = _C_QI + N_HEADS_IDX * LANES
_C_SM = _C_KI + LANES
_C_QK = _C_SM + LANES
_C_V = _C_QK + 2 * W_M
_C_O = _C_V + W_M
_C_GA = _C_O + W_M
_C_GM = _C_GA + 1024
_C_END = _C_GM + 1024
_SM_W = 0
_SM_I = N_HEADS_IDX


def _pack_w_in(w_in, d_model):
    splits = (W_A, D_LATENT, N_HEADS_IDX * HEAD_DIM_IDX, HEAD_DIM_IDX, N_HEADS_IDX,
              W_M, W_M, W_M, N_HEADS_M, N_HEADS_M, W_M, d_model, d_model)
    offs = [0]
    for s in splits:
        offs.append(offs[-1] + s)
    (q_a, c_kv, q_i, k_i, w_i, q_m, k_m, v_m, i_p, f_p, o_p, g_a, g_m) = [
        w_in[:, offs[n]:offs[n + 1]] for n in range(len(splits))]
    d = w_in.shape[0]

    def pad_heads(w, nh, hd):
        w = w.reshape(d, nh, hd)
        return jnp.pad(w, ((0, 0), (0, 0), (0, LANES - hd))).reshape(d, nh * LANES)

    small = jnp.concatenate([w_i, i_p, f_p], axis=1)
    small = jnp.pad(small, ((0, 0), (0, LANES - small.shape[1])))
    packed = jnp.concatenate([
        q_a, c_kv, pad_heads(q_i, N_HEADS_IDX, HEAD_DIM_IDX),
        jnp.pad(k_i, ((0, 0), (0, LANES - HEAD_DIM_IDX))), small, q_m, k_m, v_m, o_p, g_a, g_m], axis=1)
    assert packed.shape[1] == _C_END
    return packed.astype(BF16)


def _mixin_kernel(x_ref, gain_ref, sh_ref, sc_ref, w_ref, kvn_ref, wuk_ref, cw_ref, cb_ref,
                  qabs_ref, qidx_ref, kidx_ref, ckv_ref, ckvt_ref, wt_ref, ift_ref,
                  qk_ref, v_ref, o_ref, ga_ref, gm_ref, h_scr, xe_scr, *, tm, tiles_per_seq):
    nqb = tm // Q_BLOCK

    @pl.when(pl.program_id(0) % tiles_per_seq == 0)
    def _():
        xe_scr[:8] = jnp.zeros((8, xe_scr.shape[1]), F32)

    x = x_ref[...]
    h = _rms_norm(x, gain_ref[...]) * (1.0 + sc_ref[0]) + sh_ref[0]
    h_scr[...] = h.astype(BF16)

    def proj(lo, hi):
        return jnp.dot(h_scr[...], w_ref[:, lo:hi], preferred_element_type=F32)

    qa = proj(_C_QA, _C_CKV)
    scale = HEAD_DIM_A ** -0.5 * LOG2E
    for g in range(N_HEADS_A // 2):
        q_pair = qa[:, g * LANES:(g + 1) * LANES].astype(BF16)
        q_abs = jnp.dot(q_pair, wuk_ref[g], preferred_element_type=F32) * scale
        for j in range(2):
            qabs_ref[:, 2 * g + j] = (q_abs[:, j * D_LATENT:(j + 1) * D_LATENT]
                                      .astype(BF16).reshape(nqb, Q_BLOCK, D_LATENT))
    ckv = _rms_norm(proj(_C_CKV, _C_QI), kvn_ref[...])
    ckv_ref[...] = ckv.astype(BF16)
    ckv_t = ckv.T
    ones_row = jnp.where(lax.broadcasted_iota(I32, (CKVT_ROWS - D_LATENT, KEY_BLOCK), 0) == 0, 1.0, 0.0)
    for j in range(tm // KEY_BLOCK):
        ckvt_ref[j, :D_LATENT] = ckv_t[:, j * KEY_BLOCK:(j + 1) * KEY_BLOCK].astype(BF16)
        ckvt_ref[j, D_LATENT:] = ones_row.astype(BF16)
    qi = proj(_C_QI, _C_KI)
    for hh in range(N_HEADS_IDX):
        qidx_ref[:, hh] = qi[:, hh * LANES:(hh + 1) * LANES].astype(BF16).reshape(nqb, Q_BLOCK, LANES)
    kidx_ref[...] = proj(_C_KI, _C_SM).astype(BF16)
    small_t = proj(_C_SM, _C_QK).T
    wt_ref[...] = small_t[_SM_W:_SM_W + N_HEADS_IDX] * IDX_SCALE
    ift_ref[...] = small_t[_SM_I:_SM_I + 2 * N_HEADS_M]
    xe_scr[8:] = proj(_C_QK, _C_V)
    xe = xe_scr[...]
    xq = xe[8:]
    conv = xq * cw_ref[CONV_WIDTH - 1:CONV_WIDTH, :] + cb_ref[...]
    for d in range(1, CONV_WIDTH):
        conv = conv + pltpu.roll(xe, d, axis=0)[8:] * cw_ref[CONV_WIDTH - 1 - d:CONV_WIDTH - d, :]
    xe_scr[:8] = xe_scr[tm:]
    qk = conv * _sigmoid(conv)
    qk_ref[:, :W_M] = qk[:, :W_M].astype(BF16)
    qk_ref[:, W_M:] = (qk[:, W_M:] * (HEAD_DIM_M ** -0.5)).astype(BF16)
    v_ref[...] = proj(_C_V, _C_O).astype(BF16)
    o_ref[...] = proj(_C_O, _C_GA)
    ga_ref[...] = proj(_C_GA, _C_GM)
    gm_ref[...] = proj(_C_GM, _C_END)


def _mixin(x, gain, sh, sc, w_packed, kv_norm, wuk_t, conv_w, conv_b, *, seq):
    t, d = x.shape
    tm = MIX_TM
    per_b = seq // tm
    nqb = tm // Q_BLOCK
    row = lambda w: pl.BlockSpec((tm, w), lambda i: (i, 0))
    mod_spec = pl.BlockSpec((1, 1, d), lambda i: (i // per_b, 0, 0))
    out_shape = (
        jax.ShapeDtypeStruct((t // Q_BLOCK, N_HEADS_A, Q_BLOCK, D_LATENT), BF16),
        jax.ShapeDtypeStruct((t // Q_BLOCK, N_HEADS_IDX, Q_BLOCK, LANES), BF16),
        jax.ShapeDtypeStruct((t, LANES), BF16),
        jax.ShapeDtypeStruct((t, D_LATENT), BF16),
        jax.ShapeDtypeStruct((t // KEY_BLOCK, CKVT_ROWS, KEY_BLOCK), BF16),
        jax.ShapeDtypeStruct((N_HEADS_IDX, t), F32),
        jax.ShapeDtypeStruct((2 * N_HEADS_M, t), F32),
        jax.ShapeDtypeStruct((t, 2 * W_M), BF16),
        jax.ShapeDtypeStruct((t, W_M), BF16),
        jax.ShapeDtypeStruct((t, W_M), F32),
        jax.ShapeDtypeStruct((t, d), F32),
        jax.ShapeDtypeStruct((t, d), F32),
    )
    out_specs = (
        pl.BlockSpec((nqb, N_HEADS_A, Q_BLOCK, D_LATENT), lambda i: (i, 0, 0, 0)),
        pl.BlockSpec((nqb, N_HEADS_IDX, Q_BLOCK, LANES), lambda i: (i, 0, 0, 0)),
        row(LANES), row(D_LATENT),
        pl.BlockSpec((tm // KEY_BLOCK, CKVT_ROWS, KEY_BLOCK), lambda i: (i, 0, 0)),
        pl.BlockSpec((N_HEADS_IDX, tm), lambda i: (0, i)),
        pl.BlockSpec((2 * N_HEADS_M, tm), lambda i: (0, i)),
        row(2 * W_M), row(W_M), row(W_M), row(d), row(d),
    )
    return pl.pallas_call(
        functools.partial(_mixin_kernel, tm=tm, tiles_per_seq=per_b),
        out_shape=out_shape,
        grid=(t // tm,),
        in_specs=[pl.BlockSpec((tm, d), lambda i: (i, 0)), _resident((1, d)), mod_spec, mod_spec,
                  _resident((d, _C_END)), _resident((1, D_LATENT)),
                  _resident((N_HEADS_A // 2, LANES, 2 * D_LATENT)),
                  _resident((CONV_WIDTH, 2 * W_M)), _resident((1, 2 * W_M))],
        out_specs=out_specs,
        scratch_shapes=[pltpu.VMEM((tm, d), BF16), pltpu.VMEM((tm + 8, 2 * W_M), F32)],
        compiler_params=pltpu.CompilerParams(dimension_semantics=("arbitrary",),
                                             vmem_limit_bytes=VMEM_LIMIT),
        name="mixin",
    )(x, gain.reshape(1, d), sh, sc, w_packed, kv_norm.reshape(1, D_LATENT), wuk_t,
      conv_w, conv_b.reshape(1, -1))


def _sortable_key(score):
    bits = pltpu.bitcast(score, I32)
    bits = jnp.where(bits == INT_MIN, 0, bits)
    return jnp.where(bits < 0, bits ^ 0x7FFFFFFF, bits)


def _bit_transpose32(words):
    v = list(words)
    j, m = 16, 0x0000FFFF
    while j:
        k = 0
        while k < 32:
            t = (v[k] ^ lax.shift_right_logical(v[k + j], jnp.int32(j))) & m
            v[k] = v[k] ^ t
            v[k + j] = v[k + j] ^ (t << j)
            k = (k + j + 1) & ~j
        j >>= 1
        m = (m ^ (m << j)) & 0x7FFFFFFF
    return v


def _dsa_kernel(qidx_ref, qabs_ref, wt_ref, kidx_ref, ckv_ref, ckvt_ref, btile_ref, bmax_ref, wuvt_ref,
                out_ref, keys_scr, planes_scr, cand_scr, tau_scr, acc_scr, m_scr, ltc_scr, kmax_scr,
                qta_scr, qtc_scr, *, topk, n_qb):
    kb_sz = KEY_BLOCK
    step = pl.program_id(1)
    has_c = step >= 1
    qa = jnp.minimum(step, n_qb - 1)
    qc = jnp.maximum(step - 1, 0)
    slot_a = step & 1
    slot_c = 1 - slot_a
    n_a = qa // (kb_sz // Q_BLOCK) + 1
    n_c = qc // (kb_sz // Q_BLOCK) + 1
    qa0 = qa * Q_BLOCK
    qc0 = qc * Q_BLOCK
    row_id = lax.broadcasted_iota(I32, (kb_sz, LANES), 0)
    lane_id = lax.broadcasted_iota(I32, (kb_sz, LANES), 1)
    n_blocks = keys_scr.shape[1]
    n_groups = N_HEADS_A // 2
    pair = 2 * LANES
    ones8 = jnp.ones((8, D_LATENT), BF16)

    @pl.when(step == 0)
    def _():
        keys_scr[1, 0] = jnp.full((kb_sz, LANES), INT_MIN, I32)
        tau_scr[1] = jnp.zeros((1, LANES), I32)
        planes_scr[...] = jnp.zeros(planes_scr.shape, I32)

        def kn_body(kb, mx):
            c = ckv_ref[kb].astype(F32)
            n2 = lax.dot_general(ones8, (c * c).astype(BF16), (((1,), (1,)), ((), ())),
                                 preferred_element_type=F32)
            return jnp.maximum(mx, n2[0:1])
        mx = lax.fori_loop(0, n_blocks, kn_body, jnp.zeros((1, kb_sz), F32))
        kmax_scr[...] = jnp.max(mx, axis=1, keepdims=True)

    for g in range(n_groups):
        qi_g = qidx_ref[0, 2 * g:2 * g + 2].reshape(2 * Q_BLOCK, LANES).astype(F32)
        qta_scr[g] = qi_g.T.astype(BF16)
        qa_g = qabs_ref[0, 2 * g:2 * g + 2].reshape(2 * Q_BLOCK, D_LATENT).astype(F32)
        qtc_scr[g] = qa_g.T.astype(BF16)

    def idx_dot(kb, g):
        return jnp.dot(kidx_ref[kb], qta_scr[g], preferred_element_type=F32)

    def logits(kb, g):
        return jnp.dot(ckv_ref[kb], qtc_scr[g], preferred_element_type=F32)

    def bias_start(kb):
        delta = jnp.minimum(qc0 - kb * kb_sz, BIAS_PAD)
        return pl.multiple_of(BIAS_PAD - delta, LANES)

    qn2 = []
    for g in range(n_groups):
        q_g = qabs_ref[0, 2 * g:2 * g + 2].reshape(2 * Q_BLOCK, D_LATENT).astype(F32)
        qn2.append(lax.dot_general(ones8, (q_g * q_g).astype(BF16), (((1,), (1,)), ((), ())),
                                   preferred_element_type=F32)[0:1])
    bound = jnp.sqrt(jnp.concatenate(qn2, axis=1) * kmax_scr[...]) * 1.02 + bmax_ref[...] + 1e-3
    tau_c = tau_scr[slot_c]
    w_t = wt_ref[...]

    acc_scr[...] = jnp.zeros(acc_scr.shape, F32)
    for g in range(n_groups):
        ltc_scr[:, g * pair:(g + 1) * pair] = logits(0, g)

    bound_far = bound - jnp.concatenate([btile_ref[hh, 0:1, :] for hh in range(N_HEADS_A)], axis=1)

    def block_step(kb_raw, far):
        kb = jnp.minimum(kb_raw, n_a - 1)
        kc = jnp.minimum(kb_raw, n_c - 1)
        kc_next = jnp.minimum(kb_raw + 1, n_c - 1)
        thr = jnp.where(has_c & (kb_raw < n_c), tau_c - 1, jnp.int32(2 ** 31 - 1))
        sel = keys_scr[slot_c, kc] > thr
        ct_blk = ckvt_ref[kc]
        start = bias_start(kc)
        ref_pt = bound_far if far else bound
        score = jnp.zeros((kb_sz, LANES), F32)
        for g in range(n_groups):
            s_t = idx_dot(kb, g)
            for j in range(2):
                hh = 2 * g + j
                score = score + jnp.maximum(s_t[:, j * LANES:(j + 1) * LANES], 0.0) * w_t[hh:hh + 1, :]
        for g in range(n_groups):
            lt = ltc_scr[:, g * pair:(g + 1) * pair]
            ps = []
            for j in range(2):
                hh = 2 * g + j
                piece = lt[:, j * LANES:(j + 1) * LANES]
                if not far:
                    piece = piece + btile_ref[hh, pl.ds(start, kb_sz), :]
                ps.append(jnp.exp2(jnp.where(sel, piece, NEG_BIG) - ref_pt[:, hh * LANES:(hh + 1) * LANES]))
            ltc_scr[:, g * pair:(g + 1) * pair] = logits(kc_next, g)
            acc_scr[g] += jnp.dot(ct_blk, jnp.concatenate(ps, axis=1).astype(BF16),
                                  preferred_element_type=F32)
        valid = (kb * kb_sz + row_id) <= (qa0 + lane_id)
        keys = jnp.where(valid, _sortable_key(score), INT_MIN)
        keys_scr[slot_a, kb] = keys
        v = keys ^ INT_MIN
        for sub in range(kb_sz // PLANE_KEYS):
            r0 = sub * PLANE_KEYS
            words = _bit_transpose32([v[r0 + 8 * i:r0 + 8 * (i + 1), :] for i in range(32)])
            for bit in range(32):
                planes_scr[bit, kb * (kb_sz // PLANE_KEYS) + sub] = words[31 - bit]

    def block_body(far, it, carry):
        for u in range(BLOCK_UNROLL):
            block_step(it * BLOCK_UNROLL + u, far)
        return carry

    n_far_iters = jnp.maximum(n_c - 2, 0) // BLOCK_UNROLL
    lax.fori_loop(0, n_far_iters, functools.partial(block_body, True), 0)
    lax.fori_loop(n_far_iters, (n_a + BLOCK_UNROLL - 1) // BLOCK_UNROLL,
                  functools.partial(block_body, False), 0)
    n_kb = n_a


    n_planes = planes_scr.shape[1]
    live = n_kb * (kb_sz // PLANE_KEYS)

    def radix_select(width):
        blk_id = lax.broadcasted_iota(I32, (width, 8, LANES), 0)
        cand_scr[:width] = jnp.where(blk_id < live, -1, 0)

        def bit_body(it, carry):
            above, tau_u = carry
            bit = 31 - it
            ones = cand_scr[:width] & planes_scr[bit, :width]
            c1 = jnp.sum(jnp.sum(lax.population_count(ones), axis=0), axis=0, keepdims=True)
            take = (above + c1) >= topk
            cand_scr[:width] = jnp.where(take, ones, cand_scr[:width] ^ ones)
            above = jnp.where(take, above, above + c1)
            tau_u = jnp.where(take, tau_u | (jnp.int32(1) << bit), tau_u)
            return above, tau_u

        zero = jnp.zeros((1, LANES), I32)
        above, tau_u = lax.fori_loop(0, 32, bit_body, (zero, zero))
        n_eq = jnp.sum(jnp.sum(lax.population_count(cand_scr[:width]), axis=0), axis=0, keepdims=True)
        return above, tau_u, n_eq

    widths = [n_planes * (i + 1) // 4 for i in range(4)]
    select = functools.partial(radix_select, widths[-1])
    for width in reversed(widths[:-1]):
        select = functools.partial(lax.cond, live <= width, functools.partial(radix_select, width), select)
    n_gt, tau_u, n_eq = select()
    tau = tau_u ^ INT_MIN

    need = topk - n_gt
    overflow = n_eq > need
    seq_bits = max(1, (n_blocks * kb_sz - 1).bit_length())

    @pl.when(jnp.max(jnp.where(overflow, 1, 0)) > 0)
    def _():
        def count_ties_before(trial):
            def body(kb, acc):
                hit = jnp.where((keys_scr[slot_a, kb] == tau) & ((kb * kb_sz + row_id) < trial), 1, 0)
                return acc + jnp.sum(hit.reshape(kb_sz // 8, 8, LANES), axis=0)
            acc = lax.fori_loop(0, n_kb, body, jnp.zeros((8, LANES), I32))
            return jnp.sum(acc, axis=0, keepdims=True)

        def idx_body(it, jc):
            trial = jc | (jnp.int32(1) << (seq_bits - 1 - it))
            return jnp.where(count_ties_before(trial) < need, trial, jc)

        j_cut = lax.fori_loop(0, seq_bits, idx_body, jnp.zeros((1, LANES), I32))

        def demote_body(kb, carry):
            k = keys_scr[slot_a, kb]
            drop = overflow & (k == tau) & ((kb * kb_sz + row_id) > j_cut)
            keys_scr[slot_a, kb] = jnp.where(drop, INT_MIN, k)
            return carry

        lax.fori_loop(0, n_kb, demote_body, 0)

    tau_scr[slot_a] = jnp.maximum(tau, INT_MIN + 1)

    l_min = jnp.min(jnp.concatenate([acc_scr[g, D_LATENT:D_LATENT + 1, :] for g in range(n_groups)], axis=1))

    @pl.when(has_c & jnp.logical_not(l_min >= 2.0 ** -80))
    def _():
        m_scr[...] = jnp.full(m_scr.shape, NEG_BIG, F32)
        acc_scr[...] = jnp.zeros(acc_scr.shape, F32)

        def exact_body(kb, carry):
            sel = keys_scr[slot_c, kb] >= tau_c
            ct_blk = ckvt_ref[kb]
            start = bias_start(kb)
            for g in range(n_groups):
                lt = logits(kb, g)
                ps, alphas = [], []
                for j in range(2):
                    hh = 2 * g + j
                    sl = slice(hh * LANES, (hh + 1) * LANES)
                    piece = lt[:, j * LANES:(j + 1) * LANES] + btile_ref[hh, pl.ds(start, kb_sz), :]
                    masked = jnp.where(sel, piece, NEG_BIG)
                    m_old = m_scr[:, sl]
                    m_new = jnp.maximum(m_old, jnp.max(masked, axis=0, keepdims=True))
                    m_scr[:, sl] = m_new
                    alphas.append(jnp.exp2(m_old - m_new))
                    ps.append(jnp.exp2(masked - m_new))
                pv = jnp.dot(ct_blk, jnp.concatenate(ps, axis=1).astype(BF16), preferred_element_type=F32)
                acc_scr[g] = jnp.concatenate(alphas, axis=1) * acc_scr[g] + pv
            return carry

        lax.fori_loop(0, n_c, exact_body, 0)

    @pl.when(has_c)
    def _():
        ys = []
        for hh in range(N_HEADS_A):
            acc_h = acc_scr[hh // 2, :, (hh % 2) * LANES:(hh % 2 + 1) * LANES]
            o_h = acc_h[:D_LATENT] * (1.0 / acc_h[D_LATENT:D_LATENT + 1])
            ys.append(jnp.dot(wuvt_ref[hh], o_h.astype(BF16), preferred_element_type=F32))
        y_t = jnp.concatenate(ys, axis=0)
        out_ref[...] = y_t.T.astype(BF16)


def _dsa(q_idx, q_abs, w_t, k_idx, ckv, ckv_t, btile, bmax, wuv_t, *, batch, seq):
    t = batch * seq
    nqb = seq // Q_BLOCK
    nkb = seq // KEY_BLOCK
    topk = min(TOPK_MAX, seq // 4)
    k_idx3 = k_idx.reshape(t // KEY_BLOCK, KEY_BLOCK, LANES)
    ckv3 = ckv.reshape(t // KEY_BLOCK, KEY_BLOCK, D_LATENT)
    per_batch = lambda shape: pl.BlockSpec(shape, lambda b, q: (b,) + (0,) * (len(shape) - 1),
                                           pipeline_mode=pl.Buffered(1))
    scored = lambda b, s: b * nqb + jnp.minimum(s, nqb - 1)
    attended = lambda b, s: b * nqb + jnp.maximum(s - 1, 0)
    return pl.pallas_call(
        functools.partial(_dsa_kernel, topk=topk, n_qb=nqb),
        out_shape=jax.ShapeDtypeStruct((t, W_A), BF16),
        grid=(batch, nqb + 1),
        in_specs=[pl.BlockSpec((1, N_HEADS_IDX, Q_BLOCK, LANES), lambda b, s: (scored(b, s), 0, 0, 0)),
                  pl.BlockSpec((1, N_HEADS_A, Q_BLOCK, D_LATENT), lambda b, s: (attended(b, s), 0, 0, 0)),
                  pl.BlockSpec((N_HEADS_IDX, Q_BLOCK), lambda b, s: (0, scored(b, s))),
                  per_batch((nkb, KEY_BLOCK, LANES)),
                  per_batch((nkb, KEY_BLOCK, D_LATENT)),
                  per_batch((nkb, CKVT_ROWS, KEY_BLOCK)),
                  _resident((N_HEADS_A, BIAS_ROWS, LANES)),
                  _resident((1, N_HEADS_A * LANES)),
                  _resident((N_HEADS_A, HEAD_DIM_A, D_LATENT))],
        out_specs=pl.BlockSpec((Q_BLOCK, W_A), lambda b, s: (attended(b, s), 0)),
        scratch_shapes=[pltpu.VMEM((2, nkb, KEY_BLOCK, LANES), I32),
                        pltpu.VMEM((32, seq // PLANE_KEYS + 1, 8, LANES), I32),
                        pltpu.VMEM((seq // PLANE_KEYS + 1, 8, LANES), I32),
                        pltpu.VMEM((2, 1, LANES), I32),
                        pltpu.VMEM((N_HEADS_A // 2, CKVT_ROWS, 2 * LANES), F32),
                        pltpu.VMEM((1, N_HEADS_A * LANES), F32),
                        pltpu.VMEM((KEY_BLOCK, (N_HEADS_A + 1) * LANES), F32),
                        pltpu.VMEM((1, 1), F32),
                        pltpu.VMEM((N_HEADS_IDX // 2, LANES, 2 * LANES), BF16),
                        pltpu.VMEM((N_HEADS_A // 2, D_LATENT, 2 * LANES), BF16)],
        compiler_params=pltpu.CompilerParams(dimension_semantics=("arbitrary", "arbitrary"),
                                             vmem_limit_bytes=VMEM_LIMIT),
        name="dsa",
    )(q_idx, q_abs, w_t, k_idx3, ckv3, ckv_t, btile, bmax, wuv_t)


def _mlstm_kernel(qk_ref, v_ref, o_ref, ift_ref, gbt_ref, hn_ref,
                  out_ref, cx_scr, m_scr, *, chunk, n_batch):
    L = chunk

    @pl.when(pl.program_id(0) == 0)
    def _():
        cx_scr[...] = jnp.zeros(cx_scr.shape, F32)
        m_scr[...] = jnp.zeros(m_scr.shape, F32)

    rr = lax.broadcasted_iota(I32, (L, L), 0)
    cc = lax.broadcasted_iota(I32, (L, L), 1)
    causal = cc <= rr
    triu = jnp.where(rr <= cc, 1.0, 0.0).astype(BF16)
    lane = lax.broadcasted_iota(I32, (8, L), 1)
    ones_col = jnp.where(lax.broadcasted_iota(I32, (L, HEAD_DIM_M), 1) == 0, 1.0, 0.0).astype(BF16)
    for bi in range(n_batch):
        _mlstm_chunk(qk_ref.at[bi], v_ref.at[bi], o_ref.at[bi], ift_ref.at[bi], gbt_ref, hn_ref,
                     out_ref.at[bi], cx_scr.at[bi], m_scr.at[bi], causal, triu, lane, ones_col, L)


def _mlstm_chunk(qk_ref, v_ref, o_ref, ift_ref, gbt_ref, hn_ref, out_ref, cx_scr, m_scr,
                 causal, triu, lane, ones_col, L):
    g_t = ift_ref[...] + gbt_ref[...]
    b_all = sum(jnp.dot(piece, triu, preferred_element_type=F32) for piece in _split3(_log_sigmoid(g_t)))
    b8 = pltpu.roll(b_all, N_HEADS_M, axis=0)
    a8 = g_t - b8
    cm = a8
    shift = 1
    while shift < L:
        cm = jnp.maximum(cm, jnp.where(lane >= shift, pltpu.roll(cm, shift, axis=1), NEG_BIG))
        shift *= 2
    m_prev = m_scr[...]
    mx = jnp.maximum(m_prev, cm)
    mx_last = mx[:, L - 1:L]
    decay8 = jnp.exp(m_prev - mx_last)
    m_scr[...] = b8[:, L - 1:L] + mx_last
    rows = jnp.concatenate([-mx,
                            jnp.exp(m_prev - mx),
                            jnp.exp(-(b8 + mx)),
                            jnp.exp(a8 - mx_last),
                            jnp.zeros((LANES - 32, L), F32)], axis=0)
    cols = rows.T

    o_gate = _sigmoid(o_ref[...])
    for hh in range(N_HEADS_M):
        hs = slice(hh * HEAD_DIM_M, (hh + 1) * HEAD_DIM_M)
        qb16 = qk_ref[:, hs]
        kb16 = qk_ref[:, W_M + hh * HEAD_DIM_M:W_M + (hh + 1) * HEAD_DIM_M]
        v_ext = jnp.concatenate([v_ref[:, hs], ones_col], axis=1)
        u_c = cols[:, hh:hh + 1]
        w_inter = cols[:, 8 + hh:9 + hh]
        em_c = cols[:, 16 + hh:17 + hh]
        wgt_c = cols[:, 24 + hh:25 + hh]
        cx_prev = cx_scr[hh]

        d_mat = jnp.where(causal, jnp.exp(u_c + a8[hh:hh + 1, :]), 0.0)
        s = lax.dot_general(qb16, kb16, (((1,), (1,)), ((), ())), preferred_element_type=F32) * d_mat
        intra = jnp.dot(s.astype(BF16), v_ext, preferred_element_type=F32)
        inter = jnp.dot(qb16, cx_prev.astype(BF16), preferred_element_type=F32)
        both = w_inter * inter + intra
        num = both[:, :HEAD_DIM_M]
        den = both[:, HEAD_DIM_M:HEAD_DIM_M + 1]
        hval = num / jnp.maximum(jnp.abs(den), em_c)

        kw = kb16.astype(F32) * wgt_c
        cx_scr[hh] = decay8[hh:hh + 1] * cx_prev + jnp.dot(kw.T.astype(BF16), v_ext,
                                                           preferred_element_type=F32)

        mu = jnp.mean(hval, axis=1, keepdims=True)
        cen = hval - mu
        var = jnp.mean(cen * cen, axis=1, keepdims=True)
        hn = cen * lax.rsqrt(var + EPS) * hn_ref[:, hs]
        out_ref[:, hs] = (hn * o_gate[:, hs]).astype(BF16)


def _mlstm(qk, v, o_pre, ift, gate_bias, head_norm, *, batch, seq):
    t = batch * seq
    L = MLSTM_CHUNK
    nc = seq // L
    gbt = jnp.broadcast_to(gate_bias.reshape(2 * N_HEADS_M, 1), (2 * N_HEADS_M, L))
    ift_b = ift.reshape(2 * N_HEADS_M, batch, seq).transpose(1, 0, 2)
    row = lambda w: pl.BlockSpec((batch, L, w), lambda c: (0, c, 0))
    out = pl.pallas_call(
        functools.partial(_mlstm_kernel, chunk=L, n_batch=batch),
        out_shape=jax.ShapeDtypeStruct((batch, seq, W_M), BF16),
        grid=(nc,),
        in_specs=[row(2 * W_M), row(W_M), row(W_M),
                  pl.BlockSpec((batch, 2 * N_HEADS_M, L), lambda c: (0, 0, c)),
                  _resident((2 * N_HEADS_M, L)), _resident((1, W_M))],
        out_specs=row(W_M),
        scratch_shapes=[pltpu.VMEM((batch, N_HEADS_M, HEAD_DIM_M, 2 * HEAD_DIM_M), F32),
                        pltpu.VMEM((batch, 8, 1), F32)],
        compiler_params=pltpu.CompilerParams(dimension_semantics=("arbitrary",),
                                             vmem_limit_bytes=VMEM_LIMIT),
        name="mlstm",
    )(qk.reshape(batch, seq, 2 * W_M), v.reshape(batch, seq, W_M), o_pre.reshape(batch, seq, W_M),
      ift_b, gbt, head_norm.reshape(1, -1))
    return out.reshape(t, W_M)


def kernel(x, c, ada_w, ada_b, ffn1_norm, ffn1_w1, ffn1_w3, ffn1_w2, mix_norm, w_in, conv_w, conv_b,
           kv_norm, w_uk, w_uv, mlstm_gate_bias, mlstm_head_norm, rel_bias, w_branch_attn,
           w_branch_mlstm, w_out, ffn2_norm, ffn2_w1, ffn2_w3, ffn2_w2, final_norm):
    batch, seq, d = x.shape
    depth = ada_w.shape[0]
    assert seq % max(FFN_TM, MIX_TM, MLSTM_CHUNK, KEY_BLOCK) == 0
    t = batch * seq
    xf = x.reshape(t, d)
    btile, bmax = _bias_tiles(rel_bias)
    for l in range(depth):
        mod = _adaln(c, ada_w[l], ada_b[l]).reshape(batch, 9, 1, d)
        sh1, sc1, g1, sh2, sc2, g2, sh3, sc3, g3 = [mod[:, n] for n in range(9)]
        xf = _ffn(xf, ffn1_norm[l], sh1, sc1, g1, ffn1_w1[l], ffn1_w3[l], ffn1_w2[l], final_norm,
                  seq=seq, final_norm=False)
        wuk_hdc = w_uk[l].transpose(0, 2, 1).reshape(N_HEADS_A // 2, 2, HEAD_DIM_A, D_LATENT)
        zeros = jnp.zeros_like(wuk_hdc[:, 0])
        wuk_t = jnp.concatenate([jnp.concatenate([wuk_hdc[:, 0], zeros], axis=2),
                                 jnp.concatenate([zeros, wuk_hdc[:, 1]], axis=2)], axis=1).astype(BF16)
        (q_abs, q_idx, k_idx, ckv, ckv_t, w_t, ift, qk_m, v_m, o_pre, gate_a, gate_m) = _mixin(
            xf, mix_norm[l], sh2, sc2, _pack_w_in(w_in[l], d), kv_norm[l], wuk_t, conv_w[l], conv_b[l],
            seq=seq)
        wuv_t = w_uv[l].transpose(0, 2, 1).astype(BF16)
        y_a = _dsa(q_idx, q_abs, w_t, k_idx, ckv, ckv_t, btile, bmax, wuv_t, batch=batch, seq=seq)
        h_m = _mlstm(qk_m, v_m, o_pre, ift, mlstm_gate_bias[l], mlstm_head_norm[l], batch=batch, seq=seq)
        xf = _ffn(xf, ffn2_norm[l], sh3, sc3, g3, ffn2_w1[l], ffn2_w3[l], ffn2_w2[l], final_norm,
                  seq=seq, final_norm=(l == depth - 1),
                  merge=(y_a, h_m, gate_a, gate_m, g2, w_branch_attn[l], w_branch_mlstm[l], w_out[l]))
    return xf.reshape(batch, seq, d)
```

```python
import functools
import math

import jax
import jax.numpy as jnp
from jax import lax
from jax.experimental import pallas as pl
from jax.experimental.pallas import tpu as pltpu

F32 = jnp.float32
BF16 = jnp.bfloat16
I32 = jnp.int32

LANES = 128
VMEM_LIMIT = 56 * 1024 * 1024

N_HEADS_A = 8
HEAD_DIM_A = 64
D_LATENT = 256
N_HEADS_IDX = 8
HEAD_DIM_IDX = 64
TOPK_MAX = 256
Q_BLOCK = 128
N_BUCKETS = 32
MAX_DISTANCE = 128
N_HEADS_M = 4
HEAD_DIM_M = 128
CONV_WIDTH = 4
EPS = 1e-6
IDX_SCALE = (N_HEADS_IDX ** -0.5) * (HEAD_DIM_IDX ** -0.5)
W_A = N_HEADS_A * HEAD_DIM_A
W_M = N_HEADS_M * HEAD_DIM_M

FFN_TM = 512
FFN_CHUNK = 256
MIX_TM = 512
KEY_BLOCK = 256
BLOCK_UNROLL = 3
PLANE_KEYS = 256
MLSTM_CHUNK = 256
NEG_BIG = -1e30
INT_MIN = -2 ** 31

BIAS_PAD = 2 * KEY_BLOCK - Q_BLOCK
BIAS_ROWS = KEY_BLOCK + BIAS_PAD
CKVT_ROWS = D_LATENT + 16
LOG2E = math.log2(math.e)


def _sigmoid(x):
    return 1.0 / (1.0 + jnp.exp(-x))


def _log_sigmoid(x):
    return jnp.minimum(x, 0.0) - jnp.log(1.0 + jnp.exp(-jnp.abs(x)))


def _rms_norm(x, gain):
    ms = jnp.mean(x * x, axis=-1, keepdims=True)
    return x * lax.rsqrt(ms + EPS) * gain


def _split3(x):
    hi = x.astype(BF16)
    r1 = x - hi.astype(F32)
    mid = r1.astype(BF16)
    lo = (r1 - mid.astype(F32)).astype(BF16)
    return hi, mid, lo


def _resident(shape):
    nd = len(shape)
    return pl.BlockSpec(shape, lambda *_: (0,) * nd, pipeline_mode=pl.Buffered(1))


def _adaln_kernel(c_ref, w_ref, b_ref, o_ref):
    c = c_ref[...]
    cond = c * _sigmoid(c)
    o_ref[...] = jnp.dot(cond.astype(BF16), w_ref[...].astype(BF16),
                         preferred_element_type=F32) + b_ref[...]


def _adaln(c, ada_w, ada_b):
    b, d = c.shape
    n = ada_w.shape[1]
    rows = 8
    c_pad = jnp.zeros((rows, d), F32).at[:b].set(c)
    tn = 1024
    out = pl.pallas_call(
        _adaln_kernel,
        out_shape=jax.ShapeDtypeStruct((rows, n), F32),
        grid=(n // tn,),
        in_specs=[pl.BlockSpec((rows, d), lambda j: (0, 0)),
                  pl.BlockSpec((d, tn), lambda j: (0, j)),
                  pl.BlockSpec((1, tn), lambda j: (0, j))],
        out_specs=pl.BlockSpec((rows, tn), lambda j: (0, j)),
        compiler_params=pltpu.CompilerParams(dimension_semantics=("arbitrary",),
                                             vmem_limit_bytes=VMEM_LIMIT),
        name="adaln",
    )(c_pad, ada_w, ada_b.reshape(1, n))
    return out[:b]


def _t5_bucket(dist):
    n = jnp.maximum(dist, 0)
    max_exact = N_BUCKETS // 2
    nf = jnp.maximum(n, 1).astype(F32)
    large = max_exact + (jnp.log(nf / max_exact) / math.log(MAX_DISTANCE / max_exact)
                         * (N_BUCKETS - max_exact)).astype(I32)
    large = jnp.minimum(large, N_BUCKETS - 1)
    return jnp.where(n < max_exact, n, large)


def _bias_kernel(rel_ref, tile_ref, max_ref):
    r = lax.broadcasted_iota(I32, (BIAS_ROWS, LANES), 0)
    i = lax.broadcasted_iota(I32, (BIAS_ROWS, LANES), 1)
    bucket = _t5_bucket(i - r + BIAS_PAD)
    for h in range(N_HEADS_A):
        acc = jnp.zeros((BIAS_ROWS, LANES), F32)
        top = rel_ref[0, h] * LOG2E
        for bkt in range(N_BUCKETS):
            val = rel_ref[bkt, h] * LOG2E
            acc = jnp.where(bucket == bkt, val, acc)
            top = jnp.maximum(top, val)
        tile_ref[h] = acc
        max_ref[:, h * LANES:(h + 1) * LANES] = jnp.full((1, LANES), top, F32)


def _bias_tiles(rel_bias):
    return pl.pallas_call(
        _bias_kernel,
        out_shape=(jax.ShapeDtypeStruct((N_HEADS_A, BIAS_ROWS, LANES), F32),
                   jax.ShapeDtypeStruct((1, N_HEADS_A * LANES), F32)),
        in_specs=[pl.BlockSpec(memory_space=pltpu.SMEM)],
        out_specs=(pl.BlockSpec(memory_space=pltpu.VMEM), pl.BlockSpec(memory_space=pltpu.VMEM)),
        name="bias_tiles",
    )(rel_bias)


def _ffn_kernel(*refs, n_chunks, final_norm, merge):
    if merge:
        (x_ref, ya_ref, hm_ref, ga_ref, gm_ref, gmix_ref, wa_ref, wm_ref, wo_ref), refs = refs[:9], refs[9:]
    else:
        x_ref, refs = refs[0], refs[1:]
    gain_ref, sh_ref, sc_ref, g_ref, w1_ref, w3_ref, w2_ref, fin_ref, o_ref, h_scr, acc_scr = refs
    x = x_ref[...]
    if merge:
        pa = jnp.dot(ya_ref[...], wa_ref[...], preferred_element_type=F32)
        pm = jnp.dot(hm_ref[...], wm_ref[...], preferred_element_type=F32)
        merged = _sigmoid(ga_ref[...]) * pa + _sigmoid(gm_ref[...]) * pm
        x = x + gmix_ref[0] * jnp.dot(merged.astype(BF16), wo_ref[...], preferred_element_type=F32)
    h = _rms_norm(x, gain_ref[...]) * (1.0 + sc_ref[0]) + sh_ref[0]
    h_scr[...] = h.astype(BF16)
    for j in range(n_chunks):
        hb = h_scr[...]
        cols = slice(j * FFN_CHUNK, (j + 1) * FFN_CHUNK)
        u1 = jnp.dot(hb, w1_ref[:, cols], preferred_element_type=F32)
        u3 = jnp.dot(hb, w3_ref[:, cols], preferred_element_type=F32)
        a = (u1 * _sigmoid(u1)) * u3
        part = jnp.dot(a.astype(BF16), w2_ref[j], preferred_element_type=F32)
        if j == 0:
            acc_scr[...] = part
        else:
            acc_scr[...] += part
    out = x + (0.5 * g_ref[0]) * acc_scr[...]
    if final_norm:
        out = _rms_norm(out, fin_ref[...])
    o_ref[...] = out


def _ffn(x, gain, sh, sc, g, w1, w3, w2, fin, *, seq, final_norm, merge=None):
    t, d = x.shape
    dff = w1.shape[1]
    nch = dff // FFN_CHUNK
    w1c = w1.astype(BF16)
    w3c = w3.astype(BF16)
    w2c = w2.astype(BF16).reshape(nch, FFN_CHUNK, d)
    tm = FFN_TM
    per_b = seq // tm
    row = lambda w: pl.BlockSpec((tm, w), lambda i: (i, 0))
    mod_spec = pl.BlockSpec((1, 1, d), lambda i: (i // per_b, 0, 0))
    merge_specs, merge_args = [], []
    if merge is not None:
        y_a, h_m, gate_a, gate_m, g_mix, w_a, w_m, w_o = merge
        merge_specs = [row(W_A), row(W_M), row(d), row(d), mod_spec,
                       _resident((W_A, d)), _resident((W_M, d)), _resident((d, d))]
        merge_args = [y_a, h_m, gate_a, gate_m, g_mix, w_a.astype(BF16), w_m.astype(BF16), w_o.astype(BF16)]
    return pl.pallas_call(
        functools.partial(_ffn_kernel, n_chunks=nch, final_norm=final_norm, merge=merge is not None),
        out_shape=jax.ShapeDtypeStruct((t, d), F32),
        grid=(t // tm,),
        in_specs=[row(d)] + merge_specs + [
                  _resident((1, d)), mod_spec, mod_spec, mod_spec,
                  _resident((d, dff)), _resident((d, dff)),
                  _resident((nch, FFN_CHUNK, d)), _resident((1, d))],
        out_specs=row(d),
        scratch_shapes=[pltpu.VMEM((tm, d), BF16), pltpu.VMEM((tm, d), F32)],
        compiler_params=pltpu.CompilerParams(dimension_semantics=("arbitrary",),
                                             vmem_limit_bytes=VMEM_LIMIT),
        name="ffn_final" if final_norm else "ffn",
    )(x, *merge_args, gain.reshape(1, d), sh, sc, g, w1c, w3c, w2c, fin.reshape(1, d))


_C_QA = 0
_C_CKV = _C_QA + W_A
_C_QI = _C_CKV + D_LATENT
_C_KI = _C_QI + N_HEADS_IDX * LANES
_C_SM = _C_KI + LANES
_C_QK = _C_SM + LANES
_C_V = _C_QK + 2 * W_M
_C_O = _C_V + W_M
_C_GA = _C_O + W_M
_C_GM = _C_GA + 1024
_C_END = _C_GM + 1024
_SM_W = 0
_SM_I = N_HEADS_IDX


def _pack_w_in(w_in, d_model):
    splits = (W_A, D_LATENT, N_HEADS_IDX * HEAD_DIM_IDX, HEAD_DIM_IDX, N_HEADS_IDX,
              W_M, W_M, W_M, N_HEADS_M, N_HEADS_M, W_M, d_model, d_model)
    offs = [0]
    for s in splits:
        offs.append(offs[-1] + s)
    (q_a, c_kv, q_i, k_i, w_i, q_m, k_m, v_m, i_p, f_p, o_p, g_a, g_m) = [
        w_in[:, offs[n]:offs[n + 1]] for n in range(len(splits))]
    d = w_in.shape[0]

    def pad_heads(w, nh, hd):
        w = w.reshape(d, nh, hd)
        return jnp.pad(w, ((0, 0), (0, 0), (0, LANES - hd))).reshape(d, nh * LANES)

    small = jnp.concatenate([w_i, i_p, f_p], axis=1)
    small = jnp.pad(small, ((0, 0), (0, LANES - small.shape[1])))
    packed = jnp.concatenate([
        q_a, c_kv, pad_heads(q_i, N_HEADS_IDX, HEAD_DIM_IDX),
        jnp.pad(k_i, ((0, 0), (0, LANES - HEAD_DIM_IDX))), small, q_m, k_m, v_m, o_p, g_a, g_m], axis=1)
    assert packed.shape[1] == _C_END
    return packed.astype(BF16)


def _mixin_kernel(x_ref, gain_ref, sh_ref, sc_ref, w_ref, kvn_ref, wuk_ref, cw_ref, cb_ref,
                  qabs_ref, qidx_ref, kidx_ref, ckv_ref, ckvt_ref, wt_ref, ift_ref,
                  qk_ref, v_ref, o_ref, ga_ref, gm_ref, h_scr, xe_scr, *, tm, tiles_per_seq):
    nqb = tm // Q_BLOCK

    @pl.when(pl.program_id(0) % tiles_per_seq == 0)
    def _():
        xe_scr[:8] = jnp.zeros((8, xe_scr.shape[1]), F32)

    x = x_ref[...]
    h = _rms_norm(x, gain_ref[...]) * (1.0 + sc_ref[0]) + sh_ref[0]
    h_scr[...] = h.astype(BF16)

    def proj(lo, hi):
        return jnp.dot(h_scr[...], w_ref[:, lo:hi], preferred_element_type=F32)

    qa = proj(_C_QA, _C_CKV)
    scale = HEAD_DIM_A ** -0.5 * LOG2E
    for g in range(N_HEADS_A // 2):
        q_pair = qa[:, g * LANES:(g + 1) * LANES].astype(BF16)
        q_abs = jnp.dot(q_pair, wuk_ref[g], preferred_element_type=F32) * scale
        for j in range(2):
            qabs_ref[:, 2 * g + j] = (q_abs[:, j * D_LATENT:(j + 1) * D_LATENT]
                                      .astype(BF16).reshape(nqb, Q_BLOCK, D_LATENT))
    ckv = _rms_norm(proj(_C_CKV, _C_QI), kvn_ref[...])
    ckv_ref[...] = ckv.astype(BF16)
    ckv_t = ckv.T
    ones_row = jnp.where(lax.broadcasted_iota(I32, (CKVT_ROWS - D_LATENT, KEY_BLOCK), 0) == 0, 1.0, 0.0)
    for j in range(tm // KEY_BLOCK):
        ckvt_ref[j, :D_LATENT] = ckv_t[:, j * KEY_BLOCK:(j + 1) * KEY_BLOCK].astype(BF16)
        ckvt_ref[j, D_LATENT:] = ones_row.astype(BF16)
    qi = proj(_C_QI, _C_KI)
    for hh in range(N_HEADS_IDX):
        qidx_ref[:, hh] = qi[:, hh * LANES:(hh + 1) * LANES].astype(BF16).reshape(nqb, Q_BLOCK, LANES)
    kidx_ref[...] = proj(_C_KI, _C_SM).astype(BF16)
    small_t = proj(_C_SM, _C_QK).T
    wt_ref[...] = small_t[_SM_W:_SM_W + N_HEADS_IDX] * IDX_SCALE
    ift_ref[...] = small_t[_SM_I:_SM_I + 2 * N_HEADS_M]
    xe_scr[8:] = proj(_C_QK, _C_V)
    xe = xe_scr[...]
    xq = xe[8:]
    conv = xq * cw_ref[CONV_WIDTH - 1:CONV_WIDTH, :] + cb_ref[...]
    for d in range(1, CONV_WIDTH):
        conv = conv + pltpu.roll(xe, d, axis=0)[8:] * cw_ref[CONV_WIDTH - 1 - d:CONV_WIDTH - d, :]
    xe_scr[:8] = xe_scr[tm:]
    qk = conv * _sigmoid(conv)
    qk_ref[:, :W_M] = qk[:, :W_M].astype(BF16)
    qk_ref[:, W_M:] = (qk[:, W_M:] * (HEAD_DIM_M ** -0.5)).astype(BF16)
    v_ref[...] = proj(_C_V, _C_O).astype(BF16)
    o_ref[...] = proj(_C_O, _C_GA)
    ga_ref[...] = proj(_C_GA, _C_GM)
    gm_ref[...] = proj(_C_GM, _C_END)


def _mixin(x, gain, sh, sc, w_packed, kv_norm, wuk_t, conv_w, conv_b, *, seq):
    t, d = x.shape
    tm = MIX_TM
    per_b = seq // tm
    nqb = tm // Q_BLOCK
    row = lambda w: pl.BlockSpec((tm, w), lambda i: (i, 0))
    mod_spec = pl.BlockSpec((1, 1, d), lambda i: (i // per_b, 0, 0))
    out_shape = (
        jax.ShapeDtypeStruct((t // Q_BLOCK, N_HEADS_A, Q_BLOCK, D_LATENT), BF16),
        jax.ShapeDtypeStruct((t // Q_BLOCK, N_HEADS_IDX, Q_BLOCK, LANES), BF16),
        jax.ShapeDtypeStruct((t, LANES), BF16),
        jax.ShapeDtypeStruct((t, D_LATENT), BF16),
        jax.ShapeDtypeStruct((t // KEY_BLOCK, CKVT_ROWS, KEY_BLOCK), BF16),
        jax.ShapeDtypeStruct((N_HEADS_IDX, t), F32),
        jax.ShapeDtypeStruct((2 * N_HEADS_M, t), F32),
        jax.ShapeDtypeStruct((t, 2 * W_M), BF16),
        jax.ShapeDtypeStruct((t, W_M), BF16),
        jax.ShapeDtypeStruct((t, W_M), F32),
        jax.ShapeDtypeStruct((t, d), F32),
        jax.ShapeDtypeStruct((t, d), F32),
    )
    out_specs = (
        pl.BlockSpec((nqb, N_HEADS_A, Q_BLOCK, D_LATENT), lambda i: (i, 0, 0, 0)),
        pl.BlockSpec((nqb, N_HEADS_IDX, Q_BLOCK, LANES), lambda i: (i, 0, 0, 0)),
        row(LANES), row(D_LATENT),
        pl.BlockSpec((tm // KEY_BLOCK, CKVT_ROWS, KEY_BLOCK), lambda i: (i, 0, 0)),
        pl.BlockSpec((N_HEADS_IDX, tm), lambda i: (0, i)),
        pl.BlockSpec((2 * N_HEADS_M, tm), lambda i: (0, i)),
        row(2 * W_M), row(W_M), row(W_M), row(d), row(d),
    )
    return pl.pallas_call(
        functools.partial(_mixin_kernel, tm=tm, tiles_per_seq=per_b),
        out_shape=out_shape,
        grid=(t // tm,),
        in_specs=[pl.BlockSpec((tm, d), lambda i: (i, 0)), _resident((1, d)), mod_spec, mod_spec,
                  _resident((d, _C_END)), _resident((1, D_LATENT)),
                  _resident((N_HEADS_A // 2, LANES, 2 * D_LATENT)),
                  _resident((CONV_WIDTH, 2 * W_M)), _resident((1, 2 * W_M))],
        out_specs=out_specs,
        scratch_shapes=[pltpu.VMEM((tm, d), BF16), pltpu.VMEM((tm + 8, 2 * W_M), F32)],
        compiler_params=pltpu.CompilerParams(dimension_semantics=("arbitrary",),
                                             vmem_limit_bytes=VMEM_LIMIT),
        name="mixin",
    )(x, gain.reshape(1, d), sh, sc, w_packed, kv_norm.reshape(1, D_LATENT), wuk_t,
      conv_w, conv_b.reshape(1, -1))


def _sortable_key(score):
    bits = pltpu.bitcast(score, I32)
    bits = jnp.where(bits == INT_MIN, 0, bits)
    return jnp.where(bits < 0, bits ^ 0x7FFFFFFF, bits)


def _bit_transpose32(words):
    v = list(words)
    j, m = 16, 0x0000FFFF
    while j:
        k = 0
        while k < 32:
            t = (v[k] ^ lax.shift_right_logical(v[k + j], jnp.int32(j))) & m
            v[k] = v[k] ^ t
            v[k + j] = v[k + j] ^ (t << j)
            k = (k + j + 1) & ~j
        j >>= 1
        m = (m ^ (m << j)) & 0x7FFFFFFF
    return v


def _dsa_kernel(qidx_ref, qabs_ref, wt_ref, kidx_ref, ckv_ref, ckvt_ref, btile_ref, bmax_ref, wuvt_ref,
                out_ref, keys_scr, planes_scr, cand_scr, tau_scr, acc_scr, m_scr, ltc_scr, kmax_scr,
                qta_scr, qtc_scr, *, topk, n_qb):
    kb_sz = KEY_BLOCK
    step = pl.program_id(1)
    has_c = step >= 1
    qa = jnp.minimum(step, n_qb - 1)
    qc = jnp.maximum(step - 1, 0)
    slot_a = step & 1
    slot_c = 1 - slot_a
    n_a = qa // (kb_sz // Q_BLOCK) + 1
    n_c = qc // (kb_sz // Q_BLOCK) + 1
    qa0 = qa * Q_BLOCK
    qc0 = qc * Q_BLOCK
    row_id = lax.broadcasted_iota(I32, (kb_sz, LANES), 0)
    lane_id = lax.broadcasted_iota(I32, (kb_sz, LANES), 1)
    n_blocks = keys_scr.shape[1]
    n_groups = N_HEADS_A // 2
    pair = 2 * LANES
    ones8 = jnp.ones((8, D_LATENT), BF16)

    @pl.when(step == 0)
    def _():
        keys_scr[1, 0] = jnp.full((kb_sz, LANES), INT_MIN, I32)
        tau_scr[1] = jnp.zeros((1, LANES), I32)
        planes_scr[...] = jnp.zeros(planes_scr.shape, I32)

        def kn_body(kb, mx):
            c = ckv_ref[kb].astype(F32)
            n2 = lax.dot_general(ones8, (c * c).astype(BF16), (((1,), (1,)), ((), ())),
                                 preferred_element_type=F32)
            return jnp.maximum(mx, n2[0:1])
        mx = lax.fori_loop(0, n_blocks, kn_body, jnp.zeros((1, kb_sz), F32))
        kmax_scr[...] = jnp.max(mx, axis=1, keepdims=True)

    for g in range(n_groups):
        qi_g = qidx_ref[0, 2 * g:2 * g + 2].reshape(2 * Q_BLOCK, LANES).astype(F32)
        qta_scr[g] = qi_g.T.astype(BF16)
        qa_g = qabs_ref[0, 2 * g:2 * g + 2].reshape(2 * Q_BLOCK, D_LATENT).astype(F32)
        qtc_scr[g] = qa_g.T.astype(BF16)

    def idx_dot(kb, g):
        return jnp.dot(kidx_ref[kb], qta_scr[g], preferred_element_type=F32)

    def logits(kb, g):
        return jnp.dot(ckv_ref[kb], qtc_scr[g], preferred_element_type=F32)

    def bias_start(kb):
        delta = jnp.minimum(qc0 - kb * kb_sz, BIAS_PAD)
        return pl.multiple_of(BIAS_PAD - delta, LANES)

    qn2 = []
    for g in range(n_groups):
        q_g = qabs_ref[0, 2 * g:2 * g + 2].reshape(2 * Q_BLOCK, D_LATENT).astype(F32)
        qn2.append(lax.dot_general(ones8, (q_g * q_g).astype(BF16), (((1,), (1,)), ((), ())),
                                   preferred_element_type=F32)[0:1])
    bound = jnp.sqrt(jnp.concatenate(qn2, axis=1) * kmax_scr[...]) * 1.02 + bmax_ref[...] + 1e-3
    tau_c = tau_scr[slot_c]
    w_t = wt_ref[...]

    acc_scr[...] = jnp.zeros(acc_scr.shape, F32)
    for g in range(n_groups):
        ltc_scr[:, g * pair:(g + 1) * pair] = logits(0, g)

    bound_far = bound - jnp.concatenate([btile_ref[hh, 0:1, :] for hh in range(N_HEADS_A)], axis=1)

    def block_step(kb, far):
        kc = jnp.minimum(kb, n_c - 1)
        kc_next = jnp.minimum(kb + 1, n_c - 1)
        thr = jnp.where(has_c & (kb < n_c), tau_c - 1, jnp.int32(2 ** 31 - 1))
        sel = keys_scr[slot_c, kc] > thr
        ct_blk = ckvt_ref[kc]
        start = bias_start(kc)
        ref_pt = bound_far if far else bound
        score = jnp.zeros((kb_sz, LANES), F32)
        for g in range(n_groups):
            s_t = idx_dot(kb, g)
            for j in range(2):
                hh = 2 * g + j
                score = score + jnp.maximum(s_t[:, j * LANES:(j + 1) * LANES], 0.0) * w_t[hh:hh + 1, :]
        for g in range(n_groups):
            lt = ltc_scr[:, g * pair:(g + 1) * pair]
            ps = []
            for j in range(2):
                hh = 2 * g + j
                piece = lt[:, j * LANES:(j + 1) * LANES]
                if not far:
                    piece = piece + btile_ref[hh, pl.ds(start, kb_sz), :]
                ps.append(jnp.exp2(jnp.where(sel, piece, NEG_BIG) - ref_pt[:, hh * LANES:(hh + 1) * LANES]))
            ltc_scr[:, g * pair:(g + 1) * pair] = logits(kc_next, g)
            acc_scr[g] += jnp.dot(ct_blk, jnp.concatenate(ps, axis=1).astype(BF16),
                                  preferred_element_type=F32)
        valid = (kb * kb_sz + row_id) <= (qa0 + lane_id)
        keys = jnp.where(valid, _sortable_key(score), INT_MIN)
        keys_scr[slot_a, kb] = keys
        v = keys ^ INT_MIN
        for sub in range(kb_sz // PLANE_KEYS):
            r0 = sub * PLANE_KEYS
            words = _bit_transpose32([v[r0 + 8 * i:r0 + 8 * (i + 1), :] for i in range(32)])
            for bit in range(32):
                planes_scr[bit, kb * (kb_sz // PLANE_KEYS) + sub] = words[31 - bit]

    def block_body(far, it, carry):
        for u in range(BLOCK_UNROLL):
            block_step(it * BLOCK_UNROLL + u, far)
        return carry

    n_far_iters = jnp.maximum(n_c - 2, 0) // BLOCK_UNROLL
    lax.fori_loop(0, n_far_iters, functools.partial(block_body, True), 0)

    def near_body(kb, carry):
        block_step(kb, False)
        return carry

    lax.fori_loop(n_far_iters * BLOCK_UNROLL, n_a, near_body, 0)
    n_kb = n_a


    n_planes = planes_scr.shape[1]
    live = n_kb * (kb_sz // PLANE_KEYS)

    def radix_select(width):
        blk_id = lax.broadcasted_iota(I32, (width, 8, LANES), 0)
        cand_scr[:width] = jnp.where(blk_id < live, -1, 0)

        def bit_body(it, carry):
            above, tau_u = carry
            bit = 31 - it
            ones = cand_scr[:width] & planes_scr[bit, :width]
            c1 = jnp.sum(jnp.sum(lax.population_count(ones), axis=0), axis=0, keepdims=True)
            take = (above + c1) >= topk
            cand_scr[:width] = jnp.where(take, ones, cand_scr[:width] ^ ones)
            above = jnp.where(take, above, above + c1)
            tau_u = jnp.where(take, tau_u | (jnp.int32(1) << bit), tau_u)
            return above, tau_u

        zero = jnp.zeros((1, LANES), I32)
        above, tau_u = lax.fori_loop(0, 32, bit_body, (zero, zero))
        n_eq = jnp.sum(jnp.sum(lax.population_count(cand_scr[:width]), axis=0), axis=0, keepdims=True)
        return above, tau_u, n_eq

    widths = [n_planes * (i + 1) // 4 for i in range(4)]
    select = functools.partial(radix_select, widths[-1])
    for width in reversed(widths[:-1]):
        select = functools.partial(lax.cond, live <= width, functools.partial(radix_select, width), select)
    n_gt, tau_u, n_eq = select()
    tau = tau_u ^ INT_MIN

    need = topk - n_gt
    overflow = n_eq > need
    seq_bits = max(1, (n_blocks * kb_sz - 1).bit_length())

    @pl.when(jnp.max(jnp.where(overflow, 1, 0)) > 0)
    def _():
        def count_ties_before(trial):
            def body(kb, acc):
                hit = jnp.where((keys_scr[slot_a, kb] == tau) & ((kb * kb_sz + row_id) < trial), 1, 0)
                return acc + jnp.sum(hit.reshape(kb_sz // 8, 8, LANES), axis=0)
            acc = lax.fori_loop(0, n_kb, body, jnp.zeros((8, LANES), I32))
            return jnp.sum(acc, axis=0, keepdims=True)

        def idx_body(it, jc):
            trial = jc | (jnp.int32(1) << (seq_bits - 1 - it))
            return jnp.where(count_ties_before(trial) < need, trial, jc)

        j_cut = lax.fori_loop(0, seq_bits, idx_body, jnp.zeros((1, LANES), I32))

        def demote_body(kb, carry):
            k = keys_scr[slot_a, kb]
            drop = overflow & (k == tau) & ((kb * kb_sz + row_id) > j_cut)
            keys_scr[slot_a, kb] = jnp.where(drop, INT_MIN, k)
            return carry

        lax.fori_loop(0, n_kb, demote_body, 0)

    tau_scr[slot_a] = jnp.maximum(tau, INT_MIN + 1)

    l_min = jnp.min(jnp.concatenate([acc_scr[g, D_LATENT:D_LATENT + 1, :] for g in range(n_groups)], axis=1))

    @pl.when(has_c & jnp.logical_not(l_min >= 2.0 ** -80))
    def _():
        m_scr[...] = jnp.full(m_scr.shape, NEG_BIG, F32)
        acc_scr[...] = jnp.zeros(acc_scr.shape, F32)

        def exact_body(kb, carry):
            sel = keys_scr[slot_c, kb] >= tau_c
            ct_blk = ckvt_ref[kb]
            start = bias_start(kb)
            for g in range(n_groups):
                lt = logits(kb, g)
                ps, alphas = [], []
                for j in range(2):
                    hh = 2 * g + j
                    sl = slice(hh * LANES, (hh + 1) * LANES)
                    piece = lt[:, j * LANES:(j + 1) * LANES] + btile_ref[hh, pl.ds(start, kb_sz), :]
                    masked = jnp.where(sel, piece, NEG_BIG)
                    m_old = m_scr[:, sl]
                    m_new = jnp.maximum(m_old, jnp.max(masked, axis=0, keepdims=True))
                    m_scr[:, sl] = m_new
                    alphas.append(jnp.exp2(m_old - m_new))
                    ps.append(jnp.exp2(masked - m_new))
                pv = jnp.dot(ct_blk, jnp.concatenate(ps, axis=1).astype(BF16), preferred_element_type=F32)
                acc_scr[g] = jnp.concatenate(alphas, axis=1) * acc_scr[g] + pv
            return carry

        lax.fori_loop(0, n_c, exact_body, 0)

    @pl.when(has_c)
    def _():
        ys = []
        for hh in range(N_HEADS_A):
            acc_h = acc_scr[hh // 2, :, (hh % 2) * LANES:(hh % 2 + 1) * LANES]
            o_h = acc_h[:D_LATENT] * (1.0 / acc_h[D_LATENT:D_LATENT + 1])
            ys.append(jnp.dot(wuvt_ref[hh], o_h.astype(BF16), preferred_element_type=F32))
        y_t = jnp.concatenate(ys, axis=0)
        out_ref[...] = y_t.T.astype(BF16)


def _dsa(q_idx, q_abs, w_t, k_idx, ckv, ckv_t, btile, bmax, wuv_t, *, batch, seq):
    t = batch * seq
    nqb = seq // Q_BLOCK
    nkb = seq // KEY_BLOCK
    topk = min(TOPK_MAX, seq // 4)
    k_idx3 = k_idx.reshape(t // KEY_BLOCK, KEY_BLOCK, LANES)
    ckv3 = ckv.reshape(t // KEY_BLOCK, KEY_BLOCK, D_LATENT)
    per_batch = lambda shape: pl.BlockSpec(shape, lambda b, q: (b,) + (0,) * (len(shape) - 1),
                                           pipeline_mode=pl.Buffered(1))
    scored = lambda b, s: b * nqb + jnp.minimum(s, nqb - 1)
    attended = lambda b, s: b * nqb + jnp.maximum(s - 1, 0)
    return pl.pallas_call(
        functools.partial(_dsa_kernel, topk=topk, n_qb=nqb),
        out_shape=jax.ShapeDtypeStruct((t, W_A), BF16),
        grid=(batch, nqb + 1),
        in_specs=[pl.BlockSpec((1, N_HEADS_IDX, Q_BLOCK, LANES), lambda b, s: (scored(b, s), 0, 0, 0)),
                  pl.BlockSpec((1, N_HEADS_A, Q_BLOCK, D_LATENT), lambda b, s: (attended(b, s), 0, 0, 0)),
                  pl.BlockSpec((N_HEADS_IDX, Q_BLOCK), lambda b, s: (0, scored(b, s))),
                  per_batch((nkb, KEY_BLOCK, LANES)),
                  per_batch((nkb, KEY_BLOCK, D_LATENT)),
                  per_batch((nkb, CKVT_ROWS, KEY_BLOCK)),
                  _resident((N_HEADS_A, BIAS_ROWS, LANES)),
                  _resident((1, N_HEADS_A * LANES)),
                  _resident((N_HEADS_A, HEAD_DIM_A, D_LATENT))],
        out_specs=pl.BlockSpec((Q_BLOCK, W_A), lambda b, s: (attended(b, s), 0)),
        scratch_shapes=[pltpu.VMEM((2, nkb, KEY_BLOCK, LANES), I32),
                        pltpu.VMEM((32, seq // PLANE_KEYS + 1, 8, LANES), I32),
                        pltpu.VMEM((seq // PLANE_KEYS + 1, 8, LANES), I32),
                        pltpu.VMEM((2, 1, LANES), I32),
                        pltpu.VMEM((N_HEADS_A // 2, CKVT_ROWS, 2 * LANES), F32),
                        pltpu.VMEM((1, N_HEADS_A * LANES), F32),
                        pltpu.VMEM((KEY_BLOCK, (N_HEADS_A + 1) * LANES), F32),
                        pltpu.VMEM((1, 1), F32),
                        pltpu.VMEM((N_HEADS_IDX // 2, LANES, 2 * LANES), BF16),
                        pltpu.VMEM((N_HEADS_A // 2, D_LATENT, 2 * LANES), BF16)],
        compiler_params=pltpu.CompilerParams(dimension_semantics=("arbitrary", "arbitrary"),
                                             vmem_limit_bytes=VMEM_LIMIT),
        name="dsa",
    )(q_idx, q_abs, w_t, k_idx3, ckv3, ckv_t, btile, bmax, wuv_t)


def _mlstm_kernel(qk_ref, v_ref, o_ref, ift_ref, gbt_ref, hn_ref,
                  out_ref, cx_scr, m_scr, *, chunk, n_batch):
    L = chunk

    @pl.when(pl.program_id(0) == 0)
    def _():
        cx_scr[...] = jnp.zeros(cx_scr.shape, F32)
        m_scr[...] = jnp.zeros(m_scr.shape, F32)

    rr = lax.broadcasted_iota(I32, (L, L), 0)
    cc = lax.broadcasted_iota(I32, (L, L), 1)
    causal = cc <= rr
    triu = jnp.where(rr <= cc, 1.0, 0.0).astype(BF16)
    lane = lax.broadcasted_iota(I32, (8, L), 1)
    ones_col = jnp.where(lax.broadcasted_iota(I32, (L, HEAD_DIM_M), 1) == 0, 1.0, 0.0).astype(BF16)
    for bi in range(n_batch):
        _mlstm_chunk(qk_ref.at[bi], v_ref.at[bi], o_ref.at[bi], ift_ref.at[bi], gbt_ref, hn_ref,
                     out_ref.at[bi], cx_scr.at[bi], m_scr.at[bi], causal, triu, lane, ones_col, L)


def _mlstm_chunk(qk_ref, v_ref, o_ref, ift_ref, gbt_ref, hn_ref, out_ref, cx_scr, m_scr,
                 causal, triu, lane, ones_col, L):
    g_t = ift_ref[...] + gbt_ref[...]
    b_all = sum(jnp.dot(piece, triu, preferred_element_type=F32) for piece in _split3(_log_sigmoid(g_t)))
    b8 = pltpu.roll(b_all, N_HEADS_M, axis=0)
    a8 = g_t - b8
    cm = a8
    shift = 1
    while shift < L:
        cm = jnp.maximum(cm, jnp.where(lane >= shift, pltpu.roll(cm, shift, axis=1), NEG_BIG))
        shift *= 2
    m_prev = m_scr[...]
    mx = jnp.maximum(m_prev, cm)
    mx_last = mx[:, L - 1:L]
    decay8 = jnp.exp(m_prev - mx_last)
    m_scr[...] = b8[:, L - 1:L] + mx_last
    rows = jnp.concatenate([-mx,
                            jnp.exp(m_prev - mx),
                            jnp.exp(-(b8 + mx)),
                            jnp.exp(a8 - mx_last),
                            jnp.zeros((LANES - 32, L), F32)], axis=0)
    cols = rows.T

    o_gate = _sigmoid(o_ref[...])
    for hh in range(N_HEADS_M):
        hs = slice(hh * HEAD_DIM_M, (hh + 1) * HEAD_DIM_M)
        qb16 = qk_ref[:, hs]
        kb16 = qk_ref[:, W_M + hh * HEAD_DIM_M:W_M + (hh + 1) * HEAD_DIM_M]
        v_ext = jnp.concatenate([v_ref[:, hs], ones_col], axis=1)
        u_c = cols[:, hh:hh + 1]
        w_inter = cols[:, 8 + hh:9 + hh]
        em_c = cols[:, 16 + hh:17 + hh]
        wgt_c = cols[:, 24 + hh:25 + hh]
        cx_prev = cx_scr[hh]

        d_mat = jnp.where(causal, jnp.exp(u_c + a8[hh:hh + 1, :]), 0.0)
        s = lax.dot_general(qb16, kb16, (((1,), (1,)), ((), ())), preferred_element_type=F32) * d_mat
        intra = jnp.dot(s.astype(BF16), v_ext, preferred_element_type=F32)
        inter = jnp.dot(qb16, cx_prev.astype(BF16), preferred_element_type=F32)
        both = w_inter * inter + intra
        num = both[:, :HEAD_DIM_M]
        den = both[:, HEAD_DIM_M:HEAD_DIM_M + 1]
        hval = num / jnp.maximum(jnp.abs(den), em_c)

        kw = kb16.astype(F32) * wgt_c
        cx_scr[hh] = decay8[hh:hh + 1] * cx_prev + jnp.dot(kw.T.astype(BF16), v_ext,
                                                           preferred_element_type=F32)

        mu = jnp.mean(hval, axis=1, keepdims=True)
        cen = hval - mu
        var = jnp.mean(cen * cen, axis=1, keepdims=True)
        hn = cen * lax.rsqrt(var + EPS) * hn_ref[:, hs]
        out_ref[:, hs] = (hn * o_gate[:, hs]).astype(BF16)


def _mlstm(qk, v, o_pre, ift, gate_bias, head_norm, *, batch, seq):
    t = batch * seq
    L = MLSTM_CHUNK
    nc = seq // L
    gbt = jnp.broadcast_to(gate_bias.reshape(2 * N_HEADS_M, 1), (2 * N_HEADS_M, L))
    ift_b = ift.reshape(2 * N_HEADS_M, batch, seq).transpose(1, 0, 2)
    row = lambda w: pl.BlockSpec((batch, L, w), lambda c: (0, c, 0))
    out = pl.pallas_call(
        functools.partial(_mlstm_kernel, chunk=L, n_batch=batch),
        out_shape=jax.ShapeDtypeStruct((batch, seq, W_M), BF16),
        grid=(nc,),
        in_specs=[row(2 * W_M), row(W_M), row(W_M),
                  pl.BlockSpec((batch, 2 * N_HEADS_M, L), lambda c: (0, 0, c)),
                  _resident((2 * N_HEADS_M, L)), _resident((1, W_M))],
        out_specs=row(W_M),
        scratch_shapes=[pltpu.VMEM((batch, N_HEADS_M, HEAD_DIM_M, 2 * HEAD_DIM_M), F32),
                        pltpu.VMEM((batch, 8, 1), F32)],
        compiler_params=pltpu.CompilerParams(dimension_semantics=("arbitrary",),
                                             vmem_limit_bytes=VMEM_LIMIT),
        name="mlstm",
    )(qk.reshape(batch, seq, 2 * W_M), v.reshape(batch, seq, W_M), o_pre.reshape(batch, seq, W_M),
      ift_b, gbt, head_norm.reshape(1, -1))
    return out.reshape(t, W_M)


def kernel(x, c, ada_w, ada_b, ffn1_norm, ffn1_w1, ffn1_w3, ffn1_w2, mix_norm, w_in, conv_w, conv_b,
           kv_norm, w_uk, w_uv, mlstm_gate_bias, mlstm_head_norm, rel_bias, w_branch_attn,
           w_branch_mlstm, w_out, ffn2_norm, ffn2_w1, ffn2_w3, ffn2_w2, final_norm):
    batch, seq, d = x.shape
    depth = ada_w.shape[0]
    assert seq % max(FFN_TM, MIX_TM, MLSTM_CHUNK, KEY_BLOCK) == 0
    t = batch * seq
    xf = x.reshape(t, d)
    btile, bmax = _bias_tiles(rel_bias)
    for l in range(depth):
        mod = _adaln(c, ada_w[l], ada_b[l]).reshape(batch, 9, 1, d)
        sh1, sc1, g1, sh2, sc2, g2, sh3, sc3, g3 = [mod[:, n] for n in range(9)]
        xf = _ffn(xf, ffn1_norm[l], sh1, sc1, g1, ffn1_w1[l], ffn1_w3[l], ffn1_w2[l], final_norm,
                  seq=seq, final_norm=False)
        wuk_hdc = w_uk[l].transpose(0, 2, 1).reshape(N_HEADS_A // 2, 2, HEAD_DIM_A, D_LATENT)
        zeros = jnp.zeros_like(wuk_hdc[:, 0])
        wuk_t = jnp.concatenate([jnp.concatenate([wuk_hdc[:, 0], zeros], axis=2),
                                 jnp.concatenate([zeros, wuk_hdc[:, 1]], axis=2)], axis=1).astype(BF16)
        (q_abs, q_idx, k_idx, ckv, ckv_t, w_t, ift, qk_m, v_m, o_pre, gate_a, gate_m) = _mixin(
            xf, mix_norm[l], sh2, sc2, _pack_w_in(w_in[l], d), kv_norm[l], wuk_t, conv_w[l], conv_b[l],
            seq=seq)
        wuv_t = w_uv[l].transpose(0, 2, 1).astype(BF16)
        y_a = _dsa(q_idx, q_abs, w_t, k_idx, ckv, ckv_t, btile, bmax, wuv_t, batch=batch, seq=seq)
        h_m = _mlstm(qk_m, v_m, o_pre, ift, mlstm_gate_bias[l], mlstm_head_norm[l], batch=batch, seq=seq)
        xf = _ffn(xf, ffn2_norm[l], sh3, sc3, g3, ffn2_w1[l], ffn2_w3[l], ffn2_w2[l], final_norm,
                  seq=seq, final_norm=(l == depth - 1),
                  merge=(y_a, h_m, gate_a, gate_m, g2, w_branch_attn[l], w_branch_mlstm[l], w_out[l]))
    return xf.reshape(batch, seq, d)
```

```python
import functools
import math

import jax
import jax.numpy as jnp
from jax import lax
from jax.experimental import pallas as pl
from jax.experimental.pallas import tpu as pltpu

F32 = jnp.float32
BF16 = jnp.bfloat16
I32 = jnp.int32

LANES = 128
VMEM_LIMIT = 56 * 1024 * 1024

N_HEADS_A = 8
HEAD_DIM_A = 64
D_LATENT = 256
N_HEADS_IDX = 8
HEAD_DIM_IDX = 64
TOPK_MAX = 256
Q_BLOCK = 128
N_BUCKETS = 32
MAX_DISTANCE = 128
N_HEADS_M = 4
HEAD_DIM_M = 128
CONV_WIDTH = 4
EPS = 1e-6
IDX_SCALE = (N_HEADS_IDX ** -0.5) * (HEAD_DIM_IDX ** -0.5)
W_A = N_HEADS_A * HEAD_DIM_A
W_M = N_HEADS_M * HEAD_DIM_M

FFN_TM = 512
FFN_CHUNK = 256
MIX_TM = 512
KEY_BLOCK = 256
BLOCK_UNROLL = 4
PLANE_KEYS = 256
MLSTM_CHUNK = 256
NEG_BIG = -1e30
INT_MIN = -2 ** 31

BIAS_PAD = 2 * KEY_BLOCK - Q_BLOCK
BIAS_ROWS = KEY_BLOCK + BIAS_PAD
CKVT_ROWS = D_LATENT + 16
LOG2E = math.log2(math.e)


def _sigmoid(x):
    return 1.0 / (1.0 + jnp.exp(-x))


def _log_sigmoid(x):
    return jnp.minimum(x, 0.0) - jnp.log(1.0 + jnp.exp(-jnp.abs(x)))


def _rms_norm(x, gain):
    ms = jnp.mean(x * x, axis=-1, keepdims=True)
    return x * lax.rsqrt(ms + EPS) * gain


def _split3(x):
    hi = x.astype(BF16)
    r1 = x - hi.astype(F32)
    mid = r1.astype(BF16)
    lo = (r1 - mid.astype(F32)).astype(BF16)
    return hi, mid, lo


def _resident(shape):
    nd = len(shape)
    return pl.BlockSpec(shape, lambda *_: (0,) * nd, pipeline_mode=pl.Buffered(1))


def _adaln_kernel(c_ref, w_ref, b_ref, o_ref):
    c = c_ref[...]
    cond = c * _sigmoid(c)
    o_ref[...] = jnp.dot(cond.astype(BF16), w_ref[...].astype(BF16),
                         preferred_element_type=F32) + b_ref[...]


def _adaln(c, ada_w, ada_b):
    b, d = c.shape
    n = ada_w.shape[1]
    rows = 8
    c_pad = jnp.zeros((rows, d), F32).at[:b].set(c)
    tn = 1024
    out = pl.pallas_call(
        _adaln_kernel,
        out_shape=jax.ShapeDtypeStruct((rows, n), F32),
        grid=(n // tn,),
        in_specs=[pl.BlockSpec((rows, d), lambda j: (0, 0)),
                  pl.BlockSpec((d, tn), lambda j: (0, j)),
                  pl.BlockSpec((1, tn), lambda j: (0, j))],
        out_specs=pl.BlockSpec((rows, tn), lambda j: (0, j)),
        compiler_params=pltpu.CompilerParams(dimension_semantics=("arbitrary",),
                                             vmem_limit_bytes=VMEM_LIMIT),
        name="adaln",
    )(c_pad, ada_w, ada_b.reshape(1, n))
    return out[:b]


def _t5_bucket(dist):
    n = jnp.maximum(dist, 0)
    max_exact = N_BUCKETS // 2
    nf = jnp.maximum(n, 1).astype(F32)
    large = max_exact + (jnp.log(nf / max_exact) / math.log(MAX_DISTANCE / max_exact)
                         * (N_BUCKETS - max_exact)).astype(I32)
    large = jnp.minimum(large, N_BUCKETS - 1)
    return jnp.where(n < max_exact, n, large)


def _bias_kernel(rel_ref, tile_ref, max_ref):
    r = lax.broadcasted_iota(I32, (BIAS_ROWS, LANES), 0)
    i = lax.broadcasted_iota(I32, (BIAS_ROWS, LANES), 1)
    bucket = _t5_bucket(i - r + BIAS_PAD)
    for h in range(N_HEADS_A):
        acc = jnp.zeros((BIAS_ROWS, LANES), F32)
        top = rel_ref[0, h] * LOG2E
        for bkt in range(N_BUCKETS):
            val = rel_ref[bkt, h] * LOG2E
            acc = jnp.where(bucket == bkt, val, acc)
            top = jnp.maximum(top, val)
        tile_ref[h] = acc
        max_ref[:, h * LANES:(h + 1) * LANES] = jnp.full((1, LANES), top, F32)


def _bias_tiles(rel_bias):
    return pl.pallas_call(
        _bias_kernel,
        out_shape=(jax.ShapeDtypeStruct((N_HEADS_A, BIAS_ROWS, LANES), F32),
                   jax.ShapeDtypeStruct((1, N_HEADS_A * LANES), F32)),
        in_specs=[pl.BlockSpec(memory_space=pltpu.SMEM)],
        out_specs=(pl.BlockSpec(memory_space=pltpu.VMEM), pl.BlockSpec(memory_space=pltpu.VMEM)),
        name="bias_tiles",
    )(rel_bias)


def _ffn_kernel(*refs, n_chunks, final_norm, merge):
    if merge:
        (x_ref, ya_ref, hm_ref, ga_ref, gm_ref, gmix_ref, wa_ref, wm_ref, wo_ref), refs = refs[:9], refs[9:]
    else:
        x_ref, refs = refs[0], refs[1:]
    gain_ref, sh_ref, sc_ref, g_ref, w1_ref, w3_ref, w2_ref, fin_ref, o_ref, h_scr, acc_scr = refs
    x = x_ref[...]
    if merge:
        pa = jnp.dot(ya_ref[...], wa_ref[...], preferred_element_type=F32)
        pm = jnp.dot(hm_ref[...], wm_ref[...], preferred_element_type=F32)
        merged = _sigmoid(ga_ref[...]) * pa + _sigmoid(gm_ref[...]) * pm
        x = x + gmix_ref[0] * jnp.dot(merged.astype(BF16), wo_ref[...], preferred_element_type=F32)
    h = _rms_norm(x, gain_ref[...]) * (1.0 + sc_ref[0]) + sh_ref[0]
    h_scr[...] = h.astype(BF16)
    for j in range(n_chunks):
        hb = h_scr[...]
        cols = slice(j * FFN_CHUNK, (j + 1) * FFN_CHUNK)
        u1 = jnp.dot(hb, w1_ref[:, cols], preferred_element_type=F32)
        u3 = jnp.dot(hb, w3_ref[:, cols], preferred_element_type=F32)
        a = (u1 * _sigmoid(u1)) * u3
        part = jnp.dot(a.astype(BF16), w2_ref[j], preferred_element_type=F32)
        if j == 0:
            acc_scr[...] = part
        else:
            acc_scr[...] += part
    out = x + (0.5 * g_ref[0]) * acc_scr[...]
    if final_norm:
        out = _rms_norm(out, fin_ref[...])
    o_ref[...] = out


def _ffn(x, gain, sh, sc, g, w1, w3, w2, fin, *, seq, final_norm, merge=None):
    t, d = x.shape
    dff = w1.shape[1]
    nch = dff // FFN_CHUNK
    w1c = w1.astype(BF16)
    w3c = w3.astype(BF16)
    w2c = w2.astype(BF16).reshape(nch, FFN_CHUNK, d)
    tm = FFN_TM
    per_b = seq // tm
    row = lambda w: pl.BlockSpec((tm, w), lambda i: (i, 0))
    mod_spec = pl.BlockSpec((1, 1, d), lambda i: (i // per_b, 0, 0))
    merge_specs, merge_args = [], []
    if merge is not None:
        y_a, h_m, gate_a, gate_m, g_mix, w_a, w_m, w_o = merge
        merge_specs = [row(W_A), row(W_M), row(d), row(d), mod_spec,
                       _resident((W_A, d)), _resident((W_M, d)), _resident((d, d))]
        merge_args = [y_a, h_m, gate_a, gate_m, g_mix, w_a.astype(BF16), w_m.astype(BF16), w_o.astype(BF16)]
    return pl.pallas_call(
        functools.partial(_ffn_kernel, n_chunks=nch, final_norm=final_norm, merge=merge is not None),
        out_shape=jax.ShapeDtypeStruct((t, d), F32),
        grid=(t // tm,),
        in_specs=[row(d)] + merge_specs + [
                  _resident((1, d)), mod_spec, mod_spec, mod_spec,
                  _resident((d, dff)), _resident((d, dff)),
                  _resident((nch, FFN_CHUNK, d)), _resident((1, d))],
        out_specs=row(d),
        scratch_shapes=[pltpu.VMEM((tm, d), BF16), pltpu.VMEM((tm, d), F32)],
        compiler_params=pltpu.CompilerParams(dimension_semantics=("arbitrary",),
                                             vmem_limit_bytes=VMEM_LIMIT),
        name="ffn_final" if final_norm else "ffn",
    )(x, *merge_args, gain.reshape(1, d), sh, sc, g, w1c, w3c, w2c, fin.reshape(1, d))


_C_QA = 0
_C_CKV = _C_QA + W_A
_C_QI = _C_CKV + D_LATENT
_C_KI = _C_QI + N_HEADS_IDX * LANES
_C_SM = _C_KI + LANES
_C_QK = _C_SM + LANES
_C_V = _C_QK + 2 * W_M
_C_O = _C_V + W_M
_C_GA = _C_O + W_M
_C_GM = _C_GA + 1024
_C_END = _C_GM + 1024
_SM_W = 0
_SM_I = N_HEADS_IDX


def _pack_w_in(w_in, d_model):
    splits = (W_A, D_LATENT, N_HEADS_IDX * HEAD_DIM_IDX, HEAD_DIM_IDX, N_HEADS_IDX,
              W_M, W_M, W_M, N_HEADS_M, N_HEADS_M, W_M, d_model, d_model)
    offs = [0]
    for s in splits:
        offs.append(offs[-1] + s)
    (q_a, c_kv, q_i, k_i, w_i, q_m, k_m, v_m, i_p, f_p, o_p, g_a, g_m) = [
        w_in[:, offs[n]:offs[n + 1]] for n in range(len(splits))]
    d = w_in.shape[0]

    def pad_heads(w, nh, hd):
        w = w.reshape(d, nh, hd)
        return jnp.pad(w, ((0, 0), (0, 0), (0, LANES - hd))).reshape(d, nh * LANES)

    small = jnp.concatenate([w_i, i_p, f_p], axis=1)
    small = jnp.pad(small, ((0, 0), (0, LANES - small.shape[1])))
    packed = jnp.concatenate([
        q_a, c_kv, pad_heads(q_i, N_HEADS_IDX, HEAD_DIM_IDX),
        jnp.pad(k_i, ((0, 0), (0, LANES - HEAD_DIM_IDX))), small, q_m, k_m, v_m, o_p, g_a, g_m], axis=1)
    assert packed.shape[1] == _C_END
    return packed.astype(BF16)


def _mixin_kernel(x_ref, gain_ref, sh_ref, sc_ref, w_ref, kvn_ref, wuk_ref, cw_ref, cb_ref,
                  qabs_ref, qidx_ref, kidx_ref, ckv_ref, ckvt_ref, wt_ref, ift_ref,
                  qk_ref, v_ref, o_ref, ga_ref, gm_ref, h_scr, xe_scr, *, tm, tiles_per_seq):
    nqb = tm // Q_BLOCK

    @pl.when(pl.program_id(0) % tiles_per_seq == 0)
    def _():
        xe_scr[:8] = jnp.zeros((8, xe_scr.shape[1]), F32)

    x = x_ref[...]
    h = _rms_norm(x, gain_ref[...]) * (1.0 + sc_ref[0]) + sh_ref[0]
    h_scr[...] = h.astype(BF16)

    def proj(lo, hi):
        return jnp.dot(h_scr[...], w_ref[:, lo:hi], preferred_element_type=F32)

    qa = proj(_C_QA, _C_CKV)
    scale = HEAD_DIM_A ** -0.5 * LOG2E
    for g in range(N_HEADS_A // 2):
        q_pair = qa[:, g * LANES:(g + 1) * LANES].astype(BF16)
        q_abs = jnp.dot(q_pair, wuk_ref[g], preferred_element_type=F32) * scale
        for j in range(2):
            qabs_ref[:, 2 * g + j] = (q_abs[:, j * D_LATENT:(j + 1) * D_LATENT]
                                      .astype(BF16).reshape(nqb, Q_BLOCK, D_LATENT))
    ckv = _rms_norm(proj(_C_CKV, _C_QI), kvn_ref[...])
    ckv_ref[...] = ckv.astype(BF16)
    ckv_t = ckv.T
    ones_row = jnp.where(lax.broadcasted_iota(I32, (CKVT_ROWS - D_LATENT, KEY_BLOCK), 0) == 0, 1.0, 0.0)
    for j in range(tm // KEY_BLOCK):
        ckvt_ref[j, :D_LATENT] = ckv_t[:, j * KEY_BLOCK:(j + 1) * KEY_BLOCK].astype(BF16)
        ckvt_ref[j, D_LATENT:] = ones_row.astype(BF16)
    qi = proj(_C_QI, _C_KI)
    for hh in range(N_HEADS_IDX):
        qidx_ref[:, hh] = qi[:, hh * LANES:(hh + 1) * LANES].astype(BF16).reshape(nqb, Q_BLOCK, LANES)
    kidx_ref[...] = proj(_C_KI, _C_SM).astype(BF16)
    small_t = proj(_C_SM, _C_QK).T
    wt_ref[...] = small_t[_SM_W:_SM_W + N_HEADS_IDX] * IDX_SCALE
    ift_ref[...] = small_t[_SM_I:_SM_I + 2 * N_HEADS_M]
    xe_scr[8:] = proj(_C_QK, _C_V)
    xe = xe_scr[...]
    xq = xe[8:]
    conv = xq * cw_ref[CONV_WIDTH - 1:CONV_WIDTH, :] + cb_ref[...]
    for d in range(1, CONV_WIDTH):
        conv = conv + pltpu.roll(xe, d, axis=0)[8:] * cw_ref[CONV_WIDTH - 1 - d:CONV_WIDTH - d, :]
    xe_scr[:8] = xe_scr[tm:]
    qk = conv * _sigmoid(conv)
    qk_ref[:, :W_M] = qk[:, :W_M].astype(BF16)
    qk_ref[:, W_M:] = (qk[:, W_M:] * (HEAD_DIM_M ** -0.5)).astype(BF16)
    v_ref[...] = proj(_C_V, _C_O).astype(BF16)
    o_ref[...] = proj(_C_O, _C_GA)
    ga_ref[...] = proj(_C_GA, _C_GM)
    gm_ref[...] = proj(_C_GM, _C_END)


def _mixin(x, gain, sh, sc, w_packed, kv_norm, wuk_t, conv_w, conv_b, *, seq):
    t, d = x.shape
    tm = MIX_TM
    per_b = seq // tm
    nqb = tm // Q_BLOCK
    row = lambda w: pl.BlockSpec((tm, w), lambda i: (i, 0))
    mod_spec = pl.BlockSpec((1, 1, d), lambda i: (i // per_b, 0, 0))
    out_shape = (
        jax.ShapeDtypeStruct((t // Q_BLOCK, N_HEADS_A, Q_BLOCK, D_LATENT), BF16),
        jax.ShapeDtypeStruct((t // Q_BLOCK, N_HEADS_IDX, Q_BLOCK, LANES), BF16),
        jax.ShapeDtypeStruct((t, LANES), BF16),
        jax.ShapeDtypeStruct((t, D_LATENT), BF16),
        jax.ShapeDtypeStruct((t // KEY_BLOCK, CKVT_ROWS, KEY_BLOCK), BF16),
        jax.ShapeDtypeStruct((N_HEADS_IDX, t), F32),
        jax.ShapeDtypeStruct((2 * N_HEADS_M, t), F32),
        jax.ShapeDtypeStruct((t, 2 * W_M), BF16),
        jax.ShapeDtypeStruct((t, W_M), BF16),
        jax.ShapeDtypeStruct((t, W_M), F32),
        jax.ShapeDtypeStruct((t, d), F32),
        jax.ShapeDtypeStruct((t, d), F32),
    )
    out_specs = (
        pl.BlockSpec((nqb, N_HEADS_A, Q_BLOCK, D_LATENT), lambda i: (i, 0, 0, 0)),
        pl.BlockSpec((nqb, N_HEADS_IDX, Q_BLOCK, LANES), lambda i: (i, 0, 0, 0)),
        row(LANES), row(D_LATENT),
        pl.BlockSpec((tm // KEY_BLOCK, CKVT_ROWS, KEY_BLOCK), lambda i: (i, 0, 0)),
        pl.BlockSpec((N_HEADS_IDX, tm), lambda i: (0, i)),
        pl.BlockSpec((2 * N_HEADS_M, tm), lambda i: (0, i)),
        row(2 * W_M), row(W_M), row(W_M), row(d), row(d),
    )
    return pl.pallas_call(
        functools.partial(_mixin_kernel, tm=tm, tiles_per_seq=per_b),
        out_shape=out_shape,
        grid=(t // tm,),
        in_specs=[pl.BlockSpec((tm, d), lambda i: (i, 0)), _resident((1, d)), mod_spec, mod_spec,
                  _resident((d, _C_END)), _resident((1, D_LATENT)),
                  _resident((N_HEADS_A // 2, LANES, 2 * D_LATENT)),
                  _resident((CONV_WIDTH, 2 * W_M)), _resident((1, 2 * W_M))],
        out_specs=out_specs,
        scratch_shapes=[pltpu.VMEM((tm, d), BF16), pltpu.VMEM((tm + 8, 2 * W_M), F32)],
        compiler_params=pltpu.CompilerParams(dimension_semantics=("arbitrary",),
                                             vmem_limit_bytes=VMEM_LIMIT),
        name="mixin",
    )(x, gain.reshape(1, d), sh, sc, w_packed, kv_norm.reshape(1, D_LATENT), wuk_t,
      conv_w, conv_b.reshape(1, -1))


def _sortable_key(score):
    bits = pltpu.bitcast(score, I32)
    bits = jnp.where(bits == INT_MIN, 0, bits)
    return jnp.where(bits < 0, bits ^ 0x7FFFFFFF, bits)


def _bit_transpose32(words):
    v = list(words)
    j, m = 16, 0x0000FFFF
    while j:
        k = 0
        while k < 32:
            t = (v[k] ^ lax.shift_right_logical(v[k + j], jnp.int32(j))) & m
            v[k] = v[k] ^ t
            v[k + j] = v[k + j] ^ (t << j)
            k = (k + j + 1) & ~j
        j >>= 1
        m = (m ^ (m << j)) & 0x7FFFFFFF
    return v


def _dsa_kernel(qidx_ref, qabs_ref, wt_ref, kidx_ref, ckv_ref, ckvt_ref, btile_ref, bmax_ref, wuvt_ref,
                out_ref, keys_scr, planes_scr, cand_scr, tau_scr, acc_scr, m_scr, ltc_scr, kmax_scr,
                qta_scr, qtc_scr, *, topk, n_qb):
    kb_sz = KEY_BLOCK
    step = pl.program_id(1)
    has_c = step >= 1
    qa = jnp.minimum(step, n_qb - 1)
    qc = jnp.maximum(step - 1, 0)
    slot_a = step & 1
    slot_c = 1 - slot_a
    n_a = qa // (kb_sz // Q_BLOCK) + 1
    n_c = qc // (kb_sz // Q_BLOCK) + 1
    qa0 = qa * Q_BLOCK
    qc0 = qc * Q_BLOCK
    row_id = lax.broadcasted_iota(I32, (kb_sz, LANES), 0)
    lane_id = lax.broadcasted_iota(I32, (kb_sz, LANES), 1)
    n_blocks = keys_scr.shape[1]
    n_groups = N_HEADS_A // 2
    pair = 2 * LANES
    ones8 = jnp.ones((8, D_LATENT), BF16)

    @pl.when(step == 0)
    def _():
        keys_scr[1, 0] = jnp.full((kb_sz, LANES), INT_MIN, I32)
        tau_scr[1] = jnp.zeros((1, LANES), I32)
        planes_scr[...] = jnp.zeros(planes_scr.shape, I32)

        def kn_body(kb, mx):
            c = ckv_ref[kb].astype(F32)
            n2 = lax.dot_general(ones8, (c * c).astype(BF16), (((1,), (1,)), ((), ())),
                                 preferred_element_type=F32)
            return jnp.maximum(mx, n2[0:1])
        mx = lax.fori_loop(0, n_blocks, kn_body, jnp.zeros((1, kb_sz), F32))
        kmax_scr[...] = jnp.max(mx, axis=1, keepdims=True)

    for g in range(n_groups):
        qi_g = qidx_ref[0, 2 * g:2 * g + 2].reshape(2 * Q_BLOCK, LANES).astype(F32)
        qta_scr[g] = qi_g.T.astype(BF16)
        qa_g = qabs_ref[0, 2 * g:2 * g + 2].reshape(2 * Q_BLOCK, D_LATENT).astype(F32)
        qtc_scr[g] = qa_g.T.astype(BF16)

    def idx_dot(kb, g):
        return jnp.dot(kidx_ref[kb], qta_scr[g], preferred_element_type=F32)

    def logits(kb, g):
        return jnp.dot(ckv_ref[kb], qtc_scr[g], preferred_element_type=F32)

    def bias_start(kb):
        delta = jnp.minimum(qc0 - kb * kb_sz, BIAS_PAD)
        return pl.multiple_of(BIAS_PAD - delta, LANES)

    qn2 = []
    for g in range(n_groups):
        q_g = qabs_ref[0, 2 * g:2 * g + 2].reshape(2 * Q_BLOCK, D_LATENT).astype(F32)
        qn2.append(lax.dot_general(ones8, (q_g * q_g).astype(BF16), (((1,), (1,)), ((), ())),
                                   preferred_element_type=F32)[0:1])
    bound = jnp.sqrt(jnp.concatenate(qn2, axis=1) * kmax_scr[...]) * 1.02 + bmax_ref[...] + 1e-3
    tau_c = tau_scr[slot_c]
    w_t = wt_ref[...]

    acc_scr[...] = jnp.zeros(acc_scr.shape, F32)
    for g in range(n_groups):
        ltc_scr[:, g * pair:(g + 1) * pair] = logits(0, g)

    bound_far = bound - jnp.concatenate([btile_ref[hh, 0:1, :] for hh in range(N_HEADS_A)], axis=1)

    def block_step(kb, far):
        kc = jnp.minimum(kb, n_c - 1)
        kc_next = jnp.minimum(kb + 1, n_c - 1)
        thr = jnp.where(has_c & (kb < n_c), tau_c - 1, jnp.int32(2 ** 31 - 1))
        sel = keys_scr[slot_c, kc] > thr
        ct_blk = ckvt_ref[kc]
        start = bias_start(kc)
        ref_pt = bound_far if far else bound
        score = jnp.zeros((kb_sz, LANES), F32)
        for g in range(n_groups):
            s_t = idx_dot(kb, g)
            for j in range(2):
                hh = 2 * g + j
                score = score + jnp.maximum(s_t[:, j * LANES:(j + 1) * LANES], 0.0) * w_t[hh:hh + 1, :]
        for g in range(n_groups):
            lt = ltc_scr[:, g * pair:(g + 1) * pair]
            ps = []
            for j in range(2):
                hh = 2 * g + j
                piece = lt[:, j * LANES:(j + 1) * LANES]
                if not far:
                    piece = piece + btile_ref[hh, pl.ds(start, kb_sz), :]
                ps.append(jnp.exp2(jnp.where(sel, piece, NEG_BIG) - ref_pt[:, hh * LANES:(hh + 1) * LANES]))
            ltc_scr[:, g * pair:(g + 1) * pair] = logits(kc_next, g)
            acc_scr[g] += jnp.dot(ct_blk, jnp.concatenate(ps, axis=1).astype(BF16),
                                  preferred_element_type=F32)
        valid = (kb * kb_sz + row_id) <= (qa0 + lane_id)
        keys = jnp.where(valid, _sortable_key(score), INT_MIN)
        keys_scr[slot_a, kb] = keys
        v = keys ^ INT_MIN
        for sub in range(kb_sz // PLANE_KEYS):
            r0 = sub * PLANE_KEYS
            words = _bit_transpose32([v[r0 + 8 * i:r0 + 8 * (i + 1), :] for i in range(32)])
            for bit in range(32):
                planes_scr[bit, kb * (kb_sz // PLANE_KEYS) + sub] = words[31 - bit]

    def block_body(far, it, carry):
        for u in range(BLOCK_UNROLL):
            block_step(it * BLOCK_UNROLL + u, far)
        return carry

    n_far_iters = jnp.maximum(n_c - 2, 0) // BLOCK_UNROLL
    lax.fori_loop(0, n_far_iters, functools.partial(block_body, True), 0)

    def near_body(kb, carry):
        block_step(kb, False)
        return carry

    lax.fori_loop(n_far_iters * BLOCK_UNROLL, n_a, near_body, 0)
    n_kb = n_a


    n_planes = planes_scr.shape[1]
    live = n_kb * (kb_sz // PLANE_KEYS)

    def radix_select(width):
        blk_id = lax.broadcasted_iota(I32, (width, 8, LANES), 0)
        cand_scr[:width] = jnp.where(blk_id < live, -1, 0)

        def bit_body(it, carry):
            above, tau_u = carry
            bit = 31 - it
            ones = cand_scr[:width] & planes_scr[bit, :width]
            c1 = jnp.sum(jnp.sum(lax.population_count(ones), axis=0), axis=0, keepdims=True)
            take = (above + c1) >= topk
            cand_scr[:width] = jnp.where(take, ones, cand_scr[:width] ^ ones)
            above = jnp.where(take, above, above + c1)
            tau_u = jnp.where(take, tau_u | (jnp.int32(1) << bit), tau_u)
            return above, tau_u

        zero = jnp.zeros((1, LANES), I32)
        above, tau_u = lax.fori_loop(0, 32, bit_body, (zero, zero))
        n_eq = jnp.sum(jnp.sum(lax.population_count(cand_scr[:width]), axis=0), axis=0, keepdims=True)
        return above, tau_u, n_eq

    widths = [n_planes * (i + 1) // 4 for i in range(4)]
    select = functools.partial(radix_select, widths[-1])
    for width in reversed(widths[:-1]):
        select = functools.partial(lax.cond, live <= width, functools.partial(radix_select, width), select)
    n_gt, tau_u, n_eq = select()
    tau = tau_u ^ INT_MIN

    need = topk - n_gt
    overflow = n_eq > need
    seq_bits = max(1, (n_blocks * kb_sz - 1).bit_length())

    @pl.when(jnp.max(jnp.where(overflow, 1, 0)) > 0)
    def _():
        def count_ties_before(trial):
            def body(kb, acc):
                hit = jnp.where((keys_scr[slot_a, kb] == tau) & ((kb * kb_sz + row_id) < trial), 1, 0)
                return acc + jnp.sum(hit.reshape(kb_sz // 8, 8, LANES), axis=0)
            acc = lax.fori_loop(0, n_kb, body, jnp.zeros((8, LANES), I32))
            return jnp.sum(acc, axis=0, keepdims=True)

        def idx_body(it, jc):
            trial = jc | (jnp.int32(1) << (seq_bits - 1 - it))
            return jnp.where(count_ties_before(trial) < need, trial, jc)

        j_cut = lax.fori_loop(0, seq_bits, idx_body, jnp.zeros((1, LANES), I32))

        def demote_body(kb, carry):
            k = keys_scr[slot_a, kb]
            drop = overflow & (k == tau) & ((kb * kb_sz + row_id) > j_cut)
            keys_scr[slot_a, kb] = jnp.where(drop, INT_MIN, k)
            return carry

        lax.fori_loop(0, n_kb, demote_body, 0)

    tau_scr[slot_a] = jnp.maximum(tau, INT_MIN + 1)

    l_min = jnp.min(jnp.concatenate([acc_scr[g, D_LATENT:D_LATENT + 1, :] for g in range(n_groups)], axis=1))

    @pl.when(has_c & jnp.logical_not(l_min >= 2.0 ** -80))
    def _():
        m_scr[...] = jnp.full(m_scr.shape, NEG_BIG, F32)
        acc_scr[...] = jnp.zeros(acc_scr.shape, F32)

        def exact_body(kb, carry):
            sel = keys_scr[slot_c, kb] >= tau_c
            ct_blk = ckvt_ref[kb]
            start = bias_start(kb)
            for g in range(n_groups):
                lt = logits(kb, g)
                ps, alphas = [], []
                for j in range(2):
                    hh = 2 * g + j
                    sl = slice(hh * LANES, (hh + 1) * LANES)
                    piece = lt[:, j * LANES:(j + 1) * LANES] + btile_ref[hh, pl.ds(start, kb_sz), :]
                    masked = jnp.where(sel, piece, NEG_BIG)
                    m_old = m_scr[:, sl]
                    m_new = jnp.maximum(m_old, jnp.max(masked, axis=0, keepdims=True))
                    m_scr[:, sl] = m_new
                    alphas.append(jnp.exp2(m_old - m_new))
                    ps.append(jnp.exp2(masked - m_new))
                pv = jnp.dot(ct_blk, jnp.concatenate(ps, axis=1).astype(BF16), preferred_element_type=F32)
                acc_scr[g] = jnp.concatenate(alphas, axis=1) * acc_scr[g] + pv
            return carry

        lax.fori_loop(0, n_c, exact_body, 0)

    @pl.when(has_c)
    def _():
        ys = []
        for hh in range(N_HEADS_A):
            acc_h = acc_scr[hh // 2, :, (hh % 2) * LANES:(hh % 2 + 1) * LANES]
            o_h = acc_h[:D_LATENT] * (1.0 / acc_h[D_LATENT:D_LATENT + 1])
            ys.append(jnp.dot(wuvt_ref[hh], o_h.astype(BF16), preferred_element_type=F32))
        y_t = jnp.concatenate(ys, axis=0)
        out_ref[...] = y_t.T.astype(BF16)


def _dsa(q_idx, q_abs, w_t, k_idx, ckv, ckv_t, btile, bmax, wuv_t, *, batch, seq):
    t = batch * seq
    nqb = seq // Q_BLOCK
    nkb = seq // KEY_BLOCK
    topk = min(TOPK_MAX, seq // 4)
    k_idx3 = k_idx.reshape(t // KEY_BLOCK, KEY_BLOCK, LANES)
    ckv3 = ckv.reshape(t // KEY_BLOCK, KEY_BLOCK, D_LATENT)
    per_batch = lambda shape: pl.BlockSpec(shape, lambda b, q: (b,) + (0,) * (len(shape) - 1),
                                           pipeline_mode=pl.Buffered(1))
    scored = lambda b, s: b * nqb + jnp.minimum(s, nqb - 1)
    attended = lambda b, s: b * nqb + jnp.maximum(s - 1, 0)
    return pl.pallas_call(
        functools.partial(_dsa_kernel, topk=topk, n_qb=nqb),
        out_shape=jax.ShapeDtypeStruct((t, W_A), BF16),
        grid=(batch, nqb + 1),
        in_specs=[pl.BlockSpec((1, N_HEADS_IDX, Q_BLOCK, LANES), lambda b, s: (scored(b, s), 0, 0, 0)),
                  pl.BlockSpec((1, N_HEADS_A, Q_BLOCK, D_LATENT), lambda b, s: (attended(b, s), 0, 0, 0)),
                  pl.BlockSpec((N_HEADS_IDX, Q_BLOCK), lambda b, s: (0, scored(b, s))),
                  per_batch((nkb, KEY_BLOCK, LANES)),
                  per_batch((nkb, KEY_BLOCK, D_LATENT)),
                  per_batch((nkb, CKVT_ROWS, KEY_BLOCK)),
                  _resident((N_HEADS_A, BIAS_ROWS, LANES)),
                  _resident((1, N_HEADS_A * LANES)),
                  _resident((N_HEADS_A, HEAD_DIM_A, D_LATENT))],
        out_specs=pl.BlockSpec((Q_BLOCK, W_A), lambda b, s: (attended(b, s), 0)),
        scratch_shapes=[pltpu.VMEM((2, nkb, KEY_BLOCK, LANES), I32),
                        pltpu.VMEM((32, seq // PLANE_KEYS + 1, 8, LANES), I32),
                        pltpu.VMEM((seq // PLANE_KEYS + 1, 8, LANES), I32),
                        pltpu.VMEM((2, 1, LANES), I32),
                        pltpu.VMEM((N_HEADS_A // 2, CKVT_ROWS, 2 * LANES), F32),
                        pltpu.VMEM((1, N_HEADS_A * LANES), F32),
                        pltpu.VMEM((KEY_BLOCK, (N_HEADS_A + 1) * LANES), F32),
                        pltpu.VMEM((1, 1), F32),
                        pltpu.VMEM((N_HEADS_IDX // 2, LANES, 2 * LANES), BF16),
                        pltpu.VMEM((N_HEADS_A // 2, D_LATENT, 2 * LANES), BF16)],
        compiler_params=pltpu.CompilerParams(dimension_semantics=("arbitrary", "arbitrary"),
                                             vmem_limit_bytes=VMEM_LIMIT),
        name="dsa",
    )(q_idx, q_abs, w_t, k_idx3, ckv3, ckv_t, btile, bmax, wuv_t)


def _mlstm_kernel(qk_ref, v_ref, o_ref, ift_ref, gbt_ref, hn_ref,
                  out_ref, cx_scr, m_scr, *, chunk, n_batch):
    L = chunk

    @pl.when(pl.program_id(0) == 0)
    def _():
        cx_scr[...] = jnp.zeros(cx_scr.shape, F32)
        m_scr[...] = jnp.zeros(m_scr.shape, F32)

    rr = lax.broadcasted_iota(I32, (L, L), 0)
    cc = lax.broadcasted_iota(I32, (L, L), 1)
    causal = cc <= rr
    triu = jnp.where(rr <= cc, 1.0, 0.0).astype(BF16)
    lane = lax.broadcasted_iota(I32, (8, L), 1)
    ones_col = jnp.where(lax.broadcasted_iota(I32, (L, HEAD_DIM_M), 1) == 0, 1.0, 0.0).astype(BF16)
    for bi in range(n_batch):
        _mlstm_chunk(qk_ref.at[bi], v_ref.at[bi], o_ref.at[bi], ift_ref.at[bi], gbt_ref, hn_ref,
                     out_ref.at[bi], cx_scr.at[bi], m_scr.at[bi], causal, triu, lane, ones_col, L)


def _mlstm_chunk(qk_ref, v_ref, o_ref, ift_ref, gbt_ref, hn_ref, out_ref, cx_scr, m_scr,
                 causal, triu, lane, ones_col, L):
    g_t = ift_ref[...] + gbt_ref[...]
    b_all = sum(jnp.dot(piece, triu, preferred_element_type=F32) for piece in _split3(_log_sigmoid(g_t)))
    b8 = pltpu.roll(b_all, N_HEADS_M, axis=0)
    a8 = g_t - b8
    cm = a8
    shift = 1
    while shift < L:
        cm = jnp.maximum(cm, jnp.where(lane >= shift, pltpu.roll(cm, shift, axis=1), NEG_BIG))
        shift *= 2
    m_prev = m_scr[...]
    mx = jnp.maximum(m_prev, cm)
    mx_last = mx[:, L - 1:L]
    decay8 = jnp.exp(m_prev - mx_last)
    m_scr[...] = b8[:, L - 1:L] + mx_last
    rows = jnp.concatenate([-mx,
                            jnp.exp(m_prev - mx),
                            jnp.exp(-(b8 + mx)),
                            jnp.exp(a8 - mx_last),
                            jnp.zeros((LANES - 32, L), F32)], axis=0)
    cols = rows.T

    o_gate = _sigmoid(o_ref[...])
    for hh in range(N_HEADS_M):
        hs = slice(hh * HEAD_DIM_M, (hh + 1) * HEAD_DIM_M)
        qb16 = qk_ref[:, hs]
        kb16 = qk_ref[:, W_M + hh * HEAD_DIM_M:W_M + (hh + 1) * HEAD_DIM_M]
        v_ext = jnp.concatenate([v_ref[:, hs], ones_col], axis=1)
        u_c = cols[:, hh:hh + 1]
        w_inter = cols[:, 8 + hh:9 + hh]
        em_c = cols[:, 16 + hh:17 + hh]
        wgt_c = cols[:, 24 + hh:25 + hh]
        cx_prev = cx_scr[hh]

        d_mat = jnp.where(causal, jnp.exp(u_c + a8[hh:hh + 1, :]), 0.0)
        s = lax.dot_general(qb16, kb16, (((1,), (1,)), ((), ())), preferred_element_type=F32) * d_mat
        intra = jnp.dot(s.astype(BF16), v_ext, preferred_element_type=F32)
        inter = jnp.dot(qb16, cx_prev.astype(BF16), preferred_element_type=F32)
        both = w_inter * inter + intra
        num = both[:, :HEAD_DIM_M]
        den = both[:, HEAD_DIM_M:HEAD_DIM_M + 1]
        hval = num / jnp.maximum(jnp.abs(den), em_c)

        kw = kb16.astype(F32) * wgt_c
        cx_scr[hh] = decay8[hh:hh + 1] * cx_prev + jnp.dot(kw.T.astype(BF16), v_ext,
                                                           preferred_element_type=F32)

        mu = jnp.mean(hval, axis=1, keepdims=True)
        cen = hval - mu
        var = jnp.mean(cen * cen, axis=1, keepdims=True)
        hn = cen * lax.rsqrt(var + EPS) * hn_ref[:, hs]
        out_ref[:, hs] = (hn * o_gate[:, hs]).astype(BF16)


def _mlstm(qk, v, o_pre, ift, gate_bias, head_norm, *, batch, seq):
    t = batch * seq
    L = MLSTM_CHUNK
    nc = seq // L
    gbt = jnp.broadcast_to(gate_bias.reshape(2 * N_HEADS_M, 1), (2 * N_HEADS_M, L))
    ift_b = ift.reshape(2 * N_HEADS_M, batch, seq).transpose(1, 0, 2)
    row = lambda w: pl.BlockSpec((batch, L, w), lambda c: (0, c, 0))
    out = pl.pallas_call(
        functools.partial(_mlstm_kernel, chunk=L, n_batch=batch),
        out_shape=jax.ShapeDtypeStruct((batch, seq, W_M), BF16),
        grid=(nc,),
        in_specs=[row(2 * W_M), row(W_M), row(W_M),
                  pl.BlockSpec((batch, 2 * N_HEADS_M, L), lambda c: (0, 0, c)),
                  _resident((2 * N_HEADS_M, L)), _resident((1, W_M))],
        out_specs=row(W_M),
        scratch_shapes=[pltpu.VMEM((batch, N_HEADS_M, HEAD_DIM_M, 2 * HEAD_DIM_M), F32),
                        pltpu.VMEM((batch, 8, 1), F32)],
        compiler_params=pltpu.CompilerParams(dimension_semantics=("arbitrary",),
                                             vmem_limit_bytes=VMEM_LIMIT),
        name="mlstm",
    )(qk.reshape(batch, seq, 2 * W_M), v.reshape(batch, seq, W_M), o_pre.reshape(batch, seq, W_M),
      ift_b, gbt, head_norm.reshape(1, -1))
    return out.reshape(t, W_M)


def kernel(x, c, ada_w, ada_b, ffn1_norm, ffn1_w1, ffn1_w3, ffn1_w2, mix_norm, w_in, conv_w, conv_b,
           kv_norm, w_uk, w_uv, mlstm_gate_bias, mlstm_head_norm, rel_bias, w_branch_attn,
           w_branch_mlstm, w_out, ffn2_norm, ffn2_w1, ffn2_w3, ffn2_w2, final_norm):
    batch, seq, d = x.shape
    depth = ada_w.shape[0]
    assert seq % max(FFN_TM, MIX_TM, MLSTM_CHUNK, KEY_BLOCK) == 0
    t = batch * seq
    xf = x.reshape(t, d)
    btile, bmax = _bias_tiles(rel_bias)
    for l in range(depth):
        mod = _adaln(c, ada_w[l], ada_b[l]).reshape(batch, 9, 1, d)
        sh1, sc1, g1, sh2, sc2, g2, sh3, sc3, g3 = [mod[:, n] for n in range(9)]
        xf = _ffn(xf, ffn1_norm[l], sh1, sc1, g1, ffn1_w1[l], ffn1_w3[l], ffn1_w2[l], final_norm,
                  seq=seq, final_norm=False)
        wuk_hdc = w_uk[l].transpose(0, 2, 1).reshape(N_HEADS_A // 2, 2, HEAD_DIM_A, D_LATENT)
        zeros = jnp.zeros_like(wuk_hdc[:, 0])
        wuk_t = jnp.concatenate([jnp.concatenate([wuk_hdc[:, 0], zeros], axis=2),
                                 jnp.concatenate([zeros, wuk_hdc[:, 1]], axis=2)], axis=1).astype(BF16)
        (q_abs, q_idx, k_idx, ckv, ckv_t, w_t, ift, qk_m, v_m, o_pre, gate_a, gate_m) = _mixin(
            xf, mix_norm[l], sh2, sc2, _pack_w_in(w_in[l], d), kv_norm[l], wuk_t, conv_w[l], conv_b[l],
            seq=seq)
        wuv_t = w_uv[l].transpose(0, 2, 1).astype(BF16)
        y_a = _dsa(q_idx, q_abs, w_t, k_idx, ckv, ckv_t, btile, bmax, wuv_t, batch=batch, seq=seq)
        h_m = _mlstm(qk_m, v_m, o_pre, ift, mlstm_gate_bias[l], mlstm_head_norm[l], batch=batch, seq=seq)
        xf = _ffn(xf, ffn2_norm[l], sh3, sc3, g3, ffn2_w1[l], ffn2_w3[l], ffn2_w2[l], final_norm,
                  seq=seq, final_norm=(l == depth - 1),
                  merge=(y_a, h_m, gate_a, gate_m, g2, w_branch_attn[l], w_branch_mlstm[l], w_out[l]))
    return xf.reshape(batch, seq, d)
```

```python
import functools
import math

import jax
import jax.numpy as jnp
from jax import lax
from jax.experimental import pallas as pl
from jax.experimental.pallas import tpu as pltpu

F32 = jnp.float32
BF16 = jnp.bfloat16
I32 = jnp.int32

LANES = 128
VMEM_LIMIT = 56 * 1024 * 1024

N_HEADS_A = 8
HEAD_DIM_A = 64
D_LATENT = 256
N_HEADS_IDX = 8
HEAD_DIM_IDX = 64
TOPK_MAX = 256
Q_BLOCK = 128
N_BUCKETS = 32
MAX_DISTANCE = 128
N_HEADS_M = 4
HEAD_DIM_M = 128
CONV_WIDTH = 4
EPS = 1e-6
IDX_SCALE = (N_HEADS_IDX ** -0.5) * (HEAD_DIM_IDX ** -0.5)
W_A = N_HEADS_A * HEAD_DIM_A
W_M = N_HEADS_M * HEAD_DIM_M

FFN_TM = 512
FFN_CHUNK = 256
MIX_TM = 512
KEY_BLOCK = 256
BLOCK_UNROLL = 6
PLANE_KEYS = 256
MLSTM_CHUNK = 256
NEG_BIG = -1e30
INT_MIN = -2 ** 31

BIAS_PAD = 2 * KEY_BLOCK - Q_BLOCK
BIAS_ROWS = KEY_BLOCK + BIAS_PAD
CKVT_ROWS = D_LATENT + 16
LOG2E = math.log2(math.e)


def _sigmoid(x):
    return 1.0 / (1.0 + jnp.exp(-x))


def _log_sigmoid(x):
    return jnp.minimum(x, 0.0) - jnp.log(1.0 + jnp.exp(-jnp.abs(x)))


def _rms_norm(x, gain):
    ms = jnp.mean(x * x, axis=-1, keepdims=True)
    return x * lax.rsqrt(ms + EPS) * gain


def _split3(x):
    hi = x.astype(BF16)
    r1 = x - hi.astype(F32)
    mid = r1.astype(BF16)
    lo = (r1 - mid.astype(F32)).astype(BF16)
    return hi, mid, lo


def _resident(shape):
    nd = len(shape)
    return pl.BlockSpec(shape, lambda *_: (0,) * nd, pipeline_mode=pl.Buffered(1))


def _adaln_kernel(c_ref, w_ref, b_ref, o_ref):
    c = c_ref[...]
    cond = c * _sigmoid(c)
    o_ref[...] = jnp.dot(cond.astype(BF16), w_ref[...].astype(BF16),
                         preferred_element_type=F32) + b_ref[...]


def _adaln(c, ada_w, ada_b):
    b, d = c.shape
    n = ada_w.shape[1]
    rows = 8
    c_pad = jnp.zeros((rows, d), F32).at[:b].set(c)
    tn = 1024
    out = pl.pallas_call(
        _adaln_kernel,
        out_shape=jax.ShapeDtypeStruct((rows, n), F32),
        grid=(n // tn,),
        in_specs=[pl.BlockSpec((rows, d), lambda j: (0, 0)),
                  pl.BlockSpec((d, tn), lambda j: (0, j)),
                  pl.BlockSpec((1, tn), lambda j: (0, j))],
        out_specs=pl.BlockSpec((rows, tn), lambda j: (0, j)),
        compiler_params=pltpu.CompilerParams(dimension_semantics=("arbitrary",),
                                             vmem_limit_bytes=VMEM_LIMIT),
        name="adaln",
    )(c_pad, ada_w, ada_b.reshape(1, n))
    return out[:b]


def _t5_bucket(dist):
    n = jnp.maximum(dist, 0)
    max_exact = N_BUCKETS // 2
    nf = jnp.maximum(n, 1).astype(F32)
    large = max_exact + (jnp.log(nf / max_exact) / math.log(MAX_DISTANCE / max_exact)
                         * (N_BUCKETS - max_exact)).astype(I32)
    large = jnp.minimum(large, N_BUCKETS - 1)
    return jnp.where(n < max_exact, n, large)


def _bias_kernel(rel_ref, tile_ref, max_ref):
    r = lax.broadcasted_iota(I32, (BIAS_ROWS, LANES), 0)
    i = lax.broadcasted_iota(I32, (BIAS_ROWS, LANES), 1)
    bucket = _t5_bucket(i - r + BIAS_PAD)
    for h in range(N_HEADS_A):
        acc = jnp.zeros((BIAS_ROWS, LANES), F32)
        top = rel_ref[0, h] * LOG2E
        for bkt in range(N_BUCKETS):
            val = rel_ref[bkt, h] * LOG2E
            acc = jnp.where(bucket == bkt, val, acc)
            top = jnp.maximum(top, val)
        tile_ref[h] = acc
        max_ref[:, h * LANES:(h + 1) * LANES] = jnp.full((1, LANES), top, F32)


def _bias_tiles(rel_bias):
    return pl.pallas_call(
        _bias_kernel,
        out_shape=(jax.ShapeDtypeStruct((N_HEADS_A, BIAS_ROWS, LANES), F32),
                   jax.ShapeDtypeStruct((1, N_HEADS_A * LANES), F32)),
        in_specs=[pl.BlockSpec(memory_space=pltpu.SMEM)],
        out_specs=(pl.BlockSpec(memory_space=pltpu.VMEM), pl.BlockSpec(memory_space=pltpu.VMEM)),
        name="bias_tiles",
    )(rel_bias)


def _ffn_kernel(*refs, n_chunks, final_norm, merge):
    if merge:
        (x_ref, ya_ref, hm_ref, ga_ref, gm_ref, gmix_ref, wa_ref, wm_ref, wo_ref), refs = refs[:9], refs[9:]
    else:
        x_ref, refs = refs[0], refs[1:]
    gain_ref, sh_ref, sc_ref, g_ref, w1_ref, w3_ref, w2_ref, fin_ref, o_ref, h_scr, acc_scr = refs
    x = x_ref[...]
    if merge:
        pa = jnp.dot(ya_ref[...], wa_ref[...], preferred_element_type=F32)
        pm = jnp.dot(hm_ref[...], wm_ref[...], preferred_element_type=F32)
        merged = _sigmoid(ga_ref[...]) * pa + _sigmoid(gm_ref[...]) * pm
        x = x + gmix_ref[0] * jnp.dot(merged.astype(BF16), wo_ref[...], preferred_element_type=F32)
    h = _rms_norm(x, gain_ref[...]) * (1.0 + sc_ref[0]) + sh_ref[0]
    h_scr[...] = h.astype(BF16)
    for j in range(n_chunks):
        hb = h_scr[...]
        cols = slice(j * FFN_CHUNK, (j + 1) * FFN_CHUNK)
        u1 = jnp.dot(hb, w1_ref[:, cols], preferred_element_type=F32)
        u3 = jnp.dot(hb, w3_ref[:, cols], preferred_element_type=F32)
        a = (u1 * _sigmoid(u1)) * u3
        part = jnp.dot(a.astype(BF16), w2_ref[j], preferred_element_type=F32)
        if j == 0:
            acc_scr[...] = part
        else:
            acc_scr[...] += part
    out = x + (0.5 * g_ref[0]) * acc_scr[...]
    if final_norm:
        out = _rms_norm(out, fin_ref[...])
    o_ref[...] = out


def _ffn(x, gain, sh, sc, g, w1, w3, w2, fin, *, seq, final_norm, merge=None):
    t, d = x.shape
    dff = w1.shape[1]
    nch = dff // FFN_CHUNK
    w1c = w1.astype(BF16)
    w3c = w3.astype(BF16)
    w2c = w2.astype(BF16).reshape(nch, FFN_CHUNK, d)
    tm = FFN_TM
    per_b = seq // tm
    row = lambda w: pl.BlockSpec((tm, w), lambda i: (i, 0))
    mod_spec = pl.BlockSpec((1, 1, d), lambda i: (i // per_b, 0, 0))
    merge_specs, merge_args = [], []
    if merge is not None:
        y_a, h_m, gate_a, gate_m, g_mix, w_a, w_m, w_o = merge
        merge_specs = [row(W_A), row(W_M), row(d), row(d), mod_spec,
                       _resident((W_A, d)), _resident((W_M, d)), _resident((d, d))]
        merge_args = [y_a, h_m, gate_a, gate_m, g_mix, w_a.astype(BF16), w_m.astype(BF16), w_o.astype(BF16)]
    return pl.pallas_call(
        functools.partial(_ffn_kernel, n_chunks=nch, final_norm=final_norm, merge=merge is not None),
        out_shape=jax.ShapeDtypeStruct((t, d), F32),
        grid=(t // tm,),
        in_specs=[row(d)] + merge_specs + [
                  _resident((1, d)), mod_spec, mod_spec, mod_spec,
                  _resident((d, dff)), _resident((d, dff)),
                  _resident((nch, FFN_CHUNK, d)), _resident((1, d))],
        out_specs=row(d),
        scratch_shapes=[pltpu.VMEM((tm, d), BF16), pltpu.VMEM((tm, d), F32)],
        compiler_params=pltpu.CompilerParams(dimension_semantics=("arbitrary",),
                                             vmem_limit_bytes=VMEM_LIMIT),
        name="ffn_final" if final_norm else "ffn",
    )(x, *merge_args, gain.reshape(1, d), sh, sc, g, w1c, w3c, w2c, fin.reshape(1, d))


_C_QA = 0
_C_CKV = _C_QA + W_A
_C_QI = _C_CKV + D_LATENT
_C_KI = _C_QI + N_HEADS_IDX * LANES
_C_SM = _C_KI + LANES
_C_QK = _C_SM + LANES
_C_V = _C_QK + 2 * W_M
_C_O = _C_V + W_M
_C_GA = _C_O + W_M
_C_GM = _C_GA + 1024
_C_END = _C_GM + 1024
_SM_W = 0
_SM_I = N_HEADS_IDX


def _pack_w_in(w_in, d_model):
    splits = (W_A, D_LATENT, N_HEADS_IDX * HEAD_DIM_IDX, HEAD_DIM_IDX, N_HEADS_IDX,
              W_M, W_M, W_M, N_HEADS_M, N_HEADS_M, W_M, d_model, d_model)
    offs = [0]
    for s in splits:
        offs.append(offs[-1] + s)
    (q_a, c_kv, q_i, k_i, w_i, q_m, k_m, v_m, i_p, f_p, o_p, g_a, g_m) = [
        w_in[:, offs[n]:offs[n + 1]] for n in range(len(splits))]
    d = w_in.shape[0]

    def pad_heads(w, nh, hd):
        w = w.reshape(d, nh, hd)
        return jnp.pad(w, ((0, 0), (0, 0), (0, LANES - hd))).reshape(d, nh * LANES)

    small = jnp.concatenate([w_i, i_p, f_p], axis=1)
    small = jnp.pad(small, ((0, 0), (0, LANES - small.shape[1])))
    packed = jnp.concatenate([
        q_a, c_kv, pad_heads(q_i, N_HEADS_IDX, HEAD_DIM_IDX),
        jnp.pad(k_i, ((0, 0), (0, LANES - HEAD_DIM_IDX))), small, q_m, k_m, v_m, o_p, g_a, g_m], axis=1)
    assert packed.shape[1] == _C_END
    return packed.astype(BF16)


def _mixin_kernel(x_ref, gain_ref, sh_ref, sc_ref, w_ref, kvn_ref, wuk_ref, cw_ref, cb_ref,
                  qabs_ref, qidx_ref, kidx_ref, ckv_ref, ckvt_ref, wt_ref, ift_ref,
                  qk_ref, v_ref, o_ref, ga_ref, gm_ref, h_scr, xe_scr, *, tm, tiles_per_seq):
    nqb = tm // Q_BLOCK

    @pl.when(pl.program_id(0) % tiles_per_seq == 0)
    def _():
        xe_scr[:8] = jnp.zeros((8, xe_scr.shape[1]), F32)

    x = x_ref[...]
    h = _rms_norm(x, gain_ref[...]) * (1.0 + sc_ref[0]) + sh_ref[0]
    h_scr[...] = h.astype(BF16)

    def proj(lo, hi):
        return jnp.dot(h_scr[...], w_ref[:, lo:hi], preferred_element_type=F32)

    qa = proj(_C_QA, _C_CKV)
    scale = HEAD_DIM_A ** -0.5 * LOG2E
    for g in range(N_HEADS_A // 2):
        q_pair = qa[:, g * LANES:(g + 1) * LANES].astype(BF16)
        q_abs = jnp.dot(q_pair, wuk_ref[g], preferred_element_type=F32) * scale
        for j in range(2):
            qabs_ref[:, 2 * g + j] = (q_abs[:, j * D_LATENT:(j + 1) * D_LATENT]
                                      .astype(BF16).reshape(nqb, Q_BLOCK, D_LATENT))
    ckv = _rms_norm(proj(_C_CKV, _C_QI), kvn_ref[...])
    ckv_ref[...] = ckv.astype(BF16)
    ckv_t = ckv.T
    ones_row = jnp.where(lax.broadcasted_iota(I32, (CKVT_ROWS - D_LATENT, KEY_BLOCK), 0) == 0, 1.0, 0.0)
    for j in range(tm // KEY_BLOCK):
        ckvt_ref[j, :D_LATENT] = ckv_t[:, j * KEY_BLOCK:(j + 1) * KEY_BLOCK].astype(BF16)
        ckvt_ref[j, D_LATENT:] = ones_row.astype(BF16)
    qi = proj(_C_QI, _C_KI)
    for hh in range(N_HEADS_IDX):
        qidx_ref[:, hh] = qi[:, hh * LANES:(hh + 1) * LANES].astype(BF16).reshape(nqb, Q_BLOCK, LANES)
    kidx_ref[...] = proj(_C_KI, _C_SM).astype(BF16)
    small_t = proj(_C_SM, _C_QK).T
    wt_ref[...] = small_t[_SM_W:_SM_W + N_HEADS_IDX] * IDX_SCALE
    ift_ref[...] = small_t[_SM_I:_SM_I + 2 * N_HEADS_M]
    xe_scr[8:] = proj(_C_QK, _C_V)
    xe = xe_scr[...]
    xq = xe[8:]
    conv = xq * cw_ref[CONV_WIDTH - 1:CONV_WIDTH, :] + cb_ref[...]
    for d in range(1, CONV_WIDTH):
        conv = conv + pltpu.roll(xe, d, axis=0)[8:] * cw_ref[CONV_WIDTH - 1 - d:CONV_WIDTH - d, :]
    xe_scr[:8] = xe_scr[tm:]
    qk = conv * _sigmoid(conv)
    qk_ref[:, :W_M] = qk[:, :W_M].astype(BF16)
    qk_ref[:, W_M:] = (qk[:, W_M:] * (HEAD_DIM_M ** -0.5)).astype(BF16)
    v_ref[...] = proj(_C_V, _C_O).astype(BF16)
    o_ref[...] = proj(_C_O, _C_GA)
    ga_ref[...] = proj(_C_GA, _C_GM)
    gm_ref[...] = proj(_C_GM, _C_END)


def _mixin(x, gain, sh, sc, w_packed, kv_norm, wuk_t, conv_w, conv_b, *, seq):
    t, d = x.shape
    tm = MIX_TM
    per_b = seq // tm
    nqb = tm // Q_BLOCK
    row = lambda w: pl.BlockSpec((tm, w), lambda i: (i, 0))
    mod_spec = pl.BlockSpec((1, 1, d), lambda i: (i // per_b, 0, 0))
    out_shape = (
        jax.ShapeDtypeStruct((t // Q_BLOCK, N_HEADS_A, Q_BLOCK, D_LATENT), BF16),
        jax.ShapeDtypeStruct((t // Q_BLOCK, N_HEADS_IDX, Q_BLOCK, LANES), BF16),
        jax.ShapeDtypeStruct((t, LANES), BF16),
        jax.ShapeDtypeStruct((t, D_LATENT), BF16),
        jax.ShapeDtypeStruct((t // KEY_BLOCK, CKVT_ROWS, KEY_BLOCK), BF16),
        jax.ShapeDtypeStruct((N_HEADS_IDX, t), F32),
        jax.ShapeDtypeStruct((2 * N_HEADS_M, t), F32),
        jax.ShapeDtypeStruct((t, 2 * W_M), BF16),
        jax.ShapeDtypeStruct((t, W_M), BF16),
        jax.ShapeDtypeStruct((t, W_M), F32),
        jax.ShapeDtypeStruct((t, d), F32),
        jax.ShapeDtypeStruct((t, d), F32),
    )
    out_specs = (
        pl.BlockSpec((nqb, N_HEADS_A, Q_BLOCK, D_LATENT), lambda i: (i, 0, 0, 0)),
        pl.BlockSpec((nqb, N_HEADS_IDX, Q_BLOCK, LANES), lambda i: (i, 0, 0, 0)),
        row(LANES), row(D_LATENT),
        pl.BlockSpec((tm // KEY_BLOCK, CKVT_ROWS, KEY_BLOCK), lambda i: (i, 0, 0)),
        pl.BlockSpec((N_HEADS_IDX, tm), lambda i: (0, i)),
        pl.BlockSpec((2 * N_HEADS_M, tm), lambda i: (0, i)),
        row(2 * W_M), row(W_M), row(W_M), row(d), row(d),
    )
    return pl.pallas_call(
        functools.partial(_mixin_kernel, tm=tm, tiles_per_seq=per_b),
        out_shape=out_shape,
        grid=(t // tm,),
        in_specs=[pl.BlockSpec((tm, d), lambda i: (i, 0)), _resident((1, d)), mod_spec, mod_spec,
                  _resident((d, _C_END)), _resident((1, D_LATENT)),
                  _resident((N_HEADS_A // 2, LANES, 2 * D_LATENT)),
                  _resident((CONV_WIDTH, 2 * W_M)), _resident((1, 2 * W_M))],
        out_specs=out_specs,
        scratch_shapes=[pltpu.VMEM((tm, d), BF16), pltpu.VMEM((tm + 8, 2 * W_M), F32)],
        compiler_params=pltpu.CompilerParams(dimension_semantics=("arbitrary",),
                                             vmem_limit_bytes=VMEM_LIMIT),
        name="mixin",
    )(x, gain.reshape(1, d), sh, sc, w_packed, kv_norm.reshape(1, D_LATENT), wuk_t,
      conv_w, conv_b.reshape(1, -1))


def _sortable_key(score):
    bits = pltpu.bitcast(score, I32)
    bits = jnp.where(bits == INT_MIN, 0, bits)
    return jnp.where(bits < 0, bits ^ 0x7FFFFFFF, bits)


def _bit_transpose32(words):
    v = list(words)
    j, m = 16, 0x0000FFFF
    while j:
        k = 0
        while k < 32:
            t = (v[k] ^ lax.shift_right_logical(v[k + j], jnp.int32(j))) & m
            v[k] = v[k] ^ t
            v[k + j] = v[k + j] ^ (t << j)
            k = (k + j + 1) & ~j
        j >>= 1
        m = (m ^ (m << j)) & 0x7FFFFFFF
    return v


def _dsa_kernel(qidx_ref, qabs_ref, wt_ref, kidx_ref, ckv_ref, ckvt_ref, btile_ref, bmax_ref, wuvt_ref,
                out_ref, keys_scr, planes_scr, cand_scr, tau_scr, acc_scr, m_scr, ltc_scr, kmax_scr,
                qta_scr, qtc_scr, *, topk, n_qb):
    kb_sz = KEY_BLOCK
    step = pl.program_id(1)
    has_c = step >= 1
    qa = jnp.minimum(step, n_qb - 1)
    qc = jnp.maximum(step - 1, 0)
    slot_a = step & 1
    slot_c = 1 - slot_a
    n_a = qa // (kb_sz // Q_BLOCK) + 1
    n_c = qc // (kb_sz // Q_BLOCK) + 1
    qa0 = qa * Q_BLOCK
    qc0 = qc * Q_BLOCK
    row_id = lax.broadcasted_iota(I32, (kb_sz, LANES), 0)
    lane_id = lax.broadcasted_iota(I32, (kb_sz, LANES), 1)
    n_blocks = keys_scr.shape[1]
    n_groups = N_HEADS_A // 2
    pair = 2 * LANES
    ones8 = jnp.ones((8, D_LATENT), BF16)

    @pl.when(step == 0)
    def _():
        keys_scr[1, 0] = jnp.full((kb_sz, LANES), INT_MIN, I32)
        tau_scr[1] = jnp.zeros((1, LANES), I32)
        planes_scr[...] = jnp.zeros(planes_scr.shape, I32)

        def kn_body(kb, mx):
            c = ckv_ref[kb].astype(F32)
            n2 = lax.dot_general(ones8, (c * c).astype(BF16), (((1,), (1,)), ((), ())),
                                 preferred_element_type=F32)
            return jnp.maximum(mx, n2[0:1])
        mx = lax.fori_loop(0, n_blocks, kn_body, jnp.zeros((1, kb_sz), F32))
        kmax_scr[...] = jnp.max(mx, axis=1, keepdims=True)

    for g in range(n_groups):
        qi_g = qidx_ref[0, 2 * g:2 * g + 2].reshape(2 * Q_BLOCK, LANES).astype(F32)
        qta_scr[g] = qi_g.T.astype(BF16)
        qa_g = qabs_ref[0, 2 * g:2 * g + 2].reshape(2 * Q_BLOCK, D_LATENT).astype(F32)
        qtc_scr[g] = qa_g.T.astype(BF16)

    def idx_dot(kb, g):
        return jnp.dot(kidx_ref[kb], qta_scr[g], preferred_element_type=F32)

    def logits(kb, g):
        return jnp.dot(ckv_ref[kb], qtc_scr[g], preferred_element_type=F32)

    def bias_start(kb):
        delta = jnp.minimum(qc0 - kb * kb_sz, BIAS_PAD)
        return pl.multiple_of(BIAS_PAD - delta, LANES)

    qn2 = []
    for g in range(n_groups):
        q_g = qabs_ref[0, 2 * g:2 * g + 2].reshape(2 * Q_BLOCK, D_LATENT).astype(F32)
        qn2.append(lax.dot_general(ones8, (q_g * q_g).astype(BF16), (((1,), (1,)), ((), ())),
                                   preferred_element_type=F32)[0:1])
    bound = jnp.sqrt(jnp.concatenate(qn2, axis=1) * kmax_scr[...]) * 1.02 + bmax_ref[...] + 1e-3
    tau_c = tau_scr[slot_c]
    w_t = wt_ref[...]

    acc_scr[...] = jnp.zeros(acc_scr.shape, F32)
    for g in range(n_groups):
        ltc_scr[:, g * pair:(g + 1) * pair] = logits(0, g)

    bound_far = bound - jnp.concatenate([btile_ref[hh, 0:1, :] for hh in range(N_HEADS_A)], axis=1)

    def block_step(kb, far):
        kc = jnp.minimum(kb, n_c - 1)
        kc_next = jnp.minimum(kb + 1, n_c - 1)
        thr = jnp.where(has_c & (kb < n_c), tau_c - 1, jnp.int32(2 ** 31 - 1))
        sel = keys_scr[slot_c, kc] > thr
        ct_blk = ckvt_ref[kc]
        start = bias_start(kc)
        ref_pt = bound_far if far else bound
        score = jnp.zeros((kb_sz, LANES), F32)
        for g in range(n_groups):
            s_t = idx_dot(kb, g)
            for j in range(2):
                hh = 2 * g + j
                score = score + jnp.maximum(s_t[:, j * LANES:(j + 1) * LANES], 0.0) * w_t[hh:hh + 1, :]
        for g in range(n_groups):
            lt = ltc_scr[:, g * pair:(g + 1) * pair]
            ps = []
            for j in range(2):
                hh = 2 * g + j
                piece = lt[:, j * LANES:(j + 1) * LANES]
                if not far:
                    piece = piece + btile_ref[hh, pl.ds(start, kb_sz), :]
                ps.append(jnp.exp2(jnp.where(sel, piece, NEG_BIG) - ref_pt[:, hh * LANES:(hh + 1) * LANES]))
            ltc_scr[:, g * pair:(g + 1) * pair] = logits(kc_next, g)
            acc_scr[g] += jnp.dot(ct_blk, jnp.concatenate(ps, axis=1).astype(BF16),
                                  preferred_element_type=F32)
        valid = (kb * kb_sz + row_id) <= (qa0 + lane_id)
        keys = jnp.where(valid, _sortable_key(score), INT_MIN)
        keys_scr[slot_a, kb] = keys
        v = keys ^ INT_MIN
        for sub in range(kb_sz // PLANE_KEYS):
            r0 = sub * PLANE_KEYS
            words = _bit_transpose32([v[r0 + 8 * i:r0 + 8 * (i + 1), :] for i in range(32)])
            for bit in range(32):
                planes_scr[bit, kb * (kb_sz // PLANE_KEYS) + sub] = words[31 - bit]

    def block_body(far, it, carry):
        for u in range(BLOCK_UNROLL):
            block_step(it * BLOCK_UNROLL + u, far)
        return carry

    n_far_iters = jnp.maximum(n_c - 2, 0) // BLOCK_UNROLL
    lax.fori_loop(0, n_far_iters, functools.partial(block_body, True), 0)

    def near_body(kb, carry):
        block_step(kb, False)
        return carry

    lax.fori_loop(n_far_iters * BLOCK_UNROLL, n_a, near_body, 0)
    n_kb = n_a


    n_planes = planes_scr.shape[1]
    live = n_kb * (kb_sz // PLANE_KEYS)

    def radix_select(width):
        blk_id = lax.broadcasted_iota(I32, (width, 8, LANES), 0)
        cand_scr[:width] = jnp.where(blk_id < live, -1, 0)

        def bit_body(it, carry):
            above, tau_u = carry
            bit = 31 - it
            ones = cand_scr[:width] & planes_scr[bit, :width]
            c1 = jnp.sum(jnp.sum(lax.population_count(ones), axis=0), axis=0, keepdims=True)
            take = (above + c1) >= topk
            cand_scr[:width] = jnp.where(take, ones, cand_scr[:width] ^ ones)
            above = jnp.where(take, above, above + c1)
            tau_u = jnp.where(take, tau_u | (jnp.int32(1) << bit), tau_u)
            return above, tau_u

        zero = jnp.zeros((1, LANES), I32)
        above, tau_u = lax.fori_loop(0, 32, bit_body, (zero, zero))
        n_eq = jnp.sum(jnp.sum(lax.population_count(cand_scr[:width]), axis=0), axis=0, keepdims=True)
        return above, tau_u, n_eq

    widths = [n_planes * (i + 1) // 4 for i in range(4)]
    select = functools.partial(radix_select, widths[-1])
    for width in reversed(widths[:-1]):
        select = functools.partial(lax.cond, live <= width, functools.partial(radix_select, width), select)
    n_gt, tau_u, n_eq = select()
    tau = tau_u ^ INT_MIN

    need = topk - n_gt
    overflow = n_eq > need
    seq_bits = max(1, (n_blocks * kb_sz - 1).bit_length())

    @pl.when(jnp.max(jnp.where(overflow, 1, 0)) > 0)
    def _():
        def count_ties_before(trial):
            def body(kb, acc):
                hit = jnp.where((keys_scr[slot_a, kb] == tau) & ((kb * kb_sz + row_id) < trial), 1, 0)
                return acc + jnp.sum(hit.reshape(kb_sz // 8, 8, LANES), axis=0)
            acc = lax.fori_loop(0, n_kb, body, jnp.zeros((8, LANES), I32))
            return jnp.sum(acc, axis=0, keepdims=True)

        def idx_body(it, jc):
            trial = jc | (jnp.int32(1) << (seq_bits - 1 - it))
            return jnp.where(count_ties_before(trial) < need, trial, jc)

        j_cut = lax.fori_loop(0, seq_bits, idx_body, jnp.zeros((1, LANES), I32))

        def demote_body(kb, carry):
            k = keys_scr[slot_a, kb]
            drop = overflow & (k == tau) & ((kb * kb_sz + row_id) > j_cut)
            keys_scr[slot_a, kb] = jnp.where(drop, INT_MIN, k)
            return carry

        lax.fori_loop(0, n_kb, demote_body, 0)

    tau_scr[slot_a] = jnp.maximum(tau, INT_MIN + 1)

    l_min = jnp.min(jnp.concatenate([acc_scr[g, D_LATENT:D_LATENT + 1, :] for g in range(n_groups)], axis=1))

    @pl.when(has_c & jnp.logical_not(l_min >= 2.0 ** -80))
    def _():
        m_scr[...] = jnp.full(m_scr.shape, NEG_BIG, F32)
        acc_scr[...] = jnp.zeros(acc_scr.shape, F32)

        def exact_body(kb, carry):
            sel = keys_scr[slot_c, kb] >= tau_c
            ct_blk = ckvt_ref[kb]
            start = bias_start(kb)
            for g in range(n_groups):
                lt = logits(kb, g)
                ps, alphas = [], []
                for j in range(2):
                    hh = 2 * g + j
                    sl = slice(hh * LANES, (hh + 1) * LANES)
                    piece = lt[:, j * LANES:(j + 1) * LANES] + btile_ref[hh, pl.ds(start, kb_sz), :]
                    masked = jnp.where(sel, piece, NEG_BIG)
                    m_old = m_scr[:, sl]
                    m_new = jnp.maximum(m_old, jnp.max(masked, axis=0, keepdims=True))
                    m_scr[:, sl] = m_new
                    alphas.append(jnp.exp2(m_old - m_new))
                    ps.append(jnp.exp2(masked - m_new))
                pv = jnp.dot(ct_blk, jnp.concatenate(ps, axis=1).astype(BF16), preferred_element_type=F32)
                acc_scr[g] = jnp.concatenate(alphas, axis=1) * acc_scr[g] + pv
            return carry

        lax.fori_loop(0, n_c, exact_body, 0)

    @pl.when(has_c)
    def _():
        ys = []
        for hh in range(N_HEADS_A):
            acc_h = acc_scr[hh // 2, :, (hh % 2) * LANES:(hh % 2 + 1) * LANES]
            o_h = acc_h[:D_LATENT] * (1.0 / acc_h[D_LATENT:D_LATENT + 1])
            ys.append(jnp.dot(wuvt_ref[hh], o_h.astype(BF16), preferred_element_type=F32))
        y_t = jnp.concatenate(ys, axis=0)
        out_ref[...] = y_t.T.astype(BF16)


def _dsa(q_idx, q_abs, w_t, k_idx, ckv, ckv_t, btile, bmax, wuv_t, *, batch, seq):
    t = batch * seq
    nqb = seq // Q_BLOCK
    nkb = seq // KEY_BLOCK
    topk = min(TOPK_MAX, seq // 4)
    k_idx3 = k_idx.reshape(t // KEY_BLOCK, KEY_BLOCK, LANES)
    ckv3 = ckv.reshape(t // KEY_BLOCK, KEY_BLOCK, D_LATENT)
    per_batch = lambda shape: pl.BlockSpec(shape, lambda b, q: (b,) + (0,) * (len(shape) - 1),
                                           pipeline_mode=pl.Buffered(1))
    scored = lambda b, s: b * nqb + jnp.minimum(s, nqb - 1)
    attended = lambda b, s: b * nqb + jnp.maximum(s - 1, 0)
    return pl.pallas_call(
        functools.partial(_dsa_kernel, topk=topk, n_qb=nqb),
        out_shape=jax.ShapeDtypeStruct((t, W_A), BF16),
        grid=(batch, nqb + 1),
        in_specs=[pl.BlockSpec((1, N_HEADS_IDX, Q_BLOCK, LANES), lambda b, s: (scored(b, s), 0, 0, 0)),
                  pl.BlockSpec((1, N_HEADS_A, Q_BLOCK, D_LATENT), lambda b, s: (attended(b, s), 0, 0, 0)),
                  pl.BlockSpec((N_HEADS_IDX, Q_BLOCK), lambda b, s: (0, scored(b, s))),
                  per_batch((nkb, KEY_BLOCK, LANES)),
                  per_batch((nkb, KEY_BLOCK, D_LATENT)),
                  per_batch((nkb, CKVT_ROWS, KEY_BLOCK)),
                  _resident((N_HEADS_A, BIAS_ROWS, LANES)),
                  _resident((1, N_HEADS_A * LANES)),
                  _resident((N_HEADS_A, HEAD_DIM_A, D_LATENT))],
        out_specs=pl.BlockSpec((Q_BLOCK, W_A), lambda b, s: (attended(b, s), 0)),
        scratch_shapes=[pltpu.VMEM((2, nkb, KEY_BLOCK, LANES), I32),
                        pltpu.VMEM((32, seq // PLANE_KEYS + 1, 8, LANES), I32),
                        pltpu.VMEM((seq // PLANE_KEYS + 1, 8, LANES), I32),
                        pltpu.VMEM((2, 1, LANES), I32),
                        pltpu.VMEM((N_HEADS_A // 2, CKVT_ROWS, 2 * LANES), F32),
                        pltpu.VMEM((1, N_HEADS_A * LANES), F32),
                        pltpu.VMEM((KEY_BLOCK, (N_HEADS_A + 1) * LANES), F32),
                        pltpu.VMEM((1, 1), F32),
                        pltpu.VMEM((N_HEADS_IDX // 2, LANES, 2 * LANES), BF16),
                        pltpu.VMEM((N_HEADS_A // 2, D_LATENT, 2 * LANES), BF16)],
        compiler_params=pltpu.CompilerParams(dimension_semantics=("arbitrary", "arbitrary"),
                                             vmem_limit_bytes=VMEM_LIMIT),
        name="dsa",
    )(q_idx, q_abs, w_t, k_idx3, ckv3, ckv_t, btile, bmax, wuv_t)


def _mlstm_kernel(qk_ref, v_ref, o_ref, ift_ref, gbt_ref, hn_ref,
                  out_ref, cx_scr, m_scr, *, chunk, n_batch):
    L = chunk

    @pl.when(pl.program_id(0) == 0)
    def _():
        cx_scr[...] = jnp.zeros(cx_scr.shape, F32)
        m_scr[...] = jnp.zeros(m_scr.shape, F32)

    rr = lax.broadcasted_iota(I32, (L, L), 0)
    cc = lax.broadcasted_iota(I32, (L, L), 1)
    causal = cc <= rr
    triu = jnp.where(rr <= cc, 1.0, 0.0).astype(BF16)
    lane = lax.broadcasted_iota(I32, (8, L), 1)
    ones_col = jnp.where(lax.broadcasted_iota(I32, (L, HEAD_DIM_M), 1) == 0, 1.0, 0.0).astype(BF16)
    for bi in range(n_batch):
        _mlstm_chunk(qk_ref.at[bi], v_ref.at[bi], o_ref.at[bi], ift_ref.at[bi], gbt_ref, hn_ref,
                     out_ref.at[bi], cx_scr.at[bi], m_scr.at[bi], causal, triu, lane, ones_col, L)


def _mlstm_chunk(qk_ref, v_ref, o_ref, ift_ref, gbt_ref, hn_ref, out_ref, cx_scr, m_scr,
                 causal, triu, lane, ones_col, L):
    g_t = ift_ref[...] + gbt_ref[...]
    b_all = sum(jnp.dot(piece, triu, preferred_element_type=F32) for piece in _split3(_log_sigmoid(g_t)))
    b8 = pltpu.roll(b_all, N_HEADS_M, axis=0)
    a8 = g_t - b8
    cm = a8
    shift = 1
    while shift < L:
        cm = jnp.maximum(cm, jnp.where(lane >= shift, pltpu.roll(cm, shift, axis=1), NEG_BIG))
        shift *= 2
    m_prev = m_scr[...]
    mx = jnp.maximum(m_prev, cm)
    mx_last = mx[:, L - 1:L]
    decay8 = jnp.exp(m_prev - mx_last)
    m_scr[...] = b8[:, L - 1:L] + mx_last
    rows = jnp.concatenate([-mx,
                            jnp.exp(m_prev - mx),
                            jnp.exp(-(b8 + mx)),
                            jnp.exp(a8 - mx_last),
                            jnp.zeros((LANES - 32, L), F32)], axis=0)
    cols = rows.T

    o_gate = _sigmoid(o_ref[...])
    for hh in range(N_HEADS_M):
        hs = slice(hh * HEAD_DIM_M, (hh + 1) * HEAD_DIM_M)
        qb16 = qk_ref[:, hs]
        kb16 = qk_ref[:, W_M + hh * HEAD_DIM_M:W_M + (hh + 1) * HEAD_DIM_M]
        v_ext = jnp.concatenate([v_ref[:, hs], ones_col], axis=1)
        u_c = cols[:, hh:hh + 1]
        w_inter = cols[:, 8 + hh:9 + hh]
        em_c = cols[:, 16 + hh:17 + hh]
        wgt_c = cols[:, 24 + hh:25 + hh]
        cx_prev = cx_scr[hh]

        d_mat = jnp.where(causal, jnp.exp(u_c + a8[hh:hh + 1, :]), 0.0)
        s = lax.dot_general(qb16, kb16, (((1,), (1,)), ((), ())), preferred_element_type=F32) * d_mat
        intra = jnp.dot(s.astype(BF16), v_ext, preferred_element_type=F32)
        inter = jnp.dot(qb16, cx_prev.astype(BF16), preferred_element_type=F32)
        both = w_inter * inter + intra
        num = both[:, :HEAD_DIM_M]
        den = both[:, HEAD_DIM_M:HEAD_DIM_M + 1]
        hval = num / jnp.maximum(jnp.abs(den), em_c)

        kw = kb16.astype(F32) * wgt_c
        cx_scr[hh] = decay8[hh:hh + 1] * cx_prev + jnp.dot(kw.T.astype(BF16), v_ext,
                                                           preferred_element_type=F32)

        mu = jnp.mean(hval, axis=1, keepdims=True)
        cen = hval - mu
        var = jnp.mean(cen * cen, axis=1, keepdims=True)
        hn = cen * lax.rsqrt(var + EPS) * hn_ref[:, hs]
        out_ref[:, hs] = (hn * o_gate[:, hs]).astype(BF16)


def _mlstm(qk, v, o_pre, ift, gate_bias, head_norm, *, batch, seq):
    t = batch * seq
    L = MLSTM_CHUNK
    nc = seq // L
    gbt = jnp.broadcast_to(gate_bias.reshape(2 * N_HEADS_M, 1), (2 * N_HEADS_M, L))
    ift_b = ift.reshape(2 * N_HEADS_M, batch, seq).transpose(1, 0, 2)
    row = lambda w: pl.BlockSpec((batch, L, w), lambda c: (0, c, 0))
    out = pl.pallas_call(
        functools.partial(_mlstm_kernel, chunk=L, n_batch=batch),
        out_shape=jax.ShapeDtypeStruct((batch, seq, W_M), BF16),
        grid=(nc,),
        in_specs=[row(2 * W_M), row(W_M), row(W_M),
                  pl.BlockSpec((batch, 2 * N_HEADS_M, L), lambda c: (0, 0, c)),
                  _resident((2 * N_HEADS_M, L)), _resident((1, W_M))],
        out_specs=row(W_M),
        scratch_shapes=[pltpu.VMEM((batch, N_HEADS_M, HEAD_DIM_M, 2 * HEAD_DIM_M), F32),
                        pltpu.VMEM((batch, 8, 1), F32)],
        compiler_params=pltpu.CompilerParams(dimension_semantics=("arbitrary",),
                                             vmem_limit_bytes=VMEM_LIMIT),
        name="mlstm",
    )(qk.reshape(batch, seq, 2 * W_M), v.reshape(batch, seq, W_M), o_pre.reshape(batch, seq, W_M),
      ift_b, gbt, head_norm.reshape(1, -1))
    return out.reshape(t, W_M)


def kernel(x, c, ada_w, ada_b, ffn1_norm, ffn1_w1, ffn1_w3, ffn1_w2, mix_norm, w_in, conv_w, conv_b,
           kv_norm, w_uk, w_uv, mlstm_gate_bias, mlstm_head_norm, rel_bias, w_branch_attn,
           w_branch_mlstm, w_out, ffn2_norm, ffn2_w1, ffn2_w3, ffn2_w2, final_norm):
    batch, seq, d = x.shape
    depth = ada_w.shape[0]
    assert seq % max(FFN_TM, MIX_TM, MLSTM_CHUNK, KEY_BLOCK) == 0
    t = batch * seq
    xf = x.reshape(t, d)
    btile, bmax = _bias_tiles(rel_bias)
    for l in range(depth):
        mod = _adaln(c, ada_w[l], ada_b[l]).reshape(batch, 9, 1, d)
        sh1, sc1, g1, sh2, sc2, g2, sh3, sc3, g3 = [mod[:, n] for n in range(9)]
        xf = _ffn(xf, ffn1_norm[l], sh1, sc1, g1, ffn1_w1[l], ffn1_w3[l], ffn1_w2[l], final_norm,
                  seq=seq, final_norm=False)
        wuk_hdc = w_uk[l].transpose(0, 2, 1).reshape(N_HEADS_A // 2, 2, HEAD_DIM_A, D_LATENT)
        zeros = jnp.zeros_like(wuk_hdc[:, 0])
        wuk_t = jnp.concatenate([jnp.concatenate([wuk_hdc[:, 0], zeros], axis=2),
                                 jnp.concatenate([zeros, wuk_hdc[:, 1]], axis=2)], axis=1).astype(BF16)
        (q_abs, q_idx, k_idx, ckv, ckv_t, w_t, ift, qk_m, v_m, o_pre, gate_a, gate_m) = _mixin(
            xf, mix_norm[l], sh2, sc2, _pack_w_in(w_in[l], d), kv_norm[l], wuk_t, conv_w[l], conv_b[l],
            seq=seq)
        wuv_t = w_uv[l].transpose(0, 2, 1).astype(BF16)
        y_a = _dsa(q_idx, q_abs, w_t, k_idx, ckv, ckv_t, btile, bmax, wuv_t, batch=batch, seq=seq)
        h_m = _mlstm(qk_m, v_m, o_pre, ift, mlstm_gate_bias[l], mlstm_head_norm[l], batch=batch, seq=seq)
        xf = _ffn(xf, ffn2_norm[l], sh3, sc3, g3, ffn2_w1[l], ffn2_w3[l], ffn2_w2[l], final_norm,
                  seq=seq, final_norm=(l == depth - 1),
                  merge=(y_a, h_m, gate_a, gate_m, g2, w_branch_attn[l], w_branch_mlstm[l], w_out[l]))
    return xf.reshape(batch, seq, d)
```

```python
import functools
import math

import jax
import jax.numpy as jnp
from jax import lax
from jax.experimental import pallas as pl
from jax.experimental.pallas import tpu as pltpu

F32 = jnp.float32
BF16 = jnp.bfloat16
I32 = jnp.int32

LANES = 128
VMEM_LIMIT = 56 * 1024 * 1024

N_HEADS_A = 8
HEAD_DIM_A = 64
D_LATENT = 256
N_HEADS_IDX = 8
HEAD_DIM_IDX = 64
TOPK_MAX = 256
Q_BLOCK = 128
N_BUCKETS = 32
MAX_DISTANCE = 128
N_HEADS_M = 4
HEAD_DIM_M = 128
CONV_WIDTH = 4
EPS = 1e-6
IDX_SCALE = (N_HEADS_IDX ** -0.5) * (HEAD_DIM_IDX ** -0.5)
W_A = N_HEADS_A * HEAD_DIM_A
W_M = N_HEADS_M * HEAD_DIM_M

FFN_TM = 512
FFN_CHUNK = 256
MIX_TM = 512
KEY_BLOCK = 256
BLOCK_UNROLL = 8
PLANE_KEYS = 256
MLSTM_CHUNK = 256
NEG_BIG = -1e30
INT_MIN = -2 ** 31

BIAS_PAD = 2 * KEY_BLOCK - Q_BLOCK
BIAS_ROWS = KEY_BLOCK + BIAS_PAD
CKVT_ROWS = D_LATENT + 16
LOG2E = math.log2(math.e)


def _sigmoid(x):
    return 1.0 / (1.0 + jnp.exp(-x))


def _log_sigmoid(x):
    return jnp.minimum(x, 0.0) - jnp.log(1.0 + jnp.exp(-jnp.abs(x)))


def _rms_norm(x, gain):
    ms = jnp.mean(x * x, axis=-1, keepdims=True)
    return x * lax.rsqrt(ms + EPS) * gain


def _split3(x):
    hi = x.astype(BF16)
    r1 = x - hi.astype(F32)
    mid = r1.astype(BF16)
    lo = (r1 - mid.astype(F32)).astype(BF16)
    return hi, mid, lo


def _resident(shape):
    nd = len(shape)
    return pl.BlockSpec(shape, lambda *_: (0,) * nd, pipeline_mode=pl.Buffered(1))


def _adaln_kernel(c_ref, w_ref, b_ref, o_ref):
    c = c_ref[...]
    cond = c * _sigmoid(c)
    o_ref[...] = jnp.dot(cond.astype(BF16), w_ref[...].astype(BF16),
                         preferred_element_type=F32) + b_ref[...]


def _adaln(c, ada_w, ada_b):
    b, d = c.shape
    n = ada_w.shape[1]
    rows = 8
    c_pad = jnp.zeros((rows, d), F32).at[:b].set(c)
    tn = 1024
    out = pl.pallas_call(
        _adaln_kernel,
        out_shape=jax.ShapeDtypeStruct((rows, n), F32),
        grid=(n // tn,),
        in_specs=[pl.BlockSpec((rows, d), lambda j: (0, 0)),
                  pl.BlockSpec((d, tn), lambda j: (0, j)),
                  pl.BlockSpec((1, tn), lambda j: (0, j))],
        out_specs=pl.BlockSpec((rows, tn), lambda j: (0, j)),
        compiler_params=pltpu.CompilerParams(dimension_semantics=("arbitrary",),
                                             vmem_limit_bytes=VMEM_LIMIT),
        name="adaln",
    )(c_pad, ada_w, ada_b.reshape(1, n))
    return out[:b]


def _t5_bucket(dist):
    n = jnp.maximum(dist, 0)
    max_exact = N_BUCKETS // 2
    nf = jnp.maximum(n, 1).astype(F32)
    large = max_exact + (jnp.log(nf / max_exact) / math.log(MAX_DISTANCE / max_exact)
                         * (N_BUCKETS - max_exact)).astype(I32)
    large = jnp.minimum(large, N_BUCKETS - 1)
    return jnp.where(n < max_exact, n, large)


def _bias_kernel(rel_ref, tile_ref, max_ref):
    r = lax.broadcasted_iota(I32, (BIAS_ROWS, LANES), 0)
    i = lax.broadcasted_iota(I32, (BIAS_ROWS, LANES), 1)
    bucket = _t5_bucket(i - r + BIAS_PAD)
    for h in range(N_HEADS_A):
        acc = jnp.zeros((BIAS_ROWS, LANES), F32)
        top = rel_ref[0, h] * LOG2E
        for bkt in range(N_BUCKETS):
            val = rel_ref[bkt, h] * LOG2E
            acc = jnp.where(bucket == bkt, val, acc)
            top = jnp.maximum(top, val)
        tile_ref[h] = acc
        max_ref[:, h * LANES:(h + 1) * LANES] = jnp.full((1, LANES), top, F32)


def _bias_tiles(rel_bias):
    return pl.pallas_call(
        _bias_kernel,
        out_shape=(jax.ShapeDtypeStruct((N_HEADS_A, BIAS_ROWS, LANES), F32),
                   jax.ShapeDtypeStruct((1, N_HEADS_A * LANES), F32)),
        in_specs=[pl.BlockSpec(memory_space=pltpu.SMEM)],
        out_specs=(pl.BlockSpec(memory_space=pltpu.VMEM), pl.BlockSpec(memory_space=pltpu.VMEM)),
        name="bias_tiles",
    )(rel_bias)


def _ffn_kernel(*refs, n_chunks, final_norm, merge):
    if merge:
        (x_ref, ya_ref, hm_ref, ga_ref, gm_ref, gmix_ref, wa_ref, wm_ref, wo_ref), refs = refs[:9], refs[9:]
    else:
        x_ref, refs = refs[0], refs[1:]
    gain_ref, sh_ref, sc_ref, g_ref, w1_ref, w3_ref, w2_ref, fin_ref, o_ref, h_scr, acc_scr = refs
    x = x_ref[...]
    if merge:
        pa = jnp.dot(ya_ref[...], wa_ref[...], preferred_element_type=F32)
        pm = jnp.dot(hm_ref[...], wm_ref[...], preferred_element_type=F32)
        merged = _sigmoid(ga_ref[...]) * pa + _sigmoid(gm_ref[...]) * pm
        x = x + gmix_ref[0] * jnp.dot(merged.astype(BF16), wo_ref[...], preferred_element_type=F32)
    h = _rms_norm(x, gain_ref[...]) * (1.0 + sc_ref[0]) + sh_ref[0]
    h_scr[...] = h.astype(BF16)
    for j in range(n_chunks):
        hb = h_scr[...]
        cols = slice(j * FFN_CHUNK, (j + 1) * FFN_CHUNK)
        u1 = jnp.dot(hb, w1_ref[:, cols], preferred_element_type=F32)
        u3 = jnp.dot(hb, w3_ref[:, cols], preferred_element_type=F32)
        a = (u1 * _sigmoid(u1)) * u3
        part = jnp.dot(a.astype(BF16), w2_ref[j], preferred_element_type=F32)
        if j == 0:
            acc_scr[...] = part
        else:
            acc_scr[...] += part
    out = x + (0.5 * g_ref[0]) * acc_scr[...]
    if final_norm:
        out = _rms_norm(out, fin_ref[...])
    o_ref[...] = out


def _ffn(x, gain, sh, sc, g, w1, w3, w2, fin, *, seq, final_norm, merge=None):
    t, d = x.shape
    dff = w1.shape[1]
    nch = dff // FFN_CHUNK
    w1c = w1.astype(BF16)
    w3c = w3.astype(BF16)
    w2c = w2.astype(BF16).reshape(nch, FFN_CHUNK, d)
    tm = FFN_TM
    per_b = seq // tm
    row = lambda w: pl.BlockSpec((tm, w), lambda i: (i, 0))
    mod_spec = pl.BlockSpec((1, 1, d), lambda i: (i // per_b, 0, 0))
    merge_specs, merge_args = [], []
    if merge is not None:
        y_a, h_m, gate_a, gate_m, g_mix, w_a, w_m, w_o = merge
        merge_specs = [row(W_A), row(W_M), row(d), row(d), mod_spec,
                       _resident((W_A, d)), _resident((W_M, d)), _resident((d, d))]
        merge_args = [y_a, h_m, gate_a, gate_m, g_mix, w_a.astype(BF16), w_m.astype(BF16), w_o.astype(BF16)]
    return pl.pallas_call(
        functools.partial(_ffn_kernel, n_chunks=nch, final_norm=final_norm, merge=merge is not None),
        out_shape=jax.ShapeDtypeStruct((t, d), F32),
        grid=(t // tm,),
        in_specs=[row(d)] + merge_specs + [
                  _resident((1, d)), mod_spec, mod_spec, mod_spec,
                  _resident((d, dff)), _resident((d, dff)),
                  _resident((nch, FFN_CHUNK, d)), _resident((1, d))],
        out_specs=row(d),
        scratch_shapes=[pltpu.VMEM((tm, d), BF16), pltpu.VMEM((tm, d), F32)],
        compiler_params=pltpu.CompilerParams(dimension_semantics=("arbitrary",),
                                             vmem_limit_bytes=VMEM_LIMIT),
        name="ffn_final" if final_norm else "ffn",
    )(x, *merge_args, gain.reshape(1, d), sh, sc, g, w1c, w3c, w2c, fin.reshape(1, d))


_C_QA = 0
_C_CKV = _C_QA + W_A
_C_QI = _C_CKV + D_LATENT
_C_KI = _C_QI + N_HEADS_IDX * LANES
_C_SM = _C_KI + LANES
_C_QK = _C_SM + LANES
_C_V = _C_QK + 2 * W_M
_C_O = _C_V + W_M
_C_GA = _C_O + W_M
_C_GM = _C_GA + 1024
_C_END = _C_GM + 1024
_SM_W = 0
_SM_I = N_HEADS_IDX


def _pack_w_in(w_in, d_model):
    splits = (W_A, D_LATENT, N_HEADS_IDX * HEAD_DIM_IDX, HEAD_DIM_IDX, N_HEADS_IDX,
              W_M, W_M, W_M, N_HEADS_M, N_HEADS_M, W_M, d_model, d_model)
    offs = [0]
    for s in splits:
        offs.append(offs[-1] + s)
    (q_a, c_kv, q_i, k_i, w_i, q_m, k_m, v_m, i_p, f_p, o_p, g_a, g_m) = [
        w_in[:, offs[n]:offs[n + 1]] for n in range(len(splits))]
    d = w_in.shape[0]

    def pad_heads(w, nh, hd):
        w = w.reshape(d, nh, hd)
        return jnp.pad(w, ((0, 0), (0, 0), (0, LANES - hd))).reshape(d, nh * LANES)

    small = jnp.concatenate([w_i, i_p, f_p], axis=1)
    small = jnp.pad(small, ((0, 0), (0, LANES - small.shape[1])))
    packed = jnp.concatenate([
        q_a, c_kv, pad_heads(q_i, N_HEADS_IDX, HEAD_DIM_IDX),
        jnp.pad(k_i, ((0, 0), (0, LANES - HEAD_DIM_IDX))), small, q_m, k_m, v_m, o_p, g_a, g_m], axis=1)
    assert packed.shape[1] == _C_END
    return packed.astype(BF16)


def _mixin_kernel(x_ref, gain_ref, sh_ref, sc_ref, w_ref, kvn_ref, wuk_ref, cw_ref, cb_ref,
                  qabs_ref, qidx_ref, kidx_ref, ckv_ref, ckvt_ref, wt_ref, ift_ref,
                  qk_ref, v_ref, o_ref, ga_ref, gm_ref, h_scr, xe_scr, *, tm, tiles_per_seq):
    nqb = tm // Q_BLOCK

    @pl.when(pl.program_id(0) % tiles_per_seq == 0)
    def _():
        xe_scr[:8] = jnp.zeros((8, xe_scr.shape[1]), F32)

    x = x_ref[...]
    h = _rms_norm(x, gain_ref[...]) * (1.0 + sc_ref[0]) + sh_ref[0]
    h_scr[...] = h.astype(BF16)

    def proj(lo, hi):
        return jnp.dot(h_scr[...], w_ref[:, lo:hi], preferred_element_type=F32)

    qa = proj(_C_QA, _C_CKV)
    scale = HEAD_DIM_A ** -0.5 * LOG2E
    for g in range(N_HEADS_A // 2):
        q_pair = qa[:, g * LANES:(g + 1) * LANES].astype(BF16)
        q_abs = jnp.dot(q_pair, wuk_ref[g], preferred_element_type=F32) * scale
        for j in range(2):
            qabs_ref[:, 2 * g + j] = (q_abs[:, j * D_LATENT:(j + 1) * D_LATENT]
                                      .astype(BF16).reshape(nqb, Q_BLOCK, D_LATENT))
    ckv = _rms_norm(proj(_C_CKV, _C_QI), kvn_ref[...])
    ckv_ref[...] = ckv.astype(BF16)
    ckv_t = ckv.T
    ones_row = jnp.where(lax.broadcasted_iota(I32, (CKVT_ROWS - D_LATENT, KEY_BLOCK), 0) == 0, 1.0, 0.0)
    for j in range(tm // KEY_BLOCK):
        ckvt_ref[j, :D_LATENT] = ckv_t[:, j * KEY_BLOCK:(j + 1) * KEY_BLOCK].astype(BF16)
        ckvt_ref[j, D_LATENT:] = ones_row.astype(BF16)
    qi = proj(_C_QI, _C_KI)
    for hh in range(N_HEADS_IDX):
        qidx_ref[:, hh] = qi[:, hh * LANES:(hh + 1) * LANES].astype(BF16).reshape(nqb, Q_BLOCK, LANES)
    kidx_ref[...] = proj(_C_KI, _C_SM).astype(BF16)
    small_t = proj(_C_SM, _C_QK).T
    wt_ref[...] = small_t[_SM_W:_SM_W + N_HEADS_IDX] * IDX_SCALE
    ift_ref[...] = small_t[_SM_I:_SM_I + 2 * N_HEADS_M]
    xe_scr[8:] = proj(_C_QK, _C_V)
    xe = xe_scr[...]
    xq = xe[8:]
    conv = xq * cw_ref[CONV_WIDTH - 1:CONV_WIDTH, :] + cb_ref[...]
    for d in range(1, CONV_WIDTH):
        conv = conv + pltpu.roll(xe, d, axis=0)[8:] * cw_ref[CONV_WIDTH - 1 - d:CONV_WIDTH - d, :]
    xe_scr[:8] = xe_scr[tm:]
    qk = conv * _sigmoid(conv)
    qk_ref[:, :W_M] = qk[:, :W_M].astype(BF16)
    qk_ref[:, W_M:] = (qk[:, W_M:] * (HEAD_DIM_M ** -0.5)).astype(BF16)
    v_ref[...] = proj(_C_V, _C_O).astype(BF16)
    o_ref[...] = proj(_C_O, _C_GA)
    ga_ref[...] = proj(_C_GA, _C_GM)
    gm_ref[...] = proj(_C_GM, _C_END)


def _mixin(x, gain, sh, sc, w_packed, kv_norm, wuk_t, conv_w, conv_b, *, seq):
    t, d = x.shape
    tm = MIX_TM
    per_b = seq // tm
    nqb = tm // Q_BLOCK
    row = lambda w: pl.BlockSpec((tm, w), lambda i: (i, 0))
    mod_spec = pl.BlockSpec((1, 1, d), lambda i: (i // per_b, 0, 0))
    out_shape = (
        jax.ShapeDtypeStruct((t // Q_BLOCK, N_HEADS_A, Q_BLOCK, D_LATENT), BF16),
        jax.ShapeDtypeStruct((t // Q_BLOCK, N_HEADS_IDX, Q_BLOCK, LANES), BF16),
        jax.ShapeDtypeStruct((t, LANES), BF16),
        jax.ShapeDtypeStruct((t, D_LATENT), BF16),
        jax.ShapeDtypeStruct((t // KEY_BLOCK, CKVT_ROWS, KEY_BLOCK), BF16),
        jax.ShapeDtypeStruct((N_HEADS_IDX, t), F32),
        jax.ShapeDtypeStruct((2 * N_HEADS_M, t), F32),
        jax.ShapeDtypeStruct((t, 2 * W_M), BF16),
        jax.ShapeDtypeStruct((t, W_M), BF16),
        jax.ShapeDtypeStruct((t, W_M), F32),
        jax.ShapeDtypeStruct((t, d), F32),
        jax.ShapeDtypeStruct((t, d), F32),
    )
    out_specs = (
        pl.BlockSpec((nqb, N_HEADS_A, Q_BLOCK, D_LATENT), lambda i: (i, 0, 0, 0)),
        pl.BlockSpec((nqb, N_HEADS_IDX, Q_BLOCK, LANES), lambda i: (i, 0, 0, 0)),
        row(LANES), row(D_LATENT),
        pl.BlockSpec((tm // KEY_BLOCK, CKVT_ROWS, KEY_BLOCK), lambda i: (i, 0, 0)),
        pl.BlockSpec((N_HEADS_IDX, tm), lambda i: (0, i)),
        pl.BlockSpec((2 * N_HEADS_M, tm), lambda i: (0, i)),
        row(2 * W_M), row(W_M), row(W_M), row(d), row(d),
    )
    return pl.pallas_call(
        functools.partial(_mixin_kernel, tm=tm, tiles_per_seq=per_b),
        out_shape=out_shape,
        grid=(t // tm,),
        in_specs=[pl.BlockSpec((tm, d), lambda i: (i, 0)), _resident((1, d)), mod_spec, mod_spec,
                  _resident((d, _C_END)), _resident((1, D_LATENT)),
                  _resident((N_HEADS_A // 2, LANES, 2 * D_LATENT)),
                  _resident((CONV_WIDTH, 2 * W_M)), _resident((1, 2 * W_M))],
        out_specs=out_specs,
        scratch_shapes=[pltpu.VMEM((tm, d), BF16), pltpu.VMEM((tm + 8, 2 * W_M), F32)],
        compiler_params=pltpu.CompilerParams(dimension_semantics=("arbitrary",),
                                             vmem_limit_bytes=VMEM_LIMIT),
        name="mixin",
    )(x, gain.reshape(1, d), sh, sc, w_packed, kv_norm.reshape(1, D_LATENT), wuk_t,
      conv_w, conv_b.reshape(1, -1))


def _sortable_key(score):
    bits = pltpu.bitcast(score, I32)
    bits = jnp.where(bits == INT_MIN, 0, bits)
    return jnp.where(bits < 0, bits ^ 0x7FFFFFFF, bits)


def _bit_transpose32(words):
    v = list(words)
    j, m = 16, 0x0000FFFF
    while j:
        k = 0
        while k < 32:
            t = (v[k] ^ lax.shift_right_logical(v[k + j], jnp.int32(j))) & m
            v[k] = v[k] ^ t
            v[k + j] = v[k + j] ^ (t << j)
            k = (k + j + 1) & ~j
        j >>= 1
        m = (m ^ (m << j)) & 0x7FFFFFFF
    return v


def _dsa_kernel(qidx_ref, qabs_ref, wt_ref, kidx_ref, ckv_ref, ckvt_ref, btile_ref, bmax_ref, wuvt_ref,
                out_ref, keys_scr, planes_scr, cand_scr, tau_scr, acc_scr, m_scr, ltc_scr, kmax_scr,
                qta_scr, qtc_scr, *, topk, n_qb):
    kb_sz = KEY_BLOCK
    step = pl.program_id(1)
    has_c = step >= 1
    qa = jnp.minimum(step, n_qb - 1)
    qc = jnp.maximum(step - 1, 0)
    slot_a = step & 1
    slot_c = 1 - slot_a
    n_a = qa // (kb_sz // Q_BLOCK) + 1
    n_c = qc // (kb_sz // Q_BLOCK) + 1
    qa0 = qa * Q_BLOCK
    qc0 = qc * Q_BLOCK
    row_id = lax.broadcasted_iota(I32, (kb_sz, LANES), 0)
    lane_id = lax.broadcasted_iota(I32, (kb_sz, LANES), 1)
    n_blocks = keys_scr.shape[1]
    n_groups = N_HEADS_A // 2
    pair = 2 * LANES
    ones8 = jnp.ones((8, D_LATENT), BF16)

    @pl.when(step == 0)
    def _():
        keys_scr[1, 0] = jnp.full((kb_sz, LANES), INT_MIN, I32)
        tau_scr[1] = jnp.zeros((1, LANES), I32)
        planes_scr[...] = jnp.zeros(planes_scr.shape, I32)

        def kn_body(kb, mx):
            c = ckv_ref[kb].astype(F32)
            n2 = lax.dot_general(ones8, (c * c).astype(BF16), (((1,), (1,)), ((), ())),
                                 preferred_element_type=F32)
            return jnp.maximum(mx, n2[0:1])
        mx = lax.fori_loop(0, n_blocks, kn_body, jnp.zeros((1, kb_sz), F32))
        kmax_scr[...] = jnp.max(mx, axis=1, keepdims=True)

    for g in range(n_groups):
        qi_g = qidx_ref[0, 2 * g:2 * g + 2].reshape(2 * Q_BLOCK, LANES).astype(F32)
        qta_scr[g] = qi_g.T.astype(BF16)
        qa_g = qabs_ref[0, 2 * g:2 * g + 2].reshape(2 * Q_BLOCK, D_LATENT).astype(F32)
        qtc_scr[g] = qa_g.T.astype(BF16)

    def idx_dot(kb, g):
        return jnp.dot(kidx_ref[kb], qta_scr[g], preferred_element_type=F32)

    def logits(kb, g):
        return jnp.dot(ckv_ref[kb], qtc_scr[g], preferred_element_type=F32)

    def bias_start(kb):
        delta = jnp.minimum(qc0 - kb * kb_sz, BIAS_PAD)
        return pl.multiple_of(BIAS_PAD - delta, LANES)

    qn2 = []
    for g in range(n_groups):
        q_g = qabs_ref[0, 2 * g:2 * g + 2].reshape(2 * Q_BLOCK, D_LATENT).astype(F32)
        qn2.append(lax.dot_general(ones8, (q_g * q_g).astype(BF16), (((1,), (1,)), ((), ())),
                                   preferred_element_type=F32)[0:1])
    bound = jnp.sqrt(jnp.concatenate(qn2, axis=1) * kmax_scr[...]) * 1.02 + bmax_ref[...] + 1e-3
    tau_c = tau_scr[slot_c]
    w_t = wt_ref[...]

    acc_scr[...] = jnp.zeros(acc_scr.shape, F32)
    for g in range(n_groups):
        ltc_scr[:, g * pair:(g + 1) * pair] = logits(0, g)

    bound_far = bound - jnp.concatenate([btile_ref[hh, 0:1, :] for hh in range(N_HEADS_A)], axis=1)

    def block_step(kb, far):
        kc = jnp.minimum(kb, n_c - 1)
        kc_next = jnp.minimum(kb + 1, n_c - 1)
        thr = jnp.where(has_c & (kb < n_c), tau_c - 1, jnp.int32(2 ** 31 - 1))
        sel = keys_scr[slot_c, kc] > thr
        ct_blk = ckvt_ref[kc]
        start = bias_start(kc)
        ref_pt = bound_far if far else bound
        score = jnp.zeros((kb_sz, LANES), F32)
        for g in range(n_groups):
            s_t = idx_dot(kb, g)
            for j in range(2):
                hh = 2 * g + j
                score = score + jnp.maximum(s_t[:, j * LANES:(j + 1) * LANES], 0.0) * w_t[hh:hh + 1, :]
        for g in range(n_groups):
            lt = ltc_scr[:, g * pair:(g + 1) * pair]
            ps = []
            for j in range(2):
                hh = 2 * g + j
                piece = lt[:, j * LANES:(j + 1) * LANES]
                if not far:
                    piece = piece + btile_ref[hh, pl.ds(start, kb_sz), :]
                ps.append(jnp.exp2(jnp.where(sel, piece, NEG_BIG) - ref_pt[:, hh * LANES:(hh + 1) * LANES]))
            ltc_scr[:, g * pair:(g + 1) * pair] = logits(kc_next, g)
            acc_scr[g] += jnp.dot(ct_blk, jnp.concatenate(ps, axis=1).astype(BF16),
                                  preferred_element_type=F32)
        valid = (kb * kb_sz + row_id) <= (qa0 + lane_id)
        keys = jnp.where(valid, _sortable_key(score), INT_MIN)
        keys_scr[slot_a, kb] = keys
        v = keys ^ INT_MIN
        for sub in range(kb_sz // PLANE_KEYS):
            r0 = sub * PLANE_KEYS
            words = _bit_transpose32([v[r0 + 8 * i:r0 + 8 * (i + 1), :] for i in range(32)])
            for bit in range(32):
                planes_scr[bit, kb * (kb_sz // PLANE_KEYS) + sub] = words[31 - bit]

    def block_body(far, it, carry):
        for u in range(BLOCK_UNROLL):
            block_step(it * BLOCK_UNROLL + u, far)
        return carry

    n_far_iters = jnp.maximum(n_c - 2, 0) // BLOCK_UNROLL
    lax.fori_loop(0, n_far_iters, functools.partial(block_body, True), 0)

    def near_body(kb, carry):
        block_step(kb, False)
        return carry

    lax.fori_loop(n_far_iters * BLOCK_UNROLL, n_a, near_body, 0)
    n_kb = n_a


    n_planes = planes_scr.shape[1]
    live = n_kb * (kb_sz // PLANE_KEYS)

    def radix_select(width):
        blk_id = lax.broadcasted_iota(I32, (width, 8, LANES), 0)
        cand_scr[:width] = jnp.where(blk_id < live, -1, 0)

        def bit_body(it, carry):
            above, tau_u = carry
            bit = 31 - it
            ones = cand_scr[:width] & planes_scr[bit, :width]
            c1 = jnp.sum(jnp.sum(lax.population_count(ones), axis=0), axis=0, keepdims=True)
            take = (above + c1) >= topk
            cand_scr[:width] = jnp.where(take, ones, cand_scr[:width] ^ ones)
            above = jnp.where(take, above, above + c1)
            tau_u = jnp.where(take, tau_u | (jnp.int32(1) << bit), tau_u)
            return above, tau_u

        zero = jnp.zeros((1, LANES), I32)
        above, tau_u = lax.fori_loop(0, 32, bit_body, (zero, zero))
        n_eq = jnp.sum(jnp.sum(lax.population_count(cand_scr[:width]), axis=0), axis=0, keepdims=True)
        return above, tau_u, n_eq

    widths = [n_planes * (i + 1) // 4 for i in range(4)]
    select = functools.partial(radix_select, widths[-1])
    for width in reversed(widths[:-1]):
        select = functools.partial(lax.cond, live <= width, functools.partial(radix_select, width), select)
    n_gt, tau_u, n_eq = select()
    tau = tau_u ^ INT_MIN

    need = topk - n_gt
    overflow = n_eq > need
    seq_bits = max(1, (n_blocks * kb_sz - 1).bit_length())

    @pl.when(jnp.max(jnp.where(overflow, 1, 0)) > 0)
    def _():
        def count_ties_before(trial):
            def body(kb, acc):
                hit = jnp.where((keys_scr[slot_a, kb] == tau) & ((kb * kb_sz + row_id) < trial), 1, 0)
                return acc + jnp.sum(hit.reshape(kb_sz // 8, 8, LANES), axis=0)
            acc = lax.fori_loop(0, n_kb, body, jnp.zeros((8, LANES), I32))
            return jnp.sum(acc, axis=0, keepdims=True)

        def idx_body(it, jc):
            trial = jc | (jnp.int32(1) << (seq_bits - 1 - it))
            return jnp.where(count_ties_before(trial) < need, trial, jc)

        j_cut = lax.fori_loop(0, seq_bits, idx_body, jnp.zeros((1, LANES), I32))

        def demote_body(kb, carry):
            k = keys_scr[slot_a, kb]
            drop = overflow & (k == tau) & ((kb * kb_sz + row_id) > j_cut)
            keys_scr[slot_a, kb] = jnp.where(drop, INT_MIN, k)
            return carry

        lax.fori_loop(0, n_kb, demote_body, 0)

    tau_scr[slot_a] = jnp.maximum(tau, INT_MIN + 1)

    l_min = jnp.min(jnp.concatenate([acc_scr[g, D_LATENT:D_LATENT + 1, :] for g in range(n_groups)], axis=1))

    @pl.when(has_c & jnp.logical_not(l_min >= 2.0 ** -80))
    def _():
        m_scr[...] = jnp.full(m_scr.shape, NEG_BIG, F32)
        acc_scr[...] = jnp.zeros(acc_scr.shape, F32)

        def exact_body(kb, carry):
            sel = keys_scr[slot_c, kb] >= tau_c
            ct_blk = ckvt_ref[kb]
            start = bias_start(kb)
            for g in range(n_groups):
                lt = logits(kb, g)
                ps, alphas = [], []
                for j in range(2):
                    hh = 2 * g + j
                    sl = slice(hh * LANES, (hh + 1) * LANES)
                    piece = lt[:, j * LANES:(j + 1) * LANES] + btile_ref[hh, pl.ds(start, kb_sz), :]
                    masked = jnp.where(sel, piece, NEG_BIG)
                    m_old = m_scr[:, sl]
                    m_new = jnp.maximum(m_old, jnp.max(masked, axis=0, keepdims=True))
                    m_scr[:, sl] = m_new
                    alphas.append(jnp.exp2(m_old - m_new))
                    ps.append(jnp.exp2(masked - m_new))
                pv = jnp.dot(ct_blk, jnp.concatenate(ps, axis=1).astype(BF16), preferred_element_type=F32)
                acc_scr[g] = jnp.concatenate(alphas, axis=1) * acc_scr[g] + pv
            return carry

        lax.fori_loop(0, n_c, exact_body, 0)

    @pl.when(has_c)
    def _():
        ys = []
        for hh in range(N_HEADS_A):
            acc_h = acc_scr[hh // 2, :, (hh % 2) * LANES:(hh % 2 + 1) * LANES]
            o_h = acc_h[:D_LATENT] * (1.0 / acc_h[D_LATENT:D_LATENT + 1])
            ys.append(jnp.dot(wuvt_ref[hh], o_h.astype(BF16), preferred_element_type=F32))
        y_t = jnp.concatenate(ys, axis=0)
        out_ref[...] = y_t.T.astype(BF16)


def _dsa(q_idx, q_abs, w_t, k_idx, ckv, ckv_t, btile, bmax, wuv_t, *, batch, seq):
    t = batch * seq
    nqb = seq // Q_BLOCK
    nkb = seq // KEY_BLOCK
    topk = min(TOPK_MAX, seq // 4)
    k_idx3 = k_idx.reshape(t // KEY_BLOCK, KEY_BLOCK, LANES)
    ckv3 = ckv.reshape(t // KEY_BLOCK, KEY_BLOCK, D_LATENT)
    per_batch = lambda shape: pl.BlockSpec(shape, lambda b, q: (b,) + (0,) * (len(shape) - 1),
                                           pipeline_mode=pl.Buffered(1))
    scored = lambda b, s: b * nqb + jnp.minimum(s, nqb - 1)
    attended = lambda b, s: b * nqb + jnp.maximum(s - 1, 0)
    return pl.pallas_call(
        functools.partial(_dsa_kernel, topk=topk, n_qb=nqb),
        out_shape=jax.ShapeDtypeStruct((t, W_A), BF16),
        grid=(batch, nqb + 1),
        in_specs=[pl.BlockSpec((1, N_HEADS_IDX, Q_BLOCK, LANES), lambda b, s: (scored(b, s), 0, 0, 0)),
                  pl.BlockSpec((1, N_HEADS_A, Q_BLOCK, D_LATENT), lambda b, s: (attended(b, s), 0, 0, 0)),
                  pl.BlockSpec((N_HEADS_IDX, Q_BLOCK), lambda b, s: (0, scored(b, s))),
                  per_batch((nkb, KEY_BLOCK, LANES)),
                  per_batch((nkb, KEY_BLOCK, D_LATENT)),
                  per_batch((nkb, CKVT_ROWS, KEY_BLOCK)),
                  _resident((N_HEADS_A, BIAS_ROWS, LANES)),
                  _resident((1, N_HEADS_A * LANES)),
                  _resident((N_HEADS_A, HEAD_DIM_A, D_LATENT))],
        out_specs=pl.BlockSpec((Q_BLOCK, W_A), lambda b, s: (attended(b, s), 0)),
        scratch_shapes=[pltpu.VMEM((2, nkb, KEY_BLOCK, LANES), I32),
                        pltpu.VMEM((32, seq // PLANE_KEYS + 1, 8, LANES), I32),
                        pltpu.VMEM((seq // PLANE_KEYS + 1, 8, LANES), I32),
                        pltpu.VMEM((2, 1, LANES), I32),
                        pltpu.VMEM((N_HEADS_A // 2, CKVT_ROWS, 2 * LANES), F32),
                        pltpu.VMEM((1, N_HEADS_A * LANES), F32),
                        pltpu.VMEM((KEY_BLOCK, (N_HEADS_A + 1) * LANES), F32),
                        pltpu.VMEM((1, 1), F32),
                        pltpu.VMEM((N_HEADS_IDX // 2, LANES, 2 * LANES), BF16),
                        pltpu.VMEM((N_HEADS_A // 2, D_LATENT, 2 * LANES), BF16)],
        compiler_params=pltpu.CompilerParams(dimension_semantics=("arbitrary", "arbitrary"),
                                             vmem_limit_bytes=VMEM_LIMIT),
        name="dsa",
    )(q_idx, q_abs, w_t, k_idx3, ckv3, ckv_t, btile, bmax, wuv_t)


def _mlstm_kernel(qk_ref, v_ref, o_ref, ift_ref, gbt_ref, hn_ref,
                  out_ref, cx_scr, m_scr, *, chunk, n_batch):
    L = chunk

    @pl.when(pl.program_id(0) == 0)
    def _():
        cx_scr[...] = jnp.zeros(cx_scr.shape, F32)
        m_scr[...] = jnp.zeros(m_scr.shape, F32)

    rr = lax.broadcasted_iota(I32, (L, L), 0)
    cc = lax.broadcasted_iota(I32, (L, L), 1)
    causal = cc <= rr
    triu = jnp.where(rr <= cc, 1.0, 0.0).astype(BF16)
    lane = lax.broadcasted_iota(I32, (8, L), 1)
    ones_col = jnp.where(lax.broadcasted_iota(I32, (L, HEAD_DIM_M), 1) == 0, 1.0, 0.0).astype(BF16)
    for bi in range(n_batch):
        _mlstm_chunk(qk_ref.at[bi], v_ref.at[bi], o_ref.at[bi], ift_ref.at[bi], gbt_ref, hn_ref,
                     out_ref.at[bi], cx_scr.at[bi], m_scr.at[bi], causal, triu, lane, ones_col, L)


def _mlstm_chunk(qk_ref, v_ref, o_ref, ift_ref, gbt_ref, hn_ref, out_ref, cx_scr, m_scr,
                 causal, triu, lane, ones_col, L):
    g_t = ift_ref[...] + gbt_ref[...]
    b_all = sum(jnp.dot(piece, triu, preferred_element_type=F32) for piece in _split3(_log_sigmoid(g_t)))
    b8 = pltpu.roll(b_all, N_HEADS_M, axis=0)
    a8 = g_t - b8
    cm = a8
    shift = 1
    while shift < L:
        cm = jnp.maximum(cm, jnp.where(lane >= shift, pltpu.roll(cm, shift, axis=1), NEG_BIG))
        shift *= 2
    m_prev = m_scr[...]
    mx = jnp.maximum(m_prev, cm)
    mx_last = mx[:, L - 1:L]
    decay8 = jnp.exp(m_prev - mx_last)
    m_scr[...] = b8[:, L - 1:L] + mx_last
    rows = jnp.concatenate([-mx,
                            jnp.exp(m_prev - mx),
                            jnp.exp(-(b8 + mx)),
                            jnp.exp(a8 - mx_last),
                            jnp.zeros((LANES - 32, L), F32)], axis=0)
    cols = rows.T

    o_gate = _sigmoid(o_ref[...])
    for hh in range(N_HEADS_M):
        hs = slice(hh * HEAD_DIM_M, (hh + 1) * HEAD_DIM_M)
        qb16 = qk_ref[:, hs]
        kb16 = qk_ref[:, W_M + hh * HEAD_DIM_M:W_M + (hh + 1) * HEAD_DIM_M]
        v_ext = jnp.concatenate([v_ref[:, hs], ones_col], axis=1)
        u_c = cols[:, hh:hh + 1]
        w_inter = cols[:, 8 + hh:9 + hh]
        em_c = cols[:, 16 + hh:17 + hh]
        wgt_c = cols[:, 24 + hh:25 + hh]
        cx_prev = cx_scr[hh]

        d_mat = jnp.where(causal, jnp.exp(u_c + a8[hh:hh + 1, :]), 0.0)
        s = lax.dot_general(qb16, kb16, (((1,), (1,)), ((), ())), preferred_element_type=F32) * d_mat
        intra = jnp.dot(s.astype(BF16), v_ext, preferred_element_type=F32)
        inter = jnp.dot(qb16, cx_prev.astype(BF16), preferred_element_type=F32)
        both = w_inter * inter + intra
        num = both[:, :HEAD_DIM_M]
        den = both[:, HEAD_DIM_M:HEAD_DIM_M + 1]
        hval = num / jnp.maximum(jnp.abs(den), em_c)

        kw = kb16.astype(F32) * wgt_c
        cx_scr[hh] = decay8[hh:hh + 1] * cx_prev + jnp.dot(kw.T.astype(BF16), v_ext,
                                                           preferred_element_type=F32)

        mu = jnp.mean(hval, axis=1, keepdims=True)
        cen = hval - mu
        var = jnp.mean(cen * cen, axis=1, keepdims=True)
        hn = cen * lax.rsqrt(var + EPS) * hn_ref[:, hs]
        out_ref[:, hs] = (hn * o_gate[:, hs]).astype(BF16)


def _mlstm(qk, v, o_pre, ift, gate_bias, head_norm, *, batch, seq):
    t = batch * seq
    L = MLSTM_CHUNK
    nc = seq // L
    gbt = jnp.broadcast_to(gate_bias.reshape(2 * N_HEADS_M, 1), (2 * N_HEADS_M, L))
    ift_b = ift.reshape(2 * N_HEADS_M, batch, seq).transpose(1, 0, 2)
    row = lambda w: pl.BlockSpec((batch, L, w), lambda c: (0, c, 0))
    out = pl.pallas_call(
        functools.partial(_mlstm_kernel, chunk=L, n_batch=batch),
        out_shape=jax.ShapeDtypeStruct((batch, seq, W_M), BF16),
        grid=(nc,),
        in_specs=[row(2 * W_M), row(W_M), row(W_M),
                  pl.BlockSpec((batch, 2 * N_HEADS_M, L), lambda c: (0, 0, c)),
                  _resident((2 * N_HEADS_M, L)), _resident((1, W_M))],
        out_specs=row(W_M),
        scratch_shapes=[pltpu.VMEM((batch, N_HEADS_M, HEAD_DIM_M, 2 * HEAD_DIM_M), F32),
                        pltpu.VMEM((batch, 8, 1), F32)],
        compiler_params=pltpu.CompilerParams(dimension_semantics=("arbitrary",),
                                             vmem_limit_bytes=VMEM_LIMIT),
        name="mlstm",
    )(qk.reshape(batch, seq, 2 * W_M), v.reshape(batch, seq, W_M), o_pre.reshape(batch, seq, W_M),
      ift_b, gbt, head_norm.reshape(1, -1))
    return out.reshape(t, W_M)


def kernel(x, c, ada_w, ada_b, ffn1_norm, ffn1_w1, ffn1_w3, ffn1_w2, mix_norm, w_in, conv_w, conv_b,
           kv_norm, w_uk, w_uv, mlstm_gate_bias, mlstm_head_norm, rel_bias, w_branch_attn,
           w_branch_mlstm, w_out, ffn2_norm, ffn2_w1, ffn2_w3, ffn2_w2, final_norm):
    batch, seq, d = x.shape
    depth = ada_w.shape[0]
    assert seq % max(FFN_TM, MIX_TM, MLSTM_CHUNK, KEY_BLOCK) == 0
    t = batch * seq
    xf = x.reshape(t, d)
    btile, bmax = _bias_tiles(rel_bias)
    for l in range(depth):
        mod = _adaln(c, ada_w[l], ada_b[l]).reshape(batch, 9, 1, d)
        sh1, sc1, g1, sh2, sc2, g2, sh3, sc3, g3 = [mod[:, n] for n in range(9)]
        xf = _ffn(xf, ffn1_norm[l], sh1, sc1, g1, ffn1_w1[l], ffn1_w3[l], ffn1_w2[l], final_norm,
                  seq=seq, final_norm=False)
        wuk_hdc = w_uk[l].transpose(0, 2, 1).reshape(N_HEADS_A // 2, 2, HEAD_DIM_A, D_LATENT)
        zeros = jnp.zeros_like(wuk_hdc[:, 0])
        wuk_t = jnp.concatenate([jnp.concatenate([wuk_hdc[:, 0], zeros], axis=2),
                                 jnp.concatenate([zeros, wuk_hdc[:, 1]], axis=2)], axis=1).astype(BF16)
        (q_abs, q_idx, k_idx, ckv, ckv_t, w_t, ift, qk_m, v_m, o_pre, gate_a, gate_m) = _mixin(
            xf, mix_norm[l], sh2, sc2, _pack_w_in(w_in[l], d), kv_norm[l], wuk_t, conv_w[l], conv_b[l],
            seq=seq)
        wuv_t = w_uv[l].transpose(0, 2, 1).astype(BF16)
        y_a = _dsa(q_idx, q_abs, w_t, k_idx, ckv, ckv_t, btile, bmax, wuv_t, batch=batch, seq=seq)
        h_m = _mlstm(qk_m, v_m, o_pre, ift, mlstm_gate_bias[l], mlstm_head_norm[l], batch=batch, seq=seq)
        xf = _ffn(xf, ffn2_norm[l], sh3, sc3, g3, ffn2_w1[l], ffn2_w3[l], ffn2_w2[l], final_norm,
                  seq=seq, final_norm=(l == depth - 1),
                  merge=(y_a, h_m, gate_a, gate_m, g2, w_branch_attn[l], w_branch_mlstm[l], w_out[l]))
    return xf.reshape(batch, seq, d)
```

```python
import functools
import math

import jax
import jax.numpy as jnp
from jax import lax
from jax.experimental import pallas as pl
from jax.experimental.pallas import tpu as pltpu

F32 = jnp.float32
BF16 = jnp.bfloat16
I32 = jnp.int32

LANES = 128
VMEM_LIMIT = 56 * 1024 * 1024

N_HEADS_A = 8
HEAD_DIM_A = 64
D_LATENT = 256
N_HEADS_IDX = 8
HEAD_DIM_IDX = 64
TOPK_MAX = 256
Q_BLOCK = 128
N_BUCKETS = 32
MAX_DISTANCE = 128
N_HEADS_M = 4
HEAD_DIM_M = 128
CONV_WIDTH = 4
EPS = 1e-6
IDX_SCALE = (N_HEADS_IDX ** -0.5) * (HEAD_DIM_IDX ** -0.5)
W_A = N_HEADS_A * HEAD_DIM_A
W_M = N_HEADS_M * HEAD_DIM_M

FFN_TM = 512
FFN_CHUNK = 256
MIX_TM = 512
KEY_BLOCK = 256
BLOCK_UNROLL = 6
PLANE_KEYS = 256
MLSTM_CHUNK = 256
NEG_BIG = -1e30
INT_MIN = -2 ** 31

BIAS_PAD = 2 * KEY_BLOCK - Q_BLOCK
BIAS_ROWS = KEY_BLOCK + BIAS_PAD
CKVT_ROWS = D_LATENT + 16
LOG2E = math.log2(math.e)


def _sigmoid(x):
    return 1.0 / (1.0 + jnp.exp(-x))


def _log_sigmoid(x):
    return jnp.minimum(x, 0.0) - jnp.log(1.0 + jnp.exp(-jnp.abs(x)))


def _rms_norm(x, gain):
    ms = jnp.mean(x * x, axis=-1, keepdims=True)
    return x * lax.rsqrt(ms + EPS) * gain


def _split3(x):
    hi = x.astype(BF16)
    r1 = x - hi.astype(F32)
    mid = r1.astype(BF16)
    lo = (r1 - mid.astype(F32)).astype(BF16)
    return hi, mid, lo


def _resident(shape):
    nd = len(shape)
    return pl.BlockSpec(shape, lambda *_: (0,) * nd, pipeline_mode=pl.Buffered(1))


def _adaln_kernel(c_ref, w_ref, b_ref, o_ref):
    c = c_ref[...]
    cond = c * _sigmoid(c)
    o_ref[...] = jnp.dot(cond.astype(BF16), w_ref[...].astype(BF16),
                         preferred_element_type=F32) + b_ref[...]


def _adaln(c, ada_w, ada_b):
    b, d = c.shape
    n = ada_w.shape[1]
    rows = 8
    c_pad = jnp.zeros((rows, d), F32).at[:b].set(c)
    tn = 1024
    out = pl.pallas_call(
        _adaln_kernel,
        out_shape=jax.ShapeDtypeStruct((rows, n), F32),
        grid=(n // tn,),
        in_specs=[pl.BlockSpec((rows, d), lambda j: (0, 0)),
                  pl.BlockSpec((d, tn), lambda j: (0, j)),
                  pl.BlockSpec((1, tn), lambda j: (0, j))],
        out_specs=pl.BlockSpec((rows, tn), lambda j: (0, j)),
        compiler_params=pltpu.CompilerParams(dimension_semantics=("arbitrary",),
                                             vmem_limit_bytes=VMEM_LIMIT),
        name="adaln",
    )(c_pad, ada_w, ada_b.reshape(1, n))
    return out[:b]


def _t5_bucket(dist):
    n = jnp.maximum(dist, 0)
    max_exact = N_BUCKETS // 2
    nf = jnp.maximum(n, 1).astype(F32)
    large = max_exact + (jnp.log(nf / max_exact) / math.log(MAX_DISTANCE / max_exact)
                         * (N_BUCKETS - max_exact)).astype(I32)
    large = jnp.minimum(large, N_BUCKETS - 1)
    return jnp.where(n < max_exact, n, large)


def _bias_kernel(rel_ref, tile_ref, max_ref):
    r = lax.broadcasted_iota(I32, (BIAS_ROWS, LANES), 0)
    i = lax.broadcasted_iota(I32, (BIAS_ROWS, LANES), 1)
    bucket = _t5_bucket(i - r + BIAS_PAD)
    for h in range(N_HEADS_A):
        acc = jnp.zeros((BIAS_ROWS, LANES), F32)
        top = rel_ref[0, h] * LOG2E
        for bkt in range(N_BUCKETS):
            val = rel_ref[bkt, h] * LOG2E
            acc = jnp.where(bucket == bkt, val, acc)
            top = jnp.maximum(top, val)
        tile_ref[h] = acc
        max_ref[:, h * LANES:(h + 1) * LANES] = jnp.full((1, LANES), top, F32)


def _bias_tiles(rel_bias):
    return pl.pallas_call(
        _bias_kernel,
        out_shape=(jax.ShapeDtypeStruct((N_HEADS_A, BIAS_ROWS, LANES), F32),
                   jax.ShapeDtypeStruct((1, N_HEADS_A * LANES), F32)),
        in_specs=[pl.BlockSpec(memory_space=pltpu.SMEM)],
        out_specs=(pl.BlockSpec(memory_space=pltpu.VMEM), pl.BlockSpec(memory_space=pltpu.VMEM)),
        name="bias_tiles",
    )(rel_bias)


def _ffn_kernel(*refs, n_chunks, final_norm, merge):
    if merge:
        (x_ref, ya_ref, hm_ref, ga_ref, gm_ref, gmix_ref, wa_ref, wm_ref, wo_ref), refs = refs[:9], refs[9:]
    else:
        x_ref, refs = refs[0], refs[1:]
    gain_ref, sh_ref, sc_ref, g_ref, w1_ref, w3_ref, w2_ref, fin_ref, o_ref, h_scr, acc_scr = refs
    x = x_ref[...]
    if merge:
        pa = jnp.dot(ya_ref[...], wa_ref[...], preferred_element_type=F32)
        pm = jnp.dot(hm_ref[...], wm_ref[...], preferred_element_type=F32)
        merged = _sigmoid(ga_ref[...]) * pa + _sigmoid(gm_ref[...]) * pm
        x = x + gmix_ref[0] * jnp.dot(merged.astype(BF16), wo_ref[...], preferred_element_type=F32)
    h = _rms_norm(x, gain_ref[...]) * (1.0 + sc_ref[0]) + sh_ref[0]
    h_scr[...] = h.astype(BF16)
    for j in range(n_chunks):
        hb = h_scr[...]
        cols = slice(j * FFN_CHUNK, (j + 1) * FFN_CHUNK)
        u1 = jnp.dot(hb, w1_ref[:, cols], preferred_element_type=F32)
        u3 = jnp.dot(hb, w3_ref[:, cols], preferred_element_type=F32)
        a = (u1 * _sigmoid(u1)) * u3
        part = jnp.dot(a.astype(BF16), w2_ref[j], preferred_element_type=F32)
        if j == 0:
            acc_scr[...] = part
        else:
            acc_scr[...] += part
    out = x + (0.5 * g_ref[0]) * acc_scr[...]
    if final_norm:
        out = _rms_norm(out, fin_ref[...])
    o_ref[...] = out


def _ffn(x, gain, sh, sc, g, w1, w3, w2, fin, *, seq, final_norm, merge=None):
    t, d = x.shape
    dff = w1.shape[1]
    nch = dff // FFN_CHUNK
    w1c = w1.astype(BF16)
    w3c = w3.astype(BF16)
    w2c = w2.astype(BF16).reshape(nch, FFN_CHUNK, d)
    tm = FFN_TM
    per_b = seq // tm
    row = lambda w: pl.BlockSpec((tm, w), lambda i: (i, 0))
    mod_spec = pl.BlockSpec((1, 1, d), lambda i: (i // per_b, 0, 0))
    merge_specs, merge_args = [], []
    if merge is not None:
        y_a, h_m, gate_a, gate_m, g_mix, w_a, w_m, w_o = merge
        merge_specs = [row(W_A), row(W_M), row(d), row(d), mod_spec,
                       _resident((W_A, d)), _resident((W_M, d)), _resident((d, d))]
        merge_args = [y_a, h_m, gate_a, gate_m, g_mix, w_a.astype(BF16), w_m.astype(BF16), w_o.astype(BF16)]
    return pl.pallas_call(
        functools.partial(_ffn_kernel, n_chunks=nch, final_norm=final_norm, merge=merge is not None),
        out_shape=jax.ShapeDtypeStruct((t, d), F32),
        grid=(t // tm,),
        in_specs=[row(d)] + merge_specs + [
                  _resident((1, d)), mod_spec, mod_spec, mod_spec,
                  _resident((d, dff)), _resident((d, dff)),
                  _resident((nch, FFN_CHUNK, d)), _resident((1, d))],
        out_specs=row(d),
        scratch_shapes=[pltpu.VMEM((tm, d), BF16), pltpu.VMEM((tm, d), F32)],
        compiler_params=pltpu.CompilerParams(dimension_semantics=("arbitrary",),
                                             vmem_limit_bytes=VMEM_LIMIT),
        name="ffn_final" if final_norm else "ffn",
    )(x, *merge_args, gain.reshape(1, d), sh, sc, g, w1c, w3c, w2c, fin.reshape(1, d))


_C_QA = 0
_C_CKV = _C_QA + W_A
_C_QI = _C_CKV + D_LATENT
_C_KI = _C_QI + N_HEADS_IDX * LANES
_C_SM = _C_KI + LANES
_C_QK = _C_SM + LANES
_C_V = _C_QK + 2 * W_M
_C_O = _C_V + W_M
_C_GA = _C_O + W_M
_C_GM = _C_GA + 1024
_C_END = _C_GM + 1024
_SM_W = 0
_SM_I = N_HEADS_IDX


def _pack_w_in(w_in, d_model):
    splits = (W_A, D_LATENT, N_HEADS_IDX * HEAD_DIM_IDX, HEAD_DIM_IDX, N_HEADS_IDX,
              W_M, W_M, W_M, N_HEADS_M, N_HEADS_M, W_M, d_model, d_model)
    offs = [0]
    for s in splits:
        offs.append(offs[-1] + s)
    (q_a, c_kv, q_i, k_i, w_i, q_m, k_m, v_m, i_p, f_p, o_p, g_a, g_m) = [
        w_in[:, offs[n]:offs[n + 1]] for n in range(len(splits))]
    d = w_in.shape[0]

    def pad_heads(w, nh, hd):
        w = w.reshape(d, nh, hd)
        return jnp.pad(w, ((0, 0), (0, 0), (0, LANES - hd))).reshape(d, nh * LANES)

    small = jnp.concatenate([w_i, i_p, f_p], axis=1)
    small = jnp.pad(small, ((0, 0), (0, LANES - small.shape[1])))
    packed = jnp.concatenate([
        q_a, c_kv, pad_heads(q_i, N_HEADS_IDX, HEAD_DIM_IDX),
        jnp.pad(k_i, ((0, 0), (0, LANES - HEAD_DIM_IDX))), small, q_m, k_m, v_m, o_p, g_a, g_m], axis=1)
    assert packed.shape[1] == _C_END
    return packed.astype(BF16)


def _mixin_kernel(x_ref, gain_ref, sh_ref, sc_ref, w_ref, kvn_ref, wuk_ref, cw_ref, cb_ref,
                  qabs_ref, qidx_ref, kidx_ref, ckv_ref, ckvt_ref, wt_ref, ift_ref,
                  qk_ref, v_ref, o_ref, ga_ref, gm_ref, h_scr, xe_scr, *, tm, tiles_per_seq):
    nqb = tm // Q_BLOCK

    @pl.when(pl.program_id(0) % tiles_per_seq == 0)
    def _():
        xe_scr[:8] = jnp.zeros((8, xe_scr.shape[1]), F32)

    x = x_ref[...]
    h = _rms_norm(x, gain_ref[...]) * (1.0 + sc_ref[0]) + sh_ref[0]
    h_scr[...] = h.astype(BF16)

    def proj(lo, hi):
        return jnp.dot(h_scr[...], w_ref[:, lo:hi], preferred_element_type=F32)

    qa = proj(_C_QA, _C_CKV)
    scale = HEAD_DIM_A ** -0.5 * LOG2E
    for g in range(N_HEADS_A // 2):
        q_pair = qa[:, g * LANES:(g + 1) * LANES].astype(BF16)
        q_abs = jnp.dot(q_pair, wuk_ref[g], preferred_element_type=F32) * scale
        for j in range(2):
            qabs_ref[:, 2 * g + j] = (q_abs[:, j * D_LATENT:(j + 1) * D_LATENT]
                                      .astype(BF16).reshape(nqb, Q_BLOCK, D_LATENT))
    ckv = _rms_norm(proj(_C_CKV, _C_QI), kvn_ref[...])
    ckv_ref[...] = ckv.astype(BF16)
    ckv_t = ckv.T
    ones_row = jnp.where(lax.broadcasted_iota(I32, (CKVT_ROWS - D_LATENT, KEY_BLOCK), 0) == 0, 1.0, 0.0)
    for j in range(tm // KEY_BLOCK):
        ckvt_ref[j, :D_LATENT] = ckv_t[:, j * KEY_BLOCK:(j + 1) * KEY_BLOCK].astype(BF16)
        ckvt_ref[j, D_LATENT:] = ones_row.astype(BF16)
    qi = proj(_C_QI, _C_KI)
    for hh in range(N_HEADS_IDX):
        qidx_ref[:, hh] = qi[:, hh * LANES:(hh + 1) * LANES].astype(BF16).reshape(nqb, Q_BLOCK, LANES)
    kidx_ref[...] = proj(_C_KI, _C_SM).astype(BF16)
    small_t = proj(_C_SM, _C_QK).T
    wt_ref[...] = small_t[_SM_W:_SM_W + N_HEADS_IDX] * IDX_SCALE
    ift_ref[...] = small_t[_SM_I:_SM_I + 2 * N_HEADS_M]
    xe_scr[8:] = proj(_C_QK, _C_V)
    xe = xe_scr[...]
    xq = xe[8:]
    conv = xq * cw_ref[CONV_WIDTH - 1:CONV_WIDTH, :] + cb_ref[...]
    for d in range(1, CONV_WIDTH):
        conv = conv + pltpu.roll(xe, d, axis=0)[8:] * cw_ref[CONV_WIDTH - 1 - d:CONV_WIDTH - d, :]
    xe_scr[:8] = xe_scr[tm:]
    qk = conv * _sigmoid(conv)
    qk_ref[:, :W_M] = qk[:, :W_M].astype(BF16)
    qk_ref[:, W_M:] = (qk[:, W_M:] * (HEAD_DIM_M ** -0.5)).astype(BF16)
    v_ref[...] = proj(_C_V, _C_O).astype(BF16)
    o_ref[...] = proj(_C_O, _C_GA)
    ga_ref[...] = proj(_C_GA, _C_GM)
    gm_ref[...] = proj(_C_GM, _C_END)


def _mixin(x, gain, sh, sc, w_packed, kv_norm, wuk_t, conv_w, conv_b, *, seq):
    t, d = x.shape
    tm = MIX_TM
    per_b = seq // tm
    nqb = tm // Q_BLOCK
    row = lambda w: pl.BlockSpec((tm, w), lambda i: (i, 0))
    mod_spec = pl.BlockSpec((1, 1, d), lambda i: (i // per_b, 0, 0))
    out_shape = (
        jax.ShapeDtypeStruct((t // Q_BLOCK, N_HEADS_A, Q_BLOCK, D_LATENT), BF16),
        jax.ShapeDtypeStruct((t // Q_BLOCK, N_HEADS_IDX, Q_BLOCK, LANES), BF16),
        jax.ShapeDtypeStruct((t, LANES), BF16),
        jax.ShapeDtypeStruct((t, D_LATENT), BF16),
        jax.ShapeDtypeStruct((t // KEY_BLOCK, CKVT_ROWS, KEY_BLOCK), BF16),
        jax.ShapeDtypeStruct((N_HEADS_IDX, t), F32),
        jax.ShapeDtypeStruct((2 * N_HEADS_M, t), F32),
        jax.ShapeDtypeStruct((t, 2 * W_M), BF16),
        jax.ShapeDtypeStruct((t, W_M), BF16),
        jax.ShapeDtypeStruct((t, W_M), F32),
        jax.ShapeDtypeStruct((t, d), F32),
        jax.ShapeDtypeStruct((t, d), F32),
    )
    out_specs = (
        pl.BlockSpec((nqb, N_HEADS_A, Q_BLOCK, D_LATENT), lambda i: (i, 0, 0, 0)),
        pl.BlockSpec((nqb, N_HEADS_IDX, Q_BLOCK, LANES), lambda i: (i, 0, 0, 0)),
        row(LANES), row(D_LATENT),
        pl.BlockSpec((tm // KEY_BLOCK, CKVT_ROWS, KEY_BLOCK), lambda i: (i, 0, 0)),
        pl.BlockSpec((N_HEADS_IDX, tm), lambda i: (0, i)),
        pl.BlockSpec((2 * N_HEADS_M, tm), lambda i: (0, i)),
        row(2 * W_M), row(W_M), row(W_M), row(d), row(d),
    )
    return pl.pallas_call(
        functools.partial(_mixin_kernel, tm=tm, tiles_per_seq=per_b),
        out_shape=out_shape,
        grid=(t // tm,),
        in_specs=[pl.BlockSpec((tm, d), lambda i: (i, 0)), _resident((1, d)), mod_spec, mod_spec,
                  _resident((d, _C_END)), _resident((1, D_LATENT)),
                  _resident((N_HEADS_A // 2, LANES, 2 * D_LATENT)),
                  _resident((CONV_WIDTH, 2 * W_M)), _resident((1, 2 * W_M))],
        out_specs=out_specs,
        scratch_shapes=[pltpu.VMEM((tm, d), BF16), pltpu.VMEM((tm + 8, 2 * W_M), F32)],
        compiler_params=pltpu.CompilerParams(dimension_semantics=("arbitrary",),
                                             vmem_limit_bytes=VMEM_LIMIT),
        name="mixin",
    )(x, gain.reshape(1, d), sh, sc, w_packed, kv_norm.reshape(1, D_LATENT), wuk_t,
      conv_w, conv_b.reshape(1, -1))


def _sortable_key(score):
    bits = pltpu.bitcast(score, I32)
    bits = jnp.where(bits == INT_MIN, 0, bits)
    return jnp.where(bits < 0, bits ^ 0x7FFFFFFF, bits)


def _bit_transpose32(words):
    v = list(words)
    j, m = 16, 0x0000FFFF
    while j:
        k = 0
        while k < 32:
            t = (v[k] ^ lax.shift_right_logical(v[k + j], jnp.int32(j))) & m
            v[k] = v[k] ^ t
            v[k + j] = v[k + j] ^ (t << j)
            k = (k + j + 1) & ~j
        j >>= 1
        m = (m ^ (m << j)) & 0x7FFFFFFF
    return v


def _dsa_kernel(qidx_ref, qabs_ref, wt_ref, kidx_ref, ckv_ref, ckvt_ref, btile_ref, bmax_ref, wuvt_ref,
                out_ref, keys_scr, planes_scr, cand_scr, tau_scr, acc_scr, m_scr, ltc_scr, kmax_scr,
                qta_scr, qtc_scr, *, topk, n_qb):
    kb_sz = KEY_BLOCK
    step = pl.program_id(1)
    has_c = step >= 1
    qa = jnp.minimum(step, n_qb - 1)
    qc = jnp.maximum(step - 1, 0)
    slot_a = step & 1
    slot_c = 1 - slot_a
    n_a = qa // (kb_sz // Q_BLOCK) + 1
    n_c = qc // (kb_sz // Q_BLOCK) + 1
    qa0 = qa * Q_BLOCK
    qc0 = qc * Q_BLOCK
    row_id = lax.broadcasted_iota(I32, (kb_sz, LANES), 0)
    lane_id = lax.broadcasted_iota(I32, (kb_sz, LANES), 1)
    n_blocks = keys_scr.shape[1]
    n_groups = N_HEADS_A // 2
    pair = 2 * LANES
    ones8 = jnp.ones((8, D_LATENT), BF16)

    @pl.when(step == 0)
    def _():
        keys_scr[1, 0] = jnp.full((kb_sz, LANES), INT_MIN, I32)
        tau_scr[1] = jnp.zeros((1, LANES), I32)
        planes_scr[...] = jnp.zeros(planes_scr.shape, I32)

        def kn_body(kb, mx):
            c = ckv_ref[kb].astype(F32)
            n2 = lax.dot_general(ones8, (c * c).astype(BF16), (((1,), (1,)), ((), ())),
                                 preferred_element_type=F32)
            return jnp.maximum(mx, n2[0:1])
        mx = lax.fori_loop(0, n_blocks, kn_body, jnp.zeros((1, kb_sz), F32))
        kmax_scr[...] = jnp.max(mx, axis=1, keepdims=True)

    for g in range(n_groups):
        qi_g = qidx_ref[0, 2 * g:2 * g + 2].reshape(2 * Q_BLOCK, LANES).astype(F32)
        qta_scr[g] = qi_g.T.astype(BF16)
        qa_g = qabs_ref[0, 2 * g:2 * g + 2].reshape(2 * Q_BLOCK, D_LATENT).astype(F32)
        qtc_scr[g] = qa_g.T.astype(BF16)

    def idx_dot(kb, g):
        return jnp.dot(kidx_ref[kb], qta_scr[g], preferred_element_type=F32)

    def logits(kb, g):
        return jnp.dot(ckv_ref[kb], qtc_scr[g], preferred_element_type=F32)

    def bias_start(kb):
        delta = jnp.minimum(qc0 - kb * kb_sz, BIAS_PAD)
        return pl.multiple_of(BIAS_PAD - delta, LANES)

    qn2 = []
    for g in range(n_groups):
        q_g = qabs_ref[0, 2 * g:2 * g + 2].reshape(2 * Q_BLOCK, D_LATENT).astype(F32)
        qn2.append(lax.dot_general(ones8, (q_g * q_g).astype(BF16), (((1,), (1,)), ((), ())),
                                   preferred_element_type=F32)[0:1])
    bound = jnp.sqrt(jnp.concatenate(qn2, axis=1) * kmax_scr[...]) * 1.02 + bmax_ref[...] + 1e-3
    tau_c = tau_scr[slot_c]
    w_t = wt_ref[...]

    @pl.when(step == 0)
    def _():
        acc_scr[...] = jnp.zeros(acc_scr.shape, F32)

    for g in range(n_groups):
        ltc_scr[:, g * pair:(g + 1) * pair] = logits(0, g)

    bound_far = bound - jnp.concatenate([btile_ref[hh, 0:1, :] for hh in range(N_HEADS_A)], axis=1)

    def block_step(kb, far):
        kc = jnp.minimum(kb, n_c - 1)
        kc_next = jnp.minimum(kb + 1, n_c - 1)
        thr = jnp.where(has_c & (kb < n_c), tau_c - 1, jnp.int32(2 ** 31 - 1))
        sel = keys_scr[slot_c, kc] > thr
        ct_blk = ckvt_ref[kc]
        start = bias_start(kc)
        ref_pt = bound_far if far else bound
        score = jnp.zeros((kb_sz, LANES), F32)
        for g in range(n_groups):
            s_t = idx_dot(kb, g)
            for j in range(2):
                hh = 2 * g + j
                score = score + jnp.maximum(s_t[:, j * LANES:(j + 1) * LANES], 0.0) * w_t[hh:hh + 1, :]
        for g in range(n_groups):
            lt = ltc_scr[:, g * pair:(g + 1) * pair]
            ps = []
            for j in range(2):
                hh = 2 * g + j
                piece = lt[:, j * LANES:(j + 1) * LANES]
                if not far:
                    piece = piece + btile_ref[hh, pl.ds(start, kb_sz), :]
                ps.append(jnp.exp2(jnp.where(sel, piece, NEG_BIG) - ref_pt[:, hh * LANES:(hh + 1) * LANES]))
            ltc_scr[:, g * pair:(g + 1) * pair] = logits(kc_next, g)
            acc_scr[g] += jnp.dot(ct_blk, jnp.concatenate(ps, axis=1).astype(BF16),
                                  preferred_element_type=F32)
        valid = (kb * kb_sz + row_id) <= (qa0 + lane_id)
        keys = jnp.where(valid, _sortable_key(score), INT_MIN)
        keys_scr[slot_a, kb] = keys
        v = keys ^ INT_MIN
        for sub in range(kb_sz // PLANE_KEYS):
            r0 = sub * PLANE_KEYS
            words = _bit_transpose32([v[r0 + 8 * i:r0 + 8 * (i + 1), :] for i in range(32)])
            for bit in range(32):
                planes_scr[bit, kb * (kb_sz // PLANE_KEYS) + sub] = words[31 - bit]

    def block_body(far, it, carry):
        for u in range(BLOCK_UNROLL):
            block_step(it * BLOCK_UNROLL + u, far)
        return carry

    n_far_iters = jnp.maximum(n_c - 2, 0) // BLOCK_UNROLL
    lax.fori_loop(0, n_far_iters, functools.partial(block_body, True), 0)

    def near_body(kb, carry):
        block_step(kb, False)
        return carry

    lax.fori_loop(n_far_iters * BLOCK_UNROLL, n_a, near_body, 0)
    n_kb = n_a


    n_planes = planes_scr.shape[1]
    live = n_kb * (kb_sz // PLANE_KEYS)

    def radix_select(width):
        blk_id = lax.broadcasted_iota(I32, (width, 8, LANES), 0)
        cand_scr[:width] = jnp.where(blk_id < live, -1, 0)

        def bit_body(it, carry):
            above, tau_u = carry
            bit = 31 - it
            ones = cand_scr[:width] & planes_scr[bit, :width]
            c1 = jnp.sum(jnp.sum(lax.population_count(ones), axis=0), axis=0, keepdims=True)
            take = (above + c1) >= topk
            cand_scr[:width] = jnp.where(take, ones, cand_scr[:width] ^ ones)
            above = jnp.where(take, above, above + c1)
            tau_u = jnp.where(take, tau_u | (jnp.int32(1) << bit), tau_u)
            return above, tau_u

        zero = jnp.zeros((1, LANES), I32)
        above, tau_u = lax.fori_loop(0, 32, bit_body, (zero, zero))
        n_eq = jnp.sum(jnp.sum(lax.population_count(cand_scr[:width]), axis=0), axis=0, keepdims=True)
        return above, tau_u, n_eq

    widths = [n_planes * (i + 1) // 4 for i in range(4)]
    select = functools.partial(radix_select, widths[-1])
    for width in reversed(widths[:-1]):
        select = functools.partial(lax.cond, live <= width, functools.partial(radix_select, width), select)
    n_gt, tau_u, n_eq = select()
    tau = tau_u ^ INT_MIN

    need = topk - n_gt
    overflow = n_eq > need
    seq_bits = max(1, (n_blocks * kb_sz - 1).bit_length())

    @pl.when(jnp.max(jnp.where(overflow, 1, 0)) > 0)
    def _():
        def count_ties_before(trial):
            def body(kb, acc):
                hit = jnp.where((keys_scr[slot_a, kb] == tau) & ((kb * kb_sz + row_id) < trial), 1, 0)
                return acc + jnp.sum(hit.reshape(kb_sz // 8, 8, LANES), axis=0)
            acc = lax.fori_loop(0, n_kb, body, jnp.zeros((8, LANES), I32))
            return jnp.sum(acc, axis=0, keepdims=True)

        def idx_body(it, jc):
            trial = jc | (jnp.int32(1) << (seq_bits - 1 - it))
            return jnp.where(count_ties_before(trial) < need, trial, jc)

        j_cut = lax.fori_loop(0, seq_bits, idx_body, jnp.zeros((1, LANES), I32))

        def demote_body(kb, carry):
            k = keys_scr[slot_a, kb]
            drop = overflow & (k == tau) & ((kb * kb_sz + row_id) > j_cut)
            keys_scr[slot_a, kb] = jnp.where(drop, INT_MIN, k)
            return carry

        lax.fori_loop(0, n_kb, demote_body, 0)

    tau_scr[slot_a] = jnp.maximum(tau, INT_MIN + 1)

    l_min = jnp.min(jnp.concatenate([acc_scr[g, D_LATENT:D_LATENT + 1, :] for g in range(n_groups)], axis=1))

    @pl.when(has_c & jnp.logical_not(l_min >= 2.0 ** -80))
    def _():
        m_scr[...] = jnp.full(m_scr.shape, NEG_BIG, F32)
        acc_scr[...] = jnp.zeros(acc_scr.shape, F32)

        def exact_body(kb, carry):
            sel = keys_scr[slot_c, kb] >= tau_c
            ct_blk = ckvt_ref[kb]
            start = bias_start(kb)
            for g in range(n_groups):
                lt = logits(kb, g)
                ps, alphas = [], []
                for j in range(2):
                    hh = 2 * g + j
                    sl = slice(hh * LANES, (hh + 1) * LANES)
                    piece = lt[:, j * LANES:(j + 1) * LANES] + btile_ref[hh, pl.ds(start, kb_sz), :]
                    masked = jnp.where(sel, piece, NEG_BIG)
                    m_old = m_scr[:, sl]
                    m_new = jnp.maximum(m_old, jnp.max(masked, axis=0, keepdims=True))
                    m_scr[:, sl] = m_new
                    alphas.append(jnp.exp2(m_old - m_new))
                    ps.append(jnp.exp2(masked - m_new))
                pv = jnp.dot(ct_blk, jnp.concatenate(ps, axis=1).astype(BF16), preferred_element_type=F32)
                acc_scr[g] = jnp.concatenate(alphas, axis=1) * acc_scr[g] + pv
            return carry

        lax.fori_loop(0, n_c, exact_body, 0)

    @pl.when(has_c)
    def _():
        ys = []
        for hh in range(N_HEADS_A):
            acc_h = acc_scr[hh // 2, :, (hh % 2) * LANES:(hh % 2 + 1) * LANES]
            o_h = acc_h[:D_LATENT] * (1.0 / acc_h[D_LATENT:D_LATENT + 1])
            ys.append(jnp.dot(wuvt_ref[hh], o_h.astype(BF16), preferred_element_type=F32))
        y_t = jnp.concatenate(ys, axis=0)
        out_ref[...] = y_t.T.astype(BF16)
        acc_scr[...] = jnp.zeros(acc_scr.shape, F32)


def _dsa(q_idx, q_abs, w_t, k_idx, ckv, ckv_t, btile, bmax, wuv_t, *, batch, seq):
    t = batch * seq
    nqb = seq // Q_BLOCK
    nkb = seq // KEY_BLOCK
    topk = min(TOPK_MAX, seq // 4)
    k_idx3 = k_idx.reshape(t // KEY_BLOCK, KEY_BLOCK, LANES)
    ckv3 = ckv.reshape(t // KEY_BLOCK, KEY_BLOCK, D_LATENT)
    per_batch = lambda shape: pl.BlockSpec(shape, lambda b, q: (b,) + (0,) * (len(shape) - 1),
                                           pipeline_mode=pl.Buffered(1))
    scored = lambda b, s: b * nqb + jnp.minimum(s, nqb - 1)
    attended = lambda b, s: b * nqb + jnp.maximum(s - 1, 0)
    return pl.pallas_call(
        functools.partial(_dsa_kernel, topk=topk, n_qb=nqb),
        out_shape=jax.ShapeDtypeStruct((t, W_A), BF16),
        grid=(batch, nqb + 1),
        in_specs=[pl.BlockSpec((1, N_HEADS_IDX, Q_BLOCK, LANES), lambda b, s: (scored(b, s), 0, 0, 0)),
                  pl.BlockSpec((1, N_HEADS_A, Q_BLOCK, D_LATENT), lambda b, s: (attended(b, s), 0, 0, 0)),
                  pl.BlockSpec((N_HEADS_IDX, Q_BLOCK), lambda b, s: (0, scored(b, s))),
                  per_batch((nkb, KEY_BLOCK, LANES)),
                  per_batch((nkb, KEY_BLOCK, D_LATENT)),
                  per_batch((nkb, CKVT_ROWS, KEY_BLOCK)),
                  _resident((N_HEADS_A, BIAS_ROWS, LANES)),
                  _resident((1, N_HEADS_A * LANES)),
                  _resident((N_HEADS_A, HEAD_DIM_A, D_LATENT))],
        out_specs=pl.BlockSpec((Q_BLOCK, W_A), lambda b, s: (attended(b, s), 0)),
        scratch_shapes=[pltpu.VMEM((2, nkb, KEY_BLOCK, LANES), I32),
                        pltpu.VMEM((32, seq // PLANE_KEYS + 1, 8, LANES), I32),
                        pltpu.VMEM((seq // PLANE_KEYS + 1, 8, LANES), I32),
                        pltpu.VMEM((2, 1, LANES), I32),
                        pltpu.VMEM((N_HEADS_A // 2, CKVT_ROWS, 2 * LANES), F32),
                        pltpu.VMEM((1, N_HEADS_A * LANES), F32),
                        pltpu.VMEM((KEY_BLOCK, (N_HEADS_A + 1) * LANES), F32),
                        pltpu.VMEM((1, 1), F32),
                        pltpu.VMEM((N_HEADS_IDX // 2, LANES, 2 * LANES), BF16),
                        pltpu.VMEM((N_HEADS_A // 2, D_LATENT, 2 * LANES), BF16)],
        compiler_params=pltpu.CompilerParams(dimension_semantics=("arbitrary", "arbitrary"),
                                             vmem_limit_bytes=VMEM_LIMIT),
        name="dsa",
    )(q_idx, q_abs, w_t, k_idx3, ckv3, ckv_t, btile, bmax, wuv_t)


def _mlstm_kernel(qk_ref, v_ref, o_ref, ift_ref, gbt_ref, hn_ref,
                  out_ref, cx_scr, m_scr, *, chunk, n_batch):
    L = chunk

    @pl.when(pl.program_id(0) == 0)
    def _():
        cx_scr[...] = jnp.zeros(cx_scr.shape, F32)
        m_scr[...] = jnp.zeros(m_scr.shape, F32)

    rr = lax.broadcasted_iota(I32, (L, L), 0)
    cc = lax.broadcasted_iota(I32, (L, L), 1)
    causal = cc <= rr
    triu = jnp.where(rr <= cc, 1.0, 0.0).astype(BF16)
    lane = lax.broadcasted_iota(I32, (8, L), 1)
    ones_col = jnp.where(lax.broadcasted_iota(I32, (L, HEAD_DIM_M), 1) == 0, 1.0, 0.0).astype(BF16)
    for bi in range(n_batch):
        _mlstm_chunk(qk_ref.at[bi], v_ref.at[bi], o_ref.at[bi], ift_ref.at[bi], gbt_ref, hn_ref,
                     out_ref.at[bi], cx_scr.at[bi], m_scr.at[bi], causal, triu, lane, ones_col, L)


def _mlstm_chunk(qk_ref, v_ref, o_ref, ift_ref, gbt_ref, hn_ref, out_ref, cx_scr, m_scr,
                 causal, triu, lane, ones_col, L):
    g_t = ift_ref[...] + gbt_ref[...]
    b_all = sum(jnp.dot(piece, triu, preferred_element_type=F32) for piece in _split3(_log_sigmoid(g_t)))
    b8 = pltpu.roll(b_all, N_HEADS_M, axis=0)
    a8 = g_t - b8
    cm = a8
    shift = 1
    while shift < L:
        cm = jnp.maximum(cm, jnp.where(lane >= shift, pltpu.roll(cm, shift, axis=1), NEG_BIG))
        shift *= 2
    m_prev = m_scr[...]
    mx = jnp.maximum(m_prev, cm)
    mx_last = mx[:, L - 1:L]
    decay8 = jnp.exp(m_prev - mx_last)
    m_scr[...] = b8[:, L - 1:L] + mx_last
    rows = jnp.concatenate([-mx,
                            jnp.exp(m_prev - mx),
                            jnp.exp(-(b8 + mx)),
                            jnp.exp(a8 - mx_last),
                            jnp.zeros((LANES - 32, L), F32)], axis=0)
    cols = rows.T

    o_gate = _sigmoid(o_ref[...])
    for hh in range(N_HEADS_M):
        hs = slice(hh * HEAD_DIM_M, (hh + 1) * HEAD_DIM_M)
        qb16 = qk_ref[:, hs]
        kb16 = qk_ref[:, W_M + hh * HEAD_DIM_M:W_M + (hh + 1) * HEAD_DIM_M]
        v_ext = jnp.concatenate([v_ref[:, hs], ones_col], axis=1)
        u_c = cols[:, hh:hh + 1]
        w_inter = cols[:, 8 + hh:9 + hh]
        em_c = cols[:, 16 + hh:17 + hh]
        wgt_c = cols[:, 24 + hh:25 + hh]
        cx_prev = cx_scr[hh]

        d_mat = jnp.where(causal, jnp.exp(u_c + a8[hh:hh + 1, :]), 0.0)
        s = lax.dot_general(qb16, kb16, (((1,), (1,)), ((), ())), preferred_element_type=F32) * d_mat
        intra = jnp.dot(s.astype(BF16), v_ext, preferred_element_type=F32)
        inter = jnp.dot(qb16, cx_prev.astype(BF16), preferred_element_type=F32)
        both = w_inter * inter + intra
        num = both[:, :HEAD_DIM_M]
        den = both[:, HEAD_DIM_M:HEAD_DIM_M + 1]
        hval = num / jnp.maximum(jnp.abs(den), em_c)

        kw = kb16.astype(F32) * wgt_c
        cx_scr[hh] = decay8[hh:hh + 1] * cx_prev + jnp.dot(kw.T.astype(BF16), v_ext,
                                                           preferred_element_type=F32)

        mu = jnp.mean(hval, axis=1, keepdims=True)
        cen = hval - mu
        var = jnp.mean(cen * cen, axis=1, keepdims=True)
        hn = cen * lax.rsqrt(var + EPS) * hn_ref[:, hs]
        out_ref[:, hs] = (hn * o_gate[:, hs]).astype(BF16)


def _mlstm(qk, v, o_pre, ift, gate_bias, head_norm, *, batch, seq):
    t = batch * seq
    L = MLSTM_CHUNK
    nc = seq // L
    gbt = jnp.broadcast_to(gate_bias.reshape(2 * N_HEADS_M, 1), (2 * N_HEADS_M, L))
    ift_b = ift.reshape(2 * N_HEADS_M, batch, seq).transpose(1, 0, 2)
    row = lambda w: pl.BlockSpec((batch, L, w), lambda c: (0, c, 0))
    out = pl.pallas_call(
        functools.partial(_mlstm_kernel, chunk=L, n_batch=batch),
        out_shape=jax.ShapeDtypeStruct((batch, seq, W_M), BF16),
        grid=(nc,),
        in_specs=[row(2 * W_M), row(W_M), row(W_M),
                  pl.BlockSpec((batch, 2 * N_HEADS_M, L), lambda c: (0, 0, c)),
                  _resident((2 * N_HEADS_M, L)), _resident((1, W_M))],
        out_specs=row(W_M),
        scratch_shapes=[pltpu.VMEM((batch, N_HEADS_M, HEAD_DIM_M, 2 * HEAD_DIM_M), F32),
                        pltpu.VMEM((batch, 8, 1), F32)],
        compiler_params=pltpu.CompilerParams(dimension_semantics=("arbitrary",),
                                             vmem_limit_bytes=VMEM_LIMIT),
        name="mlstm",
    )(qk.reshape(batch, seq, 2 * W_M), v.reshape(batch, seq, W_M), o_pre.reshape(batch, seq, W_M),
      ift_b, gbt, head_norm.reshape(1, -1))
    return out.reshape(t, W_M)


def kernel(x, c, ada_w, ada_b, ffn1_norm, ffn1_w1, ffn1_w3, ffn1_w2, mix_norm, w_in, conv_w, conv_b,
           kv_norm, w_uk, w_uv, mlstm_gate_bias, mlstm_head_norm, rel_bias, w_branch_attn,
           w_branch_mlstm, w_out, ffn2_norm, ffn2_w1, ffn2_w3, ffn2_w2, final_norm):
    batch, seq, d = x.shape
    depth = ada_w.shape[0]
    assert seq % max(FFN_TM, MIX_TM, MLSTM_CHUNK, KEY_BLOCK) == 0
    t = batch * seq
    xf = x.reshape(t, d)
    btile, bmax = _bias_tiles(rel_bias)
    for l in range(depth):
        mod = _adaln(c, ada_w[l], ada_b[l]).reshape(batch, 9, 1, d)
        sh1, sc1, g1, sh2, sc2, g2, sh3, sc3, g3 = [mod[:, n] for n in range(9)]
        xf = _ffn(xf, ffn1_norm[l], sh1, sc1, g1, ffn1_w1[l], ffn1_w3[l], ffn1_w2[l], final_norm,
                  seq=seq, final_norm=False)
        wuk_hdc = w_uk[l].transpose(0, 2, 1).reshape(N_HEADS_A // 2, 2, HEAD_DIM_A, D_LATENT)
        zeros = jnp.zeros_like(wuk_hdc[:, 0])
        wuk_t = jnp.concatenate([jnp.concatenate([wuk_hdc[:, 0], zeros], axis=2),
                                 jnp.concatenate([zeros, wuk_hdc[:, 1]], axis=2)], axis=1).astype(BF16)
        (q_abs, q_idx, k_idx, ckv, ckv_t, w_t, ift, qk_m, v_m, o_pre, gate_a, gate_m) = _mixin(
            xf, mix_norm[l], sh2, sc2, _pack_w_in(w_in[l], d), kv_norm[l], wuk_t, conv_w[l], conv_b[l],
            seq=seq)
        wuv_t = w_uv[l].transpose(0, 2, 1).astype(BF16)
        y_a = _dsa(q_idx, q_abs, w_t, k_idx, ckv, ckv_t, btile, bmax, wuv_t, batch=batch, seq=seq)
        h_m = _mlstm(qk_m, v_m, o_pre, ift, mlstm_gate_bias[l], mlstm_head_norm[l], batch=batch, seq=seq)
        xf = _ffn(xf, ffn2_norm[l], sh3, sc3, g3, ffn2_w1[l], ffn2_w3[l], ffn2_w2[l], final_norm,
                  seq=seq, final_norm=(l == depth - 1),
                  merge=(y_a, h_m, gate_a, gate_m, g2, w_branch_attn[l], w_branch_mlstm[l], w_out[l]))
    return xf.reshape(batch, seq, d)
```

```python
import functools
import math

import jax
import jax.numpy as jnp
from jax import lax
from jax.experimental import pallas as pl
from jax.experimental.pallas import tpu as pltpu

F32 = jnp.float32
BF16 = jnp.bfloat16
I32 = jnp.int32

LANES = 128
VMEM_LIMIT = 56 * 1024 * 1024

N_HEADS_A = 8
HEAD_DIM_A = 64
D_LATENT = 256
N_HEADS_IDX = 8
HEAD_DIM_IDX = 64
TOPK_MAX = 256
Q_BLOCK = 128
N_BUCKETS = 32
MAX_DISTANCE = 128
N_HEADS_M = 4
HEAD_DIM_M = 128
CONV_WIDTH = 4
EPS = 1e-6
IDX_SCALE = (N_HEADS_IDX ** -0.5) * (HEAD_DIM_IDX ** -0.5)
W_A = N_HEADS_A * HEAD_DIM_A
W_M = N_HEADS_M * HEAD_DIM_M

FFN_TM = 512
FFN_CHUNK = 256
MIX_TM = 512
KEY_BLOCK = 256
BLOCK_UNROLL = 6
PLANE_KEYS = 256
MLSTM_CHUNK = 256
NEG_BIG = -1e30
INT_MIN = -2 ** 31

BIAS_PAD = 2 * KEY_BLOCK - Q_BLOCK
BIAS_ROWS = KEY_BLOCK + BIAS_PAD
CKVT_ROWS = D_LATENT + 16
LOG2E = math.log2(math.e)


def _sigmoid(x):
    return 1.0 / (1.0 + jnp.exp(-x))


def _log_sigmoid(x):
    return jnp.minimum(x, 0.0) - jnp.log(1.0 + jnp.exp(-jnp.abs(x)))


def _rms_norm(x, gain):
    ms = jnp.mean(x * x, axis=-1, keepdims=True)
    return x * lax.rsqrt(ms + EPS) * gain


def _split3(x):
    hi = x.astype(BF16)
    r1 = x - hi.astype(F32)
    mid = r1.astype(BF16)
    lo = (r1 - mid.astype(F32)).astype(BF16)
    return hi, mid, lo


def _resident(shape):
    nd = len(shape)
    return pl.BlockSpec(shape, lambda *_: (0,) * nd, pipeline_mode=pl.Buffered(1))


def _adaln_kernel(c_ref, w_ref, b_ref, o_ref):
    c = c_ref[...]
    cond = c * _sigmoid(c)
    o_ref[...] = jnp.dot(cond.astype(BF16), w_ref[...].astype(BF16),
                         preferred_element_type=F32) + b_ref[...]


def _adaln(c, ada_w, ada_b):
    b, d = c.shape
    n = ada_w.shape[1]
    rows = 8
    c_pad = jnp.zeros((rows, d), F32).at[:b].set(c)
    tn = 1024
    out = pl.pallas_call(
        _adaln_kernel,
        out_shape=jax.ShapeDtypeStruct((rows, n), F32),
        grid=(n // tn,),
        in_specs=[pl.BlockSpec((rows, d), lambda j: (0, 0)),
                  pl.BlockSpec((d, tn), lambda j: (0, j)),
                  pl.BlockSpec((1, tn), lambda j: (0, j))],
        out_specs=pl.BlockSpec((rows, tn), lambda j: (0, j)),
        compiler_params=pltpu.CompilerParams(dimension_semantics=("arbitrary",),
                                             vmem_limit_bytes=VMEM_LIMIT),
        name="adaln",
    )(c_pad, ada_w, ada_b.reshape(1, n))
    return out[:b]


def _t5_bucket(dist):
    n = jnp.maximum(dist, 0)
    max_exact = N_BUCKETS // 2
    nf = jnp.maximum(n, 1).astype(F32)
    large = max_exact + (jnp.log(nf / max_exact) / math.log(MAX_DISTANCE / max_exact)
                         * (N_BUCKETS - max_exact)).astype(I32)
    large = jnp.minimum(large, N_BUCKETS - 1)
    return jnp.where(n < max_exact, n, large)


def _bias_kernel(rel_ref, tile_ref, max_ref):
    r = lax.broadcasted_iota(I32, (BIAS_ROWS, LANES), 0)
    i = lax.broadcasted_iota(I32, (BIAS_ROWS, LANES), 1)
    bucket = _t5_bucket(i - r + BIAS_PAD)
    for h in range(N_HEADS_A):
        acc = jnp.zeros((BIAS_ROWS, LANES), F32)
        top = rel_ref[0, h] * LOG2E
        for bkt in range(N_BUCKETS):
            val = rel_ref[bkt, h] * LOG2E
            acc = jnp.where(bucket == bkt, val, acc)
            top = jnp.maximum(top, val)
        tile_ref[h] = acc
        max_ref[:, h * LANES:(h + 1) * LANES] = jnp.full((1, LANES), top, F32)


def _bias_tiles(rel_bias):
    return pl.pallas_call(
        _bias_kernel,
        out_shape=(jax.ShapeDtypeStruct((N_HEADS_A, BIAS_ROWS, LANES), F32),
                   jax.ShapeDtypeStruct((1, N_HEADS_A * LANES), F32)),
        in_specs=[pl.BlockSpec(memory_space=pltpu.SMEM)],
        out_specs=(pl.BlockSpec(memory_space=pltpu.VMEM), pl.BlockSpec(memory_space=pltpu.VMEM)),
        name="bias_tiles",
    )(rel_bias)


def _ffn_kernel(*refs, n_chunks, final_norm, merge):
    if merge:
        (x_ref, ya_ref, hm_ref, ga_ref, gm_ref, gmix_ref, wa_ref, wm_ref, wo_ref), refs = refs[:9], refs[9:]
    else:
        x_ref, refs = refs[0], refs[1:]
    gain_ref, sh_ref, sc_ref, g_ref, w1_ref, w3_ref, w2_ref, fin_ref, o_ref, h_scr, acc_scr = refs
    x = x_ref[...]
    if merge:
        pa = jnp.dot(ya_ref[...], wa_ref[...], preferred_element_type=F32)
        pm = jnp.dot(hm_ref[...], wm_ref[...], preferred_element_type=F32)
        merged = _sigmoid(ga_ref[...]) * pa + _sigmoid(gm_ref[...]) * pm
        x = x + gmix_ref[0] * jnp.dot(merged.astype(BF16), wo_ref[...], preferred_element_type=F32)
    h = _rms_norm(x, gain_ref[...]) * (1.0 + sc_ref[0]) + sh_ref[0]
    h_scr[...] = h.astype(BF16)
    for j in range(n_chunks):
        hb = h_scr[...]
        cols = slice(j * FFN_CHUNK, (j + 1) * FFN_CHUNK)
        u1 = jnp.dot(hb, w1_ref[:, cols], preferred_element_type=F32)
        u3 = jnp.dot(hb, w3_ref[:, cols], preferred_element_type=F32)
        a = (u1 * _sigmoid(u1)) * u3
        part = jnp.dot(a.astype(BF16), w2_ref[j], preferred_element_type=F32)
        if j == 0:
            acc_scr[...] = part
        else:
            acc_scr[...] += part
    out = x + (0.5 * g_ref[0]) * acc_scr[...]
    if final_norm:
        out = _rms_norm(out, fin_ref[...])
    o_ref[...] = out


def _ffn(x, gain, sh, sc, g, w1, w3, w2, fin, *, seq, final_norm, merge=None):
    t, d = x.shape
    dff = w1.shape[1]
    nch = dff // FFN_CHUNK
    w1c = w1.astype(BF16)
    w3c = w3.astype(BF16)
    w2c = w2.astype(BF16).reshape(nch, FFN_CHUNK, d)
    tm = FFN_TM
    per_b = seq // tm
    row = lambda w: pl.BlockSpec((tm, w), lambda i: (i, 0))
    mod_spec = pl.BlockSpec((1, 1, d), lambda i: (i // per_b, 0, 0))
    merge_specs, merge_args = [], []
    if merge is not None:
        y_a, h_m, gate_a, gate_m, g_mix, w_a, w_m, w_o = merge
        merge_specs = [row(W_A), row(W_M), row(d), row(d), mod_spec,
                       _resident((W_A, d)), _resident((W_M, d)), _resident((d, d))]
        merge_args = [y_a, h_m, gate_a, gate_m, g_mix, w_a.astype(BF16), w_m.astype(BF16), w_o.astype(BF16)]
    return pl.pallas_call(
        functools.partial(_ffn_kernel, n_chunks=nch, final_norm=final_norm, merge=merge is not None),
        out_shape=jax.ShapeDtypeStruct((t, d), F32),
        grid=(t // tm,),
        in_specs=[row(d)] + merge_specs + [
                  _resident((1, d)), mod_spec, mod_spec, mod_spec,
                  _resident((d, dff)), _resident((d, dff)),
                  _resident((nch, FFN_CHUNK, d)), _resident((1, d))],
        out_specs=row(d),
        scratch_shapes=[pltpu.VMEM((tm, d), BF16), pltpu.VMEM((tm, d), F32)],
        compiler_params=pltpu.CompilerParams(dimension_semantics=("arbitrary",),
                                             vmem_limit_bytes=VMEM_LIMIT),
        name="ffn_final" if final_norm else "ffn",
    )(x, *merge_args, gain.reshape(1, d), sh, sc, g, w1c, w3c, w2c, fin.reshape(1, d))


_C_QA = 0
_C_CKV = _C_QA + W_A
_C_QI = _C_CKV + D_LATENT
_C_KI = _C_QI + N_HEADS_IDX * LANES
_C_SM = _C_KI + LANES
_C_QK = _C_SM + LANES
_C_V = _C_QK + 2 * W_M
_C_O = _C_V + W_M
_C_GA = _C_O + W_M
_C_GM = _C_GA + 1024
_C_END = _C_GM + 1024
_SM_W = 0
_SM_I = N_HEADS_IDX


def _pack_w_in(w_in, d_model):
    splits = (W_A, D_LATENT, N_HEADS_IDX * HEAD_DIM_IDX, HEAD_DIM_IDX, N_HEADS_IDX,
              W_M, W_M, W_M, N_HEADS_M, N_HEADS_M, W_M, d_model, d_model)
    offs = [0]
    for s in splits:
        offs.append(offs[-1] + s)
    (q_a, c_kv, q_i, k_i, w_i, q_m, k_m, v_m, i_p, f_p, o_p, g_a, g_m) = [
        w_in[:, offs[n]:offs[n + 1]] for n in range(len(splits))]
    d = w_in.shape[0]

    def pad_heads(w, nh, hd):
        w = w.reshape(d, nh, hd)
        return jnp.pad(w, ((0, 0), (0, 0), (0, LANES - hd))).reshape(d, nh * LANES)

    small = jnp.concatenate([w_i, i_p, f_p], axis=1)
    small = jnp.pad(small, ((0, 0), (0, LANES - small.shape[1])))
    packed = jnp.concatenate([
        q_a, c_kv, pad_heads(q_i, N_HEADS_IDX, HEAD_DIM_IDX),
        jnp.pad(k_i, ((0, 0), (0, LANES - HEAD_DIM_IDX))), small, q_m, k_m, v_m, o_p, g_a, g_m], axis=1)
    assert packed.shape[1] == _C_END
    return packed.astype(BF16)


def _mixin_kernel(x_ref, gain_ref, sh_ref, sc_ref, w_ref, kvn_ref, wuk_ref, cw_ref, cb_ref,
                  qabs_ref, qidx_ref, kidx_ref, ckv_ref, ckvt_ref, wt_ref, ift_ref,
                  qk_ref, v_ref, o_ref, ga_ref, gm_ref, h_scr, xe_scr, *, tm, tiles_per_seq):
    nqb = tm // Q_BLOCK

    @pl.when(pl.program_id(0) % tiles_per_seq == 0)
    def _():
        xe_scr[:8] = jnp.zeros((8, xe_scr.shape[1]), F32)

    x = x_ref[...]
    h = _rms_norm(x, gain_ref[...]) * (1.0 + sc_ref[0]) + sh_ref[0]
    h_scr[...] = h.astype(BF16)

    def proj(lo, hi):
        return jnp.dot(h_scr[...], w_ref[:, lo:hi], preferred_element_type=F32)

    qa = proj(_C_QA, _C_CKV)
    scale = HEAD_DIM_A ** -0.5 * LOG2E
    for g in range(N_HEADS_A // 2):
        q_pair = qa[:, g * LANES:(g + 1) * LANES].astype(BF16)
        q_abs = jnp.dot(q_pair, wuk_ref[g], preferred_element_type=F32) * scale
        for j in range(2):
            qabs_ref[:, 2 * g + j] = (q_abs[:, j * D_LATENT:(j + 1) * D_LATENT]
                                      .astype(BF16).reshape(nqb, Q_BLOCK, D_LATENT))
    ckv = _rms_norm(proj(_C_CKV, _C_QI), kvn_ref[...])
    ckv_ref[...] = ckv.astype(BF16)
    ckv_t = ckv.T
    ones_row = jnp.where(lax.broadcasted_iota(I32, (CKVT_ROWS - D_LATENT, KEY_BLOCK), 0) == 0, 1.0, 0.0)
    for j in range(tm // KEY_BLOCK):
        ckvt_ref[j, :D_LATENT] = ckv_t[:, j * KEY_BLOCK:(j + 1) * KEY_BLOCK].astype(BF16)
        ckvt_ref[j, D_LATENT:] = ones_row.astype(BF16)
    qi = proj(_C_QI, _C_KI)
    for hh in range(N_HEADS_IDX):
        qidx_ref[:, hh] = qi[:, hh * LANES:(hh + 1) * LANES].astype(BF16).reshape(nqb, Q_BLOCK, LANES)
    kidx_ref[...] = proj(_C_KI, _C_SM).astype(BF16)
    small_t = proj(_C_SM, _C_QK).T
    wt_ref[...] = small_t[_SM_W:_SM_W + N_HEADS_IDX] * IDX_SCALE
    ift_ref[...] = small_t[_SM_I:_SM_I + 2 * N_HEADS_M]
    xe_scr[8:] = proj(_C_QK, _C_V)
    xe = xe_scr[...]
    xq = xe[8:]
    conv = xq * cw_ref[CONV_WIDTH - 1:CONV_WIDTH, :] + cb_ref[...]
    for d in range(1, CONV_WIDTH):
        conv = conv + pltpu.roll(xe, d, axis=0)[8:] * cw_ref[CONV_WIDTH - 1 - d:CONV_WIDTH - d, :]
    xe_scr[:8] = xe_scr[tm:]
    qk = conv * _sigmoid(conv)
    qk_ref[:, :W_M] = qk[:, :W_M].astype(BF16)
    qk_ref[:, W_M:] = (qk[:, W_M:] * (HEAD_DIM_M ** -0.5)).astype(BF16)
    v_ref[...] = proj(_C_V, _C_O).astype(BF16)
    o_ref[...] = proj(_C_O, _C_GA)
    ga_ref[...] = proj(_C_GA, _C_GM)
    gm_ref[...] = proj(_C_GM, _C_END)


def _mixin(x, gain, sh, sc, w_packed, kv_norm, wuk_t, conv_w, conv_b, *, seq):
    t, d = x.shape
    tm = MIX_TM
    per_b = seq // tm
    nqb = tm // Q_BLOCK
    row = lambda w: pl.BlockSpec((tm, w), lambda i: (i, 0))
    mod_spec = pl.BlockSpec((1, 1, d), lambda i: (i // per_b, 0, 0))
    out_shape = (
        jax.ShapeDtypeStruct((t // Q_BLOCK, N_HEADS_A, Q_BLOCK, D_LATENT), BF16),
        jax.ShapeDtypeStruct((t // Q_BLOCK, N_HEADS_IDX, Q_BLOCK, LANES), BF16),
        jax.ShapeDtypeStruct((t, LANES), BF16),
        jax.ShapeDtypeStruct((t, D_LATENT), BF16),
        jax.ShapeDtypeStruct((t // KEY_BLOCK, CKVT_ROWS, KEY_BLOCK), BF16),
        jax.ShapeDtypeStruct((N_HEADS_IDX, t), F32),
        jax.ShapeDtypeStruct((2 * N_HEADS_M, t), F32),
        jax.ShapeDtypeStruct((t, 2 * W_M), BF16),
        jax.ShapeDtypeStruct((t, W_M), BF16),
        jax.ShapeDtypeStruct((t, W_M), F32),
        jax.ShapeDtypeStruct((t, d), F32),
        jax.ShapeDtypeStruct((t, d), F32),
    )
    out_specs = (
        pl.BlockSpec((nqb, N_HEADS_A, Q_BLOCK, D_LATENT), lambda i: (i, 0, 0, 0)),
        pl.BlockSpec((nqb, N_HEADS_IDX, Q_BLOCK, LANES), lambda i: (i, 0, 0, 0)),
        row(LANES), row(D_LATENT),
        pl.BlockSpec((tm // KEY_BLOCK, CKVT_ROWS, KEY_BLOCK), lambda i: (i, 0, 0)),
        pl.BlockSpec((N_HEADS_IDX, tm), lambda i: (0, i)),
        pl.BlockSpec((2 * N_HEADS_M, tm), lambda i: (0, i)),
        row(2 * W_M), row(W_M), row(W_M), row(d), row(d),
    )
    return pl.pallas_call(
        functools.partial(_mixin_kernel, tm=tm, tiles_per_seq=per_b),
        out_shape=out_shape,
        grid=(t // tm,),
        in_specs=[pl.BlockSpec((tm, d), lambda i: (i, 0)), _resident((1, d)), mod_spec, mod_spec,
                  _resident((d, _C_END)), _resident((1, D_LATENT)),
                  _resident((N_HEADS_A // 2, LANES, 2 * D_LATENT)),
                  _resident((CONV_WIDTH, 2 * W_M)), _resident((1, 2 * W_M))],
        out_specs=out_specs,
        scratch_shapes=[pltpu.VMEM((tm, d), BF16), pltpu.VMEM((tm + 8, 2 * W_M), F32)],
        compiler_params=pltpu.CompilerParams(dimension_semantics=("arbitrary",),
                                             vmem_limit_bytes=VMEM_LIMIT),
        name="mixin",
    )(x, gain.reshape(1, d), sh, sc, w_packed, kv_norm.reshape(1, D_LATENT), wuk_t,
      conv_w, conv_b.reshape(1, -1))


def _sortable_key(score):
    bits = pltpu.bitcast(score, I32)
    bits = jnp.where(bits == INT_MIN, 0, bits)
    return jnp.where(bits < 0, bits ^ 0x7FFFFFFF, bits)


def _bit_transpose32(words):
    v = list(words)
    j, m = 16, 0x0000FFFF
    while j:
        k = 0
        while k < 32:
            t = (v[k] ^ lax.shift_right_logical(v[k + j], jnp.int32(j))) & m
            v[k] = v[k] ^ t
            v[k + j] = v[k + j] ^ (t << j)
            k = (k + j + 1) & ~j
        j >>= 1
        m = (m ^ (m << j)) & 0x7FFFFFFF
    return v


def _dsa_kernel(qidx_ref, qabs_ref, wt_ref, kidx_ref, ckv_ref, ckvt_ref, btile_ref, bmax_ref, wuvt_ref,
                out_ref, keys_scr, planes_scr, cand_scr, tau_scr, acc_scr, m_scr, ltc_scr, kmax_scr,
                qta_scr, qtc_scr, *, topk, n_qb):
    kb_sz = KEY_BLOCK
    step = pl.program_id(1)
    has_c = step >= 1
    qa = jnp.minimum(step, n_qb - 1)
    qc = jnp.maximum(step - 1, 0)
    slot_a = step & 1
    slot_c = 1 - slot_a
    n_a = qa // (kb_sz // Q_BLOCK) + 1
    n_c = qc // (kb_sz // Q_BLOCK) + 1
    qa0 = qa * Q_BLOCK
    qc0 = qc * Q_BLOCK
    row_id = lax.broadcasted_iota(I32, (kb_sz, LANES), 0)
    lane_id = lax.broadcasted_iota(I32, (kb_sz, LANES), 1)
    n_blocks = keys_scr.shape[1]
    n_groups = N_HEADS_A // 2
    pair = 2 * LANES
    ones8 = jnp.ones((8, D_LATENT), BF16)

    @pl.when(step == 0)
    def _():
        keys_scr[1, 0] = jnp.full((kb_sz, LANES), INT_MIN, I32)
        tau_scr[1] = jnp.zeros((1, LANES), I32)
        planes_scr[...] = jnp.zeros(planes_scr.shape, I32)

        def kn_body(kb, mx):
            c = ckv_ref[kb].astype(F32)
            n2 = lax.dot_general(ones8, (c * c).astype(BF16), (((1,), (1,)), ((), ())),
                                 preferred_element_type=F32)
            return jnp.maximum(mx, n2[0:1])
        mx = lax.fori_loop(0, n_blocks, kn_body, jnp.zeros((1, kb_sz), F32))
        kmax_scr[...] = jnp.max(mx, axis=1, keepdims=True)

    for g in range(n_groups):
        qi_g = qidx_ref[0, 2 * g:2 * g + 2].reshape(2 * Q_BLOCK, LANES).astype(F32)
        qta_scr[g] = qi_g.T.astype(BF16)
        qa_g = qabs_ref[0, 2 * g:2 * g + 2].reshape(2 * Q_BLOCK, D_LATENT).astype(F32)
        qtc_scr[g] = qa_g.T.astype(BF16)

    def idx_dot(kb, g):
        return jnp.dot(kidx_ref[kb], qta_scr[g], preferred_element_type=F32)

    def logits(kb, g):
        return jnp.dot(ckv_ref[kb], qtc_scr[g], preferred_element_type=F32)

    def bias_start(kb):
        delta = jnp.minimum(qc0 - kb * kb_sz, BIAS_PAD)
        return pl.multiple_of(BIAS_PAD - delta, LANES)

    qn2 = []
    for g in range(n_groups):
        q_g = qabs_ref[0, 2 * g:2 * g + 2].reshape(2 * Q_BLOCK, D_LATENT).astype(F32)
        qn2.append(lax.dot_general(ones8, (q_g * q_g).astype(BF16), (((1,), (1,)), ((), ())),
                                   preferred_element_type=F32)[0:1])
    bound = jnp.sqrt(jnp.concatenate(qn2, axis=1) * kmax_scr[...]) * 1.02 + bmax_ref[...] + 1e-3
    tau_c = tau_scr[slot_c]
    w_t = wt_ref[...]

    acc_scr[...] = jnp.zeros(acc_scr.shape, F32)
    for g in range(n_groups):
        ltc_scr[:, g * pair:(g + 1) * pair] = logits(0, g)

    bound_far = bound - jnp.concatenate([btile_ref[hh, 0:1, :] for hh in range(N_HEADS_A)], axis=1)

    def block_step(kb, far, attend=True):
        kc = kb
        kc_next = jnp.minimum(kb + 1, n_c - 1)
        thr = jnp.where(has_c, tau_c - 1, jnp.int32(2 ** 31 - 1))
        if attend:
            sel = keys_scr[slot_c, kc] > thr
            ct_blk = ckvt_ref[kc]
            start = bias_start(kc)
        ref_pt = bound_far if far else bound
        score = jnp.zeros((kb_sz, LANES), F32)
        for g in range(n_groups):
            s_t = idx_dot(kb, g)
            for j in range(2):
                hh = 2 * g + j
                score = score + jnp.maximum(s_t[:, j * LANES:(j + 1) * LANES], 0.0) * w_t[hh:hh + 1, :]
        for g in range(n_groups if attend else 0):
            lt = ltc_scr[:, g * pair:(g + 1) * pair]
            ps = []
            for j in range(2):
                hh = 2 * g + j
                piece = lt[:, j * LANES:(j + 1) * LANES]
                if not far:
                    piece = piece + btile_ref[hh, pl.ds(start, kb_sz), :]
                ps.append(jnp.exp2(jnp.where(sel, piece, NEG_BIG) - ref_pt[:, hh * LANES:(hh + 1) * LANES]))
            ltc_scr[:, g * pair:(g + 1) * pair] = logits(kc_next, g)
            acc_scr[g] += jnp.dot(ct_blk, jnp.concatenate(ps, axis=1).astype(BF16),
                                  preferred_element_type=F32)
        valid = (kb * kb_sz + row_id) <= (qa0 + lane_id)
        keys = jnp.where(valid, _sortable_key(score), INT_MIN)
        keys_scr[slot_a, kb] = keys
        v = keys ^ INT_MIN
        for sub in range(kb_sz // PLANE_KEYS):
            r0 = sub * PLANE_KEYS
            words = _bit_transpose32([v[r0 + 8 * i:r0 + 8 * (i + 1), :] for i in range(32)])
            for bit in range(32):
                planes_scr[bit, kb * (kb_sz // PLANE_KEYS) + sub] = words[31 - bit]

    def block_body(far, it, carry):
        for u in range(BLOCK_UNROLL):
            block_step(it * BLOCK_UNROLL + u, far)
        return carry

    n_far_iters = jnp.maximum(n_c - 2, 0) // BLOCK_UNROLL
    lax.fori_loop(0, n_far_iters, functools.partial(block_body, True), 0)

    def near_body(attend, kb, carry):
        block_step(kb, False, attend)
        return carry

    lax.fori_loop(n_far_iters * BLOCK_UNROLL, n_c, functools.partial(near_body, True), 0)
    lax.fori_loop(n_c, n_a, functools.partial(near_body, False), 0)
    n_kb = n_a


    n_planes = planes_scr.shape[1]
    live = n_kb * (kb_sz // PLANE_KEYS)

    def radix_select(width):
        blk_id = lax.broadcasted_iota(I32, (width, 8, LANES), 0)
        cand_scr[:width] = jnp.where(blk_id < live, -1, 0)

        def bit_body(it, carry):
            above, tau_u = carry
            bit = 31 - it
            ones = cand_scr[:width] & planes_scr[bit, :width]
            c1 = jnp.sum(jnp.sum(lax.population_count(ones), axis=0), axis=0, keepdims=True)
            take = (above + c1) >= topk
            cand_scr[:width] = jnp.where(take, ones, cand_scr[:width] ^ ones)
            above = jnp.where(take, above, above + c1)
            tau_u = jnp.where(take, tau_u | (jnp.int32(1) << bit), tau_u)
            return above, tau_u

        zero = jnp.zeros((1, LANES), I32)
        above, tau_u = lax.fori_loop(0, 32, bit_body, (zero, zero))
        n_eq = jnp.sum(jnp.sum(lax.population_count(cand_scr[:width]), axis=0), axis=0, keepdims=True)
        return above, tau_u, n_eq

    widths = [n_planes * (i + 1) // 4 for i in range(4)]
    select = functools.partial(radix_select, widths[-1])
    for width in reversed(widths[:-1]):
        select = functools.partial(lax.cond, live <= width, functools.partial(radix_select, width), select)
    n_gt, tau_u, n_eq = select()
    tau = tau_u ^ INT_MIN

    need = topk - n_gt
    overflow = n_eq > need
    seq_bits = max(1, (n_blocks * kb_sz - 1).bit_length())

    @pl.when(jnp.max(jnp.where(overflow, 1, 0)) > 0)
    def _():
        def count_ties_before(trial):
            def body(kb, acc):
                hit = jnp.where((keys_scr[slot_a, kb] == tau) & ((kb * kb_sz + row_id) < trial), 1, 0)
                return acc + jnp.sum(hit.reshape(kb_sz // 8, 8, LANES), axis=0)
            acc = lax.fori_loop(0, n_kb, body, jnp.zeros((8, LANES), I32))
            return jnp.sum(acc, axis=0, keepdims=True)

        def idx_body(it, jc):
            trial = jc | (jnp.int32(1) << (seq_bits - 1 - it))
            return jnp.where(count_ties_before(trial) < need, trial, jc)

        j_cut = lax.fori_loop(0, seq_bits, idx_body, jnp.zeros((1, LANES), I32))

        def demote_body(kb, carry):
            k = keys_scr[slot_a, kb]
            drop = overflow & (k == tau) & ((kb * kb_sz + row_id) > j_cut)
            keys_scr[slot_a, kb] = jnp.where(drop, INT_MIN, k)
            return carry

        lax.fori_loop(0, n_kb, demote_body, 0)

    tau_scr[slot_a] = jnp.maximum(tau, INT_MIN + 1)

    l_min = jnp.min(jnp.concatenate([acc_scr[g, D_LATENT:D_LATENT + 1, :] for g in range(n_groups)], axis=1))

    @pl.when(has_c & jnp.logical_not(l_min >= 2.0 ** -80))
    def _():
        m_scr[...] = jnp.full(m_scr.shape, NEG_BIG, F32)
        acc_scr[...] = jnp.zeros(acc_scr.shape, F32)

        def exact_body(kb, carry):
            sel = keys_scr[slot_c, kb] >= tau_c
            ct_blk = ckvt_ref[kb]
            start = bias_start(kb)
            for g in range(n_groups):
                lt = logits(kb, g)
                ps, alphas = [], []
                for j in range(2):
                    hh = 2 * g + j
                    sl = slice(hh * LANES, (hh + 1) * LANES)
                    piece = lt[:, j * LANES:(j + 1) * LANES] + btile_ref[hh, pl.ds(start, kb_sz), :]
                    masked = jnp.where(sel, piece, NEG_BIG)
                    m_old = m_scr[:, sl]
                    m_new = jnp.maximum(m_old, jnp.max(masked, axis=0, keepdims=True))
                    m_scr[:, sl] = m_new
                    alphas.append(jnp.exp2(m_old - m_new))
                    ps.append(jnp.exp2(masked - m_new))
                pv = jnp.dot(ct_blk, jnp.concatenate(ps, axis=1).astype(BF16), preferred_element_type=F32)
                acc_scr[g] = jnp.concatenate(alphas, axis=1) * acc_scr[g] + pv
            return carry

        lax.fori_loop(0, n_c, exact_body, 0)

    @pl.when(has_c)
    def _():
        ys = []
        for hh in range(N_HEADS_A):
            acc_h = acc_scr[hh // 2, :, (hh % 2) * LANES:(hh % 2 + 1) * LANES]
            o_h = acc_h[:D_LATENT] * (1.0 / acc_h[D_LATENT:D_LATENT + 1])
            ys.append(jnp.dot(wuvt_ref[hh], o_h.astype(BF16), preferred_element_type=F32))
        y_t = jnp.concatenate(ys, axis=0)
        out_ref[...] = y_t.T.astype(BF16)


def _dsa(q_idx, q_abs, w_t, k_idx, ckv, ckv_t, btile, bmax, wuv_t, *, batch, seq):
    t = batch * seq
    nqb = seq // Q_BLOCK
    nkb = seq // KEY_BLOCK
    topk = min(TOPK_MAX, seq // 4)
    k_idx3 = k_idx.reshape(t // KEY_BLOCK, KEY_BLOCK, LANES)
    ckv3 = ckv.reshape(t // KEY_BLOCK, KEY_BLOCK, D_LATENT)
    per_batch = lambda shape: pl.BlockSpec(shape, lambda b, q: (b,) + (0,) * (len(shape) - 1),
                                           pipeline_mode=pl.Buffered(1))
    scored = lambda b, s: b * nqb + jnp.minimum(s, nqb - 1)
    attended = lambda b, s: b * nqb + jnp.maximum(s - 1, 0)
    return pl.pallas_call(
        functools.partial(_dsa_kernel, topk=topk, n_qb=nqb),
        out_shape=jax.ShapeDtypeStruct((t, W_A), BF16),
        grid=(batch, nqb + 1),
        in_specs=[pl.BlockSpec((1, N_HEADS_IDX, Q_BLOCK, LANES), lambda b, s: (scored(b, s), 0, 0, 0)),
                  pl.BlockSpec((1, N_HEADS_A, Q_BLOCK, D_LATENT), lambda b, s: (attended(b, s), 0, 0, 0)),
                  pl.BlockSpec((N_HEADS_IDX, Q_BLOCK), lambda b, s: (0, scored(b, s))),
                  per_batch((nkb, KEY_BLOCK, LANES)),
                  per_batch((nkb, KEY_BLOCK, D_LATENT)),
                  per_batch((nkb, CKVT_ROWS, KEY_BLOCK)),
                  _resident((N_HEADS_A, BIAS_ROWS, LANES)),
                  _resident((1, N_HEADS_A * LANES)),
                  _resident((N_HEADS_A, HEAD_DIM_A, D_LATENT))],
        out_specs=pl.BlockSpec((Q_BLOCK, W_A), lambda b, s: (attended(b, s), 0)),
        scratch_shapes=[pltpu.VMEM((2, nkb, KEY_BLOCK, LANES), I32),
                        pltpu.VMEM((32, seq // PLANE_KEYS + 1, 8, LANES), I32),
                        pltpu.VMEM((seq // PLANE_KEYS + 1, 8, LANES), I32),
                        pltpu.VMEM((2, 1, LANES), I32),
                        pltpu.VMEM((N_HEADS_A // 2, CKVT_ROWS, 2 * LANES), F32),
                        pltpu.VMEM((1, N_HEADS_A * LANES), F32),
                        pltpu.VMEM((KEY_BLOCK, (N_HEADS_A + 1) * LANES), F32),
                        pltpu.VMEM((1, 1), F32),
                        pltpu.VMEM((N_HEADS_IDX // 2, LANES, 2 * LANES), BF16),
                        pltpu.VMEM((N_HEADS_A // 2, D_LATENT, 2 * LANES), BF16)],
        compiler_params=pltpu.CompilerParams(dimension_semantics=("arbitrary", "arbitrary"),
                                             vmem_limit_bytes=VMEM_LIMIT),
        name="dsa",
    )(q_idx, q_abs, w_t, k_idx3, ckv3, ckv_t, btile, bmax, wuv_t)


def _mlstm_kernel(qk_ref, v_ref, o_ref, ift_ref, gbt_ref, hn_ref,
                  out_ref, cx_scr, m_scr, *, chunk, n_batch):
    L = chunk

    @pl.when(pl.program_id(0) == 0)
    def _():
        cx_scr[...] = jnp.zeros(cx_scr.shape, F32)
        m_scr[...] = jnp.zeros(m_scr.shape, F32)

    rr = lax.broadcasted_iota(I32, (L, L), 0)
    cc = lax.broadcasted_iota(I32, (L, L), 1)
    causal = cc <= rr
    triu = jnp.where(rr <= cc, 1.0, 0.0).astype(BF16)
    lane = lax.broadcasted_iota(I32, (8, L), 1)
    ones_col = jnp.where(lax.broadcasted_iota(I32, (L, HEAD_DIM_M), 1) == 0, 1.0, 0.0).astype(BF16)
    for bi in range(n_batch):
        _mlstm_chunk(qk_ref.at[bi], v_ref.at[bi], o_ref.at[bi], ift_ref.at[bi], gbt_ref, hn_ref,
                     out_ref.at[bi], cx_scr.at[bi], m_scr.at[bi], causal, triu, lane, ones_col, L)


def _mlstm_chunk(qk_ref, v_ref, o_ref, ift_ref, gbt_ref, hn_ref, out_ref, cx_scr, m_scr,
                 causal, triu, lane, ones_col, L):
    g_t = ift_ref[...] + gbt_ref[...]
    b_all = sum(jnp.dot(piece, triu, preferred_element_type=F32) for piece in _split3(_log_sigmoid(g_t)))
    b8 = pltpu.roll(b_all, N_HEADS_M, axis=0)
    a8 = g_t - b8
    cm = a8
    shift = 1
    while shift < L:
        cm = jnp.maximum(cm, jnp.where(lane >= shift, pltpu.roll(cm, shift, axis=1), NEG_BIG))
        shift *= 2
    m_prev = m_scr[...]
    mx = jnp.maximum(m_prev, cm)
    mx_last = mx[:, L - 1:L]
    decay8 = jnp.exp(m_prev - mx_last)
    m_scr[...] = b8[:, L - 1:L] + mx_last
    rows = jnp.concatenate([-mx,
                            jnp.exp(m_prev - mx),
                            jnp.exp(-(b8 + mx)),
                            jnp.exp(a8 - mx_last),
                            jnp.zeros((LANES - 32, L), F32)], axis=0)
    cols = rows.T

    o_gate = _sigmoid(o_ref[...])
    for hh in range(N_HEADS_M):
        hs = slice(hh * HEAD_DIM_M, (hh + 1) * HEAD_DIM_M)
        qb16 = qk_ref[:, hs]
        kb16 = qk_ref[:, W_M + hh * HEAD_DIM_M:W_M + (hh + 1) * HEAD_DIM_M]
        v_ext = jnp.concatenate([v_ref[:, hs], ones_col], axis=1)
        u_c = cols[:, hh:hh + 1]
        w_inter = cols[:, 8 + hh:9 + hh]
        em_c = cols[:, 16 + hh:17 + hh]
        wgt_c = cols[:, 24 + hh:25 + hh]
        cx_prev = cx_scr[hh]

        d_mat = jnp.where(causal, jnp.exp(u_c + a8[hh:hh + 1, :]), 0.0)
        s = lax.dot_general(qb16, kb16, (((1,), (1,)), ((), ())), preferred_element_type=F32) * d_mat
        intra = jnp.dot(s.astype(BF16), v_ext, preferred_element_type=F32)
        inter = jnp.dot(qb16, cx_prev.astype(BF16), preferred_element_type=F32)
        both = w_inter * inter + intra
        num = both[:, :HEAD_DIM_M]
        den = both[:, HEAD_DIM_M:HEAD_DIM_M + 1]
        hval = num / jnp.maximum(jnp.abs(den), em_c)

        kw = kb16.astype(F32) * wgt_c
        cx_scr[hh] = decay8[hh:hh + 1] * cx_prev + jnp.dot(kw.T.astype(BF16), v_ext,
                                                           preferred_element_type=F32)

        mu = jnp.mean(hval, axis=1, keepdims=True)
        cen = hval - mu
        var = jnp.mean(cen * cen, axis=1, keepdims=True)
        hn = cen * lax.rsqrt(var + EPS) * hn_ref[:, hs]
        out_ref[:, hs] = (hn * o_gate[:, hs]).astype(BF16)


def _mlstm(qk, v, o_pre, ift, gate_bias, head_norm, *, batch, seq):
    t = batch * seq
    L = MLSTM_CHUNK
    nc = seq // L
    gbt = jnp.broadcast_to(gate_bias.reshape(2 * N_HEADS_M, 1), (2 * N_HEADS_M, L))
    ift_b = ift.reshape(2 * N_HEADS_M, batch, seq).transpose(1, 0, 2)
    row = lambda w: pl.BlockSpec((batch, L, w), lambda c: (0, c, 0))
    out = pl.pallas_call(
        functools.partial(_mlstm_kernel, chunk=L, n_batch=batch),
        out_shape=jax.ShapeDtypeStruct((batch, seq, W_M), BF16),
        grid=(nc,),
        in_specs=[row(2 * W_M), row(W_M), row(W_M),
                  pl.BlockSpec((batch, 2 * N_HEADS_M, L), lambda c: (0, 0, c)),
                  _resident((2 * N_HEADS_M, L)), _resident((1, W_M))],
        out_specs=row(W_M),
        scratch_shapes=[pltpu.VMEM((batch, N_HEADS_M, HEAD_DIM_M, 2 * HEAD_DIM_M), F32),
                        pltpu.VMEM((batch, 8, 1), F32)],
        compiler_params=pltpu.CompilerParams(dimension_semantics=("arbitrary",),
                                             vmem_limit_bytes=VMEM_LIMIT),
        name="mlstm",
    )(qk.reshape(batch, seq, 2 * W_M), v.reshape(batch, seq, W_M), o_pre.reshape(batch, seq, W_M),
      ift_b, gbt, head_norm.reshape(1, -1))
    return out.reshape(t, W_M)


def kernel(x, c, ada_w, ada_b, ffn1_norm, ffn1_w1, ffn1_w3, ffn1_w2, mix_norm, w_in, conv_w, conv_b,
           kv_norm, w_uk, w_uv, mlstm_gate_bias, mlstm_head_norm, rel_bias, w_branch_attn,
           w_branch_mlstm, w_out, ffn2_norm, ffn2_w1, ffn2_w3, ffn2_w2, final_norm):
    batch, seq, d = x.shape
    depth = ada_w.shape[0]
    assert seq % max(FFN_TM, MIX_TM, MLSTM_CHUNK, KEY_BLOCK) == 0
    t = batch * seq
    xf = x.reshape(t, d)
    btile, bmax = _bias_tiles(rel_bias)
    for l in range(depth):
        mod = _adaln(c, ada_w[l], ada_b[l]).reshape(batch, 9, 1, d)
        sh1, sc1, g1, sh2, sc2, g2, sh3, sc3, g3 = [mod[:, n] for n in range(9)]
        xf = _ffn(xf, ffn1_norm[l], sh1, sc1, g1, ffn1_w1[l], ffn1_w3[l], ffn1_w2[l], final_norm,
                  seq=seq, final_norm=False)
        wuk_hdc = w_uk[l].transpose(0, 2, 1).reshape(N_HEADS_A // 2, 2, HEAD_DIM_A, D_LATENT)
        zeros = jnp.zeros_like(wuk_hdc[:, 0])
        wuk_t = jnp.concatenate([jnp.concatenate([wuk_hdc[:, 0], zeros], axis=2),
                                 jnp.concatenate([zeros, wuk_hdc[:, 1]], axis=2)], axis=1).astype(BF16)
        (q_abs, q_idx, k_idx, ckv, ckv_t, w_t, ift, qk_m, v_m, o_pre, gate_a, gate_m) = _mixin(
            xf, mix_norm[l], sh2, sc2, _pack_w_in(w_in[l], d), kv_norm[l], wuk_t, conv_w[l], conv_b[l],
            seq=seq)
        wuv_t = w_uv[l].transpose(0, 2, 1).astype(BF16)
        y_a = _dsa(q_idx, q_abs, w_t, k_idx, ckv, ckv_t, btile, bmax, wuv_t, batch=batch, seq=seq)
        h_m = _mlstm(qk_m, v_m, o_pre, ift, mlstm_gate_bias[l], mlstm_head_norm[l], batch=batch, seq=seq)
        xf = _ffn(xf, ffn2_norm[l], sh3, sc3, g3, ffn2_w1[l], ffn2_w3[l], ffn2_w2[l], final_norm,
                  seq=seq, final_norm=(l == depth - 1),
                  merge=(y_a, h_m, gate_a, gate_m, g2, w_branch_attn[l], w_branch_mlstm[l], w_out[l]))
    return xf.reshape(batch, seq, d)
```
